```python
import math
import jax, jax.numpy as jnp
from jax import lax
import numpy as np

D_MODEL = 1024
BATCH = 8
SEQ = 8192
DEPTH = 2

N_MIXERS = 2
EPS = 1e-6
ATTN_HEADS = 16
ATTN_KV_HEADS = 4
ATTN_HEAD_DIM = D_MODEL // ATTN_HEADS
ATTN_GROUP = ATTN_HEADS // ATTN_KV_HEADS
WINDOW = 128
BLOCK = 128
ROPE_THETA = 500000.0
ROPE_DIM = ATTN_HEAD_DIM // 4
Q_W = ATTN_HEADS * ATTN_HEAD_DIM
KV_W = ATTN_KV_HEADS * ATTN_HEAD_DIM
MEM_LEN = 256
MEM_HEADS = 4
MEM_HEAD_DIM = 128
MEM_W = MEM_HEADS * MEM_HEAD_DIM
LRU_WIDTH = D_MODEL
LRU_BLOCKS = 8
LRU_BLOCK_DIM = LRU_WIDTH // LRU_BLOCKS
LRU_C = 8.0
CONV_WIDTH = 4
CONV_LEFT = (CONV_WIDTH - 1) // 2
ATTN_IN_W = Q_W + 2 * KV_W + MEM_W
LRU_IN_W = 2 * LRU_WIDTH + MEM_W
MIX_OUT_W = Q_W + MEM_W
D_FF = 4 * D_MODEL
NEG = -1e30

kernel_name = "hybrid_window_gqa_rglru_memxattn_encoder"


def rms_norm(x, g):
    xf = x.astype(jnp.float32)
    y = xf * lax.rsqrt(jnp.mean(xf * xf, axis=-1, keepdims=True) + EPS) * g.astype(jnp.float32)
    return y.astype(x.dtype)


def partial_rotary(t, positions):
    half = ROPE_DIM // 2
    inv_freq = ROPE_THETA ** (-2.0 * jnp.arange(half, dtype=jnp.float32) / ROPE_DIM)
    ang = positions.astype(jnp.float32)[..., None] * inv_freq
    cos = jnp.cos(ang)[:, :, None, :]
    sin = jnp.sin(ang)[:, :, None, :]
    tr = t[..., :ROPE_DIM].astype(jnp.float32)
    t1, t2 = tr[..., :half], tr[..., half:]
    rot = jnp.concatenate([t1 * cos - t2 * sin, t2 * cos + t1 * sin], axis=-1)
    return jnp.concatenate([rot.astype(t.dtype), t[..., ROPE_DIM:]], axis=-1)


def window_gqa(q, k, v, sinks):
    B, S = q.shape[0], q.shape[1]
    nb = S // BLOCK
    qb = q.reshape(B, nb, BLOCK, ATTN_KV_HEADS, ATTN_GROUP, ATTN_HEAD_DIM)
    pad = ((0, 0), (BLOCK, BLOCK), (0, 0), (0, 0))

    def bands(t):
        tb = jnp.pad(t, pad).reshape(B, nb + 2, BLOCK, ATTN_KV_HEADS, ATTN_HEAD_DIM)
        return jnp.concatenate([tb[:, :nb], tb[:, 1:nb + 1], tb[:, 2:]], axis=2)

    kb, vb = bands(k), bands(v)
    scores = jnp.einsum('bnqhgd,bnkhd->bnhgqk', qb, kb,
                        preferred_element_type=jnp.float32) * (ATTN_HEAD_DIM ** -0.5)
    q_idx = jnp.arange(BLOCK)
    k_idx = jnp.arange(3 * BLOCK)
    rel = k_idx[None, :] - BLOCK - q_idx[:, None]
    k_abs = jnp.arange(nb)[:, None] * BLOCK - BLOCK + k_idx[None, :]
    mask = (jnp.abs(rel) <= WINDOW)[None] & ((k_abs >= 0) & (k_abs < S))[:, None, :]
    scores = jnp.where(mask[None, :, None, None], scores, NEG)
    s = sinks.astype(jnp.float32).reshape(ATTN_KV_HEADS, ATTN_GROUP)[None, None, :, :, None, None]
    m = jnp.maximum(jnp.max(scores, axis=-1, keepdims=True), s)
    p = jnp.exp(scores - m)
    probs = p / (jnp.sum(p, axis=-1, keepdims=True) + jnp.exp(s - m))
    out = jnp.einsum('bnhgqk,bnkhd->bnqhgd', probs.astype(v.dtype), vb)
    return out.reshape(B, S, Q_W)


def memory_attention(mq, mk, mv):
    B, S = mq.shape[0], mq.shape[1]
    sc = jnp.einsum('bshd,bmhd->bhsm', mq, mk,
                    preferred_element_type=jnp.float32) * (MEM_HEAD_DIM ** -0.5)
    p = jax.nn.softmax(sc, axis=-1)
    out = jnp.einsum('bhsm,bmhd->bshd', p.astype(mv.dtype), mv)
    return out.reshape(B, S, MEM_W)


def centred_depthwise_conv(x, w, b):
    S = x.shape[1]
    xp = jnp.pad(x, ((0, 0), (CONV_LEFT, CONV_WIDTH - 1 - CONV_LEFT), (0, 0)))
    y = b
    for tap in range(CONV_WIDTH):
        y = y + xp[:, tap:tap + S] * w[tap]
    return y


def block_diag_linear(x, w, b):
    B, S = x.shape[0], x.shape[1]
    xr = x.reshape(B, S, LRU_BLOCKS, LRU_BLOCK_DIM)
    return jnp.einsum('bsnd,nde->bsne', xr, w).reshape(B, S, LRU_WIDTH) + b


def _linear_combine(c1, c2):
    a1, b1 = c1
    a2, b2 = c2
    return a1 * a2, a2 * b1 + b2


def rg_lru(x, wa, ba, wx, bx, lam, reverse):
    xf = x.astype(jnp.float32)
    r = jax.nn.sigmoid(block_diag_linear(x, wa, ba).astype(jnp.float32))
    i = jax.nn.sigmoid(block_diag_linear(x, wx, bx).astype(jnp.float32))
    log_a = -LRU_C * r * jax.nn.softplus(-lam.astype(jnp.float32))
    a = jnp.exp(log_a)
    u = jnp.sqrt(-jnp.expm1(2.0 * log_a)) * (i * xf)
    _, h = lax.associative_scan(_linear_combine, (a, u), axis=1, reverse=reverse)
    return h.astype(x.dtype)


def attn_mixer(h, positions, w_in, sinks):
    B, S = h.shape[0], h.shape[1]
    p = h @ w_in
    q, k, v, mq = jnp.split(p, [Q_W, Q_W + KV_W, Q_W + 2 * KV_W], axis=-1)
    q = partial_rotary(q.reshape(B, S, ATTN_HEADS, ATTN_HEAD_DIM), positions)
    k = partial_rotary(k.reshape(B, S, ATTN_KV_HEADS, ATTN_HEAD_DIM), positions)
    v = v.reshape(B, S, ATTN_KV_HEADS, ATTN_HEAD_DIM)
    return window_gqa(q, k, v, sinks), mq.reshape(B, S, MEM_HEADS, MEM_HEAD_DIM)


def lru_mixer(h, w_in, conv_w, conv_b, wa, ba, wx, bx, lam):
    B, S = h.shape[0], h.shape[1]
    p = h @ w_in
    xb, gate, mq = jnp.split(p, [LRU_WIDTH, 2 * LRU_WIDTH], axis=-1)
    xc = centred_depthwise_conv(xb, conv_w, conv_b)
    y = (rg_lru(xc, wa[0], ba[0], wx[0], bx[0], lam[0], False)
         + rg_lru(xc, wa[1], ba[1], wx[1], bx[1], lam[1], True))
    y = y * jax.nn.gelu(gate)
    return y, mq.reshape(B, S, MEM_HEADS, MEM_HEAD_DIM)


def squared_relu_mlp(h, w_up, w_down):
    return jnp.square(jax.nn.relu(h @ w_up)) @ w_down


def _fwd_setup_inputs(seed: int = 0) -> dict:
    key = jax.random.key(seed)
    ks = jax.random.split(key, 24)
    n_attn = (DEPTH + N_MIXERS - 1) // N_MIXERS
    n_lru = DEPTH // N_MIXERS
    f32 = jnp.float32

    def nrm(k, shape, scale):
        return jax.random.normal(k, shape, f32) * scale

    u = jax.random.uniform(ks[20], (n_lru, 2, LRU_WIDTH), f32, minval=0.9, maxval=0.999)
    s = u ** (1.0 / LRU_C)
    lam = jnp.log(s) - jnp.log1p(-s)
    positions = (jnp.arange(SEQ, dtype=jnp.int32)[None, :]
                 + jax.random.randint(ks[21], (BATCH, 1), 0, 1024, dtype=jnp.int32))
    return {
        "x": nrm(ks[0], (BATCH, SEQ, D_MODEL), 1.0),
        "mem": nrm(ks[1], (BATCH, MEM_LEN, D_MODEL), 1.0),
        "positions": positions,
        "mix_norm": 1.0 + nrm(ks[2], (DEPTH, D_MODEL), 0.05),
        "mlp_norm": 1.0 + nrm(ks[3], (DEPTH, D_MODEL), 0.05),
        "mem_norm": 1.0 + nrm(ks[4], (D_MODEL,), 0.05),
        "final_norm": 1.0 + nrm(ks[5], (D_MODEL,), 0.05),
        "w_mem_kv": nrm(ks[6], (DEPTH, D_MODEL, 2 * MEM_W), D_MODEL ** -0.5),
        "w_out": nrm(ks[7], (DEPTH, MIX_OUT_W, D_MODEL), MIX_OUT_W ** -0.5),
        "w_up": nrm(ks[8], (DEPTH, D_MODEL, D_FF), D_MODEL ** -0.5),
        "w_down": nrm(ks[9], (DEPTH, D_FF, D_MODEL), D_FF ** -0.5),
        "attn_w_in": nrm(ks[10], (n_attn, D_MODEL, ATTN_IN_W), D_MODEL ** -0.5),
        "attn_sinks": nrm(ks[11], (n_attn, ATTN_HEADS), 0.5),
        "lru_w_in": nrm(ks[12], (n_lru, D_MODEL, LRU_IN_W), D_MODEL ** -0.5),
        "lru_conv_w": nrm(ks[13], (n_lru, CONV_WIDTH, LRU_WIDTH), CONV_WIDTH ** -0.5),
        "lru_conv_b": nrm(ks[14], (n_lru, LRU_WIDTH), 0.02),
        "lru_wa": nrm(ks[15], (n_lru, 2, LRU_BLOCKS, LRU_BLOCK_DIM, LRU_BLOCK_DIM), LRU_BLOCK_DIM ** -0.5),
        "lru_ba": nrm(ks[16], (n_lru, 2, LRU_WIDTH), 0.02),
        "lru_wx": nrm(ks[17], (n_lru, 2, LRU_BLOCKS, LRU_BLOCK_DIM, LRU_BLOCK_DIM), LRU_BLOCK_DIM ** -0.5),
        "lru_bx": nrm(ks[18], (n_lru, 2, LRU_WIDTH), 0.02),
        "lru_lambda": lam,
    }


def _fwd_reference(x, mem, positions, mix_norm, mlp_norm, mem_norm, final_norm, w_mem_kv, w_out,
              w_up, w_down, attn_w_in, attn_sinks, lru_w_in, lru_conv_w, lru_conv_b,
              lru_wa, lru_ba, lru_wx, lru_bx, lru_lambda):
    B = mem.shape[0]
    mem_n = rms_norm(mem, mem_norm)
    for l in range(DEPTH):
        kind = l % N_MIXERS
        j = l // N_MIXERS
        h = rms_norm(x, mix_norm[l])
        kv = mem_n @ w_mem_kv[l]
        mk = kv[..., :MEM_W].reshape(B, MEM_LEN, MEM_HEADS, MEM_HEAD_DIM)
        mv = kv[..., MEM_W:].reshape(B, MEM_LEN, MEM_HEADS, MEM_HEAD_DIM)
        if kind == 0:
            mixed, mq = attn_mixer(h, positions, attn_w_in[j], attn_sinks[j])
        else:
            mixed, mq = lru_mixer(h, lru_w_in[j], lru_conv_w[j], lru_conv_b[j], lru_wa[j],
                                  lru_ba[j], lru_wx[j], lru_bx[j], lru_lambda[j])
        mo = memory_attention(mq, mk, mv)
        x = x + jnp.concatenate([mixed, mo], axis=-1) @ w_out[l]
        x = x + squared_relu_mlp(rms_norm(x, mlp_norm[l]), w_up[l], w_down[l])
    return rms_norm(x, final_norm)


import jax as _jax
import jax.numpy as _jnp

TWIN_FORMAT = 'train_step'
FWD_PARAMS = ['x', 'mem', 'positions', 'mix_norm', 'mlp_norm', 'mem_norm', 'final_norm', 'w_mem_kv', 'w_out', 'w_up', 'w_down', 'attn_w_in', 'attn_sinks', 'lru_w_in', 'lru_conv_w', 'lru_conv_b', 'lru_wa', 'lru_ba', 'lru_wx', 'lru_bx', 'lru_lambda']
TWIN_WEIGHTS = ['mix_norm', 'mlp_norm', 'mem_norm', 'final_norm', 'w_mem_kv', 'w_out', 'w_up', 'w_down', 'attn_w_in', 'attn_sinks', 'lru_w_in', 'lru_conv_w', 'lru_conv_b', 'lru_wa', 'lru_ba', 'lru_wx', 'lru_bx', 'lru_lambda']
TWIN_DIFF_INPUT = 'x'
TWIN_INPUTS = ['x', 'mem', 'positions', 'mix_norm', 'mlp_norm', 'mem_norm', 'final_norm', 'w_mem_kv', 'w_out', 'w_up', 'w_down', 'attn_w_in', 'attn_sinks', 'lru_w_in', 'lru_conv_w', 'lru_conv_b', 'lru_wa', 'lru_ba', 'lru_wx', 'lru_bx', 'lru_lambda', 'loss_target', 'm_mix_norm', 'm_mlp_norm', 'm_mem_norm', 'm_final_norm', 'm_w_mem_kv', 'm_w_out', 'm_w_up', 'm_w_down', 'm_attn_w_in', 'm_attn_sinks', 'm_lru_w_in', 'm_lru_conv_w', 'm_lru_conv_b', 'm_lru_wa', 'm_lru_ba', 'm_lru_wx', 'm_lru_bx', 'm_lru_lambda', 'v_mix_norm', 'v_mlp_norm', 'v_mem_norm', 'v_final_norm', 'v_w_mem_kv', 'v_w_out', 'v_w_up', 'v_w_down', 'v_attn_w_in', 'v_attn_sinks', 'v_lru_w_in', 'v_lru_conv_w', 'v_lru_conv_b', 'v_lru_wa', 'v_lru_ba', 'v_lru_wx', 'v_lru_bx', 'v_lru_lambda']
TWIN_OUTPUTS = ['loss', 'grad_x', 'grad_mix_norm', 'grad_mlp_norm', 'grad_mem_norm', 'grad_final_norm', 'grad_w_mem_kv', 'grad_w_out', 'grad_w_up', 'grad_w_down', 'grad_attn_w_in', 'grad_attn_sinks', 'grad_lru_w_in', 'grad_lru_conv_w', 'grad_lru_conv_b', 'grad_lru_wa', 'grad_lru_ba', 'grad_lru_wx', 'grad_lru_bx', 'grad_lru_lambda', 'delta_mix_norm', 'delta_mlp_norm', 'delta_mem_norm', 'delta_final_norm', 'delta_w_mem_kv', 'delta_w_out', 'delta_w_up', 'delta_w_down', 'delta_attn_w_in', 'delta_attn_sinks', 'delta_lru_w_in', 'delta_lru_conv_w', 'delta_lru_conv_b', 'delta_lru_wa', 'delta_lru_ba', 'delta_lru_wx', 'delta_lru_bx', 'delta_lru_lambda', 'new_m_mix_norm', 'new_m_mlp_norm', 'new_m_mem_norm', 'new_m_final_norm', 'new_m_w_mem_kv', 'new_m_w_out', 'new_m_w_up', 'new_m_w_down', 'new_m_attn_w_in', 'new_m_attn_sinks', 'new_m_lru_w_in', 'new_m_lru_conv_w', 'new_m_lru_conv_b', 'new_m_lru_wa', 'new_m_lru_ba', 'new_m_lru_wx', 'new_m_lru_bx', 'new_m_lru_lambda', 'new_v_mix_norm', 'new_v_mlp_norm', 'new_v_mem_norm', 'new_v_final_norm', 'new_v_w_mem_kv', 'new_v_w_out', 'new_v_w_up', 'new_v_w_down', 'new_v_attn_w_in', 'new_v_attn_sinks', 'new_v_lru_w_in', 'new_v_lru_conv_w', 'new_v_lru_conv_b', 'new_v_lru_wa', 'new_v_lru_ba', 'new_v_lru_wx', 'new_v_lru_bx', 'new_v_lru_lambda']
TWIN_LEAF_KINDS = {'loss': 'loss', 'grad_x': 'grad_x', 'grad_mix_norm': 'grad_w', 'grad_mlp_norm': 'grad_w', 'grad_mem_norm': 'grad_w', 'grad_final_norm': 'grad_w', 'grad_w_mem_kv': 'grad_w', 'grad_w_out': 'grad_w', 'grad_w_up': 'grad_w', 'grad_w_down': 'grad_w', 'grad_attn_w_in': 'grad_w', 'grad_attn_sinks': 'grad_w', 'grad_lru_w_in': 'grad_w', 'grad_lru_conv_w': 'grad_w', 'grad_lru_conv_b': 'grad_w', 'grad_lru_wa': 'grad_w', 'grad_lru_ba': 'grad_w', 'grad_lru_wx': 'grad_w', 'grad_lru_bx': 'grad_w', 'grad_lru_lambda': 'grad_w', 'delta_mix_norm': 'delta_w', 'delta_mlp_norm': 'delta_w', 'delta_mem_norm': 'delta_w', 'delta_final_norm': 'delta_w', 'delta_w_mem_kv': 'delta_w', 'delta_w_out': 'delta_w', 'delta_w_up': 'delta_w', 'delta_w_down': 'delta_w', 'delta_attn_w_in': 'delta_w', 'delta_attn_sinks': 'delta_w', 'delta_lru_w_in': 'delta_w', 'delta_lru_conv_w': 'delta_w', 'delta_lru_conv_b': 'delta_w', 'delta_lru_wa': 'delta_w', 'delta_lru_ba': 'delta_w', 'delta_lru_wx': 'delta_w', 'delta_lru_bx': 'delta_w', 'delta_lru_lambda': 'delta_w', 'new_m_mix_norm': 'new_m', 'new_m_mlp_norm': 'new_m', 'new_m_mem_norm': 'new_m', 'new_m_final_norm': 'new_m', 'new_m_w_mem_kv': 'new_m', 'new_m_w_out': 'new_m', 'new_m_w_up': 'new_m', 'new_m_w_down': 'new_m', 'new_m_attn_w_in': 'new_m', 'new_m_attn_sinks': 'new_m', 'new_m_lru_w_in': 'new_m', 'new_m_lru_conv_w': 'new_m', 'new_m_lru_conv_b': 'new_m', 'new_m_lru_wa': 'new_m', 'new_m_lru_ba': 'new_m', 'new_m_lru_wx': 'new_m', 'new_m_lru_bx': 'new_m', 'new_m_lru_lambda': 'new_m', 'new_v_mix_norm': 'new_v', 'new_v_mlp_norm': 'new_v', 'new_v_mem_norm': 'new_v', 'new_v_final_norm': 'new_v', 'new_v_w_mem_kv': 'new_v', 'new_v_w_out': 'new_v', 'new_v_w_up': 'new_v', 'new_v_w_down': 'new_v', 'new_v_attn_w_in': 'new_v', 'new_v_attn_sinks': 'new_v', 'new_v_lru_w_in': 'new_v', 'new_v_lru_conv_w': 'new_v', 'new_v_lru_conv_b': 'new_v', 'new_v_lru_wa': 'new_v', 'new_v_lru_ba': 'new_v', 'new_v_lru_wx': 'new_v', 'new_v_lru_bx': 'new_v', 'new_v_lru_lambda': 'new_v'}


def _forward(args):
    return _fwd_reference(*[args[k] for k in FWD_PARAMS])


def _output_shape():
    def fwd():
        inp = _fwd_setup_inputs(0)
        return _fwd_reference(*[inp[k] for k in FWD_PARAMS])
    out = _jax.eval_shape(fwd)
    return out.shape, out.dtype

N_MICROBATCH = 1
ADAM_LR = 0.001
ADAM_B1 = 0.9
ADAM_B2 = 0.999
ADAM_EPS = 1e-08
ADAM_WD = 0.01
ADAM_STEP = 10
PER_EXAMPLE_BATCH_AXIS = {'x': 0, 'mem': 0, 'positions': 0, 'loss_target': 0}
SHARED_INPUTS = []
_WEIGHT_DTYPES = {'mix_norm': _jnp.float32, 'mlp_norm': _jnp.float32, 'mem_norm': _jnp.float32, 'final_norm': _jnp.float32, 'w_mem_kv': _jnp.float32, 'w_out': _jnp.float32, 'w_up': _jnp.float32, 'w_down': _jnp.float32, 'attn_w_in': _jnp.float32, 'attn_sinks': _jnp.float32, 'lru_w_in': _jnp.float32, 'lru_conv_w': _jnp.float32, 'lru_conv_b': _jnp.float32, 'lru_wa': _jnp.float32, 'lru_ba': _jnp.float32, 'lru_wx': _jnp.float32, 'lru_bx': _jnp.float32, 'lru_lambda': _jnp.float32}
MOMENT_SCALE = {'mix_norm': 9.372064e-01, 'mlp_norm': 2.740512e-01, 'mem_norm': 4.199655e-02, 'final_norm': 6.589027e+01, 'w_mem_kv': 3.060104e-02, 'w_out': 7.343563e-01, 'w_up': 1.307177e-01, 'w_down': 9.565105e-01, 'attn_w_in': 4.203893e-02, 'attn_sinks': 1.691570e-03, 'lru_w_in': 8.377897e-01, 'lru_conv_w': 1.226204e+00, 'lru_conv_b': 3.699840e+00, 'lru_wa': 4.071136e-02, 'lru_ba': 8.607698e-02, 'lru_wx': 1.204654e-01, 'lru_bx': 2.537273e-01, 'lru_lambda': 2.101536e-01}


def _to_microbatches(a, axis):
    t = _jnp.moveaxis(a, axis, 0)
    t = t.reshape((N_MICROBATCH, t.shape[0] // N_MICROBATCH) + t.shape[1:])
    return _jnp.moveaxis(t, 1, axis + 1)


def setup_inputs(seed: int = 0) -> dict:
    inp = _fwd_setup_inputs(seed)
    key = _jax.random.fold_in(_jax.random.key(seed), 7919)
    shape, _ = _output_shape()
    out = dict(inp)
    out["loss_target"] = _jax.random.normal(_jax.random.fold_in(key, 0), shape, _jnp.float32)
    for i, name in enumerate(TWIN_WEIGHTS):
        w = inp[name].astype(_jnp.float32)
        if MOMENT_SCALE is None:
            s = _jnp.sqrt(_jnp.mean(_jnp.square(w)) + 1e-30)
        else:
            s = MOMENT_SCALE[name]
        km, kv = _jax.random.split(_jax.random.fold_in(key, i + 1))
        out[name] = w
        out["m_" + name] = s * _jax.random.normal(km, w.shape, _jnp.float32)
        out["v_" + name] = (s * s) * _jax.random.uniform(kv, w.shape, _jnp.float32, 0.5, 1.5)
    if N_MICROBATCH > 1:
        for name, axis in PER_EXAMPLE_BATCH_AXIS.items():
            out[name] = _to_microbatches(out[name], axis)
    return {'x': out['x'], 'mem': out['mem'], 'positions': out['positions'], 'mix_norm': out['mix_norm'], 'mlp_norm': out['mlp_norm'], 'mem_norm': out['mem_norm'], 'final_norm': out['final_norm'], 'w_mem_kv': out['w_mem_kv'], 'w_out': out['w_out'], 'w_up': out['w_up'], 'w_down': out['w_down'], 'attn_w_in': out['attn_w_in'], 'attn_sinks': out['attn_sinks'], 'lru_w_in': out['lru_w_in'], 'lru_conv_w': out['lru_conv_w'], 'lru_conv_b': out['lru_conv_b'], 'lru_wa': out['lru_wa'], 'lru_ba': out['lru_ba'], 'lru_wx': out['lru_wx'], 'lru_bx': out['lru_bx'], 'lru_lambda': out['lru_lambda'], 'loss_target': out['loss_target'], 'm_mix_norm': out['m_mix_norm'], 'm_mlp_norm': out['m_mlp_norm'], 'm_mem_norm': out['m_mem_norm'], 'm_final_norm': out['m_final_norm'], 'm_w_mem_kv': out['m_w_mem_kv'], 'm_w_out': out['m_w_out'], 'm_w_up': out['m_w_up'], 'm_w_down': out['m_w_down'], 'm_attn_w_in': out['m_attn_w_in'], 'm_attn_sinks': out['m_attn_sinks'], 'm_lru_w_in': out['m_lru_w_in'], 'm_lru_conv_w': out['m_lru_conv_w'], 'm_lru_conv_b': out['m_lru_conv_b'], 'm_lru_wa': out['m_lru_wa'], 'm_lru_ba': out['m_lru_ba'], 'm_lru_wx': out['m_lru_wx'], 'm_lru_bx': out['m_lru_bx'], 'm_lru_lambda': out['m_lru_lambda'], 'v_mix_norm': out['v_mix_norm'], 'v_mlp_norm': out['v_mlp_norm'], 'v_mem_norm': out['v_mem_norm'], 'v_final_norm': out['v_final_norm'], 'v_w_mem_kv': out['v_w_mem_kv'], 'v_w_out': out['v_w_out'], 'v_w_up': out['v_w_up'], 'v_w_down': out['v_w_down'], 'v_attn_w_in': out['v_attn_w_in'], 'v_attn_sinks': out['v_attn_sinks'], 'v_lru_w_in': out['v_lru_w_in'], 'v_lru_conv_w': out['v_lru_conv_w'], 'v_lru_conv_b': out['v_lru_conv_b'], 'v_lru_wa': out['v_lru_wa'], 'v_lru_ba': out['v_lru_ba'], 'v_lru_wx': out['v_lru_wx'], 'v_lru_bx': out['v_lru_bx'], 'v_lru_lambda': out['v_lru_lambda']}


def _loss(weights, diff, rest, loss_target):
    with _jax.named_scope("forward"):
        args = {**rest, TWIN_DIFF_INPUT: diff, **{k: w.astype(_WEIGHT_DTYPES[k]) for k, w in weights.items()}}
        y = _forward(args)
    with _jax.named_scope("loss_head"):
        err = _jnp.square(y.astype(_jnp.float32) - loss_target)
        return 0.5 * _jnp.sum(_jnp.mean(err, axis=-1)) if err.ndim else 0.5 * err


def _adamw(w, g, m, v):
    m = ADAM_B1 * m + (1.0 - ADAM_B1) * g
    v = ADAM_B2 * v + (1.0 - ADAM_B2) * _jnp.square(g)
    m_hat = m / (1.0 - ADAM_B1 ** ADAM_STEP)
    v_hat = v / (1.0 - ADAM_B2 ** ADAM_STEP)
    delta = -ADAM_LR * (m_hat / (_jnp.sqrt(v_hat) + ADAM_EPS) + ADAM_WD * w)
    return delta, m, v


def reference(x, mem, positions, mix_norm, mlp_norm, mem_norm, final_norm, w_mem_kv, w_out, w_up, w_down, attn_w_in, attn_sinks, lru_w_in, lru_conv_w, lru_conv_b, lru_wa, lru_ba, lru_wx, lru_bx, lru_lambda, loss_target, m_mix_norm, m_mlp_norm, m_mem_norm, m_final_norm, m_w_mem_kv, m_w_out, m_w_up, m_w_down, m_attn_w_in, m_attn_sinks, m_lru_w_in, m_lru_conv_w, m_lru_conv_b, m_lru_wa, m_lru_ba, m_lru_wx, m_lru_bx, m_lru_lambda, v_mix_norm, v_mlp_norm, v_mem_norm, v_final_norm, v_w_mem_kv, v_w_out, v_w_up, v_w_down, v_attn_w_in, v_attn_sinks, v_lru_w_in, v_lru_conv_w, v_lru_conv_b, v_lru_wa, v_lru_ba, v_lru_wx, v_lru_bx, v_lru_lambda):
    given = dict(x=x, mem=mem, positions=positions, mix_norm=mix_norm, mlp_norm=mlp_norm, mem_norm=mem_norm, final_norm=final_norm, w_mem_kv=w_mem_kv, w_out=w_out, w_up=w_up, w_down=w_down, attn_w_in=attn_w_in, attn_sinks=attn_sinks, lru_w_in=lru_w_in, lru_conv_w=lru_conv_w, lru_conv_b=lru_conv_b, lru_wa=lru_wa, lru_ba=lru_ba, lru_wx=lru_wx, lru_bx=lru_bx, lru_lambda=lru_lambda, loss_target=loss_target, m_mix_norm=m_mix_norm, m_mlp_norm=m_mlp_norm, m_mem_norm=m_mem_norm, m_final_norm=m_final_norm, m_w_mem_kv=m_w_mem_kv, m_w_out=m_w_out, m_w_up=m_w_up, m_w_down=m_w_down, m_attn_w_in=m_attn_w_in, m_attn_sinks=m_attn_sinks, m_lru_w_in=m_lru_w_in, m_lru_conv_w=m_lru_conv_w, m_lru_conv_b=m_lru_conv_b, m_lru_wa=m_lru_wa, m_lru_ba=m_lru_ba, m_lru_wx=m_lru_wx, m_lru_bx=m_lru_bx, m_lru_lambda=m_lru_lambda, v_mix_norm=v_mix_norm, v_mlp_norm=v_mlp_norm, v_mem_norm=v_mem_norm, v_final_norm=v_final_norm, v_w_mem_kv=v_w_mem_kv, v_w_out=v_w_out, v_w_up=v_w_up, v_w_down=v_w_down, v_attn_w_in=v_attn_w_in, v_attn_sinks=v_attn_sinks, v_lru_w_in=v_lru_w_in, v_lru_conv_w=v_lru_conv_w, v_lru_conv_b=v_lru_conv_b, v_lru_wa=v_lru_wa, v_lru_ba=v_lru_ba, v_lru_wx=v_lru_wx, v_lru_bx=v_lru_bx, v_lru_lambda=v_lru_lambda)
    weights = {n: given[n] for n in TWIN_WEIGHTS}
    shared = {n: given[n] for n in SHARED_INPUTS}
    per_example = {n: given[n] for n in ['x', 'mem', 'positions']}
    grad_fn = _jax.value_and_grad(_loss, argnums=(0, 1))

    def one_microbatch(ex, loss_target):
        ex = dict(ex)
        diff = ex.pop(TWIN_DIFF_INPUT)
        return grad_fn(weights, diff, {**shared, **ex}, loss_target)

    if N_MICROBATCH == 1:
        loss, (grad_w, grad_x) = one_microbatch(per_example, given["loss_target"])
    else:
        def body(carry, xs):
            loss_sum, grad_sum = carry
            l_k, (gw_k, gx_k) = one_microbatch(xs[0], xs[1])
            with _jax.named_scope("update"):
                return (loss_sum + l_k, _jax.tree.map(_jnp.add, grad_sum, gw_k)), gx_k

        init = (_jnp.zeros((), _jnp.float32), _jax.tree.map(_jnp.zeros_like, weights))
        (loss, grad_w), grad_x = _jax.lax.scan(body, init, (per_example, given["loss_target"]))
    with _jax.named_scope("update"):
        delta_w, new_m, new_v = {}, {}, {}
        for n in TWIN_WEIGHTS:
            delta_w[n], new_m[n], new_v[n] = _adamw(weights[n], grad_w[n], given["m_" + n], given["v_" + n])
    return (loss, grad_x, *[grad_w[n] for n in TWIN_WEIGHTS], *[delta_w[n] for n in TWIN_WEIGHTS],
            *[new_m[n] for n in TWIN_WEIGHTS], *[new_v[n] for n in TWIN_WEIGHTS])
```

```python
import functools
import math

import jax
import jax.numpy as jnp
from jax import lax
from jax.experimental import pallas as pl
from jax.experimental.pallas import tpu as pltpu

F32 = jnp.float32
BF16 = jnp.bfloat16
MESH = pl.DeviceIdType.MESH

D_MODEL = 1024
DEPTH = 2
EPS = 1e-6
ATTN_HEADS = 16
ATTN_KV_HEADS = 4
HEAD_DIM = 64
WINDOW = 128
BLOCK = 128
ROPE_THETA = 500000.0
ROPE_DIM = 16
Q_W = 1024
KV_W = 256
MEM_LEN = 256
MEM_HEADS = 4
MEM_HEAD_DIM = 128
MEM_W = 512
LRU_BLOCKS = 8
LRU_C = 8.0
ATTN_IN_W = 2048
LRU_IN_W = 2560
MIX_OUT_W = 1536
D_FF = 4096
NEG = -1e30
N_CHIPS = 4

ADAM_LR = 0.001
ADAM_B1 = 0.9
ADAM_B2 = 0.999
ADAM_EPS = 1e-08
ADAM_WD = 0.01
ADAM_STEP = 10

LANES = 128
SCAN_ROWS = 128
VMEM_LIMIT = 56 * 1024 * 1024

NT = (((1,), (1,)), ((), ()))
TN = (((0,), (0,)), ((), ()))


def _cp(sem=None):
    return pltpu.CompilerParams(dimension_semantics=sem, vmem_limit_bytes=VMEM_LIMIT)


def _mm_nn(a, w3, *, name, out_dtype=F32, norm_g=None, resid=None, relu2=False, tm=512):
    M, K = a.shape
    ns, _, n = w3.shape
    N = ns * n
    tm = min(tm, M)
    has_norm = norm_g is not None
    has_res = resid is not None

    def body(*refs):
        refs = list(refs)
        a_ref, w_ref = refs[0], refs[1]
        pos = 2
        if has_norm:
            g_ref = refs[pos]
            pos += 1
        if has_res:
            r_ref = refs[pos]
            pos += 1
        outs = refs[pos:]
        if has_norm:
            xv = a_ref[...]
            rs = lax.rsqrt(jnp.mean(xv * xv, axis=-1, keepdims=True) + EPS)
            ab = (xv * rs * g_ref[...]).astype(BF16)
            outs[-1][...] = ab
        else:
            ab = a_ref[...]
        for s in range(ns):
            acc = jnp.dot(ab, w_ref[s], preferred_element_type=F32)
            sl = slice(s * n, (s + 1) * n)
            if relu2:
                outs[0][:, sl] = acc.astype(BF16)
                rl = jnp.maximum(acc, 0.0)
                outs[1][:, sl] = (rl * rl).astype(BF16)
            elif has_res:
                outs[0][:, sl] = r_ref[:, sl] + acc
            else:
                outs[0][:, sl] = acc.astype(out_dtype)

    row = lambda w: pl.BlockSpec((tm, w), lambda i: (i, 0))
    in_specs = [row(K), pl.BlockSpec((ns, K, n), lambda i: (0, 0, 0))]
    args = [a, w3]
    if has_norm:
        in_specs.append(pl.BlockSpec((1, K), lambda i: (0, 0)))
        args.append(norm_g.reshape(1, K))
    if has_res:
        in_specs.append(row(N))
        args.append(resid)
    if relu2:
        out_shape = [jax.ShapeDtypeStruct((M, N), BF16), jax.ShapeDtypeStruct((M, N), BF16)]
        out_specs = [row(N), row(N)]
    else:
        out_shape = [jax.ShapeDtypeStruct((M, N), F32 if has_res else out_dtype)]
        out_specs = [row(N)]
    if has_norm:
        out_shape.append(jax.ShapeDtypeStruct((M, K), BF16))
        out_specs.append(row(K))
    res = pl.pallas_call(body, grid=(M // tm,), in_specs=in_specs, out_specs=out_specs, out_shape=out_shape,
                         name=name, compiler_params=_cp(("parallel",)))(*args)
    return res if len(res) > 1 else res[0]


def _mm_nt(g, w3, *, name, out_dtype=BF16, up=None, norm_x=None, norm_g=None, dres=None, tm=512):
    M = g.shape[0]
    ns, K, n = w3.shape
    tm = min(tm, M)
    has_up = up is not None
    has_norm = norm_x is not None
    has_res = dres is not None

    def body(*refs):
        refs = list(refs)
        g_ref, w_ref = refs[0], refs[1]
        pos = 2
        if has_up:
            up_ref = refs[pos]
            pos += 1
        if has_norm:
            x_ref, gn_ref = refs[pos], refs[pos + 1]
            pos += 2
        if has_res:
            r_ref = refs[pos]
            pos += 1
        outs = refs[pos:]
        acc = None
        for s in range(ns):
            part = lax.dot_general(g_ref[:, s * n:(s + 1) * n], w_ref[s], NT, preferred_element_type=F32)
            acc = part if acc is None else acc + part
        if has_up:
            outs[0][...] = (acc * (2.0 * jnp.maximum(up_ref[...].astype(F32), 0.0))).astype(BF16)
        elif has_norm:
            xv = x_ref[...]
            rs = lax.rsqrt(jnp.mean(xv * xv, axis=-1, keepdims=True) + EPS)
            xn = xv * rs
            dxn = acc * gn_ref[...]
            dx = rs * (dxn - xn * jnp.mean(dxn * xn, axis=-1, keepdims=True))
            if has_res:
                dx = dx + r_ref[...]
            outs[0][...] = dx
            outs[1][...] = dx.astype(BF16)

            @pl.when(pl.program_id(0) == 0)
            def _():
                outs[2][...] = jnp.zeros_like(outs[2])

            outs[2][...] += jnp.sum(acc * xn, axis=0, keepdims=True)
        else:
            outs[0][...] = acc.astype(out_dtype)

    row = lambda w: pl.BlockSpec((tm, w), lambda i: (i, 0))
    in_specs = [row(ns * n), pl.BlockSpec((ns, K, n), lambda i: (0, 0, 0))]
    args = [g, w3]
    if has_up:
        in_specs.append(row(K))
        args.append(up)
    if has_norm:
        in_specs += [row(K), pl.BlockSpec((1, K), lambda i: (0, 0))]
        args += [norm_x, norm_g.reshape(1, K)]
    if has_res:
        in_specs.append(row(K))
        args.append(dres)
    if has_norm:
        out_shape = [jax.ShapeDtypeStruct((M, K), F32), jax.ShapeDtypeStruct((M, K), BF16),
                     jax.ShapeDtypeStruct((1, K), F32)]
        out_specs = [row(K), row(K), pl.BlockSpec((1, K), lambda i: (0, 0))]
        sem = ("arbitrary",)
    else:
        out_shape = [jax.ShapeDtypeStruct((M, K), BF16 if has_up else out_dtype)]
        out_specs = [row(K)]
        sem = ("parallel",)
    res = pl.pallas_call(body, grid=(M // tm,), in_specs=in_specs, out_specs=out_specs, out_shape=out_shape,
                         name=name, compiler_params=_cp(sem))(*args)
    return res if len(res) > 1 else res[0]


def _mm_tn(a, g, ns, *, name, tk=512, tm=1024):
    M, K = a.shape
    n = g.shape[1] // ns
    tm = min(tm, M)
    tk = min(tk, K)

    def body(a_ref, g_ref, o_ref):
        @pl.when(pl.program_id(2) == 0)
        def _():
            o_ref[...] = jnp.zeros_like(o_ref)

        o_ref[0] += lax.dot_general(a_ref[...], g_ref[...], TN, preferred_element_type=F32)

    return pl.pallas_call(
        body, grid=(ns, K // tk, M // tm),
        in_specs=[pl.BlockSpec((tm, tk), lambda s, k, m: (m, k)), pl.BlockSpec((tm, n), lambda s, k, m: (m, s))],
        out_specs=pl.BlockSpec((1, tk, n), lambda s, k, m: (s, k, 0)),
        out_shape=jax.ShapeDtypeStruct((ns, K, n), F32), name=name,
        compiler_params=_cp(("parallel", "parallel", "arbitrary")))(a, g)


def _final(x, gain, target, *, name="final_loss", tr=256):
    S, Dm = x.shape
    tr = min(tr, S)

    def body(x_ref, g_ref, t_ref, loss_ref, dx_ref, dxb_ref, dg_ref):
        @pl.when(pl.program_id(0) == 0)
        def _():
            loss_ref[...] = jnp.zeros_like(loss_ref)
            dg_ref[...] = jnp.zeros_like(dg_ref)

        xv = x_ref[...]
        gv = g_ref[...]
        rs = lax.rsqrt(jnp.mean(xv * xv, axis=-1, keepdims=True) + EPS)
        xn = xv * rs
        err = xn * gv - t_ref[...]
        loss_ref[...] += 0.5 * jnp.sum(jnp.mean(err * err, axis=-1, keepdims=True), axis=0, keepdims=True)
        dout = err * (1.0 / Dm)
        dg_ref[...] += jnp.sum(dout * xn, axis=0, keepdims=True)
        dxn = dout * gv
        dx = rs * (dxn - xn * jnp.mean(dxn * xn, axis=-1, keepdims=True))
        dx_ref[...] = dx
        dxb_ref[...] = dx.astype(BF16)

    row = pl.BlockSpec((tr, Dm), lambda i: (i, 0))
    return pl.pallas_call(
        body, grid=(S // tr,),
        in_specs=[row, pl.BlockSpec((1, Dm), lambda i: (0, 0)), row],
        out_specs=[pl.BlockSpec((1, 1), lambda i: (0, 0)), row, row, pl.BlockSpec((1, Dm), lambda i: (0, 0))],
        out_shape=[jax.ShapeDtypeStruct((1, 1), F32), jax.ShapeDtypeStruct((S, Dm), F32),
                   jax.ShapeDtypeStruct((S, Dm), BF16), jax.ShapeDtypeStruct((1, Dm), F32)],
        name=name, compiler_params=_cp(("arbitrary",)))(x, gain.reshape(1, Dm), target)


def _rope_tables(positions):
    half = ROPE_DIM // 2
    inv_freq = ROPE_THETA ** (-2.0 * jnp.arange(half, dtype=F32) / ROPE_DIM)
    ang = positions.astype(F32)[:, None] * inv_freq
    cos, sin = jnp.cos(ang), jnp.sin(ang)
    S = positions.shape[0]
    ones = jnp.ones((S, HEAD_DIM - ROPE_DIM), F32)
    cos64 = jnp.concatenate([cos, cos, ones], axis=1)
    sin64 = jnp.concatenate([-sin, sin, 0.0 * ones], axis=1)
    return jnp.tile(cos64, (1, 2)), jnp.tile(sin64, (1, 2))


def _rope_partner(t):
    lane = lax.broadcasted_iota(jnp.int32, t.shape, 1)
    low = (lane & (HEAD_DIM - 1)) < (ROPE_DIM // 2)
    return jnp.where(low, pltpu.roll(t, LANES - ROPE_DIM // 2, 1), pltpu.roll(t, ROPE_DIM // 2, 1))


def _qk_prep(p, cos_t, sin_t, *, name="qk_prep", tr=256):
    S = p.shape[0]
    tr = min(tr, S)
    scale = HEAD_DIM ** -0.5

    def body(p_ref, c_ref, s_ref, q_ref, k_ref, v_ref):
        cs, sn = c_ref[...], s_ref[...]
        lane = lax.broadcasted_iota(jnp.int32, (tr, LANES), 1)
        lo = lane < HEAD_DIM
        for c in range(Q_W // LANES):
            t = p_ref[:, c * LANES:(c + 1) * LANES]
            q_ref[:, c * LANES:(c + 1) * LANES] = ((t * cs + _rope_partner(t) * sn) * scale).astype(BF16)
        for c in range(KV_W // LANES):
            t = p_ref[:, Q_W + c * LANES:Q_W + (c + 1) * LANES]
            kc = t * cs + _rope_partner(t) * sn
            vc = p_ref[:, Q_W + KV_W + c * LANES:Q_W + KV_W + (c + 1) * LANES]
            for arr, ref in ((kc, k_ref), (vc, v_ref)):
                sw = pltpu.roll(arr, HEAD_DIM, 1)
                ref[:, (2 * c) * LANES:(2 * c + 1) * LANES] = jnp.where(lo, arr, sw).astype(BF16)
                ref[:, (2 * c + 1) * LANES:(2 * c + 2) * LANES] = jnp.where(lo, sw, arr).astype(BF16)

    row = lambda w: pl.BlockSpec((tr, w), lambda i: (i, 0))
    return pl.pallas_call(
        body, grid=(S // tr,), in_specs=[row(ATTN_IN_W), row(LANES), row(LANES)],
        out_specs=[row(Q_W), row(2 * KV_W), row(2 * KV_W)],
        out_shape=[jax.ShapeDtypeStruct((S, Q_W), BF16), jax.ShapeDtypeStruct((S, 2 * KV_W), BF16),
                   jax.ShapeDtypeStruct((S, 2 * KV_W), BF16)],
        name=name, compiler_params=_cp(("parallel",)))(p, cos_t, sin_t)


def _qk_prep_bwd(dq, dk, dv, dmq, cos_t, sin_t, *, name="qk_prep_bwd", tr=256):
    S = dq.shape[0]
    tr = min(tr, S)

    def body(dq_ref, dk_ref, dv_ref, dmq_ref, c_ref, s_ref, o_ref):
        cs, sn = c_ref[...], s_ref[...]
        for c in range(Q_W // LANES):
            t = dq_ref[:, c * LANES:(c + 1) * LANES]
            o_ref[:, c * LANES:(c + 1) * LANES] = (t * cs - _rope_partner(t) * sn).astype(BF16)
        for c in range(KV_W // LANES):
            t = dk_ref[:, c * LANES:(c + 1) * LANES]
            o_ref[:, Q_W + c * LANES:Q_W + (c + 1) * LANES] = (t * cs - _rope_partner(t) * sn).astype(BF16)
        o_ref[:, Q_W + KV_W:Q_W + 2 * KV_W] = dv_ref[...].astype(BF16)
        o_ref[:, Q_W + 2 * KV_W:] = dmq_ref[...]

    row = lambda w: pl.BlockSpec((tr, w), lambda i: (i, 0))
    return pl.pallas_call(
        body, grid=(S // tr,), in_specs=[row(Q_W), row(KV_W), row(KV_W), row(MEM_W), row(LANES), row(LANES)],
        out_specs=row(ATTN_IN_W), out_shape=jax.ShapeDtypeStruct((S, ATTN_IN_W), BF16),
        name=name, compiler_params=_cp(("parallel",)))(dq, dk, dv, dmq, cos_t, sin_t)


def _band(n, S):
    start = pl.multiple_of(jnp.clip((n - 1) * BLOCK, 0, S - 3 * BLOCK), BLOCK)
    qi = (lax.broadcasted_iota(jnp.int32, (4 * BLOCK, 3 * BLOCK), 0) & (BLOCK - 1)) + n * BLOCK
    ki = lax.broadcasted_iota(jnp.int32, (4 * BLOCK, 3 * BLOCK), 1) + start
    return start, jnp.abs(ki - qi) <= WINDOW


def _stack_heads(ref, g, lo):
    parts = []
    for j in range(4):
        c = 2 * g + j // 2
        t = ref[:, c * LANES:(c + 1) * LANES].astype(F32)
        parts.append(jnp.where(lo if j % 2 == 0 else jnp.logical_not(lo), t, 0.0).astype(BF16))
    return jnp.concatenate(parts, axis=0)


def _sink_col(sink_ref, g):
    return jnp.concatenate([jnp.full((BLOCK, 1), sink_ref[4 * g + j], F32) for j in range(4)], axis=0)


def _attn_fwd(q, kd, vd, sinks, *, name="attn_fwd"):
    S = q.shape[0]

    def body(sink_ref, q_ref, k_ref, v_ref, o_ref, lse_ref):
        n = pl.program_id(0)
        start, mask = _band(n, S)
        lane = lax.broadcasted_iota(jnp.int32, (BLOCK, LANES), 1)
        lo = lane < HEAD_DIM
        lse_blk = jnp.zeros((BLOCK, LANES), F32)
        for g in range(ATTN_KV_HEADS):
            kg = k_ref[pl.ds(start, 3 * BLOCK), g * LANES:(g + 1) * LANES]
            vg = v_ref[pl.ds(start, 3 * BLOCK), g * LANES:(g + 1) * LANES]
            qst = _stack_heads(q_ref, g, lo)
            s = lax.dot_general(qst, kg, NT, preferred_element_type=F32)
            s = jnp.where(mask, s, NEG)
            sk = _sink_col(sink_ref, g)
            m = jnp.maximum(jnp.max(s, axis=-1, keepdims=True), sk)
            pe = jnp.exp(s - m)
            l = jnp.sum(pe, axis=-1, keepdims=True) + jnp.exp(sk - m)
            pv = jnp.dot((pe / l).astype(BF16), vg, preferred_element_type=F32)
            for jj in range(2):
                oc = jnp.where(lo, pv[(2 * jj) * BLOCK:(2 * jj + 1) * BLOCK], pv[(2 * jj + 1) * BLOCK:(2 * jj + 2) * BLOCK])
                o_ref[:, (2 * g + jj) * LANES:(2 * g + jj + 1) * LANES] = oc.astype(BF16)
            lse = m + jnp.log(l)
            for j in range(4):
                lse_blk = jnp.where(lane == 4 * g + j, lse[j * BLOCK:(j + 1) * BLOCK], lse_blk)
        lse_ref[...] = lse_blk

    full = lambda w: pl.BlockSpec((S, w), lambda i: (0, 0))
    return pl.pallas_call(
        body, grid=(S // BLOCK,),
        in_specs=[pl.BlockSpec(memory_space=pltpu.SMEM), pl.BlockSpec((BLOCK, Q_W), lambda i: (i, 0)),
                  full(2 * KV_W), full(2 * KV_W)],
        out_specs=[pl.BlockSpec((BLOCK, Q_W), lambda i: (i, 0)), pl.BlockSpec((BLOCK, LANES), lambda i: (i, 0))],
        out_shape=[jax.ShapeDtypeStruct((S, Q_W), BF16), jax.ShapeDtypeStruct((S, LANES), F32)],
        name=name, compiler_params=_cp(("parallel",)))(sinks, q, kd, vd)


def _attn_bwd(q, kd, vd, lse, sinks, dcat, *, name="attn_bwd"):
    S = q.shape[0]
    scale = HEAD_DIM ** -0.5

    def body(sink_ref, q_ref, k_ref, v_ref, lse_ref, do_ref, dq_ref, dk_ref, dv_ref, ds_ref):
        n = pl.program_id(0)

        @pl.when(n == 0)
        def _():
            dk_ref[...] = jnp.zeros_like(dk_ref)
            dv_ref[...] = jnp.zeros_like(dv_ref)
            ds_ref[...] = jnp.zeros_like(ds_ref)

        start, mask = _band(n, S)
        lane = lax.broadcasted_iota(jnp.int32, (BLOCK, LANES), 1)
        lo = lane < HEAD_DIM
        lane3 = lax.broadcasted_iota(jnp.int32, (3 * BLOCK, LANES), 1)
        row8 = lax.broadcasted_iota(jnp.int32, (8, LANES), 0)
        lane8 = lax.broadcasted_iota(jnp.int32, (8, LANES), 1)
        dsink = jnp.zeros((8, LANES), F32)
        lse_blk = lse_ref[...]
        for g in range(ATTN_KV_HEADS):
            rows = pl.ds(start, 3 * BLOCK)
            cols = slice((g // 2) * LANES, (g // 2 + 1) * LANES)
            kg = k_ref[rows, g * LANES:(g + 1) * LANES]
            vg = v_ref[rows, g * LANES:(g + 1) * LANES]
            qst = _stack_heads(q_ref, g, lo)
            dost = _stack_heads(do_ref, g, lo)
            s = lax.dot_general(qst, kg, NT, preferred_element_type=F32)
            s = jnp.where(mask, s, NEG)
            lse_col = jnp.concatenate(
                [jnp.sum(jnp.where(lane == 4 * g + j, lse_blk, 0.0), axis=1, keepdims=True) for j in range(4)], axis=0)
            p = jnp.exp(s - lse_col)
            dp = lax.dot_general(dost, vg, NT, preferred_element_type=F32)
            delta = jnp.sum(p * dp, axis=-1, keepdims=True)
            dsb = (p * (dp - delta)).astype(BF16)
            dqs = jnp.dot(dsb, kg, preferred_element_type=F32) * scale
            for jj in range(2):
                dq_ref[:, (2 * g + jj) * LANES:(2 * g + jj + 1) * LANES] = jnp.where(
                    lo, dqs[(2 * jj) * BLOCK:(2 * jj + 1) * BLOCK], dqs[(2 * jj + 1) * BLOCK:(2 * jj + 2) * BLOCK])
            half = (lane3 < HEAD_DIM) if g % 2 == 0 else (lane3 >= HEAD_DIM)
            dkr = lax.dot_general(dsb, qst, TN, preferred_element_type=F32)
            dk_ref[rows, cols] += jnp.where(half, dkr + pltpu.roll(dkr, HEAD_DIM, 1), 0.0)
            dvr = lax.dot_general(p.astype(BF16), dost, TN, preferred_element_type=F32)
            dv_ref[rows, cols] += jnp.where(half, dvr + pltpu.roll(dvr, HEAD_DIM, 1), 0.0)
            contrib = jnp.exp(_sink_col(sink_ref, g) - lse_col) * delta
            for j in range(4):
                val = -jnp.sum(contrib[j * BLOCK:(j + 1) * BLOCK], axis=0, keepdims=True)
                dsink = dsink + jnp.where((row8 == 0) & (lane8 == 4 * g + j), val, 0.0)
        ds_ref[...] += dsink

    full = lambda w: pl.BlockSpec((S, w), lambda i: (0, 0))
    blk = lambda w: pl.BlockSpec((BLOCK, w), lambda i: (i, 0))
    return pl.pallas_call(
        body, grid=(S // BLOCK,),
        in_specs=[pl.BlockSpec(memory_space=pltpu.SMEM), blk(Q_W), full(2 * KV_W), full(2 * KV_W), blk(LANES), blk(Q_W)],
        out_specs=[blk(Q_W), full(KV_W), full(KV_W), pl.BlockSpec((8, LANES), lambda i: (0, 0))],
        out_shape=[jax.ShapeDtypeStruct((S, Q_W), F32), jax.ShapeDtypeStruct((S, KV_W), F32),
                   jax.ShapeDtypeStruct((S, KV_W), F32), jax.ShapeDtypeStruct((8, LANES), F32)],
        name=name, compiler_params=_cp(("arbitrary",)))(sinks, q, kd, vd, lse, dcat)


def _mem_probs(q_ref, kv_ref, h):
    scale = MEM_HEAD_DIM ** -0.5
    qh = q_ref[:, h * LANES:(h + 1) * LANES].astype(BF16)
    s = lax.dot_general(qh, kv_ref[:, h * LANES:(h + 1) * LANES], NT, preferred_element_type=F32) * scale
    m = jnp.max(s, axis=-1, keepdims=True)
    pe = jnp.exp(s - m)
    return qh, pe / jnp.sum(pe, axis=-1, keepdims=True)


def _memattn_fwd(p, qblk, kv, *, name="memattn_fwd", tr=512):
    S = p.shape[0]
    tr = min(tr, S)

    def body(q_ref, kv_ref, o_ref):
        for h in range(MEM_HEADS):
            _, pr = _mem_probs(q_ref, kv_ref, h)
            o = jnp.dot(pr.astype(BF16), kv_ref[:, MEM_W + h * LANES:MEM_W + (h + 1) * LANES], preferred_element_type=F32)
            o_ref[:, h * LANES:(h + 1) * LANES] = o.astype(BF16)

    return pl.pallas_call(
        body, grid=(S // tr,),
        in_specs=[pl.BlockSpec((tr, MEM_W), lambda i: (i, qblk)), pl.BlockSpec((MEM_LEN, 2 * MEM_W), lambda i: (0, 0))],
        out_specs=pl.BlockSpec((tr, MEM_W), lambda i: (i, 0)),
        out_shape=jax.ShapeDtypeStruct((S, MEM_W), BF16), name=name, compiler_params=_cp(("parallel",)))(p, kv)


def _memattn_bwd(p, qblk, kv, dcat, *, name="memattn_bwd", tr=512):
    S = p.shape[0]
    tr = min(tr, S)
    scale = MEM_HEAD_DIM ** -0.5

    def body(q_ref, kv_ref, do_ref, dq_ref, dkv_ref):
        @pl.when(pl.program_id(0) == 0)
        def _():
            dkv_ref[...] = jnp.zeros_like(dkv_ref)

        for h in range(MEM_HEADS):
            qh, pr = _mem_probs(q_ref, kv_ref, h)
            doh = do_ref[:, h * LANES:(h + 1) * LANES]
            dp = lax.dot_general(doh, kv_ref[:, MEM_W + h * LANES:MEM_W + (h + 1) * LANES], NT, preferred_element_type=F32)
            delta = jnp.sum(pr * dp, axis=-1, keepdims=True)
            dsb = (pr * (dp - delta) * scale).astype(BF16)
            dq = jnp.dot(dsb, kv_ref[:, h * LANES:(h + 1) * LANES], preferred_element_type=F32)
            dq_ref[:, h * LANES:(h + 1) * LANES] = dq.astype(BF16)
            dkv_ref[:, h * LANES:(h + 1) * LANES] += lax.dot_general(dsb, qh, TN, preferred_element_type=F32)
            dkv_ref[:, MEM_W + h * LANES:MEM_W + (h + 1) * LANES] += lax.dot_general(
                pr.astype(BF16), doh, TN, preferred_element_type=F32)

    return pl.pallas_call(
        body, grid=(S // tr,),
        in_specs=[pl.BlockSpec((tr, MEM_W), lambda i: (i, qblk)), pl.BlockSpec((MEM_LEN, 2 * MEM_W), lambda i: (0, 0)),
                  pl.BlockSpec((tr, MEM_W), lambda i: (i, Q_W // MEM_W))],
        out_specs=[pl.BlockSpec((tr, MEM_W), lambda i: (i, 0)), pl.BlockSpec((MEM_LEN, 2 * MEM_W), lambda i: (0, 0))],
        out_shape=[jax.ShapeDtypeStruct((S, MEM_W), BF16), jax.ShapeDtypeStruct((MEM_LEN, 2 * MEM_W), F32)],
        name=name, compiler_params=_cp(("arbitrary",)))(p, kv, dcat)


def _sigmoid(z):
    return 1.0 / (1.0 + jnp.exp(-z))


def _expm1(z):
    poly = z * (1.0 + z * (0.5 + z * (1.0 / 6.0 + z * (1.0 / 24.0 + z * (1.0 / 120.0)))))
    return jnp.where(jnp.abs(z) < 0.1, poly, jnp.exp(z) - 1.0)


def _softplus_neg(lam):
    z = -lam
    return jnp.maximum(z, 0.0) + jnp.log(1.0 + jnp.exp(-jnp.abs(z)))


_GELU_C = math.sqrt(2.0 / math.pi)


def _gelu(z):
    return 0.5 * z * (1.0 + jnp.tanh(_GELU_C * (z + 0.044715 * z * z * z)))


def _gelu_grad(z):
    t = jnp.tanh(_GELU_C * (z + 0.044715 * z * z * z))
    return 0.5 * (1.0 + t) + 0.5 * z * (1.0 - t * t) * _GELU_C * (1.0 + 3.0 * 0.044715 * z * z)


def _row_or_zero(ref, t, S):
    ok = jnp.logical_and(t >= 0, t < S)
    return jnp.where(ok, ref[pl.ds(jnp.clip(t, 0, S - 1), 1), :], 0.0)


def _shift_down(v, first):
    ri = lax.broadcasted_iota(jnp.int32, v.shape, 0)
    return jnp.where(ri == 0, first, pltpu.roll(v, 1, 0))


def _shift_up(v, last):
    T = v.shape[0]
    ri = lax.broadcasted_iota(jnp.int32, v.shape, 0)
    return jnp.where(ri == T - 1, last, pltpu.roll(v, T - 1, 0))


def _scan_chunk(a, u, reverse):
    T = a.shape[0]
    ri = lax.broadcasted_iota(jnp.int32, a.shape, 0)
    d = 1
    while d < T:
        if reverse:
            a_s, u_s, ok = pltpu.roll(a, T - d, 0), pltpu.roll(u, T - d, 0), ri < T - d
        else:
            a_s, u_s, ok = pltpu.roll(a, d, 0), pltpu.roll(u, d, 0), ri >= d
        u = jnp.where(ok, a * u_s + u, u)
        a = jnp.where(ok, a * a_s, a)
        d *= 2
    return a, u


def _conv_taps(xb_ref, t0, S):
    T = SCAN_ROWS
    x0 = xb_ref[pl.ds(t0, T), :]
    xm1 = _shift_down(x0, _row_or_zero(xb_ref, t0 - 1, S))
    nxt0 = _row_or_zero(xb_ref, t0 + T, S)
    xp1 = _shift_up(x0, nxt0)
    xp2 = _shift_up(xp1, _row_or_zero(xb_ref, t0 + T + 1, S))
    return xm1, x0, xp1, xp2


def _lru_gates(xc, w_a, b_a, w_x, b_x, sp):
    xcb = xc.astype(BF16)
    r = _sigmoid(jnp.dot(xcb, w_a, preferred_element_type=F32) + b_a)
    i = _sigmoid(jnp.dot(xcb, w_x, preferred_element_type=F32) + b_x)
    la = -LRU_C * r * sp
    a = jnp.exp(la)
    beta = jnp.sqrt(-_expm1(2.0 * la))
    return r, i, a, beta


def _lru_specs(S):
    col = lambda off: pl.BlockSpec((S, LANES), lambda n: (0, n + off), pipeline_mode=pl.Buffered(1))
    small = lambda r: pl.BlockSpec((r, LANES), lambda n: (0, n))
    wblk = pl.BlockSpec((2, 1, LANES, LANES), lambda n: (0, n, 0, 0))
    return col, small, wblk


def _lru_fwd(p, conv_w, conv_b, wa, ba, wx, bx, lam, *, name="lru_fwd"):
    S = p.shape[0]
    T = SCAN_ROWS
    nc = S // T

    def body(xb_ref, gate_ref, cw_ref, cb_ref, wa_ref, ba_ref, wx_ref, bx_ref, lam_ref, y_ref, hf_ref, hr_ref, xc_v):
        sp = _softplus_neg(lam_ref[...])
        cw = cw_ref[...]

        def fwd_step(c, h_in):
            t0 = pl.multiple_of(c * T, T)
            xm1, x0, xp1, xp2 = _conv_taps(xb_ref, t0, S)
            xc = cb_ref[...] + xm1 * cw[0:1] + x0 * cw[1:2] + xp1 * cw[2:3] + xp2 * cw[3:4]
            xc_v[pl.ds(t0, T), :] = xc
            _, i, a, beta = _lru_gates(xc, wa_ref[0, 0], ba_ref[0:1], wx_ref[0, 0], bx_ref[0:1], sp[0:1])
            A, U = _scan_chunk(a, beta * (i * xc), False)
            hf_ref[pl.ds(t0, T), :] = A * h_in + U
            return hf_ref[pl.ds(t0 + T - 1, 1), :]

        lax.fori_loop(0, nc, fwd_step, jnp.zeros((1, LANES), F32))

        def rev_step(k, h_in):
            t0 = pl.multiple_of((nc - 1 - k) * T, T)
            xc = xc_v[pl.ds(t0, T), :]
            _, i, a, beta = _lru_gates(xc, wa_ref[1, 0], ba_ref[1:2], wx_ref[1, 0], bx_ref[1:2], sp[1:2])
            A, U = _scan_chunk(a, beta * (i * xc), True)
            h = A * h_in + U
            hr_ref[pl.ds(t0, T), :] = h
            y_ref[pl.ds(t0, T), :] = ((hf_ref[pl.ds(t0, T), :] + h) * _gelu(gate_ref[pl.ds(t0, T), :])).astype(BF16)
            return hr_ref[pl.ds(t0, 1), :]

        lax.fori_loop(0, nc, rev_step, jnp.zeros((1, LANES), F32))

    col, small, wblk = _lru_specs(S)
    colo = lambda: pl.BlockSpec((S, LANES), lambda n: (0, n))
    return pl.pallas_call(
        body, grid=(LRU_BLOCKS,),
        in_specs=[col(0), col(LRU_BLOCKS), small(4), small(1), wblk, small(2), wblk, small(2), small(2)],
        out_specs=[colo(), colo(), colo()],
        out_shape=[jax.ShapeDtypeStruct((S, D_MODEL), BF16), jax.ShapeDtypeStruct((S, D_MODEL), F32),
                   jax.ShapeDtypeStruct((S, D_MODEL), F32)],
        scratch_shapes=[pltpu.VMEM((S, LANES), F32)],
        name=name, compiler_params=_cp(("parallel",)))(p, p, conv_w, conv_b, wa, ba, wx, bx, lam)


def _lru_bwd(p, hf, hr, dcat, conv_w, conv_b, wa, ba, wx, bx, lam, *, name="lru_bwd"):
    S = p.shape[0]
    T = SCAN_ROWS
    nc = S // T

    def body(xb_ref, gate_ref, hf_ref, hr_ref, dy_ref, cw_ref, cb_ref, wa_ref, ba_ref, wx_ref, bx_ref, lam_ref,
             dxb_ref, dgate_ref, dcw_ref, dcb_ref, dwa_ref, dba_ref, dwx_ref, dbx_ref, dlam_ref, xc_v, dxc_v):
        lam_v = lam_ref[...]
        sp = _softplus_neg(lam_v)
        cw = cw_ref[...]
        for ref in (dcw_ref, dcb_ref, dwa_ref, dba_ref, dwx_ref, dbx_ref, dlam_ref):
            ref[...] = jnp.zeros_like(ref)

        def prep_step(c, carry):
            t0 = pl.multiple_of(c * T, T)
            rows = pl.ds(t0, T)
            xm1, x0, xp1, xp2 = _conv_taps(xb_ref, t0, S)
            xc_v[rows, :] = cb_ref[...] + xm1 * cw[0:1] + x0 * cw[1:2] + xp1 * cw[2:3] + xp2 * cw[3:4]
            dgate_ref[rows, :] = (dy_ref[rows, :].astype(F32) * (hf_ref[rows, :] + hr_ref[rows, :])
                                  * _gelu_grad(gate_ref[rows, :])).astype(BF16)
            return carry

        lax.fori_loop(0, nc, prep_step, 0)

        def direction(d):
            h_ref = hf_ref if d == 0 else hr_ref
            w_a, w_x = wa_ref[d, 0], wx_ref[d, 0]
            b_a, b_x, sp_d = ba_ref[d:d + 1], bx_ref[d:d + 1], sp[d:d + 1]

            def step(k, carry):
                g_in, a_in = carry
                c = (nc - 1 - k) if d == 0 else k
                t0 = pl.multiple_of(c * T, T)
                rows = pl.ds(t0, T)
                xc = xc_v[rows, :]
                r, i, a, beta = _lru_gates(xc, w_a, b_a, w_x, b_x, sp_d)
                dh = dy_ref[rows, :].astype(F32) * _gelu(gate_ref[rows, :])
                hc = h_ref[rows, :]
                if d == 0:
                    A, U = _scan_chunk(_shift_up(a, a_in), dh, True)
                    g = A * g_in + U
                    h_nb = _shift_down(hc, _row_or_zero(h_ref, t0 - 1, S))
                    nxt = (g[0:1], a[0:1])
                else:
                    A, U = _scan_chunk(_shift_down(a, a_in), dh, False)
                    g = A * g_in + U
                    h_nb = _shift_up(hc, _row_or_zero(h_ref, t0 + T, S))
                    nxt = (g[T - 1:T], a[T - 1:T])
                da = g * h_nb
                dbeta = g * (i * xc)
                tb = g * beta
                dla = da * a - dbeta * (a * a / beta)
                dzr = (dla * (-LRU_C * sp_d)) * (r * (1.0 - r))
                dzi = (tb * xc) * (i * (1.0 - i))
                dzrb, dzib, xcb = dzr.astype(BF16), dzi.astype(BF16), xc.astype(BF16)
                dwa_ref[d, 0] += lax.dot_general(xcb, dzrb, TN, preferred_element_type=F32)
                dwx_ref[d, 0] += lax.dot_general(xcb, dzib, TN, preferred_element_type=F32)
                dba_ref[d:d + 1] += jnp.sum(dzr, axis=0, keepdims=True)
                dbx_ref[d:d + 1] += jnp.sum(dzi, axis=0, keepdims=True)
                dlam_ref[d:d + 1] += jnp.sum(dla * (-LRU_C * r), axis=0, keepdims=True)
                dxc = (tb * i + lax.dot_general(dzrb, w_a, NT, preferred_element_type=F32)
                       + lax.dot_general(dzib, w_x, NT, preferred_element_type=F32))
                if d == 0:
                    dxc_v[rows, :] = dxc
                else:
                    dxc_v[rows, :] += dxc
                return nxt

            lax.fori_loop(0, nc, step, (jnp.zeros((1, LANES), F32), jnp.zeros((1, LANES), F32)))

        direction(0)
        direction(1)
        dlam_ref[...] = dlam_ref[...] * (-1.0 / (1.0 + jnp.exp(lam_v)))

        def conv_step(c, carry):
            t0 = pl.multiple_of(c * T, T)
            rows = pl.ds(t0, T)
            g0 = dxc_v[rows, :]
            gm1 = _shift_down(g0, _row_or_zero(dxc_v, t0 - 1, S))
            gm2 = _shift_down(gm1, _row_or_zero(dxc_v, t0 - 2, S))
            gp1 = _shift_up(g0, _row_or_zero(dxc_v, t0 + T, S))
            dxb_ref[rows, :] = (cw[0:1] * gp1 + cw[1:2] * g0 + cw[2:3] * gm1 + cw[3:4] * gm2).astype(BF16)
            xm1, x0, xp1, xp2 = _conv_taps(xb_ref, t0, S)
            for tap, xs in enumerate((xm1, x0, xp1, xp2)):
                dcw_ref[tap:tap + 1] += jnp.sum(g0 * xs, axis=0, keepdims=True)
            dcb_ref[...] += jnp.sum(g0, axis=0, keepdims=True)
            return carry

        lax.fori_loop(0, nc, conv_step, 0)

    col, small, wblk = _lru_specs(S)
    colo = lambda: pl.BlockSpec((S, LANES), lambda n: (0, n), pipeline_mode=pl.Buffered(1))
    return pl.pallas_call(
        body, grid=(LRU_BLOCKS,),
        in_specs=[col(0), col(LRU_BLOCKS), col(0), col(0), col(0), small(4), small(1), wblk, small(2), wblk, small(2), small(2)],
        out_specs=[colo(), colo(), small(4), small(1), wblk, small(2), wblk, small(2), small(2)],
        out_shape=[jax.ShapeDtypeStruct((S, D_MODEL), BF16), jax.ShapeDtypeStruct((S, D_MODEL), BF16),
                   jax.ShapeDtypeStruct((4, D_MODEL), F32), jax.ShapeDtypeStruct((1, D_MODEL), F32),
                   jax.ShapeDtypeStruct((2, LRU_BLOCKS, LANES, LANES), F32), jax.ShapeDtypeStruct((2, D_MODEL), F32),
                   jax.ShapeDtypeStruct((2, LRU_BLOCKS, LANES, LANES), F32), jax.ShapeDtypeStruct((2, D_MODEL), F32),
                   jax.ShapeDtypeStruct((2, D_MODEL), F32)],
        scratch_shapes=[pltpu.VMEM((S, LANES), F32), pltpu.VMEM((S, LANES), F32)],
        name=name, compiler_params=_cp(("parallel",)))(p, p, hf, hr, dcat, conv_w, conv_b, wa, ba, wx, bx, lam)


def _mlp_fwd(x, w_up, w_down, gain, l):
    up, act, h = _mm_nn(x, w_up, norm_g=gain, relu2=True, name=f"mlp_up{l}")
    return _mm_nn(act, w_down, resid=x, name=f"mlp_down{l}"), (up, act, h)


def _mlp_bwd(x, dx, dxb, saved, w_up, w_down, gain, l):
    up, act, h = saved
    g_down = _mm_tn(act, dxb, 1, name=f"dw_down{l}")
    dup = _mm_nt(dxb, w_down, up=up, name=f"d_up{l}")
    g_up = _mm_tn(h, dup, N_CHIPS, name=f"dw_up{l}")
    dx, dxb, g_gain = _mm_nt(dup, w_up, norm_x=x, norm_g=gain, dres=dx, name=f"d_mlp_in{l}")
    return dx, dxb, g_down, g_up, g_gain


def _local_step(x, mem, positions, target, W):
    S = x.shape[0]
    cos_t, sin_t = _rope_tables(positions)
    sinks = W["attn_sinks"].reshape(ATTN_HEADS)
    G = {}

    kv0, memn = _mm_nn(mem, W["w_mem_kv"][0], norm_g=W["mem_norm"], out_dtype=BF16, name="mem_kv0", tm=256)
    kv1 = _mm_nn(memn, W["w_mem_kv"][1], out_dtype=BF16, name="mem_kv1", tm=256)
    p0, h0 = _mm_nn(x, W["attn_w_in"], norm_g=W["mix_norm"][0], name="attn_in")
    q, kd, vd = _qk_prep(p0, cos_t, sin_t)
    ao, lse = _attn_fwd(q, kd, vd, sinks)
    mo0 = _memattn_fwd(p0, Q_W // MEM_W + 1, kv0, name="memattn_fwd0")
    cat0 = jnp.concatenate([ao, mo0], axis=1)
    x1 = _mm_nn(cat0, W["w_out"][0], resid=x, name="mix_out0")
    x2, mlp0 = _mlp_fwd(x1, W["w_up"][0], W["w_down"][0], W["mlp_norm"][0], 0)
    p1, h2 = _mm_nn(x2, W["lru_w_in"], norm_g=W["mix_norm"][1], name="lru_in")
    lru_w = (W["lru_conv_w"], W["lru_conv_b"], W["lru_wa"], W["lru_ba"], W["lru_wx"], W["lru_bx"], W["lru_lambda"])
    y, hf, hr = _lru_fwd(p1, *lru_w)
    mo1 = _memattn_fwd(p1, 2 * D_MODEL // MEM_W, kv1, name="memattn_fwd1")
    cat1 = jnp.concatenate([y, mo1], axis=1)
    x3 = _mm_nn(cat1, W["w_out"][1], resid=x2, name="mix_out1")
    x4, mlp1 = _mlp_fwd(x3, W["w_up"][1], W["w_down"][1], W["mlp_norm"][1], 1)
    loss, dx, dxb, G["final_norm"] = _final(x4, W["final_norm"], target)

    dx, dxb, gd1, gu1, gm1 = _mlp_bwd(x3, dx, dxb, mlp1, W["w_up"][1], W["w_down"][1], W["mlp_norm"][1], 1)
    go1 = _mm_tn(cat1, dxb, 1, name="dw_out1")
    dcat1 = _mm_nt(dxb, W["w_out"][1], name="d_mix1")
    dmq1, dkv1 = _memattn_bwd(p1, 2 * D_MODEL // MEM_W, kv1, dcat1, name="memattn_bwd1")
    (dxb1, dgate, G["lru_conv_w"], G["lru_conv_b"], G["lru_wa"], G["lru_ba"], G["lru_wx"], G["lru_bx"],
     G["lru_lambda"]) = _lru_bwd(p1, hf, hr, dcat1, *lru_w)
    dp1 = jnp.concatenate([dxb1, dgate, dmq1], axis=1)
    G["lru_w_in"] = _mm_tn(h2, dp1, N_CHIPS, name="dw_lru_in")
    dx, dxb, gx1 = _mm_nt(dp1, W["lru_w_in"], norm_x=x2, norm_g=W["mix_norm"][1], dres=dx, name="d_lru_in")

    dx, dxb, gd0, gu0, gm0 = _mlp_bwd(x1, dx, dxb, mlp0, W["w_up"][0], W["w_down"][0], W["mlp_norm"][0], 0)
    go0 = _mm_tn(cat0, dxb, 1, name="dw_out0")
    dcat0 = _mm_nt(dxb, W["w_out"][0], name="d_mix0")
    dmq0, dkv0 = _memattn_bwd(p0, Q_W // MEM_W + 1, kv0, dcat0, name="memattn_bwd0")
    dq, dk, dv, dsink = _attn_bwd(q, kd, vd, lse, sinks, dcat0)
    dp0 = _qk_prep_bwd(dq, dk, dv, dmq0, cos_t, sin_t)
    G["attn_w_in"] = _mm_tn(h0, dp0, N_CHIPS, name="dw_attn_in")
    dx, _, gx0 = _mm_nt(dp0, W["attn_w_in"], norm_x=x, norm_g=W["mix_norm"][0], dres=dx, name="d_attn_in")

    dkv0b, dkv1b = dkv0.astype(BF16), dkv1.astype(BF16)
    gk0 = _mm_tn(memn, dkv0b, 1, name="dw_kv0", tm=256)
    gk1 = _mm_tn(memn, dkv1b, 1, name="dw_kv1", tm=256)
    w_kv_both = jnp.concatenate([W["w_mem_kv"][0], W["w_mem_kv"][1]], axis=0)
    _, _, G["mem_norm"] = _mm_nt(jnp.concatenate([dkv0b, dkv1b], axis=1), w_kv_both, norm_x=mem, norm_g=W["mem_norm"],
                                 name="d_mem", tm=256)

    G["w_mem_kv"] = (gk0, gk1)
    G["w_out"] = (go0, go1)
    G["w_up"] = (gu0, gu1)
    G["w_down"] = (gd0, gd1)
    G["mix_norm"] = jnp.concatenate([gx0, gx1], axis=0)
    G["mlp_norm"] = jnp.concatenate([gm0, gm1], axis=0)
    G["attn_sinks"] = dsink[0:1, 0:ATTN_HEADS]
    return loss[0, 0], dx, G


HBM = pl.BlockSpec(memory_space=pl.ANY)


def _place():
    x, y, c = lax.axis_index("x"), lax.axis_index("y"), lax.axis_index("c")
    chips = [(1 - x, y), (x, 1 - y), (1 - x, 1 - y)]
    return x, y, c, chips


def _remote(src, dst, send_sems, recv_sems, k, to):
    return pltpu.make_async_remote_copy(src_ref=src, dst_ref=dst, send_sem=send_sems.at[k], recv_sem=recv_sems.at[k],
                                        device_id=to, device_id_type=MESH)


def _comm_call(body, out_shape, n_sems, name, *args):
    return pl.pallas_call(
        body, out_shape=out_shape, in_specs=[HBM] * len(args), out_specs=HBM,
        scratch_shapes=[pltpu.SemaphoreType.DMA((n_sems,)), pltpu.SemaphoreType.DMA((n_sems,)), pltpu.SemaphoreType.DMA],
        name=name)(*args)


def _allgather_chips(shard, *, name, forward_to_sibling):
    R, C = shard.shape
    half = R // 2

    def body(w_ref, o_ref, send_sems, recv_sems, local_sem):
        x, y, c, chips = _place()
        me = 2 * x + y
        mine = pltpu.make_async_copy(w_ref, o_ref.at[me], local_sem)
        mine.start()
        if forward_to_sibling:
            my_rows = pl.ds(pl.multiple_of(c * half, 16), half)
            sib_rows = pl.ds(pl.multiple_of((1 - c) * half, 16), half)
        else:
            my_rows = sib_rows = pl.ds(0, R)
        sends = [_remote(w_ref.at[my_rows], o_ref.at[me, my_rows], send_sems, recv_sems, j, (cx, cy, c))
                 for j, (cx, cy) in enumerate(chips)]
        for cp in sends:
            cp.start()
        passed = []
        for j, (cx, cy) in enumerate(chips):
            idx = 2 * cx + cy
            landed = o_ref.at[idx, my_rows]
            _remote(landed, landed, send_sems, recv_sems, j, (cx, cy, c)).wait_recv()
            if forward_to_sibling:
                fw = _remote(landed, landed, send_sems, recv_sems, 3 + j, (x, y, 1 - c))
                fw.start()
                passed.append(fw)
        if forward_to_sibling:
            for j, (cx, cy) in enumerate(chips):
                got = o_ref.at[2 * cx + cy, sib_rows]
                _remote(got, got, send_sems, recv_sems, 3 + j, (x, y, 1 - c)).wait_recv()
        for cp in sends + passed:
            cp.wait_send()
        mine.wait()

    return _comm_call(body, jax.ShapeDtypeStruct((N_CHIPS, R, C), shard.dtype), 6, name, shard)


def _sibling_exchange(g, *, name):
    _, R, C = g.shape
    half = R // 2

    def body(g_ref, o_ref, send_sems, recv_sems, local_sem):
        x, y, c, _ = _place()
        other = pl.ds(pl.multiple_of((1 - c) * half, 8), half)
        cps = [_remote(g_ref.at[s, other], o_ref.at[s], send_sems, recv_sems, s, (x, y, 1 - c)) for s in range(N_CHIPS)]
        for cp in cps:
            cp.start()
        for cp in cps:
            cp.wait()

    return _comm_call(body, jax.ShapeDtypeStruct((N_CHIPS, half, C), g.dtype), N_CHIPS, name, g)


def _chip_exchange(h, *, name):
    def body(h_ref, o_ref, send_sems, recv_sems, local_sem):
        x, y, c, chips = _place()
        me = 2 * x + y
        mine = pltpu.make_async_copy(h_ref.at[me], o_ref.at[me], local_sem)
        mine.start()
        cps = [_remote(h_ref.at[2 * cx + cy], o_ref.at[me], send_sems, recv_sems, j, (cx, cy, c))
               for j, (cx, cy) in enumerate(chips)]
        for cp in cps:
            cp.start()
        for j, (cx, cy) in enumerate(chips):
            got = o_ref.at[2 * cx + cy]
            _remote(got, got, send_sems, recv_sems, j, (cx, cy, c)).wait_recv()
        for cp in cps:
            cp.wait_send()
        mine.wait()

    return _comm_call(body, jax.ShapeDtypeStruct(h.shape, h.dtype), 3, name, h)


def _sibling_allgather(red, *, name):
    half, C = red.shape

    def body(r_ref, o_ref, send_sems, recv_sems, local_sem):
        x, y, c, _ = _place()
        my_rows = pl.ds(pl.multiple_of(c * half, 8), half)
        sib_rows = pl.ds(pl.multiple_of((1 - c) * half, 8), half)
        mine = pltpu.make_async_copy(r_ref, o_ref.at[my_rows], local_sem)
        mine.start()
        cp = _remote(r_ref, o_ref.at[my_rows], send_sems, recv_sems, 0, (x, y, 1 - c))
        cp.start()
        got = o_ref.at[sib_rows]
        _remote(got, got, send_sems, recv_sems, 0, (x, y, 1 - c)).wait_recv()
        cp.wait_send()
        mine.wait()

    return _comm_call(body, jax.ShapeDtypeStruct((2 * half, C), red.dtype), 1, name, red)


def _sum_halves(g, recv, c_idx, *, name="sum_halves", tr=480):
    _, R, C = g.shape
    half = R // 2
    nblk = half // tr

    def body(c_ref, g_ref, r_ref, o_ref):
        o_ref[...] = (g_ref[...] + r_ref[...]).astype(BF16)

    blk = pl.BlockSpec((1, tr, C), lambda s, i, c_ref: (s, i, 0))
    return pl.pallas_call(
        body,
        grid_spec=pltpu.PrefetchScalarGridSpec(
            num_scalar_prefetch=1, grid=(N_CHIPS, nblk),
            in_specs=[pl.BlockSpec((1, tr, C), lambda s, i, c_ref: (s, c_ref[0] * nblk + i, 0)), blk], out_specs=blk),
        out_shape=jax.ShapeDtypeStruct((N_CHIPS, half, C), BF16), name=name,
        compiler_params=_cp(("parallel", "parallel")))(c_idx, g, recv)


def _sum_chips(parts, *, name="sum_chips", tr=480):
    _, R, C = parts.shape

    def body(p_ref, o_ref):
        acc = p_ref[0].astype(F32) + p_ref[1].astype(F32)
        o_ref[...] = (acc + p_ref[2].astype(F32)) + p_ref[3].astype(F32)

    return pl.pallas_call(
        body, grid=(R // tr,), in_specs=[pl.BlockSpec((N_CHIPS, tr, C), lambda i: (0, i, 0))],
        out_specs=pl.BlockSpec((tr, C), lambda i: (i, 0)), out_shape=jax.ShapeDtypeStruct((R, C), F32),
        name=name, compiler_params=_cp(("parallel",)))(parts)


def _adamw(w, g, m, v, *, name, tr=128):
    R, C = w.shape
    bc1 = 1.0 - ADAM_B1 ** ADAM_STEP
    bc2 = 1.0 - ADAM_B2 ** ADAM_STEP

    def body(w_ref, g_ref, m_ref, v_ref, d_ref, nm_ref, nv_ref):
        gv = g_ref[...]
        nm = ADAM_B1 * m_ref[...] + (1.0 - ADAM_B1) * gv
        nv = ADAM_B2 * v_ref[...] + (1.0 - ADAM_B2) * (gv * gv)
        d_ref[...] = -ADAM_LR * ((nm / bc1) / (jnp.sqrt(nv / bc2) + ADAM_EPS) + ADAM_WD * w_ref[...])
        nm_ref[...] = nm
        nv_ref[...] = nv

    blk = pl.BlockSpec((tr, C), lambda i: (i, 0))
    return pl.pallas_call(
        body, grid=(R // tr,), in_specs=[blk] * 4, out_specs=[blk] * 3,
        out_shape=[jax.ShapeDtypeStruct((R, C), F32)] * 3, name=name, compiler_params=_cp(("parallel",)))(w, g, m, v)


ROW = 1024
BIG = ("w_mem_kv", "w_out", "w_up", "w_down", "attn_w_in", "lru_w_in")
SMALL_SHARDED = ("lru_conv_w", "lru_conv_b", "lru_ba", "lru_bx", "lru_lambda")
REPLICATED = ("mix_norm", "mlp_norm", "mem_norm", "final_norm", "attn_sinks", "lru_wa", "lru_wx")
SMALL = REPLICATED + SMALL_SHARDED
WEIGHTS = ("mix_norm", "mlp_norm", "mem_norm", "final_norm", "w_mem_kv", "w_out", "w_up", "w_down", "attn_w_in",
           "attn_sinks", "lru_w_in", "lru_conv_w", "lru_conv_b", "lru_wa", "lru_ba", "lru_wx", "lru_bx", "lru_lambda")
SMALL_W_ROWS = 32
SMALL_G_ROWS = 192
ADAM_SMALL_ROWS = 640


def _rows(a):
    return a.reshape(-1, ROW)


def _flat_pad(parts, total):
    flat = jnp.concatenate([p.reshape(-1) for p in parts])
    return jnp.pad(flat, (0, total - flat.shape[0]))


def _pad_rows(a):
    flat = a.reshape(-1)
    n = -(-flat.shape[0] // ROW) * ROW
    return jnp.pad(flat, (0, n - flat.shape[0])).reshape(-1, ROW)


def _gather_weights(P):
    big = [_rows(P[n].astype(BF16)) for n in BIG]
    small = _flat_pad([P[n] for n in SMALL_SHARDED], SMALL_W_ROWS * ROW // 2)
    small_bits = lax.bitcast_convert_type(small, BF16).reshape(SMALL_W_ROWS, ROW)
    packed = jnp.concatenate(big + [small_bits], axis=0)
    full = _allgather_chips(packed, name="allgather_weights", forward_to_sibling=True)
    W = {n: P[n] for n in REPLICATED}
    off = 0
    per_layer = {}
    for n in BIG:
        r = big[BIG.index(n)].shape[0]
        per_layer[n] = full[:, off:off + r]
        off += r
    kv = per_layer["w_mem_kv"].reshape(N_CHIPS, DEPTH, -1, D_MODEL)
    W["w_mem_kv"] = [kv[:, l].reshape(1, D_MODEL, D_MODEL) for l in range(DEPTH)]
    wo = per_layer["w_out"].reshape(N_CHIPS, DEPTH, -1, D_MODEL)
    W["w_out"] = [wo[:, l].reshape(1, MIX_OUT_W, D_MODEL) for l in range(DEPTH)]
    wu = per_layer["w_up"].reshape(N_CHIPS, DEPTH, D_MODEL, D_FF // N_CHIPS)
    W["w_up"] = [wu[:, l] for l in range(DEPTH)]
    wd = per_layer["w_down"].reshape(N_CHIPS, DEPTH, D_FF // N_CHIPS, D_MODEL)
    W["w_down"] = [wd[:, l].reshape(1, D_FF, D_MODEL) for l in range(DEPTH)]
    W["attn_w_in"] = per_layer["attn_w_in"].reshape(N_CHIPS, D_MODEL, ATTN_IN_W // N_CHIPS)
    W["lru_w_in"] = per_layer["lru_w_in"].reshape(N_CHIPS, D_MODEL, LRU_IN_W // N_CHIPS)
    sm = lax.bitcast_convert_type(full[:, off:off + SMALL_W_ROWS].reshape(N_CHIPS, -1, 2), F32)
    o = 0
    for n in SMALL_SHARDED:
        shp = P[n].shape[1:]
        cnt = math.prod(shp)
        piece = sm[:, o:o + cnt].reshape((N_CHIPS,) + shp)
        piece = jnp.moveaxis(piece, 0, -2)
        W[n] = piece.reshape(shp[:-1] + (N_CHIPS * shp[-1],)).reshape(-1, D_MODEL)
        o += cnt
    W["lru_wa"] = P["lru_wa"][0].astype(BF16)
    W["lru_wx"] = P["lru_wx"][0].astype(BF16)
    return W


def _small_grad_list(G):
    return [G["mix_norm"], G["mlp_norm"], G["mem_norm"], G["final_norm"], jnp.pad(G["attn_sinks"].reshape(-1), (0, ROW - ATTN_HEADS)),
            G["lru_wa"], G["lru_wx"], G["lru_conv_w"], G["lru_conv_b"], G["lru_ba"], G["lru_bx"], G["lru_lambda"]]


SMALL_G_SIZES = (2 * D_MODEL, 2 * D_MODEL, D_MODEL, D_MODEL, ROW, 2 * 8 * 128 * 128, 2 * 8 * 128 * 128,
                 4 * D_MODEL, D_MODEL, 2 * D_MODEL, 2 * D_MODEL, 2 * D_MODEL)


def _reduce_grads(G, c_idx, chip):
    c4 = lambda a, r: a.reshape(N_CHIPS, r, ROW)
    pieces = [
        c4(G["w_mem_kv"][0], 256), c4(G["w_mem_kv"][1], 256), c4(G["w_out"][0], 384), c4(G["w_out"][1], 384),
        G["w_up"][0], G["w_up"][1], c4(G["w_down"][0], 1024), c4(G["w_down"][1], 1024),
        c4(G["attn_w_in"], 512), c4(G["lru_w_in"], 640),
        _flat_pad(_small_grad_list(G), N_CHIPS * SMALL_G_ROWS * ROW).reshape(N_CHIPS, SMALL_G_ROWS, ROW),
    ]
    packed = jnp.concatenate(pieces, axis=1)
    recv = _sibling_exchange(packed, name="grad_sibling_exchange")
    halves = _sum_halves(packed, recv, c_idx)
    parts = _chip_exchange(halves, name="grad_chip_exchange")
    red = _sum_chips(parts)
    full = _sibling_allgather(red, name="grad_sibling_allgather")
    n_big = full.shape[0] - SMALL_G_ROWS
    small_all = _allgather_chips(full[n_big:], name="allgather_small_grads", forward_to_sibling=False)
    flat = small_all.reshape(-1)
    small = {}
    o = 0
    names = ("mix_norm", "mlp_norm", "mem_norm", "final_norm", "attn_sinks", "lru_wa", "lru_wx",
             "lru_conv_w", "lru_conv_b", "lru_ba", "lru_bx", "lru_lambda")
    for n, cnt in zip(names, SMALL_G_SIZES):
        small[n] = flat[o:o + cnt]
        o += cnt
    return full[:n_big], small


def kernel(x, mem, positions, mix_norm, mlp_norm, mem_norm, final_norm, w_mem_kv, w_out, w_up, w_down, attn_w_in, attn_sinks, lru_w_in, lru_conv_w, lru_conv_b, lru_wa, lru_ba, lru_wx, lru_bx, lru_lambda, loss_target, m_mix_norm, m_mlp_norm, m_mem_norm, m_final_norm, m_w_mem_kv, m_w_out, m_w_up, m_w_down, m_attn_w_in, m_attn_sinks, m_lru_w_in, m_lru_conv_w, m_lru_conv_b, m_lru_wa, m_lru_ba, m_lru_wx, m_lru_bx, m_lru_lambda, v_mix_norm, v_mlp_norm, v_mem_norm, v_final_norm, v_w_mem_kv, v_w_out, v_w_up, v_w_down, v_attn_w_in, v_attn_sinks, v_lru_w_in, v_lru_conv_w, v_lru_conv_b, v_lru_wa, v_lru_ba, v_lru_wx, v_lru_bx, v_lru_lambda):
    P = dict(mix_norm=mix_norm, mlp_norm=mlp_norm, mem_norm=mem_norm, final_norm=final_norm, w_mem_kv=w_mem_kv, w_out=w_out,
             w_up=w_up, w_down=w_down, attn_w_in=attn_w_in, attn_sinks=attn_sinks, lru_w_in=lru_w_in, lru_conv_w=lru_conv_w,
             lru_conv_b=lru_conv_b, lru_wa=lru_wa, lru_ba=lru_ba, lru_wx=lru_wx, lru_bx=lru_bx, lru_lambda=lru_lambda)
    M1 = dict(mix_norm=m_mix_norm, mlp_norm=m_mlp_norm, mem_norm=m_mem_norm, final_norm=m_final_norm, w_mem_kv=m_w_mem_kv,
              w_out=m_w_out, w_up=m_w_up, w_down=m_w_down, attn_w_in=m_attn_w_in, attn_sinks=m_attn_sinks, lru_w_in=m_lru_w_in,
              lru_conv_w=m_lru_conv_w, lru_conv_b=m_lru_conv_b, lru_wa=m_lru_wa, lru_ba=m_lru_ba, lru_wx=m_lru_wx,
              lru_bx=m_lru_bx, lru_lambda=m_lru_lambda)
    V2 = dict(mix_norm=v_mix_norm, mlp_norm=v_mlp_norm, mem_norm=v_mem_norm, final_norm=v_final_norm, w_mem_kv=v_w_mem_kv,
              w_out=v_w_out, w_up=v_w_up, w_down=v_w_down, attn_w_in=v_attn_w_in, attn_sinks=v_attn_sinks, lru_w_in=v_lru_w_in,
              lru_conv_w=v_lru_conv_w, lru_conv_b=v_lru_conv_b, lru_wa=v_lru_wa, lru_ba=v_lru_ba, lru_wx=v_lru_wx,
              lru_bx=v_lru_bx, lru_lambda=v_lru_lambda)
    chip = 2 * lax.axis_index("x") + lax.axis_index("y")
    c_idx = lax.axis_index("c").astype(jnp.int32).reshape(1)

    W = _gather_weights(P)
    loss, dx, G = _local_step(x[0], mem[0], positions[0], loss_target[0], W)
    loss = lax.psum(loss, ("x", "y", "c"))
    big_rows, small = _reduce_grads(G, c_idx, chip)

    grads, deltas, new_m, new_v = {}, {}, {}, {}
    off = 0
    for n in BIG:
        r = math.prod(P[n].shape) // ROW
        g = big_rows[off:off + r]
        off += r
        d, nm, nv = _adamw(_rows(P[n]), g, _rows(M1[n]), _rows(V2[n]), name=f"adamw_{n}")
        grads[n], deltas[n], new_m[n], new_v[n] = (t.reshape(P[n].shape) for t in (g, d, nm, nv))

    for n in SMALL:
        g = small[n]
        if n in SMALL_SHARDED:
            shard = P[n].shape[-1]
            g = lax.dynamic_slice_in_dim(g.reshape(-1, N_CHIPS * shard), chip * shard, shard, axis=1)
        elif n == "attn_sinks":
            g = g[:ATTN_HEADS]
        grads[n] = g.reshape(P[n].shape)
    packs = []
    for src in (P, grads, M1, V2):
        a = jnp.concatenate([_pad_rows(src[n]) for n in SMALL], axis=0)
        packs.append(jnp.pad(a, ((0, ADAM_SMALL_ROWS - a.shape[0]), (0, 0))))
    d_s, nm_s, nv_s = _adamw(*packs, name="adamw_small")
    o = 0
    for n in SMALL:
        cnt = math.prod(P[n].shape)
        r = -(-cnt // ROW)
        for dst, src in ((deltas, d_s), (new_m, nm_s), (new_v, nv_s)):
            dst[n] = src[o:o + r].reshape(-1)[:cnt].reshape(P[n].shape)
        o += r

    return (loss, dx[None], *[grads[n] for n in WEIGHTS], *[deltas[n] for n in WEIGHTS],
            *[new_m[n] for n in WEIGHTS], *[new_v[n] for n in WEIGHTS])
```

```python
import functools
import math

import jax
import jax.numpy as jnp
from jax import lax
from jax.experimental import pallas as pl
from jax.experimental.pallas import tpu as pltpu

F32 = jnp.float32
BF16 = jnp.bfloat16
MESH = pl.DeviceIdType.MESH

D_MODEL = 1024
DEPTH = 2
EPS = 1e-6
ATTN_HEADS = 16
ATTN_KV_HEADS = 4
HEAD_DIM = 64
WINDOW = 128
BLOCK = 128
ROPE_THETA = 500000.0
ROPE_DIM = 16
Q_W = 1024
KV_W = 256
MEM_LEN = 256
MEM_HEADS = 4
MEM_HEAD_DIM = 128
MEM_W = 512
LRU_BLOCKS = 8
LRU_C = 8.0
ATTN_IN_W = 2048
LRU_IN_W = 2560
MIX_OUT_W = 1536
D_FF = 4096
NEG = -1e30
N_CHIPS = 4

ADAM_LR = 0.001
ADAM_B1 = 0.9
ADAM_B2 = 0.999
ADAM_EPS = 1e-08
ADAM_WD = 0.01
ADAM_STEP = 10

LANES = 128
SCAN_ROWS = 128
VMEM_LIMIT = 56 * 1024 * 1024

NT = (((1,), (1,)), ((), ()))
TN = (((0,), (0,)), ((), ()))


def _cp(sem=None):
    return pltpu.CompilerParams(dimension_semantics=sem, vmem_limit_bytes=VMEM_LIMIT)


def _mm_nn(a, w3, *, name, out_dtype=F32, norm_g=None, resid=None, relu2=False, tm=512):
    M, K = a.shape
    ns, _, n = w3.shape
    N = ns * n
    tm = min(tm, M)
    has_norm = norm_g is not None
    has_res = resid is not None

    def body(*refs):
        refs = list(refs)
        a_ref, w_ref = refs[0], refs[1]
        pos = 2
        if has_norm:
            g_ref = refs[pos]
            pos += 1
        if has_res:
            r_ref = refs[pos]
            pos += 1
        outs = refs[pos:]
        if has_norm:
            xv = a_ref[...]
            rs = lax.rsqrt(jnp.mean(xv * xv, axis=-1, keepdims=True) + EPS)
            ab = (xv * rs * g_ref[...]).astype(BF16)
            outs[-1][...] = ab
        else:
            ab = a_ref[...]
        for s in range(ns):
            acc = jnp.dot(ab, w_ref[s], preferred_element_type=F32)
            sl = slice(s * n, (s + 1) * n)
            if relu2:
                outs[0][:, sl] = acc.astype(BF16)
                rl = jnp.maximum(acc, 0.0)
                outs[1][:, sl] = (rl * rl).astype(BF16)
            elif has_res:
                outs[0][:, sl] = r_ref[:, sl] + acc
            else:
                outs[0][:, sl] = acc.astype(out_dtype)

    row = lambda w: pl.BlockSpec((tm, w), lambda i: (i, 0))
    in_specs = [row(K), pl.BlockSpec((ns, K, n), lambda i: (0, 0, 0))]
    args = [a, w3]
    if has_norm:
        in_specs.append(pl.BlockSpec((1, K), lambda i: (0, 0)))
        args.append(norm_g.reshape(1, K))
    if has_res:
        in_specs.append(row(N))
        args.append(resid)
    if relu2:
        out_shape = [jax.ShapeDtypeStruct((M, N), BF16), jax.ShapeDtypeStruct((M, N), BF16)]
        out_specs = [row(N), row(N)]
    else:
        out_shape = [jax.ShapeDtypeStruct((M, N), F32 if has_res else out_dtype)]
        out_specs = [row(N)]
    if has_norm:
        out_shape.append(jax.ShapeDtypeStruct((M, K), BF16))
        out_specs.append(row(K))
    res = pl.pallas_call(body, grid=(M // tm,), in_specs=in_specs, out_specs=out_specs, out_shape=out_shape,
                         name=name, compiler_params=_cp(("parallel",)))(*args)
    return res if len(res) > 1 else res[0]


def _mm_nt(g, w3, *, name, out_dtype=BF16, up=None, norm_x=None, norm_g=None, dres=None, tm=512):
    M = g.shape[0]
    ns, K, n = w3.shape
    tm = min(tm, M)
    has_up = up is not None
    has_norm = norm_x is not None
    has_res = dres is not None

    def body(*refs):
        refs = list(refs)
        g_ref, w_ref = refs[0], refs[1]
        pos = 2
        if has_up:
            up_ref = refs[pos]
            pos += 1
        if has_norm:
            x_ref, gn_ref = refs[pos], refs[pos + 1]
            pos += 2
        if has_res:
            r_ref = refs[pos]
            pos += 1
        outs = refs[pos:]
        acc = None
        for s in range(ns):
            part = lax.dot_general(g_ref[:, s * n:(s + 1) * n], w_ref[s], NT, preferred_element_type=F32)
            acc = part if acc is None else acc + part
        if has_up:
            outs[0][...] = (acc * (2.0 * jnp.maximum(up_ref[...].astype(F32), 0.0))).astype(BF16)
        elif has_norm:
            xv = x_ref[...]
            rs = lax.rsqrt(jnp.mean(xv * xv, axis=-1, keepdims=True) + EPS)
            xn = xv * rs
            dxn = acc * gn_ref[...]
            dx = rs * (dxn - xn * jnp.mean(dxn * xn, axis=-1, keepdims=True))
            if has_res:
                dx = dx + r_ref[...]
            outs[0][...] = dx
            outs[1][...] = dx.astype(BF16)

            @pl.when(pl.program_id(0) == 0)
            def _():
                outs[2][...] = jnp.zeros_like(outs[2])

            outs[2][...] += jnp.sum(acc * xn, axis=0, keepdims=True)
        else:
            outs[0][...] = acc.astype(out_dtype)

    row = lambda w: pl.BlockSpec((tm, w), lambda i: (i, 0))
    in_specs = [row(ns * n), pl.BlockSpec((ns, K, n), lambda i: (0, 0, 0))]
    args = [g, w3]
    if has_up:
        in_specs.append(row(K))
        args.append(up)
    if has_norm:
        in_specs += [row(K), pl.BlockSpec((1, K), lambda i: (0, 0))]
        args += [norm_x, norm_g.reshape(1, K)]
    if has_res:
        in_specs.append(row(K))
        args.append(dres)
    if has_norm:
        out_shape = [jax.ShapeDtypeStruct((M, K), F32), jax.ShapeDtypeStruct((M, K), BF16),
                     jax.ShapeDtypeStruct((1, K), F32)]
        out_specs = [row(K), row(K), pl.BlockSpec((1, K), lambda i: (0, 0))]
        sem = ("arbitrary",)
    else:
        out_shape = [jax.ShapeDtypeStruct((M, K), BF16 if has_up else out_dtype)]
        out_specs = [row(K)]
        sem = ("parallel",)
    res = pl.pallas_call(body, grid=(M // tm,), in_specs=in_specs, out_specs=out_specs, out_shape=out_shape,
                         name=name, compiler_params=_cp(sem))(*args)
    return res if len(res) > 1 else res[0]


def _mm_tn(a, g, ns, *, name, tk=512, tm=1024):
    M, K = a.shape
    n = g.shape[1] // ns
    tm = min(tm, M)
    tk = min(tk, K)

    def body(a_ref, g_ref, o_ref):
        @pl.when(pl.program_id(2) == 0)
        def _():
            o_ref[...] = jnp.zeros_like(o_ref)

        o_ref[0] += lax.dot_general(a_ref[...], g_ref[...], TN, preferred_element_type=F32)

    return pl.pallas_call(
        body, grid=(ns, K // tk, M // tm),
        in_specs=[pl.BlockSpec((tm, tk), lambda s, k, m: (m, k)), pl.BlockSpec((tm, n), lambda s, k, m: (m, s))],
        out_specs=pl.BlockSpec((1, tk, n), lambda s, k, m: (s, k, 0)),
        out_shape=jax.ShapeDtypeStruct((ns, K, n), F32), name=name,
        compiler_params=_cp(("parallel", "parallel", "arbitrary")))(a, g)


def _final(x, gain, target, *, name="final_loss", tr=256):
    S, Dm = x.shape
    tr = min(tr, S)

    def body(x_ref, g_ref, t_ref, loss_ref, dx_ref, dxb_ref, dg_ref):
        @pl.when(pl.program_id(0) == 0)
        def _():
            loss_ref[...] = jnp.zeros_like(loss_ref)
            dg_ref[...] = jnp.zeros_like(dg_ref)

        xv = x_ref[...]
        gv = g_ref[...]
        rs = lax.rsqrt(jnp.mean(xv * xv, axis=-1, keepdims=True) + EPS)
        xn = xv * rs
        err = xn * gv - t_ref[...]
        loss_ref[...] += 0.5 * jnp.sum(jnp.mean(err * err, axis=-1, keepdims=True), axis=0, keepdims=True)
        dout = err * (1.0 / Dm)
        dg_ref[...] += jnp.sum(dout * xn, axis=0, keepdims=True)
        dxn = dout * gv
        dx = rs * (dxn - xn * jnp.mean(dxn * xn, axis=-1, keepdims=True))
        dx_ref[...] = dx
        dxb_ref[...] = dx.astype(BF16)

    row = pl.BlockSpec((tr, Dm), lambda i: (i, 0))
    return pl.pallas_call(
        body, grid=(S // tr,),
        in_specs=[row, pl.BlockSpec((1, Dm), lambda i: (0, 0)), row],
        out_specs=[pl.BlockSpec((1, 1), lambda i: (0, 0)), row, row, pl.BlockSpec((1, Dm), lambda i: (0, 0))],
        out_shape=[jax.ShapeDtypeStruct((1, 1), F32), jax.ShapeDtypeStruct((S, Dm), F32),
                   jax.ShapeDtypeStruct((S, Dm), BF16), jax.ShapeDtypeStruct((1, Dm), F32)],
        name=name, compiler_params=_cp(("arbitrary",)))(x, gain.reshape(1, Dm), target)


def _rope_tables(positions):
    half = ROPE_DIM // 2
    inv_freq = ROPE_THETA ** (-2.0 * jnp.arange(half, dtype=F32) / ROPE_DIM)
    ang = positions.astype(F32)[:, None] * inv_freq
    cos, sin = jnp.cos(ang), jnp.sin(ang)
    S = positions.shape[0]
    ones = jnp.ones((S, HEAD_DIM - ROPE_DIM), F32)
    cos64 = jnp.concatenate([cos, cos, ones], axis=1)
    sin64 = jnp.concatenate([-sin, sin, 0.0 * ones], axis=1)
    return jnp.tile(cos64, (1, 2)), jnp.tile(sin64, (1, 2))


def _rope_partner(t):
    lane = lax.broadcasted_iota(jnp.int32, t.shape, 1)
    low = (lane & (HEAD_DIM - 1)) < (ROPE_DIM // 2)
    return jnp.where(low, pltpu.roll(t, LANES - ROPE_DIM // 2, 1), pltpu.roll(t, ROPE_DIM // 2, 1))


def _qk_prep(p, cos_t, sin_t, *, name="qk_prep", tr=256):
    S = p.shape[0]
    tr = min(tr, S)
    scale = HEAD_DIM ** -0.5

    def body(p_ref, c_ref, s_ref, q_ref, k_ref, v_ref):
        cs, sn = c_ref[...], s_ref[...]
        lane = lax.broadcasted_iota(jnp.int32, (tr, LANES), 1)
        lo = lane < HEAD_DIM
        for c in range(Q_W // LANES):
            t = p_ref[:, c * LANES:(c + 1) * LANES]
            q_ref[:, c * LANES:(c + 1) * LANES] = ((t * cs + _rope_partner(t) * sn) * scale).astype(BF16)
        for c in range(KV_W // LANES):
            t = p_ref[:, Q_W + c * LANES:Q_W + (c + 1) * LANES]
            kc = t * cs + _rope_partner(t) * sn
            vc = p_ref[:, Q_W + KV_W + c * LANES:Q_W + KV_W + (c + 1) * LANES]
            for arr, ref in ((kc, k_ref), (vc, v_ref)):
                sw = pltpu.roll(arr, HEAD_DIM, 1)
                ref[:, (2 * c) * LANES:(2 * c + 1) * LANES] = jnp.where(lo, arr, sw).astype(BF16)
                ref[:, (2 * c + 1) * LANES:(2 * c + 2) * LANES] = jnp.where(lo, sw, arr).astype(BF16)

    row = lambda w: pl.BlockSpec((tr, w), lambda i: (i, 0))
    return pl.pallas_call(
        body, grid=(S // tr,), in_specs=[row(ATTN_IN_W), row(LANES), row(LANES)],
        out_specs=[row(Q_W), row(2 * KV_W), row(2 * KV_W)],
        out_shape=[jax.ShapeDtypeStruct((S, Q_W), BF16), jax.ShapeDtypeStruct((S, 2 * KV_W), BF16),
                   jax.ShapeDtypeStruct((S, 2 * KV_W), BF16)],
        name=name, compiler_params=_cp(("parallel",)))(p, cos_t, sin_t)


def _qk_prep_bwd(dq, dk, dv, dmq, cos_t, sin_t, *, name="qk_prep_bwd", tr=256):
    S = dq.shape[0]
    tr = min(tr, S)

    def body(dq_ref, dk_ref, dv_ref, dmq_ref, c_ref, s_ref, o_ref):
        cs, sn = c_ref[...], s_ref[...]
        for c in range(Q_W // LANES):
            t = dq_ref[:, c * LANES:(c + 1) * LANES]
            o_ref[:, c * LANES:(c + 1) * LANES] = (t * cs - _rope_partner(t) * sn).astype(BF16)
        for c in range(KV_W // LANES):
            t = dk_ref[:, c * LANES:(c + 1) * LANES]
            o_ref[:, Q_W + c * LANES:Q_W + (c + 1) * LANES] = (t * cs - _rope_partner(t) * sn).astype(BF16)
        o_ref[:, Q_W + KV_W:Q_W + 2 * KV_W] = dv_ref[...].astype(BF16)
        o_ref[:, Q_W + 2 * KV_W:] = dmq_ref[...]

    row = lambda w: pl.BlockSpec((tr, w), lambda i: (i, 0))
    return pl.pallas_call(
        body, grid=(S // tr,), in_specs=[row(Q_W), row(KV_W), row(KV_W), row(MEM_W), row(LANES), row(LANES)],
        out_specs=row(ATTN_IN_W), out_shape=jax.ShapeDtypeStruct((S, ATTN_IN_W), BF16),
        name=name, compiler_params=_cp(("parallel",)))(dq, dk, dv, dmq, cos_t, sin_t)


def _band(n, S):
    start = pl.multiple_of(jnp.clip((n - 1) * BLOCK, 0, S - 3 * BLOCK), BLOCK)
    qi = (lax.broadcasted_iota(jnp.int32, (4 * BLOCK, 3 * BLOCK), 0) & (BLOCK - 1)) + n * BLOCK
    ki = lax.broadcasted_iota(jnp.int32, (4 * BLOCK, 3 * BLOCK), 1) + start
    return start, jnp.abs(ki - qi) <= WINDOW


def _stack_heads(ref, g, lo):
    parts = []
    for j in range(4):
        c = 2 * g + j // 2
        t = ref[:, c * LANES:(c + 1) * LANES].astype(F32)
        parts.append(jnp.where(lo if j % 2 == 0 else jnp.logical_not(lo), t, 0.0).astype(BF16))
    return jnp.concatenate(parts, axis=0)


def _sink_col(sink_ref, g):
    return jnp.concatenate([jnp.full((BLOCK, 1), sink_ref[4 * g + j], F32) for j in range(4)], axis=0)


def _attn_fwd(q, kd, vd, sinks, *, name="attn_fwd"):
    S = q.shape[0]

    def body(sink_ref, q_ref, k_ref, v_ref, o_ref, lse_ref):
        n = pl.program_id(0)
        start, mask = _band(n, S)
        lane = lax.broadcasted_iota(jnp.int32, (BLOCK, LANES), 1)
        lo = lane < HEAD_DIM
        lse_blk = jnp.zeros((BLOCK, LANES), F32)
        for g in range(ATTN_KV_HEADS):
            kg = k_ref[pl.ds(start, 3 * BLOCK), g * LANES:(g + 1) * LANES]
            vg = v_ref[pl.ds(start, 3 * BLOCK), g * LANES:(g + 1) * LANES]
            qst = _stack_heads(q_ref, g, lo)
            s = lax.dot_general(qst, kg, NT, preferred_element_type=F32)
            s = jnp.where(mask, s, NEG)
            sk = _sink_col(sink_ref, g)
            m = jnp.maximum(jnp.max(s, axis=-1, keepdims=True), sk)
            pe = jnp.exp(s - m)
            l = jnp.sum(pe, axis=-1, keepdims=True) + jnp.exp(sk - m)
            pv = jnp.dot((pe * (1.0 / l)).astype(BF16), vg, preferred_element_type=F32)
            for jj in range(2):
                oc = jnp.where(lo, pv[(2 * jj) * BLOCK:(2 * jj + 1) * BLOCK], pv[(2 * jj + 1) * BLOCK:(2 * jj + 2) * BLOCK])
                o_ref[:, (2 * g + jj) * LANES:(2 * g + jj + 1) * LANES] = oc.astype(BF16)
            lse = m + jnp.log(l)
            for j in range(4):
                lse_blk = jnp.where(lane == 4 * g + j, lse[j * BLOCK:(j + 1) * BLOCK], lse_blk)
        lse_ref[...] = lse_blk

    full = lambda w: pl.BlockSpec((S, w), lambda i: (0, 0))
    return pl.pallas_call(
        body, grid=(S // BLOCK,),
        in_specs=[pl.BlockSpec(memory_space=pltpu.SMEM), pl.BlockSpec((BLOCK, Q_W), lambda i: (i, 0)),
                  full(2 * KV_W), full(2 * KV_W)],
        out_specs=[pl.BlockSpec((BLOCK, Q_W), lambda i: (i, 0)), pl.BlockSpec((BLOCK, LANES), lambda i: (i, 0))],
        out_shape=[jax.ShapeDtypeStruct((S, Q_W), BF16), jax.ShapeDtypeStruct((S, LANES), F32)],
        name=name, compiler_params=_cp(("parallel",)))(sinks, q, kd, vd)


def _attn_bwd(q, kd, vd, lse, sinks, dcat, *, name="attn_bwd"):
    S = q.shape[0]
    scale = HEAD_DIM ** -0.5

    def body(sink_ref, q_ref, k_ref, v_ref, lse_ref, do_ref, dq_ref, dk_ref, dv_ref, ds_ref):
        n = pl.program_id(0)

        @pl.when(n == 0)
        def _():
            dk_ref[...] = jnp.zeros_like(dk_ref)
            dv_ref[...] = jnp.zeros_like(dv_ref)
            ds_ref[...] = jnp.zeros_like(ds_ref)

        start, mask = _band(n, S)
        lane = lax.broadcasted_iota(jnp.int32, (BLOCK, LANES), 1)
        lo = lane < HEAD_DIM
        lane3 = lax.broadcasted_iota(jnp.int32, (3 * BLOCK, LANES), 1)
        row8 = lax.broadcasted_iota(jnp.int32, (8, LANES), 0)
        lane8 = lax.broadcasted_iota(jnp.int32, (8, LANES), 1)
        dsink = jnp.zeros((8, LANES), F32)
        lse_blk = lse_ref[...]
        for g in range(ATTN_KV_HEADS):
            rows = pl.ds(start, 3 * BLOCK)
            cols = slice((g // 2) * LANES, (g // 2 + 1) * LANES)
            kg = k_ref[rows, g * LANES:(g + 1) * LANES]
            vg = v_ref[rows, g * LANES:(g + 1) * LANES]
            qst = _stack_heads(q_ref, g, lo)
            dost = _stack_heads(do_ref, g, lo)
            s = lax.dot_general(qst, kg, NT, preferred_element_type=F32)
            s = jnp.where(mask, s, NEG)
            lse_col = jnp.concatenate(
                [jnp.sum(jnp.where(lane == 4 * g + j, lse_blk, 0.0), axis=1, keepdims=True) for j in range(4)], axis=0)
            p = jnp.exp(s - lse_col)
            dp = lax.dot_general(dost, vg, NT, preferred_element_type=F32)
            delta = jnp.sum(p * dp, axis=-1, keepdims=True)
            dsb = (p * (dp - delta)).astype(BF16)
            dqs = jnp.dot(dsb, kg, preferred_element_type=F32) * scale
            for jj in range(2):
                dq_ref[:, (2 * g + jj) * LANES:(2 * g + jj + 1) * LANES] = jnp.where(
                    lo, dqs[(2 * jj) * BLOCK:(2 * jj + 1) * BLOCK], dqs[(2 * jj + 1) * BLOCK:(2 * jj + 2) * BLOCK])
            half = (lane3 < HEAD_DIM) if g % 2 == 0 else (lane3 >= HEAD_DIM)
            dkr = lax.dot_general(dsb, qst, TN, preferred_element_type=F32)
            dk_ref[rows, cols] += jnp.where(half, dkr + pltpu.roll(dkr, HEAD_DIM, 1), 0.0)
            dvr = lax.dot_general(p.astype(BF16), dost, TN, preferred_element_type=F32)
            dv_ref[rows, cols] += jnp.where(half, dvr + pltpu.roll(dvr, HEAD_DIM, 1), 0.0)
            contrib = jnp.exp(_sink_col(sink_ref, g) - lse_col) * delta
            for j in range(4):
                val = -jnp.sum(contrib[j * BLOCK:(j + 1) * BLOCK], axis=0, keepdims=True)
                dsink = dsink + jnp.where((row8 == 0) & (lane8 == 4 * g + j), val, 0.0)
        ds_ref[...] += dsink

    full = lambda w: pl.BlockSpec((S, w), lambda i: (0, 0))
    blk = lambda w: pl.BlockSpec((BLOCK, w), lambda i: (i, 0))
    return pl.pallas_call(
        body, grid=(S // BLOCK,),
        in_specs=[pl.BlockSpec(memory_space=pltpu.SMEM), blk(Q_W), full(2 * KV_W), full(2 * KV_W), blk(LANES), blk(Q_W)],
        out_specs=[blk(Q_W), full(KV_W), full(KV_W), pl.BlockSpec((8, LANES), lambda i: (0, 0))],
        out_shape=[jax.ShapeDtypeStruct((S, Q_W), F32), jax.ShapeDtypeStruct((S, KV_W), F32),
                   jax.ShapeDtypeStruct((S, KV_W), F32), jax.ShapeDtypeStruct((8, LANES), F32)],
        name=name, compiler_params=_cp(("arbitrary",)))(sinks, q, kd, vd, lse, dcat)


def _mem_probs(q_ref, kv_ref, h):
    scale = MEM_HEAD_DIM ** -0.5
    qh = q_ref[:, h * LANES:(h + 1) * LANES].astype(BF16)
    s = lax.dot_general(qh, kv_ref[:, h * LANES:(h + 1) * LANES], NT, preferred_element_type=F32) * scale
    m = jnp.max(s, axis=-1, keepdims=True)
    pe = jnp.exp(s - m)
    return qh, pe * (1.0 / jnp.sum(pe, axis=-1, keepdims=True))


def _memattn_fwd(p, qblk, kv, *, name="memattn_fwd", tr=512):
    S = p.shape[0]
    tr = min(tr, S)

    def body(q_ref, kv_ref, o_ref):
        for h in range(MEM_HEADS):
            _, pr = _mem_probs(q_ref, kv_ref, h)
            o = jnp.dot(pr.astype(BF16), kv_ref[:, MEM_W + h * LANES:MEM_W + (h + 1) * LANES], preferred_element_type=F32)
            o_ref[:, h * LANES:(h + 1) * LANES] = o.astype(BF16)

    return pl.pallas_call(
        body, grid=(S // tr,),
        in_specs=[pl.BlockSpec((tr, MEM_W), lambda i: (i, qblk)), pl.BlockSpec((MEM_LEN, 2 * MEM_W), lambda i: (0, 0))],
        out_specs=pl.BlockSpec((tr, MEM_W), lambda i: (i, 0)),
        out_shape=jax.ShapeDtypeStruct((S, MEM_W), BF16), name=name, compiler_params=_cp(("parallel",)))(p, kv)


def _memattn_bwd(p, qblk, kv, dcat, *, name="memattn_bwd", tr=512):
    S = p.shape[0]
    tr = min(tr, S)
    scale = MEM_HEAD_DIM ** -0.5

    def body(q_ref, kv_ref, do_ref, dq_ref, dkv_ref):
        @pl.when(pl.program_id(0) == 0)
        def _():
            dkv_ref[...] = jnp.zeros_like(dkv_ref)

        for h in range(MEM_HEADS):
            qh, pr = _mem_probs(q_ref, kv_ref, h)
            doh = do_ref[:, h * LANES:(h + 1) * LANES]
            dp = lax.dot_general(doh, kv_ref[:, MEM_W + h * LANES:MEM_W + (h + 1) * LANES], NT, preferred_element_type=F32)
            delta = jnp.sum(pr * dp, axis=-1, keepdims=True)
            dsb = (pr * (dp - delta) * scale).astype(BF16)
            dq = jnp.dot(dsb, kv_ref[:, h * LANES:(h + 1) * LANES], preferred_element_type=F32)
            dq_ref[:, h * LANES:(h + 1) * LANES] = dq.astype(BF16)
            dkv_ref[:, h * LANES:(h + 1) * LANES] += lax.dot_general(dsb, qh, TN, preferred_element_type=F32)
            dkv_ref[:, MEM_W + h * LANES:MEM_W + (h + 1) * LANES] += lax.dot_general(
                pr.astype(BF16), doh, TN, preferred_element_type=F32)

    return pl.pallas_call(
        body, grid=(S // tr,),
        in_specs=[pl.BlockSpec((tr, MEM_W), lambda i: (i, qblk)), pl.BlockSpec((MEM_LEN, 2 * MEM_W), lambda i: (0, 0)),
                  pl.BlockSpec((tr, MEM_W), lambda i: (i, Q_W // MEM_W))],
        out_specs=[pl.BlockSpec((tr, MEM_W), lambda i: (i, 0)), pl.BlockSpec((MEM_LEN, 2 * MEM_W), lambda i: (0, 0))],
        out_shape=[jax.ShapeDtypeStruct((S, MEM_W), BF16), jax.ShapeDtypeStruct((MEM_LEN, 2 * MEM_W), F32)],
        name=name, compiler_params=_cp(("arbitrary",)))(p, kv, dcat)


def _sigmoid(z):
    return 1.0 / (1.0 + jnp.exp(-z))


def _expm1(z):
    poly = z * (1.0 + z * (0.5 + z * (1.0 / 6.0 + z * (1.0 / 24.0 + z * (1.0 / 120.0)))))
    return jnp.where(jnp.abs(z) < 0.1, poly, jnp.exp(z) - 1.0)


def _softplus_neg(lam):
    z = -lam
    return jnp.maximum(z, 0.0) + jnp.log(1.0 + jnp.exp(-jnp.abs(z)))


_GELU_C = math.sqrt(2.0 / math.pi)


def _gelu(z):
    return 0.5 * z * (1.0 + jnp.tanh(_GELU_C * (z + 0.044715 * z * z * z)))


def _gelu_grad(z):
    t = jnp.tanh(_GELU_C * (z + 0.044715 * z * z * z))
    return 0.5 * (1.0 + t) + 0.5 * z * (1.0 - t * t) * _GELU_C * (1.0 + 3.0 * 0.044715 * z * z)


def _row_or_zero(ref, t, S):
    ok = jnp.logical_and(t >= 0, t < S)
    return jnp.where(ok, ref[pl.ds(jnp.clip(t, 0, S - 1), 1), :], 0.0)


def _shift_down(v, first):
    ri = lax.broadcasted_iota(jnp.int32, v.shape, 0)
    return jnp.where(ri == 0, first, pltpu.roll(v, 1, 0))


def _shift_up(v, last):
    T = v.shape[0]
    ri = lax.broadcasted_iota(jnp.int32, v.shape, 0)
    return jnp.where(ri == T - 1, last, pltpu.roll(v, T - 1, 0))


def _scan_chunk(a, u, reverse):
    T = a.shape[0]
    ri = lax.broadcasted_iota(jnp.int32, a.shape, 0)
    d = 1
    while d < T:
        if reverse:
            a_s, u_s, ok = pltpu.roll(a, T - d, 0), pltpu.roll(u, T - d, 0), ri < T - d
        else:
            a_s, u_s, ok = pltpu.roll(a, d, 0), pltpu.roll(u, d, 0), ri >= d
        u = jnp.where(ok, a * u_s + u, u)
        a = jnp.where(ok, a * a_s, a)
        d *= 2
    return a, u


def _conv_taps(xb_ref, t0, S):
    T = SCAN_ROWS
    x0 = xb_ref[pl.ds(t0, T), :]
    xm1 = _shift_down(x0, _row_or_zero(xb_ref, t0 - 1, S))
    nxt0 = _row_or_zero(xb_ref, t0 + T, S)
    xp1 = _shift_up(x0, nxt0)
    xp2 = _shift_up(xp1, _row_or_zero(xb_ref, t0 + T + 1, S))
    return xm1, x0, xp1, xp2


def _lru_gates(xc, w_a, b_a, w_x, b_x, sp):
    xcb = xc.astype(BF16)
    r = _sigmoid(jnp.dot(xcb, w_a, preferred_element_type=F32) + b_a)
    i = _sigmoid(jnp.dot(xcb, w_x, preferred_element_type=F32) + b_x)
    la = -LRU_C * r * sp
    a = jnp.exp(la)
    beta = jnp.sqrt(-_expm1(2.0 * la))
    return r, i, a, beta


def _lru_specs(S):
    col = lambda off: pl.BlockSpec((S, LANES), lambda n: (0, n + off), pipeline_mode=pl.Buffered(1))
    small = lambda r: pl.BlockSpec((r, LANES), lambda n: (0, n))
    wblk = pl.BlockSpec((2, 1, LANES, LANES), lambda n: (0, n, 0, 0))
    return col, small, wblk


def _lru_fwd(p, conv_w, conv_b, wa, ba, wx, bx, lam, *, name="lru_fwd"):
    S = p.shape[0]
    T = SCAN_ROWS
    nc = S // T

    def body(xb_ref, gate_ref, cw_ref, cb_ref, wa_ref, ba_ref, wx_ref, bx_ref, lam_ref, y_ref, hf_ref, hr_ref, xc_v):
        sp = _softplus_neg(lam_ref[...])
        cw = cw_ref[...]

        def fwd_step(c, h_in):
            t0 = pl.multiple_of(c * T, T)
            xm1, x0, xp1, xp2 = _conv_taps(xb_ref, t0, S)
            xc = cb_ref[...] + xm1 * cw[0:1] + x0 * cw[1:2] + xp1 * cw[2:3] + xp2 * cw[3:4]
            xc_v[pl.ds(t0, T), :] = xc
            _, i, a, beta = _lru_gates(xc, wa_ref[0, 0], ba_ref[0:1], wx_ref[0, 0], bx_ref[0:1], sp[0:1])
            A, U = _scan_chunk(a, beta * (i * xc), False)
            hf_ref[pl.ds(t0, T), :] = A * h_in + U
            return hf_ref[pl.ds(t0 + T - 1, 1), :]

        lax.fori_loop(0, nc, fwd_step, jnp.zeros((1, LANES), F32))

        def rev_step(k, h_in):
            t0 = pl.multiple_of((nc - 1 - k) * T, T)
            xc = xc_v[pl.ds(t0, T), :]
            _, i, a, beta = _lru_gates(xc, wa_ref[1, 0], ba_ref[1:2], wx_ref[1, 0], bx_ref[1:2], sp[1:2])
            A, U = _scan_chunk(a, beta * (i * xc), True)
            h = A * h_in + U
            hr_ref[pl.ds(t0, T), :] = h
            y_ref[pl.ds(t0, T), :] = ((hf_ref[pl.ds(t0, T), :] + h) * _gelu(gate_ref[pl.ds(t0, T), :])).astype(BF16)
            return hr_ref[pl.ds(t0, 1), :]

        lax.fori_loop(0, nc, rev_step, jnp.zeros((1, LANES), F32))

    col, small, wblk = _lru_specs(S)
    colo = lambda: pl.BlockSpec((S, LANES), lambda n: (0, n))
    return pl.pallas_call(
        body, grid=(LRU_BLOCKS,),
        in_specs=[col(0), col(LRU_BLOCKS), small(4), small(1), wblk, small(2), wblk, small(2), small(2)],
        out_specs=[colo(), colo(), colo()],
        out_shape=[jax.ShapeDtypeStruct((S, D_MODEL), BF16), jax.ShapeDtypeStruct((S, D_MODEL), F32),
                   jax.ShapeDtypeStruct((S, D_MODEL), F32)],
        scratch_shapes=[pltpu.VMEM((S, LANES), F32)],
        name=name, compiler_params=_cp(("parallel",)))(p, p, conv_w, conv_b, wa, ba, wx, bx, lam)


def _lru_bwd(p, hf, hr, dcat, conv_w, conv_b, wa, ba, wx, bx, lam, *, name="lru_bwd"):
    S = p.shape[0]
    T = SCAN_ROWS
    nc = S // T

    def body(xb_ref, gate_ref, hf_ref, hr_ref, dy_ref, cw_ref, cb_ref, wa_ref, ba_ref, wx_ref, bx_ref, lam_ref,
             dxb_ref, dgate_ref, dcw_ref, dcb_ref, dwa_ref, dba_ref, dwx_ref, dbx_ref, dlam_ref, xc_v, dxc_v):
        lam_v = lam_ref[...]
        sp = _softplus_neg(lam_v)
        cw = cw_ref[...]
        for ref in (dcw_ref, dcb_ref, dwa_ref, dba_ref, dwx_ref, dbx_ref, dlam_ref):
            ref[...] = jnp.zeros_like(ref)

        def prep_step(c, carry):
            t0 = pl.multiple_of(c * T, T)
            rows = pl.ds(t0, T)
            xm1, x0, xp1, xp2 = _conv_taps(xb_ref, t0, S)
            xc_v[rows, :] = cb_ref[...] + xm1 * cw[0:1] + x0 * cw[1:2] + xp1 * cw[2:3] + xp2 * cw[3:4]
            dgate_ref[rows, :] = (dy_ref[rows, :].astype(F32) * (hf_ref[rows, :] + hr_ref[rows, :])
                                  * _gelu_grad(gate_ref[rows, :])).astype(BF16)
            return carry

        lax.fori_loop(0, nc, prep_step, 0)

        def direction(d):
            h_ref = hf_ref if d == 0 else hr_ref
            w_a, w_x = wa_ref[d, 0], wx_ref[d, 0]
            b_a, b_x, sp_d = ba_ref[d:d + 1], bx_ref[d:d + 1], sp[d:d + 1]

            def step(k, carry):
                g_in, a_in = carry
                c = (nc - 1 - k) if d == 0 else k
                t0 = pl.multiple_of(c * T, T)
                rows = pl.ds(t0, T)
                xc = xc_v[rows, :]
                r, i, a, beta = _lru_gates(xc, w_a, b_a, w_x, b_x, sp_d)
                dh = dy_ref[rows, :].astype(F32) * _gelu(gate_ref[rows, :])
                hc = h_ref[rows, :]
                if d == 0:
                    A, U = _scan_chunk(_shift_up(a, a_in), dh, True)
                    g = A * g_in + U
                    h_nb = _shift_down(hc, _row_or_zero(h_ref, t0 - 1, S))
                    nxt = (g[0:1], a[0:1])
                else:
                    A, U = _scan_chunk(_shift_down(a, a_in), dh, False)
                    g = A * g_in + U
                    h_nb = _shift_up(hc, _row_or_zero(h_ref, t0 + T, S))
                    nxt = (g[T - 1:T], a[T - 1:T])
                da = g * h_nb
                dbeta = g * (i * xc)
                tb = g * beta
                dla = da * a - dbeta * (a * a / beta)
                dzr = (dla * (-LRU_C * sp_d)) * (r * (1.0 - r))
                dzi = (tb * xc) * (i * (1.0 - i))
                dzrb, dzib, xcb = dzr.astype(BF16), dzi.astype(BF16), xc.astype(BF16)
                dwa_ref[d, 0] += lax.dot_general(xcb, dzrb, TN, preferred_element_type=F32)
                dwx_ref[d, 0] += lax.dot_general(xcb, dzib, TN, preferred_element_type=F32)
                dba_ref[d:d + 1] += jnp.sum(dzr, axis=0, keepdims=True)
                dbx_ref[d:d + 1] += jnp.sum(dzi, axis=0, keepdims=True)
                dlam_ref[d:d + 1] += jnp.sum(dla * (-LRU_C * r), axis=0, keepdims=True)
                dxc = (tb * i + lax.dot_general(dzrb, w_a, NT, preferred_element_type=F32)
                       + lax.dot_general(dzib, w_x, NT, preferred_element_type=F32))
                if d == 0:
                    dxc_v[rows, :] = dxc
                else:
                    dxc_v[rows, :] += dxc
                return nxt

            lax.fori_loop(0, nc, step, (jnp.zeros((1, LANES), F32), jnp.zeros((1, LANES), F32)))

        direction(0)
        direction(1)
        dlam_ref[...] = dlam_ref[...] * (-1.0 / (1.0 + jnp.exp(lam_v)))

        def conv_step(c, carry):
            t0 = pl.multiple_of(c * T, T)
            rows = pl.ds(t0, T)
            g0 = dxc_v[rows, :]
            gm1 = _shift_down(g0, _row_or_zero(dxc_v, t0 - 1, S))
            gm2 = _shift_down(gm1, _row_or_zero(dxc_v, t0 - 2, S))
            gp1 = _shift_up(g0, _row_or_zero(dxc_v, t0 + T, S))
            dxb_ref[rows, :] = (cw[0:1] * gp1 + cw[1:2] * g0 + cw[2:3] * gm1 + cw[3:4] * gm2).astype(BF16)
            xm1, x0, xp1, xp2 = _conv_taps(xb_ref, t0, S)
            for tap, xs in enumerate((xm1, x0, xp1, xp2)):
                dcw_ref[tap:tap + 1] += jnp.sum(g0 * xs, axis=0, keepdims=True)
            dcb_ref[...] += jnp.sum(g0, axis=0, keepdims=True)
            return carry

        lax.fori_loop(0, nc, conv_step, 0)

    col, small, wblk = _lru_specs(S)
    colo = lambda: pl.BlockSpec((S, LANES), lambda n: (0, n), pipeline_mode=pl.Buffered(1))
    return pl.pallas_call(
        body, grid=(LRU_BLOCKS,),
        in_specs=[col(0), col(LRU_BLOCKS), col(0), col(0), col(0), small(4), small(1), wblk, small(2), wblk, small(2), small(2)],
        out_specs=[colo(), colo(), small(4), small(1), wblk, small(2), wblk, small(2), small(2)],
        out_shape=[jax.ShapeDtypeStruct((S, D_MODEL), BF16), jax.ShapeDtypeStruct((S, D_MODEL), BF16),
                   jax.ShapeDtypeStruct((4, D_MODEL), F32), jax.ShapeDtypeStruct((1, D_MODEL), F32),
                   jax.ShapeDtypeStruct((2, LRU_BLOCKS, LANES, LANES), F32), jax.ShapeDtypeStruct((2, D_MODEL), F32),
                   jax.ShapeDtypeStruct((2, LRU_BLOCKS, LANES, LANES), F32), jax.ShapeDtypeStruct((2, D_MODEL), F32),
                   jax.ShapeDtypeStruct((2, D_MODEL), F32)],
        scratch_shapes=[pltpu.VMEM((S, LANES), F32), pltpu.VMEM((S, LANES), F32)],
        name=name, compiler_params=_cp(("parallel",)))(p, p, hf, hr, dcat, conv_w, conv_b, wa, ba, wx, bx, lam)


def _mlp_fwd(x, w_up, w_down, gain, l):
    up, act, h = _mm_nn(x, w_up, norm_g=gain, relu2=True, name=f"mlp_up{l}")
    return _mm_nn(act, w_down, resid=x, name=f"mlp_down{l}"), (up, act, h)


def _mlp_bwd(x, dx, dxb, saved, w_up, w_down, gain, l):
    up, act, h = saved
    g_down = _mm_tn(act, dxb, 1, name=f"dw_down{l}")
    dup = _mm_nt(dxb, w_down, up=up, name=f"d_up{l}")
    g_up = _mm_tn(h, dup, N_CHIPS, name=f"dw_up{l}")
    dx, dxb, g_gain = _mm_nt(dup, w_up, norm_x=x, norm_g=gain, dres=dx, name=f"d_mlp_in{l}")
    return dx, dxb, g_down, g_up, g_gain


def _local_step(x, mem, positions, target, W):
    S = x.shape[0]
    cos_t, sin_t = _rope_tables(positions)
    sinks = W["attn_sinks"].reshape(ATTN_HEADS)
    G = {}

    kv0, memn = _mm_nn(mem, W["w_mem_kv"][0], norm_g=W["mem_norm"], out_dtype=BF16, name="mem_kv0", tm=256)
    kv1 = _mm_nn(memn, W["w_mem_kv"][1], out_dtype=BF16, name="mem_kv1", tm=256)
    p0, h0 = _mm_nn(x, W["attn_w_in"], norm_g=W["mix_norm"][0], name="attn_in")
    q, kd, vd = _qk_prep(p0, cos_t, sin_t)
    ao, lse = _attn_fwd(q, kd, vd, sinks)
    mo0 = _memattn_fwd(p0, Q_W // MEM_W + 1, kv0, name="memattn_fwd0")
    cat0 = jnp.concatenate([ao, mo0], axis=1)
    x1 = _mm_nn(cat0, W["w_out"][0], resid=x, name="mix_out0")
    x2, mlp0 = _mlp_fwd(x1, W["w_up"][0], W["w_down"][0], W["mlp_norm"][0], 0)
    p1, h2 = _mm_nn(x2, W["lru_w_in"], norm_g=W["mix_norm"][1], name="lru_in")
    lru_w = (W["lru_conv_w"], W["lru_conv_b"], W["lru_wa"], W["lru_ba"], W["lru_wx"], W["lru_bx"], W["lru_lambda"])
    y, hf, hr = _lru_fwd(p1, *lru_w)
    mo1 = _memattn_fwd(p1, 2 * D_MODEL // MEM_W, kv1, name="memattn_fwd1")
    cat1 = jnp.concatenate([y, mo1], axis=1)
    x3 = _mm_nn(cat1, W["w_out"][1], resid=x2, name="mix_out1")
    x4, mlp1 = _mlp_fwd(x3, W["w_up"][1], W["w_down"][1], W["mlp_norm"][1], 1)
    loss, dx, dxb, G["final_norm"] = _final(x4, W["final_norm"], target)

    dx, dxb, gd1, gu1, gm1 = _mlp_bwd(x3, dx, dxb, mlp1, W["w_up"][1], W["w_down"][1], W["mlp_norm"][1], 1)
    go1 = _mm_tn(cat1, dxb, 1, name="dw_out1")
    dcat1 = _mm_nt(dxb, W["w_out"][1], name="d_mix1")
    dmq1, dkv1 = _memattn_bwd(p1, 2 * D_MODEL // MEM_W, kv1, dcat1, name="memattn_bwd1")
    (dxb1, dgate, G["lru_conv_w"], G["lru_conv_b"], G["lru_wa"], G["lru_ba"], G["lru_wx"], G["lru_bx"],
     G["lru_lambda"]) = _lru_bwd(p1, hf, hr, dcat1, *lru_w)
    dp1 = jnp.concatenate([dxb1, dgate, dmq1], axis=1)
    G["lru_w_in"] = _mm_tn(h2, dp1, N_CHIPS, name="dw_lru_in")
    dx, dxb, gx1 = _mm_nt(dp1, W["lru_w_in"], norm_x=x2, norm_g=W["mix_norm"][1], dres=dx, name="d_lru_in")

    dx, dxb, gd0, gu0, gm0 = _mlp_bwd(x1, dx, dxb, mlp0, W["w_up"][0], W["w_down"][0], W["mlp_norm"][0], 0)
    go0 = _mm_tn(cat0, dxb, 1, name="dw_out0")
    dcat0 = _mm_nt(dxb, W["w_out"][0], name="d_mix0")
    dmq0, dkv0 = _memattn_bwd(p0, Q_W // MEM_W + 1, kv0, dcat0, name="memattn_bwd0")
    dq, dk, dv, dsink = _attn_bwd(q, kd, vd, lse, sinks, dcat0)
    dp0 = _qk_prep_bwd(dq, dk, dv, dmq0, cos_t, sin_t)
    G["attn_w_in"] = _mm_tn(h0, dp0, N_CHIPS, name="dw_attn_in")
    dx, _, gx0 = _mm_nt(dp0, W["attn_w_in"], norm_x=x, norm_g=W["mix_norm"][0], dres=dx, name="d_attn_in")

    dkv0b, dkv1b = dkv0.astype(BF16), dkv1.astype(BF16)
    gk0 = _mm_tn(memn, dkv0b, 1, name="dw_kv0", tm=256)
    gk1 = _mm_tn(memn, dkv1b, 1, name="dw_kv1", tm=256)
    w_kv_both = jnp.concatenate([W["w_mem_kv"][0], W["w_mem_kv"][1]], axis=0)
    _, _, G["mem_norm"] = _mm_nt(jnp.concatenate([dkv0b, dkv1b], axis=1), w_kv_both, norm_x=mem, norm_g=W["mem_norm"],
                                 name="d_mem", tm=256)

    G["w_mem_kv"] = (gk0, gk1)
    G["w_out"] = (go0, go1)
    G["w_up"] = (gu0, gu1)
    G["w_down"] = (gd0, gd1)
    G["mix_norm"] = jnp.concatenate([gx0, gx1], axis=0)
    G["mlp_norm"] = jnp.concatenate([gm0, gm1], axis=0)
    G["attn_sinks"] = dsink[0:1, 0:ATTN_HEADS]
    return loss[0, 0], dx, G


HBM = pl.BlockSpec(memory_space=pl.ANY)


def _place():
    x, y, c = lax.axis_index("x"), lax.axis_index("y"), lax.axis_index("c")
    chips = [(1 - x, y), (x, 1 - y), (1 - x, 1 - y)]
    return x, y, c, chips


def _remote(src, dst, send_sems, recv_sems, k, to):
    return pltpu.make_async_remote_copy(src_ref=src, dst_ref=dst, send_sem=send_sems.at[k], recv_sem=recv_sems.at[k],
                                        device_id=to, device_id_type=MESH)


def _comm_call(body, out_shape, n_sems, name, *args, alias=None):
    return pl.pallas_call(
        body, out_shape=out_shape, in_specs=[HBM] * len(args), out_specs=HBM,
        scratch_shapes=[pltpu.SemaphoreType.DMA((n_sems,)), pltpu.SemaphoreType.DMA((n_sems,))],
        input_output_aliases=alias or {}, name=name)(*args)


def _place_slot(shard, slot, n_slots, *, name, tr):
    R, C = shard.shape

    def body(s_ref, a_ref, o_ref):
        o_ref[0] = a_ref[...]

    return pl.pallas_call(
        body,
        grid_spec=pltpu.PrefetchScalarGridSpec(
            num_scalar_prefetch=1, grid=(R // tr,), in_specs=[pl.BlockSpec((tr, C), lambda i, s_ref: (i, 0))],
            out_specs=pl.BlockSpec((1, tr, C), lambda i, s_ref: (s_ref[0], i, 0))),
        out_shape=jax.ShapeDtypeStruct((n_slots, R, C), shard.dtype), name=name,
        compiler_params=_cp(("parallel",)))(slot, shard)


def _allgather_chips(buf, *, name, forward_to_sibling):
    _, R, C = buf.shape
    half = R // 2

    def body(b_ref, o_ref, send_sems, recv_sems):
        x, y, c, chips = _place()
        me = 2 * x + y
        if forward_to_sibling:
            my_rows = pl.ds(pl.multiple_of(c * half, 16), half)
            sib_rows = pl.ds(pl.multiple_of((1 - c) * half, 16), half)
        else:
            my_rows = sib_rows = pl.ds(0, R)
        own = o_ref.at[me, my_rows]
        sends = [_remote(own, own, send_sems, recv_sems, j, (cx, cy, c)) for j, (cx, cy) in enumerate(chips)]
        for cp in sends:
            cp.start()
        passed = []
        for j, (cx, cy) in enumerate(chips):
            landed = o_ref.at[2 * cx + cy, my_rows]
            _remote(landed, landed, send_sems, recv_sems, j, (cx, cy, c)).wait_recv()
            if forward_to_sibling:
                fw = _remote(landed, landed, send_sems, recv_sems, 3 + j, (x, y, 1 - c))
                fw.start()
                passed.append(fw)
        if forward_to_sibling:
            for j, (cx, cy) in enumerate(chips):
                got = o_ref.at[2 * cx + cy, sib_rows]
                _remote(got, got, send_sems, recv_sems, 3 + j, (x, y, 1 - c)).wait_recv()
        for cp in sends + passed:
            cp.wait_send()

    return _comm_call(body, jax.ShapeDtypeStruct(buf.shape, buf.dtype), 6, name, buf, alias={0: 0})


def _sibling_exchange(g, *, name):
    _, R, C = g.shape
    half = R // 2

    def body(g_ref, o_ref, send_sems, recv_sems):
        x, y, c, _ = _place()
        other = pl.ds(pl.multiple_of((1 - c) * half, 8), half)
        cps = [_remote(g_ref.at[s, other], o_ref.at[s], send_sems, recv_sems, s, (x, y, 1 - c)) for s in range(N_CHIPS)]
        for cp in cps:
            cp.start()
        for cp in cps:
            cp.wait()

    return _comm_call(body, jax.ShapeDtypeStruct((N_CHIPS, half, C), g.dtype), N_CHIPS, name, g)


def _chip_exchange(h, parts, *, name):
    def body(h_ref, p_ref, o_ref, send_sems, recv_sems):
        x, y, c, chips = _place()
        me = 2 * x + y
        cps = [_remote(h_ref.at[2 * cx + cy], o_ref.at[me], send_sems, recv_sems, j, (cx, cy, c))
               for j, (cx, cy) in enumerate(chips)]
        for cp in cps:
            cp.start()
        for j, (cx, cy) in enumerate(chips):
            got = o_ref.at[2 * cx + cy]
            _remote(got, got, send_sems, recv_sems, j, (cx, cy, c)).wait_recv()
        for cp in cps:
            cp.wait_send()

    return _comm_call(body, jax.ShapeDtypeStruct(parts.shape, parts.dtype), 3, name, h, parts, alias={1: 0})


def _sibling_allgather(full, *, name):
    R, C = full.shape
    half = R // 2

    def body(f_ref, o_ref, send_sems, recv_sems):
        x, y, c, _ = _place()
        mine = o_ref.at[pl.ds(pl.multiple_of(c * half, 8), half)]
        cp = _remote(mine, mine, send_sems, recv_sems, 0, (x, y, 1 - c))
        cp.start()
        got = o_ref.at[pl.ds(pl.multiple_of((1 - c) * half, 8), half)]
        _remote(got, got, send_sems, recv_sems, 0, (x, y, 1 - c)).wait_recv()
        cp.wait_send()

    return _comm_call(body, jax.ShapeDtypeStruct(full.shape, full.dtype), 1, name, full, alias={0: 0})


def _sum_halves(g, recv, place, *, name="sum_halves", tr=480):
    _, R, C = g.shape
    half = R // 2
    nblk = half // tr

    def body(pl_ref, g_ref, r_ref, o_ref, own_ref):
        v = (g_ref[...] + r_ref[...]).astype(BF16)
        o_ref[...] = v

        @pl.when(pl.program_id(1) == pl_ref[1])
        def _():
            own_ref[...] = v

    blk = pl.BlockSpec((1, tr, C), lambda i, s, p: (s, i, 0))
    return pl.pallas_call(
        body,
        grid_spec=pltpu.PrefetchScalarGridSpec(
            num_scalar_prefetch=1, grid=(nblk, N_CHIPS),
            in_specs=[pl.BlockSpec((1, tr, C), lambda i, s, p: (s, p[0] * nblk + i, 0)), blk],
            out_specs=[blk, pl.BlockSpec((1, tr, C), lambda i, s, p: (p[1], i, 0))]),
        out_shape=[jax.ShapeDtypeStruct((N_CHIPS, half, C), BF16)] * 2, name=name,
        compiler_params=_cp(("parallel", "arbitrary")))(place, g, recv)


def _sum_chips(parts, place, *, name="sum_chips", tr=480):
    _, R, C = parts.shape
    nblk = R // tr

    def body(pl_ref, p_ref, o_ref):
        acc = p_ref[0].astype(F32) + p_ref[1].astype(F32)
        o_ref[...] = (acc + p_ref[2].astype(F32)) + p_ref[3].astype(F32)

    return pl.pallas_call(
        body,
        grid_spec=pltpu.PrefetchScalarGridSpec(
            num_scalar_prefetch=1, grid=(nblk,), in_specs=[pl.BlockSpec((N_CHIPS, tr, C), lambda i, p: (0, i, 0))],
            out_specs=pl.BlockSpec((tr, C), lambda i, p: (p[0] * nblk + i, 0))),
        out_shape=jax.ShapeDtypeStruct((2 * R, C), F32), name=name, compiler_params=_cp(("parallel",)))(place, parts)


def _adamw(w, g, m, v, *, name, tr=128):
    R, C = w.shape
    bc1 = 1.0 - ADAM_B1 ** ADAM_STEP
    bc2 = 1.0 - ADAM_B2 ** ADAM_STEP

    def body(w_ref, g_ref, m_ref, v_ref, d_ref, nm_ref, nv_ref):
        gv = g_ref[...]
        nm = ADAM_B1 * m_ref[...] + (1.0 - ADAM_B1) * gv
        nv = ADAM_B2 * v_ref[...] + (1.0 - ADAM_B2) * (gv * gv)
        d_ref[...] = -ADAM_LR * ((nm / bc1) / (jnp.sqrt(nv / bc2) + ADAM_EPS) + ADAM_WD * w_ref[...])
        nm_ref[...] = nm
        nv_ref[...] = nv

    blk = pl.BlockSpec((tr, C), lambda i: (i, 0))
    return pl.pallas_call(
        body, grid=(R // tr,), in_specs=[blk] * 4, out_specs=[blk] * 3,
        out_shape=[jax.ShapeDtypeStruct((R, C), F32)] * 3, name=name, compiler_params=_cp(("parallel",)))(w, g, m, v)


ROW = 1024
BIG = ("w_mem_kv", "w_out", "w_up", "w_down", "attn_w_in", "lru_w_in")
SMALL_SHARDED = ("lru_conv_w", "lru_conv_b", "lru_ba", "lru_bx", "lru_lambda")
REPLICATED = ("mix_norm", "mlp_norm", "mem_norm", "final_norm", "attn_sinks", "lru_wa", "lru_wx")
SMALL = REPLICATED + SMALL_SHARDED
WEIGHTS = ("mix_norm", "mlp_norm", "mem_norm", "final_norm", "w_mem_kv", "w_out", "w_up", "w_down", "attn_w_in",
           "attn_sinks", "lru_w_in", "lru_conv_w", "lru_conv_b", "lru_wa", "lru_ba", "lru_wx", "lru_bx", "lru_lambda")
SMALL_W_ROWS = 32
SMALL_G_ROWS = 192
ADAM_SMALL_ROWS = 640


def _rows(a):
    return a.reshape(-1, ROW)


def _flat_pad(parts, total):
    flat = jnp.concatenate([p.reshape(-1) for p in parts])
    return jnp.pad(flat, (0, total - flat.shape[0]))


def _pad_rows(a):
    flat = a.reshape(-1)
    n = -(-flat.shape[0] // ROW) * ROW
    return jnp.pad(flat, (0, n - flat.shape[0])).reshape(-1, ROW)


def _gather_weights(P, chip1):
    big = [_rows(P[n].astype(BF16)) for n in BIG]
    small = _flat_pad([P[n] for n in SMALL_SHARDED], SMALL_W_ROWS * ROW // 2)
    small_bits = lax.bitcast_convert_type(small, BF16).reshape(SMALL_W_ROWS, ROW)
    packed = jnp.concatenate(big + [small_bits], axis=0)
    placed = _place_slot(packed, chip1, N_CHIPS, name="place_weights", tr=packed.shape[0] // 5)
    full = _allgather_chips(placed, name="allgather_weights", forward_to_sibling=True)
    W = {n: P[n] for n in REPLICATED}
    off = 0
    per_layer = {}
    for n in BIG:
        r = big[BIG.index(n)].shape[0]
        per_layer[n] = full[:, off:off + r]
        off += r
    kv = per_layer["w_mem_kv"].reshape(N_CHIPS, DEPTH, -1, D_MODEL)
    W["w_mem_kv"] = [kv[:, l].reshape(1, D_MODEL, D_MODEL) for l in range(DEPTH)]
    wo = per_layer["w_out"].reshape(N_CHIPS, DEPTH, -1, D_MODEL)
    W["w_out"] = [wo[:, l].reshape(1, MIX_OUT_W, D_MODEL) for l in range(DEPTH)]
    wu = per_layer["w_up"].reshape(N_CHIPS, DEPTH, D_MODEL, D_FF // N_CHIPS)
    W["w_up"] = [wu[:, l] for l in range(DEPTH)]
    wd = per_layer["w_down"].reshape(N_CHIPS, DEPTH, D_FF // N_CHIPS, D_MODEL)
    W["w_down"] = [wd[:, l].reshape(1, D_FF, D_MODEL) for l in range(DEPTH)]
    W["attn_w_in"] = per_layer["attn_w_in"].reshape(N_CHIPS, D_MODEL, ATTN_IN_W // N_CHIPS)
    W["lru_w_in"] = per_layer["lru_w_in"].reshape(N_CHIPS, D_MODEL, LRU_IN_W // N_CHIPS)
    sm = lax.bitcast_convert_type(full[:, off:off + SMALL_W_ROWS].reshape(N_CHIPS, -1, 2), F32)
    o = 0
    for n in SMALL_SHARDED:
        shp = P[n].shape[1:]
        cnt = math.prod(shp)
        piece = sm[:, o:o + cnt].reshape((N_CHIPS,) + shp)
        piece = jnp.moveaxis(piece, 0, -2)
        W[n] = piece.reshape(shp[:-1] + (N_CHIPS * shp[-1],)).reshape(-1, D_MODEL)
        o += cnt
    W["lru_wa"] = P["lru_wa"][0].astype(BF16)
    W["lru_wx"] = P["lru_wx"][0].astype(BF16)
    return W


def _small_grad_list(G):
    return [G["mix_norm"], G["mlp_norm"], G["mem_norm"], G["final_norm"], jnp.pad(G["attn_sinks"].reshape(-1), (0, ROW - ATTN_HEADS)),
            G["lru_wa"], G["lru_wx"], G["lru_conv_w"], G["lru_conv_b"], G["lru_ba"], G["lru_bx"], G["lru_lambda"]]


SMALL_G_SIZES = (2 * D_MODEL, 2 * D_MODEL, D_MODEL, D_MODEL, ROW, 2 * 8 * 128 * 128, 2 * 8 * 128 * 128,
                 4 * D_MODEL, D_MODEL, 2 * D_MODEL, 2 * D_MODEL, 2 * D_MODEL)


def _reduce_grads(G, place, chip1):
    c4 = lambda a, r: a.reshape(N_CHIPS, r, ROW)
    pieces = [
        c4(G["w_mem_kv"][0], 256), c4(G["w_mem_kv"][1], 256), c4(G["w_out"][0], 384), c4(G["w_out"][1], 384),
        G["w_up"][0], G["w_up"][1], c4(G["w_down"][0], 1024), c4(G["w_down"][1], 1024),
        c4(G["attn_w_in"], 512), c4(G["lru_w_in"], 640),
        _flat_pad(_small_grad_list(G), N_CHIPS * SMALL_G_ROWS * ROW).reshape(N_CHIPS, SMALL_G_ROWS, ROW),
    ]
    packed = jnp.concatenate(pieces, axis=1)
    recv = _sibling_exchange(packed, name="grad_sibling_exchange")
    halves, landing = _sum_halves(packed, recv, place)
    parts = _chip_exchange(halves, landing, name="grad_chip_exchange")
    full = _sibling_allgather(_sum_chips(parts, place), name="grad_sibling_allgather")
    n_big = full.shape[0] - SMALL_G_ROWS
    small_placed = _place_slot(full[n_big:], chip1, N_CHIPS, name="place_small_grads", tr=SMALL_G_ROWS)
    small_all = _allgather_chips(small_placed, name="allgather_small_grads", forward_to_sibling=False)
    flat = small_all.reshape(-1)
    small = {}
    o = 0
    names = ("mix_norm", "mlp_norm", "mem_norm", "final_norm", "attn_sinks", "lru_wa", "lru_wx",
             "lru_conv_w", "lru_conv_b", "lru_ba", "lru_bx", "lru_lambda")
    for n, cnt in zip(names, SMALL_G_SIZES):
        small[n] = flat[o:o + cnt]
        o += cnt
    return full[:n_big], small


def kernel(x, mem, positions, mix_norm, mlp_norm, mem_norm, final_norm, w_mem_kv, w_out, w_up, w_down, attn_w_in, attn_sinks, lru_w_in, lru_conv_w, lru_conv_b, lru_wa, lru_ba, lru_wx, lru_bx, lru_lambda, loss_target, m_mix_norm, m_mlp_norm, m_mem_norm, m_final_norm, m_w_mem_kv, m_w_out, m_w_up, m_w_down, m_attn_w_in, m_attn_sinks, m_lru_w_in, m_lru_conv_w, m_lru_conv_b, m_lru_wa, m_lru_ba, m_lru_wx, m_lru_bx, m_lru_lambda, v_mix_norm, v_mlp_norm, v_mem_norm, v_final_norm, v_w_mem_kv, v_w_out, v_w_up, v_w_down, v_attn_w_in, v_attn_sinks, v_lru_w_in, v_lru_conv_w, v_lru_conv_b, v_lru_wa, v_lru_ba, v_lru_wx, v_lru_bx, v_lru_lambda):
    P = dict(mix_norm=mix_norm, mlp_norm=mlp_norm, mem_norm=mem_norm, final_norm=final_norm, w_mem_kv=w_mem_kv, w_out=w_out,
             w_up=w_up, w_down=w_down, attn_w_in=attn_w_in, attn_sinks=attn_sinks, lru_w_in=lru_w_in, lru_conv_w=lru_conv_w,
             lru_conv_b=lru_conv_b, lru_wa=lru_wa, lru_ba=lru_ba, lru_wx=lru_wx, lru_bx=lru_bx, lru_lambda=lru_lambda)
    M1 = dict(mix_norm=m_mix_norm, mlp_norm=m_mlp_norm, mem_norm=m_mem_norm, final_norm=m_final_norm, w_mem_kv=m_w_mem_kv,
              w_out=m_w_out, w_up=m_w_up, w_down=m_w_down, attn_w_in=m_attn_w_in, attn_sinks=m_attn_sinks, lru_w_in=m_lru_w_in,
              lru_conv_w=m_lru_conv_w, lru_conv_b=m_lru_conv_b, lru_wa=m_lru_wa, lru_ba=m_lru_ba, lru_wx=m_lru_wx,
              lru_bx=m_lru_bx, lru_lambda=m_lru_lambda)
    V2 = dict(mix_norm=v_mix_norm, mlp_norm=v_mlp_norm, mem_norm=v_mem_norm, final_norm=v_final_norm, w_mem_kv=v_w_mem_kv,
              w_out=v_w_out, w_up=v_w_up, w_down=v_w_down, attn_w_in=v_attn_w_in, attn_sinks=v_attn_sinks, lru_w_in=v_lru_w_in,
              lru_conv_w=v_lru_conv_w, lru_conv_b=v_lru_conv_b, lru_wa=v_lru_wa, lru_ba=v_lru_ba, lru_wx=v_lru_wx,
              lru_bx=v_lru_bx, lru_lambda=v_lru_lambda)
    chip = 2 * lax.axis_index("x") + lax.axis_index("y")
    chip1 = chip.astype(jnp.int32).reshape(1)
    place = jnp.stack([lax.axis_index("c").astype(jnp.int32), chip.astype(jnp.int32)])

    W = _gather_weights(P, chip1)
    loss, dx, G = _local_step(x[0], mem[0], positions[0], loss_target[0], W)
    loss = lax.psum(loss, ("x", "y", "c"))
    big_rows, small = _reduce_grads(G, place, chip1)

    grads, deltas, new_m, new_v = {}, {}, {}, {}
    off = 0
    for n in BIG:
        r = math.prod(P[n].shape) // ROW
        g = big_rows[off:off + r]
        off += r
        d, nm, nv = _adamw(_rows(P[n]), g, _rows(M1[n]), _rows(V2[n]), name=f"adamw_{n}")
        grads[n], deltas[n], new_m[n], new_v[n] = (t.reshape(P[n].shape) for t in (g, d, nm, nv))

    for n in SMALL:
        g = small[n]
        if n in SMALL_SHARDED:
            shard = P[n].shape[-1]
            g = lax.dynamic_slice_in_dim(g.reshape(-1, N_CHIPS * shard), chip * shard, shard, axis=1)
        elif n == "attn_sinks":
            g = g[:ATTN_HEADS]
        grads[n] = g.reshape(P[n].shape)
    packs = []
    for src in (P, grads, M1, V2):
        a = jnp.concatenate([_pad_rows(src[n]) for n in SMALL], axis=0)
        packs.append(jnp.pad(a, ((0, ADAM_SMALL_ROWS - a.shape[0]), (0, 0))))
    d_s, nm_s, nv_s = _adamw(*packs, name="adamw_small")
    o = 0
    for n in SMALL:
        cnt = math.prod(P[n].shape)
        r = -(-cnt // ROW)
        for dst, src in ((deltas, d_s), (new_m, nm_s), (new_v, nv_s)):
            dst[n] = src[o:o + r].reshape(-1)[:cnt].reshape(P[n].shape)
        o += r

    return (loss, dx[None], *[grads[n] for n in WEIGHTS], *[deltas[n] for n in WEIGHTS],
            *[new_m[n] for n in WEIGHTS], *[new_v[n] for n in WEIGHTS])
```

```python
import functools
import math

import jax
import jax.numpy as jnp
from jax import lax
from jax.experimental import pallas as pl
from jax.experimental.pallas import tpu as pltpu

F32 = jnp.float32
BF16 = jnp.bfloat16
MESH = pl.DeviceIdType.MESH

D_MODEL = 1024
DEPTH = 2
EPS = 1e-6
ATTN_HEADS = 16
ATTN_KV_HEADS = 4
HEAD_DIM = 64
WINDOW = 128
BLOCK = 128
ROPE_THETA = 500000.0
ROPE_DIM = 16
Q_W = 1024
KV_W = 256
MEM_LEN = 256
MEM_HEADS = 4
MEM_HEAD_DIM = 128
MEM_W = 512
LRU_BLOCKS = 8
LRU_C = 8.0
ATTN_IN_W = 2048
LRU_IN_W = 2560
MIX_OUT_W = 1536
D_FF = 4096
NEG = -1e30
N_CHIPS = 4

ADAM_LR = 0.001
ADAM_B1 = 0.9
ADAM_B2 = 0.999
ADAM_EPS = 1e-08
ADAM_WD = 0.01
ADAM_STEP = 10

LANES = 128
SCAN_ROWS = 512
VMEM_LIMIT = 56 * 1024 * 1024

NT = (((1,), (1,)), ((), ()))
TN = (((0,), (0,)), ((), ()))


def _cp(sem=None):
    return pltpu.CompilerParams(dimension_semantics=sem, vmem_limit_bytes=VMEM_LIMIT)


def _mm_nn(a, w3, *, name, out_dtype=F32, norm_g=None, resid=None, relu2=False, tm=512):
    M, K = a.shape
    ns, _, n = w3.shape
    N = ns * n
    tm = min(tm, M)
    has_norm = norm_g is not None
    has_res = resid is not None

    def body(*refs):
        refs = list(refs)
        a_ref, w_ref = refs[0], refs[1]
        pos = 2
        if has_norm:
            g_ref = refs[pos]
            pos += 1
        if has_res:
            r_ref = refs[pos]
            pos += 1
        outs = refs[pos:]
        if has_norm:
            xv = a_ref[...]
            rs = lax.rsqrt(jnp.mean(xv * xv, axis=-1, keepdims=True) + EPS)
            ab = (xv * rs * g_ref[...]).astype(BF16)
            outs[-1][...] = ab
        else:
            ab = a_ref[...]
        for s in range(ns):
            acc = jnp.dot(ab, w_ref[s], preferred_element_type=F32)
            sl = slice(s * n, (s + 1) * n)
            if relu2:
                outs[0][:, sl] = acc.astype(BF16)
                rl = jnp.maximum(acc, 0.0)
                outs[1][:, sl] = (rl * rl).astype(BF16)
            elif has_res:
                outs[0][:, sl] = r_ref[:, sl] + acc
            else:
                outs[0][:, sl] = acc.astype(out_dtype)

    row = lambda w: pl.BlockSpec((tm, w), lambda i: (i, 0))
    in_specs = [row(K), pl.BlockSpec((ns, K, n), lambda i: (0, 0, 0))]
    args = [a, w3]
    if has_norm:
        in_specs.append(pl.BlockSpec((1, K), lambda i: (0, 0)))
        args.append(norm_g.reshape(1, K))
    if has_res:
        in_specs.append(row(N))
        args.append(resid)
    if relu2:
        out_shape = [jax.ShapeDtypeStruct((M, N), BF16), jax.ShapeDtypeStruct((M, N), BF16)]
        out_specs = [row(N), row(N)]
    else:
        out_shape = [jax.ShapeDtypeStruct((M, N), F32 if has_res else out_dtype)]
        out_specs = [row(N)]
    if has_norm:
        out_shape.append(jax.ShapeDtypeStruct((M, K), BF16))
        out_specs.append(row(K))
    res = pl.pallas_call(body, grid=(M // tm,), in_specs=in_specs, out_specs=out_specs, out_shape=out_shape,
                         name=name, compiler_params=_cp(("parallel",)))(*args)
    return res if len(res) > 1 else res[0]


def _mm_nt(g, w3, *, name, out_dtype=BF16, up=None, norm_x=None, norm_g=None, dres=None, tm=512):
    M = g.shape[0]
    ns, K, n = w3.shape
    tm = min(tm, M)
    has_up = up is not None
    has_norm = norm_x is not None
    has_res = dres is not None

    def body(*refs):
        refs = list(refs)
        g_ref, w_ref = refs[0], refs[1]
        pos = 2
        if has_up:
            up_ref = refs[pos]
            pos += 1
        if has_norm:
            x_ref, gn_ref = refs[pos], refs[pos + 1]
            pos += 2
        if has_res:
            r_ref = refs[pos]
            pos += 1
        outs = refs[pos:]
        acc = None
        for s in range(ns):
            part = lax.dot_general(g_ref[:, s * n:(s + 1) * n], w_ref[s], NT, preferred_element_type=F32)
            acc = part if acc is None else acc + part
        if has_up:
            outs[0][...] = (acc * (2.0 * jnp.maximum(up_ref[...].astype(F32), 0.0))).astype(BF16)
        elif has_norm:
            xv = x_ref[...]
            rs = lax.rsqrt(jnp.mean(xv * xv, axis=-1, keepdims=True) + EPS)
            xn = xv * rs
            dxn = acc * gn_ref[...]
            dx = rs * (dxn - xn * jnp.mean(dxn * xn, axis=-1, keepdims=True))
            if has_res:
                dx = dx + r_ref[...]
            outs[0][...] = dx
            outs[1][...] = dx.astype(BF16)

            @pl.when(pl.program_id(0) == 0)
            def _():
                outs[2][...] = jnp.zeros_like(outs[2])

            outs[2][...] += jnp.sum(acc * xn, axis=0, keepdims=True)
        else:
            outs[0][...] = acc.astype(out_dtype)

    row = lambda w: pl.BlockSpec((tm, w), lambda i: (i, 0))
    in_specs = [row(ns * n), pl.BlockSpec((ns, K, n), lambda i: (0, 0, 0))]
    args = [g, w3]
    if has_up:
        in_specs.append(row(K))
        args.append(up)
    if has_norm:
        in_specs += [row(K), pl.BlockSpec((1, K), lambda i: (0, 0))]
        args += [norm_x, norm_g.reshape(1, K)]
    if has_res:
        in_specs.append(row(K))
        args.append(dres)
    if has_norm:
        out_shape = [jax.ShapeDtypeStruct((M, K), F32), jax.ShapeDtypeStruct((M, K), BF16),
                     jax.ShapeDtypeStruct((1, K), F32)]
        out_specs = [row(K), row(K), pl.BlockSpec((1, K), lambda i: (0, 0))]
        sem = ("arbitrary",)
    else:
        out_shape = [jax.ShapeDtypeStruct((M, K), BF16 if has_up else out_dtype)]
        out_specs = [row(K)]
        sem = ("parallel",)
    res = pl.pallas_call(body, grid=(M // tm,), in_specs=in_specs, out_specs=out_specs, out_shape=out_shape,
                         name=name, compiler_params=_cp(sem))(*args)
    return res if len(res) > 1 else res[0]


def _mm_tn(a, g, ns, *, name, tk=512, tm=1024):
    M, K = a.shape
    n = g.shape[1] // ns
    tm = min(tm, M)
    tk = min(tk, K)

    def body(a_ref, g_ref, o_ref):
        @pl.when(pl.program_id(2) == 0)
        def _():
            o_ref[...] = jnp.zeros_like(o_ref)

        o_ref[0] += lax.dot_general(a_ref[...], g_ref[...], TN, preferred_element_type=F32)

    return pl.pallas_call(
        body, grid=(ns, K // tk, M // tm),
        in_specs=[pl.BlockSpec((tm, tk), lambda s, k, m: (m, k)), pl.BlockSpec((tm, n), lambda s, k, m: (m, s))],
        out_specs=pl.BlockSpec((1, tk, n), lambda s, k, m: (s, k, 0)),
        out_shape=jax.ShapeDtypeStruct((ns, K, n), F32), name=name,
        compiler_params=_cp(("parallel", "parallel", "arbitrary")))(a, g)


def _final(x, gain, target, *, name="final_loss", tr=256):
    S, Dm = x.shape
    tr = min(tr, S)

    def body(x_ref, g_ref, t_ref, loss_ref, dx_ref, dxb_ref, dg_ref):
        @pl.when(pl.program_id(0) == 0)
        def _():
            loss_ref[...] = jnp.zeros_like(loss_ref)
            dg_ref[...] = jnp.zeros_like(dg_ref)

        xv = x_ref[...]
        gv = g_ref[...]
        rs = lax.rsqrt(jnp.mean(xv * xv, axis=-1, keepdims=True) + EPS)
        xn = xv * rs
        err = xn * gv - t_ref[...]
        loss_ref[...] += 0.5 * jnp.sum(jnp.mean(err * err, axis=-1, keepdims=True), axis=0, keepdims=True)
        dout = err * (1.0 / Dm)
        dg_ref[...] += jnp.sum(dout * xn, axis=0, keepdims=True)
        dxn = dout * gv
        dx = rs * (dxn - xn * jnp.mean(dxn * xn, axis=-1, keepdims=True))
        dx_ref[...] = dx
        dxb_ref[...] = dx.astype(BF16)

    row = pl.BlockSpec((tr, Dm), lambda i: (i, 0))
    return pl.pallas_call(
        body, grid=(S // tr,),
        in_specs=[row, pl.BlockSpec((1, Dm), lambda i: (0, 0)), row],
        out_specs=[pl.BlockSpec((1, 1), lambda i: (0, 0)), row, row, pl.BlockSpec((1, Dm), lambda i: (0, 0))],
        out_shape=[jax.ShapeDtypeStruct((1, 1), F32), jax.ShapeDtypeStruct((S, Dm), F32),
                   jax.ShapeDtypeStruct((S, Dm), BF16), jax.ShapeDtypeStruct((1, Dm), F32)],
        name=name, compiler_params=_cp(("arbitrary",)))(x, gain.reshape(1, Dm), target)


def _rope_tables(positions):
    half = ROPE_DIM // 2
    inv_freq = ROPE_THETA ** (-2.0 * jnp.arange(half, dtype=F32) / ROPE_DIM)
    ang = positions.astype(F32)[:, None] * inv_freq
    cos, sin = jnp.cos(ang), jnp.sin(ang)
    S = positions.shape[0]
    ones = jnp.ones((S, HEAD_DIM - ROPE_DIM), F32)
    cos64 = jnp.concatenate([cos, cos, ones], axis=1)
    sin64 = jnp.concatenate([-sin, sin, 0.0 * ones], axis=1)
    return jnp.tile(cos64, (1, 2)), jnp.tile(sin64, (1, 2))


def _rope_partner(t):
    lane = lax.broadcasted_iota(jnp.int32, t.shape, 1)
    low = (lane & (HEAD_DIM - 1)) < (ROPE_DIM // 2)
    return jnp.where(low, pltpu.roll(t, LANES - ROPE_DIM // 2, 1), pltpu.roll(t, ROPE_DIM // 2, 1))


def _qk_prep(p, cos_t, sin_t, *, name="qk_prep", tr=256):
    S = p.shape[0]
    tr = min(tr, S)
    scale = HEAD_DIM ** -0.5

    def body(p_ref, c_ref, s_ref, q_ref, k_ref, v_ref, va_ref):
        cs, sn = c_ref[...], s_ref[...]
        lane = lax.broadcasted_iota(jnp.int32, (tr, LANES), 1)
        lo = lane < HEAD_DIM
        for c in range(Q_W // LANES):
            t = p_ref[:, c * LANES:(c + 1) * LANES]
            q_ref[:, c * LANES:(c + 1) * LANES] = ((t * cs + _rope_partner(t) * sn) * scale).astype(BF16)
        for c in range(KV_W // LANES):
            t = p_ref[:, Q_W + c * LANES:Q_W + (c + 1) * LANES]
            kc = t * cs + _rope_partner(t) * sn
            vc = p_ref[:, Q_W + KV_W + c * LANES:Q_W + KV_W + (c + 1) * LANES]
            for arr, ref in ((kc, k_ref), (vc, v_ref)):
                sw = pltpu.roll(arr, HEAD_DIM, 1)
                ref[:, (2 * c) * LANES:(2 * c + 1) * LANES] = jnp.where(lo, arr, sw).astype(BF16)
                ref[:, (2 * c + 1) * LANES:(2 * c + 2) * LANES] = jnp.where(lo, sw, arr).astype(BF16)
            sw = pltpu.roll(vc, HEAD_DIM, 1)
            for k, aug in enumerate((jnp.where(lo, vc, 1.0), jnp.where(lo, 1.0, sw), jnp.where(lo, sw, 1.0), jnp.where(lo, 1.0, vc))):
                va_ref[:, (4 * c + k) * LANES:(4 * c + k + 1) * LANES] = aug.astype(BF16)

    row = lambda w: pl.BlockSpec((tr, w), lambda i: (i, 0))
    return pl.pallas_call(
        body, grid=(S // tr,), in_specs=[row(ATTN_IN_W), row(LANES), row(LANES)],
        out_specs=[row(Q_W), row(2 * KV_W), row(2 * KV_W), row(4 * KV_W)],
        out_shape=[jax.ShapeDtypeStruct((S, Q_W), BF16), jax.ShapeDtypeStruct((S, 2 * KV_W), BF16),
                   jax.ShapeDtypeStruct((S, 2 * KV_W), BF16), jax.ShapeDtypeStruct((S, 4 * KV_W), BF16)],
        name=name, compiler_params=_cp(("parallel",)))(p, cos_t, sin_t)


def _qk_prep_bwd(dq, dk, dv, dmq, cos_t, sin_t, *, name="qk_prep_bwd", tr=256):
    S = dq.shape[0]
    tr = min(tr, S)

    def body(dq_ref, dk_ref, dv_ref, dmq_ref, c_ref, s_ref, o_ref):
        cs, sn = c_ref[...], s_ref[...]
        for c in range(Q_W // LANES):
            t = dq_ref[:, c * LANES:(c + 1) * LANES]
            o_ref[:, c * LANES:(c + 1) * LANES] = (t * cs - _rope_partner(t) * sn).astype(BF16)
        for c in range(KV_W // LANES):
            t = dk_ref[:, c * LANES:(c + 1) * LANES]
            o_ref[:, Q_W + c * LANES:Q_W + (c + 1) * LANES] = (t * cs - _rope_partner(t) * sn).astype(BF16)
        o_ref[:, Q_W + KV_W:Q_W + 2 * KV_W] = dv_ref[...].astype(BF16)
        o_ref[:, Q_W + 2 * KV_W:] = dmq_ref[...]

    row = lambda w: pl.BlockSpec((tr, w), lambda i: (i, 0))
    return pl.pallas_call(
        body, grid=(S // tr,), in_specs=[row(Q_W), row(KV_W), row(KV_W), row(MEM_W), row(LANES), row(LANES)],
        out_specs=row(ATTN_IN_W), out_shape=jax.ShapeDtypeStruct((S, ATTN_IN_W), BF16),
        name=name, compiler_params=_cp(("parallel",)))(dq, dk, dv, dmq, cos_t, sin_t)


def _band(n, S):
    start = pl.multiple_of(jnp.clip((n - 1) * BLOCK, 0, S - 3 * BLOCK), BLOCK)
    qi = lax.broadcasted_iota(jnp.int32, (BLOCK, 3 * BLOCK), 0) + n * BLOCK
    ki = lax.broadcasted_iota(jnp.int32, (BLOCK, 3 * BLOCK), 1) + start
    return start, jnp.abs(ki - qi) <= WINDOW


def _head_operand(ref, h, lo):
    c = h // 2
    t = ref[:, c * LANES:(c + 1) * LANES].astype(F32)
    return jnp.where(lo if h % 2 == 0 else jnp.logical_not(lo), t, 0.0).astype(BF16)


GROUP = ATTN_HEADS // ATTN_KV_HEADS
EVENS_FIRST = (0, 2, 1, 3)


def _attn_fwd(q, kd, va, sinks, *, name="attn_fwd"):
    S = q.shape[0]

    def body(sink_ref, q_ref, k_ref, va_ref, o_ref, lse_ref, p_scr):
        n = pl.program_id(0)
        start, mask = _band(n, S)
        lane = lax.broadcasted_iota(jnp.int32, (BLOCK, LANES), 1)
        lo = lane < HEAD_DIM
        rows = pl.ds(start, 3 * BLOCK)
        scores = []
        for g in range(ATTN_KV_HEADS):
            qst = jnp.concatenate([_head_operand(q_ref, GROUP * g + j, lo) for j in EVENS_FIRST], axis=0)
            scores.append(lax.dot_general(qst, k_ref[rows, g * LANES:(g + 1) * LANES], NT, preferred_element_type=F32))
        ms = {}
        for g in range(ATTN_KV_HEADS):
            for pos, j in enumerate(EVENS_FIRST):
                h = GROUP * g + j
                s = jnp.where(mask, scores[g][pos * BLOCK:(pos + 1) * BLOCK], NEG)
                ms[h] = jnp.maximum(jnp.max(s, axis=-1, keepdims=True), sink_ref[h])
                p_scr[(GROUP * g + pos) * BLOCK:(GROUP * g + pos + 1) * BLOCK, :] = jnp.exp(s - ms[h]).astype(BF16)
        pvs = {}
        for g in range(ATTN_KV_HEADS):
            for par in range(2):
                r0 = (GROUP * g + 2 * par) * BLOCK
                pvs[g, par] = jnp.dot(p_scr[r0:r0 + 2 * BLOCK, :], va_ref[rows, (2 * g + par) * LANES:(2 * g + par + 1) * LANES],
                                      preferred_element_type=F32)
        lse_blk = jnp.zeros((BLOCK, LANES), F32)
        for g in range(ATTN_KV_HEADS):
            outs = {}
            for par in range(2):
                for k in range(2):
                    j = EVENS_FIRST[2 * par + k]
                    h = GROUP * g + j
                    pv = pvs[g, par][k * BLOCK:(k + 1) * BLOCK]
                    den = pltpu.roll(pv, HEAD_DIM, 1) + jnp.exp(sink_ref[h] - ms[h])
                    outs[j] = pv * (1.0 / den)
                    l = den[:, par * HEAD_DIM:par * HEAD_DIM + 1]
                    lse_blk = jnp.where(lane == h, ms[h] + jnp.log(l), lse_blk)
            for jj in range(2):
                o_ref[:, (2 * g + jj) * LANES:(2 * g + jj + 1) * LANES] = jnp.where(lo, outs[2 * jj], outs[2 * jj + 1]).astype(BF16)
        lse_ref[...] = lse_blk

    full = lambda w: pl.BlockSpec((S, w), lambda i: (0, 0))
    return pl.pallas_call(
        body, grid=(S // BLOCK,),
        in_specs=[pl.BlockSpec(memory_space=pltpu.SMEM), pl.BlockSpec((BLOCK, Q_W), lambda i: (i, 0)),
                  full(2 * KV_W), full(4 * KV_W)],
        out_specs=[pl.BlockSpec((BLOCK, Q_W), lambda i: (i, 0)), pl.BlockSpec((BLOCK, LANES), lambda i: (i, 0))],
        out_shape=[jax.ShapeDtypeStruct((S, Q_W), BF16), jax.ShapeDtypeStruct((S, LANES), F32)],
        scratch_shapes=[pltpu.VMEM((ATTN_HEADS * BLOCK, 3 * BLOCK), BF16)],
        name=name, compiler_params=_cp(("parallel",)))(sinks, q, kd, va)


def _attn_bwd(q, kd, vd, ao, lse, sinks, dcat, *, name="attn_bwd"):
    S = q.shape[0]
    scale = HEAD_DIM ** -0.5

    def body(sink_ref, q_ref, k_ref, v_ref, ao_ref, lse_ref, do_ref, dq_ref, dk_ref, dv_ref, ds_ref, p_scr, dsb_scr):
        n = pl.program_id(0)

        @pl.when(n == 0)
        def _():
            dk_ref[...] = jnp.zeros_like(dk_ref)
            dv_ref[...] = jnp.zeros_like(dv_ref)
            ds_ref[...] = jnp.zeros_like(ds_ref)

        start, mask = _band(n, S)
        lane = lax.broadcasted_iota(jnp.int32, (BLOCK, LANES), 1)
        lo = lane < HEAD_DIM
        lane3 = lax.broadcasted_iota(jnp.int32, (3 * BLOCK, LANES), 1)
        row8 = lax.broadcasted_iota(jnp.int32, (8, LANES), 0)
        lane8 = lax.broadcasted_iota(jnp.int32, (8, LANES), 1)
        dsink = jnp.zeros((8, LANES), F32)
        lse_blk = lse_ref[...]
        rows = pl.ds(start, 3 * BLOCK)
        lses, deltas = {}, {}
        for c in range(Q_W // LANES):
            prod = do_ref[:, c * LANES:(c + 1) * LANES].astype(F32) * ao_ref[:, c * LANES:(c + 1) * LANES].astype(F32)
            for k in range(2):
                h = 2 * c + k
                deltas[h] = jnp.sum(jnp.where(lo if k == 0 else jnp.logical_not(lo), prod, 0.0), axis=1, keepdims=True)
                lses[h] = jnp.sum(jnp.where(lane == h, lse_blk, 0.0), axis=1, keepdims=True)
                val = -jnp.sum(jnp.exp(sink_ref[h] - lses[h]) * deltas[h], axis=0, keepdims=True)
                dsink = dsink + jnp.where((row8 == 0) & (lane8 == h), val, 0.0)
        stack = lambda ref, g: jnp.concatenate([_head_operand(ref, GROUP * g + j, lo) for j in range(GROUP)], axis=0)
        ss, dps = [], []
        for g in range(ATTN_KV_HEADS):
            ss.append(lax.dot_general(stack(q_ref, g), k_ref[rows, g * LANES:(g + 1) * LANES], NT, preferred_element_type=F32))
            dps.append(lax.dot_general(stack(do_ref, g), v_ref[rows, g * LANES:(g + 1) * LANES], NT, preferred_element_type=F32))
        for g in range(ATTN_KV_HEADS):
            for j in range(GROUP):
                h = GROUP * g + j
                r = slice(j * BLOCK, (j + 1) * BLOCK)
                hr = slice(h * BLOCK, (h + 1) * BLOCK)
                p = jnp.exp(jnp.where(mask, ss[g][r], NEG) - lses[h])
                p_scr[hr, :] = p.astype(BF16)
                dsb_scr[hr, :] = (p * (dps[g][r] - deltas[h])).astype(BF16)
        for g in range(ATTN_KV_HEADS):
            cols = slice((g // 2) * LANES, (g // 2 + 1) * LANES)
            gr = slice(GROUP * g * BLOCK, GROUP * (g + 1) * BLOCK)
            dsg = dsb_scr[gr, :]
            dqs = jnp.dot(dsg, k_ref[rows, g * LANES:(g + 1) * LANES], preferred_element_type=F32) * scale
            for jj in range(2):
                dq_ref[:, (2 * g + jj) * LANES:(2 * g + jj + 1) * LANES] = jnp.where(
                    lo, dqs[(2 * jj) * BLOCK:(2 * jj + 1) * BLOCK], dqs[(2 * jj + 1) * BLOCK:(2 * jj + 2) * BLOCK])
            half = (lane3 < HEAD_DIM) if g % 2 == 0 else (lane3 >= HEAD_DIM)
            dkr = lax.dot_general(dsg, stack(q_ref, g), TN, preferred_element_type=F32)
            dk_ref[rows, cols] += jnp.where(half, dkr + pltpu.roll(dkr, HEAD_DIM, 1), 0.0)
            dvr = lax.dot_general(p_scr[gr, :], stack(do_ref, g), TN, preferred_element_type=F32)
            dv_ref[rows, cols] += jnp.where(half, dvr + pltpu.roll(dvr, HEAD_DIM, 1), 0.0)
        ds_ref[...] += dsink

    full = lambda w: pl.BlockSpec((S, w), lambda i: (0, 0))
    blk = lambda w: pl.BlockSpec((BLOCK, w), lambda i: (i, 0))
    return pl.pallas_call(
        body, grid=(S // BLOCK,),
        in_specs=[pl.BlockSpec(memory_space=pltpu.SMEM), blk(Q_W), full(2 * KV_W), full(2 * KV_W), blk(Q_W), blk(LANES), blk(Q_W)],
        out_specs=[blk(Q_W), full(KV_W), full(KV_W), pl.BlockSpec((8, LANES), lambda i: (0, 0))],
        out_shape=[jax.ShapeDtypeStruct((S, Q_W), F32), jax.ShapeDtypeStruct((S, KV_W), F32),
                   jax.ShapeDtypeStruct((S, KV_W), F32), jax.ShapeDtypeStruct((8, LANES), F32)],
        scratch_shapes=[pltpu.VMEM((ATTN_HEADS * BLOCK, 3 * BLOCK), BF16), pltpu.VMEM((ATTN_HEADS * BLOCK, 3 * BLOCK), BF16)],
        name=name, compiler_params=_cp(("arbitrary",)))(sinks, q, kd, vd, ao, lse, dcat)


def _mem_probs(q_ref, kv_ref, h):
    scale = MEM_HEAD_DIM ** -0.5
    qh = q_ref[:, h * LANES:(h + 1) * LANES].astype(BF16)
    s = lax.dot_general(qh, kv_ref[:, h * LANES:(h + 1) * LANES], NT, preferred_element_type=F32) * scale
    m = jnp.max(s, axis=-1, keepdims=True)
    pe = jnp.exp(s - m)
    return qh, pe * (1.0 / jnp.sum(pe, axis=-1, keepdims=True))


def _memattn_fwd(p, qblk, kv, *, name="memattn_fwd", tr=512):
    S = p.shape[0]
    tr = min(tr, S)

    def body(q_ref, kv_ref, o_ref):
        for h in range(MEM_HEADS):
            _, pr = _mem_probs(q_ref, kv_ref, h)
            o = jnp.dot(pr.astype(BF16), kv_ref[:, MEM_W + h * LANES:MEM_W + (h + 1) * LANES], preferred_element_type=F32)
            o_ref[:, h * LANES:(h + 1) * LANES] = o.astype(BF16)

    return pl.pallas_call(
        body, grid=(S // tr,),
        in_specs=[pl.BlockSpec((tr, MEM_W), lambda i: (i, qblk)), pl.BlockSpec((MEM_LEN, 2 * MEM_W), lambda i: (0, 0))],
        out_specs=pl.BlockSpec((tr, MEM_W), lambda i: (i, 0)),
        out_shape=jax.ShapeDtypeStruct((S, MEM_W), BF16), name=name, compiler_params=_cp(("parallel",)))(p, kv)


def _memattn_bwd(p, qblk, kv, dcat, *, name="memattn_bwd", tr=512):
    S = p.shape[0]
    tr = min(tr, S)
    scale = MEM_HEAD_DIM ** -0.5

    def body(q_ref, kv_ref, do_ref, dq_ref, dkv_ref):
        @pl.when(pl.program_id(0) == 0)
        def _():
            dkv_ref[...] = jnp.zeros_like(dkv_ref)

        for h in range(MEM_HEADS):
            qh, pr = _mem_probs(q_ref, kv_ref, h)
            doh = do_ref[:, h * LANES:(h + 1) * LANES]
            dp = lax.dot_general(doh, kv_ref[:, MEM_W + h * LANES:MEM_W + (h + 1) * LANES], NT, preferred_element_type=F32)
            delta = jnp.sum(pr * dp, axis=-1, keepdims=True)
            dsb = (pr * (dp - delta) * scale).astype(BF16)
            dq = jnp.dot(dsb, kv_ref[:, h * LANES:(h + 1) * LANES], preferred_element_type=F32)
            dq_ref[:, h * LANES:(h + 1) * LANES] = dq.astype(BF16)
            dkv_ref[:, h * LANES:(h + 1) * LANES] += lax.dot_general(dsb, qh, TN, preferred_element_type=F32)
            dkv_ref[:, MEM_W + h * LANES:MEM_W + (h + 1) * LANES] += lax.dot_general(
                pr.astype(BF16), doh, TN, preferred_element_type=F32)

    return pl.pallas_call(
        body, grid=(S // tr,),
        in_specs=[pl.BlockSpec((tr, MEM_W), lambda i: (i, qblk)), pl.BlockSpec((MEM_LEN, 2 * MEM_W), lambda i: (0, 0)),
                  pl.BlockSpec((tr, MEM_W), lambda i: (i, Q_W // MEM_W))],
        out_specs=[pl.BlockSpec((tr, MEM_W), lambda i: (i, 0)), pl.BlockSpec((MEM_LEN, 2 * MEM_W), lambda i: (0, 0))],
        out_shape=[jax.ShapeDtypeStruct((S, MEM_W), BF16), jax.ShapeDtypeStruct((MEM_LEN, 2 * MEM_W), F32)],
        name=name, compiler_params=_cp(("arbitrary",)))(p, kv, dcat)


def _sigmoid(z):
    return 1.0 / (1.0 + jnp.exp(-z))


def _one_minus_exp(z, exp_z):
    poly = z * (1.0 + z * (0.5 + z * (1.0 / 6.0 + z * (1.0 / 24.0 + z * (1.0 / 120.0)))))
    return jnp.where(z > -0.1, -poly, 1.0 - exp_z)


def _softplus_neg(lam):
    z = -lam
    return jnp.maximum(z, 0.0) + jnp.log(1.0 + jnp.exp(-jnp.abs(z)))


_GELU_C = math.sqrt(2.0 / math.pi)


def _gelu(z):
    return 0.5 * z * (1.0 + jnp.tanh(_GELU_C * (z + 0.044715 * z * z * z)))


def _row_or_zero(ref, t, S):
    ok = jnp.logical_and(t >= 0, t < S)
    return jnp.where(ok, ref[pl.ds(jnp.clip(t, 0, S - 1), 1), :], 0.0)


def _shift_down(v, first):
    ri = lax.broadcasted_iota(jnp.int32, v.shape, 0)
    return jnp.where(ri == 0, first, pltpu.roll(v, 1, 0))


def _shift_up(v, last):
    T = v.shape[0]
    ri = lax.broadcasted_iota(jnp.int32, v.shape, 0)
    return jnp.where(ri == T - 1, last, pltpu.roll(v, T - 1, 0))


def _scan_chunk(a, u, reverse):
    T = a.shape[0]
    ri = lax.broadcasted_iota(jnp.int32, a.shape, 0)
    d = 1
    while d < T:
        if reverse:
            a_s, u_s, ok = pltpu.roll(a, T - d, 0), pltpu.roll(u, T - d, 0), ri < T - d
        else:
            a_s, u_s, ok = pltpu.roll(a, d, 0), pltpu.roll(u, d, 0), ri >= d
        u = jnp.where(ok, a * u_s + u, u)
        a = jnp.where(ok, a * a_s, a)
        d *= 2
    return a, u


def _conv_taps(xb_ref, t0, S):
    T = SCAN_ROWS
    x0 = xb_ref[pl.ds(t0, T), :]
    xm1 = _shift_down(x0, _row_or_zero(xb_ref, t0 - 1, S))
    nxt0 = _row_or_zero(xb_ref, t0 + T, S)
    xp1 = _shift_up(x0, nxt0)
    xp2 = _shift_up(xp1, _row_or_zero(xb_ref, t0 + T + 1, S))
    return xm1, x0, xp1, xp2


def _lru_gates(xc, w_a, b_a, w_x, b_x, sp):
    xcb = xc.astype(BF16)
    r = _sigmoid(jnp.dot(xcb, w_a, preferred_element_type=F32) + b_a)
    i = _sigmoid(jnp.dot(xcb, w_x, preferred_element_type=F32) + b_x)
    la = -LRU_C * r * sp
    a = jnp.exp(la)
    beta = jnp.sqrt(_one_minus_exp(2.0 * la, a * a))
    return r, i, a, beta


def _lru_specs(S):
    col = lambda off: pl.BlockSpec((S, LANES), lambda n: (0, n + off), pipeline_mode=pl.Buffered(1))
    small = lambda r: pl.BlockSpec((r, LANES), lambda n: (0, n))
    wblk = pl.BlockSpec((2, 1, LANES, LANES), lambda n: (0, n, 0, 0))
    return col, small, wblk


def _lru_fwd(p, conv_w, conv_b, wa, ba, wx, bx, lam, *, name="lru_fwd"):
    S = p.shape[0]
    T = SCAN_ROWS
    nc = S // T

    def body(xb_ref, gate_ref, cw_ref, cb_ref, wa_ref, ba_ref, wx_ref, bx_ref, lam_ref, y_ref, hf_ref, hr_ref, xc_v):
        sp = _softplus_neg(lam_ref[...])
        cw = cw_ref[...]

        def fwd_step(c, h_in):
            t0 = pl.multiple_of(c * T, T)
            xm1, x0, xp1, xp2 = _conv_taps(xb_ref, t0, S)
            xc = cb_ref[...] + xm1 * cw[0:1] + x0 * cw[1:2] + xp1 * cw[2:3] + xp2 * cw[3:4]
            xc_v[pl.ds(t0, T), :] = xc
            _, i, a, beta = _lru_gates(xc, wa_ref[0, 0], ba_ref[0:1], wx_ref[0, 0], bx_ref[0:1], sp[0:1])
            A, U = _scan_chunk(a, beta * (i * xc), False)
            hf_ref[pl.ds(t0, T), :] = A * h_in + U
            return hf_ref[pl.ds(t0 + T - 1, 1), :]

        lax.fori_loop(0, nc, fwd_step, jnp.zeros((1, LANES), F32))

        def rev_step(k, h_in):
            t0 = pl.multiple_of((nc - 1 - k) * T, T)
            xc = xc_v[pl.ds(t0, T), :]
            _, i, a, beta = _lru_gates(xc, wa_ref[1, 0], ba_ref[1:2], wx_ref[1, 0], bx_ref[1:2], sp[1:2])
            A, U = _scan_chunk(a, beta * (i * xc), True)
            h = A * h_in + U
            hr_ref[pl.ds(t0, T), :] = h
            y_ref[pl.ds(t0, T), :] = ((hf_ref[pl.ds(t0, T), :] + h) * _gelu(gate_ref[pl.ds(t0, T), :])).astype(BF16)
            return hr_ref[pl.ds(t0, 1), :]

        lax.fori_loop(0, nc, rev_step, jnp.zeros((1, LANES), F32))

    col, small, wblk = _lru_specs(S)
    colo = lambda: pl.BlockSpec((S, LANES), lambda n: (0, n))
    return pl.pallas_call(
        body, grid=(LRU_BLOCKS,),
        in_specs=[col(0), col(LRU_BLOCKS), small(4), small(1), wblk, small(2), wblk, small(2), small(2)],
        out_specs=[colo(), colo(), colo()],
        out_shape=[jax.ShapeDtypeStruct((S, D_MODEL), BF16), jax.ShapeDtypeStruct((S, D_MODEL), F32),
                   jax.ShapeDtypeStruct((S, D_MODEL), F32)],
        scratch_shapes=[pltpu.VMEM((S, LANES), F32)],
        name=name, compiler_params=_cp(("parallel",)))(p, p, conv_w, conv_b, wa, ba, wx, bx, lam)


def _lru_bwd(p, hf, hr, dcat, conv_w, conv_b, wa, ba, wx, bx, lam, *, name="lru_bwd"):
    S = p.shape[0]
    T = SCAN_ROWS
    nc = S // T

    def body(xb_ref, gate_ref, hf_ref, hr_ref, dy_ref, cw_ref, cb_ref, wa_ref, ba_ref, wx_ref, bx_ref, lam_ref,
             dxb_ref, dgate_ref, dcw_ref, dcb_ref, dwa_ref, dba_ref, dwx_ref, dbx_ref, dlam_ref, xc_v, dxc_v, dh_v):
        lam_v = lam_ref[...]
        sp = _softplus_neg(lam_v)
        cw = cw_ref[...]
        for ref in (dcw_ref, dcb_ref, dwa_ref, dba_ref, dwx_ref, dbx_ref, dlam_ref):
            ref[...] = jnp.zeros_like(ref)

        def prep_step(c, carry):
            t0 = pl.multiple_of(c * T, T)
            rows = pl.ds(t0, T)
            xm1, x0, xp1, xp2 = _conv_taps(xb_ref, t0, S)
            xc_v[rows, :] = cb_ref[...] + xm1 * cw[0:1] + x0 * cw[1:2] + xp1 * cw[2:3] + xp2 * cw[3:4]
            z = gate_ref[rows, :]
            dy = dy_ref[rows, :].astype(F32)
            th = jnp.tanh(_GELU_C * (z + 0.044715 * z * z * z))
            dgelu = 0.5 * (1.0 + th) + 0.5 * z * (1.0 - th * th) * _GELU_C * (1.0 + 3.0 * 0.044715 * z * z)
            dgate_ref[rows, :] = (dy * (hf_ref[rows, :] + hr_ref[rows, :]) * dgelu).astype(BF16)
            dh_v[rows, :] = dy * (0.5 * z * (1.0 + th))
            return carry

        lax.fori_loop(0, nc, prep_step, 0)

        def direction(d):
            h_ref = hf_ref if d == 0 else hr_ref
            w_a, w_x = wa_ref[d, 0], wx_ref[d, 0]
            b_a, b_x, sp_d = ba_ref[d:d + 1], bx_ref[d:d + 1], sp[d:d + 1]

            def step(k, carry):
                g_in, a_in = carry
                c = (nc - 1 - k) if d == 0 else k
                t0 = pl.multiple_of(c * T, T)
                rows = pl.ds(t0, T)
                xc = xc_v[rows, :]
                r, i, a, beta = _lru_gates(xc, w_a, b_a, w_x, b_x, sp_d)
                dh = dh_v[rows, :]
                hc = h_ref[rows, :]
                if d == 0:
                    A, U = _scan_chunk(_shift_up(a, a_in), dh, True)
                    g = A * g_in + U
                    h_nb = _shift_down(hc, _row_or_zero(h_ref, t0 - 1, S))
                    nxt = (g[0:1], a[0:1])
                else:
                    A, U = _scan_chunk(_shift_down(a, a_in), dh, False)
                    g = A * g_in + U
                    h_nb = _shift_up(hc, _row_or_zero(h_ref, t0 + T, S))
                    nxt = (g[T - 1:T], a[T - 1:T])
                da = g * h_nb
                dbeta = g * (i * xc)
                tb = g * beta
                dla = da * a - dbeta * (a * a / beta)
                dzr = (dla * (-LRU_C * sp_d)) * (r * (1.0 - r))
                dzi = (tb * xc) * (i * (1.0 - i))
                dzrb, dzib, xcb = dzr.astype(BF16), dzi.astype(BF16), xc.astype(BF16)
                dwa_ref[d, 0] += lax.dot_general(xcb, dzrb, TN, preferred_element_type=F32)
                dwx_ref[d, 0] += lax.dot_general(xcb, dzib, TN, preferred_element_type=F32)
                dba_ref[d:d + 1] += jnp.sum(dzr, axis=0, keepdims=True)
                dbx_ref[d:d + 1] += jnp.sum(dzi, axis=0, keepdims=True)
                dlam_ref[d:d + 1] += jnp.sum(dla * (-LRU_C * r), axis=0, keepdims=True)
                dxc = (tb * i + lax.dot_general(dzrb, w_a, NT, preferred_element_type=F32)
                       + lax.dot_general(dzib, w_x, NT, preferred_element_type=F32))
                if d == 0:
                    dxc_v[rows, :] = dxc
                else:
                    dxc_v[rows, :] += dxc
                return nxt

            lax.fori_loop(0, nc, step, (jnp.zeros((1, LANES), F32), jnp.zeros((1, LANES), F32)))

        direction(0)
        direction(1)
        dlam_ref[...] = dlam_ref[...] * (-1.0 / (1.0 + jnp.exp(lam_v)))

        def conv_step(c, carry):
            t0 = pl.multiple_of(c * T, T)
            rows = pl.ds(t0, T)
            g0 = dxc_v[rows, :]
            gm1 = _shift_down(g0, _row_or_zero(dxc_v, t0 - 1, S))
            gm2 = _shift_down(gm1, _row_or_zero(dxc_v, t0 - 2, S))
            gp1 = _shift_up(g0, _row_or_zero(dxc_v, t0 + T, S))
            dxb_ref[rows, :] = (cw[0:1] * gp1 + cw[1:2] * g0 + cw[2:3] * gm1 + cw[3:4] * gm2).astype(BF16)
            xm1, x0, xp1, xp2 = _conv_taps(xb_ref, t0, S)
            for tap, xs in enumerate((xm1, x0, xp1, xp2)):
                dcw_ref[tap:tap + 1] += jnp.sum(g0 * xs, axis=0, keepdims=True)
            dcb_ref[...] += jnp.sum(g0, axis=0, keepdims=True)
            return carry

        lax.fori_loop(0, nc, conv_step, 0)

    col, small, wblk = _lru_specs(S)
    colo = lambda: pl.BlockSpec((S, LANES), lambda n: (0, n), pipeline_mode=pl.Buffered(1))
    return pl.pallas_call(
        body, grid=(LRU_BLOCKS,),
        in_specs=[col(0), col(LRU_BLOCKS), col(0), col(0), col(0), small(4), small(1), wblk, small(2), wblk, small(2), small(2)],
        out_specs=[colo(), colo(), small(4), small(1), wblk, small(2), wblk, small(2), small(2)],
        out_shape=[jax.ShapeDtypeStruct((S, D_MODEL), BF16), jax.ShapeDtypeStruct((S, D_MODEL), BF16),
                   jax.ShapeDtypeStruct((4, D_MODEL), F32), jax.ShapeDtypeStruct((1, D_MODEL), F32),
                   jax.ShapeDtypeStruct((2, LRU_BLOCKS, LANES, LANES), F32), jax.ShapeDtypeStruct((2, D_MODEL), F32),
                   jax.ShapeDtypeStruct((2, LRU_BLOCKS, LANES, LANES), F32), jax.ShapeDtypeStruct((2, D_MODEL), F32),
                   jax.ShapeDtypeStruct((2, D_MODEL), F32)],
        scratch_shapes=[pltpu.VMEM((S, LANES), F32), pltpu.VMEM((S, LANES), F32), pltpu.VMEM((S, LANES), F32)],
        name=name, compiler_params=_cp(("parallel",)))(p, p, hf, hr, dcat, conv_w, conv_b, wa, ba, wx, bx, lam)


def _mlp_fwd(x, w_up, w_down, gain, l):
    up, act, h = _mm_nn(x, w_up, norm_g=gain, relu2=True, name=f"mlp_up{l}")
    return _mm_nn(act, w_down, resid=x, name=f"mlp_down{l}"), (up, act, h)


def _mlp_bwd(x, dx, dxb, saved, w_up, w_down, gain, l):
    up, act, h = saved
    g_down = _mm_tn(act, dxb, 1, name=f"dw_down{l}")
    dup = _mm_nt(dxb, w_down, up=up, name=f"d_up{l}")
    g_up = _mm_tn(h, dup, N_CHIPS, name=f"dw_up{l}")
    dx, dxb, g_gain = _mm_nt(dup, w_up, norm_x=x, norm_g=gain, dres=dx, name=f"d_mlp_in{l}")
    return dx, dxb, g_down, g_up, g_gain


def _local_step(x, mem, positions, target, W):
    S = x.shape[0]
    cos_t, sin_t = _rope_tables(positions)
    sinks = W["attn_sinks"].reshape(ATTN_HEADS)
    G = {}

    kv0, memn = _mm_nn(mem, W["w_mem_kv"][0], norm_g=W["mem_norm"], out_dtype=BF16, name="mem_kv0", tm=256)
    kv1 = _mm_nn(memn, W["w_mem_kv"][1], out_dtype=BF16, name="mem_kv1", tm=256)
    p0, h0 = _mm_nn(x, W["attn_w_in"], norm_g=W["mix_norm"][0], name="attn_in")
    q, kd, vd, va = _qk_prep(p0, cos_t, sin_t)
    ao, lse = _attn_fwd(q, kd, va, sinks)
    mo0 = _memattn_fwd(p0, Q_W // MEM_W + 1, kv0, name="memattn_fwd0")
    cat0 = jnp.concatenate([ao, mo0], axis=1)
    x1 = _mm_nn(cat0, W["w_out"][0], resid=x, name="mix_out0")
    x2, mlp0 = _mlp_fwd(x1, W["w_up"][0], W["w_down"][0], W["mlp_norm"][0], 0)
    p1, h2 = _mm_nn(x2, W["lru_w_in"], norm_g=W["mix_norm"][1], name="lru_in")
    lru_w = (W["lru_conv_w"], W["lru_conv_b"], W["lru_wa"], W["lru_ba"], W["lru_wx"], W["lru_bx"], W["lru_lambda"])
    y, hf, hr = _lru_fwd(p1, *lru_w)
    mo1 = _memattn_fwd(p1, 2 * D_MODEL // MEM_W, kv1, name="memattn_fwd1")
    cat1 = jnp.concatenate([y, mo1], axis=1)
    x3 = _mm_nn(cat1, W["w_out"][1], resid=x2, name="mix_out1")
    x4, mlp1 = _mlp_fwd(x3, W["w_up"][1], W["w_down"][1], W["mlp_norm"][1], 1)
    loss, dx, dxb, G["final_norm"] = _final(x4, W["final_norm"], target)

    dx, dxb, gd1, gu1, gm1 = _mlp_bwd(x3, dx, dxb, mlp1, W["w_up"][1], W["w_down"][1], W["mlp_norm"][1], 1)
    go1 = _mm_tn(cat1, dxb, 1, name="dw_out1")
    dcat1 = _mm_nt(dxb, W["w_out"][1], name="d_mix1")
    dmq1, dkv1 = _memattn_bwd(p1, 2 * D_MODEL // MEM_W, kv1, dcat1, name="memattn_bwd1")
    (dxb1, dgate, G["lru_conv_w"], G["lru_conv_b"], G["lru_wa"], G["lru_ba"], G["lru_wx"], G["lru_bx"],
     G["lru_lambda"]) = _lru_bwd(p1, hf, hr, dcat1, *lru_w)
    dp1 = jnp.concatenate([dxb1, dgate, dmq1], axis=1)
    G["lru_w_in"] = _mm_tn(h2, dp1, N_CHIPS, name="dw_lru_in")
    dx, dxb, gx1 = _mm_nt(dp1, W["lru_w_in"], norm_x=x2, norm_g=W["mix_norm"][1], dres=dx, name="d_lru_in")

    dx, dxb, gd0, gu0, gm0 = _mlp_bwd(x1, dx, dxb, mlp0, W["w_up"][0], W["w_down"][0], W["mlp_norm"][0], 0)
    go0 = _mm_tn(cat0, dxb, 1, name="dw_out0")
    dcat0 = _mm_nt(dxb, W["w_out"][0], name="d_mix0")
    dmq0, dkv0 = _memattn_bwd(p0, Q_W // MEM_W + 1, kv0, dcat0, name="memattn_bwd0")
    dq, dk, dv, dsink = _attn_bwd(q, kd, vd, ao, lse, sinks, dcat0)
    dp0 = _qk_prep_bwd(dq, dk, dv, dmq0, cos_t, sin_t)
    G["attn_w_in"] = _mm_tn(h0, dp0, N_CHIPS, name="dw_attn_in")
    dx, _, gx0 = _mm_nt(dp0, W["attn_w_in"], norm_x=x, norm_g=W["mix_norm"][0], dres=dx, name="d_attn_in")

    dkv0b, dkv1b = dkv0.astype(BF16), dkv1.astype(BF16)
    gk0 = _mm_tn(memn, dkv0b, 1, name="dw_kv0", tm=256)
    gk1 = _mm_tn(memn, dkv1b, 1, name="dw_kv1", tm=256)
    w_kv_both = jnp.concatenate([W["w_mem_kv"][0], W["w_mem_kv"][1]], axis=0)
    _, _, G["mem_norm"] = _mm_nt(jnp.concatenate([dkv0b, dkv1b], axis=1), w_kv_both, norm_x=mem, norm_g=W["mem_norm"],
                                 name="d_mem", tm=256)

    G["w_mem_kv"] = (gk0, gk1)
    G["w_out"] = (go0, go1)
    G["w_up"] = (gu0, gu1)
    G["w_down"] = (gd0, gd1)
    G["mix_norm"] = jnp.concatenate([gx0, gx1], axis=0)
    G["mlp_norm"] = jnp.concatenate([gm0, gm1], axis=0)
    G["attn_sinks"] = dsink[0:1, 0:ATTN_HEADS]
    return loss[0, 0], dx, G


HBM = pl.BlockSpec(memory_space=pl.ANY)


def _place():
    x, y, c = lax.axis_index("x"), lax.axis_index("y"), lax.axis_index("c")
    chips = [(1 - x, y), (x, 1 - y), (1 - x, 1 - y)]
    return x, y, c, chips


def _remote(src, dst, send_sems, recv_sems, k, to):
    return pltpu.make_async_remote_copy(src_ref=src, dst_ref=dst, send_sem=send_sems.at[k], recv_sem=recv_sems.at[k],
                                        device_id=to, device_id_type=MESH)


def _comm_call(body, out_shape, n_sems, name, *args, alias=None):
    return pl.pallas_call(
        body, out_shape=out_shape, in_specs=[HBM] * len(args), out_specs=HBM,
        scratch_shapes=[pltpu.SemaphoreType.DMA((n_sems,)), pltpu.SemaphoreType.DMA((n_sems,))],
        input_output_aliases=alias or {}, name=name)(*args)


def _place_slot(shard, slot, n_slots, *, name, tr):
    R, C = shard.shape

    def body(s_ref, a_ref, o_ref):
        o_ref[0] = a_ref[...]

    return pl.pallas_call(
        body,
        grid_spec=pltpu.PrefetchScalarGridSpec(
            num_scalar_prefetch=1, grid=(R // tr,), in_specs=[pl.BlockSpec((tr, C), lambda i, s_ref: (i, 0))],
            out_specs=pl.BlockSpec((1, tr, C), lambda i, s_ref: (s_ref[0], i, 0))),
        out_shape=jax.ShapeDtypeStruct((n_slots, R, C), shard.dtype), name=name,
        compiler_params=_cp(("parallel",)))(slot, shard)


def _allgather_chips(buf, *, name, forward_to_sibling):
    _, R, C = buf.shape
    half = R // 2

    def body(b_ref, o_ref, send_sems, recv_sems):
        x, y, c, chips = _place()
        me = 2 * x + y
        if forward_to_sibling:
            my_rows = pl.ds(pl.multiple_of(c * half, 16), half)
            sib_rows = pl.ds(pl.multiple_of((1 - c) * half, 16), half)
        else:
            my_rows = sib_rows = pl.ds(0, R)
        own = o_ref.at[me, my_rows]
        sends = [_remote(own, own, send_sems, recv_sems, j, (cx, cy, c)) for j, (cx, cy) in enumerate(chips)]
        for cp in sends:
            cp.start()
        passed = []
        for j, (cx, cy) in enumerate(chips):
            landed = o_ref.at[2 * cx + cy, my_rows]
            _remote(landed, landed, send_sems, recv_sems, j, (cx, cy, c)).wait_recv()
            if forward_to_sibling:
                fw = _remote(landed, landed, send_sems, recv_sems, 3 + j, (x, y, 1 - c))
                fw.start()
                passed.append(fw)
        if forward_to_sibling:
            for j, (cx, cy) in enumerate(chips):
                got = o_ref.at[2 * cx + cy, sib_rows]
                _remote(got, got, send_sems, recv_sems, 3 + j, (x, y, 1 - c)).wait_recv()
        for cp in sends + passed:
            cp.wait_send()

    return _comm_call(body, jax.ShapeDtypeStruct(buf.shape, buf.dtype), 6, name, buf, alias={0: 0})


def _sibling_exchange(g, *, name):
    _, R, C = g.shape
    half = R // 2

    def body(g_ref, o_ref, send_sems, recv_sems):
        x, y, c, _ = _place()
        other = pl.ds(pl.multiple_of((1 - c) * half, 8), half)
        cps = [_remote(g_ref.at[s, other], o_ref.at[s], send_sems, recv_sems, s, (x, y, 1 - c)) for s in range(N_CHIPS)]
        for cp in cps:
            cp.start()
        for cp in cps:
            cp.wait()

    return _comm_call(body, jax.ShapeDtypeStruct((N_CHIPS, half, C), g.dtype), N_CHIPS, name, g)


def _chip_exchange(h, parts, *, name):
    def body(h_ref, p_ref, o_ref, send_sems, recv_sems):
        x, y, c, chips = _place()
        me = 2 * x + y
        cps = [_remote(h_ref.at[2 * cx + cy], o_ref.at[me], send_sems, recv_sems, j, (cx, cy, c))
               for j, (cx, cy) in enumerate(chips)]
        for cp in cps:
            cp.start()
        for j, (cx, cy) in enumerate(chips):
            got = o_ref.at[2 * cx + cy]
            _remote(got, got, send_sems, recv_sems, j, (cx, cy, c)).wait_recv()
        for cp in cps:
            cp.wait_send()

    return _comm_call(body, jax.ShapeDtypeStruct(parts.shape, parts.dtype), 3, name, h, parts, alias={1: 0})


def _sibling_allgather(full, *, name):
    R, C = full.shape
    half = R // 2

    def body(f_ref, o_ref, send_sems, recv_sems):
        x, y, c, _ = _place()
        mine = o_ref.at[pl.ds(pl.multiple_of(c * half, 8), half)]
        cp = _remote(mine, mine, send_sems, recv_sems, 0, (x, y, 1 - c))
        cp.start()
        got = o_ref.at[pl.ds(pl.multiple_of((1 - c) * half, 8), half)]
        _remote(got, got, send_sems, recv_sems, 0, (x, y, 1 - c)).wait_recv()
        cp.wait_send()

    return _comm_call(body, jax.ShapeDtypeStruct(full.shape, full.dtype), 1, name, full, alias={0: 0})


def _sum_halves(g, recv, place, *, name="sum_halves", tr=480):
    _, R, C = g.shape
    half = R // 2
    nblk = half // tr

    def body(pl_ref, g_ref, r_ref, o_ref, own_ref):
        v = (g_ref[...] + r_ref[...]).astype(BF16)
        o_ref[...] = v

        @pl.when(pl.program_id(1) == pl_ref[1])
        def _():
            own_ref[...] = v

    blk = pl.BlockSpec((1, tr, C), lambda i, s, p: (s, i, 0))
    return pl.pallas_call(
        body,
        grid_spec=pltpu.PrefetchScalarGridSpec(
            num_scalar_prefetch=1, grid=(nblk, N_CHIPS),
            in_specs=[pl.BlockSpec((1, tr, C), lambda i, s, p: (s, p[0] * nblk + i, 0)), blk],
            out_specs=[blk, pl.BlockSpec((1, tr, C), lambda i, s, p: (p[1], i, 0))]),
        out_shape=[jax.ShapeDtypeStruct((N_CHIPS, half, C), BF16)] * 2, name=name,
        compiler_params=_cp(("parallel", "arbitrary")))(place, g, recv)


def _sum_chips(parts, place, *, name="sum_chips", tr=480):
    _, R, C = parts.shape
    nblk = R // tr

    def body(pl_ref, p_ref, o_ref):
        acc = p_ref[0].astype(F32) + p_ref[1].astype(F32)
        o_ref[...] = (acc + p_ref[2].astype(F32)) + p_ref[3].astype(F32)

    return pl.pallas_call(
        body,
        grid_spec=pltpu.PrefetchScalarGridSpec(
            num_scalar_prefetch=1, grid=(nblk,), in_specs=[pl.BlockSpec((N_CHIPS, tr, C), lambda i, p: (0, i, 0))],
            out_specs=pl.BlockSpec((tr, C), lambda i, p: (p[0] * nblk + i, 0))),
        out_shape=jax.ShapeDtypeStruct((2 * R, C), F32), name=name, compiler_params=_cp(("parallel",)))(place, parts)


def _adamw(w, g, m, v, *, name, tr=128):
    R, C = w.shape
    bc1 = 1.0 - ADAM_B1 ** ADAM_STEP
    bc2 = 1.0 - ADAM_B2 ** ADAM_STEP

    def body(w_ref, g_ref, m_ref, v_ref, d_ref, nm_ref, nv_ref):
        gv = g_ref[...]
        nm = ADAM_B1 * m_ref[...] + (1.0 - ADAM_B1) * gv
        nv = ADAM_B2 * v_ref[...] + (1.0 - ADAM_B2) * (gv * gv)
        d_ref[...] = -ADAM_LR * ((nm / bc1) / (jnp.sqrt(nv / bc2) + ADAM_EPS) + ADAM_WD * w_ref[...])
        nm_ref[...] = nm
        nv_ref[...] = nv

    blk = pl.BlockSpec((tr, C), lambda i: (i, 0))
    return pl.pallas_call(
        body, grid=(R // tr,), in_specs=[blk] * 4, out_specs=[blk] * 3,
        out_shape=[jax.ShapeDtypeStruct((R, C), F32)] * 3, name=name, compiler_params=_cp(("parallel",)))(w, g, m, v)


ROW = 1024
BIG = ("w_mem_kv", "w_out", "w_up", "w_down", "attn_w_in", "lru_w_in")
SMALL_SHARDED = ("lru_conv_w", "lru_conv_b", "lru_ba", "lru_bx", "lru_lambda")
REPLICATED = ("mix_norm", "mlp_norm", "mem_norm", "final_norm", "attn_sinks", "lru_wa", "lru_wx")
SMALL = REPLICATED + SMALL_SHARDED
WEIGHTS = ("mix_norm", "mlp_norm", "mem_norm", "final_norm", "w_mem_kv", "w_out", "w_up", "w_down", "attn_w_in",
           "attn_sinks", "lru_w_in", "lru_conv_w", "lru_conv_b", "lru_wa", "lru_ba", "lru_wx", "lru_bx", "lru_lambda")
SMALL_W_ROWS = 32
SMALL_G_ROWS = 192
ADAM_SMALL_ROWS = 640


def _rows(a):
    return a.reshape(-1, ROW)


def _flat_pad(parts, total):
    flat = jnp.concatenate([p.reshape(-1) for p in parts])
    return jnp.pad(flat, (0, total - flat.shape[0]))


def _pad_rows(a):
    flat = a.reshape(-1)
    n = -(-flat.shape[0] // ROW) * ROW
    return jnp.pad(flat, (0, n - flat.shape[0])).reshape(-1, ROW)


def _gather_weights(P, chip1):
    big = [_rows(P[n].astype(BF16)) for n in BIG]
    small = _flat_pad([P[n] for n in SMALL_SHARDED], SMALL_W_ROWS * ROW // 2)
    small_bits = lax.bitcast_convert_type(small, BF16).reshape(SMALL_W_ROWS, ROW)
    packed = jnp.concatenate(big + [small_bits], axis=0)
    placed = _place_slot(packed, chip1, N_CHIPS, name="place_weights", tr=packed.shape[0] // 5)
    full = _allgather_chips(placed, name="allgather_weights", forward_to_sibling=True)
    W = {n: P[n] for n in REPLICATED}
    off = 0
    per_layer = {}
    for n in BIG:
        r = big[BIG.index(n)].shape[0]
        per_layer[n] = full[:, off:off + r]
        off += r
    kv = per_layer["w_mem_kv"].reshape(N_CHIPS, DEPTH, -1, D_MODEL)
    W["w_mem_kv"] = [kv[:, l].reshape(1, D_MODEL, D_MODEL) for l in range(DEPTH)]
    wo = per_layer["w_out"].reshape(N_CHIPS, DEPTH, -1, D_MODEL)
    W["w_out"] = [wo[:, l].reshape(1, MIX_OUT_W, D_MODEL) for l in range(DEPTH)]
    wu = per_layer["w_up"].reshape(N_CHIPS, DEPTH, D_MODEL, D_FF // N_CHIPS)
    W["w_up"] = [wu[:, l] for l in range(DEPTH)]
    wd = per_layer["w_down"].reshape(N_CHIPS, DEPTH, D_FF // N_CHIPS, D_MODEL)
    W["w_down"] = [wd[:, l].reshape(1, D_FF, D_MODEL) for l in range(DEPTH)]
    W["attn_w_in"] = per_layer["attn_w_in"].reshape(N_CHIPS, D_MODEL, ATTN_IN_W // N_CHIPS)
    W["lru_w_in"] = per_layer["lru_w_in"].reshape(N_CHIPS, D_MODEL, LRU_IN_W // N_CHIPS)
    sm = lax.bitcast_convert_type(full[:, off:off + SMALL_W_ROWS].reshape(N_CHIPS, -1, 2), F32)
    o = 0
    for n in SMALL_SHARDED:
        shp = P[n].shape[1:]
        cnt = math.prod(shp)
        piece = sm[:, o:o + cnt].reshape((N_CHIPS,) + shp)
        piece = jnp.moveaxis(piece, 0, -2)
        W[n] = piece.reshape(shp[:-1] + (N_CHIPS * shp[-1],)).reshape(-1, D_MODEL)
        o += cnt
    W["lru_wa"] = P["lru_wa"][0].astype(BF16)
    W["lru_wx"] = P["lru_wx"][0].astype(BF16)
    return W


def _small_grad_list(G):
    return [G["mix_norm"], G["mlp_norm"], G["mem_norm"], G["final_norm"], jnp.pad(G["attn_sinks"].reshape(-1), (0, ROW - ATTN_HEADS)),
            G["lru_wa"], G["lru_wx"], G["lru_conv_w"], G["lru_conv_b"], G["lru_ba"], G["lru_bx"], G["lru_lambda"]]


SMALL_G_SIZES = (2 * D_MODEL, 2 * D_MODEL, D_MODEL, D_MODEL, ROW, 2 * 8 * 128 * 128, 2 * 8 * 128 * 128,
                 4 * D_MODEL, D_MODEL, 2 * D_MODEL, 2 * D_MODEL, 2 * D_MODEL)


def _reduce_grads(G, place, chip1):
    c4 = lambda a, r: a.reshape(N_CHIPS, r, ROW)
    pieces = [
        c4(G["w_mem_kv"][0], 256), c4(G["w_mem_kv"][1], 256), c4(G["w_out"][0], 384), c4(G["w_out"][1], 384),
        G["w_up"][0], G["w_up"][1], c4(G["w_down"][0], 1024), c4(G["w_down"][1], 1024),
        c4(G["attn_w_in"], 512), c4(G["lru_w_in"], 640),
        _flat_pad(_small_grad_list(G), N_CHIPS * SMALL_G_ROWS * ROW).reshape(N_CHIPS, SMALL_G_ROWS, ROW),
    ]
    packed = jnp.concatenate(pieces, axis=1)
    recv = _sibling_exchange(packed, name="grad_sibling_exchange")
    halves, landing = _sum_halves(packed, recv, place)
    parts = _chip_exchange(halves, landing, name="grad_chip_exchange")
    full = _sibling_allgather(_sum_chips(parts, place), name="grad_sibling_allgather")
    n_big = full.shape[0] - SMALL_G_ROWS
    small_placed = _place_slot(full[n_big:], chip1, N_CHIPS, name="place_small_grads", tr=SMALL_G_ROWS)
    small_all = _allgather_chips(small_placed, name="allgather_small_grads", forward_to_sibling=False)
    flat = small_all.reshape(-1)
    small = {}
    o = 0
    names = ("mix_norm", "mlp_norm", "mem_norm", "final_norm", "attn_sinks", "lru_wa", "lru_wx",
             "lru_conv_w", "lru_conv_b", "lru_ba", "lru_bx", "lru_lambda")
    for n, cnt in zip(names, SMALL_G_SIZES):
        small[n] = flat[o:o + cnt]
        o += cnt
    return full[:n_big], small


def kernel(x, mem, positions, mix_norm, mlp_norm, mem_norm, final_norm, w_mem_kv, w_out, w_up, w_down, attn_w_in, attn_sinks, lru_w_in, lru_conv_w, lru_conv_b, lru_wa, lru_ba, lru_wx, lru_bx, lru_lambda, loss_target, m_mix_norm, m_mlp_norm, m_mem_norm, m_final_norm, m_w_mem_kv, m_w_out, m_w_up, m_w_down, m_attn_w_in, m_attn_sinks, m_lru_w_in, m_lru_conv_w, m_lru_conv_b, m_lru_wa, m_lru_ba, m_lru_wx, m_lru_bx, m_lru_lambda, v_mix_norm, v_mlp_norm, v_mem_norm, v_final_norm, v_w_mem_kv, v_w_out, v_w_up, v_w_down, v_attn_w_in, v_attn_sinks, v_lru_w_in, v_lru_conv_w, v_lru_conv_b, v_lru_wa, v_lru_ba, v_lru_wx, v_lru_bx, v_lru_lambda):
    P = dict(mix_norm=mix_norm, mlp_norm=mlp_norm, mem_norm=mem_norm, final_norm=final_norm, w_mem_kv=w_mem_kv, w_out=w_out,
             w_up=w_up, w_down=w_down, attn_w_in=attn_w_in, attn_sinks=attn_sinks, lru_w_in=lru_w_in, lru_conv_w=lru_conv_w,
             lru_conv_b=lru_conv_b, lru_wa=lru_wa, lru_ba=lru_ba, lru_wx=lru_wx, lru_bx=lru_bx, lru_lambda=lru_lambda)
    M1 = dict(mix_norm=m_mix_norm, mlp_norm=m_mlp_norm, mem_norm=m_mem_norm, final_norm=m_final_norm, w_mem_kv=m_w_mem_kv,
              w_out=m_w_out, w_up=m_w_up, w_down=m_w_down, attn_w_in=m_attn_w_in, attn_sinks=m_attn_sinks, lru_w_in=m_lru_w_in,
              lru_conv_w=m_lru_conv_w, lru_conv_b=m_lru_conv_b, lru_wa=m_lru_wa, lru_ba=m_lru_ba, lru_wx=m_lru_wx,
              lru_bx=m_lru_bx, lru_lambda=m_lru_lambda)
    V2 = dict(mix_norm=v_mix_norm, mlp_norm=v_mlp_norm, mem_norm=v_mem_norm, final_norm=v_final_norm, w_mem_kv=v_w_mem_kv,
              w_out=v_w_out, w_up=v_w_up, w_down=v_w_down, attn_w_in=v_attn_w_in, attn_sinks=v_attn_sinks, lru_w_in=v_lru_w_in,
              lru_conv_w=v_lru_conv_w, lru_conv_b=v_lru_conv_b, lru_wa=v_lru_wa, lru_ba=v_lru_ba, lru_wx=v_lru_wx,
              lru_bx=v_lru_bx, lru_lambda=v_lru_lambda)
    chip = 2 * lax.axis_index("x") + lax.axis_index("y")
    chip1 = chip.astype(jnp.int32).reshape(1)
    place = jnp.stack([lax.axis_index("c").astype(jnp.int32), chip.astype(jnp.int32)])

    W = _gather_weights(P, chip1)
    loss, dx, G = _local_step(x[0], mem[0], positions[0], loss_target[0], W)
    loss = lax.psum(loss, ("x", "y", "c"))
    big_rows, small = _reduce_grads(G, place, chip1)

    grads, deltas, new_m, new_v = {}, {}, {}, {}
    off = 0
    for n in BIG:
        r = math.prod(P[n].shape) // ROW
        g = big_rows[off:off + r]
        off += r
        d, nm, nv = _adamw(_rows(P[n]), g, _rows(M1[n]), _rows(V2[n]), name=f"adamw_{n}")
        grads[n], deltas[n], new_m[n], new_v[n] = (t.reshape(P[n].shape) for t in (g, d, nm, nv))

    for n in SMALL:
        g = small[n]
        if n in SMALL_SHARDED:
            shard = P[n].shape[-1]
            g = lax.dynamic_slice_in_dim(g.reshape(-1, N_CHIPS * shard), chip * shard, shard, axis=1)
        elif n == "attn_sinks":
            g = g[:ATTN_HEADS]
        grads[n] = g.reshape(P[n].shape)
    packs = []
    for src in (P, grads, M1, V2):
        a = jnp.concatenate([_pad_rows(src[n]) for n in SMALL], axis=0)
        packs.append(jnp.pad(a, ((0, ADAM_SMALL_ROWS - a.shape[0]), (0, 0))))
    d_s, nm_s, nv_s = _adamw(*packs, name="adamw_small")
    o = 0
    for n in SMALL:
        cnt = math.prod(P[n].shape)
        r = -(-cnt // ROW)
        for dst, src in ((deltas, d_s), (new_m, nm_s), (new_v, nv_s)):
            dst[n] = src[o:o + r].reshape(-1)[:cnt].reshape(P[n].shape)
        o += r

    return (loss, dx[None], *[grads[n] for n in WEIGHTS], *[deltas[n] for n in WEIGHTS],
            *[new_m[n] for n in WEIGHTS], *[new_v[n] for n in WEIGHTS])
```

```python
import functools
import math

import jax
import jax.numpy as jnp
from jax import lax
from jax.experimental import pallas as pl
from jax.experimental.pallas import tpu as pltpu

F32 = jnp.float32
BF16 = jnp.bfloat16
MESH = pl.DeviceIdType.MESH

D_MODEL = 1024
DEPTH = 2
EPS = 1e-6
ATTN_HEADS = 16
ATTN_KV_HEADS = 4
HEAD_DIM = 64
WINDOW = 128
BLOCK = 128
ROPE_THETA = 500000.0
ROPE_DIM = 16
Q_W = 1024
KV_W = 256
MEM_LEN = 256
MEM_HEADS = 4
MEM_HEAD_DIM = 128
MEM_W = 512
LRU_BLOCKS = 8
LRU_C = 8.0
ATTN_IN_W = 2048
LRU_IN_W = 2560
MIX_OUT_W = 1536
D_FF = 4096
NEG = -1e30
N_CHIPS = 4

ADAM_LR = 0.001
ADAM_B1 = 0.9
ADAM_B2 = 0.999
ADAM_EPS = 1e-08
ADAM_WD = 0.01
ADAM_STEP = 10

LANES = 128
SCAN_ROWS = 512
VMEM_LIMIT = 56 * 1024 * 1024

NT = (((1,), (1,)), ((), ()))
TN = (((0,), (0,)), ((), ()))


def _cp(sem=None):
    return pltpu.CompilerParams(dimension_semantics=sem, vmem_limit_bytes=VMEM_LIMIT)


HBM = pl.BlockSpec(memory_space=pl.ANY)
GATHER_SEMS = 6


def _place():
    x, y, c = lax.axis_index("x"), lax.axis_index("y"), lax.axis_index("c")
    chips = [(1 - x, y), (x, 1 - y), (1 - x, 1 - y)]
    return x, y, c, chips


def _remote(src, dst, send_sems, recv_sems, k, to):
    return pltpu.make_async_remote_copy(src_ref=src, dst_ref=dst, send_sem=send_sems.at[k], recv_sem=recv_sems.at[k],
                                        device_id=to, device_id_type=MESH)


def _gather_start(o_ref, send_sems, recv_sems):
    x, y, c, chips = _place()
    half = o_ref.shape[1] // 2
    own = o_ref.at[2 * x + y, pl.ds(pl.multiple_of(c * half, 16), half)]
    for j, (cx, cy) in enumerate(chips):
        _remote(own, own, send_sems, recv_sems, j, (cx, cy, c)).start()


def _gather_finish(o_ref, send_sems, recv_sems):
    x, y, c, chips = _place()
    half = o_ref.shape[1] // 2
    my_rows = pl.ds(pl.multiple_of(c * half, 16), half)
    sib_rows = pl.ds(pl.multiple_of((1 - c) * half, 16), half)
    own = o_ref.at[2 * x + y, my_rows]
    passed = []
    for j, (cx, cy) in enumerate(chips):
        landed = o_ref.at[2 * cx + cy, my_rows]
        _remote(landed, landed, send_sems, recv_sems, j, (cx, cy, c)).wait_recv()
        fw = _remote(landed, landed, send_sems, recv_sems, 3 + j, (x, y, 1 - c))
        fw.start()
        passed.append(fw)
    for j, (cx, cy) in enumerate(chips):
        got = o_ref.at[2 * cx + cy, sib_rows]
        _remote(got, got, send_sems, recv_sems, 3 + j, (x, y, 1 - c)).wait_recv()
    for j, (cx, cy) in enumerate(chips):
        _remote(own, own, send_sems, recv_sems, j, (cx, cy, c)).wait_send()
    for fw in passed:
        fw.wait_send()


class _Hosted:
    def __init__(self, buf, n_in, n_out):
        self.buf = buf
        self.on = buf is not None
        self.alias = {n_in: n_out} if self.on else {}
        self.in_specs = [HBM] if self.on else []
        self.out_specs = [HBM] if self.on else []
        self.out_shape = [jax.ShapeDtypeStruct(buf.shape, buf.dtype)] if self.on else []
        self.scratch = [pltpu.SemaphoreType.DMA((GATHER_SEMS,)), pltpu.SemaphoreType.DMA((GATHER_SEMS,))] if self.on else []
        self.args = [buf] if self.on else []

    def split(self, refs, n_in, n_out):
        refs = list(refs)
        if not self.on:
            return refs[:n_in], refs[n_in:n_in + n_out], refs[n_in + n_out:], None
        ins, outs = refs[:n_in], refs[n_in + 1:n_in + 1 + n_out]
        rest = refs[n_in + n_out + 2:]
        return ins, outs, rest[:-2], (refs[n_in + 1 + n_out], rest[-2], rest[-1])

    def run(self, gref, step, n_steps, compute):
        if gref is None:
            return compute()

        @pl.when(step == 0)
        def _():
            _gather_start(*gref)

        compute()

        @pl.when(step == n_steps - 1)
        def _():
            _gather_finish(*gref)


def _mm_nn(a, w3, *, name, out_dtype=F32, norm_g=None, resid=None, relu2=False, tm=512, gather=None):
    M, K = a.shape
    ns, _, n = w3.shape
    N = ns * n
    tm = min(tm, M)
    has_norm = norm_g is not None
    has_res = resid is not None
    n_in = 2 + has_norm + has_res
    n_out = (2 if relu2 else 1) + has_norm
    host = _Hosted(gather, n_in, n_out)

    def body(*refs):
        ins, outs, _, gref = host.split(refs, n_in, n_out)
        a_ref, w_ref = ins[0], ins[1]
        g_ref = ins[2] if has_norm else None
        r_ref = ins[-1] if has_res else None

        def compute():
            if has_norm:
                xv = a_ref[...]
                rs = lax.rsqrt(jnp.mean(xv * xv, axis=-1, keepdims=True) + EPS)
                ab = (xv * rs * g_ref[...]).astype(BF16)
                outs[-1][...] = ab
            else:
                ab = a_ref[...]
            for s in range(ns):
                acc = jnp.dot(ab, w_ref[s], preferred_element_type=F32)
                sl = slice(s * n, (s + 1) * n)
                if relu2:
                    outs[0][:, sl] = acc.astype(BF16)
                    rl = jnp.maximum(acc, 0.0)
                    outs[1][:, sl] = (rl * rl).astype(BF16)
                elif has_res:
                    outs[0][:, sl] = r_ref[:, sl] + acc
                else:
                    outs[0][:, sl] = acc.astype(out_dtype)

        host.run(gref, pl.program_id(0), M // tm, compute)

    row = lambda w: pl.BlockSpec((tm, w), lambda i: (i, 0))
    in_specs = [row(K), pl.BlockSpec((ns, K, n), lambda i: (0, 0, 0))]
    args = [a, w3]
    if has_norm:
        in_specs.append(pl.BlockSpec((1, K), lambda i: (0, 0)))
        args.append(norm_g.reshape(1, K))
    if has_res:
        in_specs.append(row(N))
        args.append(resid)
    if relu2:
        out_shape = [jax.ShapeDtypeStruct((M, N), BF16), jax.ShapeDtypeStruct((M, N), BF16)]
        out_specs = [row(N), row(N)]
    else:
        out_shape = [jax.ShapeDtypeStruct((M, N), F32 if has_res else out_dtype)]
        out_specs = [row(N)]
    if has_norm:
        out_shape.append(jax.ShapeDtypeStruct((M, K), BF16))
        out_specs.append(row(K))
    res = pl.pallas_call(body, grid=(M // tm,), in_specs=in_specs + host.in_specs, out_specs=out_specs + host.out_specs,
                         out_shape=out_shape + host.out_shape, scratch_shapes=host.scratch, input_output_aliases=host.alias,
                         name=name, compiler_params=_cp(("arbitrary",) if host.on else ("parallel",)))(*args, *host.args)
    return res if len(res) > 1 else res[0]


def _mm_nt(g, w3, *, name, out_dtype=BF16, up=None, norm_x=None, norm_g=None, dres=None, tm=512):
    M = g.shape[0]
    ns, K, n = w3.shape
    tm = min(tm, M)
    has_up = up is not None
    has_norm = norm_x is not None
    has_res = dres is not None

    def body(*refs):
        refs = list(refs)
        g_ref, w_ref = refs[0], refs[1]
        pos = 2
        if has_up:
            up_ref = refs[pos]
            pos += 1
        if has_norm:
            x_ref, gn_ref = refs[pos], refs[pos + 1]
            pos += 2
        if has_res:
            r_ref = refs[pos]
            pos += 1
        outs = refs[pos:]
        acc = None
        for s in range(ns):
            part = lax.dot_general(g_ref[:, s * n:(s + 1) * n], w_ref[s], NT, preferred_element_type=F32)
            acc = part if acc is None else acc + part
        if has_up:
            outs[0][...] = (acc * (2.0 * jnp.maximum(up_ref[...].astype(F32), 0.0))).astype(BF16)
        elif has_norm:
            xv = x_ref[...]
            rs = lax.rsqrt(jnp.mean(xv * xv, axis=-1, keepdims=True) + EPS)
            xn = xv * rs
            dxn = acc * gn_ref[...]
            dx = rs * (dxn - xn * jnp.mean(dxn * xn, axis=-1, keepdims=True))
            if has_res:
                dx = dx + r_ref[...]
            outs[0][...] = dx
            outs[1][...] = dx.astype(BF16)

            @pl.when(pl.program_id(0) == 0)
            def _():
                outs[2][...] = jnp.zeros_like(outs[2])

            outs[2][...] += jnp.sum(acc * xn, axis=0, keepdims=True)
        else:
            outs[0][...] = acc.astype(out_dtype)

    row = lambda w: pl.BlockSpec((tm, w), lambda i: (i, 0))
    in_specs = [row(ns * n), pl.BlockSpec((ns, K, n), lambda i: (0, 0, 0))]
    args = [g, w3]
    if has_up:
        in_specs.append(row(K))
        args.append(up)
    if has_norm:
        in_specs += [row(K), pl.BlockSpec((1, K), lambda i: (0, 0))]
        args += [norm_x, norm_g.reshape(1, K)]
    if has_res:
        in_specs.append(row(K))
        args.append(dres)
    if has_norm:
        out_shape = [jax.ShapeDtypeStruct((M, K), F32), jax.ShapeDtypeStruct((M, K), BF16),
                     jax.ShapeDtypeStruct((1, K), F32)]
        out_specs = [row(K), row(K), pl.BlockSpec((1, K), lambda i: (0, 0))]
        sem = ("arbitrary",)
    else:
        out_shape = [jax.ShapeDtypeStruct((M, K), BF16 if has_up else out_dtype)]
        out_specs = [row(K)]
        sem = ("parallel",)
    res = pl.pallas_call(body, grid=(M // tm,), in_specs=in_specs, out_specs=out_specs, out_shape=out_shape,
                         name=name, compiler_params=_cp(sem))(*args)
    return res if len(res) > 1 else res[0]


def _mm_tn(a, g, ns, *, name, tk=512, tm=4096):
    M, K = a.shape
    n = g.shape[1] // ns
    tm = min(tm, M)
    tk = min(tk, K)

    def body(a_ref, g_ref, o_ref):
        @pl.when(pl.program_id(2) == 0)
        def _():
            o_ref[...] = jnp.zeros_like(o_ref)

        o_ref[0] += lax.dot_general(a_ref[...], g_ref[...], TN, preferred_element_type=F32)

    return pl.pallas_call(
        body, grid=(ns, K // tk, M // tm),
        in_specs=[pl.BlockSpec((tm, tk), lambda s, k, m: (m, k)), pl.BlockSpec((tm, n), lambda s, k, m: (m, s))],
        out_specs=pl.BlockSpec((1, tk, n), lambda s, k, m: (s, k, 0)),
        out_shape=jax.ShapeDtypeStruct((ns, K, n), F32), name=name,
        compiler_params=_cp(("parallel", "parallel", "arbitrary")))(a, g)


def _final(x, gain, target, *, name="final_loss", tr=256):
    S, Dm = x.shape
    tr = min(tr, S)

    def body(x_ref, g_ref, t_ref, loss_ref, dx_ref, dxb_ref, dg_ref):
        @pl.when(pl.program_id(0) == 0)
        def _():
            loss_ref[...] = jnp.zeros_like(loss_ref)
            dg_ref[...] = jnp.zeros_like(dg_ref)

        xv = x_ref[...]
        gv = g_ref[...]
        rs = lax.rsqrt(jnp.mean(xv * xv, axis=-1, keepdims=True) + EPS)
        xn = xv * rs
        err = xn * gv - t_ref[...]
        loss_ref[...] += 0.5 * jnp.sum(jnp.mean(err * err, axis=-1, keepdims=True), axis=0, keepdims=True)
        dout = err * (1.0 / Dm)
        dg_ref[...] += jnp.sum(dout * xn, axis=0, keepdims=True)
        dxn = dout * gv
        dx = rs * (dxn - xn * jnp.mean(dxn * xn, axis=-1, keepdims=True))
        dx_ref[...] = dx
        dxb_ref[...] = dx.astype(BF16)

    row = pl.BlockSpec((tr, Dm), lambda i: (i, 0))
    return pl.pallas_call(
        body, grid=(S // tr,),
        in_specs=[row, pl.BlockSpec((1, Dm), lambda i: (0, 0)), row],
        out_specs=[pl.BlockSpec((1, 1), lambda i: (0, 0)), row, row, pl.BlockSpec((1, Dm), lambda i: (0, 0))],
        out_shape=[jax.ShapeDtypeStruct((1, 1), F32), jax.ShapeDtypeStruct((S, Dm), F32),
                   jax.ShapeDtypeStruct((S, Dm), BF16), jax.ShapeDtypeStruct((1, Dm), F32)],
        name=name, compiler_params=_cp(("arbitrary",)))(x, gain.reshape(1, Dm), target)


def _rope_tables(positions):
    half = ROPE_DIM // 2
    inv_freq = ROPE_THETA ** (-2.0 * jnp.arange(half, dtype=F32) / ROPE_DIM)
    ang = positions.astype(F32)[:, None] * inv_freq
    cos, sin = jnp.cos(ang), jnp.sin(ang)
    S = positions.shape[0]
    ones = jnp.ones((S, HEAD_DIM - ROPE_DIM), F32)
    cos64 = jnp.concatenate([cos, cos, ones], axis=1)
    sin64 = jnp.concatenate([-sin, sin, 0.0 * ones], axis=1)
    return jnp.tile(cos64, (1, 2)), jnp.tile(sin64, (1, 2))


def _rope_partner(t):
    lane = lax.broadcasted_iota(jnp.int32, t.shape, 1)
    low = (lane & (HEAD_DIM - 1)) < (ROPE_DIM // 2)
    return jnp.where(low, pltpu.roll(t, LANES - ROPE_DIM // 2, 1), pltpu.roll(t, ROPE_DIM // 2, 1))


def _qk_prep(p, cos_t, sin_t, *, name="qk_prep", tr=256, gather=None):
    S = p.shape[0]
    tr = min(tr, S)
    scale = HEAD_DIM ** -0.5
    host = _Hosted(gather, 3, 4)

    def body(*refs):
        ins, outs, _, gref = host.split(refs, 3, 4)
        host.run(gref, pl.program_id(0), S // tr, lambda: inner(*ins, *outs))

    def inner(p_ref, c_ref, s_ref, q_ref, k_ref, v_ref, va_ref):
        cs, sn = c_ref[...], s_ref[...]
        lane = lax.broadcasted_iota(jnp.int32, (tr, LANES), 1)
        lo = lane < HEAD_DIM
        for c in range(Q_W // LANES):
            t = p_ref[:, c * LANES:(c + 1) * LANES]
            q_ref[:, c * LANES:(c + 1) * LANES] = ((t * cs + _rope_partner(t) * sn) * scale).astype(BF16)
        for c in range(KV_W // LANES):
            t = p_ref[:, Q_W + c * LANES:Q_W + (c + 1) * LANES]
            kc = t * cs + _rope_partner(t) * sn
            vc = p_ref[:, Q_W + KV_W + c * LANES:Q_W + KV_W + (c + 1) * LANES]
            for arr, ref in ((kc, k_ref), (vc, v_ref)):
                sw = pltpu.roll(arr, HEAD_DIM, 1)
                ref[:, (2 * c) * LANES:(2 * c + 1) * LANES] = jnp.where(lo, arr, sw).astype(BF16)
                ref[:, (2 * c + 1) * LANES:(2 * c + 2) * LANES] = jnp.where(lo, sw, arr).astype(BF16)
            sw = pltpu.roll(vc, HEAD_DIM, 1)
            for k, aug in enumerate((jnp.where(lo, vc, 1.0), jnp.where(lo, 1.0, sw), jnp.where(lo, sw, 1.0), jnp.where(lo, 1.0, vc))):
                va_ref[:, (4 * c + k) * LANES:(4 * c + k + 1) * LANES] = aug.astype(BF16)

    row = lambda w: pl.BlockSpec((tr, w), lambda i: (i, 0))
    return pl.pallas_call(
        body, grid=(S // tr,), in_specs=[row(ATTN_IN_W), row(LANES), row(LANES)] + host.in_specs,
        out_specs=[row(Q_W), row(2 * KV_W), row(2 * KV_W), row(4 * KV_W)] + host.out_specs,
        out_shape=[jax.ShapeDtypeStruct((S, Q_W), BF16), jax.ShapeDtypeStruct((S, 2 * KV_W), BF16),
                   jax.ShapeDtypeStruct((S, 2 * KV_W), BF16), jax.ShapeDtypeStruct((S, 4 * KV_W), BF16)] + host.out_shape,
        scratch_shapes=host.scratch, input_output_aliases=host.alias,
        name=name, compiler_params=_cp(("arbitrary",) if host.on else ("parallel",)))(p, cos_t, sin_t, *host.args)


def _qk_prep_bwd(dq, dk, dv, dmq, cos_t, sin_t, *, name="qk_prep_bwd", tr=256):
    S = dq.shape[0]
    tr = min(tr, S)

    def body(dq_ref, dk_ref, dv_ref, dmq_ref, c_ref, s_ref, o_ref):
        cs, sn = c_ref[...], s_ref[...]
        for c in range(Q_W // LANES):
            t = dq_ref[:, c * LANES:(c + 1) * LANES]
            o_ref[:, c * LANES:(c + 1) * LANES] = (t * cs - _rope_partner(t) * sn).astype(BF16)
        for c in range(KV_W // LANES):
            t = dk_ref[:, c * LANES:(c + 1) * LANES]
            o_ref[:, Q_W + c * LANES:Q_W + (c + 1) * LANES] = (t * cs - _rope_partner(t) * sn).astype(BF16)
        o_ref[:, Q_W + KV_W:Q_W + 2 * KV_W] = dv_ref[...].astype(BF16)
        o_ref[:, Q_W + 2 * KV_W:] = dmq_ref[...]

    row = lambda w: pl.BlockSpec((tr, w), lambda i: (i, 0))
    return pl.pallas_call(
        body, grid=(S // tr,), in_specs=[row(Q_W), row(KV_W), row(KV_W), row(MEM_W), row(LANES), row(LANES)],
        out_specs=row(ATTN_IN_W), out_shape=jax.ShapeDtypeStruct((S, ATTN_IN_W), BF16),
        name=name, compiler_params=_cp(("parallel",)))(dq, dk, dv, dmq, cos_t, sin_t)


def _band(n, S):
    start = pl.multiple_of(jnp.clip((n - 1) * BLOCK, 0, S - 3 * BLOCK), BLOCK)
    qi = lax.broadcasted_iota(jnp.int32, (BLOCK, 3 * BLOCK), 0) + n * BLOCK
    ki = lax.broadcasted_iota(jnp.int32, (BLOCK, 3 * BLOCK), 1) + start
    return start, jnp.abs(ki - qi) <= WINDOW


def _head_operand(ref, h, lo):
    c = h // 2
    t = ref[:, c * LANES:(c + 1) * LANES].astype(F32)
    return jnp.where(lo if h % 2 == 0 else jnp.logical_not(lo), t, 0.0).astype(BF16)


GROUP = ATTN_HEADS // ATTN_KV_HEADS
EVENS_FIRST = (0, 2, 1, 3)


def _attn_fwd(q, kd, va, sinks, *, name="attn_fwd", gather=None):
    S = q.shape[0]
    host = _Hosted(gather, 4, 2)

    def body(*refs):
        ins, outs, scr, gref = host.split(refs, 4, 2)
        host.run(gref, pl.program_id(0), S // BLOCK, lambda: inner(*ins, *outs, *scr))

    def inner(sink_ref, q_ref, k_ref, va_ref, o_ref, lse_ref, p_scr):
        n = pl.program_id(0)
        start, mask = _band(n, S)
        lane = lax.broadcasted_iota(jnp.int32, (BLOCK, LANES), 1)
        lo = lane < HEAD_DIM
        rows = pl.ds(start, 3 * BLOCK)
        scores = []
        for g in range(ATTN_KV_HEADS):
            qst = jnp.concatenate([_head_operand(q_ref, GROUP * g + j, lo) for j in EVENS_FIRST], axis=0)
            scores.append(lax.dot_general(qst, k_ref[rows, g * LANES:(g + 1) * LANES], NT, preferred_element_type=F32))
        ms = {}
        for g in range(ATTN_KV_HEADS):
            for pos, j in enumerate(EVENS_FIRST):
                h = GROUP * g + j
                s = jnp.where(mask, scores[g][pos * BLOCK:(pos + 1) * BLOCK], NEG)
                ms[h] = jnp.maximum(jnp.max(s, axis=-1, keepdims=True), sink_ref[h])
                p_scr[(GROUP * g + pos) * BLOCK:(GROUP * g + pos + 1) * BLOCK, :] = jnp.exp(s - ms[h]).astype(BF16)
        pvs = {}
        for g in range(ATTN_KV_HEADS):
            for par in range(2):
                r0 = (GROUP * g + 2 * par) * BLOCK
                pvs[g, par] = jnp.dot(p_scr[r0:r0 + 2 * BLOCK, :], va_ref[rows, (2 * g + par) * LANES:(2 * g + par + 1) * LANES],
                                      preferred_element_type=F32)
        lse_blk = jnp.zeros((BLOCK, LANES), F32)
        for g in range(ATTN_KV_HEADS):
            outs = {}
            for par in range(2):
                for k in range(2):
                    j = EVENS_FIRST[2 * par + k]
                    h = GROUP * g + j
                    pv = pvs[g, par][k * BLOCK:(k + 1) * BLOCK]
                    den = pltpu.roll(pv, HEAD_DIM, 1) + jnp.exp(sink_ref[h] - ms[h])
                    outs[j] = pv * (1.0 / den)
                    l = den[:, par * HEAD_DIM:par * HEAD_DIM + 1]
                    lse_blk = jnp.where(lane == h, ms[h] + jnp.log(l), lse_blk)
            for jj in range(2):
                o_ref[:, (2 * g + jj) * LANES:(2 * g + jj + 1) * LANES] = jnp.where(lo, outs[2 * jj], outs[2 * jj + 1]).astype(BF16)
        lse_ref[...] = lse_blk

    full = lambda w: pl.BlockSpec((S, w), lambda i: (0, 0))
    return pl.pallas_call(
        body, grid=(S // BLOCK,),
        in_specs=[pl.BlockSpec(memory_space=pltpu.SMEM), pl.BlockSpec((BLOCK, Q_W), lambda i: (i, 0)),
                  full(2 * KV_W), full(4 * KV_W)] + host.in_specs,
        out_specs=[pl.BlockSpec((BLOCK, Q_W), lambda i: (i, 0)), pl.BlockSpec((BLOCK, LANES), lambda i: (i, 0))] + host.out_specs,
        out_shape=[jax.ShapeDtypeStruct((S, Q_W), BF16), jax.ShapeDtypeStruct((S, LANES), F32)] + host.out_shape,
        scratch_shapes=[pltpu.VMEM((ATTN_HEADS * BLOCK, 3 * BLOCK), BF16)] + host.scratch, input_output_aliases=host.alias,
        name=name, compiler_params=_cp(("arbitrary",) if host.on else ("parallel",)))(sinks, q, kd, va, *host.args)


def _attn_bwd(q, kd, vd, ao, lse, sinks, dcat, *, name="attn_bwd"):
    S = q.shape[0]
    scale = HEAD_DIM ** -0.5

    def body(sink_ref, q_ref, k_ref, v_ref, ao_ref, lse_ref, do_ref, dq_ref, dk_ref, dv_ref, ds_ref, p_scr, dsb_scr):
        n = pl.program_id(0)

        @pl.when(n == 0)
        def _():
            dk_ref[...] = jnp.zeros_like(dk_ref)
            dv_ref[...] = jnp.zeros_like(dv_ref)
            ds_ref[...] = jnp.zeros_like(ds_ref)

        start, mask = _band(n, S)
        lane = lax.broadcasted_iota(jnp.int32, (BLOCK, LANES), 1)
        lo = lane < HEAD_DIM
        lane3 = lax.broadcasted_iota(jnp.int32, (3 * BLOCK, LANES), 1)
        row8 = lax.broadcasted_iota(jnp.int32, (8, LANES), 0)
        lane8 = lax.broadcasted_iota(jnp.int32, (8, LANES), 1)
        dsink = jnp.zeros((8, LANES), F32)
        lse_blk = lse_ref[...]
        rows = pl.ds(start, 3 * BLOCK)
        lses, deltas = {}, {}
        for c in range(Q_W // LANES):
            prod = do_ref[:, c * LANES:(c + 1) * LANES].astype(F32) * ao_ref[:, c * LANES:(c + 1) * LANES].astype(F32)
            for k in range(2):
                h = 2 * c + k
                deltas[h] = jnp.sum(jnp.where(lo if k == 0 else jnp.logical_not(lo), prod, 0.0), axis=1, keepdims=True)
                lses[h] = jnp.sum(jnp.where(lane == h, lse_blk, 0.0), axis=1, keepdims=True)
                val = -jnp.sum(jnp.exp(sink_ref[h] - lses[h]) * deltas[h], axis=0, keepdims=True)
                dsink = dsink + jnp.where((row8 == 0) & (lane8 == h), val, 0.0)
        stack = lambda ref, g: jnp.concatenate([_head_operand(ref, GROUP * g + j, lo) for j in range(GROUP)], axis=0)
        ss, dps = [], []
        for g in range(ATTN_KV_HEADS):
            ss.append(lax.dot_general(stack(q_ref, g), k_ref[rows, g * LANES:(g + 1) * LANES], NT, preferred_element_type=F32))
            dps.append(lax.dot_general(stack(do_ref, g), v_ref[rows, g * LANES:(g + 1) * LANES], NT, preferred_element_type=F32))
        for g in range(ATTN_KV_HEADS):
            for j in range(GROUP):
                h = GROUP * g + j
                r = slice(j * BLOCK, (j + 1) * BLOCK)
                hr = slice(h * BLOCK, (h + 1) * BLOCK)
                p = jnp.exp(jnp.where(mask, ss[g][r], NEG) - lses[h])
                p_scr[hr, :] = p.astype(BF16)
                dsb_scr[hr, :] = (p * (dps[g][r] - deltas[h])).astype(BF16)
        for g in range(ATTN_KV_HEADS):
            cols = slice((g // 2) * LANES, (g // 2 + 1) * LANES)
            gr = slice(GROUP * g * BLOCK, GROUP * (g + 1) * BLOCK)
            dsg = dsb_scr[gr, :]
            dqs = jnp.dot(dsg, k_ref[rows, g * LANES:(g + 1) * LANES], preferred_element_type=F32) * scale
            for jj in range(2):
                dq_ref[:, (2 * g + jj) * LANES:(2 * g + jj + 1) * LANES] = jnp.where(
                    lo, dqs[(2 * jj) * BLOCK:(2 * jj + 1) * BLOCK], dqs[(2 * jj + 1) * BLOCK:(2 * jj + 2) * BLOCK])
            half = (lane3 < HEAD_DIM) if g % 2 == 0 else (lane3 >= HEAD_DIM)
            dkr = lax.dot_general(dsg, stack(q_ref, g), TN, preferred_element_type=F32)
            dk_ref[rows, cols] += jnp.where(half, dkr + pltpu.roll(dkr, HEAD_DIM, 1), 0.0)
            dvr = lax.dot_general(p_scr[gr, :], stack(do_ref, g), TN, preferred_element_type=F32)
            dv_ref[rows, cols] += jnp.where(half, dvr + pltpu.roll(dvr, HEAD_DIM, 1), 0.0)
        ds_ref[...] += dsink

    full = lambda w: pl.BlockSpec((S, w), lambda i: (0, 0))
    blk = lambda w: pl.BlockSpec((BLOCK, w), lambda i: (i, 0))
    return pl.pallas_call(
        body, grid=(S // BLOCK,),
        in_specs=[pl.BlockSpec(memory_space=pltpu.SMEM), blk(Q_W), full(2 * KV_W), full(2 * KV_W), blk(Q_W), blk(LANES), blk(Q_W)],
        out_specs=[blk(Q_W), full(KV_W), full(KV_W), pl.BlockSpec((8, LANES), lambda i: (0, 0))],
        out_shape=[jax.ShapeDtypeStruct((S, Q_W), F32), jax.ShapeDtypeStruct((S, KV_W), F32),
                   jax.ShapeDtypeStruct((S, KV_W), F32), jax.ShapeDtypeStruct((8, LANES), F32)],
        scratch_shapes=[pltpu.VMEM((ATTN_HEADS * BLOCK, 3 * BLOCK), BF16), pltpu.VMEM((ATTN_HEADS * BLOCK, 3 * BLOCK), BF16)],
        name=name, compiler_params=_cp(("arbitrary",)))(sinks, q, kd, vd, ao, lse, dcat)


def _mem_probs(q_ref, kv_ref, h):
    scale = MEM_HEAD_DIM ** -0.5
    qh = q_ref[:, h * LANES:(h + 1) * LANES].astype(BF16)
    s = lax.dot_general(qh, kv_ref[:, h * LANES:(h + 1) * LANES], NT, preferred_element_type=F32) * scale
    m = jnp.max(s, axis=-1, keepdims=True)
    pe = jnp.exp(s - m)
    return qh, pe * (1.0 / jnp.sum(pe, axis=-1, keepdims=True))


def _memattn_fwd(p, qblk, kv, *, name="memattn_fwd", tr=512):
    S = p.shape[0]
    tr = min(tr, S)

    def body(q_ref, kv_ref, o_ref):
        for h in range(MEM_HEADS):
            _, pr = _mem_probs(q_ref, kv_ref, h)
            o = jnp.dot(pr.astype(BF16), kv_ref[:, MEM_W + h * LANES:MEM_W + (h + 1) * LANES], preferred_element_type=F32)
            o_ref[:, h * LANES:(h + 1) * LANES] = o.astype(BF16)

    return pl.pallas_call(
        body, grid=(S // tr,),
        in_specs=[pl.BlockSpec((tr, MEM_W), lambda i: (i, qblk)), pl.BlockSpec((MEM_LEN, 2 * MEM_W), lambda i: (0, 0))],
        out_specs=pl.BlockSpec((tr, MEM_W), lambda i: (i, 0)),
        out_shape=jax.ShapeDtypeStruct((S, MEM_W), BF16), name=name, compiler_params=_cp(("parallel",)))(p, kv)


def _memattn_bwd(p, qblk, kv, dcat, *, name="memattn_bwd", tr=512):
    S = p.shape[0]
    tr = min(tr, S)
    scale = MEM_HEAD_DIM ** -0.5

    def body(q_ref, kv_ref, do_ref, dq_ref, dkv_ref):
        @pl.when(pl.program_id(0) == 0)
        def _():
            dkv_ref[...] = jnp.zeros_like(dkv_ref)

        for h in range(MEM_HEADS):
            qh, pr = _mem_probs(q_ref, kv_ref, h)
            doh = do_ref[:, h * LANES:(h + 1) * LANES]
            dp = lax.dot_general(doh, kv_ref[:, MEM_W + h * LANES:MEM_W + (h + 1) * LANES], NT, preferred_element_type=F32)
            delta = jnp.sum(pr * dp, axis=-1, keepdims=True)
            dsb = (pr * (dp - delta) * scale).astype(BF16)
            dq = jnp.dot(dsb, kv_ref[:, h * LANES:(h + 1) * LANES], preferred_element_type=F32)
            dq_ref[:, h * LANES:(h + 1) * LANES] = dq.astype(BF16)
            dkv_ref[:, h * LANES:(h + 1) * LANES] += lax.dot_general(dsb, qh, TN, preferred_element_type=F32)
            dkv_ref[:, MEM_W + h * LANES:MEM_W + (h + 1) * LANES] += lax.dot_general(
                pr.astype(BF16), doh, TN, preferred_element_type=F32)

    return pl.pallas_call(
        body, grid=(S // tr,),
        in_specs=[pl.BlockSpec((tr, MEM_W), lambda i: (i, qblk)), pl.BlockSpec((MEM_LEN, 2 * MEM_W), lambda i: (0, 0)),
                  pl.BlockSpec((tr, MEM_W), lambda i: (i, Q_W // MEM_W))],
        out_specs=[pl.BlockSpec((tr, MEM_W), lambda i: (i, 0)), pl.BlockSpec((MEM_LEN, 2 * MEM_W), lambda i: (0, 0))],
        out_shape=[jax.ShapeDtypeStruct((S, MEM_W), BF16), jax.ShapeDtypeStruct((MEM_LEN, 2 * MEM_W), F32)],
        name=name, compiler_params=_cp(("arbitrary",)))(p, kv, dcat)


def _sigmoid(z):
    return 1.0 / (1.0 + jnp.exp(-z))


def _one_minus_exp(z, exp_z):
    poly = z * (1.0 + z * (0.5 + z * (1.0 / 6.0 + z * (1.0 / 24.0 + z * (1.0 / 120.0)))))
    return jnp.where(z > -0.1, -poly, 1.0 - exp_z)


def _softplus_neg(lam):
    z = -lam
    return jnp.maximum(z, 0.0) + jnp.log(1.0 + jnp.exp(-jnp.abs(z)))


_GELU_C = math.sqrt(2.0 / math.pi)


def _gelu(z):
    return 0.5 * z * (1.0 + jnp.tanh(_GELU_C * (z + 0.044715 * z * z * z)))


def _row_or_zero(ref, t, S):
    ok = jnp.logical_and(t >= 0, t < S)
    return jnp.where(ok, ref[pl.ds(jnp.clip(t, 0, S - 1), 1), :], 0.0)


def _shift_down(v, first):
    ri = lax.broadcasted_iota(jnp.int32, v.shape, 0)
    return jnp.where(ri == 0, first, pltpu.roll(v, 1, 0))


def _shift_up(v, last):
    T = v.shape[0]
    ri = lax.broadcasted_iota(jnp.int32, v.shape, 0)
    return jnp.where(ri == T - 1, last, pltpu.roll(v, T - 1, 0))


def _scan_chunk(a, u, reverse):
    T = a.shape[0]
    ri = lax.broadcasted_iota(jnp.int32, a.shape, 0)
    d = 1
    while d < T:
        if reverse:
            a_s, u_s, ok = pltpu.roll(a, T - d, 0), pltpu.roll(u, T - d, 0), ri < T - d
        else:
            a_s, u_s, ok = pltpu.roll(a, d, 0), pltpu.roll(u, d, 0), ri >= d
        u = jnp.where(ok, a * u_s + u, u)
        a = jnp.where(ok, a * a_s, a)
        d *= 2
    return a, u


def _conv_taps(xb_ref, t0, S):
    T = SCAN_ROWS
    x0 = xb_ref[pl.ds(t0, T), :]
    xm1 = _shift_down(x0, _row_or_zero(xb_ref, t0 - 1, S))
    nxt0 = _row_or_zero(xb_ref, t0 + T, S)
    xp1 = _shift_up(x0, nxt0)
    xp2 = _shift_up(xp1, _row_or_zero(xb_ref, t0 + T + 1, S))
    return xm1, x0, xp1, xp2


def _lru_gates(xc, w_a, b_a, w_x, b_x, sp):
    xcb = xc.astype(BF16)
    r = _sigmoid(jnp.dot(xcb, w_a, preferred_element_type=F32) + b_a)
    i = _sigmoid(jnp.dot(xcb, w_x, preferred_element_type=F32) + b_x)
    la = -LRU_C * r * sp
    a = jnp.exp(la)
    beta = jnp.sqrt(_one_minus_exp(2.0 * la, a * a))
    return r, i, a, beta


def _lru_specs(S):
    col = lambda off: pl.BlockSpec((S, LANES), lambda n: (0, n + off), pipeline_mode=pl.Buffered(1))
    small = lambda r: pl.BlockSpec((r, LANES), lambda n: (0, n))
    wblk = pl.BlockSpec((2, 1, LANES, LANES), lambda n: (0, n, 0, 0))
    return col, small, wblk


def _lru_fwd(p, conv_w, conv_b, wa, ba, wx, bx, lam, *, name="lru_fwd"):
    S = p.shape[0]
    T = SCAN_ROWS
    nc = S // T

    def body(xb_ref, gate_ref, cw_ref, cb_ref, wa_ref, ba_ref, wx_ref, bx_ref, lam_ref, y_ref, hf_ref, hr_ref, xc_v):
        sp = _softplus_neg(lam_ref[...])
        cw = cw_ref[...]

        def fwd_step(c, h_in):
            t0 = pl.multiple_of(c * T, T)
            xm1, x0, xp1, xp2 = _conv_taps(xb_ref, t0, S)
            xc = cb_ref[...] + xm1 * cw[0:1] + x0 * cw[1:2] + xp1 * cw[2:3] + xp2 * cw[3:4]
            xc_v[pl.ds(t0, T), :] = xc
            _, i, a, beta = _lru_gates(xc, wa_ref[0, 0], ba_ref[0:1], wx_ref[0, 0], bx_ref[0:1], sp[0:1])
            A, U = _scan_chunk(a, beta * (i * xc), False)
            hf_ref[pl.ds(t0, T), :] = A * h_in + U
            return hf_ref[pl.ds(t0 + T - 1, 1), :]

        lax.fori_loop(0, nc, fwd_step, jnp.zeros((1, LANES), F32))

        def rev_step(k, h_in):
            t0 = pl.multiple_of((nc - 1 - k) * T, T)
            xc = xc_v[pl.ds(t0, T), :]
            _, i, a, beta = _lru_gates(xc, wa_ref[1, 0], ba_ref[1:2], wx_ref[1, 0], bx_ref[1:2], sp[1:2])
            A, U = _scan_chunk(a, beta * (i * xc), True)
            h = A * h_in + U
            hr_ref[pl.ds(t0, T), :] = h
            y_ref[pl.ds(t0, T), :] = ((hf_ref[pl.ds(t0, T), :] + h) * _gelu(gate_ref[pl.ds(t0, T), :])).astype(BF16)
            return hr_ref[pl.ds(t0, 1), :]

        lax.fori_loop(0, nc, rev_step, jnp.zeros((1, LANES), F32))

    col, small, wblk = _lru_specs(S)
    colo = lambda: pl.BlockSpec((S, LANES), lambda n: (0, n))
    return pl.pallas_call(
        body, grid=(LRU_BLOCKS,),
        in_specs=[col(0), col(LRU_BLOCKS), small(4), small(1), wblk, small(2), wblk, small(2), small(2)],
        out_specs=[colo(), colo(), colo()],
        out_shape=[jax.ShapeDtypeStruct((S, D_MODEL), BF16), jax.ShapeDtypeStruct((S, D_MODEL), F32),
                   jax.ShapeDtypeStruct((S, D_MODEL), F32)],
        scratch_shapes=[pltpu.VMEM((S, LANES), F32)],
        name=name, compiler_params=_cp(("parallel",)))(p, p, conv_w, conv_b, wa, ba, wx, bx, lam)


def _lru_bwd(p, hf, hr, dcat, conv_w, conv_b, wa, ba, wx, bx, lam, *, name="lru_bwd"):
    S = p.shape[0]
    T = SCAN_ROWS
    nc = S // T

    def body(xb_ref, gate_ref, hf_ref, hr_ref, dy_ref, cw_ref, cb_ref, wa_ref, ba_ref, wx_ref, bx_ref, lam_ref,
             dxb_ref, dgate_ref, dcw_ref, dcb_ref, dwa_ref, dba_ref, dwx_ref, dbx_ref, dlam_ref, xc_v, dxc_v, dh_v):
        lam_v = lam_ref[...]
        sp = _softplus_neg(lam_v)
        cw = cw_ref[...]
        for ref in (dcw_ref, dcb_ref, dwa_ref, dba_ref, dwx_ref, dbx_ref, dlam_ref):
            ref[...] = jnp.zeros_like(ref)

        def prep_step(c, carry):
            t0 = pl.multiple_of(c * T, T)
            rows = pl.ds(t0, T)
            xm1, x0, xp1, xp2 = _conv_taps(xb_ref, t0, S)
            xc_v[rows, :] = cb_ref[...] + xm1 * cw[0:1] + x0 * cw[1:2] + xp1 * cw[2:3] + xp2 * cw[3:4]
            z = gate_ref[rows, :]
            dy = dy_ref[rows, :].astype(F32)
            th = jnp.tanh(_GELU_C * (z + 0.044715 * z * z * z))
            dgelu = 0.5 * (1.0 + th) + 0.5 * z * (1.0 - th * th) * _GELU_C * (1.0 + 3.0 * 0.044715 * z * z)
            dgate_ref[rows, :] = (dy * (hf_ref[rows, :] + hr_ref[rows, :]) * dgelu).astype(BF16)
            dh_v[rows, :] = dy * (0.5 * z * (1.0 + th))
            return carry

        lax.fori_loop(0, nc, prep_step, 0)

        def direction(d):
            h_ref = hf_ref if d == 0 else hr_ref
            w_a, w_x = wa_ref[d, 0], wx_ref[d, 0]
            b_a, b_x, sp_d = ba_ref[d:d + 1], bx_ref[d:d + 1], sp[d:d + 1]

            def step(k, carry):
                g_in, a_in = carry
                c = (nc - 1 - k) if d == 0 else k
                t0 = pl.multiple_of(c * T, T)
                rows = pl.ds(t0, T)
                xc = xc_v[rows, :]
                r, i, a, beta = _lru_gates(xc, w_a, b_a, w_x, b_x, sp_d)
                dh = dh_v[rows, :]
                hc = h_ref[rows, :]
                if d == 0:
                    A, U = _scan_chunk(_shift_up(a, a_in), dh, True)
                    g = A * g_in + U
                    h_nb = _shift_down(hc, _row_or_zero(h_ref, t0 - 1, S))
                    nxt = (g[0:1], a[0:1])
                else:
                    A, U = _scan_chunk(_shift_down(a, a_in), dh, False)
                    g = A * g_in + U
                    h_nb = _shift_up(hc, _row_or_zero(h_ref, t0 + T, S))
                    nxt = (g[T - 1:T], a[T - 1:T])
                da = g * h_nb
                dbeta = g * (i * xc)
                tb = g * beta
                dla = da * a - dbeta * (a * a / beta)
                dzr = (dla * (-LRU_C * sp_d)) * (r * (1.0 - r))
                dzi = (tb * xc) * (i * (1.0 - i))
                dzrb, dzib, xcb = dzr.astype(BF16), dzi.astype(BF16), xc.astype(BF16)
                dwa_ref[d, 0] += lax.dot_general(xcb, dzrb, TN, preferred_element_type=F32)
                dwx_ref[d, 0] += lax.dot_general(xcb, dzib, TN, preferred_element_type=F32)
                dba_ref[d:d + 1] += jnp.sum(dzr, axis=0, keepdims=True)
                dbx_ref[d:d + 1] += jnp.sum(dzi, axis=0, keepdims=True)
                dlam_ref[d:d + 1] += jnp.sum(dla * (-LRU_C * r), axis=0, keepdims=True)
                dxc = (tb * i + lax.dot_general(dzrb, w_a, NT, preferred_element_type=F32)
                       + lax.dot_general(dzib, w_x, NT, preferred_element_type=F32))
                if d == 0:
                    dxc_v[rows, :] = dxc
                else:
                    dxc_v[rows, :] += dxc
                return nxt

            lax.fori_loop(0, nc, step, (jnp.zeros((1, LANES), F32), jnp.zeros((1, LANES), F32)))

        direction(0)
        direction(1)
        dlam_ref[...] = dlam_ref[...] * (-1.0 / (1.0 + jnp.exp(lam_v)))

        def conv_step(c, carry):
            t0 = pl.multiple_of(c * T, T)
            rows = pl.ds(t0, T)
            g0 = dxc_v[rows, :]
            gm1 = _shift_down(g0, _row_or_zero(dxc_v, t0 - 1, S))
            gm2 = _shift_down(gm1, _row_or_zero(dxc_v, t0 - 2, S))
            gp1 = _shift_up(g0, _row_or_zero(dxc_v, t0 + T, S))
            dxb_ref[rows, :] = (cw[0:1] * gp1 + cw[1:2] * g0 + cw[2:3] * gm1 + cw[3:4] * gm2).astype(BF16)
            xm1, x0, xp1, xp2 = _conv_taps(xb_ref, t0, S)
            for tap, xs in enumerate((xm1, x0, xp1, xp2)):
                dcw_ref[tap:tap + 1] += jnp.sum(g0 * xs, axis=0, keepdims=True)
            dcb_ref[...] += jnp.sum(g0, axis=0, keepdims=True)
            return carry

        lax.fori_loop(0, nc, conv_step, 0)

    col, small, wblk = _lru_specs(S)
    colo = lambda: pl.BlockSpec((S, LANES), lambda n: (0, n), pipeline_mode=pl.Buffered(1))
    return pl.pallas_call(
        body, grid=(LRU_BLOCKS,),
        in_specs=[col(0), col(LRU_BLOCKS), col(0), col(0), col(0), small(4), small(1), wblk, small(2), wblk, small(2), small(2)],
        out_specs=[colo(), colo(), small(4), small(1), wblk, small(2), wblk, small(2), small(2)],
        out_shape=[jax.ShapeDtypeStruct((S, D_MODEL), BF16), jax.ShapeDtypeStruct((S, D_MODEL), BF16),
                   jax.ShapeDtypeStruct((4, D_MODEL), F32), jax.ShapeDtypeStruct((1, D_MODEL), F32),
                   jax.ShapeDtypeStruct((2, LRU_BLOCKS, LANES, LANES), F32), jax.ShapeDtypeStruct((2, D_MODEL), F32),
                   jax.ShapeDtypeStruct((2, LRU_BLOCKS, LANES, LANES), F32), jax.ShapeDtypeStruct((2, D_MODEL), F32),
                   jax.ShapeDtypeStruct((2, D_MODEL), F32)],
        scratch_shapes=[pltpu.VMEM((S, LANES), F32), pltpu.VMEM((S, LANES), F32), pltpu.VMEM((S, LANES), F32)],
        name=name, compiler_params=_cp(("parallel",)))(p, p, hf, hr, dcat, conv_w, conv_b, wa, ba, wx, bx, lam)


def _mlp_fwd(x, w_up, w_down, gain, l):
    up, act, h = _mm_nn(x, w_up, norm_g=gain, relu2=True, name=f"mlp_up{l}")
    return _mm_nn(act, w_down, resid=x, name=f"mlp_down{l}"), (up, act, h)


def _mlp_bwd(x, dx, dxb, saved, w_up, w_down, gain, l):
    up, act, h = saved
    g_down = _mm_tn(act, dxb, 1, name=f"dw_down{l}")
    dup = _mm_nt(dxb, w_down, up=up, name=f"d_up{l}")
    g_up = _mm_tn(h, dup, N_CHIPS, name=f"dw_up{l}")
    dx, dxb, g_gain = _mm_nt(dup, w_up, norm_x=x, norm_g=gain, dres=dx, name=f"d_mlp_in{l}")
    return dx, dxb, g_down, g_up, g_gain


def _local_step(x, mem, positions, target, W, pending=None):
    cos_t, sin_t = _rope_tables(positions)
    sinks = W["attn_sinks"].reshape(ATTN_HEADS)
    G = {}

    def hosting(late, fn, *args, **kw):
        if pending is None:
            return fn(*args, **kw)
        *res, buf = fn(*args, gather=pending[late], **kw)
        if late.startswith("w_down"):
            W.setdefault("w_down", [None] * DEPTH)[int(late[-1])] = _ready(late, buf)
        else:
            W[late] = _ready(late, buf)
        return res if len(res) > 1 else res[0]

    kv0, memn = _mm_nn(mem, W["w_mem_kv"][0], norm_g=W["mem_norm"], out_dtype=BF16, name="mem_kv0", tm=256)
    kv1 = _mm_nn(memn, W["w_mem_kv"][1], out_dtype=BF16, name="mem_kv1", tm=256)
    p0, h0 = hosting("w_out", _mm_nn, x, W["attn_w_in"], norm_g=W["mix_norm"][0], name="attn_in")
    q, kd, vd, va = hosting("lru_w_in", _qk_prep, p0, cos_t, sin_t)
    ao, lse = hosting("w_up", _attn_fwd, q, kd, va, sinks)
    mo0 = _memattn_fwd(p0, Q_W // MEM_W + 1, kv0, name="memattn_fwd0")
    cat0 = jnp.concatenate([ao, mo0], axis=1)
    x1 = hosting("w_down0", _mm_nn, cat0, W["w_out"][0], resid=x, name="mix_out0")
    up0, act0, h1 = hosting("w_down1", _mm_nn, x1, W["w_up"][0], norm_g=W["mlp_norm"][0], relu2=True, name="mlp_up0")
    x2, mlp0 = _mm_nn(act0, W["w_down"][0], resid=x1, name="mlp_down0"), (up0, act0, h1)
    p1, h2 = _mm_nn(x2, W["lru_w_in"], norm_g=W["mix_norm"][1], name="lru_in")
    lru_w = (W["lru_conv_w"], W["lru_conv_b"], W["lru_wa"], W["lru_ba"], W["lru_wx"], W["lru_bx"], W["lru_lambda"])
    y, hf, hr = _lru_fwd(p1, *lru_w)
    mo1 = _memattn_fwd(p1, 2 * D_MODEL // MEM_W, kv1, name="memattn_fwd1")
    cat1 = jnp.concatenate([y, mo1], axis=1)
    x3 = _mm_nn(cat1, W["w_out"][1], resid=x2, name="mix_out1")
    x4, mlp1 = _mlp_fwd(x3, W["w_up"][1], W["w_down"][1], W["mlp_norm"][1], 1)
    loss, dx, dxb, G["final_norm"] = _final(x4, W["final_norm"], target)

    dx, dxb, gd1, gu1, gm1 = _mlp_bwd(x3, dx, dxb, mlp1, W["w_up"][1], W["w_down"][1], W["mlp_norm"][1], 1)
    go1 = _mm_tn(cat1, dxb, 1, name="dw_out1")
    dcat1 = _mm_nt(dxb, W["w_out"][1], name="d_mix1")
    dmq1, dkv1 = _memattn_bwd(p1, 2 * D_MODEL // MEM_W, kv1, dcat1, name="memattn_bwd1")
    (dxb1, dgate, G["lru_conv_w"], G["lru_conv_b"], G["lru_wa"], G["lru_ba"], G["lru_wx"], G["lru_bx"],
     G["lru_lambda"]) = _lru_bwd(p1, hf, hr, dcat1, *lru_w)
    dp1 = jnp.concatenate([dxb1, dgate, dmq1], axis=1)
    G["lru_w_in"] = _mm_tn(h2, dp1, N_CHIPS, name="dw_lru_in")
    dx, dxb, gx1 = _mm_nt(dp1, W["lru_w_in"], norm_x=x2, norm_g=W["mix_norm"][1], dres=dx, name="d_lru_in")

    dx, dxb, gd0, gu0, gm0 = _mlp_bwd(x1, dx, dxb, mlp0, W["w_up"][0], W["w_down"][0], W["mlp_norm"][0], 0)
    go0 = _mm_tn(cat0, dxb, 1, name="dw_out0")
    dcat0 = _mm_nt(dxb, W["w_out"][0], name="d_mix0")
    dmq0, dkv0 = _memattn_bwd(p0, Q_W // MEM_W + 1, kv0, dcat0, name="memattn_bwd0")
    dq, dk, dv, dsink = _attn_bwd(q, kd, vd, ao, lse, sinks, dcat0)
    dp0 = _qk_prep_bwd(dq, dk, dv, dmq0, cos_t, sin_t)
    G["attn_w_in"] = _mm_tn(h0, dp0, N_CHIPS, name="dw_attn_in")
    dx, _, gx0 = _mm_nt(dp0, W["attn_w_in"], norm_x=x, norm_g=W["mix_norm"][0], dres=dx, name="d_attn_in")

    dkv0b, dkv1b = dkv0.astype(BF16), dkv1.astype(BF16)
    gk0 = _mm_tn(memn, dkv0b, 1, name="dw_kv0", tm=256)
    gk1 = _mm_tn(memn, dkv1b, 1, name="dw_kv1", tm=256)
    w_kv_both = jnp.concatenate([W["w_mem_kv"][0], W["w_mem_kv"][1]], axis=0)
    _, _, G["mem_norm"] = _mm_nt(jnp.concatenate([dkv0b, dkv1b], axis=1), w_kv_both, norm_x=mem, norm_g=W["mem_norm"],
                                 name="d_mem", tm=256)

    G["w_mem_kv"] = (gk0, gk1)
    G["w_out"] = (go0, go1)
    G["w_up"] = (gu0, gu1)
    G["w_down"] = (gd0, gd1)
    G["mix_norm"] = jnp.concatenate([gx0, gx1], axis=0)
    G["mlp_norm"] = jnp.concatenate([gm0, gm1], axis=0)
    G["attn_sinks"] = dsink[0:1, 0:ATTN_HEADS]
    return loss[0, 0], dx, G


def _comm_call(body, out_shape, n_sems, name, *args, alias=None):
    return pl.pallas_call(
        body, out_shape=out_shape, in_specs=[HBM] * len(args), out_specs=HBM,
        scratch_shapes=[pltpu.SemaphoreType.DMA((n_sems,)), pltpu.SemaphoreType.DMA((n_sems,))],
        input_output_aliases=alias or {}, name=name)(*args)


def _place_slot(shard, slot, n_slots, *, name, tr):
    R, C = shard.shape

    def body(s_ref, a_ref, o_ref):
        o_ref[0] = a_ref[...]

    return pl.pallas_call(
        body,
        grid_spec=pltpu.PrefetchScalarGridSpec(
            num_scalar_prefetch=1, grid=(R // tr,), in_specs=[pl.BlockSpec((tr, C), lambda i, s_ref: (i, 0))],
            out_specs=pl.BlockSpec((1, tr, C), lambda i, s_ref: (s_ref[0], i, 0))),
        out_shape=jax.ShapeDtypeStruct((n_slots, R, C), shard.dtype), name=name,
        compiler_params=_cp(("parallel",)))(slot, shard)


def _allgather_chips(buf, *, name, forward_to_sibling):
    def body(b_ref, o_ref, send_sems, recv_sems):
        if forward_to_sibling:
            _gather_start(o_ref, send_sems, recv_sems)
            _gather_finish(o_ref, send_sems, recv_sems)
            return
        x, y, c, chips = _place()
        own = o_ref.at[2 * x + y]
        sends = [_remote(own, own, send_sems, recv_sems, j, (cx, cy, c)) for j, (cx, cy) in enumerate(chips)]
        for cp in sends:
            cp.start()
        for j, (cx, cy) in enumerate(chips):
            landed = o_ref.at[2 * cx + cy]
            _remote(landed, landed, send_sems, recv_sems, j, (cx, cy, c)).wait_recv()
        for cp in sends:
            cp.wait_send()

    return _comm_call(body, jax.ShapeDtypeStruct(buf.shape, buf.dtype), GATHER_SEMS, name, buf, alias={0: 0})


def _sibling_exchange(g, *, name):
    _, R, C = g.shape
    half = R // 2

    def body(g_ref, o_ref, send_sems, recv_sems):
        x, y, c, _ = _place()
        other = pl.ds(pl.multiple_of((1 - c) * half, 8), half)
        cps = [_remote(g_ref.at[s, other], o_ref.at[s], send_sems, recv_sems, s, (x, y, 1 - c)) for s in range(N_CHIPS)]
        for cp in cps:
            cp.start()
        for cp in cps:
            cp.wait()

    return _comm_call(body, jax.ShapeDtypeStruct((N_CHIPS, half, C), g.dtype), N_CHIPS, name, g)


def _chip_exchange(h, parts, *, name):
    def body(h_ref, p_ref, o_ref, send_sems, recv_sems):
        x, y, c, chips = _place()
        me = 2 * x + y
        cps = [_remote(h_ref.at[2 * cx + cy], o_ref.at[me], send_sems, recv_sems, j, (cx, cy, c))
               for j, (cx, cy) in enumerate(chips)]
        for cp in cps:
            cp.start()
        for j, (cx, cy) in enumerate(chips):
            got = o_ref.at[2 * cx + cy]
            _remote(got, got, send_sems, recv_sems, j, (cx, cy, c)).wait_recv()
        for cp in cps:
            cp.wait_send()

    return _comm_call(body, jax.ShapeDtypeStruct(parts.shape, parts.dtype), 3, name, h, parts, alias={1: 0})


def _sibling_allgather(full, *, name):
    R, C = full.shape
    half = R // 2

    def body(f_ref, o_ref, send_sems, recv_sems):
        x, y, c, _ = _place()
        mine = o_ref.at[pl.ds(pl.multiple_of(c * half, 8), half)]
        cp = _remote(mine, mine, send_sems, recv_sems, 0, (x, y, 1 - c))
        cp.start()
        got = o_ref.at[pl.ds(pl.multiple_of((1 - c) * half, 8), half)]
        _remote(got, got, send_sems, recv_sems, 0, (x, y, 1 - c)).wait_recv()
        cp.wait_send()

    return _comm_call(body, jax.ShapeDtypeStruct(full.shape, full.dtype), 1, name, full, alias={0: 0})


def _sum_halves(g, recv, place, *, name="sum_halves", tr=480):
    _, R, C = g.shape
    half = R // 2
    nblk = half // tr

    def body(pl_ref, g_ref, r_ref, o_ref, own_ref):
        v = (g_ref[...] + r_ref[...]).astype(BF16)
        o_ref[...] = v

        @pl.when(pl.program_id(1) == pl_ref[1])
        def _():
            own_ref[...] = v

    blk = pl.BlockSpec((1, tr, C), lambda i, s, p: (s, i, 0))
    return pl.pallas_call(
        body,
        grid_spec=pltpu.PrefetchScalarGridSpec(
            num_scalar_prefetch=1, grid=(nblk, N_CHIPS),
            in_specs=[pl.BlockSpec((1, tr, C), lambda i, s, p: (s, p[0] * nblk + i, 0)), blk],
            out_specs=[blk, pl.BlockSpec((1, tr, C), lambda i, s, p: (p[1], i, 0))]),
        out_shape=[jax.ShapeDtypeStruct((N_CHIPS, half, C), BF16)] * 2, name=name,
        compiler_params=_cp(("parallel", "arbitrary")))(place, g, recv)


def _sum_chips(parts, place, *, name="sum_chips", tr=480):
    _, R, C = parts.shape
    nblk = R // tr

    def body(pl_ref, p_ref, o_ref):
        acc = p_ref[0].astype(F32) + p_ref[1].astype(F32)
        o_ref[...] = (acc + p_ref[2].astype(F32)) + p_ref[3].astype(F32)

    return pl.pallas_call(
        body,
        grid_spec=pltpu.PrefetchScalarGridSpec(
            num_scalar_prefetch=1, grid=(nblk,), in_specs=[pl.BlockSpec((N_CHIPS, tr, C), lambda i, p: (0, i, 0))],
            out_specs=pl.BlockSpec((tr, C), lambda i, p: (p[0] * nblk + i, 0))),
        out_shape=jax.ShapeDtypeStruct((2 * R, C), F32), name=name, compiler_params=_cp(("parallel",)))(place, parts)


def _adamw(w, g, m, v, *, name, tr=128):
    R, C = w.shape
    bc1 = 1.0 - ADAM_B1 ** ADAM_STEP
    bc2 = 1.0 - ADAM_B2 ** ADAM_STEP

    def body(w_ref, g_ref, m_ref, v_ref, d_ref, nm_ref, nv_ref):
        gv = g_ref[...]
        nm = ADAM_B1 * m_ref[...] + (1.0 - ADAM_B1) * gv
        nv = ADAM_B2 * v_ref[...] + (1.0 - ADAM_B2) * (gv * gv)
        d_ref[...] = -ADAM_LR * ((nm / bc1) / (jnp.sqrt(nv / bc2) + ADAM_EPS) + ADAM_WD * w_ref[...])
        nm_ref[...] = nm
        nv_ref[...] = nv

    blk = pl.BlockSpec((tr, C), lambda i: (i, 0))
    return pl.pallas_call(
        body, grid=(R // tr,), in_specs=[blk] * 4, out_specs=[blk] * 3,
        out_shape=[jax.ShapeDtypeStruct((R, C), F32)] * 3, name=name, compiler_params=_cp(("parallel",)))(w, g, m, v)


ROW = 1024
BIG = ("w_mem_kv", "w_out", "w_up", "w_down", "attn_w_in", "lru_w_in")
SMALL_SHARDED = ("lru_conv_w", "lru_conv_b", "lru_ba", "lru_bx", "lru_lambda")
REPLICATED = ("mix_norm", "mlp_norm", "mem_norm", "final_norm", "attn_sinks", "lru_wa", "lru_wx")
SMALL = REPLICATED + SMALL_SHARDED
WEIGHTS = ("mix_norm", "mlp_norm", "mem_norm", "final_norm", "w_mem_kv", "w_out", "w_up", "w_down", "attn_w_in",
           "attn_sinks", "lru_w_in", "lru_conv_w", "lru_conv_b", "lru_wa", "lru_ba", "lru_wx", "lru_bx", "lru_lambda")
SMALL_W_ROWS = 32
SMALL_G_ROWS = 192
ADAM_SMALL_ROWS = 640


def _rows(a):
    return a.reshape(-1, ROW)


def _flat_pad(parts, total):
    flat = jnp.concatenate([p.reshape(-1) for p in parts])
    return jnp.pad(flat, (0, total - flat.shape[0]))


def _pad_rows(a):
    flat = a.reshape(-1)
    n = -(-flat.shape[0] // ROW) * ROW
    return jnp.pad(flat, (0, n - flat.shape[0])).reshape(-1, ROW)


LATE = ("w_out", "lru_w_in", "w_up", "w_down0", "w_down1")


def _ready(name, full):
    if name == "w_out":
        wo = full.reshape(N_CHIPS, DEPTH, -1, D_MODEL)
        return [wo[:, l].reshape(1, MIX_OUT_W, D_MODEL) for l in range(DEPTH)]
    if name == "w_up":
        wu = full.reshape(N_CHIPS, DEPTH, D_MODEL, D_FF // N_CHIPS)
        return [wu[:, l] for l in range(DEPTH)]
    if name == "lru_w_in":
        return full.reshape(N_CHIPS, D_MODEL, LRU_IN_W // N_CHIPS)
    return full.reshape(1, D_FF, D_MODEL)


def _gather_weights(P, chip1):
    bf = lambda a: _rows(a.astype(BF16))
    small = _flat_pad([P[n] for n in SMALL_SHARDED], SMALL_W_ROWS * ROW // 2)
    small_bits = lax.bitcast_convert_type(small, BF16).reshape(SMALL_W_ROWS, ROW)
    early = jnp.concatenate([bf(P["attn_w_in"]), bf(P["w_mem_kv"]), small_bits], axis=0)
    n_in, n_kv = P["attn_w_in"].size // ROW, P["w_mem_kv"].size // ROW
    placed = _place_slot(early, chip1, N_CHIPS, name="place_weights", tr=early.shape[0] // 2)
    full = _allgather_chips(placed, name="allgather_weights", forward_to_sibling=True)
    late = {"w_out": bf(P["w_out"]), "lru_w_in": bf(P["lru_w_in"]), "w_up": bf(P["w_up"]),
            "w_down0": bf(P["w_down"][0]), "w_down1": bf(P["w_down"][1])}
    pending = {n: _place_slot(late[n], chip1, N_CHIPS, name=f"place_{n}", tr=late[n].shape[0] // 2) for n in LATE}
    W = {n: P[n] for n in REPLICATED}
    W["attn_w_in"] = full[:, :n_in].reshape(N_CHIPS, D_MODEL, ATTN_IN_W // N_CHIPS)
    kv = full[:, n_in:n_in + n_kv].reshape(N_CHIPS, DEPTH, -1, D_MODEL)
    W["w_mem_kv"] = [kv[:, l].reshape(1, D_MODEL, D_MODEL) for l in range(DEPTH)]
    sm = lax.bitcast_convert_type(full[:, n_in + n_kv:].reshape(N_CHIPS, -1, 2), F32)
    o = 0
    for n in SMALL_SHARDED:
        shp = P[n].shape[1:]
        cnt = math.prod(shp)
        piece = sm[:, o:o + cnt].reshape((N_CHIPS,) + shp)
        piece = jnp.moveaxis(piece, 0, -2)
        W[n] = piece.reshape(shp[:-1] + (N_CHIPS * shp[-1],)).reshape(-1, D_MODEL)
        o += cnt
    W["lru_wa"] = P["lru_wa"][0].astype(BF16)
    W["lru_wx"] = P["lru_wx"][0].astype(BF16)
    return W, pending


def _small_grad_list(G):
    return [G["mix_norm"], G["mlp_norm"], G["mem_norm"], G["final_norm"], jnp.pad(G["attn_sinks"].reshape(-1), (0, ROW - ATTN_HEADS)),
            G["lru_wa"], G["lru_wx"], G["lru_conv_w"], G["lru_conv_b"], G["lru_ba"], G["lru_bx"], G["lru_lambda"]]


SMALL_G_SIZES = (2 * D_MODEL, 2 * D_MODEL, D_MODEL, D_MODEL, ROW, 2 * 8 * 128 * 128, 2 * 8 * 128 * 128,
                 4 * D_MODEL, D_MODEL, 2 * D_MODEL, 2 * D_MODEL, 2 * D_MODEL)


def _reduce_grads(G, place, chip1):
    c4 = lambda a, r: a.reshape(N_CHIPS, r, ROW)
    pieces = [
        c4(G["w_mem_kv"][0], 256), c4(G["w_mem_kv"][1], 256), c4(G["w_out"][0], 384), c4(G["w_out"][1], 384),
        G["w_up"][0], G["w_up"][1], c4(G["w_down"][0], 1024), c4(G["w_down"][1], 1024),
        c4(G["attn_w_in"], 512), c4(G["lru_w_in"], 640),
        _flat_pad(_small_grad_list(G), N_CHIPS * SMALL_G_ROWS * ROW).reshape(N_CHIPS, SMALL_G_ROWS, ROW),
    ]
    packed = jnp.concatenate(pieces, axis=1)
    recv = _sibling_exchange(packed, name="grad_sibling_exchange")
    halves, landing = _sum_halves(packed, recv, place)
    parts = _chip_exchange(halves, landing, name="grad_chip_exchange")
    full = _sibling_allgather(_sum_chips(parts, place), name="grad_sibling_allgather")
    n_big = full.shape[0] - SMALL_G_ROWS
    small_placed = _place_slot(full[n_big:], chip1, N_CHIPS, name="place_small_grads", tr=SMALL_G_ROWS)
    small_all = _allgather_chips(small_placed, name="allgather_small_grads", forward_to_sibling=False)
    flat = small_all.reshape(-1)
    small = {}
    o = 0
    names = ("mix_norm", "mlp_norm", "mem_norm", "final_norm", "attn_sinks", "lru_wa", "lru_wx",
             "lru_conv_w", "lru_conv_b", "lru_ba", "lru_bx", "lru_lambda")
    for n, cnt in zip(names, SMALL_G_SIZES):
        small[n] = flat[o:o + cnt]
        o += cnt
    return full[:n_big], small


def kernel(x, mem, positions, mix_norm, mlp_norm, mem_norm, final_norm, w_mem_kv, w_out, w_up, w_down, attn_w_in, attn_sinks, lru_w_in, lru_conv_w, lru_conv_b, lru_wa, lru_ba, lru_wx, lru_bx, lru_lambda, loss_target, m_mix_norm, m_mlp_norm, m_mem_norm, m_final_norm, m_w_mem_kv, m_w_out, m_w_up, m_w_down, m_attn_w_in, m_attn_sinks, m_lru_w_in, m_lru_conv_w, m_lru_conv_b, m_lru_wa, m_lru_ba, m_lru_wx, m_lru_bx, m_lru_lambda, v_mix_norm, v_mlp_norm, v_mem_norm, v_final_norm, v_w_mem_kv, v_w_out, v_w_up, v_w_down, v_attn_w_in, v_attn_sinks, v_lru_w_in, v_lru_conv_w, v_lru_conv_b, v_lru_wa, v_lru_ba, v_lru_wx, v_lru_bx, v_lru_lambda):
    P = dict(mix_norm=mix_norm, mlp_norm=mlp_norm, mem_norm=mem_norm, final_norm=final_norm, w_mem_kv=w_mem_kv, w_out=w_out,
             w_up=w_up, w_down=w_down, attn_w_in=attn_w_in, attn_sinks=attn_sinks, lru_w_in=lru_w_in, lru_conv_w=lru_conv_w,
             lru_conv_b=lru_conv_b, lru_wa=lru_wa, lru_ba=lru_ba, lru_wx=lru_wx, lru_bx=lru_bx, lru_lambda=lru_lambda)
    M1 = dict(mix_norm=m_mix_norm, mlp_norm=m_mlp_norm, mem_norm=m_mem_norm, final_norm=m_final_norm, w_mem_kv=m_w_mem_kv,
              w_out=m_w_out, w_up=m_w_up, w_down=m_w_down, attn_w_in=m_attn_w_in, attn_sinks=m_attn_sinks, lru_w_in=m_lru_w_in,
              lru_conv_w=m_lru_conv_w, lru_conv_b=m_lru_conv_b, lru_wa=m_lru_wa, lru_ba=m_lru_ba, lru_wx=m_lru_wx,
              lru_bx=m_lru_bx, lru_lambda=m_lru_lambda)
    V2 = dict(mix_norm=v_mix_norm, mlp_norm=v_mlp_norm, mem_norm=v_mem_norm, final_norm=v_final_norm, w_mem_kv=v_w_mem_kv,
              w_out=v_w_out, w_up=v_w_up, w_down=v_w_down, attn_w_in=v_attn_w_in, attn_sinks=v_attn_sinks, lru_w_in=v_lru_w_in,
              lru_conv_w=v_lru_conv_w, lru_conv_b=v_lru_conv_b, lru_wa=v_lru_wa, lru_ba=v_lru_ba, lru_wx=v_lru_wx,
              lru_bx=v_lru_bx, lru_lambda=v_lru_lambda)
    chip = 2 * lax.axis_index("x") + lax.axis_index("y")
    chip1 = chip.astype(jnp.int32).reshape(1)
    place = jnp.stack([lax.axis_index("c").astype(jnp.int32), chip.astype(jnp.int32)])

    W, pending = _gather_weights(P, chip1)
    loss, dx, G = _local_step(x[0], mem[0], positions[0], loss_target[0], W, pending)
    loss = lax.psum(loss, ("x", "y", "c"))
    big_rows, small = _reduce_grads(G, place, chip1)

    grads, deltas, new_m, new_v = {}, {}, {}, {}
    off = 0
    for n in BIG:
        r = math.prod(P[n].shape) // ROW
        g = big_rows[off:off + r]
        off += r
        d, nm, nv = _adamw(_rows(P[n]), g, _rows(M1[n]), _rows(V2[n]), name=f"adamw_{n}")
        grads[n], deltas[n], new_m[n], new_v[n] = (t.reshape(P[n].shape) for t in (g, d, nm, nv))

    for n in SMALL:
        g = small[n]
        if n in SMALL_SHARDED:
            shard = P[n].shape[-1]
            g = lax.dynamic_slice_in_dim(g.reshape(-1, N_CHIPS * shard), chip * shard, shard, axis=1)
        elif n == "attn_sinks":
            g = g[:ATTN_HEADS]
        grads[n] = g.reshape(P[n].shape)
    packs = []
    for src in (P, grads, M1, V2):
        a = jnp.concatenate([_pad_rows(src[n]) for n in SMALL], axis=0)
        packs.append(jnp.pad(a, ((0, ADAM_SMALL_ROWS - a.shape[0]), (0, 0))))
    d_s, nm_s, nv_s = _adamw(*packs, name="adamw_small")
    o = 0
    for n in SMALL:
        cnt = math.prod(P[n].shape)
        r = -(-cnt // ROW)
        for dst, src in ((deltas, d_s), (new_m, nm_s), (new_v, nv_s)):
            dst[n] = src[o:o + r].reshape(-1)[:cnt].reshape(P[n].shape)
        o += r

    return (loss, dx[None], *[grads[n] for n in WEIGHTS], *[deltas[n] for n in WEIGHTS],
            *[new_m[n] for n in WEIGHTS], *[new_v[n] for n in WEIGHTS])
```

```python
import functools
import math

import jax
import jax.numpy as jnp
from jax import lax
from jax.experimental import pallas as pl
from jax.experimental.pallas import tpu as pltpu

F32 = jnp.float32
BF16 = jnp.bfloat16
MESH = pl.DeviceIdType.MESH

D_MODEL = 1024
DEPTH = 2
EPS = 1e-6
ATTN_HEADS = 16
ATTN_KV_HEADS = 4
HEAD_DIM = 64
WINDOW = 128
BLOCK = 128
ROPE_THETA = 500000.0
ROPE_DIM = 16
Q_W = 1024
KV_W = 256
MEM_LEN = 256
MEM_HEADS = 4
MEM_HEAD_DIM = 128
MEM_W = 512
LRU_BLOCKS = 8
LRU_C = 8.0
ATTN_IN_W = 2048
LRU_IN_W = 2560
MIX_OUT_W = 1536
D_FF = 4096
NEG = -1e30
N_CHIPS = 4

ADAM_LR = 0.001
ADAM_B1 = 0.9
ADAM_B2 = 0.999
ADAM_EPS = 1e-08
ADAM_WD = 0.01
ADAM_STEP = 10

LANES = 128
SCAN_ROWS = 512
VMEM_LIMIT = 56 * 1024 * 1024

NT = (((1,), (1,)), ((), ()))
TN = (((0,), (0,)), ((), ()))


def _cp(sem=None):
    return pltpu.CompilerParams(dimension_semantics=sem, vmem_limit_bytes=VMEM_LIMIT)


HBM = pl.BlockSpec(memory_space=pl.ANY)
GATHER_SEMS = 6


def _place():
    x, y, c = lax.axis_index("x"), lax.axis_index("y"), lax.axis_index("c")
    chips = [(1 - x, y), (x, 1 - y), (1 - x, 1 - y)]
    return x, y, c, chips


def _remote(src, dst, send_sems, recv_sems, k, to):
    return pltpu.make_async_remote_copy(src_ref=src, dst_ref=dst, send_sem=send_sems.at[k], recv_sem=recv_sems.at[k],
                                        device_id=to, device_id_type=MESH)


def _gather_start(o_ref, send_sems, recv_sems):
    x, y, c, chips = _place()
    half = o_ref.shape[1] // 2
    own = o_ref.at[2 * x + y, pl.ds(pl.multiple_of(c * half, 16), half)]
    for j, (cx, cy) in enumerate(chips):
        _remote(own, own, send_sems, recv_sems, j, (cx, cy, c)).start()


def _gather_finish(o_ref, send_sems, recv_sems):
    x, y, c, chips = _place()
    half = o_ref.shape[1] // 2
    my_rows = pl.ds(pl.multiple_of(c * half, 16), half)
    sib_rows = pl.ds(pl.multiple_of((1 - c) * half, 16), half)
    own = o_ref.at[2 * x + y, my_rows]
    passed = []
    for j, (cx, cy) in enumerate(chips):
        landed = o_ref.at[2 * cx + cy, my_rows]
        _remote(landed, landed, send_sems, recv_sems, j, (cx, cy, c)).wait_recv()
        fw = _remote(landed, landed, send_sems, recv_sems, 3 + j, (x, y, 1 - c))
        fw.start()
        passed.append(fw)
    for j, (cx, cy) in enumerate(chips):
        got = o_ref.at[2 * cx + cy, sib_rows]
        _remote(got, got, send_sems, recv_sems, 3 + j, (x, y, 1 - c)).wait_recv()
    for j, (cx, cy) in enumerate(chips):
        _remote(own, own, send_sems, recv_sems, j, (cx, cy, c)).wait_send()
    for fw in passed:
        fw.wait_send()


def _exchange_start(h_ref, o_ref, send_sems, recv_sems):
    x, y, c, chips = _place()
    for j, (cx, cy) in enumerate(chips):
        _remote(h_ref.at[2 * cx + cy], o_ref.at[2 * x + y], send_sems, recv_sems, j, (cx, cy, c)).start()


def _exchange_finish(h_ref, o_ref, send_sems, recv_sems):
    x, y, c, chips = _place()
    for j, (cx, cy) in enumerate(chips):
        got = o_ref.at[2 * cx + cy]
        _remote(got, got, send_sems, recv_sems, j, (cx, cy, c)).wait_recv()
    for j, (cx, cy) in enumerate(chips):
        _remote(h_ref.at[2 * cx + cy], o_ref.at[2 * x + y], send_sems, recv_sems, j, (cx, cy, c)).wait_send()


class _Rider:
    def __init__(self, args, start, finish):
        self.args, self.start, self.finish = list(args), start, finish


def _gather_rider(buf):
    return None if buf is None else _Rider([buf], _gather_start, _gather_finish)


def _exchange_rider(h, landing):
    return _Rider([h, landing], _exchange_start, _exchange_finish)


class _Hosted:
    def __init__(self, rider, n_in, n_out):
        self.rider = rider
        self.on = rider is not None
        self.args = rider.args if self.on else []
        k = len(self.args)
        self.alias = {n_in + k - 1: n_out} if self.on else {}
        self.in_specs = [HBM] * k
        self.out_specs = [HBM] if self.on else []
        self.out_shape = [jax.ShapeDtypeStruct(self.args[-1].shape, self.args[-1].dtype)] if self.on else []
        self.scratch = [pltpu.SemaphoreType.DMA((GATHER_SEMS,)), pltpu.SemaphoreType.DMA((GATHER_SEMS,))] if self.on else []

    def split(self, refs, n_in, n_out):
        refs = list(refs)
        if not self.on:
            return refs[:n_in], refs[n_in:n_in + n_out], refs[n_in + n_out:], None
        k = len(self.args)
        ins, outs = refs[:n_in], refs[n_in + k:n_in + k + n_out]
        rest = refs[n_in + k + n_out + 1:]
        rrefs = refs[n_in:n_in + k - 1] + [refs[n_in + k + n_out], rest[-2], rest[-1]]
        return ins, outs, rest[:-2], rrefs

    def run(self, rrefs, step, n_steps, compute):
        if rrefs is None:
            return compute()

        @pl.when(step == 0)
        def _():
            self.rider.start(*rrefs)

        compute()

        @pl.when(step == n_steps - 1)
        def _():
            self.rider.finish(*rrefs)


def _mm_nn(a, w3, *, name, out_dtype=F32, norm_g=None, resid=None, relu2=False, tm=512, gather=None):
    M, K = a.shape
    ns, _, n = w3.shape
    N = ns * n
    tm = min(tm, M)
    has_norm = norm_g is not None
    has_res = resid is not None
    n_in = 2 + has_norm + has_res
    n_out = (2 if relu2 else 1) + has_norm
    host = _Hosted(_gather_rider(gather), n_in, n_out)

    def body(*refs):
        ins, outs, _, gref = host.split(refs, n_in, n_out)
        a_ref, w_ref = ins[0], ins[1]
        g_ref = ins[2] if has_norm else None
        r_ref = ins[-1] if has_res else None

        def compute():
            if has_norm:
                xv = a_ref[...]
                rs = lax.rsqrt(jnp.mean(xv * xv, axis=-1, keepdims=True) + EPS)
                ab = (xv * rs * g_ref[...]).astype(BF16)
                outs[-1][...] = ab
            else:
                ab = a_ref[...]
            for s in range(ns):
                acc = jnp.dot(ab, w_ref[s], preferred_element_type=F32)
                sl = slice(s * n, (s + 1) * n)
                if relu2:
                    outs[0][:, sl] = acc.astype(BF16)
                    rl = jnp.maximum(acc, 0.0)
                    outs[1][:, sl] = (rl * rl).astype(BF16)
                elif has_res:
                    outs[0][:, sl] = r_ref[:, sl] + acc
                else:
                    outs[0][:, sl] = acc.astype(out_dtype)

        host.run(gref, pl.program_id(0), M // tm, compute)

    row = lambda w: pl.BlockSpec((tm, w), lambda i: (i, 0))
    in_specs = [row(K), pl.BlockSpec((ns, K, n), lambda i: (0, 0, 0))]
    args = [a, w3]
    if has_norm:
        in_specs.append(pl.BlockSpec((1, K), lambda i: (0, 0)))
        args.append(norm_g.reshape(1, K))
    if has_res:
        in_specs.append(row(N))
        args.append(resid)
    if relu2:
        out_shape = [jax.ShapeDtypeStruct((M, N), BF16), jax.ShapeDtypeStruct((M, N), BF16)]
        out_specs = [row(N), row(N)]
    else:
        out_shape = [jax.ShapeDtypeStruct((M, N), F32 if has_res else out_dtype)]
        out_specs = [row(N)]
    if has_norm:
        out_shape.append(jax.ShapeDtypeStruct((M, K), BF16))
        out_specs.append(row(K))
    res = pl.pallas_call(body, grid=(M // tm,), in_specs=in_specs + host.in_specs, out_specs=out_specs + host.out_specs,
                         out_shape=out_shape + host.out_shape, scratch_shapes=host.scratch, input_output_aliases=host.alias,
                         name=name, compiler_params=_cp(("arbitrary",) if host.on else ("parallel",)))(*args, *host.args)
    return res if len(res) > 1 else res[0]


def _mm_nt(g, w3, *, name, out_dtype=BF16, up=None, norm_x=None, norm_g=None, dres=None, tm=512):
    M = g.shape[0]
    ns, K, n = w3.shape
    tm = min(tm, M)
    has_up = up is not None
    has_norm = norm_x is not None
    has_res = dres is not None

    def body(*refs):
        refs = list(refs)
        g_ref, w_ref = refs[0], refs[1]
        pos = 2
        if has_up:
            up_ref = refs[pos]
            pos += 1
        if has_norm:
            x_ref, gn_ref = refs[pos], refs[pos + 1]
            pos += 2
        if has_res:
            r_ref = refs[pos]
            pos += 1
        outs = refs[pos:]
        acc = None
        for s in range(ns):
            part = lax.dot_general(g_ref[:, s * n:(s + 1) * n], w_ref[s], NT, preferred_element_type=F32)
            acc = part if acc is None else acc + part
        if has_up:
            outs[0][...] = (acc * (2.0 * jnp.maximum(up_ref[...].astype(F32), 0.0))).astype(BF16)
        elif has_norm:
            xv = x_ref[...]
            rs = lax.rsqrt(jnp.mean(xv * xv, axis=-1, keepdims=True) + EPS)
            xn = xv * rs
            dxn = acc * gn_ref[...]
            dx = rs * (dxn - xn * jnp.mean(dxn * xn, axis=-1, keepdims=True))
            if has_res:
                dx = dx + r_ref[...]
            outs[0][...] = dx
            outs[1][...] = dx.astype(BF16)

            @pl.when(pl.program_id(0) == 0)
            def _():
                outs[2][...] = jnp.zeros_like(outs[2])

            outs[2][...] += jnp.sum(acc * xn, axis=0, keepdims=True)
        else:
            outs[0][...] = acc.astype(out_dtype)

    row = lambda w: pl.BlockSpec((tm, w), lambda i: (i, 0))
    in_specs = [row(ns * n), pl.BlockSpec((ns, K, n), lambda i: (0, 0, 0))]
    args = [g, w3]
    if has_up:
        in_specs.append(row(K))
        args.append(up)
    if has_norm:
        in_specs += [row(K), pl.BlockSpec((1, K), lambda i: (0, 0))]
        args += [norm_x, norm_g.reshape(1, K)]
    if has_res:
        in_specs.append(row(K))
        args.append(dres)
    if has_norm:
        out_shape = [jax.ShapeDtypeStruct((M, K), F32), jax.ShapeDtypeStruct((M, K), BF16),
                     jax.ShapeDtypeStruct((1, K), F32)]
        out_specs = [row(K), row(K), pl.BlockSpec((1, K), lambda i: (0, 0))]
        sem = ("arbitrary",)
    else:
        out_shape = [jax.ShapeDtypeStruct((M, K), BF16 if has_up else out_dtype)]
        out_specs = [row(K)]
        sem = ("parallel",)
    res = pl.pallas_call(body, grid=(M // tm,), in_specs=in_specs, out_specs=out_specs, out_shape=out_shape,
                         name=name, compiler_params=_cp(sem))(*args)
    return res if len(res) > 1 else res[0]


def _mm_tn(a, g, ns, *, name, tk=512, tm=4096, packed=None):
    M, K = a.shape
    n = g.shape[1] // ns
    tm = min(tm, M)
    tk = min(tk, K)

    def body(*refs):
        a_ref, g_ref, o_ref = refs[0], refs[1], refs[-1]

        @pl.when(pl.program_id(2) == 0)
        def _():
            o_ref[...] = jnp.zeros_like(o_ref)

        o_ref[0] += lax.dot_general(a_ref[...], g_ref[...], TN, preferred_element_type=F32)

    in_specs = [pl.BlockSpec((tm, tk), lambda s, k, m: (m, k)), pl.BlockSpec((tm, n), lambda s, k, m: (m, s))]
    args = [a, g]
    alias = {}
    if packed is None:
        out_spec = pl.BlockSpec((1, tk, n), lambda s, k, m: (s, k, 0))
        out_shape = jax.ShapeDtypeStruct((ns, K, n), F32)
    else:
        buf, rows, off = packed
        per_chip = K * ns // N_CHIPS
        assert n == ROW and per_chip % tk == 0 and off % tk == 0
        if ns == N_CHIPS:
            out_spec = pl.BlockSpec((1, tk, n), lambda s, k, m: (s, off // tk + k, 0))
        else:
            kpc = per_chip // tk
            out_spec = pl.BlockSpec((1, tk, n), lambda s, k, m: (k // kpc, off // tk + k % kpc, 0))
        out_shape = jax.ShapeDtypeStruct((N_CHIPS, rows, ROW), F32)
        if buf is not None:
            in_specs.append(HBM)
            args.append(buf)
            alias = {2: 0}
    return pl.pallas_call(
        body, grid=(ns, K // tk, M // tm), in_specs=in_specs, out_specs=out_spec, out_shape=out_shape, name=name,
        input_output_aliases=alias, compiler_params=_cp(("parallel", "parallel", "arbitrary")))(*args)


def _final(x, gain, target, *, name="final_loss", tr=256):
    S, Dm = x.shape
    tr = min(tr, S)

    def body(x_ref, g_ref, t_ref, loss_ref, dx_ref, dxb_ref, dg_ref):
        @pl.when(pl.program_id(0) == 0)
        def _():
            loss_ref[...] = jnp.zeros_like(loss_ref)
            dg_ref[...] = jnp.zeros_like(dg_ref)

        xv = x_ref[...]
        gv = g_ref[...]
        rs = lax.rsqrt(jnp.mean(xv * xv, axis=-1, keepdims=True) + EPS)
        xn = xv * rs
        err = xn * gv - t_ref[...]
        loss_ref[...] += 0.5 * jnp.sum(jnp.mean(err * err, axis=-1, keepdims=True), axis=0, keepdims=True)
        dout = err * (1.0 / Dm)
        dg_ref[...] += jnp.sum(dout * xn, axis=0, keepdims=True)
        dxn = dout * gv
        dx = rs * (dxn - xn * jnp.mean(dxn * xn, axis=-1, keepdims=True))
        dx_ref[...] = dx
        dxb_ref[...] = dx.astype(BF16)

    row = pl.BlockSpec((tr, Dm), lambda i: (i, 0))
    return pl.pallas_call(
        body, grid=(S // tr,),
        in_specs=[row, pl.BlockSpec((1, Dm), lambda i: (0, 0)), row],
        out_specs=[pl.BlockSpec((1, 1), lambda i: (0, 0)), row, row, pl.BlockSpec((1, Dm), lambda i: (0, 0))],
        out_shape=[jax.ShapeDtypeStruct((1, 1), F32), jax.ShapeDtypeStruct((S, Dm), F32),
                   jax.ShapeDtypeStruct((S, Dm), BF16), jax.ShapeDtypeStruct((1, Dm), F32)],
        name=name, compiler_params=_cp(("arbitrary",)))(x, gain.reshape(1, Dm), target)


def _rope_tables(positions):
    half = ROPE_DIM // 2
    inv_freq = ROPE_THETA ** (-2.0 * jnp.arange(half, dtype=F32) / ROPE_DIM)
    ang = positions.astype(F32)[:, None] * inv_freq
    cos, sin = jnp.cos(ang), jnp.sin(ang)
    S = positions.shape[0]
    ones = jnp.ones((S, HEAD_DIM - ROPE_DIM), F32)
    cos64 = jnp.concatenate([cos, cos, ones], axis=1)
    sin64 = jnp.concatenate([-sin, sin, 0.0 * ones], axis=1)
    return jnp.tile(cos64, (1, 2)), jnp.tile(sin64, (1, 2))


def _rope_partner(t):
    lane = lax.broadcasted_iota(jnp.int32, t.shape, 1)
    low = (lane & (HEAD_DIM - 1)) < (ROPE_DIM // 2)
    return jnp.where(low, pltpu.roll(t, LANES - ROPE_DIM // 2, 1), pltpu.roll(t, ROPE_DIM // 2, 1))


def _qk_prep(p, cos_t, sin_t, *, name="qk_prep", tr=256, gather=None):
    S = p.shape[0]
    tr = min(tr, S)
    scale = HEAD_DIM ** -0.5
    host = _Hosted(_gather_rider(gather), 3, 4)

    def body(*refs):
        ins, outs, _, gref = host.split(refs, 3, 4)
        host.run(gref, pl.program_id(0), S // tr, lambda: inner(*ins, *outs))

    def inner(p_ref, c_ref, s_ref, q_ref, k_ref, v_ref, va_ref):
        cs, sn = c_ref[...], s_ref[...]
        lane = lax.broadcasted_iota(jnp.int32, (tr, LANES), 1)
        lo = lane < HEAD_DIM
        for c in range(Q_W // LANES):
            t = p_ref[:, c * LANES:(c + 1) * LANES]
            q_ref[:, c * LANES:(c + 1) * LANES] = ((t * cs + _rope_partner(t) * sn) * scale).astype(BF16)
        for c in range(KV_W // LANES):
            t = p_ref[:, Q_W + c * LANES:Q_W + (c + 1) * LANES]
            kc = t * cs + _rope_partner(t) * sn
            vc = p_ref[:, Q_W + KV_W + c * LANES:Q_W + KV_W + (c + 1) * LANES]
            for arr, ref in ((kc, k_ref), (vc, v_ref)):
                sw = pltpu.roll(arr, HEAD_DIM, 1)
                ref[:, (2 * c) * LANES:(2 * c + 1) * LANES] = jnp.where(lo, arr, sw).astype(BF16)
                ref[:, (2 * c + 1) * LANES:(2 * c + 2) * LANES] = jnp.where(lo, sw, arr).astype(BF16)
            sw = pltpu.roll(vc, HEAD_DIM, 1)
            for k, aug in enumerate((jnp.where(lo, vc, 1.0), jnp.where(lo, 1.0, sw), jnp.where(lo, sw, 1.0), jnp.where(lo, 1.0, vc))):
                va_ref[:, (4 * c + k) * LANES:(4 * c + k + 1) * LANES] = aug.astype(BF16)

    row = lambda w: pl.BlockSpec((tr, w), lambda i: (i, 0))
    return pl.pallas_call(
        body, grid=(S // tr,), in_specs=[row(ATTN_IN_W), row(LANES), row(LANES)] + host.in_specs,
        out_specs=[row(Q_W), row(2 * KV_W), row(2 * KV_W), row(4 * KV_W)] + host.out_specs,
        out_shape=[jax.ShapeDtypeStruct((S, Q_W), BF16), jax.ShapeDtypeStruct((S, 2 * KV_W), BF16),
                   jax.ShapeDtypeStruct((S, 2 * KV_W), BF16), jax.ShapeDtypeStruct((S, 4 * KV_W), BF16)] + host.out_shape,
        scratch_shapes=host.scratch, input_output_aliases=host.alias,
        name=name, compiler_params=_cp(("arbitrary",) if host.on else ("parallel",)))(p, cos_t, sin_t, *host.args)


def _qk_prep_bwd(dq, dk, dv, dmq, cos_t, sin_t, *, name="qk_prep_bwd", tr=256):
    S = dq.shape[0]
    tr = min(tr, S)

    def body(dq_ref, dk_ref, dv_ref, dmq_ref, c_ref, s_ref, o_ref):
        cs, sn = c_ref[...], s_ref[...]
        for c in range(Q_W // LANES):
            t = dq_ref[:, c * LANES:(c + 1) * LANES]
            o_ref[:, c * LANES:(c + 1) * LANES] = (t * cs - _rope_partner(t) * sn).astype(BF16)
        for c in range(KV_W // LANES):
            t = dk_ref[:, c * LANES:(c + 1) * LANES]
            o_ref[:, Q_W + c * LANES:Q_W + (c + 1) * LANES] = (t * cs - _rope_partner(t) * sn).astype(BF16)
        o_ref[:, Q_W + KV_W:Q_W + 2 * KV_W] = dv_ref[...].astype(BF16)
        o_ref[:, Q_W + 2 * KV_W:] = dmq_ref[...]

    row = lambda w: pl.BlockSpec((tr, w), lambda i: (i, 0))
    return pl.pallas_call(
        body, grid=(S // tr,), in_specs=[row(Q_W), row(KV_W), row(KV_W), row(MEM_W), row(LANES), row(LANES)],
        out_specs=row(ATTN_IN_W), out_shape=jax.ShapeDtypeStruct((S, ATTN_IN_W), BF16),
        name=name, compiler_params=_cp(("parallel",)))(dq, dk, dv, dmq, cos_t, sin_t)


def _band(n, S):
    start = pl.multiple_of(jnp.clip((n - 1) * BLOCK, 0, S - 3 * BLOCK), BLOCK)
    qi = lax.broadcasted_iota(jnp.int32, (BLOCK, 3 * BLOCK), 0) + n * BLOCK
    ki = lax.broadcasted_iota(jnp.int32, (BLOCK, 3 * BLOCK), 1) + start
    return start, jnp.abs(ki - qi) <= WINDOW


def _head_operand(ref, h, lo):
    c = h // 2
    t = ref[:, c * LANES:(c + 1) * LANES].astype(F32)
    return jnp.where(lo if h % 2 == 0 else jnp.logical_not(lo), t, 0.0).astype(BF16)


GROUP = ATTN_HEADS // ATTN_KV_HEADS
EVENS_FIRST = (0, 2, 1, 3)


def _attn_fwd(q, kd, va, sinks, *, name="attn_fwd", gather=None):
    S = q.shape[0]
    host = _Hosted(_gather_rider(gather), 4, 2)

    def body(*refs):
        ins, outs, scr, gref = host.split(refs, 4, 2)
        host.run(gref, pl.program_id(0), S // BLOCK, lambda: inner(*ins, *outs, *scr))

    def inner(sink_ref, q_ref, k_ref, va_ref, o_ref, lse_ref, p_scr):
        n = pl.program_id(0)
        start, mask = _band(n, S)
        lane = lax.broadcasted_iota(jnp.int32, (BLOCK, LANES), 1)
        lo = lane < HEAD_DIM
        rows = pl.ds(start, 3 * BLOCK)
        scores = []
        for g in range(ATTN_KV_HEADS):
            qst = jnp.concatenate([_head_operand(q_ref, GROUP * g + j, lo) for j in EVENS_FIRST], axis=0)
            scores.append(lax.dot_general(qst, k_ref[rows, g * LANES:(g + 1) * LANES], NT, preferred_element_type=F32))
        ms = {}
        for g in range(ATTN_KV_HEADS):
            for pos, j in enumerate(EVENS_FIRST):
                h = GROUP * g + j
                s = jnp.where(mask, scores[g][pos * BLOCK:(pos + 1) * BLOCK], NEG)
                ms[h] = jnp.maximum(jnp.max(s, axis=-1, keepdims=True), sink_ref[h])
                p_scr[(GROUP * g + pos) * BLOCK:(GROUP * g + pos + 1) * BLOCK, :] = jnp.exp(s - ms[h]).astype(BF16)
        pvs = {}
        for g in range(ATTN_KV_HEADS):
            for par in range(2):
                r0 = (GROUP * g + 2 * par) * BLOCK
                pvs[g, par] = jnp.dot(p_scr[r0:r0 + 2 * BLOCK, :], va_ref[rows, (2 * g + par) * LANES:(2 * g + par + 1) * LANES],
                                      preferred_element_type=F32)
        lse_blk = jnp.zeros((BLOCK, LANES), F32)
        for g in range(ATTN_KV_HEADS):
            outs = {}
            for par in range(2):
                for k in range(2):
                    j = EVENS_FIRST[2 * par + k]
                    h = GROUP * g + j
                    pv = pvs[g, par][k * BLOCK:(k + 1) * BLOCK]
                    den = pltpu.roll(pv, HEAD_DIM, 1) + jnp.exp(sink_ref[h] - ms[h])
                    outs[j] = pv * (1.0 / den)
                    l = den[:, par * HEAD_DIM:par * HEAD_DIM + 1]
                    lse_blk = jnp.where(lane == h, ms[h] + jnp.log(l), lse_blk)
            for jj in range(2):
                o_ref[:, (2 * g + jj) * LANES:(2 * g + jj + 1) * LANES] = jnp.where(lo, outs[2 * jj], outs[2 * jj + 1]).astype(BF16)
        lse_ref[...] = lse_blk

    full = lambda w: pl.BlockSpec((S, w), lambda i: (0, 0))
    return pl.pallas_call(
        body, grid=(S // BLOCK,),
        in_specs=[pl.BlockSpec(memory_space=pltpu.SMEM), pl.BlockSpec((BLOCK, Q_W), lambda i: (i, 0)),
                  full(2 * KV_W), full(4 * KV_W)] + host.in_specs,
        out_specs=[pl.BlockSpec((BLOCK, Q_W), lambda i: (i, 0)), pl.BlockSpec((BLOCK, LANES), lambda i: (i, 0))] + host.out_specs,
        out_shape=[jax.ShapeDtypeStruct((S, Q_W), BF16), jax.ShapeDtypeStruct((S, LANES), F32)] + host.out_shape,
        scratch_shapes=[pltpu.VMEM((ATTN_HEADS * BLOCK, 3 * BLOCK), BF16)] + host.scratch, input_output_aliases=host.alias,
        name=name, compiler_params=_cp(("arbitrary",) if host.on else ("parallel",)))(sinks, q, kd, va, *host.args)


def _attn_bwd(q, kd, vd, ao, lse, sinks, dcat, *, name="attn_bwd", rider=None):
    S = q.shape[0]
    scale = HEAD_DIM ** -0.5
    host = _Hosted(rider, 7, 4)

    def body(*refs):
        ins, outs, scr, rrefs = host.split(refs, 7, 4)
        host.run(rrefs, pl.program_id(0), S // BLOCK, lambda: inner(*ins, *outs, *scr))

    def inner(sink_ref, q_ref, k_ref, v_ref, ao_ref, lse_ref, do_ref, dq_ref, dk_ref, dv_ref, ds_ref, p_scr, dsb_scr):
        n = pl.program_id(0)

        @pl.when(n == 0)
        def _():
            dk_ref[...] = jnp.zeros_like(dk_ref)
            dv_ref[...] = jnp.zeros_like(dv_ref)
            ds_ref[...] = jnp.zeros_like(ds_ref)

        start, mask = _band(n, S)
        lane = lax.broadcasted_iota(jnp.int32, (BLOCK, LANES), 1)
        lo = lane < HEAD_DIM
        lane3 = lax.broadcasted_iota(jnp.int32, (3 * BLOCK, LANES), 1)
        row8 = lax.broadcasted_iota(jnp.int32, (8, LANES), 0)
        lane8 = lax.broadcasted_iota(jnp.int32, (8, LANES), 1)
        dsink = jnp.zeros((8, LANES), F32)
        lse_blk = lse_ref[...]
        rows = pl.ds(start, 3 * BLOCK)
        lses, deltas = {}, {}
        for c in range(Q_W // LANES):
            prod = do_ref[:, c * LANES:(c + 1) * LANES].astype(F32) * ao_ref[:, c * LANES:(c + 1) * LANES].astype(F32)
            for k in range(2):
                h = 2 * c + k
                deltas[h] = jnp.sum(jnp.where(lo if k == 0 else jnp.logical_not(lo), prod, 0.0), axis=1, keepdims=True)
                lses[h] = jnp.sum(jnp.where(lane == h, lse_blk, 0.0), axis=1, keepdims=True)
                val = -jnp.sum(jnp.exp(sink_ref[h] - lses[h]) * deltas[h], axis=0, keepdims=True)
                dsink = dsink + jnp.where((row8 == 0) & (lane8 == h), val, 0.0)
        stack = lambda ref, g: jnp.concatenate([_head_operand(ref, GROUP * g + j, lo) for j in range(GROUP)], axis=0)
        ss, dps = [], []
        for g in range(ATTN_KV_HEADS):
            ss.append(lax.dot_general(stack(q_ref, g), k_ref[rows, g * LANES:(g + 1) * LANES], NT, preferred_element_type=F32))
            dps.append(lax.dot_general(stack(do_ref, g), v_ref[rows, g * LANES:(g + 1) * LANES], NT, preferred_element_type=F32))
        for g in range(ATTN_KV_HEADS):
            for j in range(GROUP):
                h = GROUP * g + j
                r = slice(j * BLOCK, (j + 1) * BLOCK)
                hr = slice(h * BLOCK, (h + 1) * BLOCK)
                p = jnp.exp(jnp.where(mask, ss[g][r], NEG) - lses[h])
                p_scr[hr, :] = p.astype(BF16)
                dsb_scr[hr, :] = (p * (dps[g][r] - deltas[h])).astype(BF16)
        for g in range(ATTN_KV_HEADS):
            cols = slice((g // 2) * LANES, (g // 2 + 1) * LANES)
            gr = slice(GROUP * g * BLOCK, GROUP * (g + 1) * BLOCK)
            dsg = dsb_scr[gr, :]
            dqs = jnp.dot(dsg, k_ref[rows, g * LANES:(g + 1) * LANES], preferred_element_type=F32) * scale
            for jj in range(2):
                dq_ref[:, (2 * g + jj) * LANES:(2 * g + jj + 1) * LANES] = jnp.where(
                    lo, dqs[(2 * jj) * BLOCK:(2 * jj + 1) * BLOCK], dqs[(2 * jj + 1) * BLOCK:(2 * jj + 2) * BLOCK])
            half = (lane3 < HEAD_DIM) if g % 2 == 0 else (lane3 >= HEAD_DIM)
            dkr = lax.dot_general(dsg, stack(q_ref, g), TN, preferred_element_type=F32)
            dk_ref[rows, cols] += jnp.where(half, dkr + pltpu.roll(dkr, HEAD_DIM, 1), 0.0)
            dvr = lax.dot_general(p_scr[gr, :], stack(do_ref, g), TN, preferred_element_type=F32)
            dv_ref[rows, cols] += jnp.where(half, dvr + pltpu.roll(dvr, HEAD_DIM, 1), 0.0)
        ds_ref[...] += dsink

    full = lambda w: pl.BlockSpec((S, w), lambda i: (0, 0))
    blk = lambda w: pl.BlockSpec((BLOCK, w), lambda i: (i, 0))
    return pl.pallas_call(
        body, grid=(S // BLOCK,),
        in_specs=[pl.BlockSpec(memory_space=pltpu.SMEM), blk(Q_W), full(2 * KV_W), full(2 * KV_W), blk(Q_W), blk(LANES), blk(Q_W)]
        + host.in_specs,
        out_specs=[blk(Q_W), full(KV_W), full(KV_W), pl.BlockSpec((8, LANES), lambda i: (0, 0))] + host.out_specs,
        out_shape=[jax.ShapeDtypeStruct((S, Q_W), F32), jax.ShapeDtypeStruct((S, KV_W), F32),
                   jax.ShapeDtypeStruct((S, KV_W), F32), jax.ShapeDtypeStruct((8, LANES), F32)] + host.out_shape,
        scratch_shapes=[pltpu.VMEM((ATTN_HEADS * BLOCK, 3 * BLOCK), BF16), pltpu.VMEM((ATTN_HEADS * BLOCK, 3 * BLOCK), BF16)]
        + host.scratch, input_output_aliases=host.alias,
        name=name, compiler_params=_cp(("arbitrary",)))(sinks, q, kd, vd, ao, lse, dcat, *host.args)


def _mem_probs(q_ref, kv_ref, h):
    scale = MEM_HEAD_DIM ** -0.5
    qh = q_ref[:, h * LANES:(h + 1) * LANES].astype(BF16)
    s = lax.dot_general(qh, kv_ref[:, h * LANES:(h + 1) * LANES], NT, preferred_element_type=F32) * scale
    m = jnp.max(s, axis=-1, keepdims=True)
    pe = jnp.exp(s - m)
    return qh, pe * (1.0 / jnp.sum(pe, axis=-1, keepdims=True))


def _memattn_fwd(p, qblk, kv, *, name="memattn_fwd", tr=512):
    S = p.shape[0]
    tr = min(tr, S)

    def body(q_ref, kv_ref, o_ref):
        for h in range(MEM_HEADS):
            _, pr = _mem_probs(q_ref, kv_ref, h)
            o = jnp.dot(pr.astype(BF16), kv_ref[:, MEM_W + h * LANES:MEM_W + (h + 1) * LANES], preferred_element_type=F32)
            o_ref[:, h * LANES:(h + 1) * LANES] = o.astype(BF16)

    return pl.pallas_call(
        body, grid=(S // tr,),
        in_specs=[pl.BlockSpec((tr, MEM_W), lambda i: (i, qblk)), pl.BlockSpec((MEM_LEN, 2 * MEM_W), lambda i: (0, 0))],
        out_specs=pl.BlockSpec((tr, MEM_W), lambda i: (i, 0)),
        out_shape=jax.ShapeDtypeStruct((S, MEM_W), BF16), name=name, compiler_params=_cp(("parallel",)))(p, kv)


def _memattn_bwd(p, qblk, kv, dcat, *, name="memattn_bwd", tr=512):
    S = p.shape[0]
    tr = min(tr, S)
    scale = MEM_HEAD_DIM ** -0.5

    def body(q_ref, kv_ref, do_ref, dq_ref, dkv_ref):
        @pl.when(pl.program_id(0) == 0)
        def _():
            dkv_ref[...] = jnp.zeros_like(dkv_ref)

        for h in range(MEM_HEADS):
            qh, pr = _mem_probs(q_ref, kv_ref, h)
            doh = do_ref[:, h * LANES:(h + 1) * LANES]
            dp = lax.dot_general(doh, kv_ref[:, MEM_W + h * LANES:MEM_W + (h + 1) * LANES], NT, preferred_element_type=F32)
            delta = jnp.sum(pr * dp, axis=-1, keepdims=True)
            dsb = (pr * (dp - delta) * scale).astype(BF16)
            dq = jnp.dot(dsb, kv_ref[:, h * LANES:(h + 1) * LANES], preferred_element_type=F32)
            dq_ref[:, h * LANES:(h + 1) * LANES] = dq.astype(BF16)
            dkv_ref[:, h * LANES:(h + 1) * LANES] += lax.dot_general(dsb, qh, TN, preferred_element_type=F32)
            dkv_ref[:, MEM_W + h * LANES:MEM_W + (h + 1) * LANES] += lax.dot_general(
                pr.astype(BF16), doh, TN, preferred_element_type=F32)

    return pl.pallas_call(
        body, grid=(S // tr,),
        in_specs=[pl.BlockSpec((tr, MEM_W), lambda i: (i, qblk)), pl.BlockSpec((MEM_LEN, 2 * MEM_W), lambda i: (0, 0)),
                  pl.BlockSpec((tr, MEM_W), lambda i: (i, Q_W // MEM_W))],
        out_specs=[pl.BlockSpec((tr, MEM_W), lambda i: (i, 0)), pl.BlockSpec((MEM_LEN, 2 * MEM_W), lambda i: (0, 0))],
        out_shape=[jax.ShapeDtypeStruct((S, MEM_W), BF16), jax.ShapeDtypeStruct((MEM_LEN, 2 * MEM_W), F32)],
        name=name, compiler_params=_cp(("arbitrary",)))(p, kv, dcat)


def _sigmoid(z):
    return 1.0 / (1.0 + jnp.exp(-z))


def _one_minus_exp(z, exp_z):
    poly = z * (1.0 + z * (0.5 + z * (1.0 / 6.0 + z * (1.0 / 24.0 + z * (1.0 / 120.0)))))
    return jnp.where(z > -0.1, -poly, 1.0 - exp_z)


def _softplus_neg(lam):
    z = -lam
    return jnp.maximum(z, 0.0) + jnp.log(1.0 + jnp.exp(-jnp.abs(z)))


_GELU_C = math.sqrt(2.0 / math.pi)


def _gelu(z):
    return 0.5 * z * (1.0 + jnp.tanh(_GELU_C * (z + 0.044715 * z * z * z)))


def _row_or_zero(ref, t, S):
    ok = jnp.logical_and(t >= 0, t < S)
    return jnp.where(ok, ref[pl.ds(jnp.clip(t, 0, S - 1), 1), :], 0.0)


def _shift_down(v, first):
    ri = lax.broadcasted_iota(jnp.int32, v.shape, 0)
    return jnp.where(ri == 0, first, pltpu.roll(v, 1, 0))


def _shift_up(v, last):
    T = v.shape[0]
    ri = lax.broadcasted_iota(jnp.int32, v.shape, 0)
    return jnp.where(ri == T - 1, last, pltpu.roll(v, T - 1, 0))


def _scan_chunk(a, u, reverse):
    T = a.shape[0]
    ri = lax.broadcasted_iota(jnp.int32, a.shape, 0)
    d = 1
    while d < T:
        if reverse:
            a_s, u_s, ok = pltpu.roll(a, T - d, 0), pltpu.roll(u, T - d, 0), ri < T - d
        else:
            a_s, u_s, ok = pltpu.roll(a, d, 0), pltpu.roll(u, d, 0), ri >= d
        u = jnp.where(ok, a * u_s + u, u)
        a = jnp.where(ok, a * a_s, a)
        d *= 2
    return a, u


def _conv_taps(xb_ref, t0, S):
    T = SCAN_ROWS
    x0 = xb_ref[pl.ds(t0, T), :]
    xm1 = _shift_down(x0, _row_or_zero(xb_ref, t0 - 1, S))
    nxt0 = _row_or_zero(xb_ref, t0 + T, S)
    xp1 = _shift_up(x0, nxt0)
    xp2 = _shift_up(xp1, _row_or_zero(xb_ref, t0 + T + 1, S))
    return xm1, x0, xp1, xp2


def _lru_gates(xc, w_a, b_a, w_x, b_x, sp):
    xcb = xc.astype(BF16)
    r = _sigmoid(jnp.dot(xcb, w_a, preferred_element_type=F32) + b_a)
    i = _sigmoid(jnp.dot(xcb, w_x, preferred_element_type=F32) + b_x)
    la = -LRU_C * r * sp
    a = jnp.exp(la)
    beta = jnp.sqrt(_one_minus_exp(2.0 * la, a * a))
    return r, i, a, beta


def _lru_specs(S):
    col = lambda off: pl.BlockSpec((S, LANES), lambda n: (0, n + off), pipeline_mode=pl.Buffered(1))
    small = lambda r: pl.BlockSpec((r, LANES), lambda n: (0, n))
    wblk = pl.BlockSpec((2, 1, LANES, LANES), lambda n: (0, n, 0, 0))
    return col, small, wblk


def _lru_fwd(p, conv_w, conv_b, wa, ba, wx, bx, lam, *, name="lru_fwd"):
    S = p.shape[0]
    T = SCAN_ROWS
    nc = S // T

    def body(xb_ref, gate_ref, cw_ref, cb_ref, wa_ref, ba_ref, wx_ref, bx_ref, lam_ref, y_ref, hf_ref, hr_ref, xc_v):
        sp = _softplus_neg(lam_ref[...])
        cw = cw_ref[...]

        def fwd_step(c, h_in):
            t0 = pl.multiple_of(c * T, T)
            xm1, x0, xp1, xp2 = _conv_taps(xb_ref, t0, S)
            xc = cb_ref[...] + xm1 * cw[0:1] + x0 * cw[1:2] + xp1 * cw[2:3] + xp2 * cw[3:4]
            xc_v[pl.ds(t0, T), :] = xc
            _, i, a, beta = _lru_gates(xc, wa_ref[0, 0], ba_ref[0:1], wx_ref[0, 0], bx_ref[0:1], sp[0:1])
            A, U = _scan_chunk(a, beta * (i * xc), False)
            hf_ref[pl.ds(t0, T), :] = A * h_in + U
            return hf_ref[pl.ds(t0 + T - 1, 1), :]

        lax.fori_loop(0, nc, fwd_step, jnp.zeros((1, LANES), F32))

        def rev_step(k, h_in):
            t0 = pl.multiple_of((nc - 1 - k) * T, T)
            xc = xc_v[pl.ds(t0, T), :]
            _, i, a, beta = _lru_gates(xc, wa_ref[1, 0], ba_ref[1:2], wx_ref[1, 0], bx_ref[1:2], sp[1:2])
            A, U = _scan_chunk(a, beta * (i * xc), True)
            h = A * h_in + U
            hr_ref[pl.ds(t0, T), :] = h
            y_ref[pl.ds(t0, T), :] = ((hf_ref[pl.ds(t0, T), :] + h) * _gelu(gate_ref[pl.ds(t0, T), :])).astype(BF16)
            return hr_ref[pl.ds(t0, 1), :]

        lax.fori_loop(0, nc, rev_step, jnp.zeros((1, LANES), F32))

    col, small, wblk = _lru_specs(S)
    colo = lambda: pl.BlockSpec((S, LANES), lambda n: (0, n))
    return pl.pallas_call(
        body, grid=(LRU_BLOCKS,),
        in_specs=[col(0), col(LRU_BLOCKS), small(4), small(1), wblk, small(2), wblk, small(2), small(2)],
        out_specs=[colo(), colo(), colo()],
        out_shape=[jax.ShapeDtypeStruct((S, D_MODEL), BF16), jax.ShapeDtypeStruct((S, D_MODEL), F32),
                   jax.ShapeDtypeStruct((S, D_MODEL), F32)],
        scratch_shapes=[pltpu.VMEM((S, LANES), F32)],
        name=name, compiler_params=_cp(("parallel",)))(p, p, conv_w, conv_b, wa, ba, wx, bx, lam)


def _lru_bwd(p, hf, hr, dcat, conv_w, conv_b, wa, ba, wx, bx, lam, *, name="lru_bwd"):
    S = p.shape[0]
    T = SCAN_ROWS
    nc = S // T

    def body(xb_ref, gate_ref, hf_ref, hr_ref, dy_ref, cw_ref, cb_ref, wa_ref, ba_ref, wx_ref, bx_ref, lam_ref,
             dxb_ref, dgate_ref, dcw_ref, dcb_ref, dwa_ref, dba_ref, dwx_ref, dbx_ref, dlam_ref, xc_v, dxc_v, dh_v):
        lam_v = lam_ref[...]
        sp = _softplus_neg(lam_v)
        cw = cw_ref[...]
        for ref in (dcw_ref, dcb_ref, dwa_ref, dba_ref, dwx_ref, dbx_ref, dlam_ref):
            ref[...] = jnp.zeros_like(ref)

        def prep_step(c, carry):
            t0 = pl.multiple_of(c * T, T)
            rows = pl.ds(t0, T)
            xm1, x0, xp1, xp2 = _conv_taps(xb_ref, t0, S)
            xc_v[rows, :] = cb_ref[...] + xm1 * cw[0:1] + x0 * cw[1:2] + xp1 * cw[2:3] + xp2 * cw[3:4]
            z = gate_ref[rows, :]
            dy = dy_ref[rows, :].astype(F32)
            th = jnp.tanh(_GELU_C * (z + 0.044715 * z * z * z))
            dgelu = 0.5 * (1.0 + th) + 0.5 * z * (1.0 - th * th) * _GELU_C * (1.0 + 3.0 * 0.044715 * z * z)
            dgate_ref[rows, :] = (dy * (hf_ref[rows, :] + hr_ref[rows, :]) * dgelu).astype(BF16)
            dh_v[rows, :] = dy * (0.5 * z * (1.0 + th))
            return carry

        lax.fori_loop(0, nc, prep_step, 0)

        def direction(d):
            h_ref = hf_ref if d == 0 else hr_ref
            w_a, w_x = wa_ref[d, 0], wx_ref[d, 0]
            b_a, b_x, sp_d = ba_ref[d:d + 1], bx_ref[d:d + 1], sp[d:d + 1]

            def step(k, carry):
                g_in, a_in = carry
                c = (nc - 1 - k) if d == 0 else k
                t0 = pl.multiple_of(c * T, T)
                rows = pl.ds(t0, T)
                xc = xc_v[rows, :]
                r, i, a, beta = _lru_gates(xc, w_a, b_a, w_x, b_x, sp_d)
                dh = dh_v[rows, :]
                hc = h_ref[rows, :]
                if d == 0:
                    A, U = _scan_chunk(_shift_up(a, a_in), dh, True)
                    g = A * g_in + U
                    h_nb = _shift_down(hc, _row_or_zero(h_ref, t0 - 1, S))
                    nxt = (g[0:1], a[0:1])
                else:
                    A, U = _scan_chunk(_shift_down(a, a_in), dh, False)
                    g = A * g_in + U
                    h_nb = _shift_up(hc, _row_or_zero(h_ref, t0 + T, S))
                    nxt = (g[T - 1:T], a[T - 1:T])
                da = g * h_nb
                dbeta = g * (i * xc)
                tb = g * beta
                dla = da * a - dbeta * (a * a / beta)
                dzr = (dla * (-LRU_C * sp_d)) * (r * (1.0 - r))
                dzi = (tb * xc) * (i * (1.0 - i))
                dzrb, dzib, xcb = dzr.astype(BF16), dzi.astype(BF16), xc.astype(BF16)
                dwa_ref[d, 0] += lax.dot_general(xcb, dzrb, TN, preferred_element_type=F32)
                dwx_ref[d, 0] += lax.dot_general(xcb, dzib, TN, preferred_element_type=F32)
                dba_ref[d:d + 1] += jnp.sum(dzr, axis=0, keepdims=True)
                dbx_ref[d:d + 1] += jnp.sum(dzi, axis=0, keepdims=True)
                dlam_ref[d:d + 1] += jnp.sum(dla * (-LRU_C * r), axis=0, keepdims=True)
                dxc = (tb * i + lax.dot_general(dzrb, w_a, NT, preferred_element_type=F32)
                       + lax.dot_general(dzib, w_x, NT, preferred_element_type=F32))
                if d == 0:
                    dxc_v[rows, :] = dxc
                else:
                    dxc_v[rows, :] += dxc
                return nxt

            lax.fori_loop(0, nc, step, (jnp.zeros((1, LANES), F32), jnp.zeros((1, LANES), F32)))

        direction(0)
        direction(1)
        dlam_ref[...] = dlam_ref[...] * (-1.0 / (1.0 + jnp.exp(lam_v)))

        def conv_step(c, carry):
            t0 = pl.multiple_of(c * T, T)
            rows = pl.ds(t0, T)
            g0 = dxc_v[rows, :]
            gm1 = _shift_down(g0, _row_or_zero(dxc_v, t0 - 1, S))
            gm2 = _shift_down(gm1, _row_or_zero(dxc_v, t0 - 2, S))
            gp1 = _shift_up(g0, _row_or_zero(dxc_v, t0 + T, S))
            dxb_ref[rows, :] = (cw[0:1] * gp1 + cw[1:2] * g0 + cw[2:3] * gm1 + cw[3:4] * gm2).astype(BF16)
            xm1, x0, xp1, xp2 = _conv_taps(xb_ref, t0, S)
            for tap, xs in enumerate((xm1, x0, xp1, xp2)):
                dcw_ref[tap:tap + 1] += jnp.sum(g0 * xs, axis=0, keepdims=True)
            dcb_ref[...] += jnp.sum(g0, axis=0, keepdims=True)
            return carry

        lax.fori_loop(0, nc, conv_step, 0)

    col, small, wblk = _lru_specs(S)
    colo = lambda: pl.BlockSpec((S, LANES), lambda n: (0, n), pipeline_mode=pl.Buffered(1))
    return pl.pallas_call(
        body, grid=(LRU_BLOCKS,),
        in_specs=[col(0), col(LRU_BLOCKS), col(0), col(0), col(0), small(4), small(1), wblk, small(2), wblk, small(2), small(2)],
        out_specs=[colo(), colo(), small(4), small(1), wblk, small(2), wblk, small(2), small(2)],
        out_shape=[jax.ShapeDtypeStruct((S, D_MODEL), BF16), jax.ShapeDtypeStruct((S, D_MODEL), BF16),
                   jax.ShapeDtypeStruct((4, D_MODEL), F32), jax.ShapeDtypeStruct((1, D_MODEL), F32),
                   jax.ShapeDtypeStruct((2, LRU_BLOCKS, LANES, LANES), F32), jax.ShapeDtypeStruct((2, D_MODEL), F32),
                   jax.ShapeDtypeStruct((2, LRU_BLOCKS, LANES, LANES), F32), jax.ShapeDtypeStruct((2, D_MODEL), F32),
                   jax.ShapeDtypeStruct((2, D_MODEL), F32)],
        scratch_shapes=[pltpu.VMEM((S, LANES), F32), pltpu.VMEM((S, LANES), F32), pltpu.VMEM((S, LANES), F32)],
        name=name, compiler_params=_cp(("parallel",)))(p, p, hf, hr, dcat, conv_w, conv_b, wa, ba, wx, bx, lam)


def _mlp_fwd(x, w_up, w_down, gain, l):
    up, act, h = _mm_nn(x, w_up, norm_g=gain, relu2=True, name=f"mlp_up{l}")
    return _mm_nn(act, w_down, resid=x, name=f"mlp_down{l}"), (up, act, h)


PK_UP, PK_DOWN, PK_OUT, PK_KV, PK_IN = 0, 1024, 2048, 2432, 2688
SMALL_G_ROWS = 192
PK_SMALL = PK_IN + 512
PK_ROWS = {0: PK_SMALL + SMALL_G_ROWS, 1: PK_IN + 640}


def _mlp_bwd(x, dx, dxb, saved, w_up, w_down, gain, l):
    up, act, h = saved
    pk = _mm_tn(act, dxb, 1, name=f"dw_down{l}", packed=(None, PK_ROWS[l], PK_DOWN))
    dup = _mm_nt(dxb, w_down, up=up, name=f"d_up{l}")
    pk = _mm_tn(h, dup, N_CHIPS, name=f"dw_up{l}", packed=(pk, PK_ROWS[l], PK_UP))
    dx, dxb, g_gain = _mm_nt(dup, w_up, norm_x=x, norm_g=gain, dres=dx, name=f"d_mlp_in{l}")
    return dx, dxb, pk, g_gain


def _reduce_first(pk, place, tag):
    recv = _sibling_exchange(pk, name=f"grad_sibling_exchange{tag}")
    return _sum_halves(pk, recv, place, name=f"sum_halves{tag}", tr=pk.shape[1] // 4)


def _reduce_last(parts, place, tag):
    return _sibling_allgather(_sum_chips(parts, place, name=f"sum_chips{tag}", tr=parts.shape[1] // 2),
                              name=f"grad_sibling_allgather{tag}")


def _local_step(x, mem, positions, target, W, pending=None, place=None):
    cos_t, sin_t = _rope_tables(positions)
    sinks = W["attn_sinks"].reshape(ATTN_HEADS)
    G = {}

    def hosting(late, fn, *args, **kw):
        if pending is None:
            return fn(*args, **kw)
        *res, buf = fn(*args, gather=pending[late], **kw)
        if late.startswith("w_down"):
            W.setdefault("w_down", [None] * DEPTH)[int(late[-1])] = _ready(late, buf)
        else:
            W[late] = _ready(late, buf)
        return res if len(res) > 1 else res[0]

    kv0, memn = _mm_nn(mem, W["w_mem_kv"][0], norm_g=W["mem_norm"], out_dtype=BF16, name="mem_kv0", tm=256)
    kv1 = _mm_nn(memn, W["w_mem_kv"][1], out_dtype=BF16, name="mem_kv1", tm=256)
    p0, h0 = hosting("w_out", _mm_nn, x, W["attn_w_in"], norm_g=W["mix_norm"][0], name="attn_in")
    q, kd, vd, va = hosting("lru_w_in", _qk_prep, p0, cos_t, sin_t)
    ao, lse = hosting("w_up", _attn_fwd, q, kd, va, sinks)
    mo0 = _memattn_fwd(p0, Q_W // MEM_W + 1, kv0, name="memattn_fwd0")
    cat0 = jnp.concatenate([ao, mo0], axis=1)
    x1 = hosting("w_down0", _mm_nn, cat0, W["w_out"][0], resid=x, name="mix_out0")
    up0, act0, h1 = hosting("w_down1", _mm_nn, x1, W["w_up"][0], norm_g=W["mlp_norm"][0], relu2=True, name="mlp_up0")
    x2, mlp0 = _mm_nn(act0, W["w_down"][0], resid=x1, name="mlp_down0"), (up0, act0, h1)
    p1, h2 = _mm_nn(x2, W["lru_w_in"], norm_g=W["mix_norm"][1], name="lru_in")
    lru_w = (W["lru_conv_w"], W["lru_conv_b"], W["lru_wa"], W["lru_ba"], W["lru_wx"], W["lru_bx"], W["lru_lambda"])
    y, hf, hr = _lru_fwd(p1, *lru_w)
    mo1 = _memattn_fwd(p1, 2 * D_MODEL // MEM_W, kv1, name="memattn_fwd1")
    cat1 = jnp.concatenate([y, mo1], axis=1)
    x3 = _mm_nn(cat1, W["w_out"][1], resid=x2, name="mix_out1")
    x4, mlp1 = _mlp_fwd(x3, W["w_up"][1], W["w_down"][1], W["mlp_norm"][1], 1)
    loss, dx, dxb, G["final_norm"] = _final(x4, W["final_norm"], target)

    def put(pk, off, g):
        return pk.at[:, off:off + g.size // (N_CHIPS * ROW)].set(g.reshape(N_CHIPS, -1, ROW))

    dx, dxb, pk1, gm1 = _mlp_bwd(x3, dx, dxb, mlp1, W["w_up"][1], W["w_down"][1], W["mlp_norm"][1], 1)
    pk1 = _mm_tn(cat1, dxb, 1, name="dw_out1", tk=128, packed=(pk1, PK_ROWS[1], PK_OUT))
    dcat1 = _mm_nt(dxb, W["w_out"][1], name="d_mix1")
    dmq1, dkv1 = _memattn_bwd(p1, 2 * D_MODEL // MEM_W, kv1, dcat1, name="memattn_bwd1")
    dkv1b = dkv1.astype(BF16)
    pk1 = _mm_tn(memn, dkv1b, 1, name="dw_kv1", tm=256, tk=128, packed=(pk1, PK_ROWS[1], PK_KV))
    (dxb1, dgate, G["lru_conv_w"], G["lru_conv_b"], G["lru_wa"], G["lru_ba"], G["lru_wx"], G["lru_bx"],
     G["lru_lambda"]) = _lru_bwd(p1, hf, hr, dcat1, *lru_w)
    dp1 = jnp.concatenate([dxb1, dgate, dmq1], axis=1)
    pk1 = put(pk1, PK_IN, _mm_tn(h2, dp1, N_CHIPS, name="dw_lru_in"))
    dx, dxb, gx1 = _mm_nt(dp1, W["lru_w_in"], norm_x=x2, norm_g=W["mix_norm"][1], dres=dx, name="d_lru_in")
    rider = None
    if place is not None:
        rider = _exchange_rider(*_reduce_first(pk1, place, "1"))

    dx, dxb, pk0, gm0 = _mlp_bwd(x1, dx, dxb, mlp0, W["w_up"][0], W["w_down"][0], W["mlp_norm"][0], 0)
    pk0 = _mm_tn(cat0, dxb, 1, name="dw_out0", tk=128, packed=(pk0, PK_ROWS[0], PK_OUT))
    dcat0 = _mm_nt(dxb, W["w_out"][0], name="d_mix0")
    dmq0, dkv0 = _memattn_bwd(p0, Q_W // MEM_W + 1, kv0, dcat0, name="memattn_bwd0")
    dkv0b = dkv0.astype(BF16)
    pk0 = _mm_tn(memn, dkv0b, 1, name="dw_kv0", tm=256, tk=128, packed=(pk0, PK_ROWS[0], PK_KV))
    dq, dk, dv, dsink, *parts1 = _attn_bwd(q, kd, vd, ao, lse, sinks, dcat0, rider=rider)
    if place is not None:
        pk1 = _reduce_last(parts1[0], place, "1")
    dp0 = _qk_prep_bwd(dq, dk, dv, dmq0, cos_t, sin_t)
    pk0 = put(pk0, PK_IN, _mm_tn(h0, dp0, N_CHIPS, name="dw_attn_in"))
    dx, _, gx0 = _mm_nt(dp0, W["attn_w_in"], norm_x=x, norm_g=W["mix_norm"][0], dres=dx, name="d_attn_in")

    w_kv_both = jnp.concatenate([W["w_mem_kv"][0], W["w_mem_kv"][1]], axis=0)
    _, _, G["mem_norm"] = _mm_nt(jnp.concatenate([dkv0b, dkv1b], axis=1), w_kv_both, norm_x=mem, norm_g=W["mem_norm"],
                                 name="d_mem", tm=256)

    G["mix_norm"] = jnp.concatenate([gx0, gx1], axis=0)
    G["mlp_norm"] = jnp.concatenate([gm0, gm1], axis=0)
    G["attn_sinks"] = dsink[0:1, 0:ATTN_HEADS]
    pk0 = put(pk0, PK_SMALL, _flat_pad(_small_grad_list(G), N_CHIPS * SMALL_G_ROWS * ROW))
    return loss[0, 0], dx, G, pk0, pk1


def _comm_call(body, out_shape, n_sems, name, *args, alias=None):
    return pl.pallas_call(
        body, out_shape=out_shape, in_specs=[HBM] * len(args), out_specs=HBM,
        scratch_shapes=[pltpu.SemaphoreType.DMA((n_sems,)), pltpu.SemaphoreType.DMA((n_sems,))],
        input_output_aliases=alias or {}, name=name)(*args)


def _place_slot(shard, slot, n_slots, *, name, tr):
    R, C = shard.shape

    def body(s_ref, a_ref, o_ref):
        o_ref[0] = a_ref[...]

    return pl.pallas_call(
        body,
        grid_spec=pltpu.PrefetchScalarGridSpec(
            num_scalar_prefetch=1, grid=(R // tr,), in_specs=[pl.BlockSpec((tr, C), lambda i, s_ref: (i, 0))],
            out_specs=pl.BlockSpec((1, tr, C), lambda i, s_ref: (s_ref[0], i, 0))),
        out_shape=jax.ShapeDtypeStruct((n_slots, R, C), shard.dtype), name=name,
        compiler_params=_cp(("parallel",)))(slot, shard)


def _allgather_chips(buf, *, name, forward_to_sibling):
    def body(b_ref, o_ref, send_sems, recv_sems):
        if forward_to_sibling:
            _gather_start(o_ref, send_sems, recv_sems)
            _gather_finish(o_ref, send_sems, recv_sems)
            return
        x, y, c, chips = _place()
        own = o_ref.at[2 * x + y]
        sends = [_remote(own, own, send_sems, recv_sems, j, (cx, cy, c)) for j, (cx, cy) in enumerate(chips)]
        for cp in sends:
            cp.start()
        for j, (cx, cy) in enumerate(chips):
            landed = o_ref.at[2 * cx + cy]
            _remote(landed, landed, send_sems, recv_sems, j, (cx, cy, c)).wait_recv()
        for cp in sends:
            cp.wait_send()

    return _comm_call(body, jax.ShapeDtypeStruct(buf.shape, buf.dtype), GATHER_SEMS, name, buf, alias={0: 0})


def _sibling_exchange(g, *, name):
    _, R, C = g.shape
    half = R // 2

    def body(g_ref, o_ref, send_sems, recv_sems):
        x, y, c, _ = _place()
        other = pl.ds(pl.multiple_of((1 - c) * half, 8), half)
        cps = [_remote(g_ref.at[s, other], o_ref.at[s], send_sems, recv_sems, s, (x, y, 1 - c)) for s in range(N_CHIPS)]
        for cp in cps:
            cp.start()
        for cp in cps:
            cp.wait()

    return _comm_call(body, jax.ShapeDtypeStruct((N_CHIPS, half, C), g.dtype), N_CHIPS, name, g)


def _chip_exchange(h, parts, *, name):
    def body(h_ref, p_ref, o_ref, send_sems, recv_sems):
        x, y, c, chips = _place()
        me = 2 * x + y
        cps = [_remote(h_ref.at[2 * cx + cy], o_ref.at[me], send_sems, recv_sems, j, (cx, cy, c))
               for j, (cx, cy) in enumerate(chips)]
        for cp in cps:
            cp.start()
        for j, (cx, cy) in enumerate(chips):
            got = o_ref.at[2 * cx + cy]
            _remote(got, got, send_sems, recv_sems, j, (cx, cy, c)).wait_recv()
        for cp in cps:
            cp.wait_send()

    return _comm_call(body, jax.ShapeDtypeStruct(parts.shape, parts.dtype), 3, name, h, parts, alias={1: 0})


def _sibling_allgather(full, *, name):
    R, C = full.shape
    half = R // 2

    def body(f_ref, o_ref, send_sems, recv_sems):
        x, y, c, _ = _place()
        mine = o_ref.at[pl.ds(pl.multiple_of(c * half, 8), half)]
        cp = _remote(mine, mine, send_sems, recv_sems, 0, (x, y, 1 - c))
        cp.start()
        got = o_ref.at[pl.ds(pl.multiple_of((1 - c) * half, 8), half)]
        _remote(got, got, send_sems, recv_sems, 0, (x, y, 1 - c)).wait_recv()
        cp.wait_send()

    return _comm_call(body, jax.ShapeDtypeStruct(full.shape, full.dtype), 1, name, full, alias={0: 0})


def _sum_halves(g, recv, place, *, name="sum_halves", tr=480):
    _, R, C = g.shape
    half = R // 2
    nblk = half // tr

    def body(pl_ref, g_ref, r_ref, o_ref, own_ref):
        v = (g_ref[...] + r_ref[...]).astype(BF16)
        o_ref[...] = v

        @pl.when(pl.program_id(1) == pl_ref[1])
        def _():
            own_ref[...] = v

    blk = pl.BlockSpec((1, tr, C), lambda i, s, p: (s, i, 0))
    return pl.pallas_call(
        body,
        grid_spec=pltpu.PrefetchScalarGridSpec(
            num_scalar_prefetch=1, grid=(nblk, N_CHIPS),
            in_specs=[pl.BlockSpec((1, tr, C), lambda i, s, p: (s, p[0] * nblk + i, 0)), blk],
            out_specs=[blk, pl.BlockSpec((1, tr, C), lambda i, s, p: (p[1], i, 0))]),
        out_shape=[jax.ShapeDtypeStruct((N_CHIPS, half, C), BF16)] * 2, name=name,
        compiler_params=_cp(("parallel", "arbitrary")))(place, g, recv)


def _sum_chips(parts, place, *, name="sum_chips", tr=480):
    _, R, C = parts.shape
    nblk = R // tr

    def body(pl_ref, p_ref, o_ref):
        acc = p_ref[0].astype(F32) + p_ref[1].astype(F32)
        o_ref[...] = (acc + p_ref[2].astype(F32)) + p_ref[3].astype(F32)

    return pl.pallas_call(
        body,
        grid_spec=pltpu.PrefetchScalarGridSpec(
            num_scalar_prefetch=1, grid=(nblk,), in_specs=[pl.BlockSpec((N_CHIPS, tr, C), lambda i, p: (0, i, 0))],
            out_specs=pl.BlockSpec((tr, C), lambda i, p: (p[0] * nblk + i, 0))),
        out_shape=jax.ShapeDtypeStruct((2 * R, C), F32), name=name, compiler_params=_cp(("parallel",)))(place, parts)


def _adamw(w, g, m, v, *, name, tr=128):
    R, C = w.shape
    bc1 = 1.0 - ADAM_B1 ** ADAM_STEP
    bc2 = 1.0 - ADAM_B2 ** ADAM_STEP

    def body(w_ref, g_ref, m_ref, v_ref, d_ref, nm_ref, nv_ref):
        gv = g_ref[...]
        nm = ADAM_B1 * m_ref[...] + (1.0 - ADAM_B1) * gv
        nv = ADAM_B2 * v_ref[...] + (1.0 - ADAM_B2) * (gv * gv)
        d_ref[...] = -ADAM_LR * ((nm / bc1) / (jnp.sqrt(nv / bc2) + ADAM_EPS) + ADAM_WD * w_ref[...])
        nm_ref[...] = nm
        nv_ref[...] = nv

    blk = pl.BlockSpec((tr, C), lambda i: (i, 0))
    return pl.pallas_call(
        body, grid=(R // tr,), in_specs=[blk] * 4, out_specs=[blk] * 3,
        out_shape=[jax.ShapeDtypeStruct((R, C), F32)] * 3, name=name, compiler_params=_cp(("parallel",)))(w, g, m, v)


ROW = 1024
BIG = ("w_mem_kv", "w_out", "w_up", "w_down", "attn_w_in", "lru_w_in")
SMALL_SHARDED = ("lru_conv_w", "lru_conv_b", "lru_ba", "lru_bx", "lru_lambda")
REPLICATED = ("mix_norm", "mlp_norm", "mem_norm", "final_norm", "attn_sinks", "lru_wa", "lru_wx")
SMALL = REPLICATED + SMALL_SHARDED
WEIGHTS = ("mix_norm", "mlp_norm", "mem_norm", "final_norm", "w_mem_kv", "w_out", "w_up", "w_down", "attn_w_in",
           "attn_sinks", "lru_w_in", "lru_conv_w", "lru_conv_b", "lru_wa", "lru_ba", "lru_wx", "lru_bx", "lru_lambda")
SMALL_W_ROWS = 32
ADAM_SMALL_ROWS = 640


def _rows(a):
    return a.reshape(-1, ROW)


def _flat_pad(parts, total):
    flat = jnp.concatenate([p.reshape(-1) for p in parts])
    return jnp.pad(flat, (0, total - flat.shape[0]))


def _pad_rows(a):
    flat = a.reshape(-1)
    n = -(-flat.shape[0] // ROW) * ROW
    return jnp.pad(flat, (0, n - flat.shape[0])).reshape(-1, ROW)


LATE = ("w_out", "lru_w_in", "w_up", "w_down0", "w_down1")


def _ready(name, full):
    if name == "w_out":
        wo = full.reshape(N_CHIPS, DEPTH, -1, D_MODEL)
        return [wo[:, l].reshape(1, MIX_OUT_W, D_MODEL) for l in range(DEPTH)]
    if name == "w_up":
        wu = full.reshape(N_CHIPS, DEPTH, D_MODEL, D_FF // N_CHIPS)
        return [wu[:, l] for l in range(DEPTH)]
    if name == "lru_w_in":
        return full.reshape(N_CHIPS, D_MODEL, LRU_IN_W // N_CHIPS)
    return full.reshape(1, D_FF, D_MODEL)


def _gather_weights(P, chip1):
    bf = lambda a: _rows(a.astype(BF16))
    small = _flat_pad([P[n] for n in SMALL_SHARDED], SMALL_W_ROWS * ROW // 2)
    small_bits = lax.bitcast_convert_type(small, BF16).reshape(SMALL_W_ROWS, ROW)
    early = jnp.concatenate([bf(P["attn_w_in"]), bf(P["w_mem_kv"]), small_bits], axis=0)
    n_in, n_kv = P["attn_w_in"].size // ROW, P["w_mem_kv"].size // ROW
    placed = _place_slot(early, chip1, N_CHIPS, name="place_weights", tr=early.shape[0] // 2)
    full = _allgather_chips(placed, name="allgather_weights", forward_to_sibling=True)
    late = {"w_out": bf(P["w_out"]), "lru_w_in": bf(P["lru_w_in"]), "w_up": bf(P["w_up"]),
            "w_down0": bf(P["w_down"][0]), "w_down1": bf(P["w_down"][1])}
    pending = {n: _place_slot(late[n], chip1, N_CHIPS, name=f"place_{n}", tr=late[n].shape[0] // 2) for n in LATE}
    W = {n: P[n] for n in REPLICATED}
    W["attn_w_in"] = full[:, :n_in].reshape(N_CHIPS, D_MODEL, ATTN_IN_W // N_CHIPS)
    kv = full[:, n_in:n_in + n_kv].reshape(N_CHIPS, DEPTH, -1, D_MODEL)
    W["w_mem_kv"] = [kv[:, l].reshape(1, D_MODEL, D_MODEL) for l in range(DEPTH)]
    sm = lax.bitcast_convert_type(full[:, n_in + n_kv:].reshape(N_CHIPS, -1, 2), F32)
    o = 0
    for n in SMALL_SHARDED:
        shp = P[n].shape[1:]
        cnt = math.prod(shp)
        piece = sm[:, o:o + cnt].reshape((N_CHIPS,) + shp)
        piece = jnp.moveaxis(piece, 0, -2)
        W[n] = piece.reshape(shp[:-1] + (N_CHIPS * shp[-1],)).reshape(-1, D_MODEL)
        o += cnt
    W["lru_wa"] = P["lru_wa"][0].astype(BF16)
    W["lru_wx"] = P["lru_wx"][0].astype(BF16)
    return W, pending


def _small_grad_list(G):
    return [G["mix_norm"], G["mlp_norm"], G["mem_norm"], G["final_norm"], jnp.pad(G["attn_sinks"].reshape(-1), (0, ROW - ATTN_HEADS)),
            G["lru_wa"], G["lru_wx"], G["lru_conv_w"], G["lru_conv_b"], G["lru_ba"], G["lru_bx"], G["lru_lambda"]]


SMALL_G_SIZES = (2 * D_MODEL, 2 * D_MODEL, D_MODEL, D_MODEL, ROW, 2 * 8 * 128 * 128, 2 * 8 * 128 * 128,
                 4 * D_MODEL, D_MODEL, 2 * D_MODEL, 2 * D_MODEL, 2 * D_MODEL)


def _finish_grads(pk0, full1, place, chip1):
    parts0 = _chip_exchange(*_reduce_first(pk0, place, "0"), name="grad_chip_exchange0")
    full0 = _reduce_last(parts0, place, "0")
    small_placed = _place_slot(full0[PK_SMALL:], chip1, N_CHIPS, name="place_small_grads", tr=SMALL_G_ROWS)
    small_all = _allgather_chips(small_placed, name="allgather_small_grads", forward_to_sibling=False)
    flat = small_all.reshape(-1)
    small = {}
    o = 0
    names = ("mix_norm", "mlp_norm", "mem_norm", "final_norm", "attn_sinks", "lru_wa", "lru_wx",
             "lru_conv_w", "lru_conv_b", "lru_ba", "lru_bx", "lru_lambda")
    for n, cnt in zip(names, SMALL_G_SIZES):
        small[n] = flat[o:o + cnt]
        o += cnt
    both = lambda off, r: jnp.concatenate([full0[off:off + r], full1[off:off + r]], axis=0)
    big = {"w_up": both(PK_UP, 1024), "w_down": both(PK_DOWN, 1024), "w_out": both(PK_OUT, 384), "w_mem_kv": both(PK_KV, 256),
           "attn_w_in": full0[PK_IN:PK_IN + 512], "lru_w_in": full1[PK_IN:PK_IN + 640]}
    return big, small


def kernel(x, mem, positions, mix_norm, mlp_norm, mem_norm, final_norm, w_mem_kv, w_out, w_up, w_down, attn_w_in, attn_sinks, lru_w_in, lru_conv_w, lru_conv_b, lru_wa, lru_ba, lru_wx, lru_bx, lru_lambda, loss_target, m_mix_norm, m_mlp_norm, m_mem_norm, m_final_norm, m_w_mem_kv, m_w_out, m_w_up, m_w_down, m_attn_w_in, m_attn_sinks, m_lru_w_in, m_lru_conv_w, m_lru_conv_b, m_lru_wa, m_lru_ba, m_lru_wx, m_lru_bx, m_lru_lambda, v_mix_norm, v_mlp_norm, v_mem_norm, v_final_norm, v_w_mem_kv, v_w_out, v_w_up, v_w_down, v_attn_w_in, v_attn_sinks, v_lru_w_in, v_lru_conv_w, v_lru_conv_b, v_lru_wa, v_lru_ba, v_lru_wx, v_lru_bx, v_lru_lambda):
    P = dict(mix_norm=mix_norm, mlp_norm=mlp_norm, mem_norm=mem_norm, final_norm=final_norm, w_mem_kv=w_mem_kv, w_out=w_out,
             w_up=w_up, w_down=w_down, attn_w_in=attn_w_in, attn_sinks=attn_sinks, lru_w_in=lru_w_in, lru_conv_w=lru_conv_w,
             lru_conv_b=lru_conv_b, lru_wa=lru_wa, lru_ba=lru_ba, lru_wx=lru_wx, lru_bx=lru_bx, lru_lambda=lru_lambda)
    M1 = dict(mix_norm=m_mix_norm, mlp_norm=m_mlp_norm, mem_norm=m_mem_norm, final_norm=m_final_norm, w_mem_kv=m_w_mem_kv,
              w_out=m_w_out, w_up=m_w_up, w_down=m_w_down, attn_w_in=m_attn_w_in, attn_sinks=m_attn_sinks, lru_w_in=m_lru_w_in,
              lru_conv_w=m_lru_conv_w, lru_conv_b=m_lru_conv_b, lru_wa=m_lru_wa, lru_ba=m_lru_ba, lru_wx=m_lru_wx,
              lru_bx=m_lru_bx, lru_lambda=m_lru_lambda)
    V2 = dict(mix_norm=v_mix_norm, mlp_norm=v_mlp_norm, mem_norm=v_mem_norm, final_norm=v_final_norm, w_mem_kv=v_w_mem_kv,
              w_out=v_w_out, w_up=v_w_up, w_down=v_w_down, attn_w_in=v_attn_w_in, attn_sinks=v_attn_sinks, lru_w_in=v_lru_w_in,
              lru_conv_w=v_lru_conv_w, lru_conv_b=v_lru_conv_b, lru_wa=v_lru_wa, lru_ba=v_lru_ba, lru_wx=v_lru_wx,
              lru_bx=v_lru_bx, lru_lambda=v_lru_lambda)
    chip = 2 * lax.axis_index("x") + lax.axis_index("y")
    chip1 = chip.astype(jnp.int32).reshape(1)
    place = jnp.stack([lax.axis_index("c").astype(jnp.int32), chip.astype(jnp.int32)])

    W, pending = _gather_weights(P, chip1)
    loss, dx, _, pk0, full1 = _local_step(x[0], mem[0], positions[0], loss_target[0], W, pending, place)
    loss = lax.psum(loss, ("x", "y", "c"))
    big, small = _finish_grads(pk0, full1, place, chip1)

    grads, deltas, new_m, new_v = {}, {}, {}, {}
    for n in BIG:
        g = big[n]
        d, nm, nv = _adamw(_rows(P[n]), g, _rows(M1[n]), _rows(V2[n]), name=f"adamw_{n}")
        grads[n], deltas[n], new_m[n], new_v[n] = (t.reshape(P[n].shape) for t in (g, d, nm, nv))

    for n in SMALL:
        g = small[n]
        if n in SMALL_SHARDED:
            shard = P[n].shape[-1]
            g = lax.dynamic_slice_in_dim(g.reshape(-1, N_CHIPS * shard), chip * shard, shard, axis=1)
        elif n == "attn_sinks":
            g = g[:ATTN_HEADS]
        grads[n] = g.reshape(P[n].shape)
    packs = []
    for src in (P, grads, M1, V2):
        a = jnp.concatenate([_pad_rows(src[n]) for n in SMALL], axis=0)
        packs.append(jnp.pad(a, ((0, ADAM_SMALL_ROWS - a.shape[0]), (0, 0))))
    d_s, nm_s, nv_s = _adamw(*packs, name="adamw_small")
    o = 0
    for n in SMALL:
        cnt = math.prod(P[n].shape)
        r = -(-cnt // ROW)
        for dst, src in ((deltas, d_s), (new_m, nm_s), (new_v, nv_s)):
            dst[n] = src[o:o + r].reshape(-1)[:cnt].reshape(P[n].shape)
        o += r

    return (loss, dx[None], *[grads[n] for n in WEIGHTS], *[deltas[n] for n in WEIGHTS],
            *[new_m[n] for n in WEIGHTS], *[new_v[n] for n in WEIGHTS])
```

```python
import functools
import math

import jax
import jax.numpy as jnp
from jax import lax
from jax.experimental import pallas as pl
from jax.experimental.pallas import tpu as pltpu

F32 = jnp.float32
BF16 = jnp.bfloat16
MESH = pl.DeviceIdType.MESH

D_MODEL = 1024
DEPTH = 2
EPS = 1e-6
ATTN_HEADS = 16
ATTN_KV_HEADS = 4
HEAD_DIM = 64
WINDOW = 128
BLOCK = 128
ROPE_THETA = 500000.0
ROPE_DIM = 16
Q_W = 1024
KV_W = 256
MEM_LEN = 256
MEM_HEADS = 4
MEM_HEAD_DIM = 128
MEM_W = 512
LRU_BLOCKS = 8
LRU_C = 8.0
ATTN_IN_W = 2048
LRU_IN_W = 2560
MIX_OUT_W = 1536
D_FF = 4096
NEG = -1e30
N_CHIPS = 4

ADAM_LR = 0.001
ADAM_B1 = 0.9
ADAM_B2 = 0.999
ADAM_EPS = 1e-08
ADAM_WD = 0.01
ADAM_STEP = 10

LANES = 128
SCAN_ROWS = 512
VMEM_LIMIT = 56 * 1024 * 1024

NT = (((1,), (1,)), ((), ()))
TN = (((0,), (0,)), ((), ()))


def _cp(sem=None):
    return pltpu.CompilerParams(dimension_semantics=sem, vmem_limit_bytes=VMEM_LIMIT)


HBM = pl.BlockSpec(memory_space=pl.ANY)
GATHER_SEMS = 6


def _place():
    x, y, c = lax.axis_index("x"), lax.axis_index("y"), lax.axis_index("c")
    chips = [(1 - x, y), (x, 1 - y), (1 - x, 1 - y)]
    return x, y, c, chips


def _remote(src, dst, send_sems, recv_sems, k, to):
    return pltpu.make_async_remote_copy(src_ref=src, dst_ref=dst, send_sem=send_sems.at[k], recv_sem=recv_sems.at[k],
                                        device_id=to, device_id_type=MESH)


def _gather_start(o_ref, send_sems, recv_sems):
    x, y, c, chips = _place()
    half = o_ref.shape[1] // 2
    own = o_ref.at[2 * x + y, pl.ds(pl.multiple_of(c * half, 16), half)]
    for j, (cx, cy) in enumerate(chips):
        _remote(own, own, send_sems, recv_sems, j, (cx, cy, c)).start()


def _gather_finish(o_ref, send_sems, recv_sems):
    x, y, c, chips = _place()
    half = o_ref.shape[1] // 2
    my_rows = pl.ds(pl.multiple_of(c * half, 16), half)
    sib_rows = pl.ds(pl.multiple_of((1 - c) * half, 16), half)
    own = o_ref.at[2 * x + y, my_rows]
    passed = []
    for j, (cx, cy) in enumerate(chips):
        landed = o_ref.at[2 * cx + cy, my_rows]
        _remote(landed, landed, send_sems, recv_sems, j, (cx, cy, c)).wait_recv()
        fw = _remote(landed, landed, send_sems, recv_sems, 3 + j, (x, y, 1 - c))
        fw.start()
        passed.append(fw)
    for j, (cx, cy) in enumerate(chips):
        got = o_ref.at[2 * cx + cy, sib_rows]
        _remote(got, got, send_sems, recv_sems, 3 + j, (x, y, 1 - c)).wait_recv()
    for j, (cx, cy) in enumerate(chips):
        _remote(own, own, send_sems, recv_sems, j, (cx, cy, c)).wait_send()
    for fw in passed:
        fw.wait_send()


def _exchange_start(h_ref, o_ref, send_sems, recv_sems):
    x, y, c, chips = _place()
    for j, (cx, cy) in enumerate(chips):
        _remote(h_ref.at[2 * cx + cy], o_ref.at[2 * x + y], send_sems, recv_sems, j, (cx, cy, c)).start()


def _exchange_finish(h_ref, o_ref, send_sems, recv_sems):
    x, y, c, chips = _place()
    for j, (cx, cy) in enumerate(chips):
        got = o_ref.at[2 * cx + cy]
        _remote(got, got, send_sems, recv_sems, j, (cx, cy, c)).wait_recv()
    for j, (cx, cy) in enumerate(chips):
        _remote(h_ref.at[2 * cx + cy], o_ref.at[2 * x + y], send_sems, recv_sems, j, (cx, cy, c)).wait_send()


class _Rider:
    def __init__(self, args, start, finish):
        self.args, self.start, self.finish = list(args), start, finish


def _gather_rider(buf):
    return None if buf is None else _Rider([buf], _gather_start, _gather_finish)


def _exchange_rider(h, landing):
    return _Rider([h, landing], _exchange_start, _exchange_finish)


class _Hosted:
    def __init__(self, rider, n_in, n_out):
        self.rider = rider
        self.on = rider is not None
        self.args = rider.args if self.on else []
        k = len(self.args)
        self.alias = {n_in + k - 1: n_out} if self.on else {}
        self.in_specs = [HBM] * k
        self.out_specs = [HBM] if self.on else []
        self.out_shape = [jax.ShapeDtypeStruct(self.args[-1].shape, self.args[-1].dtype)] if self.on else []
        self.scratch = [pltpu.SemaphoreType.DMA((GATHER_SEMS,)), pltpu.SemaphoreType.DMA((GATHER_SEMS,))] if self.on else []

    def split(self, refs, n_in, n_out):
        refs = list(refs)
        if not self.on:
            return refs[:n_in], refs[n_in:n_in + n_out], refs[n_in + n_out:], None
        k = len(self.args)
        ins, outs = refs[:n_in], refs[n_in + k:n_in + k + n_out]
        rest = refs[n_in + k + n_out + 1:]
        rrefs = refs[n_in:n_in + k - 1] + [refs[n_in + k + n_out], rest[-2], rest[-1]]
        return ins, outs, rest[:-2], rrefs

    def run(self, rrefs, step, n_steps, compute):
        if rrefs is None:
            return compute()

        @pl.when(step == 0)
        def _():
            self.rider.start(*rrefs)

        compute()

        @pl.when(step == n_steps - 1)
        def _():
            self.rider.finish(*rrefs)


def _mm_nn(a, w3, *, name, out_dtype=F32, norm_g=None, resid=None, relu2=False, tm=512, gather=None):
    M, K = a.shape
    ns, _, n = w3.shape
    N = ns * n
    tm = min(tm, M)
    has_norm = norm_g is not None
    has_res = resid is not None
    n_in = 2 + has_norm + has_res
    n_out = (2 if relu2 else 1) + has_norm
    host = _Hosted(_gather_rider(gather), n_in, n_out)

    def body(*refs):
        ins, outs, _, gref = host.split(refs, n_in, n_out)
        a_ref, w_ref = ins[0], ins[1]
        g_ref = ins[2] if has_norm else None
        r_ref = ins[-1] if has_res else None

        def compute():
            if has_norm:
                xv = a_ref[...]
                rs = lax.rsqrt(jnp.mean(xv * xv, axis=-1, keepdims=True) + EPS)
                ab = (xv * rs * g_ref[...]).astype(BF16)
                outs[-1][...] = ab
            else:
                ab = a_ref[...]
            for s in range(ns):
                acc = jnp.dot(ab, w_ref[s], preferred_element_type=F32)
                sl = slice(s * n, (s + 1) * n)
                if relu2:
                    outs[0][:, sl] = acc.astype(BF16)
                    rl = jnp.maximum(acc, 0.0)
                    outs[1][:, sl] = (rl * rl).astype(BF16)
                elif has_res:
                    outs[0][:, sl] = r_ref[:, sl] + acc
                else:
                    outs[0][:, sl] = acc.astype(out_dtype)

        host.run(gref, pl.program_id(0), M // tm, compute)

    row = lambda w: pl.BlockSpec((tm, w), lambda i: (i, 0))
    in_specs = [row(K), pl.BlockSpec((ns, K, n), lambda i: (0, 0, 0))]
    args = [a, w3]
    if has_norm:
        in_specs.append(pl.BlockSpec((1, K), lambda i: (0, 0)))
        args.append(norm_g.reshape(1, K))
    if has_res:
        in_specs.append(row(N))
        args.append(resid)
    if relu2:
        out_shape = [jax.ShapeDtypeStruct((M, N), BF16), jax.ShapeDtypeStruct((M, N), BF16)]
        out_specs = [row(N), row(N)]
    else:
        out_shape = [jax.ShapeDtypeStruct((M, N), F32 if has_res else out_dtype)]
        out_specs = [row(N)]
    if has_norm:
        out_shape.append(jax.ShapeDtypeStruct((M, K), BF16))
        out_specs.append(row(K))
    res = pl.pallas_call(body, grid=(M // tm,), in_specs=in_specs + host.in_specs, out_specs=out_specs + host.out_specs,
                         out_shape=out_shape + host.out_shape, scratch_shapes=host.scratch, input_output_aliases=host.alias,
                         name=name, compiler_params=_cp(("arbitrary",) if host.on else ("parallel",)))(*args, *host.args)
    return res if len(res) > 1 else res[0]


def _mm_nt(g, w3, *, name, out_dtype=BF16, up=None, norm_x=None, norm_g=None, dres=None, tm=512):
    M = g.shape[0]
    ns, K, n = w3.shape
    tm = min(tm, M)
    has_up = up is not None
    has_norm = norm_x is not None
    has_res = dres is not None

    def body(*refs):
        refs = list(refs)
        g_ref, w_ref = refs[0], refs[1]
        pos = 2
        if has_up:
            up_ref = refs[pos]
            pos += 1
        if has_norm:
            x_ref, gn_ref = refs[pos], refs[pos + 1]
            pos += 2
        if has_res:
            r_ref = refs[pos]
            pos += 1
        outs = refs[pos:]
        acc = None
        for s in range(ns):
            part = lax.dot_general(g_ref[:, s * n:(s + 1) * n], w_ref[s], NT, preferred_element_type=F32)
            acc = part if acc is None else acc + part
        if has_up:
            outs[0][...] = (acc * (2.0 * jnp.maximum(up_ref[...].astype(F32), 0.0))).astype(BF16)
        elif has_norm:
            xv = x_ref[...]
            rs = lax.rsqrt(jnp.mean(xv * xv, axis=-1, keepdims=True) + EPS)
            xn = xv * rs
            dxn = acc * gn_ref[...]
            dx = rs * (dxn - xn * jnp.mean(dxn * xn, axis=-1, keepdims=True))
            if has_res:
                dx = dx + r_ref[...]
            outs[0][...] = dx
            outs[1][...] = dx.astype(BF16)

            @pl.when(pl.program_id(0) == 0)
            def _():
                outs[2][...] = jnp.zeros_like(outs[2])

            outs[2][...] += jnp.sum(acc * xn, axis=0, keepdims=True)
        else:
            outs[0][...] = acc.astype(out_dtype)

    row = lambda w: pl.BlockSpec((tm, w), lambda i: (i, 0))
    in_specs = [row(ns * n), pl.BlockSpec((ns, K, n), lambda i: (0, 0, 0))]
    args = [g, w3]
    if has_up:
        in_specs.append(row(K))
        args.append(up)
    if has_norm:
        in_specs += [row(K), pl.BlockSpec((1, K), lambda i: (0, 0))]
        args += [norm_x, norm_g.reshape(1, K)]
    if has_res:
        in_specs.append(row(K))
        args.append(dres)
    if has_norm:
        out_shape = [jax.ShapeDtypeStruct((M, K), F32), jax.ShapeDtypeStruct((M, K), BF16),
                     jax.ShapeDtypeStruct((1, K), F32)]
        out_specs = [row(K), row(K), pl.BlockSpec((1, K), lambda i: (0, 0))]
        sem = ("arbitrary",)
    else:
        out_shape = [jax.ShapeDtypeStruct((M, K), BF16 if has_up else out_dtype)]
        out_specs = [row(K)]
        sem = ("parallel",)
    res = pl.pallas_call(body, grid=(M // tm,), in_specs=in_specs, out_specs=out_specs, out_shape=out_shape,
                         name=name, compiler_params=_cp(sem))(*args)
    return res if len(res) > 1 else res[0]


def _mm_tn(a, g, ns, *, name, tk=512, tm=4096, packed=None):
    M, K = a.shape
    n = g.shape[1] // ns
    tm = min(tm, M)
    tk = min(tk, K)

    def body(*refs):
        a_ref, g_ref, o_ref = refs[0], refs[1], refs[-1]

        @pl.when(pl.program_id(2) == 0)
        def _():
            o_ref[...] = jnp.zeros_like(o_ref)

        o_ref[0] += lax.dot_general(a_ref[...], g_ref[...], TN, preferred_element_type=F32)

    in_specs = [pl.BlockSpec((tm, tk), lambda s, k, m: (m, k)), pl.BlockSpec((tm, n), lambda s, k, m: (m, s))]
    args = [a, g]
    alias = {}
    if packed is None:
        out_spec = pl.BlockSpec((1, tk, n), lambda s, k, m: (s, k, 0))
        out_shape = jax.ShapeDtypeStruct((ns, K, n), F32)
    else:
        buf, rows, off = packed
        per_chip = K * ns // N_CHIPS
        assert n == ROW and per_chip % tk == 0 and off % tk == 0
        if ns == N_CHIPS:
            out_spec = pl.BlockSpec((1, tk, n), lambda s, k, m: (s, off // tk + k, 0))
        else:
            kpc = per_chip // tk
            out_spec = pl.BlockSpec((1, tk, n), lambda s, k, m: (k // kpc, off // tk + k % kpc, 0))
        out_shape = jax.ShapeDtypeStruct((N_CHIPS, rows, ROW), F32)
        if buf is not None:
            in_specs.append(HBM)
            args.append(buf)
            alias = {2: 0}
    return pl.pallas_call(
        body, grid=(ns, K // tk, M // tm), in_specs=in_specs, out_specs=out_spec, out_shape=out_shape, name=name,
        input_output_aliases=alias, compiler_params=_cp(("parallel", "parallel", "arbitrary")))(*args)


def _final(x, gain, target, *, name="final_loss", tr=256):
    S, Dm = x.shape
    tr = min(tr, S)

    def body(x_ref, g_ref, t_ref, loss_ref, dx_ref, dxb_ref, dg_ref):
        @pl.when(pl.program_id(0) == 0)
        def _():
            loss_ref[...] = jnp.zeros_like(loss_ref)
            dg_ref[...] = jnp.zeros_like(dg_ref)

        xv = x_ref[...]
        gv = g_ref[...]
        rs = lax.rsqrt(jnp.mean(xv * xv, axis=-1, keepdims=True) + EPS)
        xn = xv * rs
        err = xn * gv - t_ref[...]
        loss_ref[...] += 0.5 * jnp.sum(jnp.mean(err * err, axis=-1, keepdims=True), axis=0, keepdims=True)
        dout = err * (1.0 / Dm)
        dg_ref[...] += jnp.sum(dout * xn, axis=0, keepdims=True)
        dxn = dout * gv
        dx = rs * (dxn - xn * jnp.mean(dxn * xn, axis=-1, keepdims=True))
        dx_ref[...] = dx
        dxb_ref[...] = dx.astype(BF16)

    row = pl.BlockSpec((tr, Dm), lambda i: (i, 0))
    return pl.pallas_call(
        body, grid=(S // tr,),
        in_specs=[row, pl.BlockSpec((1, Dm), lambda i: (0, 0)), row],
        out_specs=[pl.BlockSpec((1, 1), lambda i: (0, 0)), row, row, pl.BlockSpec((1, Dm), lambda i: (0, 0))],
        out_shape=[jax.ShapeDtypeStruct((1, 1), F32), jax.ShapeDtypeStruct((S, Dm), F32),
                   jax.ShapeDtypeStruct((S, Dm), BF16), jax.ShapeDtypeStruct((1, Dm), F32)],
        name=name, compiler_params=_cp(("arbitrary",)))(x, gain.reshape(1, Dm), target)


def _rope_tables(positions):
    half = ROPE_DIM // 2
    inv_freq = ROPE_THETA ** (-2.0 * jnp.arange(half, dtype=F32) / ROPE_DIM)
    ang = positions.astype(F32)[:, None] * inv_freq
    cos, sin = jnp.cos(ang), jnp.sin(ang)
    S = positions.shape[0]
    ones = jnp.ones((S, HEAD_DIM - ROPE_DIM), F32)
    cos64 = jnp.concatenate([cos, cos, ones], axis=1)
    sin64 = jnp.concatenate([-sin, sin, 0.0 * ones], axis=1)
    return jnp.tile(cos64, (1, 2)), jnp.tile(sin64, (1, 2))


def _rope_partner(t):
    lane = lax.broadcasted_iota(jnp.int32, t.shape, 1)
    low = (lane & (HEAD_DIM - 1)) < (ROPE_DIM // 2)
    return jnp.where(low, pltpu.roll(t, LANES - ROPE_DIM // 2, 1), pltpu.roll(t, ROPE_DIM // 2, 1))


def _qk_prep(p, cos_t, sin_t, *, name="qk_prep", tr=256, gather=None):
    S = p.shape[0]
    tr = min(tr, S)
    scale = HEAD_DIM ** -0.5
    host = _Hosted(_gather_rider(gather), 3, 4)

    def body(*refs):
        ins, outs, _, gref = host.split(refs, 3, 4)
        host.run(gref, pl.program_id(0), S // tr, lambda: inner(*ins, *outs))

    def inner(p_ref, c_ref, s_ref, q_ref, k_ref, v_ref, va_ref):
        cs, sn = c_ref[...], s_ref[...]
        lane = lax.broadcasted_iota(jnp.int32, (tr, LANES), 1)
        lo = lane < HEAD_DIM
        for c in range(Q_W // LANES):
            t = p_ref[:, c * LANES:(c + 1) * LANES]
            q_ref[:, c * LANES:(c + 1) * LANES] = ((t * cs + _rope_partner(t) * sn) * scale).astype(BF16)
        for c in range(KV_W // LANES):
            t = p_ref[:, Q_W + c * LANES:Q_W + (c + 1) * LANES]
            kc = t * cs + _rope_partner(t) * sn
            vc = p_ref[:, Q_W + KV_W + c * LANES:Q_W + KV_W + (c + 1) * LANES]
            for arr, ref in ((kc, k_ref), (vc, v_ref)):
                sw = pltpu.roll(arr, HEAD_DIM, 1)
                ref[:, (2 * c) * LANES:(2 * c + 1) * LANES] = jnp.where(lo, arr, sw).astype(BF16)
                ref[:, (2 * c + 1) * LANES:(2 * c + 2) * LANES] = jnp.where(lo, sw, arr).astype(BF16)
            sw = pltpu.roll(vc, HEAD_DIM, 1)
            for k, aug in enumerate((jnp.where(lo, vc, 1.0), jnp.where(lo, 1.0, sw), jnp.where(lo, sw, 1.0), jnp.where(lo, 1.0, vc))):
                va_ref[:, (4 * c + k) * LANES:(4 * c + k + 1) * LANES] = aug.astype(BF16)

    row = lambda w: pl.BlockSpec((tr, w), lambda i: (i, 0))
    return pl.pallas_call(
        body, grid=(S // tr,), in_specs=[row(ATTN_IN_W), row(LANES), row(LANES)] + host.in_specs,
        out_specs=[row(Q_W), row(2 * KV_W), row(2 * KV_W), row(4 * KV_W)] + host.out_specs,
        out_shape=[jax.ShapeDtypeStruct((S, Q_W), BF16), jax.ShapeDtypeStruct((S, 2 * KV_W), BF16),
                   jax.ShapeDtypeStruct((S, 2 * KV_W), BF16), jax.ShapeDtypeStruct((S, 4 * KV_W), BF16)] + host.out_shape,
        scratch_shapes=host.scratch, input_output_aliases=host.alias,
        name=name, compiler_params=_cp(("arbitrary",) if host.on else ("parallel",)))(p, cos_t, sin_t, *host.args)


def _qk_prep_bwd(dq, dk, dv, dmq, cos_t, sin_t, *, name="qk_prep_bwd", tr=256):
    S = dq.shape[0]
    tr = min(tr, S)

    def body(dq_ref, dk_ref, dv_ref, dmq_ref, c_ref, s_ref, o_ref):
        cs, sn = c_ref[...], s_ref[...]
        for c in range(Q_W // LANES):
            t = dq_ref[:, c * LANES:(c + 1) * LANES]
            o_ref[:, c * LANES:(c + 1) * LANES] = (t * cs - _rope_partner(t) * sn).astype(BF16)
        for c in range(KV_W // LANES):
            t = dk_ref[:, c * LANES:(c + 1) * LANES]
            o_ref[:, Q_W + c * LANES:Q_W + (c + 1) * LANES] = (t * cs - _rope_partner(t) * sn).astype(BF16)
        o_ref[:, Q_W + KV_W:Q_W + 2 * KV_W] = dv_ref[...].astype(BF16)
        o_ref[:, Q_W + 2 * KV_W:] = dmq_ref[...]

    row = lambda w: pl.BlockSpec((tr, w), lambda i: (i, 0))
    return pl.pallas_call(
        body, grid=(S // tr,), in_specs=[row(Q_W), row(KV_W), row(KV_W), row(MEM_W), row(LANES), row(LANES)],
        out_specs=row(ATTN_IN_W), out_shape=jax.ShapeDtypeStruct((S, ATTN_IN_W), BF16),
        name=name, compiler_params=_cp(("parallel",)))(dq, dk, dv, dmq, cos_t, sin_t)


def _band(n, S):
    start = pl.multiple_of(jnp.clip((n - 1) * BLOCK, 0, S - 3 * BLOCK), BLOCK)
    qi = lax.broadcasted_iota(jnp.int32, (BLOCK, 3 * BLOCK), 0) + n * BLOCK
    ki = lax.broadcasted_iota(jnp.int32, (BLOCK, 3 * BLOCK), 1) + start
    return start, jnp.abs(ki - qi) <= WINDOW


def _head_operand(ref, h, lo):
    c = h // 2
    t = ref[:, c * LANES:(c + 1) * LANES].astype(F32)
    return jnp.where(lo if h % 2 == 0 else jnp.logical_not(lo), t, 0.0).astype(BF16)


GROUP = ATTN_HEADS // ATTN_KV_HEADS
EVENS_FIRST = (0, 2, 1, 3)


def _attn_fwd(q, kd, va, sinks, *, name="attn_fwd", gather=None):
    S = q.shape[0]
    host = _Hosted(_gather_rider(gather), 4, 2)

    def body(*refs):
        ins, outs, scr, gref = host.split(refs, 4, 2)
        host.run(gref, pl.program_id(0), S // BLOCK, lambda: inner(*ins, *outs, *scr))

    def inner(sink_ref, q_ref, k_ref, va_ref, o_ref, lse_ref, p_scr):
        n = pl.program_id(0)
        start, mask = _band(n, S)
        lane = lax.broadcasted_iota(jnp.int32, (BLOCK, LANES), 1)
        lo = lane < HEAD_DIM
        rows = pl.ds(start, 3 * BLOCK)
        scores = []
        for g in range(ATTN_KV_HEADS):
            qst = jnp.concatenate([_head_operand(q_ref, GROUP * g + j, lo) for j in EVENS_FIRST], axis=0)
            scores.append(lax.dot_general(qst, k_ref[rows, g * LANES:(g + 1) * LANES], NT, preferred_element_type=F32))
        ms = {}
        for g in range(ATTN_KV_HEADS):
            for pos, j in enumerate(EVENS_FIRST):
                h = GROUP * g + j
                s = jnp.where(mask, scores[g][pos * BLOCK:(pos + 1) * BLOCK], NEG)
                ms[h] = jnp.maximum(jnp.max(s, axis=-1, keepdims=True), sink_ref[h])
                p_scr[(GROUP * g + pos) * BLOCK:(GROUP * g + pos + 1) * BLOCK, :] = jnp.exp(s - ms[h]).astype(BF16)
        pvs = {}
        for g in range(ATTN_KV_HEADS):
            for par in range(2):
                r0 = (GROUP * g + 2 * par) * BLOCK
                pvs[g, par] = jnp.dot(p_scr[r0:r0 + 2 * BLOCK, :], va_ref[rows, (2 * g + par) * LANES:(2 * g + par + 1) * LANES],
                                      preferred_element_type=F32)
        lse_blk = jnp.zeros((BLOCK, LANES), F32)
        for g in range(ATTN_KV_HEADS):
            outs = {}
            for par in range(2):
                for k in range(2):
                    j = EVENS_FIRST[2 * par + k]
                    h = GROUP * g + j
                    pv = pvs[g, par][k * BLOCK:(k + 1) * BLOCK]
                    den = pltpu.roll(pv, HEAD_DIM, 1) + jnp.exp(sink_ref[h] - ms[h])
                    outs[j] = pv * (1.0 / den)
                    l = den[:, par * HEAD_DIM:par * HEAD_DIM + 1]
                    lse_blk = jnp.where(lane == h, ms[h] + jnp.log(l), lse_blk)
            for jj in range(2):
                o_ref[:, (2 * g + jj) * LANES:(2 * g + jj + 1) * LANES] = jnp.where(lo, outs[2 * jj], outs[2 * jj + 1]).astype(BF16)
        lse_ref[...] = lse_blk

    full = lambda w: pl.BlockSpec((S, w), lambda i: (0, 0))
    return pl.pallas_call(
        body, grid=(S // BLOCK,),
        in_specs=[pl.BlockSpec(memory_space=pltpu.SMEM), pl.BlockSpec((BLOCK, Q_W), lambda i: (i, 0)),
                  full(2 * KV_W), full(4 * KV_W)] + host.in_specs,
        out_specs=[pl.BlockSpec((BLOCK, Q_W), lambda i: (i, 0)), pl.BlockSpec((BLOCK, LANES), lambda i: (i, 0))] + host.out_specs,
        out_shape=[jax.ShapeDtypeStruct((S, Q_W), BF16), jax.ShapeDtypeStruct((S, LANES), F32)] + host.out_shape,
        scratch_shapes=[pltpu.VMEM((ATTN_HEADS * BLOCK, 3 * BLOCK), BF16)] + host.scratch, input_output_aliases=host.alias,
        name=name, compiler_params=_cp(("arbitrary",) if host.on else ("parallel",)))(sinks, q, kd, va, *host.args)


def _attn_bwd(q, kd, vd, ao, lse, sinks, dcat, *, name="attn_bwd", rider=None):
    S = q.shape[0]
    scale = HEAD_DIM ** -0.5
    host = _Hosted(rider, 7, 4)

    def body(*refs):
        ins, outs, scr, rrefs = host.split(refs, 7, 4)
        host.run(rrefs, pl.program_id(0), S // BLOCK, lambda: inner(*ins, *outs, *scr))

    def inner(sink_ref, q_ref, k_ref, v_ref, ao_ref, lse_ref, do_ref, dq_ref, dk_ref, dv_ref, ds_ref, p_scr, dsb_scr):
        n = pl.program_id(0)

        @pl.when(n == 0)
        def _():
            dk_ref[...] = jnp.zeros_like(dk_ref)
            dv_ref[...] = jnp.zeros_like(dv_ref)
            ds_ref[...] = jnp.zeros_like(ds_ref)

        start, mask = _band(n, S)
        lane = lax.broadcasted_iota(jnp.int32, (BLOCK, LANES), 1)
        lo = lane < HEAD_DIM
        lane3 = lax.broadcasted_iota(jnp.int32, (3 * BLOCK, LANES), 1)
        row8 = lax.broadcasted_iota(jnp.int32, (8, LANES), 0)
        lane8 = lax.broadcasted_iota(jnp.int32, (8, LANES), 1)
        dsink = jnp.zeros((8, LANES), F32)
        lse_blk = lse_ref[...]
        rows = pl.ds(start, 3 * BLOCK)
        lses, deltas = {}, {}
        for c in range(Q_W // LANES):
            prod = do_ref[:, c * LANES:(c + 1) * LANES].astype(F32) * ao_ref[:, c * LANES:(c + 1) * LANES].astype(F32)
            for k in range(2):
                h = 2 * c + k
                deltas[h] = jnp.sum(jnp.where(lo if k == 0 else jnp.logical_not(lo), prod, 0.0), axis=1, keepdims=True)
                lses[h] = jnp.sum(jnp.where(lane == h, lse_blk, 0.0), axis=1, keepdims=True)
                val = -jnp.sum(jnp.exp(sink_ref[h] - lses[h]) * deltas[h], axis=0, keepdims=True)
                dsink = dsink + jnp.where((row8 == 0) & (lane8 == h), val, 0.0)
        stack = lambda ref, g: jnp.concatenate([_head_operand(ref, GROUP * g + j, lo) for j in range(GROUP)], axis=0)
        ss, dps = [], []
        for g in range(ATTN_KV_HEADS):
            ss.append(lax.dot_general(stack(q_ref, g), k_ref[rows, g * LANES:(g + 1) * LANES], NT, preferred_element_type=F32))
            dps.append(lax.dot_general(stack(do_ref, g), v_ref[rows, g * LANES:(g + 1) * LANES], NT, preferred_element_type=F32))
        for g in range(ATTN_KV_HEADS):
            for j in range(GROUP):
                h = GROUP * g + j
                r = slice(j * BLOCK, (j + 1) * BLOCK)
                hr = slice(h * BLOCK, (h + 1) * BLOCK)
                p = jnp.exp(jnp.where(mask, ss[g][r], NEG) - lses[h])
                p_scr[hr, :] = p.astype(BF16)
                dsb_scr[hr, :] = (p * (dps[g][r] - deltas[h])).astype(BF16)
        for g in range(ATTN_KV_HEADS):
            cols = slice((g // 2) * LANES, (g // 2 + 1) * LANES)
            gr = slice(GROUP * g * BLOCK, GROUP * (g + 1) * BLOCK)
            dsg = dsb_scr[gr, :]
            dqs = jnp.dot(dsg, k_ref[rows, g * LANES:(g + 1) * LANES], preferred_element_type=F32) * scale
            for jj in range(2):
                dq_ref[:, (2 * g + jj) * LANES:(2 * g + jj + 1) * LANES] = jnp.where(
                    lo, dqs[(2 * jj) * BLOCK:(2 * jj + 1) * BLOCK], dqs[(2 * jj + 1) * BLOCK:(2 * jj + 2) * BLOCK])
            half = (lane3 < HEAD_DIM) if g % 2 == 0 else (lane3 >= HEAD_DIM)
            dkr = lax.dot_general(dsg, stack(q_ref, g), TN, preferred_element_type=F32)
            dk_ref[rows, cols] += jnp.where(half, dkr + pltpu.roll(dkr, HEAD_DIM, 1), 0.0)
            dvr = lax.dot_general(p_scr[gr, :], stack(do_ref, g), TN, preferred_element_type=F32)
            dv_ref[rows, cols] += jnp.where(half, dvr + pltpu.roll(dvr, HEAD_DIM, 1), 0.0)
        ds_ref[...] += dsink

    full = lambda w: pl.BlockSpec((S, w), lambda i: (0, 0))
    blk = lambda w: pl.BlockSpec((BLOCK, w), lambda i: (i, 0))
    return pl.pallas_call(
        body, grid=(S // BLOCK,),
        in_specs=[pl.BlockSpec(memory_space=pltpu.SMEM), blk(Q_W), full(2 * KV_W), full(2 * KV_W), blk(Q_W), blk(LANES), blk(Q_W)]
        + host.in_specs,
        out_specs=[blk(Q_W), full(KV_W), full(KV_W), pl.BlockSpec((8, LANES), lambda i: (0, 0))] + host.out_specs,
        out_shape=[jax.ShapeDtypeStruct((S, Q_W), F32), jax.ShapeDtypeStruct((S, KV_W), F32),
                   jax.ShapeDtypeStruct((S, KV_W), F32), jax.ShapeDtypeStruct((8, LANES), F32)] + host.out_shape,
        scratch_shapes=[pltpu.VMEM((ATTN_HEADS * BLOCK, 3 * BLOCK), BF16), pltpu.VMEM((ATTN_HEADS * BLOCK, 3 * BLOCK), BF16)]
        + host.scratch, input_output_aliases=host.alias,
        name=name, compiler_params=_cp(("arbitrary",)))(sinks, q, kd, vd, ao, lse, dcat, *host.args)


def _mem_probs(q_ref, kv_ref, h):
    scale = MEM_HEAD_DIM ** -0.5
    qh = q_ref[:, h * LANES:(h + 1) * LANES].astype(BF16)
    s = lax.dot_general(qh, kv_ref[:, h * LANES:(h + 1) * LANES], NT, preferred_element_type=F32) * scale
    m = jnp.max(s, axis=-1, keepdims=True)
    pe = jnp.exp(s - m)
    return qh, pe * (1.0 / jnp.sum(pe, axis=-1, keepdims=True))


def _memattn_fwd(p, qblk, kv, *, name="memattn_fwd", tr=512):
    S = p.shape[0]
    tr = min(tr, S)

    def body(q_ref, kv_ref, o_ref):
        for h in range(MEM_HEADS):
            _, pr = _mem_probs(q_ref, kv_ref, h)
            o = jnp.dot(pr.astype(BF16), kv_ref[:, MEM_W + h * LANES:MEM_W + (h + 1) * LANES], preferred_element_type=F32)
            o_ref[:, h * LANES:(h + 1) * LANES] = o.astype(BF16)

    return pl.pallas_call(
        body, grid=(S // tr,),
        in_specs=[pl.BlockSpec((tr, MEM_W), lambda i: (i, qblk)), pl.BlockSpec((MEM_LEN, 2 * MEM_W), lambda i: (0, 0))],
        out_specs=pl.BlockSpec((tr, MEM_W), lambda i: (i, 0)),
        out_shape=jax.ShapeDtypeStruct((S, MEM_W), BF16), name=name, compiler_params=_cp(("parallel",)))(p, kv)


def _memattn_bwd(p, qblk, kv, dcat, *, name="memattn_bwd", tr=512):
    S = p.shape[0]
    tr = min(tr, S)
    scale = MEM_HEAD_DIM ** -0.5

    def body(q_ref, kv_ref, do_ref, dq_ref, dkv_ref):
        @pl.when(pl.program_id(0) == 0)
        def _():
            dkv_ref[...] = jnp.zeros_like(dkv_ref)

        for h in range(MEM_HEADS):
            qh, pr = _mem_probs(q_ref, kv_ref, h)
            doh = do_ref[:, h * LANES:(h + 1) * LANES]
            dp = lax.dot_general(doh, kv_ref[:, MEM_W + h * LANES:MEM_W + (h + 1) * LANES], NT, preferred_element_type=F32)
            delta = jnp.sum(pr * dp, axis=-1, keepdims=True)
            dsb = (pr * (dp - delta) * scale).astype(BF16)
            dq = jnp.dot(dsb, kv_ref[:, h * LANES:(h + 1) * LANES], preferred_element_type=F32)
            dq_ref[:, h * LANES:(h + 1) * LANES] = dq.astype(BF16)
            dkv_ref[:, h * LANES:(h + 1) * LANES] += lax.dot_general(dsb, qh, TN, preferred_element_type=F32)
            dkv_ref[:, MEM_W + h * LANES:MEM_W + (h + 1) * LANES] += lax.dot_general(
                pr.astype(BF16), doh, TN, preferred_element_type=F32)

    return pl.pallas_call(
        body, grid=(S // tr,),
        in_specs=[pl.BlockSpec((tr, MEM_W), lambda i: (i, qblk)), pl.BlockSpec((MEM_LEN, 2 * MEM_W), lambda i: (0, 0)),
                  pl.BlockSpec((tr, MEM_W), lambda i: (i, Q_W // MEM_W))],
        out_specs=[pl.BlockSpec((tr, MEM_W), lambda i: (i, 0)), pl.BlockSpec((MEM_LEN, 2 * MEM_W), lambda i: (0, 0))],
        out_shape=[jax.ShapeDtypeStruct((S, MEM_W), BF16), jax.ShapeDtypeStruct((MEM_LEN, 2 * MEM_W), F32)],
        name=name, compiler_params=_cp(("arbitrary",)))(p, kv, dcat)


def _sqrt(v):
    return jnp.where(v > 0.0, v * lax.rsqrt(v), 0.0)


def _sigmoid(z):
    return 1.0 / (1.0 + jnp.exp(-z))


def _one_minus_exp(z, exp_z):
    poly = z * (1.0 + z * (0.5 + z * (1.0 / 6.0 + z * (1.0 / 24.0 + z * (1.0 / 120.0)))))
    return jnp.where(z > -0.1, -poly, 1.0 - exp_z)


def _softplus_neg(lam):
    z = -lam
    return jnp.maximum(z, 0.0) + jnp.log(1.0 + jnp.exp(-jnp.abs(z)))


_GELU_C = math.sqrt(2.0 / math.pi)


def _gelu(z):
    return 0.5 * z * (1.0 + jnp.tanh(_GELU_C * (z + 0.044715 * z * z * z)))


def _row_or_zero(ref, t, S):
    ok = jnp.logical_and(t >= 0, t < S)
    return jnp.where(ok, ref[pl.ds(jnp.clip(t, 0, S - 1), 1), :], 0.0)


def _shift_down(v, first):
    ri = lax.broadcasted_iota(jnp.int32, v.shape, 0)
    return jnp.where(ri == 0, first, pltpu.roll(v, 1, 0))


def _shift_up(v, last):
    T = v.shape[0]
    ri = lax.broadcasted_iota(jnp.int32, v.shape, 0)
    return jnp.where(ri == T - 1, last, pltpu.roll(v, T - 1, 0))


def _scan_chunk(a, u, reverse):
    T = a.shape[0]
    ri = lax.broadcasted_iota(jnp.int32, a.shape, 0)
    d = 1
    while d < T:
        if reverse:
            a_s, u_s, ok = pltpu.roll(a, T - d, 0), pltpu.roll(u, T - d, 0), ri < T - d
        else:
            a_s, u_s, ok = pltpu.roll(a, d, 0), pltpu.roll(u, d, 0), ri >= d
        u = jnp.where(ok, a * u_s + u, u)
        a = jnp.where(ok, a * a_s, a)
        d *= 2
    return a, u


def _conv_taps(xb_ref, t0, S):
    T = SCAN_ROWS
    x0 = xb_ref[pl.ds(t0, T), :]
    xm1 = _shift_down(x0, _row_or_zero(xb_ref, t0 - 1, S))
    nxt0 = _row_or_zero(xb_ref, t0 + T, S)
    xp1 = _shift_up(x0, nxt0)
    xp2 = _shift_up(xp1, _row_or_zero(xb_ref, t0 + T + 1, S))
    return xm1, x0, xp1, xp2


def _lru_gates(xc, w_a, b_a, w_x, b_x, sp):
    xcb = xc.astype(BF16)
    r = _sigmoid(jnp.dot(xcb, w_a, preferred_element_type=F32) + b_a)
    i = _sigmoid(jnp.dot(xcb, w_x, preferred_element_type=F32) + b_x)
    la = -LRU_C * r * sp
    a = jnp.exp(la)
    return r, i, a, _sqrt(_one_minus_exp(2.0 * la, a * a))


def _lru_specs(S):
    col = lambda off: pl.BlockSpec((S, LANES), lambda n: (0, n + off), pipeline_mode=pl.Buffered(1))
    small = lambda r: pl.BlockSpec((r, LANES), lambda n: (0, n))
    wblk = pl.BlockSpec((2, 1, LANES, LANES), lambda n: (0, n, 0, 0))
    return col, small, wblk


def _lru_fwd(p, conv_w, conv_b, wa, ba, wx, bx, lam, *, name="lru_fwd"):
    S = p.shape[0]
    T = SCAN_ROWS
    nc = S // T

    def body(xb_ref, gate_ref, cw_ref, cb_ref, wa_ref, ba_ref, wx_ref, bx_ref, lam_ref, y_ref, hf_ref, hr_ref, xc_v):
        sp = _softplus_neg(lam_ref[...])
        cw = cw_ref[...]

        def fwd_step(c, h_in):
            t0 = pl.multiple_of(c * T, T)
            xm1, x0, xp1, xp2 = _conv_taps(xb_ref, t0, S)
            xc = cb_ref[...] + xm1 * cw[0:1] + x0 * cw[1:2] + xp1 * cw[2:3] + xp2 * cw[3:4]
            xc_v[pl.ds(t0, T), :] = xc
            _, i, a, beta = _lru_gates(xc, wa_ref[0, 0], ba_ref[0:1], wx_ref[0, 0], bx_ref[0:1], sp[0:1])
            A, U = _scan_chunk(a, beta * (i * xc), False)
            hf_ref[pl.ds(t0, T), :] = A * h_in + U
            return hf_ref[pl.ds(t0 + T - 1, 1), :]

        lax.fori_loop(0, nc, fwd_step, jnp.zeros((1, LANES), F32))

        def rev_step(k, h_in):
            t0 = pl.multiple_of((nc - 1 - k) * T, T)
            xc = xc_v[pl.ds(t0, T), :]
            _, i, a, beta = _lru_gates(xc, wa_ref[1, 0], ba_ref[1:2], wx_ref[1, 0], bx_ref[1:2], sp[1:2])
            A, U = _scan_chunk(a, beta * (i * xc), True)
            h = A * h_in + U
            hr_ref[pl.ds(t0, T), :] = h
            y_ref[pl.ds(t0, T), :] = ((hf_ref[pl.ds(t0, T), :] + h) * _gelu(gate_ref[pl.ds(t0, T), :])).astype(BF16)
            return hr_ref[pl.ds(t0, 1), :]

        lax.fori_loop(0, nc, rev_step, jnp.zeros((1, LANES), F32))

    col, small, wblk = _lru_specs(S)
    colo = lambda: pl.BlockSpec((S, LANES), lambda n: (0, n))
    return pl.pallas_call(
        body, grid=(LRU_BLOCKS,),
        in_specs=[col(0), col(LRU_BLOCKS), small(4), small(1), wblk, small(2), wblk, small(2), small(2)],
        out_specs=[colo(), colo(), colo()],
        out_shape=[jax.ShapeDtypeStruct((S, D_MODEL), BF16), jax.ShapeDtypeStruct((S, D_MODEL), F32),
                   jax.ShapeDtypeStruct((S, D_MODEL), F32)],
        scratch_shapes=[pltpu.VMEM((S, LANES), F32)],
        name=name, compiler_params=_cp(("parallel",)))(p, p, conv_w, conv_b, wa, ba, wx, bx, lam)


def _lru_bwd(p, hf, hr, dcat, conv_w, conv_b, wa, ba, wx, bx, lam, *, name="lru_bwd"):
    S = p.shape[0]
    T = SCAN_ROWS
    nc = S // T

    def body(xb_ref, gate_ref, hf_ref, hr_ref, dy_ref, cw_ref, cb_ref, wa_ref, ba_ref, wx_ref, bx_ref, lam_ref,
             dxb_ref, dgate_ref, dcw_ref, dcb_ref, dwa_ref, dba_ref, dwx_ref, dbx_ref, dlam_ref, xc_v, dxc_v, dh_v):
        lam_v = lam_ref[...]
        sp = _softplus_neg(lam_v)
        cw = cw_ref[...]
        for ref in (dcw_ref, dcb_ref, dwa_ref, dba_ref, dwx_ref, dbx_ref, dlam_ref):
            ref[...] = jnp.zeros_like(ref)

        def prep_step(c, carry):
            t0 = pl.multiple_of(c * T, T)
            rows = pl.ds(t0, T)
            xm1, x0, xp1, xp2 = _conv_taps(xb_ref, t0, S)
            xc_v[rows, :] = cb_ref[...] + xm1 * cw[0:1] + x0 * cw[1:2] + xp1 * cw[2:3] + xp2 * cw[3:4]
            z = gate_ref[rows, :]
            dy = dy_ref[rows, :].astype(F32)
            th = jnp.tanh(_GELU_C * (z + 0.044715 * z * z * z))
            dgelu = 0.5 * (1.0 + th) + 0.5 * z * (1.0 - th * th) * _GELU_C * (1.0 + 3.0 * 0.044715 * z * z)
            dgate_ref[rows, :] = (dy * (hf_ref[rows, :] + hr_ref[rows, :]) * dgelu).astype(BF16)
            dh_v[rows, :] = dy * (0.5 * z * (1.0 + th))
            return carry

        lax.fori_loop(0, nc, prep_step, 0)

        def direction(d):
            h_ref = hf_ref if d == 0 else hr_ref
            w_a, w_x = wa_ref[d, 0], wx_ref[d, 0]
            b_a, b_x, sp_d = ba_ref[d:d + 1], bx_ref[d:d + 1], sp[d:d + 1]

            def step(k, carry):
                g_in, a_in = carry
                c = (nc - 1 - k) if d == 0 else k
                t0 = pl.multiple_of(c * T, T)
                rows = pl.ds(t0, T)
                xc = xc_v[rows, :]
                r, i, a, beta = _lru_gates(xc, w_a, b_a, w_x, b_x, sp_d)
                dh = dh_v[rows, :]
                hc = h_ref[rows, :]
                if d == 0:
                    A, U = _scan_chunk(_shift_up(a, a_in), dh, True)
                    g = A * g_in + U
                    h_nb = _shift_down(hc, _row_or_zero(h_ref, t0 - 1, S))
                    nxt = (g[0:1], a[0:1])
                else:
                    A, U = _scan_chunk(_shift_down(a, a_in), dh, False)
                    g = A * g_in + U
                    h_nb = _shift_up(hc, _row_or_zero(h_ref, t0 + T, S))
                    nxt = (g[T - 1:T], a[T - 1:T])
                da = g * h_nb
                dbeta = g * (i * xc)
                tb = g * beta
                dla = da * a - dbeta * (a * a / beta)
                dzr = (dla * (-LRU_C * sp_d)) * (r * (1.0 - r))
                dzi = (tb * xc) * (i * (1.0 - i))
                dzrb, dzib, xcb = dzr.astype(BF16), dzi.astype(BF16), xc.astype(BF16)
                dwa_ref[d, 0] += lax.dot_general(xcb, dzrb, TN, preferred_element_type=F32)
                dwx_ref[d, 0] += lax.dot_general(xcb, dzib, TN, preferred_element_type=F32)
                dba_ref[d:d + 1] += jnp.sum(dzr, axis=0, keepdims=True)
                dbx_ref[d:d + 1] += jnp.sum(dzi, axis=0, keepdims=True)
                dlam_ref[d:d + 1] += jnp.sum(dla * (-LRU_C * r), axis=0, keepdims=True)
                dxc = (tb * i + lax.dot_general(dzrb, w_a, NT, preferred_element_type=F32)
                       + lax.dot_general(dzib, w_x, NT, preferred_element_type=F32))
                if d == 0:
                    dxc_v[rows, :] = dxc
                else:
                    dxc_v[rows, :] += dxc
                return nxt

            lax.fori_loop(0, nc, step, (jnp.zeros((1, LANES), F32), jnp.zeros((1, LANES), F32)))

        direction(0)
        direction(1)
        dlam_ref[...] = dlam_ref[...] * (-1.0 / (1.0 + jnp.exp(lam_v)))

        def conv_step(c, carry):
            t0 = pl.multiple_of(c * T, T)
            rows = pl.ds(t0, T)
            g0 = dxc_v[rows, :]
            gm1 = _shift_down(g0, _row_or_zero(dxc_v, t0 - 1, S))
            gm2 = _shift_down(gm1, _row_or_zero(dxc_v, t0 - 2, S))
            gp1 = _shift_up(g0, _row_or_zero(dxc_v, t0 + T, S))
            dxb_ref[rows, :] = (cw[0:1] * gp1 + cw[1:2] * g0 + cw[2:3] * gm1 + cw[3:4] * gm2).astype(BF16)
            xm1, x0, xp1, xp2 = _conv_taps(xb_ref, t0, S)
            for tap, xs in enumerate((xm1, x0, xp1, xp2)):
                dcw_ref[tap:tap + 1] += jnp.sum(g0 * xs, axis=0, keepdims=True)
            dcb_ref[...] += jnp.sum(g0, axis=0, keepdims=True)
            return carry

        lax.fori_loop(0, nc, conv_step, 0)

    col, small, wblk = _lru_specs(S)
    colo = lambda: pl.BlockSpec((S, LANES), lambda n: (0, n), pipeline_mode=pl.Buffered(1))
    return pl.pallas_call(
        body, grid=(LRU_BLOCKS,),
        in_specs=[col(0), col(LRU_BLOCKS), col(0), col(0), col(0), small(4), small(1), wblk, small(2), wblk, small(2), small(2)],
        out_specs=[colo(), colo(), small(4), small(1), wblk, small(2), wblk, small(2), small(2)],
        out_shape=[jax.ShapeDtypeStruct((S, D_MODEL), BF16), jax.ShapeDtypeStruct((S, D_MODEL), BF16),
                   jax.ShapeDtypeStruct((4, D_MODEL), F32), jax.ShapeDtypeStruct((1, D_MODEL), F32),
                   jax.ShapeDtypeStruct((2, LRU_BLOCKS, LANES, LANES), F32), jax.ShapeDtypeStruct((2, D_MODEL), F32),
                   jax.ShapeDtypeStruct((2, LRU_BLOCKS, LANES, LANES), F32), jax.ShapeDtypeStruct((2, D_MODEL), F32),
                   jax.ShapeDtypeStruct((2, D_MODEL), F32)],
        scratch_shapes=[pltpu.VMEM((S, LANES), F32), pltpu.VMEM((S, LANES), F32), pltpu.VMEM((S, LANES), F32)],
        name=name, compiler_params=_cp(("parallel",)))(p, p, hf, hr, dcat, conv_w, conv_b, wa, ba, wx, bx, lam)


def _mlp_fwd(x, w_up, w_down, gain, l):
    up, act, h = _mm_nn(x, w_up, norm_g=gain, relu2=True, name=f"mlp_up{l}")
    return _mm_nn(act, w_down, resid=x, name=f"mlp_down{l}"), (up, act, h)


PK_UP, PK_DOWN, PK_KV, PK_OUT, PK_IN = 0, 1024, 2048, 2304, 2688
SMALL_G_ROWS = 192
PK_SMALL = PK_IN + 512
PK_ROWS = {0: PK_SMALL + SMALL_G_ROWS, 1: PK_IN + 640}


def _mlp_bwd(x, dx, dxb, saved, w_up, w_down, gain, l):
    up, act, h = saved
    pk = _mm_tn(act, dxb, 1, name=f"dw_down{l}", packed=(None, PK_ROWS[l], PK_DOWN))
    dup = _mm_nt(dxb, w_down, up=up, name=f"d_up{l}")
    pk = _mm_tn(h, dup, N_CHIPS, name=f"dw_up{l}", packed=(pk, PK_ROWS[l], PK_UP))
    dx, dxb, g_gain = _mm_nt(dup, w_up, norm_x=x, norm_g=gain, dres=dx, name=f"d_mlp_in{l}")
    return dx, dxb, pk, g_gain


def _reduce_first(pk, place, tag):
    recv = _sibling_exchange(pk, name=f"grad_sibling_exchange{tag}")
    return _sum_halves(pk, recv, place, name=f"sum_halves{tag}", tr=pk.shape[1] // 4)


def _reduce_last(parts, place, tag):
    return _sibling_allgather(_sum_chips(parts, place, name=f"sum_chips{tag}", tr=parts.shape[1] // 2),
                              name=f"grad_sibling_allgather{tag}")


def _local_step(x, mem, positions, target, W, pending=None, place=None):
    cos_t, sin_t = _rope_tables(positions)
    sinks = W["attn_sinks"].reshape(ATTN_HEADS)
    G = {}

    def hosting(late, fn, *args, **kw):
        if pending is None:
            return fn(*args, **kw)
        *res, buf = fn(*args, gather=pending[late], **kw)
        if late.startswith("w_down"):
            W.setdefault("w_down", [None] * DEPTH)[int(late[-1])] = _ready(late, buf)
        else:
            W[late] = _ready(late, buf)
        return res if len(res) > 1 else res[0]

    kv0, memn = _mm_nn(mem, W["w_mem_kv"][0], norm_g=W["mem_norm"], out_dtype=BF16, name="mem_kv0", tm=256)
    kv1 = _mm_nn(memn, W["w_mem_kv"][1], out_dtype=BF16, name="mem_kv1", tm=256)
    p0, h0 = hosting("w_out", _mm_nn, x, W["attn_w_in"], norm_g=W["mix_norm"][0], name="attn_in")
    q, kd, vd, va = hosting("lru_w_in", _qk_prep, p0, cos_t, sin_t)
    ao, lse = hosting("w_up", _attn_fwd, q, kd, va, sinks)
    mo0 = _memattn_fwd(p0, Q_W // MEM_W + 1, kv0, name="memattn_fwd0")
    cat0 = jnp.concatenate([ao, mo0], axis=1)
    x1 = hosting("w_down0", _mm_nn, cat0, W["w_out"][0], resid=x, name="mix_out0")
    up0, act0, h1 = hosting("w_down1", _mm_nn, x1, W["w_up"][0], norm_g=W["mlp_norm"][0], relu2=True, name="mlp_up0")
    x2, mlp0 = _mm_nn(act0, W["w_down"][0], resid=x1, name="mlp_down0"), (up0, act0, h1)
    p1, h2 = _mm_nn(x2, W["lru_w_in"], norm_g=W["mix_norm"][1], name="lru_in")
    lru_w = (W["lru_conv_w"], W["lru_conv_b"], W["lru_wa"], W["lru_ba"], W["lru_wx"], W["lru_bx"], W["lru_lambda"])
    y, hf, hr = _lru_fwd(p1, *lru_w)
    mo1 = _memattn_fwd(p1, 2 * D_MODEL // MEM_W, kv1, name="memattn_fwd1")
    cat1 = jnp.concatenate([y, mo1], axis=1)
    x3 = _mm_nn(cat1, W["w_out"][1], resid=x2, name="mix_out1")
    x4, mlp1 = _mlp_fwd(x3, W["w_up"][1], W["w_down"][1], W["mlp_norm"][1], 1)
    loss, dx, dxb, G["final_norm"] = _final(x4, W["final_norm"], target)

    def put(pk, off, g):
        return pk.at[:, off:off + g.size // (N_CHIPS * ROW)].set(g.reshape(N_CHIPS, -1, ROW))

    dx, dxb, pk1, gm1 = _mlp_bwd(x3, dx, dxb, mlp1, W["w_up"][1], W["w_down"][1], W["mlp_norm"][1], 1)
    pk1 = _mm_tn(cat1, dxb, 1, name="dw_out1", tk=384, packed=(pk1, PK_ROWS[1], PK_OUT))
    dcat1 = _mm_nt(dxb, W["w_out"][1], name="d_mix1")
    dmq1, dkv1 = _memattn_bwd(p1, 2 * D_MODEL // MEM_W, kv1, dcat1, name="memattn_bwd1")
    dkv1b = dkv1.astype(BF16)
    pk1 = _mm_tn(memn, dkv1b, 1, name="dw_kv1", tm=256, tk=256, packed=(pk1, PK_ROWS[1], PK_KV))
    (dxb1, dgate, G["lru_conv_w"], G["lru_conv_b"], G["lru_wa"], G["lru_ba"], G["lru_wx"], G["lru_bx"],
     G["lru_lambda"]) = _lru_bwd(p1, hf, hr, dcat1, *lru_w)
    dp1 = jnp.concatenate([dxb1, dgate, dmq1], axis=1)
    pk1 = put(pk1, PK_IN, _mm_tn(h2, dp1, N_CHIPS, name="dw_lru_in"))
    dx, dxb, gx1 = _mm_nt(dp1, W["lru_w_in"], norm_x=x2, norm_g=W["mix_norm"][1], dres=dx, name="d_lru_in")
    rider = None
    if place is not None:
        rider = _exchange_rider(*_reduce_first(pk1, place, "1"))

    dx, dxb, pk0, gm0 = _mlp_bwd(x1, dx, dxb, mlp0, W["w_up"][0], W["w_down"][0], W["mlp_norm"][0], 0)
    pk0 = _mm_tn(cat0, dxb, 1, name="dw_out0", tk=384, packed=(pk0, PK_ROWS[0], PK_OUT))
    dcat0 = _mm_nt(dxb, W["w_out"][0], name="d_mix0")
    dmq0, dkv0 = _memattn_bwd(p0, Q_W // MEM_W + 1, kv0, dcat0, name="memattn_bwd0")
    dkv0b = dkv0.astype(BF16)
    pk0 = _mm_tn(memn, dkv0b, 1, name="dw_kv0", tm=256, tk=256, packed=(pk0, PK_ROWS[0], PK_KV))
    dq, dk, dv, dsink, *parts1 = _attn_bwd(q, kd, vd, ao, lse, sinks, dcat0, rider=rider)
    if place is not None:
        pk1 = _reduce_last(parts1[0], place, "1")
    dp0 = _qk_prep_bwd(dq, dk, dv, dmq0, cos_t, sin_t)
    pk0 = put(pk0, PK_IN, _mm_tn(h0, dp0, N_CHIPS, name="dw_attn_in"))
    dx, _, gx0 = _mm_nt(dp0, W["attn_w_in"], norm_x=x, norm_g=W["mix_norm"][0], dres=dx, name="d_attn_in")

    w_kv_both = jnp.concatenate([W["w_mem_kv"][0], W["w_mem_kv"][1]], axis=0)
    _, _, G["mem_norm"] = _mm_nt(jnp.concatenate([dkv0b, dkv1b], axis=1), w_kv_both, norm_x=mem, norm_g=W["mem_norm"],
                                 name="d_mem", tm=256)

    G["mix_norm"] = jnp.concatenate([gx0, gx1], axis=0)
    G["mlp_norm"] = jnp.concatenate([gm0, gm1], axis=0)
    G["attn_sinks"] = dsink[0:1, 0:ATTN_HEADS]
    pk0 = put(pk0, PK_SMALL, _flat_pad(_small_grad_list(G), N_CHIPS * SMALL_G_ROWS * ROW))
    return loss[0, 0], dx, G, pk0, pk1


def _comm_call(body, out_shape, n_sems, name, *args, alias=None):
    return pl.pallas_call(
        body, out_shape=out_shape, in_specs=[HBM] * len(args), out_specs=HBM,
        scratch_shapes=[pltpu.SemaphoreType.DMA((n_sems,)), pltpu.SemaphoreType.DMA((n_sems,))],
        input_output_aliases=alias or {}, name=name)(*args)


def _place_slot(shard, slot, n_slots, *, name, tr):
    R, C = shard.shape

    def body(s_ref, a_ref, o_ref):
        o_ref[0] = a_ref[...]

    return pl.pallas_call(
        body,
        grid_spec=pltpu.PrefetchScalarGridSpec(
            num_scalar_prefetch=1, grid=(R // tr,), in_specs=[pl.BlockSpec((tr, C), lambda i, s_ref: (i, 0))],
            out_specs=pl.BlockSpec((1, tr, C), lambda i, s_ref: (s_ref[0], i, 0))),
        out_shape=jax.ShapeDtypeStruct((n_slots, R, C), shard.dtype), name=name,
        compiler_params=_cp(("parallel",)))(slot, shard)


def _allgather_chips(buf, *, name, forward_to_sibling):
    def body(b_ref, o_ref, send_sems, recv_sems):
        if forward_to_sibling:
            _gather_start(o_ref, send_sems, recv_sems)
            _gather_finish(o_ref, send_sems, recv_sems)
            return
        x, y, c, chips = _place()
        own = o_ref.at[2 * x + y]
        sends = [_remote(own, own, send_sems, recv_sems, j, (cx, cy, c)) for j, (cx, cy) in enumerate(chips)]
        for cp in sends:
            cp.start()
        for j, (cx, cy) in enumerate(chips):
            landed = o_ref.at[2 * cx + cy]
            _remote(landed, landed, send_sems, recv_sems, j, (cx, cy, c)).wait_recv()
        for cp in sends:
            cp.wait_send()

    return _comm_call(body, jax.ShapeDtypeStruct(buf.shape, buf.dtype), GATHER_SEMS, name, buf, alias={0: 0})


def _sibling_exchange(g, *, name):
    _, R, C = g.shape
    half = R // 2

    def body(g_ref, o_ref, send_sems, recv_sems):
        x, y, c, _ = _place()
        other = pl.ds(pl.multiple_of((1 - c) * half, 8), half)
        cps = [_remote(g_ref.at[s, other], o_ref.at[s], send_sems, recv_sems, s, (x, y, 1 - c)) for s in range(N_CHIPS)]
        for cp in cps:
            cp.start()
        for cp in cps:
            cp.wait()

    return _comm_call(body, jax.ShapeDtypeStruct((N_CHIPS, half, C), g.dtype), N_CHIPS, name, g)


def _chip_exchange(h, parts, *, name):
    def body(h_ref, p_ref, o_ref, send_sems, recv_sems):
        x, y, c, chips = _place()
        me = 2 * x + y
        cps = [_remote(h_ref.at[2 * cx + cy], o_ref.at[me], send_sems, recv_sems, j, (cx, cy, c))
               for j, (cx, cy) in enumerate(chips)]
        for cp in cps:
            cp.start()
        for j, (cx, cy) in enumerate(chips):
            got = o_ref.at[2 * cx + cy]
            _remote(got, got, send_sems, recv_sems, j, (cx, cy, c)).wait_recv()
        for cp in cps:
            cp.wait_send()

    return _comm_call(body, jax.ShapeDtypeStruct(parts.shape, parts.dtype), 3, name, h, parts, alias={1: 0})


def _sibling_allgather(full, *, name):
    R, C = full.shape
    half = R // 2

    def body(f_ref, o_ref, send_sems, recv_sems):
        x, y, c, _ = _place()
        mine = o_ref.at[pl.ds(pl.multiple_of(c * half, 8), half)]
        cp = _remote(mine, mine, send_sems, recv_sems, 0, (x, y, 1 - c))
        cp.start()
        got = o_ref.at[pl.ds(pl.multiple_of((1 - c) * half, 8), half)]
        _remote(got, got, send_sems, recv_sems, 0, (x, y, 1 - c)).wait_recv()
        cp.wait_send()

    return _comm_call(body, jax.ShapeDtypeStruct(full.shape, full.dtype), 1, name, full, alias={0: 0})


def _sum_halves(g, recv, place, *, name="sum_halves", tr=480):
    _, R, C = g.shape
    half = R // 2
    nblk = half // tr

    def body(pl_ref, g_ref, r_ref, o_ref, own_ref):
        v = (g_ref[...] + r_ref[...]).astype(BF16)
        o_ref[...] = v

        @pl.when(pl.program_id(1) == pl_ref[1])
        def _():
            own_ref[...] = v

    blk = pl.BlockSpec((1, tr, C), lambda i, s, p: (s, i, 0))
    return pl.pallas_call(
        body,
        grid_spec=pltpu.PrefetchScalarGridSpec(
            num_scalar_prefetch=1, grid=(nblk, N_CHIPS),
            in_specs=[pl.BlockSpec((1, tr, C), lambda i, s, p: (s, p[0] * nblk + i, 0)), blk],
            out_specs=[blk, pl.BlockSpec((1, tr, C), lambda i, s, p: (p[1], i, 0))]),
        out_shape=[jax.ShapeDtypeStruct((N_CHIPS, half, C), BF16)] * 2, name=name,
        compiler_params=_cp(("parallel", "arbitrary")))(place, g, recv)


def _sum_chips(parts, place, *, name="sum_chips", tr=480):
    _, R, C = parts.shape
    nblk = R // tr

    def body(pl_ref, p_ref, o_ref):
        acc = p_ref[0].astype(F32) + p_ref[1].astype(F32)
        o_ref[...] = (acc + p_ref[2].astype(F32)) + p_ref[3].astype(F32)

    return pl.pallas_call(
        body,
        grid_spec=pltpu.PrefetchScalarGridSpec(
            num_scalar_prefetch=1, grid=(nblk,), in_specs=[pl.BlockSpec((N_CHIPS, tr, C), lambda i, p: (0, i, 0))],
            out_specs=pl.BlockSpec((tr, C), lambda i, p: (p[0] * nblk + i, 0))),
        out_shape=jax.ShapeDtypeStruct((2 * R, C), F32), name=name, compiler_params=_cp(("parallel",)))(place, parts)


def _adamw(w, g, m, v, *, name, tr=128):
    R, C = w.shape
    bc1 = 1.0 - ADAM_B1 ** ADAM_STEP
    bc2 = 1.0 - ADAM_B2 ** ADAM_STEP

    def body(w_ref, g_ref, m_ref, v_ref, d_ref, nm_ref, nv_ref):
        gv = g_ref[...]
        nm = ADAM_B1 * m_ref[...] + (1.0 - ADAM_B1) * gv
        nv = ADAM_B2 * v_ref[...] + (1.0 - ADAM_B2) * (gv * gv)
        d_ref[...] = -ADAM_LR * ((nm / bc1) / (_sqrt(nv / bc2) + ADAM_EPS) + ADAM_WD * w_ref[...])
        nm_ref[...] = nm
        nv_ref[...] = nv

    blk = pl.BlockSpec((tr, C), lambda i: (i, 0))
    return pl.pallas_call(
        body, grid=(R // tr,), in_specs=[blk] * 4, out_specs=[blk] * 3,
        out_shape=[jax.ShapeDtypeStruct((R, C), F32)] * 3, name=name, compiler_params=_cp(("parallel",)))(w, g, m, v)


ROW = 1024
BIG = ("w_mem_kv", "w_out", "w_up", "w_down", "attn_w_in", "lru_w_in")
SMALL_SHARDED = ("lru_conv_w", "lru_conv_b", "lru_ba", "lru_bx", "lru_lambda")
REPLICATED = ("mix_norm", "mlp_norm", "mem_norm", "final_norm", "attn_sinks", "lru_wa", "lru_wx")
SMALL = REPLICATED + SMALL_SHARDED
WEIGHTS = ("mix_norm", "mlp_norm", "mem_norm", "final_norm", "w_mem_kv", "w_out", "w_up", "w_down", "attn_w_in",
           "attn_sinks", "lru_w_in", "lru_conv_w", "lru_conv_b", "lru_wa", "lru_ba", "lru_wx", "lru_bx", "lru_lambda")
SMALL_W_ROWS = 32
ADAM_SMALL_ROWS = 640


def _rows(a):
    return a.reshape(-1, ROW)


def _flat_pad(parts, total):
    flat = jnp.concatenate([p.reshape(-1) for p in parts])
    return jnp.pad(flat, (0, total - flat.shape[0]))


def _pad_rows(a):
    flat = a.reshape(-1)
    n = -(-flat.shape[0] // ROW) * ROW
    return jnp.pad(flat, (0, n - flat.shape[0])).reshape(-1, ROW)


LATE = ("w_out", "lru_w_in", "w_up", "w_down0", "w_down1")


def _ready(name, full):
    if name == "w_out":
        wo = full.reshape(N_CHIPS, DEPTH, -1, D_MODEL)
        return [wo[:, l].reshape(1, MIX_OUT_W, D_MODEL) for l in range(DEPTH)]
    if name == "w_up":
        wu = full.reshape(N_CHIPS, DEPTH, D_MODEL, D_FF // N_CHIPS)
        return [wu[:, l] for l in range(DEPTH)]
    if name == "lru_w_in":
        return full.reshape(N_CHIPS, D_MODEL, LRU_IN_W // N_CHIPS)
    return full.reshape(1, D_FF, D_MODEL)


def _gather_weights(P, chip1):
    bf = lambda a: _rows(a.astype(BF16))
    small = _flat_pad([P[n] for n in SMALL_SHARDED], SMALL_W_ROWS * ROW // 2)
    small_bits = lax.bitcast_convert_type(small, BF16).reshape(SMALL_W_ROWS, ROW)
    early = jnp.concatenate([bf(P["attn_w_in"]), bf(P["w_mem_kv"]), small_bits], axis=0)
    n_in, n_kv = P["attn_w_in"].size // ROW, P["w_mem_kv"].size // ROW
    placed = _place_slot(early, chip1, N_CHIPS, name="place_weights", tr=early.shape[0] // 2)
    full = _allgather_chips(placed, name="allgather_weights", forward_to_sibling=True)
    late = {"w_out": bf(P["w_out"]), "lru_w_in": bf(P["lru_w_in"]), "w_up": bf(P["w_up"]),
            "w_down0": bf(P["w_down"][0]), "w_down1": bf(P["w_down"][1])}
    pending = {n: _place_slot(late[n], chip1, N_CHIPS, name=f"place_{n}", tr=late[n].shape[0] // 2) for n in LATE}
    W = {n: P[n] for n in REPLICATED}
    W["attn_w_in"] = full[:, :n_in].reshape(N_CHIPS, D_MODEL, ATTN_IN_W // N_CHIPS)
    kv = full[:, n_in:n_in + n_kv].reshape(N_CHIPS, DEPTH, -1, D_MODEL)
    W["w_mem_kv"] = [kv[:, l].reshape(1, D_MODEL, D_MODEL) for l in range(DEPTH)]
    sm = lax.bitcast_convert_type(full[:, n_in + n_kv:].reshape(N_CHIPS, -1, 2), F32)
    o = 0
    for n in SMALL_SHARDED:
        shp = P[n].shape[1:]
        cnt = math.prod(shp)
        piece = sm[:, o:o + cnt].reshape((N_CHIPS,) + shp)
        piece = jnp.moveaxis(piece, 0, -2)
        W[n] = piece.reshape(shp[:-1] + (N_CHIPS * shp[-1],)).reshape(-1, D_MODEL)
        o += cnt
    W["lru_wa"] = P["lru_wa"][0].astype(BF16)
    W["lru_wx"] = P["lru_wx"][0].astype(BF16)
    return W, pending


def _small_grad_list(G):
    return [G["mix_norm"], G["mlp_norm"], G["mem_norm"], G["final_norm"], jnp.pad(G["attn_sinks"].reshape(-1), (0, ROW - ATTN_HEADS)),
            G["lru_wa"], G["lru_wx"], G["lru_conv_w"], G["lru_conv_b"], G["lru_ba"], G["lru_bx"], G["lru_lambda"]]


SMALL_G_SIZES = (2 * D_MODEL, 2 * D_MODEL, D_MODEL, D_MODEL, ROW, 2 * 8 * 128 * 128, 2 * 8 * 128 * 128,
                 4 * D_MODEL, D_MODEL, 2 * D_MODEL, 2 * D_MODEL, 2 * D_MODEL)


def _finish_grads(pk0, full1, place, chip1):
    parts0 = _chip_exchange(*_reduce_first(pk0, place, "0"), name="grad_chip_exchange0")
    full0 = _reduce_last(parts0, place, "0")
    small_placed = _place_slot(full0[PK_SMALL:], chip1, N_CHIPS, name="place_small_grads", tr=SMALL_G_ROWS)
    small_all = _allgather_chips(small_placed, name="allgather_small_grads", forward_to_sibling=False)
    flat = small_all.reshape(-1)
    small = {}
    o = 0
    names = ("mix_norm", "mlp_norm", "mem_norm", "final_norm", "attn_sinks", "lru_wa", "lru_wx",
             "lru_conv_w", "lru_conv_b", "lru_ba", "lru_bx", "lru_lambda")
    for n, cnt in zip(names, SMALL_G_SIZES):
        small[n] = flat[o:o + cnt]
        o += cnt
    both = lambda off, r: jnp.concatenate([full0[off:off + r], full1[off:off + r]], axis=0)
    big = {"w_up": both(PK_UP, 1024), "w_down": both(PK_DOWN, 1024), "w_out": both(PK_OUT, 384), "w_mem_kv": both(PK_KV, 256),
           "attn_w_in": full0[PK_IN:PK_IN + 512], "lru_w_in": full1[PK_IN:PK_IN + 640]}
    return big, small


def kernel(x, mem, positions, mix_norm, mlp_norm, mem_norm, final_norm, w_mem_kv, w_out, w_up, w_down, attn_w_in, attn_sinks, lru_w_in, lru_conv_w, lru_conv_b, lru_wa, lru_ba, lru_wx, lru_bx, lru_lambda, loss_target, m_mix_norm, m_mlp_norm, m_mem_norm, m_final_norm, m_w_mem_kv, m_w_out, m_w_up, m_w_down, m_attn_w_in, m_attn_sinks, m_lru_w_in, m_lru_conv_w, m_lru_conv_b, m_lru_wa, m_lru_ba, m_lru_wx, m_lru_bx, m_lru_lambda, v_mix_norm, v_mlp_norm, v_mem_norm, v_final_norm, v_w_mem_kv, v_w_out, v_w_up, v_w_down, v_attn_w_in, v_attn_sinks, v_lru_w_in, v_lru_conv_w, v_lru_conv_b, v_lru_wa, v_lru_ba, v_lru_wx, v_lru_bx, v_lru_lambda):
    P = dict(mix_norm=mix_norm, mlp_norm=mlp_norm, mem_norm=mem_norm, final_norm=final_norm, w_mem_kv=w_mem_kv, w_out=w_out,
             w_up=w_up, w_down=w_down, attn_w_in=attn_w_in, attn_sinks=attn_sinks, lru_w_in=lru_w_in, lru_conv_w=lru_conv_w,
             lru_conv_b=lru_conv_b, lru_wa=lru_wa, lru_ba=lru_ba, lru_wx=lru_wx, lru_bx=lru_bx, lru_lambda=lru_lambda)
    M1 = dict(mix_norm=m_mix_norm, mlp_norm=m_mlp_norm, mem_norm=m_mem_norm, final_norm=m_final_norm, w_mem_kv=m_w_mem_kv,
              w_out=m_w_out, w_up=m_w_up, w_down=m_w_down, attn_w_in=m_attn_w_in, attn_sinks=m_attn_sinks, lru_w_in=m_lru_w_in,
              lru_conv_w=m_lru_conv_w, lru_conv_b=m_lru_conv_b, lru_wa=m_lru_wa, lru_ba=m_lru_ba, lru_wx=m_lru_wx,
              lru_bx=m_lru_bx, lru_lambda=m_lru_lambda)
    V2 = dict(mix_norm=v_mix_norm, mlp_norm=v_mlp_norm, mem_norm=v_mem_norm, final_norm=v_final_norm, w_mem_kv=v_w_mem_kv,
              w_out=v_w_out, w_up=v_w_up, w_down=v_w_down, attn_w_in=v_attn_w_in, attn_sinks=v_attn_sinks, lru_w_in=v_lru_w_in,
              lru_conv_w=v_lru_conv_w, lru_conv_b=v_lru_conv_b, lru_wa=v_lru_wa, lru_ba=v_lru_ba, lru_wx=v_lru_wx,
              lru_bx=v_lru_bx, lru_lambda=v_lru_lambda)
    chip = 2 * lax.axis_index("x") + lax.axis_index("y")
    chip1 = chip.astype(jnp.int32).reshape(1)
    place = jnp.stack([lax.axis_index("c").astype(jnp.int32), chip.astype(jnp.int32)])

    W, pending = _gather_weights(P, chip1)
    loss, dx, _, pk0, full1 = _local_step(x[0], mem[0], positions[0], loss_target[0], W, pending, place)
    loss = lax.psum(loss, ("x", "y", "c"))
    big, small = _finish_grads(pk0, full1, place, chip1)

    grads, deltas, new_m, new_v = {}, {}, {}, {}
    for n in BIG:
        g = big[n]
        d, nm, nv = _adamw(_rows(P[n]), g, _rows(M1[n]), _rows(V2[n]), name=f"adamw_{n}")
        grads[n], deltas[n], new_m[n], new_v[n] = (t.reshape(P[n].shape) for t in (g, d, nm, nv))

    for n in SMALL:
        g = small[n]
        if n in SMALL_SHARDED:
            shard = P[n].shape[-1]
            g = lax.dynamic_slice_in_dim(g.reshape(-1, N_CHIPS * shard), chip * shard, shard, axis=1)
        elif n == "attn_sinks":
            g = g[:ATTN_HEADS]
        grads[n] = g.reshape(P[n].shape)
    packs = []
    for src in (P, grads, M1, V2):
        a = jnp.concatenate([_pad_rows(src[n]) for n in SMALL], axis=0)
        packs.append(jnp.pad(a, ((0, ADAM_SMALL_ROWS - a.shape[0]), (0, 0))))
    d_s, nm_s, nv_s = _adamw(*packs, name="adamw_small")
    o = 0
    for n in SMALL:
        cnt = math.prod(P[n].shape)
        r = -(-cnt // ROW)
        for dst, src in ((deltas, d_s), (new_m, nm_s), (new_v, nv_s)):
            dst[n] = src[o:o + r].reshape(-1)[:cnt].reshape(P[n].shape)
        o += r

    return (loss, dx[None], *[grads[n] for n in WEIGHTS], *[deltas[n] for n in WEIGHTS],
            *[new_m[n] for n in WEIGHTS], *[new_v[n] for n in WEIGHTS])
```

```python
import functools
import math

import jax
import jax.numpy as jnp
from jax import lax
from jax.experimental import pallas as pl
from jax.experimental.pallas import tpu as pltpu

F32 = jnp.float32
BF16 = jnp.bfloat16
MESH = pl.DeviceIdType.MESH

D_MODEL = 1024
DEPTH = 2
EPS = 1e-6
ATTN_HEADS = 16
ATTN_KV_HEADS = 4
HEAD_DIM = 64
WINDOW = 128
BLOCK = 128
ROPE_THETA = 500000.0
ROPE_DIM = 16
Q_W = 1024
KV_W = 256
MEM_LEN = 256
MEM_HEADS = 4
MEM_HEAD_DIM = 128
MEM_W = 512
LRU_BLOCKS = 8
LRU_C = 8.0
ATTN_IN_W = 2048
LRU_IN_W = 2560
MIX_OUT_W = 1536
D_FF = 4096
NEG = -1e30
N_CHIPS = 4

ADAM_LR = 0.001
ADAM_B1 = 0.9
ADAM_B2 = 0.999
ADAM_EPS = 1e-08
ADAM_WD = 0.01
ADAM_STEP = 10

LANES = 128
SCAN_ROWS = 512
VMEM_LIMIT = 56 * 1024 * 1024

NT = (((1,), (1,)), ((), ()))
TN = (((0,), (0,)), ((), ()))


def _cp(sem=None):
    return pltpu.CompilerParams(dimension_semantics=sem, vmem_limit_bytes=VMEM_LIMIT)


HBM = pl.BlockSpec(memory_space=pl.ANY)
GATHER_SEMS = 6


def _place():
    x, y, c = lax.axis_index("x"), lax.axis_index("y"), lax.axis_index("c")
    chips = [(1 - x, y), (x, 1 - y), (1 - x, 1 - y)]
    return x, y, c, chips


def _remote(src, dst, send_sems, recv_sems, k, to):
    return pltpu.make_async_remote_copy(src_ref=src, dst_ref=dst, send_sem=send_sems.at[k], recv_sem=recv_sems.at[k],
                                        device_id=to, device_id_type=MESH)


def _gather_start(o_ref, send_sems, recv_sems):
    x, y, c, chips = _place()
    half = o_ref.shape[1] // 2
    own = o_ref.at[2 * x + y, pl.ds(pl.multiple_of(c * half, 16), half)]
    for j, (cx, cy) in enumerate(chips):
        _remote(own, own, send_sems, recv_sems, j, (cx, cy, c)).start()


def _gather_finish(o_ref, send_sems, recv_sems):
    x, y, c, chips = _place()
    half = o_ref.shape[1] // 2
    my_rows = pl.ds(pl.multiple_of(c * half, 16), half)
    sib_rows = pl.ds(pl.multiple_of((1 - c) * half, 16), half)
    own = o_ref.at[2 * x + y, my_rows]
    passed = []
    for j, (cx, cy) in enumerate(chips):
        landed = o_ref.at[2 * cx + cy, my_rows]
        _remote(landed, landed, send_sems, recv_sems, j, (cx, cy, c)).wait_recv()
        fw = _remote(landed, landed, send_sems, recv_sems, 3 + j, (x, y, 1 - c))
        fw.start()
        passed.append(fw)
    for j, (cx, cy) in enumerate(chips):
        got = o_ref.at[2 * cx + cy, sib_rows]
        _remote(got, got, send_sems, recv_sems, 3 + j, (x, y, 1 - c)).wait_recv()
    for j, (cx, cy) in enumerate(chips):
        _remote(own, own, send_sems, recv_sems, j, (cx, cy, c)).wait_send()
    for fw in passed:
        fw.wait_send()


def _exchange_start(h_ref, o_ref, send_sems, recv_sems):
    x, y, c, chips = _place()
    for j, (cx, cy) in enumerate(chips):
        _remote(h_ref.at[2 * cx + cy], o_ref.at[2 * x + y], send_sems, recv_sems, j, (cx, cy, c)).start()


def _exchange_finish(h_ref, o_ref, send_sems, recv_sems):
    x, y, c, chips = _place()
    for j, (cx, cy) in enumerate(chips):
        got = o_ref.at[2 * cx + cy]
        _remote(got, got, send_sems, recv_sems, j, (cx, cy, c)).wait_recv()
    for j, (cx, cy) in enumerate(chips):
        _remote(h_ref.at[2 * cx + cy], o_ref.at[2 * x + y], send_sems, recv_sems, j, (cx, cy, c)).wait_send()


def _sib_exchange_copies(g_ref, o_ref, send_sems, recv_sems):
    x, y, c, _ = _place()
    half = g_ref.shape[1] // 2
    other = pl.ds(pl.multiple_of((1 - c) * half, 8), half)
    return [_remote(g_ref.at[s, other], o_ref.at[s], send_sems, recv_sems, s, (x, y, 1 - c)) for s in range(N_CHIPS)]


def _sib_exchange_start(*refs):
    for cp in _sib_exchange_copies(*refs):
        cp.start()


def _sib_exchange_finish(*refs):
    for cp in _sib_exchange_copies(*refs):
        cp.wait()


def _sib_allgather_start(o_ref, send_sems, recv_sems):
    x, y, c, _ = _place()
    half = o_ref.shape[0] // 2
    mine = o_ref.at[pl.ds(pl.multiple_of(c * half, 8), half)]
    _remote(mine, mine, send_sems, recv_sems, 0, (x, y, 1 - c)).start()


def _sib_allgather_finish(o_ref, send_sems, recv_sems):
    x, y, c, _ = _place()
    half = o_ref.shape[0] // 2
    mine = o_ref.at[pl.ds(pl.multiple_of(c * half, 8), half)]
    got = o_ref.at[pl.ds(pl.multiple_of((1 - c) * half, 8), half)]
    _remote(got, got, send_sems, recv_sems, 0, (x, y, 1 - c)).wait_recv()
    _remote(mine, mine, send_sems, recv_sems, 0, (x, y, 1 - c)).wait_send()


class _Rider:
    def __init__(self, args, start, finish):
        self.args, self.start, self.finish = list(args), start, finish


def _gather_rider(buf):
    return None if buf is None else _Rider([buf], _gather_start, _gather_finish)


def _exchange_rider(h, landing):
    return _Rider([h, landing], _exchange_start, _exchange_finish)


def _sib_exchange_rider(g):
    landing = lax.empty((N_CHIPS, g.shape[1] // 2, g.shape[2]), g.dtype)
    return _Rider([g, landing], _sib_exchange_start, _sib_exchange_finish)


def _sib_allgather_rider(full):
    return _Rider([full], _sib_allgather_start, _sib_allgather_finish)


class _Hosted:
    def __init__(self, rider, n_in, n_out):
        self.rider = rider
        self.on = rider is not None
        self.args = rider.args if self.on else []
        k = len(self.args)
        self.alias = {n_in + k - 1: n_out} if self.on else {}
        self.in_specs = [HBM] * k
        self.out_specs = [HBM] if self.on else []
        self.out_shape = [jax.ShapeDtypeStruct(self.args[-1].shape, self.args[-1].dtype)] if self.on else []
        self.scratch = [pltpu.SemaphoreType.DMA((GATHER_SEMS,)), pltpu.SemaphoreType.DMA((GATHER_SEMS,))] if self.on else []

    def split(self, refs, n_in, n_out):
        refs = list(refs)
        if not self.on:
            return refs[:n_in], refs[n_in:n_in + n_out], refs[n_in + n_out:], None
        k = len(self.args)
        ins, outs = refs[:n_in], refs[n_in + k:n_in + k + n_out]
        rest = refs[n_in + k + n_out + 1:]
        rrefs = refs[n_in:n_in + k - 1] + [refs[n_in + k + n_out], rest[-2], rest[-1]]
        return ins, outs, rest[:-2], rrefs

    def run(self, rrefs, step, n_steps, compute):
        if rrefs is None:
            return compute()

        @pl.when(step == 0)
        def _():
            self.rider.start(*rrefs)

        compute()

        @pl.when(step == n_steps - 1)
        def _():
            self.rider.finish(*rrefs)


def _mm_nn(a, w3, *, name, out_dtype=F32, norm_g=None, resid=None, relu2=False, tm=512, gather=None):
    M, K = a.shape
    ns, _, n = w3.shape
    N = ns * n
    tm = min(tm, M)
    has_norm = norm_g is not None
    has_res = resid is not None
    n_in = 2 + has_norm + has_res
    n_out = (2 if relu2 else 1) + has_norm
    host = _Hosted(_gather_rider(gather), n_in, n_out)

    def body(*refs):
        ins, outs, _, gref = host.split(refs, n_in, n_out)
        a_ref, w_ref = ins[0], ins[1]
        g_ref = ins[2] if has_norm else None
        r_ref = ins[-1] if has_res else None

        def compute():
            if has_norm:
                xv = a_ref[...]
                rs = lax.rsqrt(jnp.mean(xv * xv, axis=-1, keepdims=True) + EPS)
                ab = (xv * rs * g_ref[...]).astype(BF16)
                outs[-1][...] = ab
            else:
                ab = a_ref[...]
            for s in range(ns):
                acc = jnp.dot(ab, w_ref[s], preferred_element_type=F32)
                sl = slice(s * n, (s + 1) * n)
                if relu2:
                    outs[0][:, sl] = acc.astype(BF16)
                    rl = jnp.maximum(acc, 0.0)
                    outs[1][:, sl] = (rl * rl).astype(BF16)
                elif has_res:
                    outs[0][:, sl] = r_ref[:, sl] + acc
                else:
                    outs[0][:, sl] = acc.astype(out_dtype)

        host.run(gref, pl.program_id(0), M // tm, compute)

    row = lambda w: pl.BlockSpec((tm, w), lambda i: (i, 0))
    in_specs = [row(K), pl.BlockSpec((ns, K, n), lambda i: (0, 0, 0))]
    args = [a, w3]
    if has_norm:
        in_specs.append(pl.BlockSpec((1, K), lambda i: (0, 0)))
        args.append(norm_g.reshape(1, K))
    if has_res:
        in_specs.append(row(N))
        args.append(resid)
    if relu2:
        out_shape = [jax.ShapeDtypeStruct((M, N), BF16), jax.ShapeDtypeStruct((M, N), BF16)]
        out_specs = [row(N), row(N)]
    else:
        out_shape = [jax.ShapeDtypeStruct((M, N), F32 if has_res else out_dtype)]
        out_specs = [row(N)]
    if has_norm:
        out_shape.append(jax.ShapeDtypeStruct((M, K), BF16))
        out_specs.append(row(K))
    res = pl.pallas_call(body, grid=(M // tm,), in_specs=in_specs + host.in_specs, out_specs=out_specs + host.out_specs,
                         out_shape=out_shape + host.out_shape, scratch_shapes=host.scratch, input_output_aliases=host.alias,
                         name=name, compiler_params=_cp(("arbitrary",) if host.on else ("parallel",)))(*args, *host.args)
    return res if len(res) > 1 else res[0]


def _mm_nt(g, w3, *, name, out_dtype=BF16, up=None, norm_x=None, norm_g=None, dres=None, tm=512):
    M = g.shape[0]
    ns, K, n = w3.shape
    tm = min(tm, M)
    has_up = up is not None
    has_norm = norm_x is not None
    has_res = dres is not None

    def body(*refs):
        refs = list(refs)
        g_ref, w_ref = refs[0], refs[1]
        pos = 2
        if has_up:
            up_ref = refs[pos]
            pos += 1
        if has_norm:
            x_ref, gn_ref = refs[pos], refs[pos + 1]
            pos += 2
        if has_res:
            r_ref = refs[pos]
            pos += 1
        outs = refs[pos:]
        acc = None
        for s in range(ns):
            part = lax.dot_general(g_ref[:, s * n:(s + 1) * n], w_ref[s], NT, preferred_element_type=F32)
            acc = part if acc is None else acc + part
        if has_up:
            outs[0][...] = (acc * (2.0 * jnp.maximum(up_ref[...].astype(F32), 0.0))).astype(BF16)
        elif has_norm:
            xv = x_ref[...]
            rs = lax.rsqrt(jnp.mean(xv * xv, axis=-1, keepdims=True) + EPS)
            xn = xv * rs
            dxn = acc * gn_ref[...]
            dx = rs * (dxn - xn * jnp.mean(dxn * xn, axis=-1, keepdims=True))
            if has_res:
                dx = dx + r_ref[...]
            outs[0][...] = dx
            outs[1][...] = dx.astype(BF16)

            @pl.when(pl.program_id(0) == 0)
            def _():
                outs[2][...] = jnp.zeros_like(outs[2])

            outs[2][...] += jnp.sum(acc * xn, axis=0, keepdims=True)
        else:
            outs[0][...] = acc.astype(out_dtype)

    row = lambda w: pl.BlockSpec((tm, w), lambda i: (i, 0))
    in_specs = [row(ns * n), pl.BlockSpec((ns, K, n), lambda i: (0, 0, 0))]
    args = [g, w3]
    if has_up:
        in_specs.append(row(K))
        args.append(up)
    if has_norm:
        in_specs += [row(K), pl.BlockSpec((1, K), lambda i: (0, 0))]
        args += [norm_x, norm_g.reshape(1, K)]
    if has_res:
        in_specs.append(row(K))
        args.append(dres)
    if has_norm:
        out_shape = [jax.ShapeDtypeStruct((M, K), F32), jax.ShapeDtypeStruct((M, K), BF16),
                     jax.ShapeDtypeStruct((1, K), F32)]
        out_specs = [row(K), row(K), pl.BlockSpec((1, K), lambda i: (0, 0))]
        sem = ("arbitrary",)
    else:
        out_shape = [jax.ShapeDtypeStruct((M, K), BF16 if has_up else out_dtype)]
        out_specs = [row(K)]
        sem = ("parallel",)
    res = pl.pallas_call(body, grid=(M // tm,), in_specs=in_specs, out_specs=out_specs, out_shape=out_shape,
                         name=name, compiler_params=_cp(sem))(*args)
    return res if len(res) > 1 else res[0]


def _mm_tn(a, g, ns, *, name, tk=512, tm=4096, packed=None, rider=None):
    M, K = a.shape
    n = g.shape[1] // ns
    tm = min(tm, M)
    tk = min(tk, K)
    nk, nm = K // tk, M // tm
    n_in = 3 if (packed is not None and packed[0] is not None) else 2
    host = _Hosted(rider, n_in, 1)

    def body(*refs):
        ins, outs, _, rrefs = host.split(refs, n_in, 1)
        a_ref, g_ref, o_ref = ins[0], ins[1], outs[0]

        def compute():
            @pl.when(pl.program_id(2) == 0)
            def _():
                o_ref[...] = jnp.zeros_like(o_ref)

            o_ref[0] += lax.dot_general(a_ref[...], g_ref[...], TN, preferred_element_type=F32)

        step = (pl.program_id(0) * nk + pl.program_id(1)) * nm + pl.program_id(2)
        host.run(rrefs, step, ns * nk * nm, compute)

    in_specs = [pl.BlockSpec((tm, tk), lambda s, k, m: (m, k)), pl.BlockSpec((tm, n), lambda s, k, m: (m, s))]
    args = [a, g]
    alias = {}
    if packed is None:
        out_spec = pl.BlockSpec((1, tk, n), lambda s, k, m: (s, k, 0))
        out_shape = jax.ShapeDtypeStruct((ns, K, n), F32)
    else:
        buf, rows, off = packed
        per_chip = K * ns // N_CHIPS
        assert n == ROW and per_chip % tk == 0 and off % tk == 0
        if ns == N_CHIPS:
            out_spec = pl.BlockSpec((1, tk, n), lambda s, k, m: (s, off // tk + k, 0))
        else:
            kpc = per_chip // tk
            out_spec = pl.BlockSpec((1, tk, n), lambda s, k, m: (k // kpc, off // tk + k % kpc, 0))
        out_shape = jax.ShapeDtypeStruct((N_CHIPS, rows, ROW), F32)
        if buf is not None:
            in_specs.append(HBM)
            args.append(buf)
            alias = {2: 0}
    sem = ("arbitrary",) * 3 if host.on else ("parallel", "parallel", "arbitrary")
    res = pl.pallas_call(
        body, grid=(ns, nk, nm), in_specs=in_specs + host.in_specs, out_specs=[out_spec] + host.out_specs,
        out_shape=[out_shape] + host.out_shape, scratch_shapes=host.scratch, name=name,
        input_output_aliases={**alias, **host.alias}, compiler_params=_cp(sem))(*args, *host.args)
    return res if host.on else res[0]


def _final(x, gain, target, *, name="final_loss", tr=256):
    S, Dm = x.shape
    tr = min(tr, S)

    def body(x_ref, g_ref, t_ref, loss_ref, dx_ref, dxb_ref, dg_ref):
        @pl.when(pl.program_id(0) == 0)
        def _():
            loss_ref[...] = jnp.zeros_like(loss_ref)
            dg_ref[...] = jnp.zeros_like(dg_ref)

        xv = x_ref[...]
        gv = g_ref[...]
        rs = lax.rsqrt(jnp.mean(xv * xv, axis=-1, keepdims=True) + EPS)
        xn = xv * rs
        err = xn * gv - t_ref[...]
        loss_ref[...] += 0.5 * jnp.sum(jnp.mean(err * err, axis=-1, keepdims=True), axis=0, keepdims=True)
        dout = err * (1.0 / Dm)
        dg_ref[...] += jnp.sum(dout * xn, axis=0, keepdims=True)
        dxn = dout * gv
        dx = rs * (dxn - xn * jnp.mean(dxn * xn, axis=-1, keepdims=True))
        dx_ref[...] = dx
        dxb_ref[...] = dx.astype(BF16)

    row = pl.BlockSpec((tr, Dm), lambda i: (i, 0))
    return pl.pallas_call(
        body, grid=(S // tr,),
        in_specs=[row, pl.BlockSpec((1, Dm), lambda i: (0, 0)), row],
        out_specs=[pl.BlockSpec((1, 1), lambda i: (0, 0)), row, row, pl.BlockSpec((1, Dm), lambda i: (0, 0))],
        out_shape=[jax.ShapeDtypeStruct((1, 1), F32), jax.ShapeDtypeStruct((S, Dm), F32),
                   jax.ShapeDtypeStruct((S, Dm), BF16), jax.ShapeDtypeStruct((1, Dm), F32)],
        name=name, compiler_params=_cp(("arbitrary",)))(x, gain.reshape(1, Dm), target)


def _rope_tables(positions):
    half = ROPE_DIM // 2
    inv_freq = ROPE_THETA ** (-2.0 * jnp.arange(half, dtype=F32) / ROPE_DIM)
    ang = positions.astype(F32)[:, None] * inv_freq
    cos, sin = jnp.cos(ang), jnp.sin(ang)
    S = positions.shape[0]
    ones = jnp.ones((S, HEAD_DIM - ROPE_DIM), F32)
    cos64 = jnp.concatenate([cos, cos, ones], axis=1)
    sin64 = jnp.concatenate([-sin, sin, 0.0 * ones], axis=1)
    return jnp.tile(cos64, (1, 2)), jnp.tile(sin64, (1, 2))


def _rope_partner(t):
    lane = lax.broadcasted_iota(jnp.int32, t.shape, 1)
    low = (lane & (HEAD_DIM - 1)) < (ROPE_DIM // 2)
    return jnp.where(low, pltpu.roll(t, LANES - ROPE_DIM // 2, 1), pltpu.roll(t, ROPE_DIM // 2, 1))


def _qk_prep(p, cos_t, sin_t, *, name="qk_prep", tr=256, gather=None):
    S = p.shape[0]
    tr = min(tr, S)
    scale = HEAD_DIM ** -0.5
    host = _Hosted(_gather_rider(gather), 3, 4)

    def body(*refs):
        ins, outs, _, gref = host.split(refs, 3, 4)
        host.run(gref, pl.program_id(0), S // tr, lambda: inner(*ins, *outs))

    def inner(p_ref, c_ref, s_ref, q_ref, k_ref, v_ref, va_ref):
        cs, sn = c_ref[...], s_ref[...]
        lane = lax.broadcasted_iota(jnp.int32, (tr, LANES), 1)
        lo = lane < HEAD_DIM
        for c in range(Q_W // LANES):
            t = p_ref[:, c * LANES:(c + 1) * LANES]
            q_ref[:, c * LANES:(c + 1) * LANES] = ((t * cs + _rope_partner(t) * sn) * scale).astype(BF16)
        for c in range(KV_W // LANES):
            t = p_ref[:, Q_W + c * LANES:Q_W + (c + 1) * LANES]
            kc = t * cs + _rope_partner(t) * sn
            vc = p_ref[:, Q_W + KV_W + c * LANES:Q_W + KV_W + (c + 1) * LANES]
            for arr, ref in ((kc, k_ref), (vc, v_ref)):
                sw = pltpu.roll(arr, HEAD_DIM, 1)
                ref[:, (2 * c) * LANES:(2 * c + 1) * LANES] = jnp.where(lo, arr, sw).astype(BF16)
                ref[:, (2 * c + 1) * LANES:(2 * c + 2) * LANES] = jnp.where(lo, sw, arr).astype(BF16)
            sw = pltpu.roll(vc, HEAD_DIM, 1)
            for k, aug in enumerate((jnp.where(lo, vc, 1.0), jnp.where(lo, 1.0, sw), jnp.where(lo, sw, 1.0), jnp.where(lo, 1.0, vc))):
                va_ref[:, (4 * c + k) * LANES:(4 * c + k + 1) * LANES] = aug.astype(BF16)

    row = lambda w: pl.BlockSpec((tr, w), lambda i: (i, 0))
    return pl.pallas_call(
        body, grid=(S // tr,), in_specs=[row(ATTN_IN_W), row(LANES), row(LANES)] + host.in_specs,
        out_specs=[row(Q_W), row(2 * KV_W), row(2 * KV_W), row(4 * KV_W)] + host.out_specs,
        out_shape=[jax.ShapeDtypeStruct((S, Q_W), BF16), jax.ShapeDtypeStruct((S, 2 * KV_W), BF16),
                   jax.ShapeDtypeStruct((S, 2 * KV_W), BF16), jax.ShapeDtypeStruct((S, 4 * KV_W), BF16)] + host.out_shape,
        scratch_shapes=host.scratch, input_output_aliases=host.alias,
        name=name, compiler_params=_cp(("arbitrary",) if host.on else ("parallel",)))(p, cos_t, sin_t, *host.args)


def _qk_prep_bwd(dq, dk, dv, dmq, cos_t, sin_t, *, name="qk_prep_bwd", tr=256):
    S = dq.shape[0]
    tr = min(tr, S)

    def body(dq_ref, dk_ref, dv_ref, dmq_ref, c_ref, s_ref, o_ref):
        cs, sn = c_ref[...], s_ref[...]
        for c in range(Q_W // LANES):
            t = dq_ref[:, c * LANES:(c + 1) * LANES]
            o_ref[:, c * LANES:(c + 1) * LANES] = (t * cs - _rope_partner(t) * sn).astype(BF16)
        for c in range(KV_W // LANES):
            t = dk_ref[:, c * LANES:(c + 1) * LANES]
            o_ref[:, Q_W + c * LANES:Q_W + (c + 1) * LANES] = (t * cs - _rope_partner(t) * sn).astype(BF16)
        o_ref[:, Q_W + KV_W:Q_W + 2 * KV_W] = dv_ref[...].astype(BF16)
        o_ref[:, Q_W + 2 * KV_W:] = dmq_ref[...]

    row = lambda w: pl.BlockSpec((tr, w), lambda i: (i, 0))
    return pl.pallas_call(
        body, grid=(S // tr,), in_specs=[row(Q_W), row(KV_W), row(KV_W), row(MEM_W), row(LANES), row(LANES)],
        out_specs=row(ATTN_IN_W), out_shape=jax.ShapeDtypeStruct((S, ATTN_IN_W), BF16),
        name=name, compiler_params=_cp(("parallel",)))(dq, dk, dv, dmq, cos_t, sin_t)


def _band(n, S):
    start = pl.multiple_of(jnp.clip((n - 1) * BLOCK, 0, S - 3 * BLOCK), BLOCK)
    qi = lax.broadcasted_iota(jnp.int32, (BLOCK, 3 * BLOCK), 0) + n * BLOCK
    ki = lax.broadcasted_iota(jnp.int32, (BLOCK, 3 * BLOCK), 1) + start
    return start, jnp.abs(ki - qi) <= WINDOW


def _head_operand(ref, h, lo):
    c = h // 2
    t = ref[:, c * LANES:(c + 1) * LANES].astype(F32)
    return jnp.where(lo if h % 2 == 0 else jnp.logical_not(lo), t, 0.0).astype(BF16)


GROUP = ATTN_HEADS // ATTN_KV_HEADS
EVENS_FIRST = (0, 2, 1, 3)


def _attn_fwd(q, kd, va, sinks, *, name="attn_fwd", gather=None):
    S = q.shape[0]
    host = _Hosted(_gather_rider(gather), 4, 2)

    def body(*refs):
        ins, outs, scr, gref = host.split(refs, 4, 2)
        host.run(gref, pl.program_id(0), S // BLOCK, lambda: inner(*ins, *outs, *scr))

    def inner(sink_ref, q_ref, k_ref, va_ref, o_ref, lse_ref, p_scr):
        n = pl.program_id(0)
        start, mask = _band(n, S)
        lane = lax.broadcasted_iota(jnp.int32, (BLOCK, LANES), 1)
        lo = lane < HEAD_DIM
        rows = pl.ds(start, 3 * BLOCK)
        scores = []
        for g in range(ATTN_KV_HEADS):
            qst = jnp.concatenate([_head_operand(q_ref, GROUP * g + j, lo) for j in EVENS_FIRST], axis=0)
            scores.append(lax.dot_general(qst, k_ref[rows, g * LANES:(g + 1) * LANES], NT, preferred_element_type=F32))
        ms = {}
        for g in range(ATTN_KV_HEADS):
            for pos, j in enumerate(EVENS_FIRST):
                h = GROUP * g + j
                s = jnp.where(mask, scores[g][pos * BLOCK:(pos + 1) * BLOCK], NEG)
                ms[h] = jnp.maximum(jnp.max(s, axis=-1, keepdims=True), sink_ref[h])
                p_scr[(GROUP * g + pos) * BLOCK:(GROUP * g + pos + 1) * BLOCK, :] = jnp.exp(s - ms[h]).astype(BF16)
        pvs = {}
        for g in range(ATTN_KV_HEADS):
            for par in range(2):
                r0 = (GROUP * g + 2 * par) * BLOCK
                pvs[g, par] = jnp.dot(p_scr[r0:r0 + 2 * BLOCK, :], va_ref[rows, (2 * g + par) * LANES:(2 * g + par + 1) * LANES],
                                      preferred_element_type=F32)
        lse_blk = jnp.zeros((BLOCK, LANES), F32)
        for g in range(ATTN_KV_HEADS):
            outs = {}
            for par in range(2):
                for k in range(2):
                    j = EVENS_FIRST[2 * par + k]
                    h = GROUP * g + j
                    pv = pvs[g, par][k * BLOCK:(k + 1) * BLOCK]
                    den = pltpu.roll(pv, HEAD_DIM, 1) + jnp.exp(sink_ref[h] - ms[h])
                    outs[j] = pv * (1.0 / den)
                    l = den[:, par * HEAD_DIM:par * HEAD_DIM + 1]
                    lse_blk = jnp.where(lane == h, ms[h] + jnp.log(l), lse_blk)
            for jj in range(2):
                o_ref[:, (2 * g + jj) * LANES:(2 * g + jj + 1) * LANES] = jnp.where(lo, outs[2 * jj], outs[2 * jj + 1]).astype(BF16)
        lse_ref[...] = lse_blk

    full = lambda w: pl.BlockSpec((S, w), lambda i: (0, 0))
    return pl.pallas_call(
        body, grid=(S // BLOCK,),
        in_specs=[pl.BlockSpec(memory_space=pltpu.SMEM), pl.BlockSpec((BLOCK, Q_W), lambda i: (i, 0)),
                  full(2 * KV_W), full(4 * KV_W)] + host.in_specs,
        out_specs=[pl.BlockSpec((BLOCK, Q_W), lambda i: (i, 0)), pl.BlockSpec((BLOCK, LANES), lambda i: (i, 0))] + host.out_specs,
        out_shape=[jax.ShapeDtypeStruct((S, MIX_OUT_W), BF16), jax.ShapeDtypeStruct((S, LANES), F32)] + host.out_shape,
        scratch_shapes=[pltpu.VMEM((ATTN_HEADS * BLOCK, 3 * BLOCK), BF16)] + host.scratch, input_output_aliases=host.alias,
        name=name, compiler_params=_cp(("arbitrary",) if host.on else ("parallel",)))(sinks, q, kd, va, *host.args)


def _attn_bwd(q, kd, vd, ao, lse, sinks, dcat, *, name="attn_bwd", rider=None):
    S = q.shape[0]
    scale = HEAD_DIM ** -0.5
    host = _Hosted(rider, 7, 4)

    def body(*refs):
        ins, outs, scr, rrefs = host.split(refs, 7, 4)
        host.run(rrefs, pl.program_id(0), S // BLOCK, lambda: inner(*ins, *outs, *scr))

    def inner(sink_ref, q_ref, k_ref, v_ref, ao_ref, lse_ref, do_ref, dq_ref, dk_ref, dv_ref, ds_ref, p_scr, dsb_scr):
        n = pl.program_id(0)

        @pl.when(n == 0)
        def _():
            dk_ref[...] = jnp.zeros_like(dk_ref)
            dv_ref[...] = jnp.zeros_like(dv_ref)
            ds_ref[...] = jnp.zeros_like(ds_ref)

        start, mask = _band(n, S)
        lane = lax.broadcasted_iota(jnp.int32, (BLOCK, LANES), 1)
        lo = lane < HEAD_DIM
        lane3 = lax.broadcasted_iota(jnp.int32, (3 * BLOCK, LANES), 1)
        row8 = lax.broadcasted_iota(jnp.int32, (8, LANES), 0)
        lane8 = lax.broadcasted_iota(jnp.int32, (8, LANES), 1)
        dsink = jnp.zeros((8, LANES), F32)
        lse_blk = lse_ref[...]
        rows = pl.ds(start, 3 * BLOCK)
        lses, deltas = {}, {}
        for c in range(Q_W // LANES):
            prod = do_ref[:, c * LANES:(c + 1) * LANES].astype(F32) * ao_ref[:, c * LANES:(c + 1) * LANES].astype(F32)
            for k in range(2):
                h = 2 * c + k
                deltas[h] = jnp.sum(jnp.where(lo if k == 0 else jnp.logical_not(lo), prod, 0.0), axis=1, keepdims=True)
                lses[h] = jnp.sum(jnp.where(lane == h, lse_blk, 0.0), axis=1, keepdims=True)
                val = -jnp.sum(jnp.exp(sink_ref[h] - lses[h]) * deltas[h], axis=0, keepdims=True)
                dsink = dsink + jnp.where((row8 == 0) & (lane8 == h), val, 0.0)
        stack = lambda ref, g: jnp.concatenate([_head_operand(ref, GROUP * g + j, lo) for j in range(GROUP)], axis=0)
        ss, dps = [], []
        for g in range(ATTN_KV_HEADS):
            ss.append(lax.dot_general(stack(q_ref, g), k_ref[rows, g * LANES:(g + 1) * LANES], NT, preferred_element_type=F32))
            dps.append(lax.dot_general(stack(do_ref, g), v_ref[rows, g * LANES:(g + 1) * LANES], NT, preferred_element_type=F32))
        for g in range(ATTN_KV_HEADS):
            for j in range(GROUP):
                h = GROUP * g + j
                r = slice(j * BLOCK, (j + 1) * BLOCK)
                hr = slice(h * BLOCK, (h + 1) * BLOCK)
                p = jnp.exp(jnp.where(mask, ss[g][r], NEG) - lses[h])
                p_scr[hr, :] = p.astype(BF16)
                dsb_scr[hr, :] = (p * (dps[g][r] - deltas[h])).astype(BF16)
        for g in range(ATTN_KV_HEADS):
            cols = slice((g // 2) * LANES, (g // 2 + 1) * LANES)
            gr = slice(GROUP * g * BLOCK, GROUP * (g + 1) * BLOCK)
            dsg = dsb_scr[gr, :]
            dqs = jnp.dot(dsg, k_ref[rows, g * LANES:(g + 1) * LANES], preferred_element_type=F32) * scale
            for jj in range(2):
                dq_ref[:, (2 * g + jj) * LANES:(2 * g + jj + 1) * LANES] = jnp.where(
                    lo, dqs[(2 * jj) * BLOCK:(2 * jj + 1) * BLOCK], dqs[(2 * jj + 1) * BLOCK:(2 * jj + 2) * BLOCK])
            half = (lane3 < HEAD_DIM) if g % 2 == 0 else (lane3 >= HEAD_DIM)
            dkr = lax.dot_general(dsg, stack(q_ref, g), TN, preferred_element_type=F32)
            dk_ref[rows, cols] += jnp.where(half, dkr + pltpu.roll(dkr, HEAD_DIM, 1), 0.0)
            dvr = lax.dot_general(p_scr[gr, :], stack(do_ref, g), TN, preferred_element_type=F32)
            dv_ref[rows, cols] += jnp.where(half, dvr + pltpu.roll(dvr, HEAD_DIM, 1), 0.0)
        ds_ref[...] += dsink

    full = lambda w: pl.BlockSpec((S, w), lambda i: (0, 0))
    blk = lambda w: pl.BlockSpec((BLOCK, w), lambda i: (i, 0))
    return pl.pallas_call(
        body, grid=(S // BLOCK,),
        in_specs=[pl.BlockSpec(memory_space=pltpu.SMEM), blk(Q_W), full(2 * KV_W), full(2 * KV_W), blk(Q_W), blk(LANES), blk(Q_W)]
        + host.in_specs,
        out_specs=[blk(Q_W), full(KV_W), full(KV_W), pl.BlockSpec((8, LANES), lambda i: (0, 0))] + host.out_specs,
        out_shape=[jax.ShapeDtypeStruct((S, Q_W), F32), jax.ShapeDtypeStruct((S, KV_W), F32),
                   jax.ShapeDtypeStruct((S, KV_W), F32), jax.ShapeDtypeStruct((8, LANES), F32)] + host.out_shape,
        scratch_shapes=[pltpu.VMEM((ATTN_HEADS * BLOCK, 3 * BLOCK), BF16), pltpu.VMEM((ATTN_HEADS * BLOCK, 3 * BLOCK), BF16)]
        + host.scratch, input_output_aliases=host.alias,
        name=name, compiler_params=_cp(("arbitrary",)))(sinks, q, kd, vd, ao, lse, dcat, *host.args)


def _mem_probs(q_ref, kv_ref, h):
    scale = MEM_HEAD_DIM ** -0.5
    qh = q_ref[:, h * LANES:(h + 1) * LANES].astype(BF16)
    s = lax.dot_general(qh, kv_ref[:, h * LANES:(h + 1) * LANES], NT, preferred_element_type=F32) * scale
    m = jnp.max(s, axis=-1, keepdims=True)
    pe = jnp.exp(s - m)
    return qh, pe * (1.0 / jnp.sum(pe, axis=-1, keepdims=True))


def _memattn_fwd(p, qblk, kv, cat, *, name="memattn_fwd", tr=512):
    S = p.shape[0]
    tr = min(tr, S)

    def body(q_ref, kv_ref, cat_ref, o_ref):
        for h in range(MEM_HEADS):
            _, pr = _mem_probs(q_ref, kv_ref, h)
            o = jnp.dot(pr.astype(BF16), kv_ref[:, MEM_W + h * LANES:MEM_W + (h + 1) * LANES], preferred_element_type=F32)
            o_ref[:, h * LANES:(h + 1) * LANES] = o.astype(BF16)

    return pl.pallas_call(
        body, grid=(S // tr,),
        in_specs=[pl.BlockSpec((tr, MEM_W), lambda i: (i, qblk)), pl.BlockSpec((MEM_LEN, 2 * MEM_W), lambda i: (0, 0)), HBM],
        out_specs=pl.BlockSpec((tr, MEM_W), lambda i: (i, Q_W // MEM_W)),
        out_shape=jax.ShapeDtypeStruct((S, MIX_OUT_W), BF16), input_output_aliases={2: 0},
        name=name, compiler_params=_cp(("parallel",)))(p, kv, cat)


def _memattn_bwd(p, qblk, kv, dcat, *, name="memattn_bwd", tr=512):
    S = p.shape[0]
    tr = min(tr, S)
    scale = MEM_HEAD_DIM ** -0.5

    def body(q_ref, kv_ref, do_ref, dq_ref, dkv_ref):
        @pl.when(pl.program_id(0) == 0)
        def _():
            dkv_ref[...] = jnp.zeros_like(dkv_ref)

        for h in range(MEM_HEADS):
            qh, pr = _mem_probs(q_ref, kv_ref, h)
            doh = do_ref[:, h * LANES:(h + 1) * LANES]
            dp = lax.dot_general(doh, kv_ref[:, MEM_W + h * LANES:MEM_W + (h + 1) * LANES], NT, preferred_element_type=F32)
            delta = jnp.sum(pr * dp, axis=-1, keepdims=True)
            dsb = (pr * (dp - delta) * scale).astype(BF16)
            dq = jnp.dot(dsb, kv_ref[:, h * LANES:(h + 1) * LANES], preferred_element_type=F32)
            dq_ref[:, h * LANES:(h + 1) * LANES] = dq.astype(BF16)
            dkv_ref[:, h * LANES:(h + 1) * LANES] += lax.dot_general(dsb, qh, TN, preferred_element_type=F32)
            dkv_ref[:, MEM_W + h * LANES:MEM_W + (h + 1) * LANES] += lax.dot_general(
                pr.astype(BF16), doh, TN, preferred_element_type=F32)

    return pl.pallas_call(
        body, grid=(S // tr,),
        in_specs=[pl.BlockSpec((tr, MEM_W), lambda i: (i, qblk)), pl.BlockSpec((MEM_LEN, 2 * MEM_W), lambda i: (0, 0)),
                  pl.BlockSpec((tr, MEM_W), lambda i: (i, Q_W // MEM_W))],
        out_specs=[pl.BlockSpec((tr, MEM_W), lambda i: (i, 0)), pl.BlockSpec((MEM_LEN, 2 * MEM_W), lambda i: (0, 0))],
        out_shape=[jax.ShapeDtypeStruct((S, MEM_W), BF16), jax.ShapeDtypeStruct((MEM_LEN, 2 * MEM_W), F32)],
        name=name, compiler_params=_cp(("arbitrary",)))(p, kv, dcat)


def _sqrt(v):
    return jnp.where(v > 0.0, v * lax.rsqrt(v), 0.0)


def _sigmoid(z):
    return 1.0 / (1.0 + jnp.exp(-z))


def _one_minus_exp(z, exp_z):
    poly = z * (1.0 + z * (0.5 + z * (1.0 / 6.0 + z * (1.0 / 24.0 + z * (1.0 / 120.0)))))
    return jnp.where(z > -0.1, -poly, 1.0 - exp_z)


def _softplus_neg(lam):
    z = -lam
    return jnp.maximum(z, 0.0) + jnp.log(1.0 + jnp.exp(-jnp.abs(z)))


_GELU_C = math.sqrt(2.0 / math.pi)


def _gelu(z):
    return 0.5 * z * (1.0 + jnp.tanh(_GELU_C * (z + 0.044715 * z * z * z)))


def _row_or_zero(ref, t, S):
    ok = jnp.logical_and(t >= 0, t < S)
    return jnp.where(ok, ref[pl.ds(jnp.clip(t, 0, S - 1), 1), :], 0.0)


def _shift_down(v, first):
    ri = lax.broadcasted_iota(jnp.int32, v.shape, 0)
    return jnp.where(ri == 0, first, pltpu.roll(v, 1, 0))


def _shift_up(v, last):
    T = v.shape[0]
    ri = lax.broadcasted_iota(jnp.int32, v.shape, 0)
    return jnp.where(ri == T - 1, last, pltpu.roll(v, T - 1, 0))


def _scan_chunk(a, u, reverse):
    T = a.shape[0]
    ri = lax.broadcasted_iota(jnp.int32, a.shape, 0)
    d = 1
    while d < T:
        if reverse:
            a_s, u_s, ok = pltpu.roll(a, T - d, 0), pltpu.roll(u, T - d, 0), ri < T - d
        else:
            a_s, u_s, ok = pltpu.roll(a, d, 0), pltpu.roll(u, d, 0), ri >= d
        u = jnp.where(ok, a * u_s + u, u)
        a = jnp.where(ok, a * a_s, a)
        d *= 2
    return a, u


def _conv_taps(xb_ref, t0, S):
    T = SCAN_ROWS
    x0 = xb_ref[pl.ds(t0, T), :]
    xm1 = _shift_down(x0, _row_or_zero(xb_ref, t0 - 1, S))
    nxt0 = _row_or_zero(xb_ref, t0 + T, S)
    xp1 = _shift_up(x0, nxt0)
    xp2 = _shift_up(xp1, _row_or_zero(xb_ref, t0 + T + 1, S))
    return xm1, x0, xp1, xp2


def _lru_gates(xc, w_a, b_a, w_x, b_x, sp):
    xcb = xc.astype(BF16)
    r = _sigmoid(jnp.dot(xcb, w_a, preferred_element_type=F32) + b_a)
    i = _sigmoid(jnp.dot(xcb, w_x, preferred_element_type=F32) + b_x)
    la = -LRU_C * r * sp
    a = jnp.exp(la)
    return r, i, a, _sqrt(_one_minus_exp(2.0 * la, a * a))


def _lru_specs(S):
    col = lambda off: pl.BlockSpec((S, LANES), lambda n: (0, n + off), pipeline_mode=pl.Buffered(1))
    small = lambda r: pl.BlockSpec((r, LANES), lambda n: (0, n))
    wblk = pl.BlockSpec((2, 1, LANES, LANES), lambda n: (0, n, 0, 0))
    return col, small, wblk


def _lru_fwd(p, conv_w, conv_b, wa, ba, wx, bx, lam, *, name="lru_fwd"):
    S = p.shape[0]
    T = SCAN_ROWS
    nc = S // T

    def body(xb_ref, gate_ref, cw_ref, cb_ref, wa_ref, ba_ref, wx_ref, bx_ref, lam_ref, y_ref, hf_ref, hr_ref, xc_v):
        sp = _softplus_neg(lam_ref[...])
        cw = cw_ref[...]

        def fwd_step(c, h_in):
            t0 = pl.multiple_of(c * T, T)
            xm1, x0, xp1, xp2 = _conv_taps(xb_ref, t0, S)
            xc = cb_ref[...] + xm1 * cw[0:1] + x0 * cw[1:2] + xp1 * cw[2:3] + xp2 * cw[3:4]
            xc_v[pl.ds(t0, T), :] = xc
            _, i, a, beta = _lru_gates(xc, wa_ref[0, 0], ba_ref[0:1], wx_ref[0, 0], bx_ref[0:1], sp[0:1])
            A, U = _scan_chunk(a, beta * (i * xc), False)
            hf_ref[pl.ds(t0, T), :] = A * h_in + U
            return hf_ref[pl.ds(t0 + T - 1, 1), :]

        lax.fori_loop(0, nc, fwd_step, jnp.zeros((1, LANES), F32))

        def rev_step(k, h_in):
            t0 = pl.multiple_of((nc - 1 - k) * T, T)
            xc = xc_v[pl.ds(t0, T), :]
            _, i, a, beta = _lru_gates(xc, wa_ref[1, 0], ba_ref[1:2], wx_ref[1, 0], bx_ref[1:2], sp[1:2])
            A, U = _scan_chunk(a, beta * (i * xc), True)
            h = A * h_in + U
            hr_ref[pl.ds(t0, T), :] = h
            y_ref[pl.ds(t0, T), :] = ((hf_ref[pl.ds(t0, T), :] + h) * _gelu(gate_ref[pl.ds(t0, T), :])).astype(BF16)
            return hr_ref[pl.ds(t0, 1), :]

        lax.fori_loop(0, nc, rev_step, jnp.zeros((1, LANES), F32))

    col, small, wblk = _lru_specs(S)
    colo = lambda: pl.BlockSpec((S, LANES), lambda n: (0, n))
    return pl.pallas_call(
        body, grid=(LRU_BLOCKS,),
        in_specs=[col(0), col(LRU_BLOCKS), small(4), small(1), wblk, small(2), wblk, small(2), small(2)],
        out_specs=[colo(), colo(), colo()],
        out_shape=[jax.ShapeDtypeStruct((S, MIX_OUT_W), BF16), jax.ShapeDtypeStruct((S, D_MODEL), F32),
                   jax.ShapeDtypeStruct((S, D_MODEL), F32)],
        scratch_shapes=[pltpu.VMEM((S, LANES), F32)],
        name=name, compiler_params=_cp(("parallel",)))(p, p, conv_w, conv_b, wa, ba, wx, bx, lam)


def _lru_bwd(p, hf, hr, dcat, conv_w, conv_b, wa, ba, wx, bx, lam, *, name="lru_bwd"):
    S = p.shape[0]
    T = SCAN_ROWS
    nc = S // T

    def body(xb_ref, gate_ref, hf_ref, hr_ref, dy_ref, cw_ref, cb_ref, wa_ref, ba_ref, wx_ref, bx_ref, lam_ref,
             dxb_ref, dgate_ref, dcw_ref, dcb_ref, dwa_ref, dba_ref, dwx_ref, dbx_ref, dlam_ref, xc_v, dxc_v, dh_v):
        lam_v = lam_ref[...]
        sp = _softplus_neg(lam_v)
        cw = cw_ref[...]
        for ref in (dcw_ref, dcb_ref, dwa_ref, dba_ref, dwx_ref, dbx_ref, dlam_ref):
            ref[...] = jnp.zeros_like(ref)

        def prep_step(c, carry):
            t0 = pl.multiple_of(c * T, T)
            rows = pl.ds(t0, T)
            xm1, x0, xp1, xp2 = _conv_taps(xb_ref, t0, S)
            xc_v[rows, :] = cb_ref[...] + xm1 * cw[0:1] + x0 * cw[1:2] + xp1 * cw[2:3] + xp2 * cw[3:4]
            z = gate_ref[rows, :]
            dy = dy_ref[rows, :].astype(F32)
            th = jnp.tanh(_GELU_C * (z + 0.044715 * z * z * z))
            dgelu = 0.5 * (1.0 + th) + 0.5 * z * (1.0 - th * th) * _GELU_C * (1.0 + 3.0 * 0.044715 * z * z)
            dgate_ref[rows, :] = (dy * (hf_ref[rows, :] + hr_ref[rows, :]) * dgelu).astype(BF16)
            dh_v[rows, :] = dy * (0.5 * z * (1.0 + th))
            return carry

        lax.fori_loop(0, nc, prep_step, 0)

        def direction(d):
            h_ref = hf_ref if d == 0 else hr_ref
            w_a, w_x = wa_ref[d, 0], wx_ref[d, 0]
            b_a, b_x, sp_d = ba_ref[d:d + 1], bx_ref[d:d + 1], sp[d:d + 1]

            def step(k, carry):
                g_in, a_in = carry
                c = (nc - 1 - k) if d == 0 else k
                t0 = pl.multiple_of(c * T, T)
                rows = pl.ds(t0, T)
                xc = xc_v[rows, :]
                r, i, a, beta = _lru_gates(xc, w_a, b_a, w_x, b_x, sp_d)
                dh = dh_v[rows, :]
                hc = h_ref[rows, :]
                if d == 0:
                    A, U = _scan_chunk(_shift_up(a, a_in), dh, True)
                    g = A * g_in + U
                    h_nb = _shift_down(hc, _row_or_zero(h_ref, t0 - 1, S))
                    nxt = (g[0:1], a[0:1])
                else:
                    A, U = _scan_chunk(_shift_down(a, a_in), dh, False)
                    g = A * g_in + U
                    h_nb = _shift_up(hc, _row_or_zero(h_ref, t0 + T, S))
                    nxt = (g[T - 1:T], a[T - 1:T])
                da = g * h_nb
                dbeta = g * (i * xc)
                tb = g * beta
                dla = da * a - dbeta * (a * a / beta)
                dzr = (dla * (-LRU_C * sp_d)) * (r * (1.0 - r))
                dzi = (tb * xc) * (i * (1.0 - i))
                dzrb, dzib, xcb = dzr.astype(BF16), dzi.astype(BF16), xc.astype(BF16)
                dwa_ref[d, 0] += lax.dot_general(xcb, dzrb, TN, preferred_element_type=F32)
                dwx_ref[d, 0] += lax.dot_general(xcb, dzib, TN, preferred_element_type=F32)
                dba_ref[d:d + 1] += jnp.sum(dzr, axis=0, keepdims=True)
                dbx_ref[d:d + 1] += jnp.sum(dzi, axis=0, keepdims=True)
                dlam_ref[d:d + 1] += jnp.sum(dla * (-LRU_C * r), axis=0, keepdims=True)
                dxc = (tb * i + lax.dot_general(dzrb, w_a, NT, preferred_element_type=F32)
                       + lax.dot_general(dzib, w_x, NT, preferred_element_type=F32))
                if d == 0:
                    dxc_v[rows, :] = dxc
                else:
                    dxc_v[rows, :] += dxc
                return nxt

            lax.fori_loop(0, nc, step, (jnp.zeros((1, LANES), F32), jnp.zeros((1, LANES), F32)))

        direction(0)
        direction(1)
        dlam_ref[...] = dlam_ref[...] * (-1.0 / (1.0 + jnp.exp(lam_v)))

        def conv_step(c, carry):
            t0 = pl.multiple_of(c * T, T)
            rows = pl.ds(t0, T)
            g0 = dxc_v[rows, :]
            gm1 = _shift_down(g0, _row_or_zero(dxc_v, t0 - 1, S))
            gm2 = _shift_down(gm1, _row_or_zero(dxc_v, t0 - 2, S))
            gp1 = _shift_up(g0, _row_or_zero(dxc_v, t0 + T, S))
            dxb_ref[rows, :] = (cw[0:1] * gp1 + cw[1:2] * g0 + cw[2:3] * gm1 + cw[3:4] * gm2).astype(BF16)
            xm1, x0, xp1, xp2 = _conv_taps(xb_ref, t0, S)
            for tap, xs in enumerate((xm1, x0, xp1, xp2)):
                dcw_ref[tap:tap + 1] += jnp.sum(g0 * xs, axis=0, keepdims=True)
            dcb_ref[...] += jnp.sum(g0, axis=0, keepdims=True)
            return carry

        lax.fori_loop(0, nc, conv_step, 0)

    col, small, wblk = _lru_specs(S)
    colo = lambda: pl.BlockSpec((S, LANES), lambda n: (0, n), pipeline_mode=pl.Buffered(1))
    return pl.pallas_call(
        body, grid=(LRU_BLOCKS,),
        in_specs=[col(0), col(LRU_BLOCKS), col(0), col(0), col(0), small(4), small(1), wblk, small(2), wblk, small(2), small(2)],
        out_specs=[colo(), colo(), small(4), small(1), wblk, small(2), wblk, small(2), small(2)],
        out_shape=[jax.ShapeDtypeStruct((S, D_MODEL), BF16), jax.ShapeDtypeStruct((S, D_MODEL), BF16),
                   jax.ShapeDtypeStruct((4, D_MODEL), F32), jax.ShapeDtypeStruct((1, D_MODEL), F32),
                   jax.ShapeDtypeStruct((2, LRU_BLOCKS, LANES, LANES), F32), jax.ShapeDtypeStruct((2, D_MODEL), F32),
                   jax.ShapeDtypeStruct((2, LRU_BLOCKS, LANES, LANES), F32), jax.ShapeDtypeStruct((2, D_MODEL), F32),
                   jax.ShapeDtypeStruct((2, D_MODEL), F32)],
        scratch_shapes=[pltpu.VMEM((S, LANES), F32), pltpu.VMEM((S, LANES), F32), pltpu.VMEM((S, LANES), F32)],
        name=name, compiler_params=_cp(("parallel",)))(p, p, hf, hr, dcat, conv_w, conv_b, wa, ba, wx, bx, lam)


def _mlp_fwd(x, w_up, w_down, gain, l):
    up, act, h = _mm_nn(x, w_up, norm_g=gain, relu2=True, name=f"mlp_up{l}")
    return _mm_nn(act, w_down, resid=x, name=f"mlp_down{l}"), (up, act, h)


PK_UP, PK_DOWN, PK_KV, PK_OUT, PK_IN = 0, 1024, 2048, 2304, 2688
SMALL_G_ROWS = 192
PK_SMALL = PK_IN + 512
PK_ROWS = {0: PK_SMALL + SMALL_G_ROWS, 1: PK_IN + 640}


def _mlp_bwd(x, dx, dxb, saved, w_up, w_down, gain, l, rider=None):
    up, act, h = saved
    pk = _mm_tn(act, dxb, 1, name=f"dw_down{l}", packed=(None, PK_ROWS[l], PK_DOWN), rider=rider)
    pk, carried = pk if rider is not None else (pk, None)
    dup = _mm_nt(dxb, w_down, up=up, name=f"d_up{l}")
    pk = _mm_tn(h, dup, N_CHIPS, name=f"dw_up{l}", packed=(pk, PK_ROWS[l], PK_UP))
    dx, dxb, g_gain = _mm_nt(dup, w_up, norm_x=x, norm_g=gain, dres=dx, name=f"d_mlp_in{l}")
    return dx, dxb, pk, g_gain, carried


def _reduce_first(pk, place, tag, recv=None):
    if recv is None:
        recv = _sibling_exchange(pk, name=f"grad_sibling_exchange{tag}")
    return _sum_halves(pk, recv, place, name=f"sum_halves{tag}", tr=pk.shape[1] // 4)


def _sum_parts(parts, place, tag):
    return _sum_chips(parts, place, name=f"sum_chips{tag}", tr=parts.shape[1] // 2)


def _reduce_last(parts, place, tag):
    return _sibling_allgather(_sum_parts(parts, place, tag), name=f"grad_sibling_allgather{tag}")


def _local_step(x, mem, positions, target, W, pending=None, place=None):
    cos_t, sin_t = _rope_tables(positions)
    sinks = W["attn_sinks"].reshape(ATTN_HEADS)
    G = {}

    def hosting(late, fn, *args, **kw):
        if pending is None:
            return fn(*args, **kw)
        *res, buf = fn(*args, gather=pending[late], **kw)
        if late.startswith("w_down"):
            W.setdefault("w_down", [None] * DEPTH)[int(late[-1])] = _ready(late, buf)
        else:
            W[late] = _ready(late, buf)
        return res if len(res) > 1 else res[0]

    kv0, memn = _mm_nn(mem, W["w_mem_kv"][0], norm_g=W["mem_norm"], out_dtype=BF16, name="mem_kv0", tm=256)
    kv1 = _mm_nn(memn, W["w_mem_kv"][1], out_dtype=BF16, name="mem_kv1", tm=256)
    p0, h0 = hosting("w_out", _mm_nn, x, W["attn_w_in"], norm_g=W["mix_norm"][0], name="attn_in")
    q, kd, vd, va = hosting("lru_w_in", _qk_prep, p0, cos_t, sin_t)
    ao, lse = hosting("w_up", _attn_fwd, q, kd, va, sinks)
    cat0 = _memattn_fwd(p0, Q_W // MEM_W + 1, kv0, ao, name="memattn_fwd0")
    x1 = hosting("w_down0", _mm_nn, cat0, W["w_out"][0], resid=x, name="mix_out0")
    up0, act0, h1 = hosting("w_down1", _mm_nn, x1, W["w_up"][0], norm_g=W["mlp_norm"][0], relu2=True, name="mlp_up0")
    x2, mlp0 = _mm_nn(act0, W["w_down"][0], resid=x1, name="mlp_down0"), (up0, act0, h1)
    p1, h2 = _mm_nn(x2, W["lru_w_in"], norm_g=W["mix_norm"][1], name="lru_in")
    lru_w = (W["lru_conv_w"], W["lru_conv_b"], W["lru_wa"], W["lru_ba"], W["lru_wx"], W["lru_bx"], W["lru_lambda"])
    y, hf, hr = _lru_fwd(p1, *lru_w)
    cat1 = _memattn_fwd(p1, 2 * D_MODEL // MEM_W, kv1, y, name="memattn_fwd1")
    x3 = _mm_nn(cat1, W["w_out"][1], resid=x2, name="mix_out1")
    x4, mlp1 = _mlp_fwd(x3, W["w_up"][1], W["w_down"][1], W["mlp_norm"][1], 1)
    loss, dx, dxb, G["final_norm"] = _final(x4, W["final_norm"], target)

    def put(pk, off, g):
        return pk.at[:, off:off + g.size // (N_CHIPS * ROW)].set(g.reshape(N_CHIPS, -1, ROW))

    dx, dxb, pk1, gm1, _ = _mlp_bwd(x3, dx, dxb, mlp1, W["w_up"][1], W["w_down"][1], W["mlp_norm"][1], 1)
    pk1 = _mm_tn(cat1, dxb, 1, name="dw_out1", tk=384, packed=(pk1, PK_ROWS[1], PK_OUT))
    dcat1 = _mm_nt(dxb, W["w_out"][1], name="d_mix1")
    dmq1, dkv1 = _memattn_bwd(p1, 2 * D_MODEL // MEM_W, kv1, dcat1, name="memattn_bwd1")
    dkv1b = dkv1.astype(BF16)
    pk1 = _mm_tn(memn, dkv1b, 1, name="dw_kv1", tm=256, tk=256, packed=(pk1, PK_ROWS[1], PK_KV))
    (dxb1, dgate, G["lru_conv_w"], G["lru_conv_b"], G["lru_wa"], G["lru_ba"], G["lru_wx"], G["lru_bx"],
     G["lru_lambda"]) = _lru_bwd(p1, hf, hr, dcat1, *lru_w)
    dp1 = jnp.concatenate([dxb1, dgate, dmq1], axis=1)
    pk1 = put(pk1, PK_IN, _mm_tn(h2, dp1, N_CHIPS, name="dw_lru_in"))
    dx, dxb, gx1 = _mm_nt(dp1, W["lru_w_in"], norm_x=x2, norm_g=W["mix_norm"][1], dres=dx, name="d_lru_in")
    dist = place is not None
    dx, dxb, pk0, gm0, recv1 = _mlp_bwd(x1, dx, dxb, mlp0, W["w_up"][0], W["w_down"][0], W["mlp_norm"][0], 0,
                                        rider=_sib_exchange_rider(pk1) if dist else None)
    rider = _exchange_rider(*_reduce_first(pk1, place, "1", recv1)) if dist else None
    pk0 = _mm_tn(cat0, dxb, 1, name="dw_out0", tk=384, packed=(pk0, PK_ROWS[0], PK_OUT))
    dcat0 = _mm_nt(dxb, W["w_out"][0], name="d_mix0")
    dmq0, dkv0 = _memattn_bwd(p0, Q_W // MEM_W + 1, kv0, dcat0, name="memattn_bwd0")
    dkv0b = dkv0.astype(BF16)
    pk0 = _mm_tn(memn, dkv0b, 1, name="dw_kv0", tm=256, tk=256, packed=(pk0, PK_ROWS[0], PK_KV))
    dq, dk, dv, dsink, *parts1 = _attn_bwd(q, kd, vd, cat0, lse, sinks, dcat0, rider=rider)
    dp0 = _qk_prep_bwd(dq, dk, dv, dmq0, cos_t, sin_t)
    g_in = _mm_tn(h0, dp0, N_CHIPS, name="dw_attn_in",
                  rider=_sib_allgather_rider(_sum_parts(parts1[0], place, "1")) if dist else None)
    if dist:
        g_in, pk1 = g_in
    pk0 = put(pk0, PK_IN, g_in)
    dx, _, gx0 = _mm_nt(dp0, W["attn_w_in"], norm_x=x, norm_g=W["mix_norm"][0], dres=dx, name="d_attn_in")

    w_kv_both = jnp.concatenate([W["w_mem_kv"][0], W["w_mem_kv"][1]], axis=0)
    _, _, G["mem_norm"] = _mm_nt(jnp.concatenate([dkv0b, dkv1b], axis=1), w_kv_both, norm_x=mem, norm_g=W["mem_norm"],
                                 name="d_mem", tm=256)

    G["mix_norm"] = jnp.concatenate([gx0, gx1], axis=0)
    G["mlp_norm"] = jnp.concatenate([gm0, gm1], axis=0)
    G["attn_sinks"] = dsink[0:1, 0:ATTN_HEADS]
    pk0 = put(pk0, PK_SMALL, _flat_pad(_small_grad_list(G), N_CHIPS * SMALL_G_ROWS * ROW))
    return loss[0, 0], dx, G, pk0, pk1


def _comm_call(body, out_shape, n_sems, name, *args, alias=None):
    return pl.pallas_call(
        body, out_shape=out_shape, in_specs=[HBM] * len(args), out_specs=HBM,
        scratch_shapes=[pltpu.SemaphoreType.DMA((n_sems,)), pltpu.SemaphoreType.DMA((n_sems,))],
        input_output_aliases=alias or {}, name=name)(*args)


def _place_slot(shard, slot, n_slots, *, name, tr):
    R, C = shard.shape

    def body(s_ref, a_ref, o_ref):
        o_ref[0] = a_ref[...]

    return pl.pallas_call(
        body,
        grid_spec=pltpu.PrefetchScalarGridSpec(
            num_scalar_prefetch=1, grid=(R // tr,), in_specs=[pl.BlockSpec((tr, C), lambda i, s_ref: (i, 0))],
            out_specs=pl.BlockSpec((1, tr, C), lambda i, s_ref: (s_ref[0], i, 0))),
        out_shape=jax.ShapeDtypeStruct((n_slots, R, C), shard.dtype), name=name,
        compiler_params=_cp(("parallel",)))(slot, shard)


def _allgather_chips(buf, *, name, forward_to_sibling):
    def body(b_ref, o_ref, send_sems, recv_sems):
        if forward_to_sibling:
            _gather_start(o_ref, send_sems, recv_sems)
            _gather_finish(o_ref, send_sems, recv_sems)
            return
        x, y, c, chips = _place()
        own = o_ref.at[2 * x + y]
        sends = [_remote(own, own, send_sems, recv_sems, j, (cx, cy, c)) for j, (cx, cy) in enumerate(chips)]
        for cp in sends:
            cp.start()
        for j, (cx, cy) in enumerate(chips):
            landed = o_ref.at[2 * cx + cy]
            _remote(landed, landed, send_sems, recv_sems, j, (cx, cy, c)).wait_recv()
        for cp in sends:
            cp.wait_send()

    return _comm_call(body, jax.ShapeDtypeStruct(buf.shape, buf.dtype), GATHER_SEMS, name, buf, alias={0: 0})


def _sibling_exchange(g, *, name):
    _, R, C = g.shape
    half = R // 2

    def body(g_ref, o_ref, send_sems, recv_sems):
        _sib_exchange_start(g_ref, o_ref, send_sems, recv_sems)
        _sib_exchange_finish(g_ref, o_ref, send_sems, recv_sems)

    return _comm_call(body, jax.ShapeDtypeStruct((N_CHIPS, half, C), g.dtype), N_CHIPS, name, g)


def _chip_exchange(h, parts, *, name):
    def body(h_ref, p_ref, o_ref, send_sems, recv_sems):
        x, y, c, chips = _place()
        me = 2 * x + y
        cps = [_remote(h_ref.at[2 * cx + cy], o_ref.at[me], send_sems, recv_sems, j, (cx, cy, c))
               for j, (cx, cy) in enumerate(chips)]
        for cp in cps:
            cp.start()
        for j, (cx, cy) in enumerate(chips):
            got = o_ref.at[2 * cx + cy]
            _remote(got, got, send_sems, recv_sems, j, (cx, cy, c)).wait_recv()
        for cp in cps:
            cp.wait_send()

    return _comm_call(body, jax.ShapeDtypeStruct(parts.shape, parts.dtype), 3, name, h, parts, alias={1: 0})


def _sibling_allgather(full, *, name):
    def body(f_ref, o_ref, send_sems, recv_sems):
        _sib_allgather_start(o_ref, send_sems, recv_sems)
        _sib_allgather_finish(o_ref, send_sems, recv_sems)

    return _comm_call(body, jax.ShapeDtypeStruct(full.shape, full.dtype), 1, name, full, alias={0: 0})


def _sum_halves(g, recv, place, *, name="sum_halves", tr=480):
    _, R, C = g.shape
    half = R // 2
    nblk = half // tr

    def body(pl_ref, g_ref, r_ref, o_ref, own_ref):
        v = (g_ref[...] + r_ref[...]).astype(BF16)
        o_ref[...] = v

        @pl.when(pl.program_id(1) == pl_ref[1])
        def _():
            own_ref[...] = v

    blk = pl.BlockSpec((1, tr, C), lambda i, s, p: (s, i, 0))
    return pl.pallas_call(
        body,
        grid_spec=pltpu.PrefetchScalarGridSpec(
            num_scalar_prefetch=1, grid=(nblk, N_CHIPS),
            in_specs=[pl.BlockSpec((1, tr, C), lambda i, s, p: (s, p[0] * nblk + i, 0)), blk],
            out_specs=[blk, pl.BlockSpec((1, tr, C), lambda i, s, p: (p[1], i, 0))]),
        out_shape=[jax.ShapeDtypeStruct((N_CHIPS, half, C), BF16)] * 2, name=name,
        compiler_params=_cp(("parallel", "arbitrary")))(place, g, recv)


def _sum_chips(parts, place, *, name="sum_chips", tr=480):
    _, R, C = parts.shape
    nblk = R // tr

    def body(pl_ref, p_ref, o_ref):
        acc = p_ref[0].astype(F32) + p_ref[1].astype(F32)
        o_ref[...] = (acc + p_ref[2].astype(F32)) + p_ref[3].astype(F32)

    return pl.pallas_call(
        body,
        grid_spec=pltpu.PrefetchScalarGridSpec(
            num_scalar_prefetch=1, grid=(nblk,), in_specs=[pl.BlockSpec((N_CHIPS, tr, C), lambda i, p: (0, i, 0))],
            out_specs=pl.BlockSpec((tr, C), lambda i, p: (p[0] * nblk + i, 0))),
        out_shape=jax.ShapeDtypeStruct((2 * R, C), F32), name=name, compiler_params=_cp(("parallel",)))(place, parts)


def _adamw(w, g, m, v, *, name, tr=128):
    R, C = w.shape
    bc1 = 1.0 - ADAM_B1 ** ADAM_STEP
    bc2 = 1.0 - ADAM_B2 ** ADAM_STEP

    def body(w_ref, g_ref, m_ref, v_ref, d_ref, nm_ref, nv_ref):
        gv = g_ref[...]
        nm = ADAM_B1 * m_ref[...] + (1.0 - ADAM_B1) * gv
        nv = ADAM_B2 * v_ref[...] + (1.0 - ADAM_B2) * (gv * gv)
        d_ref[...] = -ADAM_LR * ((nm / bc1) / (_sqrt(nv / bc2) + ADAM_EPS) + ADAM_WD * w_ref[...])
        nm_ref[...] = nm
        nv_ref[...] = nv

    blk = pl.BlockSpec((tr, C), lambda i: (i, 0))
    return pl.pallas_call(
        body, grid=(R // tr,), in_specs=[blk] * 4, out_specs=[blk] * 3,
        out_shape=[jax.ShapeDtypeStruct((R, C), F32)] * 3, name=name, compiler_params=_cp(("parallel",)))(w, g, m, v)


ROW = 1024
BIG = ("w_mem_kv", "w_out", "w_up", "w_down", "attn_w_in", "lru_w_in")
SMALL_SHARDED = ("lru_conv_w", "lru_conv_b", "lru_ba", "lru_bx", "lru_lambda")
REPLICATED = ("mix_norm", "mlp_norm", "mem_norm", "final_norm", "attn_sinks", "lru_wa", "lru_wx")
SMALL = REPLICATED + SMALL_SHARDED
WEIGHTS = ("mix_norm", "mlp_norm", "mem_norm", "final_norm", "w_mem_kv", "w_out", "w_up", "w_down", "attn_w_in",
           "attn_sinks", "lru_w_in", "lru_conv_w", "lru_conv_b", "lru_wa", "lru_ba", "lru_wx", "lru_bx", "lru_lambda")
SMALL_W_ROWS = 32
ADAM_SMALL_ROWS = 640


def _rows(a):
    return a.reshape(-1, ROW)


def _flat_pad(parts, total):
    flat = jnp.concatenate([p.reshape(-1) for p in parts])
    return jnp.pad(flat, (0, total - flat.shape[0]))


def _pad_rows(a):
    flat = a.reshape(-1)
    n = -(-flat.shape[0] // ROW) * ROW
    return jnp.pad(flat, (0, n - flat.shape[0])).reshape(-1, ROW)


LATE = ("w_out", "lru_w_in", "w_up", "w_down0", "w_down1")


def _ready(name, full):
    if name == "w_out":
        wo = full.reshape(N_CHIPS, DEPTH, -1, D_MODEL)
        return [wo[:, l].reshape(1, MIX_OUT_W, D_MODEL) for l in range(DEPTH)]
    if name == "w_up":
        wu = full.reshape(N_CHIPS, DEPTH, D_MODEL, D_FF // N_CHIPS)
        return [wu[:, l] for l in range(DEPTH)]
    if name == "lru_w_in":
        return full.reshape(N_CHIPS, D_MODEL, LRU_IN_W // N_CHIPS)
    return full.reshape(1, D_FF, D_MODEL)


def _gather_weights(P, chip1):
    bf = lambda a: _rows(a.astype(BF16))
    small = _flat_pad([P[n] for n in SMALL_SHARDED], SMALL_W_ROWS * ROW // 2)
    small_bits = lax.bitcast_convert_type(small, BF16).reshape(SMALL_W_ROWS, ROW)
    early = jnp.concatenate([bf(P["attn_w_in"]), bf(P["w_mem_kv"]), small_bits], axis=0)
    n_in, n_kv = P["attn_w_in"].size // ROW, P["w_mem_kv"].size // ROW
    placed = _place_slot(early, chip1, N_CHIPS, name="place_weights", tr=early.shape[0] // 2)
    full = _allgather_chips(placed, name="allgather_weights", forward_to_sibling=True)
    late = {"w_out": bf(P["w_out"]), "lru_w_in": bf(P["lru_w_in"]), "w_up": bf(P["w_up"]),
            "w_down0": bf(P["w_down"][0]), "w_down1": bf(P["w_down"][1])}
    pending = {n: _place_slot(late[n], chip1, N_CHIPS, name=f"place_{n}", tr=late[n].shape[0] // 2) for n in LATE}
    W = {n: P[n] for n in REPLICATED}
    W["attn_w_in"] = full[:, :n_in].reshape(N_CHIPS, D_MODEL, ATTN_IN_W // N_CHIPS)
    kv = full[:, n_in:n_in + n_kv].reshape(N_CHIPS, DEPTH, -1, D_MODEL)
    W["w_mem_kv"] = [kv[:, l].reshape(1, D_MODEL, D_MODEL) for l in range(DEPTH)]
    sm = lax.bitcast_convert_type(full[:, n_in + n_kv:].reshape(N_CHIPS, -1, 2), F32)
    o = 0
    for n in SMALL_SHARDED:
        shp = P[n].shape[1:]
        cnt = math.prod(shp)
        piece = sm[:, o:o + cnt].reshape((N_CHIPS,) + shp)
        piece = jnp.moveaxis(piece, 0, -2)
        W[n] = piece.reshape(shp[:-1] + (N_CHIPS * shp[-1],)).reshape(-1, D_MODEL)
        o += cnt
    W["lru_wa"] = P["lru_wa"][0].astype(BF16)
    W["lru_wx"] = P["lru_wx"][0].astype(BF16)
    return W, pending


def _small_grad_list(G):
    return [G["mix_norm"], G["mlp_norm"], G["mem_norm"], G["final_norm"], jnp.pad(G["attn_sinks"].reshape(-1), (0, ROW - ATTN_HEADS)),
            G["lru_wa"], G["lru_wx"], G["lru_conv_w"], G["lru_conv_b"], G["lru_ba"], G["lru_bx"], G["lru_lambda"]]


SMALL_G_SIZES = (2 * D_MODEL, 2 * D_MODEL, D_MODEL, D_MODEL, ROW, 2 * 8 * 128 * 128, 2 * 8 * 128 * 128,
                 4 * D_MODEL, D_MODEL, 2 * D_MODEL, 2 * D_MODEL, 2 * D_MODEL)


def _finish_grads(pk0, full1, place, chip1):
    parts0 = _chip_exchange(*_reduce_first(pk0, place, "0"), name="grad_chip_exchange0")
    full0 = _reduce_last(parts0, place, "0")
    small_placed = _place_slot(full0[PK_SMALL:], chip1, N_CHIPS, name="place_small_grads", tr=SMALL_G_ROWS)
    small_all = _allgather_chips(small_placed, name="allgather_small_grads", forward_to_sibling=False)
    flat = small_all.reshape(-1)
    small = {}
    o = 0
    names = ("mix_norm", "mlp_norm", "mem_norm", "final_norm", "attn_sinks", "lru_wa", "lru_wx",
             "lru_conv_w", "lru_conv_b", "lru_ba", "lru_bx", "lru_lambda")
    for n, cnt in zip(names, SMALL_G_SIZES):
        small[n] = flat[o:o + cnt]
        o += cnt
    both = lambda off, r: jnp.concatenate([full0[off:off + r], full1[off:off + r]], axis=0)
    big = {"w_up": both(PK_UP, 1024), "w_down": both(PK_DOWN, 1024), "w_out": both(PK_OUT, 384), "w_mem_kv": both(PK_KV, 256),
           "attn_w_in": full0[PK_IN:PK_IN + 512], "lru_w_in": full1[PK_IN:PK_IN + 640]}
    return big, small


def kernel(x, mem, positions, mix_norm, mlp_norm, mem_norm, final_norm, w_mem_kv, w_out, w_up, w_down, attn_w_in, attn_sinks, lru_w_in, lru_conv_w, lru_conv_b, lru_wa, lru_ba, lru_wx, lru_bx, lru_lambda, loss_target, m_mix_norm, m_mlp_norm, m_mem_norm, m_final_norm, m_w_mem_kv, m_w_out, m_w_up, m_w_down, m_attn_w_in, m_attn_sinks, m_lru_w_in, m_lru_conv_w, m_lru_conv_b, m_lru_wa, m_lru_ba, m_lru_wx, m_lru_bx, m_lru_lambda, v_mix_norm, v_mlp_norm, v_mem_norm, v_final_norm, v_w_mem_kv, v_w_out, v_w_up, v_w_down, v_attn_w_in, v_attn_sinks, v_lru_w_in, v_lru_conv_w, v_lru_conv_b, v_lru_wa, v_lru_ba, v_lru_wx, v_lru_bx, v_lru_lambda):
    P = dict(mix_norm=mix_norm, mlp_norm=mlp_norm, mem_norm=mem_norm, final_norm=final_norm, w_mem_kv=w_mem_kv, w_out=w_out,
             w_up=w_up, w_down=w_down, attn_w_in=attn_w_in, attn_sinks=attn_sinks, lru_w_in=lru_w_in, lru_conv_w=lru_conv_w,
             lru_conv_b=lru_conv_b, lru_wa=lru_wa, lru_ba=lru_ba, lru_wx=lru_wx, lru_bx=lru_bx, lru_lambda=lru_lambda)
    M1 = dict(mix_norm=m_mix_norm, mlp_norm=m_mlp_norm, mem_norm=m_mem_norm, final_norm=m_final_norm, w_mem_kv=m_w_mem_kv,
              w_out=m_w_out, w_up=m_w_up, w_down=m_w_down, attn_w_in=m_attn_w_in, attn_sinks=m_attn_sinks, lru_w_in=m_lru_w_in,
              lru_conv_w=m_lru_conv_w, lru_conv_b=m_lru_conv_b, lru_wa=m_lru_wa, lru_ba=m_lru_ba, lru_wx=m_lru_wx,
              lru_bx=m_lru_bx, lru_lambda=m_lru_lambda)
    V2 = dict(mix_norm=v_mix_norm, mlp_norm=v_mlp_norm, mem_norm=v_mem_norm, final_norm=v_final_norm, w_mem_kv=v_w_mem_kv,
              w_out=v_w_out, w_up=v_w_up, w_down=v_w_down, attn_w_in=v_attn_w_in, attn_sinks=v_attn_sinks, lru_w_in=v_lru_w_in,
              lru_conv_w=v_lru_conv_w, lru_conv_b=v_lru_conv_b, lru_wa=v_lru_wa, lru_ba=v_lru_ba, lru_wx=v_lru_wx,
              lru_bx=v_lru_bx, lru_lambda=v_lru_lambda)
    chip = 2 * lax.axis_index("x") + lax.axis_index("y")
    chip1 = chip.astype(jnp.int32).reshape(1)
    place = jnp.stack([lax.axis_index("c").astype(jnp.int32), chip.astype(jnp.int32)])

    W, pending = _gather_weights(P, chip1)
    loss, dx, _, pk0, full1 = _local_step(x[0], mem[0], positions[0], loss_target[0], W, pending, place)
    loss = lax.psum(loss, ("x", "y", "c"))
    big, small = _finish_grads(pk0, full1, place, chip1)

    grads, deltas, new_m, new_v = {}, {}, {}, {}
    for n in BIG:
        g = big[n]
        d, nm, nv = _adamw(_rows(P[n]), g, _rows(M1[n]), _rows(V2[n]), name=f"adamw_{n}")
        grads[n], deltas[n], new_m[n], new_v[n] = (t.reshape(P[n].shape) for t in (g, d, nm, nv))

    for n in SMALL:
        g = small[n]
        if n in SMALL_SHARDED:
            shard = P[n].shape[-1]
            g = lax.dynamic_slice_in_dim(g.reshape(-1, N_CHIPS * shard), chip * shard, shard, axis=1)
        elif n == "attn_sinks":
            g = g[:ATTN_HEADS]
        grads[n] = g.reshape(P[n].shape)
    packs = []
    for src in (P, grads, M1, V2):
        a = jnp.concatenate([_pad_rows(src[n]) for n in SMALL], axis=0)
        packs.append(jnp.pad(a, ((0, ADAM_SMALL_ROWS - a.shape[0]), (0, 0))))
    d_s, nm_s, nv_s = _adamw(*packs, name="adamw_small")
    o = 0
    for n in SMALL:
        cnt = math.prod(P[n].shape)
        r = -(-cnt // ROW)
        for dst, src in ((deltas, d_s), (new_m, nm_s), (new_v, nv_s)):
            dst[n] = src[o:o + r].reshape(-1)[:cnt].reshape(P[n].shape)
        o += r

    return (loss, dx[None], *[grads[n] for n in WEIGHTS], *[deltas[n] for n in WEIGHTS],
            *[new_m[n] for n in WEIGHTS], *[new_v[n] for n in WEIGHTS])
```

```python
import functools
import math

import jax
import jax.numpy as jnp
from jax import lax
from jax.experimental import pallas as pl
from jax.experimental.pallas import tpu as pltpu

F32 = jnp.float32
BF16 = jnp.bfloat16
MESH = pl.DeviceIdType.MESH

D_MODEL = 1024
DEPTH = 2
EPS = 1e-6
ATTN_HEADS = 16
ATTN_KV_HEADS = 4
HEAD_DIM = 64
WINDOW = 128
BLOCK = 128
ROPE_THETA = 500000.0
ROPE_DIM = 16
Q_W = 1024
KV_W = 256
MEM_LEN = 256
MEM_HEADS = 4
MEM_HEAD_DIM = 128
MEM_W = 512
LRU_BLOCKS = 8
LRU_C = 8.0
ATTN_IN_W = 2048
LRU_IN_W = 2560
MIX_OUT_W = 1536
D_FF = 4096
NEG = -1e30
N_CHIPS = 4

ADAM_LR = 0.001
ADAM_B1 = 0.9
ADAM_B2 = 0.999
ADAM_EPS = 1e-08
ADAM_WD = 0.01
ADAM_STEP = 10

LANES = 128
SCAN_ROWS = 512
VMEM_LIMIT = 56 * 1024 * 1024

NT = (((1,), (1,)), ((), ()))
TN = (((0,), (0,)), ((), ()))


def _cp(sem=None):
    return pltpu.CompilerParams(dimension_semantics=sem, vmem_limit_bytes=VMEM_LIMIT)


HBM = pl.BlockSpec(memory_space=pl.ANY)
GATHER_SEMS = 6


def _place():
    x, y, c = lax.axis_index("x"), lax.axis_index("y"), lax.axis_index("c")
    chips = [(1 - x, y), (x, 1 - y), (1 - x, 1 - y)]
    return x, y, c, chips


def _remote(src, dst, send_sems, recv_sems, k, to):
    return pltpu.make_async_remote_copy(src_ref=src, dst_ref=dst, send_sem=send_sems.at[k], recv_sem=recv_sems.at[k],
                                        device_id=to, device_id_type=MESH)


def _gather_start(o_ref, send_sems, recv_sems):
    x, y, c, chips = _place()
    half = o_ref.shape[1] // 2
    own = o_ref.at[2 * x + y, pl.ds(pl.multiple_of(c * half, 16), half)]
    for j, (cx, cy) in enumerate(chips):
        _remote(own, own, send_sems, recv_sems, j, (cx, cy, c)).start()


def _gather_finish(o_ref, send_sems, recv_sems):
    x, y, c, chips = _place()
    half = o_ref.shape[1] // 2
    my_rows = pl.ds(pl.multiple_of(c * half, 16), half)
    sib_rows = pl.ds(pl.multiple_of((1 - c) * half, 16), half)
    own = o_ref.at[2 * x + y, my_rows]
    passed = []
    for j, (cx, cy) in enumerate(chips):
        landed = o_ref.at[2 * cx + cy, my_rows]
        _remote(landed, landed, send_sems, recv_sems, j, (cx, cy, c)).wait_recv()
        fw = _remote(landed, landed, send_sems, recv_sems, 3 + j, (x, y, 1 - c))
        fw.start()
        passed.append(fw)
    for j, (cx, cy) in enumerate(chips):
        got = o_ref.at[2 * cx + cy, sib_rows]
        _remote(got, got, send_sems, recv_sems, 3 + j, (x, y, 1 - c)).wait_recv()
    for j, (cx, cy) in enumerate(chips):
        _remote(own, own, send_sems, recv_sems, j, (cx, cy, c)).wait_send()
    for fw in passed:
        fw.wait_send()


def _exchange_start(h_ref, o_ref, send_sems, recv_sems, rows=None, base=0):
    x, y, c, chips = _place()
    rows = pl.ds(0, h_ref.shape[1]) if rows is None else rows
    for j, (cx, cy) in enumerate(chips):
        _remote(h_ref.at[2 * cx + cy, rows], o_ref.at[2 * x + y, rows], send_sems, recv_sems, base + j, (cx, cy, c)).start()


def _exchange_finish(h_ref, o_ref, send_sems, recv_sems, rows=None, base=0):
    x, y, c, chips = _place()
    rows = pl.ds(0, h_ref.shape[1]) if rows is None else rows
    for j, (cx, cy) in enumerate(chips):
        got = o_ref.at[2 * cx + cy, rows]
        _remote(got, got, send_sems, recv_sems, base + j, (cx, cy, c)).wait_recv()
    for j, (cx, cy) in enumerate(chips):
        _remote(h_ref.at[2 * cx + cy, rows], o_ref.at[2 * x + y, rows], send_sems, recv_sems, base + j, (cx, cy, c)).wait_send()


def _sib_exchange_copies(g_ref, o_ref, send_sems, recv_sems):
    x, y, c, _ = _place()
    half = g_ref.shape[1] // 2
    other = pl.ds(pl.multiple_of((1 - c) * half, 8), half)
    return [_remote(g_ref.at[s, other], o_ref.at[s], send_sems, recv_sems, s, (x, y, 1 - c)) for s in range(N_CHIPS)]


def _sib_exchange_start(*refs):
    for cp in _sib_exchange_copies(*refs):
        cp.start()


def _sib_exchange_finish(*refs):
    for cp in _sib_exchange_copies(*refs):
        cp.wait()


def _sib_allgather_start(*refs):
    *o_refs, send_sems, recv_sems = refs
    x, y, c, _ = _place()
    for i, o_ref in enumerate(o_refs):
        half = o_ref.shape[0] // 2
        mine = o_ref.at[pl.ds(pl.multiple_of(c * half, 8), half)]
        _remote(mine, mine, send_sems, recv_sems, i, (x, y, 1 - c)).start()


def _sib_allgather_finish(*refs):
    *o_refs, send_sems, recv_sems = refs
    x, y, c, _ = _place()
    for i, o_ref in enumerate(o_refs):
        half = o_ref.shape[0] // 2
        mine = o_ref.at[pl.ds(pl.multiple_of(c * half, 8), half)]
        got = o_ref.at[pl.ds(pl.multiple_of((1 - c) * half, 8), half)]
        _remote(got, got, send_sems, recv_sems, i, (x, y, 1 - c)).wait_recv()
        _remote(mine, mine, send_sems, recv_sems, i, (x, y, 1 - c)).wait_send()


class _Rider:
    def __init__(self, args, start, finish, inplace=1):
        self.args, self.start, self.finish, self.inplace = list(args), start, finish, inplace


def _gather_rider(buf):
    return None if buf is None else _Rider([buf], _gather_start, _gather_finish)


def _exchange_rider(*parts):
    n = len(parts)
    assert 3 * n <= GATHER_SEMS

    def run(fn):
        def go(*refs):
            sems = refs[2 * n:]
            for i, (_, _, r0, nr) in enumerate(parts):
                fn(refs[i], refs[n + i], *sems, rows=pl.ds(r0, nr), base=3 * i)
        return go

    return _Rider([p[0] for p in parts] + [p[1] for p in parts], run(_exchange_start), run(_exchange_finish), inplace=n)


def _sib_exchange_rider(g):
    landing = lax.empty((N_CHIPS, g.shape[1] // 2, g.shape[2]), g.dtype)
    return _Rider([g, landing], _sib_exchange_start, _sib_exchange_finish)


def _sib_allgather_rider(*fulls):
    return _Rider(fulls, _sib_allgather_start, _sib_allgather_finish, inplace=len(fulls))


class _Hosted:
    def __init__(self, rider, n_in, n_out):
        self.rider = rider
        self.on = rider is not None
        self.args = rider.args if self.on else []
        k = len(self.args)
        p = self.p = rider.inplace if self.on else 0
        self.alias = {n_in + k - p + i: n_out + i for i in range(p)}
        self.in_specs = [HBM] * k
        self.out_specs = [HBM] * p
        self.out_shape = [jax.ShapeDtypeStruct(a.shape, a.dtype) for a in self.args[k - p:]]
        self.scratch = [pltpu.SemaphoreType.DMA((GATHER_SEMS,)), pltpu.SemaphoreType.DMA((GATHER_SEMS,))] if self.on else []

    def split(self, refs, n_in, n_out):
        refs = list(refs)
        if not self.on:
            return refs[:n_in], refs[n_in:n_in + n_out], refs[n_in + n_out:], None
        k, p = len(self.args), self.p
        ins, outs = refs[:n_in], refs[n_in + k:n_in + k + n_out]
        rest = refs[n_in + k + n_out + p:]
        rrefs = refs[n_in:n_in + k - p] + refs[n_in + k + n_out:n_in + k + n_out + p] + [rest[-2], rest[-1]]
        return ins, outs, rest[:-2], rrefs

    def run(self, rrefs, step, n_steps, compute):
        if rrefs is None:
            return compute()

        @pl.when(step == 0)
        def _():
            self.rider.start(*rrefs)

        compute()

        @pl.when(step == n_steps - 1)
        def _():
            self.rider.finish(*rrefs)


def _mm_nn(a, w3, *, name, out_dtype=F32, norm_g=None, resid=None, relu2=False, tm=512, gather=None):
    M, K = a.shape
    ns, _, n = w3.shape
    N = ns * n
    tm = min(tm, M)
    has_norm = norm_g is not None
    has_res = resid is not None
    n_in = 2 + has_norm + has_res
    n_out = (2 if relu2 else 1) + has_norm
    host = _Hosted(_gather_rider(gather), n_in, n_out)

    def body(*refs):
        ins, outs, _, gref = host.split(refs, n_in, n_out)
        a_ref, w_ref = ins[0], ins[1]
        g_ref = ins[2] if has_norm else None
        r_ref = ins[-1] if has_res else None

        def compute():
            if has_norm:
                xv = a_ref[...]
                rs = lax.rsqrt(jnp.mean(xv * xv, axis=-1, keepdims=True) + EPS)
                ab = (xv * rs * g_ref[...]).astype(BF16)
                outs[-1][...] = ab
            else:
                ab = a_ref[...]
            for s in range(ns):
                acc = jnp.dot(ab, w_ref[s], preferred_element_type=F32)
                sl = slice(s * n, (s + 1) * n)
                if relu2:
                    outs[0][:, sl] = acc.astype(BF16)
                    rl = jnp.maximum(acc, 0.0)
                    outs[1][:, sl] = (rl * rl).astype(BF16)
                elif has_res:
                    outs[0][:, sl] = r_ref[:, sl] + acc
                else:
                    outs[0][:, sl] = acc.astype(out_dtype)

        host.run(gref, pl.program_id(0), M // tm, compute)

    row = lambda w: pl.BlockSpec((tm, w), lambda i: (i, 0))
    in_specs = [row(K), pl.BlockSpec((ns, K, n), lambda i: (0, 0, 0))]
    args = [a, w3]
    if has_norm:
        in_specs.append(pl.BlockSpec((1, K), lambda i: (0, 0)))
        args.append(norm_g.reshape(1, K))
    if has_res:
        in_specs.append(row(N))
        args.append(resid)
    if relu2:
        out_shape = [jax.ShapeDtypeStruct((M, N), BF16), jax.ShapeDtypeStruct((M, N), BF16)]
        out_specs = [row(N), row(N)]
    else:
        out_shape = [jax.ShapeDtypeStruct((M, N), F32 if has_res else out_dtype)]
        out_specs = [row(N)]
    if has_norm:
        out_shape.append(jax.ShapeDtypeStruct((M, K), BF16))
        out_specs.append(row(K))
    res = pl.pallas_call(body, grid=(M // tm,), in_specs=in_specs + host.in_specs, out_specs=out_specs + host.out_specs,
                         out_shape=out_shape + host.out_shape, scratch_shapes=host.scratch, input_output_aliases=host.alias,
                         name=name, compiler_params=_cp(("arbitrary",) if host.on else ("parallel",)))(*args, *host.args)
    return res if len(res) > 1 else res[0]


def _mm_nt(g, w3, *, name, out_dtype=BF16, up=None, norm_x=None, norm_g=None, dres=None, tm=512):
    M = g.shape[0]
    ns, K, n = w3.shape
    tm = min(tm, M)
    has_up = up is not None
    has_norm = norm_x is not None
    has_res = dres is not None

    def body(*refs):
        refs = list(refs)
        g_ref, w_ref = refs[0], refs[1]
        pos = 2
        if has_up:
            up_ref = refs[pos]
            pos += 1
        if has_norm:
            x_ref, gn_ref = refs[pos], refs[pos + 1]
            pos += 2
        if has_res:
            r_ref = refs[pos]
            pos += 1
        outs = refs[pos:]
        acc = None
        for s in range(ns):
            part = lax.dot_general(g_ref[:, s * n:(s + 1) * n], w_ref[s], NT, preferred_element_type=F32)
            acc = part if acc is None else acc + part
        if has_up:
            outs[0][...] = (acc * (2.0 * jnp.maximum(up_ref[...].astype(F32), 0.0))).astype(BF16)
        elif has_norm:
            xv = x_ref[...]
            rs = lax.rsqrt(jnp.mean(xv * xv, axis=-1, keepdims=True) + EPS)
            xn = xv * rs
            dxn = acc * gn_ref[...]
            dx = rs * (dxn - xn * jnp.mean(dxn * xn, axis=-1, keepdims=True))
            if has_res:
                dx = dx + r_ref[...]
            outs[0][...] = dx
            outs[1][...] = dx.astype(BF16)

            @pl.when(pl.program_id(0) == 0)
            def _():
                outs[2][...] = jnp.zeros_like(outs[2])

            outs[2][...] += jnp.sum(acc * xn, axis=0, keepdims=True)
        else:
            outs[0][...] = acc.astype(out_dtype)

    row = lambda w: pl.BlockSpec((tm, w), lambda i: (i, 0))
    in_specs = [row(ns * n), pl.BlockSpec((ns, K, n), lambda i: (0, 0, 0))]
    args = [g, w3]
    if has_up:
        in_specs.append(row(K))
        args.append(up)
    if has_norm:
        in_specs += [row(K), pl.BlockSpec((1, K), lambda i: (0, 0))]
        args += [norm_x, norm_g.reshape(1, K)]
    if has_res:
        in_specs.append(row(K))
        args.append(dres)
    if has_norm:
        out_shape = [jax.ShapeDtypeStruct((M, K), F32), jax.ShapeDtypeStruct((M, K), BF16),
                     jax.ShapeDtypeStruct((1, K), F32)]
        out_specs = [row(K), row(K), pl.BlockSpec((1, K), lambda i: (0, 0))]
        sem = ("arbitrary",)
    else:
        out_shape = [jax.ShapeDtypeStruct((M, K), BF16 if has_up else out_dtype)]
        out_specs = [row(K)]
        sem = ("parallel",)
    res = pl.pallas_call(body, grid=(M // tm,), in_specs=in_specs, out_specs=out_specs, out_shape=out_shape,
                         name=name, compiler_params=_cp(sem))(*args)
    return res if len(res) > 1 else res[0]


def _mm_tn(a, g, ns, *, name, tk=512, tm=4096, packed=None, rider=None):
    M, K = a.shape
    n = g.shape[1] // ns
    tm = min(tm, M)
    tk = min(tk, K)
    nk, nm = K // tk, M // tm
    n_in = 3 if (packed is not None and packed[0] is not None) else 2
    host = _Hosted(rider, n_in, 1)

    def body(*refs):
        ins, outs, _, rrefs = host.split(refs, n_in, 1)
        a_ref, g_ref, o_ref = ins[0], ins[1], outs[0]

        def compute():
            @pl.when(pl.program_id(2) == 0)
            def _():
                o_ref[...] = jnp.zeros_like(o_ref)

            o_ref[0] += lax.dot_general(a_ref[...], g_ref[...], TN, preferred_element_type=F32)

        step = (pl.program_id(0) * nk + pl.program_id(1)) * nm + pl.program_id(2)
        host.run(rrefs, step, ns * nk * nm, compute)

    in_specs = [pl.BlockSpec((tm, tk), lambda s, k, m: (m, k)), pl.BlockSpec((tm, n), lambda s, k, m: (m, s))]
    args = [a, g]
    alias = {}
    if packed is None:
        out_spec = pl.BlockSpec((1, tk, n), lambda s, k, m: (s, k, 0))
        out_shape = jax.ShapeDtypeStruct((ns, K, n), F32)
    else:
        buf, rows, off = packed
        per_chip = K * ns // N_CHIPS
        assert n == ROW and per_chip % tk == 0 and off % tk == 0
        if ns == N_CHIPS:
            out_spec = pl.BlockSpec((1, tk, n), lambda s, k, m: (s, off // tk + k, 0))
        else:
            kpc = per_chip // tk
            out_spec = pl.BlockSpec((1, tk, n), lambda s, k, m: (k // kpc, off // tk + k % kpc, 0))
        out_shape = jax.ShapeDtypeStruct((N_CHIPS, rows, ROW), F32)
        if buf is not None:
            in_specs.append(HBM)
            args.append(buf)
            alias = {2: 0}
    sem = ("arbitrary",) * 3 if host.on else ("parallel", "parallel", "arbitrary")
    res = pl.pallas_call(
        body, grid=(ns, nk, nm), in_specs=in_specs + host.in_specs, out_specs=[out_spec] + host.out_specs,
        out_shape=[out_shape] + host.out_shape, scratch_shapes=host.scratch, name=name,
        input_output_aliases={**alias, **host.alias}, compiler_params=_cp(sem))(*args, *host.args)
    return res if host.on else res[0]


def _final(x, gain, target, *, name="final_loss", tr=256):
    S, Dm = x.shape
    tr = min(tr, S)

    def body(x_ref, g_ref, t_ref, loss_ref, dx_ref, dxb_ref, dg_ref):
        @pl.when(pl.program_id(0) == 0)
        def _():
            loss_ref[...] = jnp.zeros_like(loss_ref)
            dg_ref[...] = jnp.zeros_like(dg_ref)

        xv = x_ref[...]
        gv = g_ref[...]
        rs = lax.rsqrt(jnp.mean(xv * xv, axis=-1, keepdims=True) + EPS)
        xn = xv * rs
        err = xn * gv - t_ref[...]
        loss_ref[...] += 0.5 * jnp.sum(jnp.mean(err * err, axis=-1, keepdims=True), axis=0, keepdims=True)
        dout = err * (1.0 / Dm)
        dg_ref[...] += jnp.sum(dout * xn, axis=0, keepdims=True)
        dxn = dout * gv
        dx = rs * (dxn - xn * jnp.mean(dxn * xn, axis=-1, keepdims=True))
        dx_ref[...] = dx
        dxb_ref[...] = dx.astype(BF16)

    row = pl.BlockSpec((tr, Dm), lambda i: (i, 0))
    return pl.pallas_call(
        body, grid=(S // tr,),
        in_specs=[row, pl.BlockSpec((1, Dm), lambda i: (0, 0)), row],
        out_specs=[pl.BlockSpec((1, 1), lambda i: (0, 0)), row, row, pl.BlockSpec((1, Dm), lambda i: (0, 0))],
        out_shape=[jax.ShapeDtypeStruct((1, 1), F32), jax.ShapeDtypeStruct((S, Dm), F32),
                   jax.ShapeDtypeStruct((S, Dm), BF16), jax.ShapeDtypeStruct((1, Dm), F32)],
        name=name, compiler_params=_cp(("arbitrary",)))(x, gain.reshape(1, Dm), target)


def _rope_tables(positions):
    half = ROPE_DIM // 2
    inv_freq = ROPE_THETA ** (-2.0 * jnp.arange(half, dtype=F32) / ROPE_DIM)
    ang = positions.astype(F32)[:, None] * inv_freq
    cos, sin = jnp.cos(ang), jnp.sin(ang)
    S = positions.shape[0]
    ones = jnp.ones((S, HEAD_DIM - ROPE_DIM), F32)
    cos64 = jnp.concatenate([cos, cos, ones], axis=1)
    sin64 = jnp.concatenate([-sin, sin, 0.0 * ones], axis=1)
    return jnp.tile(cos64, (1, 2)), jnp.tile(sin64, (1, 2))


def _rope_partner(t):
    lane = lax.broadcasted_iota(jnp.int32, t.shape, 1)
    low = (lane & (HEAD_DIM - 1)) < (ROPE_DIM // 2)
    return jnp.where(low, pltpu.roll(t, LANES - ROPE_DIM // 2, 1), pltpu.roll(t, ROPE_DIM // 2, 1))


def _qk_prep(p, cos_t, sin_t, *, name="qk_prep", tr=256, gather=None):
    S = p.shape[0]
    tr = min(tr, S)
    scale = HEAD_DIM ** -0.5
    host = _Hosted(_gather_rider(gather), 3, 4)

    def body(*refs):
        ins, outs, _, gref = host.split(refs, 3, 4)
        host.run(gref, pl.program_id(0), S // tr, lambda: inner(*ins, *outs))

    def inner(p_ref, c_ref, s_ref, q_ref, k_ref, v_ref, va_ref):
        cs, sn = c_ref[...], s_ref[...]
        lane = lax.broadcasted_iota(jnp.int32, (tr, LANES), 1)
        lo = lane < HEAD_DIM
        for c in range(Q_W // LANES):
            t = p_ref[:, c * LANES:(c + 1) * LANES]
            q_ref[:, c * LANES:(c + 1) * LANES] = ((t * cs + _rope_partner(t) * sn) * scale).astype(BF16)
        for c in range(KV_W // LANES):
            t = p_ref[:, Q_W + c * LANES:Q_W + (c + 1) * LANES]
            kc = t * cs + _rope_partner(t) * sn
            vc = p_ref[:, Q_W + KV_W + c * LANES:Q_W + KV_W + (c + 1) * LANES]
            for arr, ref in ((kc, k_ref), (vc, v_ref)):
                sw = pltpu.roll(arr, HEAD_DIM, 1)
                ref[:, (2 * c) * LANES:(2 * c + 1) * LANES] = jnp.where(lo, arr, sw).astype(BF16)
                ref[:, (2 * c + 1) * LANES:(2 * c + 2) * LANES] = jnp.where(lo, sw, arr).astype(BF16)
            sw = pltpu.roll(vc, HEAD_DIM, 1)
            for k, aug in enumerate((jnp.where(lo, vc, 1.0), jnp.where(lo, 1.0, sw), jnp.where(lo, sw, 1.0), jnp.where(lo, 1.0, vc))):
                va_ref[:, (4 * c + k) * LANES:(4 * c + k + 1) * LANES] = aug.astype(BF16)

    row = lambda w: pl.BlockSpec((tr, w), lambda i: (i, 0))
    return pl.pallas_call(
        body, grid=(S // tr,), in_specs=[row(ATTN_IN_W), row(LANES), row(LANES)] + host.in_specs,
        out_specs=[row(Q_W), row(2 * KV_W), row(2 * KV_W), row(4 * KV_W)] + host.out_specs,
        out_shape=[jax.ShapeDtypeStruct((S, Q_W), BF16), jax.ShapeDtypeStruct((S, 2 * KV_W), BF16),
                   jax.ShapeDtypeStruct((S, 2 * KV_W), BF16), jax.ShapeDtypeStruct((S, 4 * KV_W), BF16)] + host.out_shape,
        scratch_shapes=host.scratch, input_output_aliases=host.alias,
        name=name, compiler_params=_cp(("arbitrary",) if host.on else ("parallel",)))(p, cos_t, sin_t, *host.args)


def _qk_prep_bwd(dq, dk, dv, dmq, cos_t, sin_t, *, name="qk_prep_bwd", tr=256):
    S = dq.shape[0]
    tr = min(tr, S)

    def body(dq_ref, dk_ref, dv_ref, dmq_ref, c_ref, s_ref, o_ref):
        cs, sn = c_ref[...], s_ref[...]
        for c in range(Q_W // LANES):
            t = dq_ref[:, c * LANES:(c + 1) * LANES]
            o_ref[:, c * LANES:(c + 1) * LANES] = (t * cs - _rope_partner(t) * sn).astype(BF16)
        for c in range(KV_W // LANES):
            t = dk_ref[:, c * LANES:(c + 1) * LANES]
            o_ref[:, Q_W + c * LANES:Q_W + (c + 1) * LANES] = (t * cs - _rope_partner(t) * sn).astype(BF16)
        o_ref[:, Q_W + KV_W:Q_W + 2 * KV_W] = dv_ref[...].astype(BF16)
        o_ref[:, Q_W + 2 * KV_W:] = dmq_ref[...]

    row = lambda w: pl.BlockSpec((tr, w), lambda i: (i, 0))
    return pl.pallas_call(
        body, grid=(S // tr,), in_specs=[row(Q_W), row(KV_W), row(KV_W), row(MEM_W), row(LANES), row(LANES)],
        out_specs=row(ATTN_IN_W), out_shape=jax.ShapeDtypeStruct((S, ATTN_IN_W), BF16),
        name=name, compiler_params=_cp(("parallel",)))(dq, dk, dv, dmq, cos_t, sin_t)


def _band(n, S):
    start = pl.multiple_of(jnp.clip((n - 1) * BLOCK, 0, S - 3 * BLOCK), BLOCK)
    qi = lax.broadcasted_iota(jnp.int32, (BLOCK, 3 * BLOCK), 0) + n * BLOCK
    ki = lax.broadcasted_iota(jnp.int32, (BLOCK, 3 * BLOCK), 1) + start
    return start, jnp.abs(ki - qi) <= WINDOW


def _head_operand(ref, h, lo):
    c = h // 2
    t = ref[:, c * LANES:(c + 1) * LANES].astype(F32)
    return jnp.where(lo if h % 2 == 0 else jnp.logical_not(lo), t, 0.0).astype(BF16)


GROUP = ATTN_HEADS // ATTN_KV_HEADS
EVENS_FIRST = (0, 2, 1, 3)


def _attn_fwd(q, kd, va, sinks, *, name="attn_fwd", gather=None):
    S = q.shape[0]
    host = _Hosted(_gather_rider(gather), 4, 2)

    def body(*refs):
        ins, outs, scr, gref = host.split(refs, 4, 2)
        host.run(gref, pl.program_id(0), S // BLOCK, lambda: inner(*ins, *outs, *scr))

    def inner(sink_ref, q_ref, k_ref, va_ref, o_ref, lse_ref, p_scr):
        n = pl.program_id(0)
        start, mask = _band(n, S)
        lane = lax.broadcasted_iota(jnp.int32, (BLOCK, LANES), 1)
        lo = lane < HEAD_DIM
        rows = pl.ds(start, 3 * BLOCK)
        scores = []
        for g in range(ATTN_KV_HEADS):
            qst = jnp.concatenate([_head_operand(q_ref, GROUP * g + j, lo) for j in EVENS_FIRST], axis=0)
            scores.append(lax.dot_general(qst, k_ref[rows, g * LANES:(g + 1) * LANES], NT, preferred_element_type=F32))
        ms = {}
        for g in range(ATTN_KV_HEADS):
            for pos, j in enumerate(EVENS_FIRST):
                h = GROUP * g + j
                s = jnp.where(mask, scores[g][pos * BLOCK:(pos + 1) * BLOCK], NEG)
                ms[h] = jnp.maximum(jnp.max(s, axis=-1, keepdims=True), sink_ref[h])
                p_scr[(GROUP * g + pos) * BLOCK:(GROUP * g + pos + 1) * BLOCK, :] = jnp.exp(s - ms[h]).astype(BF16)
        pvs = {}
        for g in range(ATTN_KV_HEADS):
            for par in range(2):
                r0 = (GROUP * g + 2 * par) * BLOCK
                pvs[g, par] = jnp.dot(p_scr[r0:r0 + 2 * BLOCK, :], va_ref[rows, (2 * g + par) * LANES:(2 * g + par + 1) * LANES],
                                      preferred_element_type=F32)
        lse_blk = jnp.zeros((BLOCK, LANES), F32)
        for g in range(ATTN_KV_HEADS):
            outs = {}
            for par in range(2):
                for k in range(2):
                    j = EVENS_FIRST[2 * par + k]
                    h = GROUP * g + j
                    pv = pvs[g, par][k * BLOCK:(k + 1) * BLOCK]
                    den = pltpu.roll(pv, HEAD_DIM, 1) + jnp.exp(sink_ref[h] - ms[h])
                    outs[j] = pv * (1.0 / den)
                    l = den[:, par * HEAD_DIM:par * HEAD_DIM + 1]
                    lse_blk = jnp.where(lane == h, ms[h] + jnp.log(l), lse_blk)
            for jj in range(2):
                o_ref[:, (2 * g + jj) * LANES:(2 * g + jj + 1) * LANES] = jnp.where(lo, outs[2 * jj], outs[2 * jj + 1]).astype(BF16)
        lse_ref[...] = lse_blk

    full = lambda w: pl.BlockSpec((S, w), lambda i: (0, 0))
    return pl.pallas_call(
        body, grid=(S // BLOCK,),
        in_specs=[pl.BlockSpec(memory_space=pltpu.SMEM), pl.BlockSpec((BLOCK, Q_W), lambda i: (i, 0)),
                  full(2 * KV_W), full(4 * KV_W)] + host.in_specs,
        out_specs=[pl.BlockSpec((BLOCK, Q_W), lambda i: (i, 0)), pl.BlockSpec((BLOCK, LANES), lambda i: (i, 0))] + host.out_specs,
        out_shape=[jax.ShapeDtypeStruct((S, MIX_OUT_W), BF16), jax.ShapeDtypeStruct((S, LANES), F32)] + host.out_shape,
        scratch_shapes=[pltpu.VMEM((ATTN_HEADS * BLOCK, 3 * BLOCK), BF16)] + host.scratch, input_output_aliases=host.alias,
        name=name, compiler_params=_cp(("arbitrary",) if host.on else ("parallel",)))(sinks, q, kd, va, *host.args)


def _attn_bwd(q, kd, vd, ao, lse, sinks, dcat, *, name="attn_bwd", rider=None):
    S = q.shape[0]
    scale = HEAD_DIM ** -0.5
    host = _Hosted(rider, 7, 4)

    def body(*refs):
        ins, outs, scr, rrefs = host.split(refs, 7, 4)
        host.run(rrefs, pl.program_id(0), S // BLOCK, lambda: inner(*ins, *outs, *scr))

    def inner(sink_ref, q_ref, k_ref, v_ref, ao_ref, lse_ref, do_ref, dq_ref, dk_ref, dv_ref, ds_ref, p_scr, dsb_scr):
        n = pl.program_id(0)

        @pl.when(n == 0)
        def _():
            dk_ref[...] = jnp.zeros_like(dk_ref)
            dv_ref[...] = jnp.zeros_like(dv_ref)
            ds_ref[...] = jnp.zeros_like(ds_ref)

        start, mask = _band(n, S)
        lane = lax.broadcasted_iota(jnp.int32, (BLOCK, LANES), 1)
        lo = lane < HEAD_DIM
        lane3 = lax.broadcasted_iota(jnp.int32, (3 * BLOCK, LANES), 1)
        row8 = lax.broadcasted_iota(jnp.int32, (8, LANES), 0)
        lane8 = lax.broadcasted_iota(jnp.int32, (8, LANES), 1)
        dsink = jnp.zeros((8, LANES), F32)
        lse_blk = lse_ref[...]
        rows = pl.ds(start, 3 * BLOCK)
        lses, deltas = {}, {}
        for c in range(Q_W // LANES):
            prod = do_ref[:, c * LANES:(c + 1) * LANES].astype(F32) * ao_ref[:, c * LANES:(c + 1) * LANES].astype(F32)
            for k in range(2):
                h = 2 * c + k
                deltas[h] = jnp.sum(jnp.where(lo if k == 0 else jnp.logical_not(lo), prod, 0.0), axis=1, keepdims=True)
                lses[h] = jnp.sum(jnp.where(lane == h, lse_blk, 0.0), axis=1, keepdims=True)
                val = -jnp.sum(jnp.exp(sink_ref[h] - lses[h]) * deltas[h], axis=0, keepdims=True)
                dsink = dsink + jnp.where((row8 == 0) & (lane8 == h), val, 0.0)
        stack = lambda ref, g: jnp.concatenate([_head_operand(ref, GROUP * g + j, lo) for j in range(GROUP)], axis=0)
        ss, dps = [], []
        for g in range(ATTN_KV_HEADS):
            ss.append(lax.dot_general(stack(q_ref, g), k_ref[rows, g * LANES:(g + 1) * LANES], NT, preferred_element_type=F32))
            dps.append(lax.dot_general(stack(do_ref, g), v_ref[rows, g * LANES:(g + 1) * LANES], NT, preferred_element_type=F32))
        for g in range(ATTN_KV_HEADS):
            for j in range(GROUP):
                h = GROUP * g + j
                r = slice(j * BLOCK, (j + 1) * BLOCK)
                hr = slice(h * BLOCK, (h + 1) * BLOCK)
                p = jnp.exp(jnp.where(mask, ss[g][r], NEG) - lses[h])
                p_scr[hr, :] = p.astype(BF16)
                dsb_scr[hr, :] = (p * (dps[g][r] - deltas[h])).astype(BF16)
        for g in range(ATTN_KV_HEADS):
            cols = slice((g // 2) * LANES, (g // 2 + 1) * LANES)
            gr = slice(GROUP * g * BLOCK, GROUP * (g + 1) * BLOCK)
            dsg = dsb_scr[gr, :]
            dqs = jnp.dot(dsg, k_ref[rows, g * LANES:(g + 1) * LANES], preferred_element_type=F32) * scale
            for jj in range(2):
                dq_ref[:, (2 * g + jj) * LANES:(2 * g + jj + 1) * LANES] = jnp.where(
                    lo, dqs[(2 * jj) * BLOCK:(2 * jj + 1) * BLOCK], dqs[(2 * jj + 1) * BLOCK:(2 * jj + 2) * BLOCK])
            half = (lane3 < HEAD_DIM) if g % 2 == 0 else (lane3 >= HEAD_DIM)
            dkr = lax.dot_general(dsg, stack(q_ref, g), TN, preferred_element_type=F32)
            dk_ref[rows, cols] += jnp.where(half, dkr + pltpu.roll(dkr, HEAD_DIM, 1), 0.0)
            dvr = lax.dot_general(p_scr[gr, :], stack(do_ref, g), TN, preferred_element_type=F32)
            dv_ref[rows, cols] += jnp.where(half, dvr + pltpu.roll(dvr, HEAD_DIM, 1), 0.0)
        ds_ref[...] += dsink

    full = lambda w: pl.BlockSpec((S, w), lambda i: (0, 0))
    blk = lambda w: pl.BlockSpec((BLOCK, w), lambda i: (i, 0))
    return pl.pallas_call(
        body, grid=(S // BLOCK,),
        in_specs=[pl.BlockSpec(memory_space=pltpu.SMEM), blk(Q_W), full(2 * KV_W), full(2 * KV_W), blk(Q_W), blk(LANES), blk(Q_W)]
        + host.in_specs,
        out_specs=[blk(Q_W), full(KV_W), full(KV_W), pl.BlockSpec((8, LANES), lambda i: (0, 0))] + host.out_specs,
        out_shape=[jax.ShapeDtypeStruct((S, Q_W), F32), jax.ShapeDtypeStruct((S, KV_W), F32),
                   jax.ShapeDtypeStruct((S, KV_W), F32), jax.ShapeDtypeStruct((8, LANES), F32)] + host.out_shape,
        scratch_shapes=[pltpu.VMEM((ATTN_HEADS * BLOCK, 3 * BLOCK), BF16), pltpu.VMEM((ATTN_HEADS * BLOCK, 3 * BLOCK), BF16)]
        + host.scratch, input_output_aliases=host.alias,
        name=name, compiler_params=_cp(("arbitrary",)))(sinks, q, kd, vd, ao, lse, dcat, *host.args)


def _mem_probs(q_ref, kv_ref, h):
    scale = MEM_HEAD_DIM ** -0.5
    qh = q_ref[:, h * LANES:(h + 1) * LANES].astype(BF16)
    s = lax.dot_general(qh, kv_ref[:, h * LANES:(h + 1) * LANES], NT, preferred_element_type=F32) * scale
    m = jnp.max(s, axis=-1, keepdims=True)
    pe = jnp.exp(s - m)
    return qh, pe * (1.0 / jnp.sum(pe, axis=-1, keepdims=True))


def _memattn_fwd(p, qblk, kv, cat, *, name="memattn_fwd", tr=512):
    S = p.shape[0]
    tr = min(tr, S)

    def body(q_ref, kv_ref, cat_ref, o_ref):
        for h in range(MEM_HEADS):
            _, pr = _mem_probs(q_ref, kv_ref, h)
            o = jnp.dot(pr.astype(BF16), kv_ref[:, MEM_W + h * LANES:MEM_W + (h + 1) * LANES], preferred_element_type=F32)
            o_ref[:, h * LANES:(h + 1) * LANES] = o.astype(BF16)

    return pl.pallas_call(
        body, grid=(S // tr,),
        in_specs=[pl.BlockSpec((tr, MEM_W), lambda i: (i, qblk)), pl.BlockSpec((MEM_LEN, 2 * MEM_W), lambda i: (0, 0)), HBM],
        out_specs=pl.BlockSpec((tr, MEM_W), lambda i: (i, Q_W // MEM_W)),
        out_shape=jax.ShapeDtypeStruct((S, MIX_OUT_W), BF16), input_output_aliases={2: 0},
        name=name, compiler_params=_cp(("parallel",)))(p, kv, cat)


def _memattn_bwd(p, qblk, kv, dcat, *, name="memattn_bwd", tr=512):
    S = p.shape[0]
    tr = min(tr, S)
    scale = MEM_HEAD_DIM ** -0.5

    def body(q_ref, kv_ref, do_ref, dq_ref, dkv_ref):
        @pl.when(pl.program_id(0) == 0)
        def _():
            dkv_ref[...] = jnp.zeros_like(dkv_ref)

        for h in range(MEM_HEADS):
            qh, pr = _mem_probs(q_ref, kv_ref, h)
            doh = do_ref[:, h * LANES:(h + 1) * LANES]
            dp = lax.dot_general(doh, kv_ref[:, MEM_W + h * LANES:MEM_W + (h + 1) * LANES], NT, preferred_element_type=F32)
            delta = jnp.sum(pr * dp, axis=-1, keepdims=True)
            dsb = (pr * (dp - delta) * scale).astype(BF16)
            dq = jnp.dot(dsb, kv_ref[:, h * LANES:(h + 1) * LANES], preferred_element_type=F32)
            dq_ref[:, h * LANES:(h + 1) * LANES] = dq.astype(BF16)
            dkv_ref[:, h * LANES:(h + 1) * LANES] += lax.dot_general(dsb, qh, TN, preferred_element_type=F32)
            dkv_ref[:, MEM_W + h * LANES:MEM_W + (h + 1) * LANES] += lax.dot_general(
                pr.astype(BF16), doh, TN, preferred_element_type=F32)

    return pl.pallas_call(
        body, grid=(S // tr,),
        in_specs=[pl.BlockSpec((tr, MEM_W), lambda i: (i, qblk)), pl.BlockSpec((MEM_LEN, 2 * MEM_W), lambda i: (0, 0)),
                  pl.BlockSpec((tr, MEM_W), lambda i: (i, Q_W // MEM_W))],
        out_specs=[pl.BlockSpec((tr, MEM_W), lambda i: (i, 0)), pl.BlockSpec((MEM_LEN, 2 * MEM_W), lambda i: (0, 0))],
        out_shape=[jax.ShapeDtypeStruct((S, MEM_W), BF16), jax.ShapeDtypeStruct((MEM_LEN, 2 * MEM_W), F32)],
        name=name, compiler_params=_cp(("arbitrary",)))(p, kv, dcat)


def _sqrt(v):
    return jnp.where(v > 0.0, v * lax.rsqrt(v), 0.0)


def _sigmoid(z):
    return 1.0 / (1.0 + jnp.exp(-z))


def _one_minus_exp(z, exp_z):
    poly = z * (1.0 + z * (0.5 + z * (1.0 / 6.0 + z * (1.0 / 24.0 + z * (1.0 / 120.0)))))
    return jnp.where(z > -0.1, -poly, 1.0 - exp_z)


def _softplus_neg(lam):
    z = -lam
    return jnp.maximum(z, 0.0) + jnp.log(1.0 + jnp.exp(-jnp.abs(z)))


_GELU_C = math.sqrt(2.0 / math.pi)


def _gelu(z):
    return 0.5 * z * (1.0 + jnp.tanh(_GELU_C * (z + 0.044715 * z * z * z)))


def _row_or_zero(ref, t, S):
    ok = jnp.logical_and(t >= 0, t < S)
    return jnp.where(ok, ref[pl.ds(jnp.clip(t, 0, S - 1), 1), :], 0.0)


def _shift_down(v, first):
    ri = lax.broadcasted_iota(jnp.int32, v.shape, 0)
    return jnp.where(ri == 0, first, pltpu.roll(v, 1, 0))


def _shift_up(v, last):
    T = v.shape[0]
    ri = lax.broadcasted_iota(jnp.int32, v.shape, 0)
    return jnp.where(ri == T - 1, last, pltpu.roll(v, T - 1, 0))


def _scan_chunk(a, u, reverse):
    T = a.shape[0]
    ri = lax.broadcasted_iota(jnp.int32, a.shape, 0)
    d = 1
    while d < T:
        if reverse:
            a_s, u_s, ok = pltpu.roll(a, T - d, 0), pltpu.roll(u, T - d, 0), ri < T - d
        else:
            a_s, u_s, ok = pltpu.roll(a, d, 0), pltpu.roll(u, d, 0), ri >= d
        u = jnp.where(ok, a * u_s + u, u)
        a = jnp.where(ok, a * a_s, a)
        d *= 2
    return a, u


def _conv_taps(xb_ref, t0, S):
    T = SCAN_ROWS
    x0 = xb_ref[pl.ds(t0, T), :]
    xm1 = _shift_down(x0, _row_or_zero(xb_ref, t0 - 1, S))
    nxt0 = _row_or_zero(xb_ref, t0 + T, S)
    xp1 = _shift_up(x0, nxt0)
    xp2 = _shift_up(xp1, _row_or_zero(xb_ref, t0 + T + 1, S))
    return xm1, x0, xp1, xp2


def _lru_gates(xc, w_a, b_a, w_x, b_x, sp):
    xcb = xc.astype(BF16)
    r = _sigmoid(jnp.dot(xcb, w_a, preferred_element_type=F32) + b_a)
    i = _sigmoid(jnp.dot(xcb, w_x, preferred_element_type=F32) + b_x)
    la = -LRU_C * r * sp
    a = jnp.exp(la)
    return r, i, a, _sqrt(_one_minus_exp(2.0 * la, a * a))


def _lru_specs(S):
    col = lambda off: pl.BlockSpec((S, LANES), lambda n: (0, n + off), pipeline_mode=pl.Buffered(1))
    small = lambda r: pl.BlockSpec((r, LANES), lambda n: (0, n))
    wblk = pl.BlockSpec((2, 1, LANES, LANES), lambda n: (0, n, 0, 0))
    return col, small, wblk


def _lru_fwd(p, conv_w, conv_b, wa, ba, wx, bx, lam, *, name="lru_fwd"):
    S = p.shape[0]
    T = SCAN_ROWS
    nc = S // T

    def body(xb_ref, gate_ref, cw_ref, cb_ref, wa_ref, ba_ref, wx_ref, bx_ref, lam_ref, y_ref, hf_ref, hr_ref, xc_v):
        sp = _softplus_neg(lam_ref[...])
        cw = cw_ref[...]

        def fwd_step(c, h_in):
            t0 = pl.multiple_of(c * T, T)
            xm1, x0, xp1, xp2 = _conv_taps(xb_ref, t0, S)
            xc = cb_ref[...] + xm1 * cw[0:1] + x0 * cw[1:2] + xp1 * cw[2:3] + xp2 * cw[3:4]
            xc_v[pl.ds(t0, T), :] = xc
            _, i, a, beta = _lru_gates(xc, wa_ref[0, 0], ba_ref[0:1], wx_ref[0, 0], bx_ref[0:1], sp[0:1])
            A, U = _scan_chunk(a, beta * (i * xc), False)
            hf_ref[pl.ds(t0, T), :] = A * h_in + U
            return hf_ref[pl.ds(t0 + T - 1, 1), :]

        lax.fori_loop(0, nc, fwd_step, jnp.zeros((1, LANES), F32))

        def rev_step(k, h_in):
            t0 = pl.multiple_of((nc - 1 - k) * T, T)
            xc = xc_v[pl.ds(t0, T), :]
            _, i, a, beta = _lru_gates(xc, wa_ref[1, 0], ba_ref[1:2], wx_ref[1, 0], bx_ref[1:2], sp[1:2])
            A, U = _scan_chunk(a, beta * (i * xc), True)
            h = A * h_in + U
            hr_ref[pl.ds(t0, T), :] = h
            y_ref[pl.ds(t0, T), :] = ((hf_ref[pl.ds(t0, T), :] + h) * _gelu(gate_ref[pl.ds(t0, T), :])).astype(BF16)
            return hr_ref[pl.ds(t0, 1), :]

        lax.fori_loop(0, nc, rev_step, jnp.zeros((1, LANES), F32))

    col, small, wblk = _lru_specs(S)
    colo = lambda: pl.BlockSpec((S, LANES), lambda n: (0, n))
    return pl.pallas_call(
        body, grid=(LRU_BLOCKS,),
        in_specs=[col(0), col(LRU_BLOCKS), small(4), small(1), wblk, small(2), wblk, small(2), small(2)],
        out_specs=[colo(), colo(), colo()],
        out_shape=[jax.ShapeDtypeStruct((S, MIX_OUT_W), BF16), jax.ShapeDtypeStruct((S, D_MODEL), F32),
                   jax.ShapeDtypeStruct((S, D_MODEL), F32)],
        scratch_shapes=[pltpu.VMEM((S, LANES), F32)],
        name=name, compiler_params=_cp(("parallel",)))(p, p, conv_w, conv_b, wa, ba, wx, bx, lam)


def _lru_bwd(p, hf, hr, dcat, conv_w, conv_b, wa, ba, wx, bx, lam, *, name="lru_bwd"):
    S = p.shape[0]
    T = SCAN_ROWS
    nc = S // T

    def body(xb_ref, gate_ref, hf_ref, hr_ref, dy_ref, cw_ref, cb_ref, wa_ref, ba_ref, wx_ref, bx_ref, lam_ref,
             dxb_ref, dgate_ref, dcw_ref, dcb_ref, dwa_ref, dba_ref, dwx_ref, dbx_ref, dlam_ref, xc_v, dxc_v, dh_v):
        lam_v = lam_ref[...]
        sp = _softplus_neg(lam_v)
        cw = cw_ref[...]
        for ref in (dcw_ref, dcb_ref, dwa_ref, dba_ref, dwx_ref, dbx_ref, dlam_ref):
            ref[...] = jnp.zeros_like(ref)

        def prep_step(c, carry):
            t0 = pl.multiple_of(c * T, T)
            rows = pl.ds(t0, T)
            xm1, x0, xp1, xp2 = _conv_taps(xb_ref, t0, S)
            xc_v[rows, :] = cb_ref[...] + xm1 * cw[0:1] + x0 * cw[1:2] + xp1 * cw[2:3] + xp2 * cw[3:4]
            z = gate_ref[rows, :]
            dy = dy_ref[rows, :].astype(F32)
            th = jnp.tanh(_GELU_C * (z + 0.044715 * z * z * z))
            dgelu = 0.5 * (1.0 + th) + 0.5 * z * (1.0 - th * th) * _GELU_C * (1.0 + 3.0 * 0.044715 * z * z)
            dgate_ref[rows, :] = (dy * (hf_ref[rows, :] + hr_ref[rows, :]) * dgelu).astype(BF16)
            dh_v[rows, :] = dy * (0.5 * z * (1.0 + th))
            return carry

        lax.fori_loop(0, nc, prep_step, 0)

        def direction(d):
            h_ref = hf_ref if d == 0 else hr_ref
            w_a, w_x = wa_ref[d, 0], wx_ref[d, 0]
            b_a, b_x, sp_d = ba_ref[d:d + 1], bx_ref[d:d + 1], sp[d:d + 1]

            def step(k, carry):
                g_in, a_in = carry
                c = (nc - 1 - k) if d == 0 else k
                t0 = pl.multiple_of(c * T, T)
                rows = pl.ds(t0, T)
                xc = xc_v[rows, :]
                r, i, a, beta = _lru_gates(xc, w_a, b_a, w_x, b_x, sp_d)
                dh = dh_v[rows, :]
                hc = h_ref[rows, :]
                if d == 0:
                    A, U = _scan_chunk(_shift_up(a, a_in), dh, True)
                    g = A * g_in + U
                    h_nb = _shift_down(hc, _row_or_zero(h_ref, t0 - 1, S))
                    nxt = (g[0:1], a[0:1])
                else:
                    A, U = _scan_chunk(_shift_down(a, a_in), dh, False)
                    g = A * g_in + U
                    h_nb = _shift_up(hc, _row_or_zero(h_ref, t0 + T, S))
                    nxt = (g[T - 1:T], a[T - 1:T])
                da = g * h_nb
                dbeta = g * (i * xc)
                tb = g * beta
                dla = da * a - dbeta * (a * a / beta)
                dzr = (dla * (-LRU_C * sp_d)) * (r * (1.0 - r))
                dzi = (tb * xc) * (i * (1.0 - i))
                dzrb, dzib, xcb = dzr.astype(BF16), dzi.astype(BF16), xc.astype(BF16)
                dwa_ref[d, 0] += lax.dot_general(xcb, dzrb, TN, preferred_element_type=F32)
                dwx_ref[d, 0] += lax.dot_general(xcb, dzib, TN, preferred_element_type=F32)
                dba_ref[d:d + 1] += jnp.sum(dzr, axis=0, keepdims=True)
                dbx_ref[d:d + 1] += jnp.sum(dzi, axis=0, keepdims=True)
                dlam_ref[d:d + 1] += jnp.sum(dla * (-LRU_C * r), axis=0, keepdims=True)
                dxc = (tb * i + lax.dot_general(dzrb, w_a, NT, preferred_element_type=F32)
                       + lax.dot_general(dzib, w_x, NT, preferred_element_type=F32))
                if d == 0:
                    dxc_v[rows, :] = dxc
                else:
                    dxc_v[rows, :] += dxc
                return nxt

            lax.fori_loop(0, nc, step, (jnp.zeros((1, LANES), F32), jnp.zeros((1, LANES), F32)))

        direction(0)
        direction(1)
        dlam_ref[...] = dlam_ref[...] * (-1.0 / (1.0 + jnp.exp(lam_v)))

        def conv_step(c, carry):
            t0 = pl.multiple_of(c * T, T)
            rows = pl.ds(t0, T)
            g0 = dxc_v[rows, :]
            gm1 = _shift_down(g0, _row_or_zero(dxc_v, t0 - 1, S))
            gm2 = _shift_down(gm1, _row_or_zero(dxc_v, t0 - 2, S))
            gp1 = _shift_up(g0, _row_or_zero(dxc_v, t0 + T, S))
            dxb_ref[rows, :] = (cw[0:1] * gp1 + cw[1:2] * g0 + cw[2:3] * gm1 + cw[3:4] * gm2).astype(BF16)
            xm1, x0, xp1, xp2 = _conv_taps(xb_ref, t0, S)
            for tap, xs in enumerate((xm1, x0, xp1, xp2)):
                dcw_ref[tap:tap + 1] += jnp.sum(g0 * xs, axis=0, keepdims=True)
            dcb_ref[...] += jnp.sum(g0, axis=0, keepdims=True)
            return carry

        lax.fori_loop(0, nc, conv_step, 0)

    col, small, wblk = _lru_specs(S)
    colo = lambda: pl.BlockSpec((S, LANES), lambda n: (0, n), pipeline_mode=pl.Buffered(1))
    return pl.pallas_call(
        body, grid=(LRU_BLOCKS,),
        in_specs=[col(0), col(LRU_BLOCKS), col(0), col(0), col(0), small(4), small(1), wblk, small(2), wblk, small(2), small(2)],
        out_specs=[colo(), colo(), small(4), small(1), wblk, small(2), wblk, small(2), small(2)],
        out_shape=[jax.ShapeDtypeStruct((S, D_MODEL), BF16), jax.ShapeDtypeStruct((S, D_MODEL), BF16),
                   jax.ShapeDtypeStruct((4, D_MODEL), F32), jax.ShapeDtypeStruct((1, D_MODEL), F32),
                   jax.ShapeDtypeStruct((2, LRU_BLOCKS, LANES, LANES), F32), jax.ShapeDtypeStruct((2, D_MODEL), F32),
                   jax.ShapeDtypeStruct((2, LRU_BLOCKS, LANES, LANES), F32), jax.ShapeDtypeStruct((2, D_MODEL), F32),
                   jax.ShapeDtypeStruct((2, D_MODEL), F32)],
        scratch_shapes=[pltpu.VMEM((S, LANES), F32), pltpu.VMEM((S, LANES), F32), pltpu.VMEM((S, LANES), F32)],
        name=name, compiler_params=_cp(("parallel",)))(p, p, hf, hr, dcat, conv_w, conv_b, wa, ba, wx, bx, lam)


def _mlp_fwd(x, w_up, w_down, gain, l):
    up, act, h = _mm_nn(x, w_up, norm_g=gain, relu2=True, name=f"mlp_up{l}")
    return _mm_nn(act, w_down, resid=x, name=f"mlp_down{l}"), (up, act, h)


PK_UP, PK_DOWN, PK_KV, PK_OUT, PK_IN = 0, 1024, 2048, 2304, 2688
PK_ROWS = {0: PK_IN, 1: PK_IN + 640}
SMALL_G_ROWS = 192
PKF_SMALL = 512
PKF_ROWS = PKF_SMALL + SMALL_G_ROWS


def _mlp_bwd(x, dx, dxb, saved, w_up, w_down, gain, l, rider=None, next_rider=None):
    up, act, h = saved
    pk = _mm_tn(act, dxb, 1, name=f"dw_down{l}", packed=(None, PK_ROWS[l], PK_DOWN), rider=rider)
    pk, carried = pk if rider is not None else (pk, None)
    dup = _mm_nt(dxb, w_down, up=up, name=f"d_up{l}")
    rider_up = next_rider(carried) if next_rider is not None else None
    pk = _mm_tn(h, dup, N_CHIPS, name=f"dw_up{l}", packed=(pk, PK_ROWS[l], PK_UP), rider=rider_up)
    pk, carried = pk if rider_up is not None else (pk, carried)
    dx, dxb, g_gain = _mm_nt(dup, w_up, norm_x=x, norm_g=gain, dres=dx, name=f"d_mlp_in{l}")
    return dx, dxb, pk, g_gain, carried


def _reduce_first(pk, place, tag, recv=None):
    if recv is None:
        recv = _sibling_exchange(pk, name=f"grad_sibling_exchange{tag}")
    return _sum_halves(pk, recv, place, name=f"sum_halves{tag}", tr=pk.shape[1] // 4)


def _sum_parts(parts, place, tag):
    return _sum_chips(parts, place, name=f"sum_chips{tag}", tr=parts.shape[1] // 2)


def _reduce_last(parts, place, tag):
    return _sibling_allgather(_sum_parts(parts, place, tag), name=f"grad_sibling_allgather{tag}")


def _local_step(x, mem, positions, target, W, pending=None, place=None):
    cos_t, sin_t = _rope_tables(positions)
    sinks = W["attn_sinks"].reshape(ATTN_HEADS)
    G = {}

    def hosting(late, fn, *args, **kw):
        if pending is None:
            return fn(*args, **kw)
        *res, buf = fn(*args, gather=pending[late], **kw)
        if late.startswith("w_down"):
            W.setdefault("w_down", [None] * DEPTH)[int(late[-1])] = _ready(late, buf)
        else:
            W[late] = _ready(late, buf)
        return res if len(res) > 1 else res[0]

    kv0, memn = _mm_nn(mem, W["w_mem_kv"][0], norm_g=W["mem_norm"], out_dtype=BF16, name="mem_kv0", tm=256)
    kv1 = _mm_nn(memn, W["w_mem_kv"][1], out_dtype=BF16, name="mem_kv1", tm=256)
    p0, h0 = hosting("w_out", _mm_nn, x, W["attn_w_in"], norm_g=W["mix_norm"][0], name="attn_in")
    q, kd, vd, va = hosting("lru_w_in", _qk_prep, p0, cos_t, sin_t)
    ao, lse = hosting("w_up", _attn_fwd, q, kd, va, sinks)
    cat0 = _memattn_fwd(p0, Q_W // MEM_W + 1, kv0, ao, name="memattn_fwd0")
    x1 = hosting("w_down0", _mm_nn, cat0, W["w_out"][0], resid=x, name="mix_out0")
    up0, act0, h1 = hosting("w_down1", _mm_nn, x1, W["w_up"][0], norm_g=W["mlp_norm"][0], relu2=True, name="mlp_up0")
    x2, mlp0 = _mm_nn(act0, W["w_down"][0], resid=x1, name="mlp_down0"), (up0, act0, h1)
    p1, h2 = _mm_nn(x2, W["lru_w_in"], norm_g=W["mix_norm"][1], name="lru_in")
    lru_w = (W["lru_conv_w"], W["lru_conv_b"], W["lru_wa"], W["lru_ba"], W["lru_wx"], W["lru_bx"], W["lru_lambda"])
    y, hf, hr = _lru_fwd(p1, *lru_w)
    cat1 = _memattn_fwd(p1, 2 * D_MODEL // MEM_W, kv1, y, name="memattn_fwd1")
    x3 = _mm_nn(cat1, W["w_out"][1], resid=x2, name="mix_out1")
    x4, mlp1 = _mlp_fwd(x3, W["w_up"][1], W["w_down"][1], W["mlp_norm"][1], 1)
    loss, dx, dxb, G["final_norm"] = _final(x4, W["final_norm"], target)

    def put(pk, off, g):
        return pk.at[:, off:off + g.size // (N_CHIPS * ROW)].set(g.reshape(N_CHIPS, -1, ROW))

    dx, dxb, pk1, gm1, _ = _mlp_bwd(x3, dx, dxb, mlp1, W["w_up"][1], W["w_down"][1], W["mlp_norm"][1], 1)
    pk1 = _mm_tn(cat1, dxb, 1, name="dw_out1", tk=384, packed=(pk1, PK_ROWS[1], PK_OUT))
    dcat1 = _mm_nt(dxb, W["w_out"][1], name="d_mix1")
    dmq1, dkv1 = _memattn_bwd(p1, 2 * D_MODEL // MEM_W, kv1, dcat1, name="memattn_bwd1")
    dkv1b = dkv1.astype(BF16)
    pk1 = _mm_tn(memn, dkv1b, 1, name="dw_kv1", tm=256, tk=256, packed=(pk1, PK_ROWS[1], PK_KV))
    (dxb1, dgate, G["lru_conv_w"], G["lru_conv_b"], G["lru_wa"], G["lru_ba"], G["lru_wx"], G["lru_bx"],
     G["lru_lambda"]) = _lru_bwd(p1, hf, hr, dcat1, *lru_w)
    dp1 = jnp.concatenate([dxb1, dgate, dmq1], axis=1)
    pk1 = put(pk1, PK_IN, _mm_tn(h2, dp1, N_CHIPS, name="dw_lru_in"))
    dx, dxb, gx1 = _mm_nt(dp1, W["lru_w_in"], norm_x=x2, norm_g=W["mix_norm"][1], dres=dx, name="d_lru_in")
    dist = place is not None
    h1_rows = PK_ROWS[1] // 4
    kept = {}

    def first_half(recv1):
        kept["halves1"], landing1 = _reduce_first(pk1, place, "1", recv1)
        return _exchange_rider((kept["halves1"], landing1, 0, h1_rows))

    dx, dxb, pk0, gm0, landing1 = _mlp_bwd(x1, dx, dxb, mlp0, W["w_up"][0], W["w_down"][0], W["mlp_norm"][0], 0,
                                           rider=_sib_exchange_rider(pk1) if dist else None,
                                           next_rider=first_half if dist else None)
    pk0 = _mm_tn(cat0, dxb, 1, name="dw_out0", tk=384, packed=(pk0, PK_ROWS[0], PK_OUT))
    dcat0 = _mm_nt(dxb, W["w_out"][0], name="d_mix0")
    dmq0, dkv0 = _memattn_bwd(p0, Q_W // MEM_W + 1, kv0, dcat0, name="memattn_bwd0")
    dkv0b = dkv0.astype(BF16)
    pk0 = _mm_tn(memn, dkv0b, 1, name="dw_kv0", tm=256, tk=256, packed=(pk0, PK_ROWS[0], PK_KV))
    rider = None
    if dist:
        halves0, landing0 = _reduce_first(pk0, place, "0")
        rider = _exchange_rider((kept["halves1"], landing1, h1_rows, h1_rows), (halves0, landing0, 0, halves0.shape[1]))
    dq, dk, dv, dsink, *parts = _attn_bwd(q, kd, vd, cat0, lse, sinks, dcat0, rider=rider)
    dp0 = _qk_prep_bwd(dq, dk, dv, dmq0, cos_t, sin_t)
    g_in = _mm_tn(h0, dp0, N_CHIPS, name="dw_attn_in",
                  rider=_sib_allgather_rider(_sum_parts(parts[0], place, "1"), _sum_parts(parts[1], place, "0")) if dist else None)
    if dist:
        g_in, pk1, pk0 = g_in
    dx, _, gx0 = _mm_nt(dp0, W["attn_w_in"], norm_x=x, norm_g=W["mix_norm"][0], dres=dx, name="d_attn_in")

    w_kv_both = jnp.concatenate([W["w_mem_kv"][0], W["w_mem_kv"][1]], axis=0)
    _, _, G["mem_norm"] = _mm_nt(jnp.concatenate([dkv0b, dkv1b], axis=1), w_kv_both, norm_x=mem, norm_g=W["mem_norm"],
                                 name="d_mem", tm=256)

    G["mix_norm"] = jnp.concatenate([gx0, gx1], axis=0)
    G["mlp_norm"] = jnp.concatenate([gm0, gm1], axis=0)
    G["attn_sinks"] = dsink[0:1, 0:ATTN_HEADS]
    small = _flat_pad(_small_grad_list(G), N_CHIPS * SMALL_G_ROWS * ROW).reshape(N_CHIPS, SMALL_G_ROWS, ROW)
    pkf = jnp.concatenate([g_in.reshape(N_CHIPS, PKF_SMALL, ROW), small], axis=1)
    return loss[0, 0], dx, G, pkf, pk0, pk1


def _comm_call(body, out_shape, n_sems, name, *args, alias=None):
    return pl.pallas_call(
        body, out_shape=out_shape, in_specs=[HBM] * len(args), out_specs=HBM,
        scratch_shapes=[pltpu.SemaphoreType.DMA((n_sems,)), pltpu.SemaphoreType.DMA((n_sems,))],
        input_output_aliases=alias or {}, name=name)(*args)


def _place_slot(shard, slot, n_slots, *, name, tr):
    R, C = shard.shape

    def body(s_ref, a_ref, o_ref):
        o_ref[0] = a_ref[...]

    return pl.pallas_call(
        body,
        grid_spec=pltpu.PrefetchScalarGridSpec(
            num_scalar_prefetch=1, grid=(R // tr,), in_specs=[pl.BlockSpec((tr, C), lambda i, s_ref: (i, 0))],
            out_specs=pl.BlockSpec((1, tr, C), lambda i, s_ref: (s_ref[0], i, 0))),
        out_shape=jax.ShapeDtypeStruct((n_slots, R, C), shard.dtype), name=name,
        compiler_params=_cp(("parallel",)))(slot, shard)


def _allgather_chips(buf, *, name, forward_to_sibling):
    def body(b_ref, o_ref, send_sems, recv_sems):
        if forward_to_sibling:
            _gather_start(o_ref, send_sems, recv_sems)
            _gather_finish(o_ref, send_sems, recv_sems)
            return
        x, y, c, chips = _place()
        own = o_ref.at[2 * x + y]
        sends = [_remote(own, own, send_sems, recv_sems, j, (cx, cy, c)) for j, (cx, cy) in enumerate(chips)]
        for cp in sends:
            cp.start()
        for j, (cx, cy) in enumerate(chips):
            landed = o_ref.at[2 * cx + cy]
            _remote(landed, landed, send_sems, recv_sems, j, (cx, cy, c)).wait_recv()
        for cp in sends:
            cp.wait_send()

    return _comm_call(body, jax.ShapeDtypeStruct(buf.shape, buf.dtype), GATHER_SEMS, name, buf, alias={0: 0})


def _sibling_exchange(g, *, name):
    _, R, C = g.shape
    half = R // 2

    def body(g_ref, o_ref, send_sems, recv_sems):
        _sib_exchange_start(g_ref, o_ref, send_sems, recv_sems)
        _sib_exchange_finish(g_ref, o_ref, send_sems, recv_sems)

    return _comm_call(body, jax.ShapeDtypeStruct((N_CHIPS, half, C), g.dtype), N_CHIPS, name, g)


def _chip_exchange(h, parts, *, name):
    def body(h_ref, p_ref, o_ref, send_sems, recv_sems):
        x, y, c, chips = _place()
        me = 2 * x + y
        cps = [_remote(h_ref.at[2 * cx + cy], o_ref.at[me], send_sems, recv_sems, j, (cx, cy, c))
               for j, (cx, cy) in enumerate(chips)]
        for cp in cps:
            cp.start()
        for j, (cx, cy) in enumerate(chips):
            got = o_ref.at[2 * cx + cy]
            _remote(got, got, send_sems, recv_sems, j, (cx, cy, c)).wait_recv()
        for cp in cps:
            cp.wait_send()

    return _comm_call(body, jax.ShapeDtypeStruct(parts.shape, parts.dtype), 3, name, h, parts, alias={1: 0})


def _sibling_allgather(full, *, name):
    def body(f_ref, o_ref, send_sems, recv_sems):
        _sib_allgather_start(o_ref, send_sems, recv_sems)
        _sib_allgather_finish(o_ref, send_sems, recv_sems)

    return _comm_call(body, jax.ShapeDtypeStruct(full.shape, full.dtype), 1, name, full, alias={0: 0})


def _sum_halves(g, recv, place, *, name="sum_halves", tr=480):
    _, R, C = g.shape
    half = R // 2
    nblk = half // tr

    def body(pl_ref, g_ref, r_ref, o_ref, own_ref):
        v = (g_ref[...] + r_ref[...]).astype(BF16)
        o_ref[...] = v

        @pl.when(pl.program_id(1) == pl_ref[1])
        def _():
            own_ref[...] = v

    blk = pl.BlockSpec((1, tr, C), lambda i, s, p: (s, i, 0))
    return pl.pallas_call(
        body,
        grid_spec=pltpu.PrefetchScalarGridSpec(
            num_scalar_prefetch=1, grid=(nblk, N_CHIPS),
            in_specs=[pl.BlockSpec((1, tr, C), lambda i, s, p: (s, p[0] * nblk + i, 0)), blk],
            out_specs=[blk, pl.BlockSpec((1, tr, C), lambda i, s, p: (p[1], i, 0))]),
        out_shape=[jax.ShapeDtypeStruct((N_CHIPS, half, C), BF16)] * 2, name=name,
        compiler_params=_cp(("parallel", "arbitrary")))(place, g, recv)


def _sum_chips(parts, place, *, name="sum_chips", tr=480):
    _, R, C = parts.shape
    nblk = R // tr

    def body(pl_ref, p_ref, o_ref):
        acc = p_ref[0].astype(F32) + p_ref[1].astype(F32)
        o_ref[...] = (acc + p_ref[2].astype(F32)) + p_ref[3].astype(F32)

    return pl.pallas_call(
        body,
        grid_spec=pltpu.PrefetchScalarGridSpec(
            num_scalar_prefetch=1, grid=(nblk,), in_specs=[pl.BlockSpec((N_CHIPS, tr, C), lambda i, p: (0, i, 0))],
            out_specs=pl.BlockSpec((tr, C), lambda i, p: (p[0] * nblk + i, 0))),
        out_shape=jax.ShapeDtypeStruct((2 * R, C), F32), name=name, compiler_params=_cp(("parallel",)))(place, parts)


def _adamw(w, g, m, v, *, name, tr=128):
    R, C = w.shape
    bc1 = 1.0 - ADAM_B1 ** ADAM_STEP
    bc2 = 1.0 - ADAM_B2 ** ADAM_STEP

    def body(w_ref, g_ref, m_ref, v_ref, d_ref, nm_ref, nv_ref):
        gv = g_ref[...]
        nm = ADAM_B1 * m_ref[...] + (1.0 - ADAM_B1) * gv
        nv = ADAM_B2 * v_ref[...] + (1.0 - ADAM_B2) * (gv * gv)
        d_ref[...] = -ADAM_LR * ((nm / bc1) / (_sqrt(nv / bc2) + ADAM_EPS) + ADAM_WD * w_ref[...])
        nm_ref[...] = nm
        nv_ref[...] = nv

    blk = pl.BlockSpec((tr, C), lambda i: (i, 0))
    return pl.pallas_call(
        body, grid=(R // tr,), in_specs=[blk] * 4, out_specs=[blk] * 3,
        out_shape=[jax.ShapeDtypeStruct((R, C), F32)] * 3, name=name, compiler_params=_cp(("parallel",)))(w, g, m, v)


ROW = 1024
BIG = ("w_mem_kv", "w_out", "w_up", "w_down", "attn_w_in", "lru_w_in")
SMALL_SHARDED = ("lru_conv_w", "lru_conv_b", "lru_ba", "lru_bx", "lru_lambda")
REPLICATED = ("mix_norm", "mlp_norm", "mem_norm", "final_norm", "attn_sinks", "lru_wa", "lru_wx")
SMALL = REPLICATED + SMALL_SHARDED
WEIGHTS = ("mix_norm", "mlp_norm", "mem_norm", "final_norm", "w_mem_kv", "w_out", "w_up", "w_down", "attn_w_in",
           "attn_sinks", "lru_w_in", "lru_conv_w", "lru_conv_b", "lru_wa", "lru_ba", "lru_wx", "lru_bx", "lru_lambda")
SMALL_W_ROWS = 32
ADAM_SMALL_ROWS = 640


def _rows(a):
    return a.reshape(-1, ROW)


def _flat_pad(parts, total):
    flat = jnp.concatenate([p.reshape(-1) for p in parts])
    return jnp.pad(flat, (0, total - flat.shape[0]))


def _pad_rows(a):
    flat = a.reshape(-1)
    n = -(-flat.shape[0] // ROW) * ROW
    return jnp.pad(flat, (0, n - flat.shape[0])).reshape(-1, ROW)


LATE = ("w_out", "lru_w_in", "w_up", "w_down0", "w_down1")


def _ready(name, full):
    if name == "w_out":
        wo = full.reshape(N_CHIPS, DEPTH, -1, D_MODEL)
        return [wo[:, l].reshape(1, MIX_OUT_W, D_MODEL) for l in range(DEPTH)]
    if name == "w_up":
        wu = full.reshape(N_CHIPS, DEPTH, D_MODEL, D_FF // N_CHIPS)
        return [wu[:, l] for l in range(DEPTH)]
    if name == "lru_w_in":
        return full.reshape(N_CHIPS, D_MODEL, LRU_IN_W // N_CHIPS)
    return full.reshape(1, D_FF, D_MODEL)


def _gather_weights(P, chip1):
    bf = lambda a: _rows(a.astype(BF16))
    small = _flat_pad([P[n] for n in SMALL_SHARDED], SMALL_W_ROWS * ROW // 2)
    small_bits = lax.bitcast_convert_type(small, BF16).reshape(SMALL_W_ROWS, ROW)
    early = jnp.concatenate([bf(P["attn_w_in"]), bf(P["w_mem_kv"]), small_bits], axis=0)
    n_in, n_kv = P["attn_w_in"].size // ROW, P["w_mem_kv"].size // ROW
    placed = _place_slot(early, chip1, N_CHIPS, name="place_weights", tr=early.shape[0] // 2)
    full = _allgather_chips(placed, name="allgather_weights", forward_to_sibling=True)
    late = {"w_out": bf(P["w_out"]), "lru_w_in": bf(P["lru_w_in"]), "w_up": bf(P["w_up"]),
            "w_down0": bf(P["w_down"][0]), "w_down1": bf(P["w_down"][1])}
    pending = {n: _place_slot(late[n], chip1, N_CHIPS, name=f"place_{n}", tr=late[n].shape[0] // 2) for n in LATE}
    W = {n: P[n] for n in REPLICATED}
    W["attn_w_in"] = full[:, :n_in].reshape(N_CHIPS, D_MODEL, ATTN_IN_W // N_CHIPS)
    kv = full[:, n_in:n_in + n_kv].reshape(N_CHIPS, DEPTH, -1, D_MODEL)
    W["w_mem_kv"] = [kv[:, l].reshape(1, D_MODEL, D_MODEL) for l in range(DEPTH)]
    sm = lax.bitcast_convert_type(full[:, n_in + n_kv:].reshape(N_CHIPS, -1, 2), F32)
    o = 0
    for n in SMALL_SHARDED:
        shp = P[n].shape[1:]
        cnt = math.prod(shp)
        piece = sm[:, o:o + cnt].reshape((N_CHIPS,) + shp)
        piece = jnp.moveaxis(piece, 0, -2)
        W[n] = piece.reshape(shp[:-1] + (N_CHIPS * shp[-1],)).reshape(-1, D_MODEL)
        o += cnt
    W["lru_wa"] = P["lru_wa"][0].astype(BF16)
    W["lru_wx"] = P["lru_wx"][0].astype(BF16)
    return W, pending


def _small_grad_list(G):
    return [G["mix_norm"], G["mlp_norm"], G["mem_norm"], G["final_norm"], jnp.pad(G["attn_sinks"].reshape(-1), (0, ROW - ATTN_HEADS)),
            G["lru_wa"], G["lru_wx"], G["lru_conv_w"], G["lru_conv_b"], G["lru_ba"], G["lru_bx"], G["lru_lambda"]]


SMALL_G_SIZES = (2 * D_MODEL, 2 * D_MODEL, D_MODEL, D_MODEL, ROW, 2 * 8 * 128 * 128, 2 * 8 * 128 * 128,
                 4 * D_MODEL, D_MODEL, 2 * D_MODEL, 2 * D_MODEL, 2 * D_MODEL)


def _finish_grads(pkf, full0, full1, place, chip1):
    partsf = _chip_exchange(*_reduce_first(pkf, place, "f"), name="grad_chip_exchange_last")
    fullf = _reduce_last(partsf, place, "f")
    small_placed = _place_slot(fullf[PKF_SMALL:], chip1, N_CHIPS, name="place_small_grads", tr=SMALL_G_ROWS)
    small_all = _allgather_chips(small_placed, name="allgather_small_grads", forward_to_sibling=False)
    flat = small_all.reshape(-1)
    small = {}
    o = 0
    names = ("mix_norm", "mlp_norm", "mem_norm", "final_norm", "attn_sinks", "lru_wa", "lru_wx",
             "lru_conv_w", "lru_conv_b", "lru_ba", "lru_bx", "lru_lambda")
    for n, cnt in zip(names, SMALL_G_SIZES):
        small[n] = flat[o:o + cnt]
        o += cnt
    both = lambda off, r: jnp.concatenate([full0[off:off + r], full1[off:off + r]], axis=0)
    big = {"w_up": both(PK_UP, 1024), "w_down": both(PK_DOWN, 1024), "w_out": both(PK_OUT, 384), "w_mem_kv": both(PK_KV, 256),
           "attn_w_in": fullf[:PKF_SMALL], "lru_w_in": full1[PK_IN:PK_IN + 640]}
    return big, small


def kernel(x, mem, positions, mix_norm, mlp_norm, mem_norm, final_norm, w_mem_kv, w_out, w_up, w_down, attn_w_in, attn_sinks, lru_w_in, lru_conv_w, lru_conv_b, lru_wa, lru_ba, lru_wx, lru_bx, lru_lambda, loss_target, m_mix_norm, m_mlp_norm, m_mem_norm, m_final_norm, m_w_mem_kv, m_w_out, m_w_up, m_w_down, m_attn_w_in, m_attn_sinks, m_lru_w_in, m_lru_conv_w, m_lru_conv_b, m_lru_wa, m_lru_ba, m_lru_wx, m_lru_bx, m_lru_lambda, v_mix_norm, v_mlp_norm, v_mem_norm, v_final_norm, v_w_mem_kv, v_w_out, v_w_up, v_w_down, v_attn_w_in, v_attn_sinks, v_lru_w_in, v_lru_conv_w, v_lru_conv_b, v_lru_wa, v_lru_ba, v_lru_wx, v_lru_bx, v_lru_lambda):
    P = dict(mix_norm=mix_norm, mlp_norm=mlp_norm, mem_norm=mem_norm, final_norm=final_norm, w_mem_kv=w_mem_kv, w_out=w_out,
             w_up=w_up, w_down=w_down, attn_w_in=attn_w_in, attn_sinks=attn_sinks, lru_w_in=lru_w_in, lru_conv_w=lru_conv_w,
             lru_conv_b=lru_conv_b, lru_wa=lru_wa, lru_ba=lru_ba, lru_wx=lru_wx, lru_bx=lru_bx, lru_lambda=lru_lambda)
    M1 = dict(mix_norm=m_mix_norm, mlp_norm=m_mlp_norm, mem_norm=m_mem_norm, final_norm=m_final_norm, w_mem_kv=m_w_mem_kv,
              w_out=m_w_out, w_up=m_w_up, w_down=m_w_down, attn_w_in=m_attn_w_in, attn_sinks=m_attn_sinks, lru_w_in=m_lru_w_in,
              lru_conv_w=m_lru_conv_w, lru_conv_b=m_lru_conv_b, lru_wa=m_lru_wa, lru_ba=m_lru_ba, lru_wx=m_lru_wx,
              lru_bx=m_lru_bx, lru_lambda=m_lru_lambda)
    V2 = dict(mix_norm=v_mix_norm, mlp_norm=v_mlp_norm, mem_norm=v_mem_norm, final_norm=v_final_norm, w_mem_kv=v_w_mem_kv,
              w_out=v_w_out, w_up=v_w_up, w_down=v_w_down, attn_w_in=v_attn_w_in, attn_sinks=v_attn_sinks, lru_w_in=v_lru_w_in,
              lru_conv_w=v_lru_conv_w, lru_conv_b=v_lru_conv_b, lru_wa=v_lru_wa, lru_ba=v_lru_ba, lru_wx=v_lru_wx,
              lru_bx=v_lru_bx, lru_lambda=v_lru_lambda)
    chip = 2 * lax.axis_index("x") + lax.axis_index("y")
    chip1 = chip.astype(jnp.int32).reshape(1)
    place = jnp.stack([lax.axis_index("c").astype(jnp.int32), chip.astype(jnp.int32)])

    W, pending = _gather_weights(P, chip1)
    loss, dx, _, pkf, full0, full1 = _local_step(x[0], mem[0], positions[0], loss_target[0], W, pending, place)
    loss = lax.psum(loss, ("x", "y", "c"))
    big, small = _finish_grads(pkf, full0, full1, place, chip1)

    grads, deltas, new_m, new_v = {}, {}, {}, {}
    for n in BIG:
        g = big[n]
        d, nm, nv = _adamw(_rows(P[n]), g, _rows(M1[n]), _rows(V2[n]), name=f"adamw_{n}")
        grads[n], deltas[n], new_m[n], new_v[n] = (t.reshape(P[n].shape) for t in (g, d, nm, nv))

    for n in SMALL:
        g = small[n]
        if n in SMALL_SHARDED:
            shard = P[n].shape[-1]
            g = lax.dynamic_slice_in_dim(g.reshape(-1, N_CHIPS * shard), chip * shard, shard, axis=1)
        elif n == "attn_sinks":
            g = g[:ATTN_HEADS]
        grads[n] = g.reshape(P[n].shape)
    packs = []
    for src in (P, grads, M1, V2):
        a = jnp.concatenate([_pad_rows(src[n]) for n in SMALL], axis=0)
        packs.append(jnp.pad(a, ((0, ADAM_SMALL_ROWS - a.shape[0]), (0, 0))))
    d_s, nm_s, nv_s = _adamw(*packs, name="adamw_small")
    o = 0
    for n in SMALL:
        cnt = math.prod(P[n].shape)
        r = -(-cnt // ROW)
        for dst, src in ((deltas, d_s), (new_m, nm_s), (new_v, nv_s)):
            dst[n] = src[o:o + r].reshape(-1)[:cnt].reshape(P[n].shape)
        o += r

    return (loss, dx[None], *[grads[n] for n in WEIGHTS], *[deltas[n] for n in WEIGHTS],
            *[new_m[n] for n in WEIGHTS], *[new_v[n] for n in WEIGHTS])
```

```python
import functools
import math

import jax
import jax.numpy as jnp
from jax import lax
from jax.experimental import pallas as pl
from jax.experimental.pallas import tpu as pltpu

F32 = jnp.float32
BF16 = jnp.bfloat16
MESH = pl.DeviceIdType.MESH

D_MODEL = 1024
DEPTH = 2
EPS = 1e-6
ATTN_HEADS = 16
ATTN_KV_HEADS = 4
HEAD_DIM = 64
WINDOW = 128
BLOCK = 128
ROPE_THETA = 500000.0
ROPE_DIM = 16
Q_W = 1024
KV_W = 256
MEM_LEN = 256
MEM_HEADS = 4
MEM_HEAD_DIM = 128
MEM_W = 512
LRU_BLOCKS = 8
LRU_C = 8.0
ATTN_IN_W = 2048
LRU_IN_W = 2560
MIX_OUT_W = 1536
D_FF = 4096
NEG = -1e30
N_CHIPS = 4

ADAM_LR = 0.001
ADAM_B1 = 0.9
ADAM_B2 = 0.999
ADAM_EPS = 1e-08
ADAM_WD = 0.01
ADAM_STEP = 10

LANES = 128
SCAN_ROWS = 512
VMEM_LIMIT = 56 * 1024 * 1024

NT = (((1,), (1,)), ((), ()))
TN = (((0,), (0,)), ((), ()))


def _cp(sem=None):
    return pltpu.CompilerParams(dimension_semantics=sem, vmem_limit_bytes=VMEM_LIMIT)


HBM = pl.BlockSpec(memory_space=pl.ANY)
GATHER_SEMS = 6


def _place():
    x, y, c = lax.axis_index("x"), lax.axis_index("y"), lax.axis_index("c")
    chips = [(1 - x, y), (x, 1 - y), (1 - x, 1 - y)]
    return x, y, c, chips


def _remote(src, dst, send_sems, recv_sems, k, to):
    return pltpu.make_async_remote_copy(src_ref=src, dst_ref=dst, send_sem=send_sems.at[k], recv_sem=recv_sems.at[k],
                                        device_id=to, device_id_type=MESH)


def _gather_start(o_ref, send_sems, recv_sems):
    x, y, c, chips = _place()
    half = o_ref.shape[1] // 2
    own = o_ref.at[2 * x + y, pl.ds(pl.multiple_of(c * half, 16), half)]
    for j, (cx, cy) in enumerate(chips):
        _remote(own, own, send_sems, recv_sems, j, (cx, cy, c)).start()


def _gather_forward(o_ref, send_sems, recv_sems):
    x, y, c, chips = _place()
    half = o_ref.shape[1] // 2
    my_rows = pl.ds(pl.multiple_of(c * half, 16), half)
    for j, (cx, cy) in enumerate(chips):
        landed = o_ref.at[2 * cx + cy, my_rows]
        _remote(landed, landed, send_sems, recv_sems, j, (cx, cy, c)).wait_recv()
        _remote(landed, landed, send_sems, recv_sems, 3 + j, (x, y, 1 - c)).start()


def _gather_drain(o_ref, send_sems, recv_sems):
    x, y, c, chips = _place()
    half = o_ref.shape[1] // 2
    my_rows = pl.ds(pl.multiple_of(c * half, 16), half)
    sib_rows = pl.ds(pl.multiple_of((1 - c) * half, 16), half)
    own = o_ref.at[2 * x + y, my_rows]
    for j, (cx, cy) in enumerate(chips):
        got = o_ref.at[2 * cx + cy, sib_rows]
        _remote(got, got, send_sems, recv_sems, 3 + j, (x, y, 1 - c)).wait_recv()
    for j, (cx, cy) in enumerate(chips):
        _remote(own, own, send_sems, recv_sems, j, (cx, cy, c)).wait_send()
        landed = o_ref.at[2 * cx + cy, my_rows]
        _remote(landed, landed, send_sems, recv_sems, 3 + j, (x, y, 1 - c)).wait_send()


def _gather_finish(o_ref, send_sems, recv_sems):
    _gather_forward(o_ref, send_sems, recv_sems)
    _gather_drain(o_ref, send_sems, recv_sems)


def _exchange_start(h_ref, o_ref, send_sems, recv_sems, rows=None, base=0):
    x, y, c, chips = _place()
    rows = pl.ds(0, h_ref.shape[1]) if rows is None else rows
    for j, (cx, cy) in enumerate(chips):
        _remote(h_ref.at[2 * cx + cy, rows], o_ref.at[2 * x + y, rows], send_sems, recv_sems, base + j, (cx, cy, c)).start()


def _exchange_finish(h_ref, o_ref, send_sems, recv_sems, rows=None, base=0):
    x, y, c, chips = _place()
    rows = pl.ds(0, h_ref.shape[1]) if rows is None else rows
    for j, (cx, cy) in enumerate(chips):
        got = o_ref.at[2 * cx + cy, rows]
        _remote(got, got, send_sems, recv_sems, base + j, (cx, cy, c)).wait_recv()
    for j, (cx, cy) in enumerate(chips):
        _remote(h_ref.at[2 * cx + cy, rows], o_ref.at[2 * x + y, rows], send_sems, recv_sems, base + j, (cx, cy, c)).wait_send()


def _sib_exchange_copies(g_ref, o_ref, send_sems, recv_sems):
    x, y, c, _ = _place()
    half = g_ref.shape[1] // 2
    other = pl.ds(pl.multiple_of((1 - c) * half, 8), half)
    return [_remote(g_ref.at[s, other], o_ref.at[s], send_sems, recv_sems, s, (x, y, 1 - c)) for s in range(N_CHIPS)]


def _sib_exchange_start(*refs):
    for cp in _sib_exchange_copies(*refs):
        cp.start()


def _sib_exchange_finish(*refs):
    for cp in _sib_exchange_copies(*refs):
        cp.wait()


def _sib_allgather_start(*refs):
    *o_refs, send_sems, recv_sems = refs
    x, y, c, _ = _place()
    for i, o_ref in enumerate(o_refs):
        half = o_ref.shape[0] // 2
        mine = o_ref.at[pl.ds(pl.multiple_of(c * half, 8), half)]
        _remote(mine, mine, send_sems, recv_sems, i, (x, y, 1 - c)).start()


def _sib_allgather_finish(*refs):
    *o_refs, send_sems, recv_sems = refs
    x, y, c, _ = _place()
    for i, o_ref in enumerate(o_refs):
        half = o_ref.shape[0] // 2
        mine = o_ref.at[pl.ds(pl.multiple_of(c * half, 8), half)]
        got = o_ref.at[pl.ds(pl.multiple_of((1 - c) * half, 8), half)]
        _remote(got, got, send_sems, recv_sems, i, (x, y, 1 - c)).wait_recv()
        _remote(mine, mine, send_sems, recv_sems, i, (x, y, 1 - c)).wait_send()


class _Rider:
    def __init__(self, args, start, finish, inplace=1, mid=None):
        self.args, self.start, self.finish, self.inplace, self.mid = list(args), start, finish, inplace, mid


def _gather_rider(buf):
    return None if buf is None else _Rider([buf], _gather_start, _gather_finish, mid=(_gather_forward, _gather_drain))


def _exchange_rider(*parts):
    n = len(parts)
    assert 3 * n <= GATHER_SEMS

    def run(fn):
        def go(*refs):
            sems = refs[2 * n:]
            for i, (_, _, r0, nr) in enumerate(parts):
                fn(refs[i], refs[n + i], *sems, rows=pl.ds(r0, nr), base=3 * i)
        return go

    return _Rider([p[0] for p in parts] + [p[1] for p in parts], run(_exchange_start), run(_exchange_finish), inplace=n)


def _sib_exchange_rider(g):
    landing = lax.empty((N_CHIPS, g.shape[1] // 2, g.shape[2]), g.dtype)
    return _Rider([g, landing], _sib_exchange_start, _sib_exchange_finish)


def _sib_allgather_rider(*fulls):
    return _Rider(fulls, _sib_allgather_start, _sib_allgather_finish, inplace=len(fulls))


class _Hosted:
    def __init__(self, rider, n_in, n_out):
        self.rider = rider
        self.on = rider is not None
        self.args = rider.args if self.on else []
        k = len(self.args)
        p = self.p = rider.inplace if self.on else 0
        self.alias = {n_in + k - p + i: n_out + i for i in range(p)}
        self.in_specs = [HBM] * k
        self.out_specs = [HBM] * p
        self.out_shape = [jax.ShapeDtypeStruct(a.shape, a.dtype) for a in self.args[k - p:]]
        self.scratch = [pltpu.SemaphoreType.DMA((GATHER_SEMS,)), pltpu.SemaphoreType.DMA((GATHER_SEMS,))] if self.on else []

    def split(self, refs, n_in, n_out):
        refs = list(refs)
        if not self.on:
            return refs[:n_in], refs[n_in:n_in + n_out], refs[n_in + n_out:], None
        k, p = len(self.args), self.p
        ins, outs = refs[:n_in], refs[n_in + k:n_in + k + n_out]
        rest = refs[n_in + k + n_out + p:]
        rrefs = refs[n_in:n_in + k - p] + refs[n_in + k + n_out:n_in + k + n_out + p] + [rest[-2], rest[-1]]
        return ins, outs, rest[:-2], rrefs

    def run(self, rrefs, step, n_steps, compute):
        if rrefs is None:
            return compute()

        mid = self.rider.mid
        mid_step = (3 * n_steps) // 4
        two_stage = mid is not None and 0 < mid_step < n_steps - 1

        @pl.when(step == 0)
        def _():
            self.rider.start(*rrefs)

        compute()

        if two_stage:
            @pl.when(step == mid_step)
            def _():
                mid[0](*rrefs)

        @pl.when(step == n_steps - 1)
        def _():
            (mid[1] if two_stage else self.rider.finish)(*rrefs)


def _mm_nn(a, w3, *, name, out_dtype=F32, norm_g=None, resid=None, relu2=False, tm=512, gather=None):
    M, K = a.shape
    ns, _, n = w3.shape
    N = ns * n
    tm = min(tm, M)
    has_norm = norm_g is not None
    has_res = resid is not None
    n_in = 2 + has_norm + has_res
    n_out = (2 if relu2 else 1) + has_norm
    host = _Hosted(_gather_rider(gather), n_in, n_out)

    def body(*refs):
        ins, outs, _, gref = host.split(refs, n_in, n_out)
        a_ref, w_ref = ins[0], ins[1]
        g_ref = ins[2] if has_norm else None
        r_ref = ins[-1] if has_res else None

        def compute():
            if has_norm:
                xv = a_ref[...]
                rs = lax.rsqrt(jnp.mean(xv * xv, axis=-1, keepdims=True) + EPS)
                ab = (xv * rs * g_ref[...]).astype(BF16)
                outs[-1][...] = ab
            else:
                ab = a_ref[...]
            for s in range(ns):
                acc = jnp.dot(ab, w_ref[s], preferred_element_type=F32)
                sl = slice(s * n, (s + 1) * n)
                if relu2:
                    outs[0][:, sl] = acc.astype(BF16)
                    rl = jnp.maximum(acc, 0.0)
                    outs[1][:, sl] = (rl * rl).astype(BF16)
                elif has_res:
                    outs[0][:, sl] = r_ref[:, sl] + acc
                else:
                    outs[0][:, sl] = acc.astype(out_dtype)

        host.run(gref, pl.program_id(0), M // tm, compute)

    row = lambda w: pl.BlockSpec((tm, w), lambda i: (i, 0))
    in_specs = [row(K), pl.BlockSpec((ns, K, n), lambda i: (0, 0, 0))]
    args = [a, w3]
    if has_norm:
        in_specs.append(pl.BlockSpec((1, K), lambda i: (0, 0)))
        args.append(norm_g.reshape(1, K))
    if has_res:
        in_specs.append(row(N))
        args.append(resid)
    if relu2:
        out_shape = [jax.ShapeDtypeStruct((M, N), BF16), jax.ShapeDtypeStruct((M, N), BF16)]
        out_specs = [row(N), row(N)]
    else:
        out_shape = [jax.ShapeDtypeStruct((M, N), F32 if has_res else out_dtype)]
        out_specs = [row(N)]
    if has_norm:
        out_shape.append(jax.ShapeDtypeStruct((M, K), BF16))
        out_specs.append(row(K))
    res = pl.pallas_call(body, grid=(M // tm,), in_specs=in_specs + host.in_specs, out_specs=out_specs + host.out_specs,
                         out_shape=out_shape + host.out_shape, scratch_shapes=host.scratch, input_output_aliases=host.alias,
                         name=name, compiler_params=_cp(("arbitrary",) if host.on else ("parallel",)))(*args, *host.args)
    return res if len(res) > 1 else res[0]


def _mm_nt(g, w3, *, name, out_dtype=BF16, up=None, norm_x=None, norm_g=None, dres=None, tm=512):
    M = g.shape[0]
    ns, K, n = w3.shape
    tm = min(tm, M)
    has_up = up is not None
    has_norm = norm_x is not None
    has_res = dres is not None

    def body(*refs):
        refs = list(refs)
        g_ref, w_ref = refs[0], refs[1]
        pos = 2
        if has_up:
            up_ref = refs[pos]
            pos += 1
        if has_norm:
            x_ref, gn_ref = refs[pos], refs[pos + 1]
            pos += 2
        if has_res:
            r_ref = refs[pos]
            pos += 1
        outs = refs[pos:]
        acc = None
        for s in range(ns):
            part = lax.dot_general(g_ref[:, s * n:(s + 1) * n], w_ref[s], NT, preferred_element_type=F32)
            acc = part if acc is None else acc + part
        if has_up:
            outs[0][...] = (acc * (2.0 * jnp.maximum(up_ref[...].astype(F32), 0.0))).astype(BF16)
        elif has_norm:
            xv = x_ref[...]
            rs = lax.rsqrt(jnp.mean(xv * xv, axis=-1, keepdims=True) + EPS)
            xn = xv * rs
            dxn = acc * gn_ref[...]
            dx = rs * (dxn - xn * jnp.mean(dxn * xn, axis=-1, keepdims=True))
            if has_res:
                dx = dx + r_ref[...]
            outs[0][...] = dx
            outs[1][...] = dx.astype(BF16)

            @pl.when(pl.program_id(0) == 0)
            def _():
                outs[2][...] = jnp.zeros_like(outs[2])

            outs[2][...] += jnp.sum(acc * xn, axis=0, keepdims=True)
        else:
            outs[0][...] = acc.astype(out_dtype)

    row = lambda w: pl.BlockSpec((tm, w), lambda i: (i, 0))
    in_specs = [row(ns * n), pl.BlockSpec((ns, K, n), lambda i: (0, 0, 0))]
    args = [g, w3]
    if has_up:
        in_specs.append(row(K))
        args.append(up)
    if has_norm:
        in_specs += [row(K), pl.BlockSpec((1, K), lambda i: (0, 0))]
        args += [norm_x, norm_g.reshape(1, K)]
    if has_res:
        in_specs.append(row(K))
        args.append(dres)
    if has_norm:
        out_shape = [jax.ShapeDtypeStruct((M, K), F32), jax.ShapeDtypeStruct((M, K), BF16),
                     jax.ShapeDtypeStruct((1, K), F32)]
        out_specs = [row(K), row(K), pl.BlockSpec((1, K), lambda i: (0, 0))]
        sem = ("arbitrary",)
    else:
        out_shape = [jax.ShapeDtypeStruct((M, K), BF16 if has_up else out_dtype)]
        out_specs = [row(K)]
        sem = ("parallel",)
    res = pl.pallas_call(body, grid=(M // tm,), in_specs=in_specs, out_specs=out_specs, out_shape=out_shape,
                         name=name, compiler_params=_cp(sem))(*args)
    return res if len(res) > 1 else res[0]


def _mm_tn(a, g, ns, *, name, tk=512, tm=4096, packed=None, rider=None):
    M, K = a.shape
    n = g.shape[1] // ns
    tm = min(tm, M)
    tk = min(tk, K)
    nk, nm = K // tk, M // tm
    n_in = 3 if (packed is not None and packed[0] is not None) else 2
    host = _Hosted(rider, n_in, 1)

    def body(*refs):
        ins, outs, _, rrefs = host.split(refs, n_in, 1)
        a_ref, g_ref, o_ref = ins[0], ins[1], outs[0]

        def compute():
            @pl.when(pl.program_id(2) == 0)
            def _():
                o_ref[...] = jnp.zeros_like(o_ref)

            o_ref[0] += lax.dot_general(a_ref[...], g_ref[...], TN, preferred_element_type=F32)

        step = (pl.program_id(0) * nk + pl.program_id(1)) * nm + pl.program_id(2)
        host.run(rrefs, step, ns * nk * nm, compute)

    in_specs = [pl.BlockSpec((tm, tk), lambda s, k, m: (m, k)), pl.BlockSpec((tm, n), lambda s, k, m: (m, s))]
    args = [a, g]
    alias = {}
    if packed is None:
        out_spec = pl.BlockSpec((1, tk, n), lambda s, k, m: (s, k, 0))
        out_shape = jax.ShapeDtypeStruct((ns, K, n), F32)
    else:
        buf, rows, off = packed
        per_chip = K * ns // N_CHIPS
        assert n == ROW and per_chip % tk == 0 and off % tk == 0
        if ns == N_CHIPS:
            out_spec = pl.BlockSpec((1, tk, n), lambda s, k, m: (s, off // tk + k, 0))
        else:
            kpc = per_chip // tk
            out_spec = pl.BlockSpec((1, tk, n), lambda s, k, m: (k // kpc, off // tk + k % kpc, 0))
        out_shape = jax.ShapeDtypeStruct((N_CHIPS, rows, ROW), F32)
        if buf is not None:
            in_specs.append(HBM)
            args.append(buf)
            alias = {2: 0}
    sem = ("arbitrary",) * 3 if host.on else ("parallel", "parallel", "arbitrary")
    res = pl.pallas_call(
        body, grid=(ns, nk, nm), in_specs=in_specs + host.in_specs, out_specs=[out_spec] + host.out_specs,
        out_shape=[out_shape] + host.out_shape, scratch_shapes=host.scratch, name=name,
        input_output_aliases={**alias, **host.alias}, compiler_params=_cp(sem))(*args, *host.args)
    return res if host.on else res[0]


def _final(act, w_down, x, gain, target, *, name="mlp_down_final", tr=512):
    S, Dm = x.shape
    tr = min(tr, S)
    Kf = act.shape[1]

    def body(a_ref, w_ref, x_ref, g_ref, t_ref, loss_ref, dx_ref, dxb_ref, dg_ref):
        @pl.when(pl.program_id(0) == 0)
        def _():
            loss_ref[...] = jnp.zeros_like(loss_ref)
            dg_ref[...] = jnp.zeros_like(dg_ref)

        xv = x_ref[...] + jnp.dot(a_ref[...], w_ref[0], preferred_element_type=F32)
        gv = g_ref[...]
        rs = lax.rsqrt(jnp.mean(xv * xv, axis=-1, keepdims=True) + EPS)
        xn = xv * rs
        err = xn * gv - t_ref[...]
        loss_ref[...] += 0.5 * jnp.sum(jnp.mean(err * err, axis=-1, keepdims=True), axis=0, keepdims=True)
        dout = err * (1.0 / Dm)
        dg_ref[...] += jnp.sum(dout * xn, axis=0, keepdims=True)
        dxn = dout * gv
        dx = rs * (dxn - xn * jnp.mean(dxn * xn, axis=-1, keepdims=True))
        dx_ref[...] = dx
        dxb_ref[...] = dx.astype(BF16)

    row = pl.BlockSpec((tr, Dm), lambda i: (i, 0))
    return pl.pallas_call(
        body, grid=(S // tr,),
        in_specs=[pl.BlockSpec((tr, Kf), lambda i: (i, 0)), pl.BlockSpec((1, Kf, Dm), lambda i: (0, 0, 0)), row,
                  pl.BlockSpec((1, Dm), lambda i: (0, 0)), row],
        out_specs=[pl.BlockSpec((1, 1), lambda i: (0, 0)), row, row, pl.BlockSpec((1, Dm), lambda i: (0, 0))],
        out_shape=[jax.ShapeDtypeStruct((1, 1), F32), jax.ShapeDtypeStruct((S, Dm), F32),
                   jax.ShapeDtypeStruct((S, Dm), BF16), jax.ShapeDtypeStruct((1, Dm), F32)],
        name=name, compiler_params=_cp(("arbitrary",)))(act, w_down, x, gain.reshape(1, Dm), target)


def _rope_tables(positions):
    half = ROPE_DIM // 2
    inv_freq = ROPE_THETA ** (-2.0 * jnp.arange(half, dtype=F32) / ROPE_DIM)
    ang = positions.astype(F32)[:, None] * inv_freq
    cos, sin = jnp.cos(ang), jnp.sin(ang)
    S = positions.shape[0]
    ones = jnp.ones((S, HEAD_DIM - ROPE_DIM), F32)
    cos64 = jnp.concatenate([cos, cos, ones], axis=1)
    sin64 = jnp.concatenate([-sin, sin, 0.0 * ones], axis=1)
    return jnp.tile(cos64, (1, 2)), jnp.tile(sin64, (1, 2))


def _rope_partner(t):
    lane = lax.broadcasted_iota(jnp.int32, t.shape, 1)
    low = (lane & (HEAD_DIM - 1)) < (ROPE_DIM // 2)
    return jnp.where(low, pltpu.roll(t, LANES - ROPE_DIM // 2, 1), pltpu.roll(t, ROPE_DIM // 2, 1))


def _qk_prep(p, cos_t, sin_t, *, name="qk_prep", tr=256, gather=None):
    S = p.shape[0]
    tr = min(tr, S)
    scale = HEAD_DIM ** -0.5
    host = _Hosted(_gather_rider(gather), 3, 4)

    def body(*refs):
        ins, outs, _, gref = host.split(refs, 3, 4)
        host.run(gref, pl.program_id(0), S // tr, lambda: inner(*ins, *outs))

    def inner(p_ref, c_ref, s_ref, q_ref, k_ref, v_ref, va_ref):
        cs, sn = c_ref[...], s_ref[...]
        lane = lax.broadcasted_iota(jnp.int32, (tr, LANES), 1)
        lo = lane < HEAD_DIM
        for c in range(Q_W // LANES):
            t = p_ref[:, c * LANES:(c + 1) * LANES]
            q_ref[:, c * LANES:(c + 1) * LANES] = ((t * cs + _rope_partner(t) * sn) * scale).astype(BF16)
        for c in range(KV_W // LANES):
            t = p_ref[:, Q_W + c * LANES:Q_W + (c + 1) * LANES]
            kc = t * cs + _rope_partner(t) * sn
            vc = p_ref[:, Q_W + KV_W + c * LANES:Q_W + KV_W + (c + 1) * LANES]
            for arr, ref in ((kc, k_ref), (vc, v_ref)):
                sw = pltpu.roll(arr, HEAD_DIM, 1)
                ref[:, (2 * c) * LANES:(2 * c + 1) * LANES] = jnp.where(lo, arr, sw).astype(BF16)
                ref[:, (2 * c + 1) * LANES:(2 * c + 2) * LANES] = jnp.where(lo, sw, arr).astype(BF16)
            sw = pltpu.roll(vc, HEAD_DIM, 1)
            for k, aug in enumerate((jnp.where(lo, vc, 1.0), jnp.where(lo, 1.0, sw), jnp.where(lo, sw, 1.0), jnp.where(lo, 1.0, vc))):
                va_ref[:, (4 * c + k) * LANES:(4 * c + k + 1) * LANES] = aug.astype(BF16)

    row = lambda w: pl.BlockSpec((tr, w), lambda i: (i, 0))
    return pl.pallas_call(
        body, grid=(S // tr,), in_specs=[row(ATTN_IN_W), row(LANES), row(LANES)] + host.in_specs,
        out_specs=[row(Q_W), row(2 * KV_W), row(2 * KV_W), row(4 * KV_W)] + host.out_specs,
        out_shape=[jax.ShapeDtypeStruct((S, Q_W), BF16), jax.ShapeDtypeStruct((S, 2 * KV_W), BF16),
                   jax.ShapeDtypeStruct((S, 2 * KV_W), BF16), jax.ShapeDtypeStruct((S, 4 * KV_W), BF16)] + host.out_shape,
        scratch_shapes=host.scratch, input_output_aliases=host.alias,
        name=name, compiler_params=_cp(("arbitrary",) if host.on else ("parallel",)))(p, cos_t, sin_t, *host.args)


def _qk_prep_bwd(dq, dk, dv, dmq, cos_t, sin_t, *, name="qk_prep_bwd", tr=256):
    S = dq.shape[0]
    tr = min(tr, S)

    def body(dq_ref, dk_ref, dv_ref, dmq_ref, c_ref, s_ref, o_ref):
        cs, sn = c_ref[...], s_ref[...]
        for c in range(Q_W // LANES):
            t = dq_ref[:, c * LANES:(c + 1) * LANES]
            o_ref[:, c * LANES:(c + 1) * LANES] = (t * cs - _rope_partner(t) * sn).astype(BF16)
        for c in range(KV_W // LANES):
            t = dk_ref[:, c * LANES:(c + 1) * LANES]
            o_ref[:, Q_W + c * LANES:Q_W + (c + 1) * LANES] = (t * cs - _rope_partner(t) * sn).astype(BF16)
        o_ref[:, Q_W + KV_W:Q_W + 2 * KV_W] = dv_ref[...].astype(BF16)
        o_ref[:, Q_W + 2 * KV_W:] = dmq_ref[...]

    row = lambda w: pl.BlockSpec((tr, w), lambda i: (i, 0))
    return pl.pallas_call(
        body, grid=(S // tr,), in_specs=[row(Q_W), row(KV_W), row(KV_W), row(MEM_W), row(LANES), row(LANES)],
        out_specs=row(ATTN_IN_W), out_shape=jax.ShapeDtypeStruct((S, ATTN_IN_W), BF16),
        name=name, compiler_params=_cp(("parallel",)))(dq, dk, dv, dmq, cos_t, sin_t)


def _band(n, S):
    start = pl.multiple_of(jnp.clip((n - 1) * BLOCK, 0, S - 3 * BLOCK), BLOCK)
    qi = lax.broadcasted_iota(jnp.int32, (BLOCK, 3 * BLOCK), 0) + n * BLOCK
    ki = lax.broadcasted_iota(jnp.int32, (BLOCK, 3 * BLOCK), 1) + start
    return start, jnp.abs(ki - qi) <= WINDOW


def _head_operand(ref, h, lo):
    c = h // 2
    t = ref[:, c * LANES:(c + 1) * LANES].astype(F32)
    return jnp.where(lo if h % 2 == 0 else jnp.logical_not(lo), t, 0.0).astype(BF16)


GROUP = ATTN_HEADS // ATTN_KV_HEADS
EVENS_FIRST = (0, 2, 1, 3)


def _attn_fwd(q, kd, va, sinks, *, name="attn_fwd", gather=None):
    S = q.shape[0]
    host = _Hosted(_gather_rider(gather), 4, 2)

    def body(*refs):
        ins, outs, scr, gref = host.split(refs, 4, 2)
        host.run(gref, pl.program_id(0), S // BLOCK, lambda: inner(*ins, *outs, *scr))

    def inner(sink_ref, q_ref, k_ref, va_ref, o_ref, lse_ref, p_scr):
        n = pl.program_id(0)
        start, mask = _band(n, S)
        lane = lax.broadcasted_iota(jnp.int32, (BLOCK, LANES), 1)
        lo = lane < HEAD_DIM
        rows = pl.ds(start, 3 * BLOCK)
        scores = []
        for g in range(ATTN_KV_HEADS):
            qst = jnp.concatenate([_head_operand(q_ref, GROUP * g + j, lo) for j in EVENS_FIRST], axis=0)
            scores.append(lax.dot_general(qst, k_ref[rows, g * LANES:(g + 1) * LANES], NT, preferred_element_type=F32))
        ms = {}
        for g in range(ATTN_KV_HEADS):
            for pos, j in enumerate(EVENS_FIRST):
                h = GROUP * g + j
                s = jnp.where(mask, scores[g][pos * BLOCK:(pos + 1) * BLOCK], NEG)
                ms[h] = jnp.maximum(jnp.max(s, axis=-1, keepdims=True), sink_ref[h])
                p_scr[(GROUP * g + pos) * BLOCK:(GROUP * g + pos + 1) * BLOCK, :] = jnp.exp(s - ms[h]).astype(BF16)
        pvs = {}
        for g in range(ATTN_KV_HEADS):
            for par in range(2):
                r0 = (GROUP * g + 2 * par) * BLOCK
                pvs[g, par] = jnp.dot(p_scr[r0:r0 + 2 * BLOCK, :], va_ref[rows, (2 * g + par) * LANES:(2 * g + par + 1) * LANES],
                                      preferred_element_type=F32)
        lse_blk = jnp.zeros((BLOCK, LANES), F32)
        for g in range(ATTN_KV_HEADS):
            outs = {}
            for par in range(2):
                for k in range(2):
                    j = EVENS_FIRST[2 * par + k]
                    h = GROUP * g + j
                    pv = pvs[g, par][k * BLOCK:(k + 1) * BLOCK]
                    den = pltpu.roll(pv, HEAD_DIM, 1) + jnp.exp(sink_ref[h] - ms[h])
                    outs[j] = pv * (1.0 / den)
                    l = den[:, par * HEAD_DIM:par * HEAD_DIM + 1]
                    lse_blk = jnp.where(lane == h, ms[h] + jnp.log(l), lse_blk)
            for jj in range(2):
                o_ref[:, (2 * g + jj) * LANES:(2 * g + jj + 1) * LANES] = jnp.where(lo, outs[2 * jj], outs[2 * jj + 1]).astype(BF16)
        lse_ref[...] = lse_blk

    full = lambda w: pl.BlockSpec((S, w), lambda i: (0, 0))
    return pl.pallas_call(
        body, grid=(S // BLOCK,),
        in_specs=[pl.BlockSpec(memory_space=pltpu.SMEM), pl.BlockSpec((BLOCK, Q_W), lambda i: (i, 0)),
                  full(2 * KV_W), full(4 * KV_W)] + host.in_specs,
        out_specs=[pl.BlockSpec((BLOCK, Q_W), lambda i: (i, 0)), pl.BlockSpec((BLOCK, LANES), lambda i: (i, 0))] + host.out_specs,
        out_shape=[jax.ShapeDtypeStruct((S, MIX_OUT_W), BF16), jax.ShapeDtypeStruct((S, LANES), F32)] + host.out_shape,
        scratch_shapes=[pltpu.VMEM((ATTN_HEADS * BLOCK, 3 * BLOCK), BF16)] + host.scratch, input_output_aliases=host.alias,
        name=name, compiler_params=_cp(("arbitrary",) if host.on else ("parallel",)))(sinks, q, kd, va, *host.args)


def _attn_bwd(q, kd, vd, ao, lse, sinks, dcat, *, name="attn_bwd", rider=None):
    S = q.shape[0]
    scale = HEAD_DIM ** -0.5
    host = _Hosted(rider, 7, 4)

    def body(*refs):
        ins, outs, scr, rrefs = host.split(refs, 7, 4)
        host.run(rrefs, pl.program_id(0), S // BLOCK, lambda: inner(*ins, *outs, *scr))

    def inner(sink_ref, q_ref, k_ref, v_ref, ao_ref, lse_ref, do_ref, dq_ref, dk_ref, dv_ref, ds_ref, p_scr, dsb_scr):
        n = pl.program_id(0)

        @pl.when(n == 0)
        def _():
            dk_ref[...] = jnp.zeros_like(dk_ref)
            dv_ref[...] = jnp.zeros_like(dv_ref)
            ds_ref[...] = jnp.zeros_like(ds_ref)

        start, mask = _band(n, S)
        lane = lax.broadcasted_iota(jnp.int32, (BLOCK, LANES), 1)
        lo = lane < HEAD_DIM
        lane3 = lax.broadcasted_iota(jnp.int32, (3 * BLOCK, LANES), 1)
        row8 = lax.broadcasted_iota(jnp.int32, (8, LANES), 0)
        lane8 = lax.broadcasted_iota(jnp.int32, (8, LANES), 1)
        dsink = jnp.zeros((8, LANES), F32)
        lse_blk = lse_ref[...]
        rows = pl.ds(start, 3 * BLOCK)
        lses, deltas = {}, {}
        for c in range(Q_W // LANES):
            prod = do_ref[:, c * LANES:(c + 1) * LANES].astype(F32) * ao_ref[:, c * LANES:(c + 1) * LANES].astype(F32)
            for k in range(2):
                h = 2 * c + k
                deltas[h] = jnp.sum(jnp.where(lo if k == 0 else jnp.logical_not(lo), prod, 0.0), axis=1, keepdims=True)
                lses[h] = jnp.sum(jnp.where(lane == h, lse_blk, 0.0), axis=1, keepdims=True)
                val = -jnp.sum(jnp.exp(sink_ref[h] - lses[h]) * deltas[h], axis=0, keepdims=True)
                dsink = dsink + jnp.where((row8 == 0) & (lane8 == h), val, 0.0)
        stack = lambda ref, g: jnp.concatenate([_head_operand(ref, GROUP * g + j, lo) for j in range(GROUP)], axis=0)
        ss, dps = [], []
        for g in range(ATTN_KV_HEADS):
            ss.append(lax.dot_general(stack(q_ref, g), k_ref[rows, g * LANES:(g + 1) * LANES], NT, preferred_element_type=F32))
            dps.append(lax.dot_general(stack(do_ref, g), v_ref[rows, g * LANES:(g + 1) * LANES], NT, preferred_element_type=F32))
        for g in range(ATTN_KV_HEADS):
            for j in range(GROUP):
                h = GROUP * g + j
                r = slice(j * BLOCK, (j + 1) * BLOCK)
                hr = slice(h * BLOCK, (h + 1) * BLOCK)
                p = jnp.exp(jnp.where(mask, ss[g][r], NEG) - lses[h])
                p_scr[hr, :] = p.astype(BF16)
                dsb_scr[hr, :] = (p * (dps[g][r] - deltas[h])).astype(BF16)
        for g in range(ATTN_KV_HEADS):
            cols = slice((g // 2) * LANES, (g // 2 + 1) * LANES)
            gr = slice(GROUP * g * BLOCK, GROUP * (g + 1) * BLOCK)
            dsg = dsb_scr[gr, :]
            dqs = jnp.dot(dsg, k_ref[rows, g * LANES:(g + 1) * LANES], preferred_element_type=F32) * scale
            for jj in range(2):
                dq_ref[:, (2 * g + jj) * LANES:(2 * g + jj + 1) * LANES] = jnp.where(
                    lo, dqs[(2 * jj) * BLOCK:(2 * jj + 1) * BLOCK], dqs[(2 * jj + 1) * BLOCK:(2 * jj + 2) * BLOCK])
            half = (lane3 < HEAD_DIM) if g % 2 == 0 else (lane3 >= HEAD_DIM)
            dkr = lax.dot_general(dsg, stack(q_ref, g), TN, preferred_element_type=F32)
            dk_ref[rows, cols] += jnp.where(half, dkr + pltpu.roll(dkr, HEAD_DIM, 1), 0.0)
            dvr = lax.dot_general(p_scr[gr, :], stack(do_ref, g), TN, preferred_element_type=F32)
            dv_ref[rows, cols] += jnp.where(half, dvr + pltpu.roll(dvr, HEAD_DIM, 1), 0.0)
        ds_ref[...] += dsink

    full = lambda w: pl.BlockSpec((S, w), lambda i: (0, 0))
    blk = lambda w: pl.BlockSpec((BLOCK, w), lambda i: (i, 0))
    return pl.pallas_call(
        body, grid=(S // BLOCK,),
        in_specs=[pl.BlockSpec(memory_space=pltpu.SMEM), blk(Q_W), full(2 * KV_W), full(2 * KV_W), blk(Q_W), blk(LANES), blk(Q_W)]
        + host.in_specs,
        out_specs=[blk(Q_W), full(KV_W), full(KV_W), pl.BlockSpec((8, LANES), lambda i: (0, 0))] + host.out_specs,
        out_shape=[jax.ShapeDtypeStruct((S, Q_W), F32), jax.ShapeDtypeStruct((S, KV_W), F32),
                   jax.ShapeDtypeStruct((S, KV_W), F32), jax.ShapeDtypeStruct((8, LANES), F32)] + host.out_shape,
        scratch_shapes=[pltpu.VMEM((ATTN_HEADS * BLOCK, 3 * BLOCK), BF16), pltpu.VMEM((ATTN_HEADS * BLOCK, 3 * BLOCK), BF16)]
        + host.scratch, input_output_aliases=host.alias,
        name=name, compiler_params=_cp(("arbitrary",)))(sinks, q, kd, vd, ao, lse, dcat, *host.args)


def _mem_probs(q_ref, kv_ref, h):
    scale = MEM_HEAD_DIM ** -0.5
    qh = q_ref[:, h * LANES:(h + 1) * LANES].astype(BF16)
    s = lax.dot_general(qh, kv_ref[:, h * LANES:(h + 1) * LANES], NT, preferred_element_type=F32) * scale
    m = jnp.max(s, axis=-1, keepdims=True)
    pe = jnp.exp(s - m)
    return qh, pe * (1.0 / jnp.sum(pe, axis=-1, keepdims=True))


def _memattn_fwd(p, qblk, kv, cat, *, name="memattn_fwd", tr=512):
    S = p.shape[0]
    tr = min(tr, S)

    def body(q_ref, kv_ref, cat_ref, o_ref):
        for h in range(MEM_HEADS):
            _, pr = _mem_probs(q_ref, kv_ref, h)
            o = jnp.dot(pr.astype(BF16), kv_ref[:, MEM_W + h * LANES:MEM_W + (h + 1) * LANES], preferred_element_type=F32)
            o_ref[:, h * LANES:(h + 1) * LANES] = o.astype(BF16)

    return pl.pallas_call(
        body, grid=(S // tr,),
        in_specs=[pl.BlockSpec((tr, MEM_W), lambda i: (i, qblk)), pl.BlockSpec((MEM_LEN, 2 * MEM_W), lambda i: (0, 0)), HBM],
        out_specs=pl.BlockSpec((tr, MEM_W), lambda i: (i, Q_W // MEM_W)),
        out_shape=jax.ShapeDtypeStruct((S, MIX_OUT_W), BF16), input_output_aliases={2: 0},
        name=name, compiler_params=_cp(("parallel",)))(p, kv, cat)


def _memattn_bwd(p, qblk, kv, dcat, *, name="memattn_bwd", tr=512):
    S = p.shape[0]
    tr = min(tr, S)
    scale = MEM_HEAD_DIM ** -0.5

    def body(q_ref, kv_ref, do_ref, dq_ref, dkv_ref):
        @pl.when(pl.program_id(0) == 0)
        def _():
            dkv_ref[...] = jnp.zeros_like(dkv_ref)

        for h in range(MEM_HEADS):
            qh, pr = _mem_probs(q_ref, kv_ref, h)
            doh = do_ref[:, h * LANES:(h + 1) * LANES]
            dp = lax.dot_general(doh, kv_ref[:, MEM_W + h * LANES:MEM_W + (h + 1) * LANES], NT, preferred_element_type=F32)
            delta = jnp.sum(pr * dp, axis=-1, keepdims=True)
            dsb = (pr * (dp - delta) * scale).astype(BF16)
            dq = jnp.dot(dsb, kv_ref[:, h * LANES:(h + 1) * LANES], preferred_element_type=F32)
            dq_ref[:, h * LANES:(h + 1) * LANES] = dq.astype(BF16)
            dkv_ref[:, h * LANES:(h + 1) * LANES] += lax.dot_general(dsb, qh, TN, preferred_element_type=F32)
            dkv_ref[:, MEM_W + h * LANES:MEM_W + (h + 1) * LANES] += lax.dot_general(
                pr.astype(BF16), doh, TN, preferred_element_type=F32)

    return pl.pallas_call(
        body, grid=(S // tr,),
        in_specs=[pl.BlockSpec((tr, MEM_W), lambda i: (i, qblk)), pl.BlockSpec((MEM_LEN, 2 * MEM_W), lambda i: (0, 0)),
                  pl.BlockSpec((tr, MEM_W), lambda i: (i, Q_W // MEM_W))],
        out_specs=[pl.BlockSpec((tr, MEM_W), lambda i: (i, 0)), pl.BlockSpec((MEM_LEN, 2 * MEM_W), lambda i: (0, 0))],
        out_shape=[jax.ShapeDtypeStruct((S, MEM_W), BF16), jax.ShapeDtypeStruct((MEM_LEN, 2 * MEM_W), F32)],
        name=name, compiler_params=_cp(("arbitrary",)))(p, kv, dcat)


def _sqrt(v):
    return jnp.where(v > 0.0, v * lax.rsqrt(v), 0.0)


def _sigmoid(z):
    return 1.0 / (1.0 + jnp.exp(-z))


def _one_minus_exp(z, exp_z):
    poly = z * (1.0 + z * (0.5 + z * (1.0 / 6.0 + z * (1.0 / 24.0 + z * (1.0 / 120.0)))))
    return jnp.where(z > -0.1, -poly, 1.0 - exp_z)


def _softplus_neg(lam):
    z = -lam
    return jnp.maximum(z, 0.0) + jnp.log(1.0 + jnp.exp(-jnp.abs(z)))


_GELU_C = math.sqrt(2.0 / math.pi)


def _gelu(z):
    return 0.5 * z * (1.0 + jnp.tanh(_GELU_C * (z + 0.044715 * z * z * z)))


def _row_or_zero(ref, t, S):
    ok = jnp.logical_and(t >= 0, t < S)
    return jnp.where(ok, ref[pl.ds(jnp.clip(t, 0, S - 1), 1), :], 0.0)


def _shift_down(v, first):
    ri = lax.broadcasted_iota(jnp.int32, v.shape, 0)
    return jnp.where(ri == 0, first, pltpu.roll(v, 1, 0))


def _shift_up(v, last):
    T = v.shape[0]
    ri = lax.broadcasted_iota(jnp.int32, v.shape, 0)
    return jnp.where(ri == T - 1, last, pltpu.roll(v, T - 1, 0))


def _scan_chunk(a, u, reverse):
    T = a.shape[0]
    ri = lax.broadcasted_iota(jnp.int32, a.shape, 0)
    d = 1
    while d < T:
        if reverse:
            a_s, u_s, ok = pltpu.roll(a, T - d, 0), pltpu.roll(u, T - d, 0), ri < T - d
        else:
            a_s, u_s, ok = pltpu.roll(a, d, 0), pltpu.roll(u, d, 0), ri >= d
        u = jnp.where(ok, a * u_s + u, u)
        a = jnp.where(ok, a * a_s, a)
        d *= 2
    return a, u


def _conv_taps(xb_ref, t0, S):
    T = SCAN_ROWS
    x0 = xb_ref[pl.ds(t0, T), :]
    xm1 = _shift_down(x0, _row_or_zero(xb_ref, t0 - 1, S))
    nxt0 = _row_or_zero(xb_ref, t0 + T, S)
    xp1 = _shift_up(x0, nxt0)
    xp2 = _shift_up(xp1, _row_or_zero(xb_ref, t0 + T + 1, S))
    return xm1, x0, xp1, xp2


def _lru_gates(xc, w_a, b_a, w_x, b_x, sp):
    xcb = xc.astype(BF16)
    r = _sigmoid(jnp.dot(xcb, w_a, preferred_element_type=F32) + b_a)
    i = _sigmoid(jnp.dot(xcb, w_x, preferred_element_type=F32) + b_x)
    la = -LRU_C * r * sp
    a = jnp.exp(la)
    return r, i, a, _sqrt(_one_minus_exp(2.0 * la, a * a))


def _lru_specs(S):
    col = lambda off: pl.BlockSpec((S, LANES), lambda n: (0, n + off), pipeline_mode=pl.Buffered(1))
    small = lambda r: pl.BlockSpec((r, LANES), lambda n: (0, n))
    wblk = pl.BlockSpec((2, 1, LANES, LANES), lambda n: (0, n, 0, 0))
    return col, small, wblk


def _lru_fwd(p, conv_w, conv_b, wa, ba, wx, bx, lam, *, name="lru_fwd"):
    S = p.shape[0]
    T = SCAN_ROWS
    nc = S // T

    def body(xb_ref, gate_ref, cw_ref, cb_ref, wa_ref, ba_ref, wx_ref, bx_ref, lam_ref, y_ref, hf_ref, hr_ref, xc_v):
        sp = _softplus_neg(lam_ref[...])
        cw = cw_ref[...]

        def fwd_step(c, h_in):
            t0 = pl.multiple_of(c * T, T)
            xm1, x0, xp1, xp2 = _conv_taps(xb_ref, t0, S)
            xc = cb_ref[...] + xm1 * cw[0:1] + x0 * cw[1:2] + xp1 * cw[2:3] + xp2 * cw[3:4]
            xc_v[pl.ds(t0, T), :] = xc
            _, i, a, beta = _lru_gates(xc, wa_ref[0, 0], ba_ref[0:1], wx_ref[0, 0], bx_ref[0:1], sp[0:1])
            A, U = _scan_chunk(a, beta * (i * xc), False)
            hf_ref[pl.ds(t0, T), :] = A * h_in + U
            return hf_ref[pl.ds(t0 + T - 1, 1), :]

        lax.fori_loop(0, nc, fwd_step, jnp.zeros((1, LANES), F32))

        def rev_step(k, h_in):
            t0 = pl.multiple_of((nc - 1 - k) * T, T)
            xc = xc_v[pl.ds(t0, T), :]
            _, i, a, beta = _lru_gates(xc, wa_ref[1, 0], ba_ref[1:2], wx_ref[1, 0], bx_ref[1:2], sp[1:2])
            A, U = _scan_chunk(a, beta * (i * xc), True)
            h = A * h_in + U
            hr_ref[pl.ds(t0, T), :] = h
            y_ref[pl.ds(t0, T), :] = ((hf_ref[pl.ds(t0, T), :] + h) * _gelu(gate_ref[pl.ds(t0, T), :])).astype(BF16)
            return hr_ref[pl.ds(t0, 1), :]

        lax.fori_loop(0, nc, rev_step, jnp.zeros((1, LANES), F32))

    col, small, wblk = _lru_specs(S)
    colo = lambda: pl.BlockSpec((S, LANES), lambda n: (0, n))
    return pl.pallas_call(
        body, grid=(LRU_BLOCKS,),
        in_specs=[col(0), col(LRU_BLOCKS), small(4), small(1), wblk, small(2), wblk, small(2), small(2)],
        out_specs=[colo(), colo(), colo()],
        out_shape=[jax.ShapeDtypeStruct((S, MIX_OUT_W), BF16), jax.ShapeDtypeStruct((S, D_MODEL), F32),
                   jax.ShapeDtypeStruct((S, D_MODEL), F32)],
        scratch_shapes=[pltpu.VMEM((S, LANES), F32)],
        name=name, compiler_params=_cp(("parallel",)))(p, p, conv_w, conv_b, wa, ba, wx, bx, lam)


def _lru_bwd(p, hf, hr, dcat, conv_w, conv_b, wa, ba, wx, bx, lam, *, name="lru_bwd"):
    S = p.shape[0]
    T = SCAN_ROWS
    nc = S // T

    def body(xb_ref, gate_ref, hf_ref, hr_ref, dy_ref, cw_ref, cb_ref, wa_ref, ba_ref, wx_ref, bx_ref, lam_ref,
             dxb_ref, dgate_ref, dcw_ref, dcb_ref, dwa_ref, dba_ref, dwx_ref, dbx_ref, dlam_ref, xc_v, dxc_v, dh_v):
        lam_v = lam_ref[...]
        sp = _softplus_neg(lam_v)
        cw = cw_ref[...]
        for ref in (dcw_ref, dcb_ref, dwa_ref, dba_ref, dwx_ref, dbx_ref, dlam_ref):
            ref[...] = jnp.zeros_like(ref)

        def prep_step(c, carry):
            t0 = pl.multiple_of(c * T, T)
            rows = pl.ds(t0, T)
            xm1, x0, xp1, xp2 = _conv_taps(xb_ref, t0, S)
            xc_v[rows, :] = cb_ref[...] + xm1 * cw[0:1] + x0 * cw[1:2] + xp1 * cw[2:3] + xp2 * cw[3:4]
            z = gate_ref[rows, :]
            dy = dy_ref[rows, :].astype(F32)
            th = jnp.tanh(_GELU_C * (z + 0.044715 * z * z * z))
            dgelu = 0.5 * (1.0 + th) + 0.5 * z * (1.0 - th * th) * _GELU_C * (1.0 + 3.0 * 0.044715 * z * z)
            dgate_ref[rows, :] = (dy * (hf_ref[rows, :] + hr_ref[rows, :]) * dgelu).astype(BF16)
            dh_v[rows, :] = dy * (0.5 * z * (1.0 + th))
            return carry

        lax.fori_loop(0, nc, prep_step, 0)

        def direction(d):
            h_ref = hf_ref if d == 0 else hr_ref
            w_a, w_x = wa_ref[d, 0], wx_ref[d, 0]
            b_a, b_x, sp_d = ba_ref[d:d + 1], bx_ref[d:d + 1], sp[d:d + 1]

            def step(k, carry):
                g_in, a_in = carry
                c = (nc - 1 - k) if d == 0 else k
                t0 = pl.multiple_of(c * T, T)
                rows = pl.ds(t0, T)
                xc = xc_v[rows, :]
                r, i, a, beta = _lru_gates(xc, w_a, b_a, w_x, b_x, sp_d)
                dh = dh_v[rows, :]
                hc = h_ref[rows, :]
                if d == 0:
                    A, U = _scan_chunk(_shift_up(a, a_in), dh, True)
                    g = A * g_in + U
                    h_nb = _shift_down(hc, _row_or_zero(h_ref, t0 - 1, S))
                    nxt = (g[0:1], a[0:1])
                else:
                    A, U = _scan_chunk(_shift_down(a, a_in), dh, False)
                    g = A * g_in + U
                    h_nb = _shift_up(hc, _row_or_zero(h_ref, t0 + T, S))
                    nxt = (g[T - 1:T], a[T - 1:T])
                da = g * h_nb
                dbeta = g * (i * xc)
                tb = g * beta
                dla = da * a - dbeta * (a * a / beta)
                dzr = (dla * (-LRU_C * sp_d)) * (r * (1.0 - r))
                dzi = (tb * xc) * (i * (1.0 - i))
                dzrb, dzib, xcb = dzr.astype(BF16), dzi.astype(BF16), xc.astype(BF16)
                dwa_ref[d, 0] += lax.dot_general(xcb, dzrb, TN, preferred_element_type=F32)
                dwx_ref[d, 0] += lax.dot_general(xcb, dzib, TN, preferred_element_type=F32)
                dba_ref[d:d + 1] += jnp.sum(dzr, axis=0, keepdims=True)
                dbx_ref[d:d + 1] += jnp.sum(dzi, axis=0, keepdims=True)
                dlam_ref[d:d + 1] += jnp.sum(dla * (-LRU_C * r), axis=0, keepdims=True)
                dxc = (tb * i + lax.dot_general(dzrb, w_a, NT, preferred_element_type=F32)
                       + lax.dot_general(dzib, w_x, NT, preferred_element_type=F32))
                if d == 0:
                    dxc_v[rows, :] = dxc
                else:
                    dxc_v[rows, :] += dxc
                return nxt

            lax.fori_loop(0, nc, step, (jnp.zeros((1, LANES), F32), jnp.zeros((1, LANES), F32)))

        direction(0)
        direction(1)
        dlam_ref[...] = dlam_ref[...] * (-1.0 / (1.0 + jnp.exp(lam_v)))

        def conv_step(c, carry):
            t0 = pl.multiple_of(c * T, T)
            rows = pl.ds(t0, T)
            g0 = dxc_v[rows, :]
            gm1 = _shift_down(g0, _row_or_zero(dxc_v, t0 - 1, S))
            gm2 = _shift_down(gm1, _row_or_zero(dxc_v, t0 - 2, S))
            gp1 = _shift_up(g0, _row_or_zero(dxc_v, t0 + T, S))
            dxb_ref[rows, :] = (cw[0:1] * gp1 + cw[1:2] * g0 + cw[2:3] * gm1 + cw[3:4] * gm2).astype(BF16)
            xm1, x0, xp1, xp2 = _conv_taps(xb_ref, t0, S)
            for tap, xs in enumerate((xm1, x0, xp1, xp2)):
                dcw_ref[tap:tap + 1] += jnp.sum(g0 * xs, axis=0, keepdims=True)
            dcb_ref[...] += jnp.sum(g0, axis=0, keepdims=True)
            return carry

        lax.fori_loop(0, nc, conv_step, 0)

    col, small, wblk = _lru_specs(S)
    colo = lambda: pl.BlockSpec((S, LANES), lambda n: (0, n), pipeline_mode=pl.Buffered(1))
    return pl.pallas_call(
        body, grid=(LRU_BLOCKS,),
        in_specs=[col(0), col(LRU_BLOCKS), col(0), col(0), col(0), small(4), small(1), wblk, small(2), wblk, small(2), small(2)],
        out_specs=[colo(), colo(), small(4), small(1), wblk, small(2), wblk, small(2), small(2)],
        out_shape=[jax.ShapeDtypeStruct((S, D_MODEL), BF16), jax.ShapeDtypeStruct((S, D_MODEL), BF16),
                   jax.ShapeDtypeStruct((4, D_MODEL), F32), jax.ShapeDtypeStruct((1, D_MODEL), F32),
                   jax.ShapeDtypeStruct((2, LRU_BLOCKS, LANES, LANES), F32), jax.ShapeDtypeStruct((2, D_MODEL), F32),
                   jax.ShapeDtypeStruct((2, LRU_BLOCKS, LANES, LANES), F32), jax.ShapeDtypeStruct((2, D_MODEL), F32),
                   jax.ShapeDtypeStruct((2, D_MODEL), F32)],
        scratch_shapes=[pltpu.VMEM((S, LANES), F32), pltpu.VMEM((S, LANES), F32), pltpu.VMEM((S, LANES), F32)],
        name=name, compiler_params=_cp(("parallel",)))(p, p, hf, hr, dcat, conv_w, conv_b, wa, ba, wx, bx, lam)


PK_UP, PK_DOWN, PK_KV, PK_OUT, PK_IN = 0, 1024, 2048, 2304, 2688
PK_ROWS = {0: PK_IN, 1: PK_IN + 640}
SMALL_G_ROWS = 192
PKF_SMALL = 512
PKF_ROWS = PKF_SMALL + SMALL_G_ROWS


def _mlp_bwd(x, dx, dxb, saved, w_up, w_down, gain, l, rider=None, next_rider=None):
    up, act, h = saved
    pk = _mm_tn(act, dxb, 1, name=f"dw_down{l}", packed=(None, PK_ROWS[l], PK_DOWN), rider=rider)
    pk, carried = pk if rider is not None else (pk, None)
    dup = _mm_nt(dxb, w_down, up=up, name=f"d_up{l}")
    rider_up = next_rider(carried) if next_rider is not None else None
    pk = _mm_tn(h, dup, N_CHIPS, name=f"dw_up{l}", packed=(pk, PK_ROWS[l], PK_UP), rider=rider_up)
    pk, carried = pk if rider_up is not None else (pk, carried)
    dx, dxb, g_gain = _mm_nt(dup, w_up, norm_x=x, norm_g=gain, dres=dx, name=f"d_mlp_in{l}")
    return dx, dxb, pk, g_gain, carried


def _reduce_first(pk, place, tag, recv=None):
    if recv is None:
        recv = _sibling_exchange(pk, name=f"grad_sibling_exchange{tag}")
    return _sum_halves(pk, recv, place, name=f"sum_halves{tag}", tr=pk.shape[1] // 4)


def _sum_parts(parts, place, tag):
    return _sum_chips(parts, place, name=f"sum_chips{tag}", tr=parts.shape[1] // 2)


def _reduce_last(parts, place, tag):
    return _sibling_allgather(_sum_parts(parts, place, tag), name=f"grad_sibling_allgather{tag}")


def _local_step(x, mem, positions, target, W, pending=None, place=None):
    cos_t, sin_t = _rope_tables(positions)
    sinks = W["attn_sinks"].reshape(ATTN_HEADS)
    G = {}

    def hosting(late, fn, *args, **kw):
        if pending is None:
            return fn(*args, **kw)
        *res, buf = fn(*args, gather=pending[late], **kw)
        if late.startswith("w_down"):
            W.setdefault("w_down", [None] * DEPTH)[int(late[-1])] = _ready(late, buf)
        else:
            W[late] = _ready(late, buf)
        return res if len(res) > 1 else res[0]

    kv0, memn = _mm_nn(mem, W["w_mem_kv"][0], norm_g=W["mem_norm"], out_dtype=BF16, name="mem_kv0", tm=256)
    kv1 = _mm_nn(memn, W["w_mem_kv"][1], out_dtype=BF16, name="mem_kv1", tm=256)
    p0, h0 = hosting("w_out", _mm_nn, x, W["attn_w_in"], norm_g=W["mix_norm"][0], name="attn_in")
    q, kd, vd, va = hosting("lru_w_in", _qk_prep, p0, cos_t, sin_t)
    ao, lse = hosting("w_up", _attn_fwd, q, kd, va, sinks)
    cat0 = _memattn_fwd(p0, Q_W // MEM_W + 1, kv0, ao, name="memattn_fwd0")
    x1 = hosting("w_down0", _mm_nn, cat0, W["w_out"][0], resid=x, name="mix_out0")
    up0, act0, h1 = hosting("w_down1", _mm_nn, x1, W["w_up"][0], norm_g=W["mlp_norm"][0], relu2=True, name="mlp_up0")
    x2, mlp0 = _mm_nn(act0, W["w_down"][0], resid=x1, name="mlp_down0"), (up0, act0, h1)
    p1, h2 = _mm_nn(x2, W["lru_w_in"], norm_g=W["mix_norm"][1], name="lru_in")
    lru_w = (W["lru_conv_w"], W["lru_conv_b"], W["lru_wa"], W["lru_ba"], W["lru_wx"], W["lru_bx"], W["lru_lambda"])
    y, hf, hr = _lru_fwd(p1, *lru_w)
    cat1 = _memattn_fwd(p1, 2 * D_MODEL // MEM_W, kv1, y, name="memattn_fwd1")
    x3 = _mm_nn(cat1, W["w_out"][1], resid=x2, name="mix_out1")
    mlp1 = _mm_nn(x3, W["w_up"][1], norm_g=W["mlp_norm"][1], relu2=True, name="mlp_up1")
    loss, dx, dxb, G["final_norm"] = _final(mlp1[1], W["w_down"][1], x3, W["final_norm"], target)

    def put(pk, off, g):
        return pk.at[:, off:off + g.size // (N_CHIPS * ROW)].set(g.reshape(N_CHIPS, -1, ROW))

    dx, dxb, pk1, gm1, _ = _mlp_bwd(x3, dx, dxb, mlp1, W["w_up"][1], W["w_down"][1], W["mlp_norm"][1], 1)
    pk1 = _mm_tn(cat1, dxb, 1, name="dw_out1", tk=384, packed=(pk1, PK_ROWS[1], PK_OUT))
    dcat1 = _mm_nt(dxb, W["w_out"][1], name="d_mix1")
    dmq1, dkv1 = _memattn_bwd(p1, 2 * D_MODEL // MEM_W, kv1, dcat1, name="memattn_bwd1")
    dkv1b = dkv1.astype(BF16)
    pk1 = _mm_tn(memn, dkv1b, 1, name="dw_kv1", tm=256, tk=256, packed=(pk1, PK_ROWS[1], PK_KV))
    (dxb1, dgate, G["lru_conv_w"], G["lru_conv_b"], G["lru_wa"], G["lru_ba"], G["lru_wx"], G["lru_bx"],
     G["lru_lambda"]) = _lru_bwd(p1, hf, hr, dcat1, *lru_w)
    dp1 = jnp.concatenate([dxb1, dgate, dmq1], axis=1)
    pk1 = put(pk1, PK_IN, _mm_tn(h2, dp1, N_CHIPS, name="dw_lru_in"))
    dx, dxb, gx1 = _mm_nt(dp1, W["lru_w_in"], norm_x=x2, norm_g=W["mix_norm"][1], dres=dx, name="d_lru_in")
    dist = place is not None
    h1_rows = PK_ROWS[1] // 4
    kept = {}

    def first_half(recv1):
        kept["halves1"], landing1 = _reduce_first(pk1, place, "1", recv1)
        return _exchange_rider((kept["halves1"], landing1, 0, h1_rows))

    dx, dxb, pk0, gm0, landing1 = _mlp_bwd(x1, dx, dxb, mlp0, W["w_up"][0], W["w_down"][0], W["mlp_norm"][0], 0,
                                           rider=_sib_exchange_rider(pk1) if dist else None,
                                           next_rider=first_half if dist else None)
    pk0 = _mm_tn(cat0, dxb, 1, name="dw_out0", tk=384, packed=(pk0, PK_ROWS[0], PK_OUT))
    dcat0 = _mm_nt(dxb, W["w_out"][0], name="d_mix0")
    dmq0, dkv0 = _memattn_bwd(p0, Q_W // MEM_W + 1, kv0, dcat0, name="memattn_bwd0")
    dkv0b = dkv0.astype(BF16)
    pk0 = _mm_tn(memn, dkv0b, 1, name="dw_kv0", tm=256, tk=256, packed=(pk0, PK_ROWS[0], PK_KV))
    rider = None
    if dist:
        halves0, landing0 = _reduce_first(pk0, place, "0")
        rider = _exchange_rider((kept["halves1"], landing1, h1_rows, h1_rows), (halves0, landing0, 0, halves0.shape[1]))
    dq, dk, dv, dsink, *parts = _attn_bwd(q, kd, vd, cat0, lse, sinks, dcat0, rider=rider)
    dp0 = _qk_prep_bwd(dq, dk, dv, dmq0, cos_t, sin_t)
    g_in = _mm_tn(h0, dp0, N_CHIPS, name="dw_attn_in",
                  rider=_sib_allgather_rider(_sum_parts(parts[0], place, "1"), _sum_parts(parts[1], place, "0")) if dist else None)
    if dist:
        g_in, pk1, pk0 = g_in
    dx, _, gx0 = _mm_nt(dp0, W["attn_w_in"], norm_x=x, norm_g=W["mix_norm"][0], dres=dx, name="d_attn_in")

    w_kv_both = jnp.concatenate([W["w_mem_kv"][0], W["w_mem_kv"][1]], axis=0)
    _, _, G["mem_norm"] = _mm_nt(jnp.concatenate([dkv0b, dkv1b], axis=1), w_kv_both, norm_x=mem, norm_g=W["mem_norm"],
                                 name="d_mem", tm=256)

    G["mix_norm"] = jnp.concatenate([gx0, gx1], axis=0)
    G["mlp_norm"] = jnp.concatenate([gm0, gm1], axis=0)
    G["attn_sinks"] = dsink[0:1, 0:ATTN_HEADS]
    small = _flat_pad(_small_grad_list(G), N_CHIPS * SMALL_G_ROWS * ROW).reshape(N_CHIPS, SMALL_G_ROWS, ROW)
    pkf = jnp.concatenate([g_in.reshape(N_CHIPS, PKF_SMALL, ROW), small], axis=1)
    return loss[0, 0], dx, G, pkf, pk0, pk1


def _comm_call(body, out_shape, n_sems, name, *args, alias=None):
    return pl.pallas_call(
        body, out_shape=out_shape, in_specs=[HBM] * len(args), out_specs=HBM,
        scratch_shapes=[pltpu.SemaphoreType.DMA((n_sems,)), pltpu.SemaphoreType.DMA((n_sems,))],
        input_output_aliases=alias or {}, name=name)(*args)


def _place_slot(shard, slot, n_slots, *, name, tr):
    R, C = shard.shape

    def body(s_ref, a_ref, o_ref):
        o_ref[0] = a_ref[...]

    return pl.pallas_call(
        body,
        grid_spec=pltpu.PrefetchScalarGridSpec(
            num_scalar_prefetch=1, grid=(R // tr,), in_specs=[pl.BlockSpec((tr, C), lambda i, s_ref: (i, 0))],
            out_specs=pl.BlockSpec((1, tr, C), lambda i, s_ref: (s_ref[0], i, 0))),
        out_shape=jax.ShapeDtypeStruct((n_slots, R, C), shard.dtype), name=name,
        compiler_params=_cp(("parallel",)))(slot, shard)


def _allgather_chips(buf, *, name, forward_to_sibling):
    def body(b_ref, o_ref, send_sems, recv_sems):
        if forward_to_sibling:
            _gather_start(o_ref, send_sems, recv_sems)
            _gather_finish(o_ref, send_sems, recv_sems)
            return
        x, y, c, chips = _place()
        own = o_ref.at[2 * x + y]
        sends = [_remote(own, own, send_sems, recv_sems, j, (cx, cy, c)) for j, (cx, cy) in enumerate(chips)]
        for cp in sends:
            cp.start()
        for j, (cx, cy) in enumerate(chips):
            landed = o_ref.at[2 * cx + cy]
            _remote(landed, landed, send_sems, recv_sems, j, (cx, cy, c)).wait_recv()
        for cp in sends:
            cp.wait_send()

    return _comm_call(body, jax.ShapeDtypeStruct(buf.shape, buf.dtype), GATHER_SEMS, name, buf, alias={0: 0})


def _sibling_exchange(g, *, name):
    _, R, C = g.shape
    half = R // 2

    def body(g_ref, o_ref, send_sems, recv_sems):
        _sib_exchange_start(g_ref, o_ref, send_sems, recv_sems)
        _sib_exchange_finish(g_ref, o_ref, send_sems, recv_sems)

    return _comm_call(body, jax.ShapeDtypeStruct((N_CHIPS, half, C), g.dtype), N_CHIPS, name, g)


def _chip_exchange(h, parts, *, name):
    def body(h_ref, p_ref, o_ref, send_sems, recv_sems):
        x, y, c, chips = _place()
        me = 2 * x + y
        cps = [_remote(h_ref.at[2 * cx + cy], o_ref.at[me], send_sems, recv_sems, j, (cx, cy, c))
               for j, (cx, cy) in enumerate(chips)]
        for cp in cps:
            cp.start()
        for j, (cx, cy) in enumerate(chips):
            got = o_ref.at[2 * cx + cy]
            _remote(got, got, send_sems, recv_sems, j, (cx, cy, c)).wait_recv()
        for cp in cps:
            cp.wait_send()

    return _comm_call(body, jax.ShapeDtypeStruct(parts.shape, parts.dtype), 3, name, h, parts, alias={1: 0})


def _sibling_allgather(full, *, name):
    def body(f_ref, o_ref, send_sems, recv_sems):
        _sib_allgather_start(o_ref, send_sems, recv_sems)
        _sib_allgather_finish(o_ref, send_sems, recv_sems)

    return _comm_call(body, jax.ShapeDtypeStruct(full.shape, full.dtype), 1, name, full, alias={0: 0})


def _sum_halves(g, recv, place, *, name="sum_halves", tr=480):
    _, R, C = g.shape
    half = R // 2
    nblk = half // tr

    def body(pl_ref, g_ref, r_ref, o_ref, own_ref):
        v = (g_ref[...] + r_ref[...]).astype(BF16)
        o_ref[...] = v

        @pl.when(pl.program_id(1) == pl_ref[1])
        def _():
            own_ref[...] = v

    blk = pl.BlockSpec((1, tr, C), lambda i, s, p: (s, i, 0))
    return pl.pallas_call(
        body,
        grid_spec=pltpu.PrefetchScalarGridSpec(
            num_scalar_prefetch=1, grid=(nblk, N_CHIPS),
            in_specs=[pl.BlockSpec((1, tr, C), lambda i, s, p: (s, p[0] * nblk + i, 0)), blk],
            out_specs=[blk, pl.BlockSpec((1, tr, C), lambda i, s, p: (p[1], i, 0))]),
        out_shape=[jax.ShapeDtypeStruct((N_CHIPS, half, C), BF16)] * 2, name=name,
        compiler_params=_cp(("parallel", "arbitrary")))(place, g, recv)


def _sum_chips(parts, place, *, name="sum_chips", tr=480):
    _, R, C = parts.shape
    nblk = R // tr

    def body(pl_ref, p_ref, o_ref):
        acc = p_ref[0].astype(F32) + p_ref[1].astype(F32)
        o_ref[...] = (acc + p_ref[2].astype(F32)) + p_ref[3].astype(F32)

    return pl.pallas_call(
        body,
        grid_spec=pltpu.PrefetchScalarGridSpec(
            num_scalar_prefetch=1, grid=(nblk,), in_specs=[pl.BlockSpec((N_CHIPS, tr, C), lambda i, p: (0, i, 0))],
            out_specs=pl.BlockSpec((tr, C), lambda i, p: (p[0] * nblk + i, 0))),
        out_shape=jax.ShapeDtypeStruct((2 * R, C), F32), name=name, compiler_params=_cp(("parallel",)))(place, parts)


def _adamw(w, g, m, v, *, name, tr=128):
    R, C = w.shape
    bc1 = 1.0 - ADAM_B1 ** ADAM_STEP
    bc2 = 1.0 - ADAM_B2 ** ADAM_STEP

    def body(w_ref, g_ref, m_ref, v_ref, d_ref, nm_ref, nv_ref):
        gv = g_ref[...]
        nm = ADAM_B1 * m_ref[...] + (1.0 - ADAM_B1) * gv
        nv = ADAM_B2 * v_ref[...] + (1.0 - ADAM_B2) * (gv * gv)
        d_ref[...] = -ADAM_LR * ((nm / bc1) / (_sqrt(nv / bc2) + ADAM_EPS) + ADAM_WD * w_ref[...])
        nm_ref[...] = nm
        nv_ref[...] = nv

    blk = pl.BlockSpec((tr, C), lambda i: (i, 0))
    return pl.pallas_call(
        body, grid=(R // tr,), in_specs=[blk] * 4, out_specs=[blk] * 3,
        out_shape=[jax.ShapeDtypeStruct((R, C), F32)] * 3, name=name, compiler_params=_cp(("parallel",)))(w, g, m, v)


ROW = 1024
BIG = ("w_mem_kv", "w_out", "w_up", "w_down", "attn_w_in", "lru_w_in")
SMALL_SHARDED = ("lru_conv_w", "lru_conv_b", "lru_ba", "lru_bx", "lru_lambda")
REPLICATED = ("mix_norm", "mlp_norm", "mem_norm", "final_norm", "attn_sinks", "lru_wa", "lru_wx")
SMALL = REPLICATED + SMALL_SHARDED
WEIGHTS = ("mix_norm", "mlp_norm", "mem_norm", "final_norm", "w_mem_kv", "w_out", "w_up", "w_down", "attn_w_in",
           "attn_sinks", "lru_w_in", "lru_conv_w", "lru_conv_b", "lru_wa", "lru_ba", "lru_wx", "lru_bx", "lru_lambda")
SMALL_W_ROWS = 32
ADAM_SMALL_ROWS = 640


def _rows(a):
    return a.reshape(-1, ROW)


def _flat_pad(parts, total):
    flat = jnp.concatenate([p.reshape(-1) for p in parts])
    return jnp.pad(flat, (0, total - flat.shape[0]))


def _pad_rows(a):
    flat = a.reshape(-1)
    n = -(-flat.shape[0] // ROW) * ROW
    return jnp.pad(flat, (0, n - flat.shape[0])).reshape(-1, ROW)


LATE = ("w_out", "lru_w_in", "w_up", "w_down0", "w_down1")


def _ready(name, full):
    if name == "w_out":
        wo = full.reshape(N_CHIPS, DEPTH, -1, D_MODEL)
        return [wo[:, l].reshape(1, MIX_OUT_W, D_MODEL) for l in range(DEPTH)]
    if name == "w_up":
        wu = full.reshape(N_CHIPS, DEPTH, D_MODEL, D_FF // N_CHIPS)
        return [wu[:, l] for l in range(DEPTH)]
    if name == "lru_w_in":
        return full.reshape(N_CHIPS, D_MODEL, LRU_IN_W // N_CHIPS)
    return full.reshape(1, D_FF, D_MODEL)


def _gather_weights(P, chip1):
    bf = lambda a: _rows(a.astype(BF16))
    small = _flat_pad([P[n] for n in SMALL_SHARDED], SMALL_W_ROWS * ROW // 2)
    small_bits = lax.bitcast_convert_type(small, BF16).reshape(SMALL_W_ROWS, ROW)
    early = jnp.concatenate([bf(P["attn_w_in"]), bf(P["w_mem_kv"]), small_bits], axis=0)
    n_in, n_kv = P["attn_w_in"].size // ROW, P["w_mem_kv"].size // ROW
    placed = _place_slot(early, chip1, N_CHIPS, name="place_weights", tr=early.shape[0] // 2)
    full = _allgather_chips(placed, name="allgather_weights", forward_to_sibling=True)
    late = {"w_out": bf(P["w_out"]), "lru_w_in": bf(P["lru_w_in"]), "w_up": bf(P["w_up"]),
            "w_down0": bf(P["w_down"][0]), "w_down1": bf(P["w_down"][1])}
    pending = {n: _place_slot(late[n], chip1, N_CHIPS, name=f"place_{n}", tr=late[n].shape[0] // 2) for n in LATE}
    W = {n: P[n] for n in REPLICATED}
    W["attn_w_in"] = full[:, :n_in].reshape(N_CHIPS, D_MODEL, ATTN_IN_W // N_CHIPS)
    kv = full[:, n_in:n_in + n_kv].reshape(N_CHIPS, DEPTH, -1, D_MODEL)
    W["w_mem_kv"] = [kv[:, l].reshape(1, D_MODEL, D_MODEL) for l in range(DEPTH)]
    sm = lax.bitcast_convert_type(full[:, n_in + n_kv:].reshape(N_CHIPS, -1, 2), F32)
    o = 0
    for n in SMALL_SHARDED:
        shp = P[n].shape[1:]
        cnt = math.prod(shp)
        piece = sm[:, o:o + cnt].reshape((N_CHIPS,) + shp)
        piece = jnp.moveaxis(piece, 0, -2)
        W[n] = piece.reshape(shp[:-1] + (N_CHIPS * shp[-1],)).reshape(-1, D_MODEL)
        o += cnt
    W["lru_wa"] = P["lru_wa"][0].astype(BF16)
    W["lru_wx"] = P["lru_wx"][0].astype(BF16)
    return W, pending


def _small_grad_list(G):
    return [G["mix_norm"], G["mlp_norm"], G["mem_norm"], G["final_norm"], jnp.pad(G["attn_sinks"].reshape(-1), (0, ROW - ATTN_HEADS)),
            G["lru_wa"], G["lru_wx"], G["lru_conv_w"], G["lru_conv_b"], G["lru_ba"], G["lru_bx"], G["lru_lambda"]]


SMALL_G_SIZES = (2 * D_MODEL, 2 * D_MODEL, D_MODEL, D_MODEL, ROW, 2 * 8 * 128 * 128, 2 * 8 * 128 * 128,
                 4 * D_MODEL, D_MODEL, 2 * D_MODEL, 2 * D_MODEL, 2 * D_MODEL)


def _finish_grads(pkf, full0, full1, place, chip1):
    partsf = _chip_exchange(*_reduce_first(pkf, place, "f"), name="grad_chip_exchange_last")
    fullf = _reduce_last(partsf, place, "f")
    small_placed = _place_slot(fullf[PKF_SMALL:], chip1, N_CHIPS, name="place_small_grads", tr=SMALL_G_ROWS)
    small_all = _allgather_chips(small_placed, name="allgather_small_grads", forward_to_sibling=False)
    flat = small_all.reshape(-1)
    small = {}
    o = 0
    names = ("mix_norm", "mlp_norm", "mem_norm", "final_norm", "attn_sinks", "lru_wa", "lru_wx",
             "lru_conv_w", "lru_conv_b", "lru_ba", "lru_bx", "lru_lambda")
    for n, cnt in zip(names, SMALL_G_SIZES):
        small[n] = flat[o:o + cnt]
        o += cnt
    both = lambda off, r: jnp.concatenate([full0[off:off + r], full1[off:off + r]], axis=0)
    big = {"w_up": both(PK_UP, 1024), "w_down": both(PK_DOWN, 1024), "w_out": both(PK_OUT, 384), "w_mem_kv": both(PK_KV, 256),
           "attn_w_in": fullf[:PKF_SMALL], "lru_w_in": full1[PK_IN:PK_IN + 640]}
    return big, small


def kernel(x, mem, positions, mix_norm, mlp_norm, mem_norm, final_norm, w_mem_kv, w_out, w_up, w_down, attn_w_in, attn_sinks, lru_w_in, lru_conv_w, lru_conv_b, lru_wa, lru_ba, lru_wx, lru_bx, lru_lambda, loss_target, m_mix_norm, m_mlp_norm, m_mem_norm, m_final_norm, m_w_mem_kv, m_w_out, m_w_up, m_w_down, m_attn_w_in, m_attn_sinks, m_lru_w_in, m_lru_conv_w, m_lru_conv_b, m_lru_wa, m_lru_ba, m_lru_wx, m_lru_bx, m_lru_lambda, v_mix_norm, v_mlp_norm, v_mem_norm, v_final_norm, v_w_mem_kv, v_w_out, v_w_up, v_w_down, v_attn_w_in, v_attn_sinks, v_lru_w_in, v_lru_conv_w, v_lru_conv_b, v_lru_wa, v_lru_ba, v_lru_wx, v_lru_bx, v_lru_lambda):
    P = dict(mix_norm=mix_norm, mlp_norm=mlp_norm, mem_norm=mem_norm, final_norm=final_norm, w_mem_kv=w_mem_kv, w_out=w_out,
             w_up=w_up, w_down=w_down, attn_w_in=attn_w_in, attn_sinks=attn_sinks, lru_w_in=lru_w_in, lru_conv_w=lru_conv_w,
             lru_conv_b=lru_conv_b, lru_wa=lru_wa, lru_ba=lru_ba, lru_wx=lru_wx, lru_bx=lru_bx, lru_lambda=lru_lambda)
    M1 = dict(mix_norm=m_mix_norm, mlp_norm=m_mlp_norm, mem_norm=m_mem_norm, final_norm=m_final_norm, w_mem_kv=m_w_mem_kv,
              w_out=m_w_out, w_up=m_w_up, w_down=m_w_down, attn_w_in=m_attn_w_in, attn_sinks=m_attn_sinks, lru_w_in=m_lru_w_in,
              lru_conv_w=m_lru_conv_w, lru_conv_b=m_lru_conv_b, lru_wa=m_lru_wa, lru_ba=m_lru_ba, lru_wx=m_lru_wx,
              lru_bx=m_lru_bx, lru_lambda=m_lru_lambda)
    V2 = dict(mix_norm=v_mix_norm, mlp_norm=v_mlp_norm, mem_norm=v_mem_norm, final_norm=v_final_norm, w_mem_kv=v_w_mem_kv,
              w_out=v_w_out, w_up=v_w_up, w_down=v_w_down, attn_w_in=v_attn_w_in, attn_sinks=v_attn_sinks, lru_w_in=v_lru_w_in,
              lru_conv_w=v_lru_conv_w, lru_conv_b=v_lru_conv_b, lru_wa=v_lru_wa, lru_ba=v_lru_ba, lru_wx=v_lru_wx,
              lru_bx=v_lru_bx, lru_lambda=v_lru_lambda)
    chip = 2 * lax.axis_index("x") + lax.axis_index("y")
    chip1 = chip.astype(jnp.int32).reshape(1)
    place = jnp.stack([lax.axis_index("c").astype(jnp.int32), chip.astype(jnp.int32)])

    W, pending = _gather_weights(P, chip1)
    loss, dx, _, pkf, full0, full1 = _local_step(x[0], mem[0], positions[0], loss_target[0], W, pending, place)
    loss = lax.psum(loss, ("x", "y", "c"))
    big, small = _finish_grads(pkf, full0, full1, place, chip1)

    grads, deltas, new_m, new_v = {}, {}, {}, {}
    for n in BIG:
        g = big[n]
        d, nm, nv = _adamw(_rows(P[n]), g, _rows(M1[n]), _rows(V2[n]), name=f"adamw_{n}")
        grads[n], deltas[n], new_m[n], new_v[n] = (t.reshape(P[n].shape) for t in (g, d, nm, nv))

    for n in SMALL:
        g = small[n]
        if n in SMALL_SHARDED:
            shard = P[n].shape[-1]
            g = lax.dynamic_slice_in_dim(g.reshape(-1, N_CHIPS * shard), chip * shard, shard, axis=1)
        elif n == "attn_sinks":
            g = g[:ATTN_HEADS]
        grads[n] = g.reshape(P[n].shape)
    packs = []
    for src in (P, grads, M1, V2):
        a = jnp.concatenate([_pad_rows(src[n]) for n in SMALL], axis=0)
        packs.append(jnp.pad(a, ((0, ADAM_SMALL_ROWS - a.shape[0]), (0, 0))))
    d_s, nm_s, nv_s = _adamw(*packs, name="adamw_small")
    o = 0
    for n in SMALL:
        cnt = math.prod(P[n].shape)
        r = -(-cnt // ROW)
        for dst, src in ((deltas, d_s), (new_m, nm_s), (new_v, nv_s)):
            dst[n] = src[o:o + r].reshape(-1)[:cnt].reshape(P[n].shape)
        o += r

    return (loss, dx[None], *[grads[n] for n in WEIGHTS], *[deltas[n] for n in WEIGHTS],
            *[new_m[n] for n in WEIGHTS], *[new_v[n] for n in WEIGHTS])
```

```python
import functools
import math

import jax
import jax.numpy as jnp
from jax import lax
from jax.experimental import pallas as pl
from jax.experimental.pallas import tpu as pltpu

F32 = jnp.float32
BF16 = jnp.bfloat16
MESH = pl.DeviceIdType.MESH

D_MODEL = 1024
DEPTH = 2
EPS = 1e-6
ATTN_HEADS = 16
ATTN_KV_HEADS = 4
HEAD_DIM = 64
WINDOW = 128
BLOCK = 128
ROPE_THETA = 500000.0
ROPE_DIM = 16
Q_W = 1024
KV_W = 256
MEM_LEN = 256
MEM_HEADS = 4
MEM_HEAD_DIM = 128
MEM_W = 512
LRU_BLOCKS = 8
LRU_C = 8.0
ATTN_IN_W = 2048
LRU_IN_W = 2560
MIX_OUT_W = 1536
D_FF = 4096
NEG = -1e30
N_CHIPS = 4

ADAM_LR = 0.001
ADAM_B1 = 0.9
ADAM_B2 = 0.999
ADAM_EPS = 1e-08
ADAM_WD = 0.01
ADAM_STEP = 10

LANES = 128
SCAN_ROWS = 512
VMEM_LIMIT = 56 * 1024 * 1024

NT = (((1,), (1,)), ((), ()))
TN = (((0,), (0,)), ((), ()))


def _cp(sem=None):
    return pltpu.CompilerParams(dimension_semantics=sem, vmem_limit_bytes=VMEM_LIMIT)


HBM = pl.BlockSpec(memory_space=pl.ANY)
GATHER_SEMS = 6


def _place():
    x, y, c = lax.axis_index("x"), lax.axis_index("y"), lax.axis_index("c")
    chips = [(1 - x, y), (x, 1 - y), (1 - x, 1 - y)]
    return x, y, c, chips


def _remote(src, dst, send_sems, recv_sems, k, to):
    return pltpu.make_async_remote_copy(src_ref=src, dst_ref=dst, send_sem=send_sems.at[k], recv_sem=recv_sems.at[k],
                                        device_id=to, device_id_type=MESH)


def _gather_start(o_ref, send_sems, recv_sems):
    x, y, c, chips = _place()
    half = o_ref.shape[1] // 2
    own = o_ref.at[2 * x + y, pl.ds(pl.multiple_of(c * half, 16), half)]
    for j, (cx, cy) in enumerate(chips):
        _remote(own, own, send_sems, recv_sems, j, (cx, cy, c)).start()


def _gather_forward(o_ref, send_sems, recv_sems):
    x, y, c, chips = _place()
    half = o_ref.shape[1] // 2
    my_rows = pl.ds(pl.multiple_of(c * half, 16), half)
    for j, (cx, cy) in enumerate(chips):
        landed = o_ref.at[2 * cx + cy, my_rows]
        _remote(landed, landed, send_sems, recv_sems, j, (cx, cy, c)).wait_recv()
        _remote(landed, landed, send_sems, recv_sems, 3 + j, (x, y, 1 - c)).start()


def _gather_drain(o_ref, send_sems, recv_sems):
    x, y, c, chips = _place()
    half = o_ref.shape[1] // 2
    my_rows = pl.ds(pl.multiple_of(c * half, 16), half)
    sib_rows = pl.ds(pl.multiple_of((1 - c) * half, 16), half)
    own = o_ref.at[2 * x + y, my_rows]
    for j, (cx, cy) in enumerate(chips):
        got = o_ref.at[2 * cx + cy, sib_rows]
        _remote(got, got, send_sems, recv_sems, 3 + j, (x, y, 1 - c)).wait_recv()
    for j, (cx, cy) in enumerate(chips):
        _remote(own, own, send_sems, recv_sems, j, (cx, cy, c)).wait_send()
        landed = o_ref.at[2 * cx + cy, my_rows]
        _remote(landed, landed, send_sems, recv_sems, 3 + j, (x, y, 1 - c)).wait_send()


def _gather_finish(o_ref, send_sems, recv_sems):
    _gather_forward(o_ref, send_sems, recv_sems)
    _gather_drain(o_ref, send_sems, recv_sems)


def _exchange_start(h_ref, o_ref, send_sems, recv_sems, rows=None, base=0):
    x, y, c, chips = _place()
    rows = pl.ds(0, h_ref.shape[1]) if rows is None else rows
    for j, (cx, cy) in enumerate(chips):
        _remote(h_ref.at[2 * cx + cy, rows], o_ref.at[2 * x + y, rows], send_sems, recv_sems, base + j, (cx, cy, c)).start()


def _exchange_finish(h_ref, o_ref, send_sems, recv_sems, rows=None, base=0):
    x, y, c, chips = _place()
    rows = pl.ds(0, h_ref.shape[1]) if rows is None else rows
    for j, (cx, cy) in enumerate(chips):
        got = o_ref.at[2 * cx + cy, rows]
        _remote(got, got, send_sems, recv_sems, base + j, (cx, cy, c)).wait_recv()
    for j, (cx, cy) in enumerate(chips):
        _remote(h_ref.at[2 * cx + cy, rows], o_ref.at[2 * x + y, rows], send_sems, recv_sems, base + j, (cx, cy, c)).wait_send()


def _sib_exchange_copies(g_ref, o_ref, send_sems, recv_sems):
    x, y, c, _ = _place()
    half = g_ref.shape[1] // 2
    other = pl.ds(pl.multiple_of((1 - c) * half, 8), half)
    return [_remote(g_ref.at[s, other], o_ref.at[s], send_sems, recv_sems, s, (x, y, 1 - c)) for s in range(N_CHIPS)]


def _sib_exchange_start(*refs):
    for cp in _sib_exchange_copies(*refs):
        cp.start()


def _sib_exchange_finish(*refs):
    for cp in _sib_exchange_copies(*refs):
        cp.wait()


def _sib_allgather_start(*refs):
    *o_refs, send_sems, recv_sems = refs
    x, y, c, _ = _place()
    for i, o_ref in enumerate(o_refs):
        half = o_ref.shape[0] // 2
        mine = o_ref.at[pl.ds(pl.multiple_of(c * half, 8), half)]
        _remote(mine, mine, send_sems, recv_sems, i, (x, y, 1 - c)).start()


def _sib_allgather_finish(*refs):
    *o_refs, send_sems, recv_sems = refs
    x, y, c, _ = _place()
    for i, o_ref in enumerate(o_refs):
        half = o_ref.shape[0] // 2
        mine = o_ref.at[pl.ds(pl.multiple_of(c * half, 8), half)]
        got = o_ref.at[pl.ds(pl.multiple_of((1 - c) * half, 8), half)]
        _remote(got, got, send_sems, recv_sems, i, (x, y, 1 - c)).wait_recv()
        _remote(mine, mine, send_sems, recv_sems, i, (x, y, 1 - c)).wait_send()


class _Rider:
    def __init__(self, args, start, finish, inplace=1, mid=None):
        self.args, self.start, self.finish, self.inplace, self.mid = list(args), start, finish, inplace, mid


def _gather_rider(buf):
    return None if buf is None else _Rider([buf], _gather_start, _gather_finish, mid=(_gather_forward, _gather_drain))


def _exchange_rider(*parts):
    n = len(parts)
    assert 3 * n <= GATHER_SEMS

    def run(fn):
        def go(*refs):
            sems = refs[2 * n:]
            for i, (_, _, r0, nr) in enumerate(parts):
                fn(refs[i], refs[n + i], *sems, rows=pl.ds(r0, nr), base=3 * i)
        return go

    return _Rider([p[0] for p in parts] + [p[1] for p in parts], run(_exchange_start), run(_exchange_finish), inplace=n)


def _sib_exchange_rider(g):
    landing = lax.empty((N_CHIPS, g.shape[1] // 2, g.shape[2]), g.dtype)
    return _Rider([g, landing], _sib_exchange_start, _sib_exchange_finish)


def _sib_allgather_rider(*fulls):
    return _Rider(fulls, _sib_allgather_start, _sib_allgather_finish, inplace=len(fulls))


class _Hosted:
    def __init__(self, rider, n_in, n_out):
        self.rider = rider
        self.on = rider is not None
        self.args = rider.args if self.on else []
        k = len(self.args)
        p = self.p = rider.inplace if self.on else 0
        self.alias = {n_in + k - p + i: n_out + i for i in range(p)}
        self.in_specs = [HBM] * k
        self.out_specs = [HBM] * p
        self.out_shape = [jax.ShapeDtypeStruct(a.shape, a.dtype) for a in self.args[k - p:]]
        self.scratch = [pltpu.SemaphoreType.DMA((GATHER_SEMS,)), pltpu.SemaphoreType.DMA((GATHER_SEMS,))] if self.on else []

    def split(self, refs, n_in, n_out):
        refs = list(refs)
        if not self.on:
            return refs[:n_in], refs[n_in:n_in + n_out], refs[n_in + n_out:], None
        k, p = len(self.args), self.p
        ins, outs = refs[:n_in], refs[n_in + k:n_in + k + n_out]
        rest = refs[n_in + k + n_out + p:]
        rrefs = refs[n_in:n_in + k - p] + refs[n_in + k + n_out:n_in + k + n_out + p] + [rest[-2], rest[-1]]
        return ins, outs, rest[:-2], rrefs

    def run(self, rrefs, step, n_steps, compute):
        if rrefs is None:
            return compute()

        mid = self.rider.mid
        mid_step = (3 * n_steps) // 4
        two_stage = mid is not None and 0 < mid_step < n_steps - 1

        @pl.when(step == 0)
        def _():
            self.rider.start(*rrefs)

        compute()

        if two_stage:
            @pl.when(step == mid_step)
            def _():
                mid[0](*rrefs)

        @pl.when(step == n_steps - 1)
        def _():
            (mid[1] if two_stage else self.rider.finish)(*rrefs)


def _mm_nn(a, w3, *, name, out_dtype=F32, norm_g=None, resid=None, relu2=False, tm=512, gather=None):
    M, K = a.shape
    ns, _, n = w3.shape
    N = ns * n
    tm = min(tm, M)
    has_norm = norm_g is not None
    has_res = resid is not None
    n_in = 2 + has_norm + has_res
    n_out = (2 if relu2 else 1) + has_norm
    host = _Hosted(_gather_rider(gather), n_in, n_out)

    def body(*refs):
        ins, outs, _, gref = host.split(refs, n_in, n_out)
        a_ref, w_ref = ins[0], ins[1]
        g_ref = ins[2] if has_norm else None
        r_ref = ins[-1] if has_res else None

        def compute():
            if has_norm:
                xv = a_ref[...]
                rs = lax.rsqrt(jnp.mean(xv * xv, axis=-1, keepdims=True) + EPS)
                ab = (xv * rs * g_ref[...]).astype(BF16)
                outs[-1][...] = ab
            else:
                ab = a_ref[...]
            for s in range(ns):
                acc = jnp.dot(ab, w_ref[s], preferred_element_type=F32)
                sl = slice(s * n, (s + 1) * n)
                if relu2:
                    outs[0][:, sl] = acc.astype(BF16)
                    rl = jnp.maximum(acc, 0.0)
                    outs[1][:, sl] = (rl * rl).astype(BF16)
                elif has_res:
                    outs[0][:, sl] = r_ref[:, sl] + acc
                else:
                    outs[0][:, sl] = acc.astype(out_dtype)

        host.run(gref, pl.program_id(0), M // tm, compute)

    row = lambda w: pl.BlockSpec((tm, w), lambda i: (i, 0))
    in_specs = [row(K), pl.BlockSpec((ns, K, n), lambda i: (0, 0, 0))]
    args = [a, w3]
    if has_norm:
        in_specs.append(pl.BlockSpec((1, K), lambda i: (0, 0)))
        args.append(norm_g.reshape(1, K))
    if has_res:
        in_specs.append(row(N))
        args.append(resid)
    if relu2:
        out_shape = [jax.ShapeDtypeStruct((M, N), BF16), jax.ShapeDtypeStruct((M, N), BF16)]
        out_specs = [row(N), row(N)]
    else:
        out_shape = [jax.ShapeDtypeStruct((M, N), F32 if has_res else out_dtype)]
        out_specs = [row(N)]
    if has_norm:
        out_shape.append(jax.ShapeDtypeStruct((M, K), BF16))
        out_specs.append(row(K))
    res = pl.pallas_call(body, grid=(M // tm,), in_specs=in_specs + host.in_specs, out_specs=out_specs + host.out_specs,
                         out_shape=out_shape + host.out_shape, scratch_shapes=host.scratch, input_output_aliases=host.alias,
                         name=name, compiler_params=_cp(("arbitrary",) if host.on else ("parallel",)))(*args, *host.args)
    return res if len(res) > 1 else res[0]


def _mm_nt(g, w3, *, name, out_dtype=BF16, up=None, norm_x=None, norm_g=None, dres=None, tm=512):
    M = g.shape[0]
    ns, K, n = w3.shape
    tm = min(tm, M)
    has_up = up is not None
    has_norm = norm_x is not None
    has_res = dres is not None

    def body(*refs):
        refs = list(refs)
        g_ref, w_ref = refs[0], refs[1]
        pos = 2
        if has_up:
            up_ref = refs[pos]
            pos += 1
        if has_norm:
            x_ref, gn_ref = refs[pos], refs[pos + 1]
            pos += 2
        if has_res:
            r_ref = refs[pos]
            pos += 1
        outs = refs[pos:]
        acc = None
        for s in range(ns):
            part = lax.dot_general(g_ref[:, s * n:(s + 1) * n], w_ref[s], NT, preferred_element_type=F32)
            acc = part if acc is None else acc + part
        if has_up:
            outs[0][...] = (acc * (2.0 * jnp.maximum(up_ref[...].astype(F32), 0.0))).astype(BF16)
        elif has_norm:
            xv = x_ref[...]
            rs = lax.rsqrt(jnp.mean(xv * xv, axis=-1, keepdims=True) + EPS)
            xn = xv * rs
            dxn = acc * gn_ref[...]
            dx = rs * (dxn - xn * jnp.mean(dxn * xn, axis=-1, keepdims=True))
            if has_res:
                dx = dx + r_ref[...]
            outs[0][...] = dx
            outs[1][...] = dx.astype(BF16)

            @pl.when(pl.program_id(0) == 0)
            def _():
                outs[2][...] = jnp.zeros_like(outs[2])

            outs[2][...] += jnp.sum(acc * xn, axis=0, keepdims=True)
        else:
            outs[0][...] = acc.astype(out_dtype)

    row = lambda w: pl.BlockSpec((tm, w), lambda i: (i, 0))
    in_specs = [row(ns * n), pl.BlockSpec((ns, K, n), lambda i: (0, 0, 0))]
    args = [g, w3]
    if has_up:
        in_specs.append(row(K))
        args.append(up)
    if has_norm:
        in_specs += [row(K), pl.BlockSpec((1, K), lambda i: (0, 0))]
        args += [norm_x, norm_g.reshape(1, K)]
    if has_res:
        in_specs.append(row(K))
        args.append(dres)
    if has_norm:
        out_shape = [jax.ShapeDtypeStruct((M, K), F32), jax.ShapeDtypeStruct((M, K), BF16),
                     jax.ShapeDtypeStruct((1, K), F32)]
        out_specs = [row(K), row(K), pl.BlockSpec((1, K), lambda i: (0, 0))]
        sem = ("arbitrary",)
    else:
        out_shape = [jax.ShapeDtypeStruct((M, K), BF16 if has_up else out_dtype)]
        out_specs = [row(K)]
        sem = ("parallel",)
    res = pl.pallas_call(body, grid=(M // tm,), in_specs=in_specs, out_specs=out_specs, out_shape=out_shape,
                         name=name, compiler_params=_cp(sem))(*args)
    return res if len(res) > 1 else res[0]


def _mm_tn(a, g, ns, *, name, tk=512, tm=4096, packed=None, rider=None):
    M, K = a.shape
    n = g.shape[1] // ns
    tm = min(tm, M)
    tk = min(tk, K)
    nk, nm = K // tk, M // tm
    n_in = 3 if (packed is not None and packed[0] is not None) else 2
    host = _Hosted(rider, n_in, 1)

    def body(*refs):
        ins, outs, _, rrefs = host.split(refs, n_in, 1)
        a_ref, g_ref, o_ref = ins[0], ins[1], outs[0]

        def compute():
            @pl.when(pl.program_id(2) == 0)
            def _():
                o_ref[...] = jnp.zeros_like(o_ref)

            o_ref[0] += lax.dot_general(a_ref[...], g_ref[...], TN, preferred_element_type=F32)

        step = (pl.program_id(0) * nk + pl.program_id(1)) * nm + pl.program_id(2)
        host.run(rrefs, step, ns * nk * nm, compute)

    in_specs = [pl.BlockSpec((tm, tk), lambda s, k, m: (m, k)), pl.BlockSpec((tm, n), lambda s, k, m: (m, s))]
    args = [a, g]
    alias = {}
    if packed is None:
        out_spec = pl.BlockSpec((1, tk, n), lambda s, k, m: (s, k, 0))
        out_shape = jax.ShapeDtypeStruct((ns, K, n), F32)
    else:
        buf, rows, off = packed
        per_chip = K * ns // N_CHIPS
        assert n == ROW and per_chip % tk == 0 and off % tk == 0
        if ns == N_CHIPS:
            out_spec = pl.BlockSpec((1, tk, n), lambda s, k, m: (s, off // tk + k, 0))
        else:
            kpc = per_chip // tk
            out_spec = pl.BlockSpec((1, tk, n), lambda s, k, m: (k // kpc, off // tk + k % kpc, 0))
        out_shape = jax.ShapeDtypeStruct((N_CHIPS, rows, ROW), F32)
        if buf is not None:
            in_specs.append(HBM)
            args.append(buf)
            alias = {2: 0}
    sem = ("arbitrary",) * 3 if host.on else ("parallel", "parallel", "arbitrary")
    res = pl.pallas_call(
        body, grid=(ns, nk, nm), in_specs=in_specs + host.in_specs, out_specs=[out_spec] + host.out_specs,
        out_shape=[out_shape] + host.out_shape, scratch_shapes=host.scratch, name=name,
        input_output_aliases={**alias, **host.alias}, compiler_params=_cp(sem))(*args, *host.args)
    return res if host.on else res[0]


def _final(act, w_down, x, gain, target, *, name="mlp_down_final", tr=512):
    S, Dm = x.shape
    tr = min(tr, S)
    Kf = act.shape[1]

    def body(a_ref, w_ref, x_ref, g_ref, t_ref, loss_ref, dx_ref, dxb_ref, dg_ref):
        @pl.when(pl.program_id(0) == 0)
        def _():
            loss_ref[...] = jnp.zeros_like(loss_ref)
            dg_ref[...] = jnp.zeros_like(dg_ref)

        xv = x_ref[...] + jnp.dot(a_ref[...], w_ref[0], preferred_element_type=F32)
        gv = g_ref[...]
        rs = lax.rsqrt(jnp.mean(xv * xv, axis=-1, keepdims=True) + EPS)
        xn = xv * rs
        err = xn * gv - t_ref[...]
        loss_ref[...] += 0.5 * jnp.sum(jnp.mean(err * err, axis=-1, keepdims=True), axis=0, keepdims=True)
        dout = err * (1.0 / Dm)
        dg_ref[...] += jnp.sum(dout * xn, axis=0, keepdims=True)
        dxn = dout * gv
        dx = rs * (dxn - xn * jnp.mean(dxn * xn, axis=-1, keepdims=True))
        dx_ref[...] = dx
        dxb_ref[...] = dx.astype(BF16)

    row = pl.BlockSpec((tr, Dm), lambda i: (i, 0))
    return pl.pallas_call(
        body, grid=(S // tr,),
        in_specs=[pl.BlockSpec((tr, Kf), lambda i: (i, 0)), pl.BlockSpec((1, Kf, Dm), lambda i: (0, 0, 0)), row,
                  pl.BlockSpec((1, Dm), lambda i: (0, 0)), row],
        out_specs=[pl.BlockSpec((1, 1), lambda i: (0, 0)), row, row, pl.BlockSpec((1, Dm), lambda i: (0, 0))],
        out_shape=[jax.ShapeDtypeStruct((1, 1), F32), jax.ShapeDtypeStruct((S, Dm), F32),
                   jax.ShapeDtypeStruct((S, Dm), BF16), jax.ShapeDtypeStruct((1, Dm), F32)],
        name=name, compiler_params=_cp(("arbitrary",)))(act, w_down, x, gain.reshape(1, Dm), target)


def _rope_tables(positions):
    half = ROPE_DIM // 2
    inv_freq = ROPE_THETA ** (-2.0 * jnp.arange(half, dtype=F32) / ROPE_DIM)
    ang = positions.astype(F32)[:, None] * inv_freq
    cos, sin = jnp.cos(ang), jnp.sin(ang)
    S = positions.shape[0]
    ones = jnp.ones((S, HEAD_DIM - ROPE_DIM), F32)
    cos64 = jnp.concatenate([cos, cos, ones], axis=1)
    sin64 = jnp.concatenate([-sin, sin, 0.0 * ones], axis=1)
    return jnp.tile(cos64, (1, 2)), jnp.tile(sin64, (1, 2))


def _rope_partner(t):
    lane = lax.broadcasted_iota(jnp.int32, t.shape, 1)
    low = (lane & (HEAD_DIM - 1)) < (ROPE_DIM // 2)
    return jnp.where(low, pltpu.roll(t, LANES - ROPE_DIM // 2, 1), pltpu.roll(t, ROPE_DIM // 2, 1))


def _qk_prep(p, cos_t, sin_t, *, name="qk_prep", tr=256, gather=None):
    S = p.shape[0]
    tr = min(tr, S)
    scale = HEAD_DIM ** -0.5
    host = _Hosted(_gather_rider(gather), 3, 4)

    def body(*refs):
        ins, outs, _, gref = host.split(refs, 3, 4)
        host.run(gref, pl.program_id(0), S // tr, lambda: inner(*ins, *outs))

    def inner(p_ref, c_ref, s_ref, q_ref, k_ref, v_ref, va_ref):
        cs, sn = c_ref[...], s_ref[...]
        lane = lax.broadcasted_iota(jnp.int32, (tr, LANES), 1)
        lo = lane < HEAD_DIM
        for c in range(Q_W // LANES):
            t = p_ref[:, c * LANES:(c + 1) * LANES]
            q_ref[:, c * LANES:(c + 1) * LANES] = ((t * cs + _rope_partner(t) * sn) * scale).astype(BF16)
        for c in range(KV_W // LANES):
            t = p_ref[:, Q_W + c * LANES:Q_W + (c + 1) * LANES]
            kc = t * cs + _rope_partner(t) * sn
            vc = p_ref[:, Q_W + KV_W + c * LANES:Q_W + KV_W + (c + 1) * LANES]
            for arr, ref in ((kc, k_ref), (vc, v_ref)):
                sw = pltpu.roll(arr, HEAD_DIM, 1)
                ref[:, (2 * c) * LANES:(2 * c + 1) * LANES] = jnp.where(lo, arr, sw).astype(BF16)
                ref[:, (2 * c + 1) * LANES:(2 * c + 2) * LANES] = jnp.where(lo, sw, arr).astype(BF16)
            sw = pltpu.roll(vc, HEAD_DIM, 1)
            for k, aug in enumerate((jnp.where(lo, vc, 1.0), jnp.where(lo, 1.0, sw), jnp.where(lo, sw, 1.0), jnp.where(lo, 1.0, vc))):
                va_ref[:, (4 * c + k) * LANES:(4 * c + k + 1) * LANES] = aug.astype(BF16)

    row = lambda w: pl.BlockSpec((tr, w), lambda i: (i, 0))
    return pl.pallas_call(
        body, grid=(S // tr,), in_specs=[row(ATTN_IN_W), row(LANES), row(LANES)] + host.in_specs,
        out_specs=[row(Q_W), row(2 * KV_W), row(2 * KV_W), row(4 * KV_W)] + host.out_specs,
        out_shape=[jax.ShapeDtypeStruct((S, Q_W), BF16), jax.ShapeDtypeStruct((S, 2 * KV_W), BF16),
                   jax.ShapeDtypeStruct((S, 2 * KV_W), BF16), jax.ShapeDtypeStruct((S, 4 * KV_W), BF16)] + host.out_shape,
        scratch_shapes=host.scratch, input_output_aliases=host.alias,
        name=name, compiler_params=_cp(("arbitrary",) if host.on else ("parallel",)))(p, cos_t, sin_t, *host.args)


def _qk_prep_bwd(dq, dk, dv, dmq, cos_t, sin_t, *, name="qk_prep_bwd", tr=256):
    S = dq.shape[0]
    tr = min(tr, S)

    def body(dq_ref, dk_ref, dv_ref, dmq_ref, c_ref, s_ref, o_ref):
        cs, sn = c_ref[...], s_ref[...]
        for c in range(Q_W // LANES):
            t = dq_ref[:, c * LANES:(c + 1) * LANES]
            o_ref[:, c * LANES:(c + 1) * LANES] = (t * cs - _rope_partner(t) * sn).astype(BF16)
        for c in range(KV_W // LANES):
            t = dk_ref[:, c * LANES:(c + 1) * LANES]
            o_ref[:, Q_W + c * LANES:Q_W + (c + 1) * LANES] = (t * cs - _rope_partner(t) * sn).astype(BF16)
        o_ref[:, Q_W + KV_W:Q_W + 2 * KV_W] = dv_ref[...].astype(BF16)
        o_ref[:, Q_W + 2 * KV_W:] = dmq_ref[...]

    row = lambda w: pl.BlockSpec((tr, w), lambda i: (i, 0))
    return pl.pallas_call(
        body, grid=(S // tr,), in_specs=[row(Q_W), row(KV_W), row(KV_W), row(MEM_W), row(LANES), row(LANES)],
        out_specs=row(ATTN_IN_W), out_shape=jax.ShapeDtypeStruct((S, ATTN_IN_W), BF16),
        name=name, compiler_params=_cp(("parallel",)))(dq, dk, dv, dmq, cos_t, sin_t)


def _band(n, S):
    start = pl.multiple_of(jnp.clip((n - 1) * BLOCK, 0, S - 3 * BLOCK), BLOCK)
    qi = lax.broadcasted_iota(jnp.int32, (BLOCK, 3 * BLOCK), 0) + n * BLOCK
    ki = lax.broadcasted_iota(jnp.int32, (BLOCK, 3 * BLOCK), 1) + start
    return start, jnp.abs(ki - qi) <= WINDOW


def _head_operand(ref, h, lo):
    c = h // 2
    t = ref[:, c * LANES:(c + 1) * LANES].astype(F32)
    return jnp.where(lo if h % 2 == 0 else jnp.logical_not(lo), t, 0.0).astype(BF16)


GROUP = ATTN_HEADS // ATTN_KV_HEADS
EVENS_FIRST = (0, 2, 1, 3)


def _attn_fwd(q, kd, va, sinks, *, name="attn_fwd", gather=None):
    S = q.shape[0]
    host = _Hosted(_gather_rider(gather), 4, 2)

    def body(*refs):
        ins, outs, scr, gref = host.split(refs, 4, 2)
        host.run(gref, pl.program_id(0), S // BLOCK, lambda: inner(*ins, *outs, *scr))

    def inner(sink_ref, q_ref, k_ref, va_ref, o_ref, lse_ref, p_scr):
        n = pl.program_id(0)
        start, mask = _band(n, S)
        lane = lax.broadcasted_iota(jnp.int32, (BLOCK, LANES), 1)
        lo = lane < HEAD_DIM
        rows = pl.ds(start, 3 * BLOCK)
        scores = []
        for g in range(ATTN_KV_HEADS):
            qst = jnp.concatenate([_head_operand(q_ref, GROUP * g + j, lo) for j in EVENS_FIRST], axis=0)
            scores.append(lax.dot_general(qst, k_ref[rows, g * LANES:(g + 1) * LANES], NT, preferred_element_type=F32))
        ms = {}
        for g in range(ATTN_KV_HEADS):
            for pos, j in enumerate(EVENS_FIRST):
                h = GROUP * g + j
                s = jnp.where(mask, scores[g][pos * BLOCK:(pos + 1) * BLOCK], NEG)
                ms[h] = jnp.maximum(jnp.max(s, axis=-1, keepdims=True), sink_ref[h])
                p_scr[(GROUP * g + pos) * BLOCK:(GROUP * g + pos + 1) * BLOCK, :] = jnp.exp(s - ms[h]).astype(BF16)
        pvs = {}
        for g in range(ATTN_KV_HEADS):
            for par in range(2):
                r0 = (GROUP * g + 2 * par) * BLOCK
                pvs[g, par] = jnp.dot(p_scr[r0:r0 + 2 * BLOCK, :], va_ref[rows, (2 * g + par) * LANES:(2 * g + par + 1) * LANES],
                                      preferred_element_type=F32)
        lse_blk = jnp.zeros((BLOCK, LANES), F32)
        for g in range(ATTN_KV_HEADS):
            outs = {}
            for par in range(2):
                for k in range(2):
                    j = EVENS_FIRST[2 * par + k]
                    h = GROUP * g + j
                    pv = pvs[g, par][k * BLOCK:(k + 1) * BLOCK]
                    den = pltpu.roll(pv, HEAD_DIM, 1) + jnp.exp(sink_ref[h] - ms[h])
                    outs[j] = pv * (1.0 / den)
                    l = den[:, par * HEAD_DIM:par * HEAD_DIM + 1]
                    lse_blk = jnp.where(lane == h, ms[h] + jnp.log(l), lse_blk)
            for jj in range(2):
                o_ref[:, (2 * g + jj) * LANES:(2 * g + jj + 1) * LANES] = jnp.where(lo, outs[2 * jj], outs[2 * jj + 1]).astype(BF16)
        lse_ref[...] = lse_blk

    full = lambda w: pl.BlockSpec((S, w), lambda i: (0, 0))
    return pl.pallas_call(
        body, grid=(S // BLOCK,),
        in_specs=[pl.BlockSpec(memory_space=pltpu.SMEM), pl.BlockSpec((BLOCK, Q_W), lambda i: (i, 0)),
                  full(2 * KV_W), full(4 * KV_W)] + host.in_specs,
        out_specs=[pl.BlockSpec((BLOCK, Q_W), lambda i: (i, 0)), pl.BlockSpec((BLOCK, LANES), lambda i: (i, 0))] + host.out_specs,
        out_shape=[jax.ShapeDtypeStruct((S, MIX_OUT_W), BF16), jax.ShapeDtypeStruct((S, LANES), F32)] + host.out_shape,
        scratch_shapes=[pltpu.VMEM((ATTN_HEADS * BLOCK, 3 * BLOCK), BF16)] + host.scratch, input_output_aliases=host.alias,
        name=name, compiler_params=_cp(("arbitrary",) if host.on else ("parallel",)))(sinks, q, kd, va, *host.args)


def _attn_bwd(q, kd, vd, ao, lse, sinks, dcat, *, name="attn_bwd", rider=None):
    S = q.shape[0]
    scale = HEAD_DIM ** -0.5
    host = _Hosted(rider, 7, 4)

    def body(*refs):
        ins, outs, scr, rrefs = host.split(refs, 7, 4)
        host.run(rrefs, pl.program_id(0), S // BLOCK, lambda: inner(*ins, *outs, *scr))

    def inner(sink_ref, q_ref, k_ref, v_ref, ao_ref, lse_ref, do_ref, dq_ref, dk_ref, dv_ref, ds_ref, p_scr, dsb_scr):
        n = pl.program_id(0)

        @pl.when(n == 0)
        def _():
            dk_ref[...] = jnp.zeros_like(dk_ref)
            dv_ref[...] = jnp.zeros_like(dv_ref)
            ds_ref[...] = jnp.zeros_like(ds_ref)

        start, mask = _band(n, S)
        lane = lax.broadcasted_iota(jnp.int32, (BLOCK, LANES), 1)
        lo = lane < HEAD_DIM
        lane3 = lax.broadcasted_iota(jnp.int32, (3 * BLOCK, LANES), 1)
        row8 = lax.broadcasted_iota(jnp.int32, (8, LANES), 0)
        lane8 = lax.broadcasted_iota(jnp.int32, (8, LANES), 1)
        dsink = jnp.zeros((8, LANES), F32)
        lse_blk = lse_ref[...]
        rows = pl.ds(start, 3 * BLOCK)
        lses, deltas = {}, {}
        for c in range(Q_W // LANES):
            prod = do_ref[:, c * LANES:(c + 1) * LANES].astype(F32) * ao_ref[:, c * LANES:(c + 1) * LANES].astype(F32)
            for k in range(2):
                h = 2 * c + k
                deltas[h] = jnp.sum(jnp.where(lo if k == 0 else jnp.logical_not(lo), prod, 0.0), axis=1, keepdims=True)
                lses[h] = jnp.sum(jnp.where(lane == h, lse_blk, 0.0), axis=1, keepdims=True)
                val = -jnp.sum(jnp.exp(sink_ref[h] - lses[h]) * deltas[h], axis=0, keepdims=True)
                dsink = dsink + jnp.where((row8 == 0) & (lane8 == h), val, 0.0)
        stack = lambda ref, g: jnp.concatenate([_head_operand(ref, GROUP * g + j, lo) for j in range(GROUP)], axis=0)
        ss, dps = [], []
        for g in range(ATTN_KV_HEADS):
            ss.append(lax.dot_general(stack(q_ref, g), k_ref[rows, g * LANES:(g + 1) * LANES], NT, preferred_element_type=F32))
            dps.append(lax.dot_general(stack(do_ref, g), v_ref[rows, g * LANES:(g + 1) * LANES], NT, preferred_element_type=F32))
        for g in range(ATTN_KV_HEADS):
            for j in range(GROUP):
                h = GROUP * g + j
                r = slice(j * BLOCK, (j + 1) * BLOCK)
                hr = slice(h * BLOCK, (h + 1) * BLOCK)
                p = jnp.exp(jnp.where(mask, ss[g][r], NEG) - lses[h])
                p_scr[hr, :] = p.astype(BF16)
                dsb_scr[hr, :] = (p * (dps[g][r] - deltas[h])).astype(BF16)
        for g in range(ATTN_KV_HEADS):
            cols = slice((g // 2) * LANES, (g // 2 + 1) * LANES)
            gr = slice(GROUP * g * BLOCK, GROUP * (g + 1) * BLOCK)
            dsg = dsb_scr[gr, :]
            dqs = jnp.dot(dsg, k_ref[rows, g * LANES:(g + 1) * LANES], preferred_element_type=F32) * scale
            for jj in range(2):
                dq_ref[:, (2 * g + jj) * LANES:(2 * g + jj + 1) * LANES] = jnp.where(
                    lo, dqs[(2 * jj) * BLOCK:(2 * jj + 1) * BLOCK], dqs[(2 * jj + 1) * BLOCK:(2 * jj + 2) * BLOCK])
            half = (lane3 < HEAD_DIM) if g % 2 == 0 else (lane3 >= HEAD_DIM)
            dkr = lax.dot_general(dsg, stack(q_ref, g), TN, preferred_element_type=F32)
            dk_ref[rows, cols] += jnp.where(half, dkr + pltpu.roll(dkr, HEAD_DIM, 1), 0.0)
            dvr = lax.dot_general(p_scr[gr, :], stack(do_ref, g), TN, preferred_element_type=F32)
            dv_ref[rows, cols] += jnp.where(half, dvr + pltpu.roll(dvr, HEAD_DIM, 1), 0.0)
        ds_ref[...] += dsink

    full = lambda w: pl.BlockSpec((S, w), lambda i: (0, 0))
    blk = lambda w: pl.BlockSpec((BLOCK, w), lambda i: (i, 0))
    return pl.pallas_call(
        body, grid=(S // BLOCK,),
        in_specs=[pl.BlockSpec(memory_space=pltpu.SMEM), blk(Q_W), full(2 * KV_W), full(2 * KV_W), blk(Q_W), blk(LANES), blk(Q_W)]
        + host.in_specs,
        out_specs=[blk(Q_W), full(KV_W), full(KV_W), pl.BlockSpec((8, LANES), lambda i: (0, 0))] + host.out_specs,
        out_shape=[jax.ShapeDtypeStruct((S, Q_W), F32), jax.ShapeDtypeStruct((S, KV_W), F32),
                   jax.ShapeDtypeStruct((S, KV_W), F32), jax.ShapeDtypeStruct((8, LANES), F32)] + host.out_shape,
        scratch_shapes=[pltpu.VMEM((ATTN_HEADS * BLOCK, 3 * BLOCK), BF16), pltpu.VMEM((ATTN_HEADS * BLOCK, 3 * BLOCK), BF16)]
        + host.scratch, input_output_aliases=host.alias,
        name=name, compiler_params=_cp(("arbitrary",)))(sinks, q, kd, vd, ao, lse, dcat, *host.args)


def _mem_probs(q_ref, kv_ref, h):
    scale = MEM_HEAD_DIM ** -0.5
    qh = q_ref[:, h * LANES:(h + 1) * LANES].astype(BF16)
    s = lax.dot_general(qh, kv_ref[:, h * LANES:(h + 1) * LANES], NT, preferred_element_type=F32) * scale
    m = jnp.max(s, axis=-1, keepdims=True)
    pe = jnp.exp(s - m)
    return qh, pe * (1.0 / jnp.sum(pe, axis=-1, keepdims=True))


def _memattn_fwd(p, qblk, kv, cat, *, name="memattn_fwd", tr=512):
    S = p.shape[0]
    tr = min(tr, S)

    def body(q_ref, kv_ref, cat_ref, o_ref):
        for h in range(MEM_HEADS):
            _, pr = _mem_probs(q_ref, kv_ref, h)
            o = jnp.dot(pr.astype(BF16), kv_ref[:, MEM_W + h * LANES:MEM_W + (h + 1) * LANES], preferred_element_type=F32)
            o_ref[:, h * LANES:(h + 1) * LANES] = o.astype(BF16)

    return pl.pallas_call(
        body, grid=(S // tr,),
        in_specs=[pl.BlockSpec((tr, MEM_W), lambda i: (i, qblk)), pl.BlockSpec((MEM_LEN, 2 * MEM_W), lambda i: (0, 0)), HBM],
        out_specs=pl.BlockSpec((tr, MEM_W), lambda i: (i, Q_W // MEM_W)),
        out_shape=jax.ShapeDtypeStruct((S, MIX_OUT_W), BF16), input_output_aliases={2: 0},
        name=name, compiler_params=_cp(("parallel",)))(p, kv, cat)


def _memattn_bwd(p, qblk, kv, dcat, *, name="memattn_bwd", tr=512, rider=None):
    S = p.shape[0]
    tr = min(tr, S)
    scale = MEM_HEAD_DIM ** -0.5
    host = _Hosted(rider, 3, 2)

    def body(*refs):
        ins, outs, _, rrefs = host.split(refs, 3, 2)
        host.run(rrefs, pl.program_id(0), S // tr, lambda: inner(*ins, *outs))

    def inner(q_ref, kv_ref, do_ref, dq_ref, dkv_ref):
        @pl.when(pl.program_id(0) == 0)
        def _():
            dkv_ref[...] = jnp.zeros_like(dkv_ref)

        for h in range(MEM_HEADS):
            qh, pr = _mem_probs(q_ref, kv_ref, h)
            doh = do_ref[:, h * LANES:(h + 1) * LANES]
            dp = lax.dot_general(doh, kv_ref[:, MEM_W + h * LANES:MEM_W + (h + 1) * LANES], NT, preferred_element_type=F32)
            delta = jnp.sum(pr * dp, axis=-1, keepdims=True)
            dsb = (pr * (dp - delta) * scale).astype(BF16)
            dq = jnp.dot(dsb, kv_ref[:, h * LANES:(h + 1) * LANES], preferred_element_type=F32)
            dq_ref[:, h * LANES:(h + 1) * LANES] = dq.astype(BF16)
            dkv_ref[:, h * LANES:(h + 1) * LANES] += lax.dot_general(dsb, qh, TN, preferred_element_type=F32)
            dkv_ref[:, MEM_W + h * LANES:MEM_W + (h + 1) * LANES] += lax.dot_general(
                pr.astype(BF16), doh, TN, preferred_element_type=F32)

    return pl.pallas_call(
        body, grid=(S // tr,),
        in_specs=[pl.BlockSpec((tr, MEM_W), lambda i: (i, qblk)), pl.BlockSpec((MEM_LEN, 2 * MEM_W), lambda i: (0, 0)),
                  pl.BlockSpec((tr, MEM_W), lambda i: (i, Q_W // MEM_W))] + host.in_specs,
        out_specs=[pl.BlockSpec((tr, MEM_W), lambda i: (i, 0)), pl.BlockSpec((MEM_LEN, 2 * MEM_W), lambda i: (0, 0))]
        + host.out_specs,
        out_shape=[jax.ShapeDtypeStruct((S, MEM_W), BF16), jax.ShapeDtypeStruct((MEM_LEN, 2 * MEM_W), F32)] + host.out_shape,
        scratch_shapes=host.scratch, input_output_aliases=host.alias,
        name=name, compiler_params=_cp(("arbitrary",)))(p, kv, dcat, *host.args)


def _sqrt(v):
    return jnp.where(v > 0.0, v * lax.rsqrt(v), 0.0)


def _sigmoid(z):
    return 1.0 / (1.0 + jnp.exp(-z))


def _one_minus_exp(z, exp_z):
    poly = z * (1.0 + z * (0.5 + z * (1.0 / 6.0 + z * (1.0 / 24.0 + z * (1.0 / 120.0)))))
    return jnp.where(z > -0.1, -poly, 1.0 - exp_z)


def _softplus_neg(lam):
    z = -lam
    return jnp.maximum(z, 0.0) + jnp.log(1.0 + jnp.exp(-jnp.abs(z)))


_GELU_C = math.sqrt(2.0 / math.pi)


def _gelu(z):
    return 0.5 * z * (1.0 + jnp.tanh(_GELU_C * (z + 0.044715 * z * z * z)))


def _row_or_zero(ref, t, S):
    ok = jnp.logical_and(t >= 0, t < S)
    return jnp.where(ok, ref[pl.ds(jnp.clip(t, 0, S - 1), 1), :], 0.0)


def _shift_down(v, first):
    ri = lax.broadcasted_iota(jnp.int32, v.shape, 0)
    return jnp.where(ri == 0, first, pltpu.roll(v, 1, 0))


def _shift_up(v, last):
    T = v.shape[0]
    ri = lax.broadcasted_iota(jnp.int32, v.shape, 0)
    return jnp.where(ri == T - 1, last, pltpu.roll(v, T - 1, 0))


def _scan_chunk(a, u, reverse):
    T = a.shape[0]
    ri = lax.broadcasted_iota(jnp.int32, a.shape, 0)
    d = 1
    while d < T:
        if reverse:
            a_s, u_s, ok = pltpu.roll(a, T - d, 0), pltpu.roll(u, T - d, 0), ri < T - d
        else:
            a_s, u_s, ok = pltpu.roll(a, d, 0), pltpu.roll(u, d, 0), ri >= d
        u = jnp.where(ok, a * u_s + u, u)
        a = jnp.where(ok, a * a_s, a)
        d *= 2
    return a, u


def _conv_taps(xb_ref, t0, S):
    T = SCAN_ROWS
    x0 = xb_ref[pl.ds(t0, T), :]
    xm1 = _shift_down(x0, _row_or_zero(xb_ref, t0 - 1, S))
    nxt0 = _row_or_zero(xb_ref, t0 + T, S)
    xp1 = _shift_up(x0, nxt0)
    xp2 = _shift_up(xp1, _row_or_zero(xb_ref, t0 + T + 1, S))
    return xm1, x0, xp1, xp2


def _lru_gates(xc, w_a, b_a, w_x, b_x, sp):
    xcb = xc.astype(BF16)
    r = _sigmoid(jnp.dot(xcb, w_a, preferred_element_type=F32) + b_a)
    i = _sigmoid(jnp.dot(xcb, w_x, preferred_element_type=F32) + b_x)
    la = -LRU_C * r * sp
    a = jnp.exp(la)
    b2 = _one_minus_exp(2.0 * la, a * a)
    inv_beta = lax.rsqrt(b2)
    return r, i, a, jnp.where(b2 > 0.0, b2 * inv_beta, 0.0), inv_beta


def _lru_specs(S):
    col = lambda off: pl.BlockSpec((S, LANES), lambda n: (0, n + off), pipeline_mode=pl.Buffered(1))
    small = lambda r: pl.BlockSpec((r, LANES), lambda n: (0, n))
    wblk = pl.BlockSpec((2, 1, LANES, LANES), lambda n: (0, n, 0, 0))
    return col, small, wblk


def _lru_fwd(p, conv_w, conv_b, wa, ba, wx, bx, lam, *, name="lru_fwd"):
    S = p.shape[0]
    T = SCAN_ROWS
    nc = S // T

    def body(xb_ref, gate_ref, cw_ref, cb_ref, wa_ref, ba_ref, wx_ref, bx_ref, lam_ref, y_ref, hf_ref, hr_ref, xc_v):
        sp = _softplus_neg(lam_ref[...])
        cw = cw_ref[...]

        def fwd_step(c, h_in):
            t0 = pl.multiple_of(c * T, T)
            xm1, x0, xp1, xp2 = _conv_taps(xb_ref, t0, S)
            xc = cb_ref[...] + xm1 * cw[0:1] + x0 * cw[1:2] + xp1 * cw[2:3] + xp2 * cw[3:4]
            xc_v[pl.ds(t0, T), :] = xc
            _, i, a, beta, _ = _lru_gates(xc, wa_ref[0, 0], ba_ref[0:1], wx_ref[0, 0], bx_ref[0:1], sp[0:1])
            A, U = _scan_chunk(a, beta * (i * xc), False)
            hf_ref[pl.ds(t0, T), :] = A * h_in + U
            return hf_ref[pl.ds(t0 + T - 1, 1), :]

        lax.fori_loop(0, nc, fwd_step, jnp.zeros((1, LANES), F32))

        def rev_step(k, h_in):
            t0 = pl.multiple_of((nc - 1 - k) * T, T)
            xc = xc_v[pl.ds(t0, T), :]
            _, i, a, beta, _ = _lru_gates(xc, wa_ref[1, 0], ba_ref[1:2], wx_ref[1, 0], bx_ref[1:2], sp[1:2])
            A, U = _scan_chunk(a, beta * (i * xc), True)
            h = A * h_in + U
            hr_ref[pl.ds(t0, T), :] = h
            y_ref[pl.ds(t0, T), :] = ((hf_ref[pl.ds(t0, T), :] + h) * _gelu(gate_ref[pl.ds(t0, T), :])).astype(BF16)
            return hr_ref[pl.ds(t0, 1), :]

        lax.fori_loop(0, nc, rev_step, jnp.zeros((1, LANES), F32))

    col, small, wblk = _lru_specs(S)
    colo = lambda: pl.BlockSpec((S, LANES), lambda n: (0, n))
    return pl.pallas_call(
        body, grid=(LRU_BLOCKS,),
        in_specs=[col(0), col(LRU_BLOCKS), small(4), small(1), wblk, small(2), wblk, small(2), small(2)],
        out_specs=[colo(), colo(), colo()],
        out_shape=[jax.ShapeDtypeStruct((S, MIX_OUT_W), BF16), jax.ShapeDtypeStruct((S, D_MODEL), F32),
                   jax.ShapeDtypeStruct((S, D_MODEL), F32)],
        scratch_shapes=[pltpu.VMEM((S, LANES), F32)],
        name=name, compiler_params=_cp(("parallel",)))(p, p, conv_w, conv_b, wa, ba, wx, bx, lam)


def _lru_bwd(p, hf, hr, dcat, conv_w, conv_b, wa, ba, wx, bx, lam, *, name="lru_bwd"):
    S = p.shape[0]
    T = SCAN_ROWS
    nc = S // T

    def body(xb_ref, gate_ref, hf_ref, hr_ref, dy_ref, cw_ref, cb_ref, wa_ref, ba_ref, wx_ref, bx_ref, lam_ref,
             dxb_ref, dgate_ref, dcw_ref, dcb_ref, dwa_ref, dba_ref, dwx_ref, dbx_ref, dlam_ref, xc_v, dxc_v, dh_v):
        lam_v = lam_ref[...]
        sp = _softplus_neg(lam_v)
        cw = cw_ref[...]
        for ref in (dcw_ref, dcb_ref, dwa_ref, dba_ref, dwx_ref, dbx_ref, dlam_ref):
            ref[...] = jnp.zeros_like(ref)

        def prep_step(c, carry):
            t0 = pl.multiple_of(c * T, T)
            rows = pl.ds(t0, T)
            xm1, x0, xp1, xp2 = _conv_taps(xb_ref, t0, S)
            xc_v[rows, :] = cb_ref[...] + xm1 * cw[0:1] + x0 * cw[1:2] + xp1 * cw[2:3] + xp2 * cw[3:4]
            z = gate_ref[rows, :]
            dy = dy_ref[rows, :].astype(F32)
            th = jnp.tanh(_GELU_C * (z + 0.044715 * z * z * z))
            dgelu = 0.5 * (1.0 + th) + 0.5 * z * (1.0 - th * th) * _GELU_C * (1.0 + 3.0 * 0.044715 * z * z)
            dgate_ref[rows, :] = (dy * (hf_ref[rows, :] + hr_ref[rows, :]) * dgelu).astype(BF16)
            dh_v[rows, :] = dy * (0.5 * z * (1.0 + th))
            return carry

        lax.fori_loop(0, nc, prep_step, 0)

        def direction(d):
            h_ref = hf_ref if d == 0 else hr_ref
            w_a, w_x = wa_ref[d, 0], wx_ref[d, 0]
            b_a, b_x, sp_d = ba_ref[d:d + 1], bx_ref[d:d + 1], sp[d:d + 1]

            def step(k, carry):
                g_in, a_in = carry
                c = (nc - 1 - k) if d == 0 else k
                t0 = pl.multiple_of(c * T, T)
                rows = pl.ds(t0, T)
                xc = xc_v[rows, :]
                r, i, a, beta, inv_beta = _lru_gates(xc, w_a, b_a, w_x, b_x, sp_d)
                dh = dh_v[rows, :]
                hc = h_ref[rows, :]
                if d == 0:
                    A, U = _scan_chunk(_shift_up(a, a_in), dh, True)
                    g = A * g_in + U
                    h_nb = _shift_down(hc, _row_or_zero(h_ref, t0 - 1, S))
                    nxt = (g[0:1], a[0:1])
                else:
                    A, U = _scan_chunk(_shift_down(a, a_in), dh, False)
                    g = A * g_in + U
                    h_nb = _shift_up(hc, _row_or_zero(h_ref, t0 + T, S))
                    nxt = (g[T - 1:T], a[T - 1:T])
                da = g * h_nb
                dbeta = g * (i * xc)
                tb = g * beta
                dla = da * a - dbeta * (a * a * inv_beta)
                dzr = (dla * (-LRU_C * sp_d)) * (r * (1.0 - r))
                dzi = (tb * xc) * (i * (1.0 - i))
                dzrb, dzib, xcb = dzr.astype(BF16), dzi.astype(BF16), xc.astype(BF16)
                dwa_ref[d, 0] += lax.dot_general(xcb, dzrb, TN, preferred_element_type=F32)
                dwx_ref[d, 0] += lax.dot_general(xcb, dzib, TN, preferred_element_type=F32)
                dba_ref[d:d + 1] += jnp.sum(dzr, axis=0, keepdims=True)
                dbx_ref[d:d + 1] += jnp.sum(dzi, axis=0, keepdims=True)
                dlam_ref[d:d + 1] += jnp.sum(dla * (-LRU_C * r), axis=0, keepdims=True)
                dxc = (tb * i + lax.dot_general(dzrb, w_a, NT, preferred_element_type=F32)
                       + lax.dot_general(dzib, w_x, NT, preferred_element_type=F32))
                if d == 0:
                    dxc_v[rows, :] = dxc
                else:
                    dxc_v[rows, :] += dxc
                return nxt

            lax.fori_loop(0, nc, step, (jnp.zeros((1, LANES), F32), jnp.zeros((1, LANES), F32)))

        direction(0)
        direction(1)
        dlam_ref[...] = dlam_ref[...] * (-1.0 / (1.0 + jnp.exp(lam_v)))

        def conv_step(c, carry):
            t0 = pl.multiple_of(c * T, T)
            rows = pl.ds(t0, T)
            g0 = dxc_v[rows, :]
            gm1 = _shift_down(g0, _row_or_zero(dxc_v, t0 - 1, S))
            gm2 = _shift_down(gm1, _row_or_zero(dxc_v, t0 - 2, S))
            gp1 = _shift_up(g0, _row_or_zero(dxc_v, t0 + T, S))
            dxb_ref[rows, :] = (cw[0:1] * gp1 + cw[1:2] * g0 + cw[2:3] * gm1 + cw[3:4] * gm2).astype(BF16)
            xm1, x0, xp1, xp2 = _conv_taps(xb_ref, t0, S)
            for tap, xs in enumerate((xm1, x0, xp1, xp2)):
                dcw_ref[tap:tap + 1] += jnp.sum(g0 * xs, axis=0, keepdims=True)
            dcb_ref[...] += jnp.sum(g0, axis=0, keepdims=True)
            return carry

        lax.fori_loop(0, nc, conv_step, 0)

    col, small, wblk = _lru_specs(S)
    colo = lambda: pl.BlockSpec((S, LANES), lambda n: (0, n), pipeline_mode=pl.Buffered(1))
    return pl.pallas_call(
        body, grid=(LRU_BLOCKS,),
        in_specs=[col(0), col(LRU_BLOCKS), col(0), col(0), col(0), small(4), small(1), wblk, small(2), wblk, small(2), small(2)],
        out_specs=[colo(), colo(), small(4), small(1), wblk, small(2), wblk, small(2), small(2)],
        out_shape=[jax.ShapeDtypeStruct((S, D_MODEL), BF16), jax.ShapeDtypeStruct((S, D_MODEL), BF16),
                   jax.ShapeDtypeStruct((4, D_MODEL), F32), jax.ShapeDtypeStruct((1, D_MODEL), F32),
                   jax.ShapeDtypeStruct((2, LRU_BLOCKS, LANES, LANES), F32), jax.ShapeDtypeStruct((2, D_MODEL), F32),
                   jax.ShapeDtypeStruct((2, LRU_BLOCKS, LANES, LANES), F32), jax.ShapeDtypeStruct((2, D_MODEL), F32),
                   jax.ShapeDtypeStruct((2, D_MODEL), F32)],
        scratch_shapes=[pltpu.VMEM((S, LANES), F32), pltpu.VMEM((S, LANES), F32), pltpu.VMEM((S, LANES), F32)],
        name=name, compiler_params=_cp(("parallel",)))(p, p, hf, hr, dcat, conv_w, conv_b, wa, ba, wx, bx, lam)


PK_UP, PK_DOWN, PK_KV, PK_OUT, PK_IN = 0, 1024, 2048, 2304, 2688
PK_ROWS = {0: PK_IN, 1: PK_IN + 640}
SMALL_G_ROWS = 192
PKF_KV, PKF_SMALL = 512, 768
PKF_ROWS = PKF_SMALL + SMALL_G_ROWS


def _mlp_bwd(x, dx, dxb, saved, w_up, w_down, gain, l, rider=None, next_rider=None):
    up, act, h = saved
    pk = _mm_tn(act, dxb, 1, name=f"dw_down{l}", packed=(None, PK_ROWS[l], PK_DOWN), rider=rider)
    pk, carried = pk if rider is not None else (pk, None)
    dup = _mm_nt(dxb, w_down, up=up, name=f"d_up{l}")
    rider_up = next_rider(carried) if next_rider is not None else None
    pk = _mm_tn(h, dup, N_CHIPS, name=f"dw_up{l}", packed=(pk, PK_ROWS[l], PK_UP), rider=rider_up)
    pk, carried = pk if rider_up is not None else (pk, carried)
    dx, dxb, g_gain = _mm_nt(dup, w_up, norm_x=x, norm_g=gain, dres=dx, name=f"d_mlp_in{l}")
    return dx, dxb, pk, g_gain, carried


def _reduce_first(pk, place, tag, recv=None):
    if recv is None:
        recv = _sibling_exchange(pk, name=f"grad_sibling_exchange{tag}")
    return _sum_halves(pk, recv, place, name=f"sum_halves{tag}", tr=pk.shape[1] // 4)


def _sum_parts(parts, place, tag):
    return _sum_chips(parts, place, name=f"sum_chips{tag}", tr=parts.shape[1] // 2)


def _reduce_last(parts, place, tag):
    return _sibling_allgather(_sum_parts(parts, place, tag), name=f"grad_sibling_allgather{tag}")


def _local_step(x, mem, positions, target, W, pending=None, place=None):
    cos_t, sin_t = _rope_tables(positions)
    sinks = W["attn_sinks"].reshape(ATTN_HEADS)
    G = {}

    def hosting(late, fn, *args, **kw):
        if pending is None:
            return fn(*args, **kw)
        *res, buf = fn(*args, gather=pending[late], **kw)
        if late.startswith("w_down"):
            W.setdefault("w_down", [None] * DEPTH)[int(late[-1])] = _ready(late, buf)
        elif late == "w_out":
            W["w_out"], W["w_mem_kv"] = _ready(late, buf)
        else:
            W[late] = _ready(late, buf)
        return res if len(res) > 1 else res[0]

    p0, h0 = hosting("w_out", _mm_nn, x, W["attn_w_in"], norm_g=W["mix_norm"][0], name="attn_in")
    kv0, memn = _mm_nn(mem, W["w_mem_kv"][0], norm_g=W["mem_norm"], out_dtype=BF16, name="mem_kv0", tm=256)
    kv1 = _mm_nn(memn, W["w_mem_kv"][1], out_dtype=BF16, name="mem_kv1", tm=256)
    q, kd, vd, va = hosting("lru_w_in", _qk_prep, p0, cos_t, sin_t)
    ao, lse = hosting("w_up", _attn_fwd, q, kd, va, sinks)
    cat0 = _memattn_fwd(p0, Q_W // MEM_W + 1, kv0, ao, name="memattn_fwd0")
    x1 = hosting("w_down0", _mm_nn, cat0, W["w_out"][0], resid=x, name="mix_out0")
    up0, act0, h1 = hosting("w_down1", _mm_nn, x1, W["w_up"][0], norm_g=W["mlp_norm"][0], relu2=True, name="mlp_up0")
    x2, mlp0 = _mm_nn(act0, W["w_down"][0], resid=x1, name="mlp_down0"), (up0, act0, h1)
    p1, h2 = _mm_nn(x2, W["lru_w_in"], norm_g=W["mix_norm"][1], name="lru_in")
    lru_w = (W["lru_conv_w"], W["lru_conv_b"], W["lru_wa"], W["lru_ba"], W["lru_wx"], W["lru_bx"], W["lru_lambda"])
    y, hf, hr = _lru_fwd(p1, *lru_w)
    cat1 = _memattn_fwd(p1, 2 * D_MODEL // MEM_W, kv1, y, name="memattn_fwd1")
    x3 = _mm_nn(cat1, W["w_out"][1], resid=x2, name="mix_out1")
    mlp1 = _mm_nn(x3, W["w_up"][1], norm_g=W["mlp_norm"][1], relu2=True, name="mlp_up1")
    loss, dx, dxb, G["final_norm"] = _final(mlp1[1], W["w_down"][1], x3, W["final_norm"], target)

    def put(pk, off, g):
        return pk.at[:, off:off + g.size // (N_CHIPS * ROW)].set(g.reshape(N_CHIPS, -1, ROW))

    dx, dxb, pk1, gm1, _ = _mlp_bwd(x3, dx, dxb, mlp1, W["w_up"][1], W["w_down"][1], W["mlp_norm"][1], 1)
    pk1 = _mm_tn(cat1, dxb, 1, name="dw_out1", tk=384, packed=(pk1, PK_ROWS[1], PK_OUT))
    dcat1 = _mm_nt(dxb, W["w_out"][1], name="d_mix1")
    dmq1, dkv1 = _memattn_bwd(p1, 2 * D_MODEL // MEM_W, kv1, dcat1, name="memattn_bwd1")
    dkv1b = dkv1.astype(BF16)
    pk1 = _mm_tn(memn, dkv1b, 1, name="dw_kv1", tm=256, tk=256, packed=(pk1, PK_ROWS[1], PK_KV))
    (dxb1, dgate, G["lru_conv_w"], G["lru_conv_b"], G["lru_wa"], G["lru_ba"], G["lru_wx"], G["lru_bx"],
     G["lru_lambda"]) = _lru_bwd(p1, hf, hr, dcat1, *lru_w)
    dp1 = jnp.concatenate([dxb1, dgate, dmq1], axis=1)
    pk1 = put(pk1, PK_IN, _mm_tn(h2, dp1, N_CHIPS, name="dw_lru_in"))
    dx, dxb, gx1 = _mm_nt(dp1, W["lru_w_in"], norm_x=x2, norm_g=W["mix_norm"][1], dres=dx, name="d_lru_in")
    dist = place is not None
    h1_rows = PK_ROWS[1] // 4
    kept = {}

    def first_half(recv1):
        kept["halves1"], landing1 = _reduce_first(pk1, place, "1", recv1)
        return _exchange_rider((kept["halves1"], landing1, 0, h1_rows))

    dx, dxb, pk0, gm0, landing1 = _mlp_bwd(x1, dx, dxb, mlp0, W["w_up"][0], W["w_down"][0], W["mlp_norm"][0], 0,
                                           rider=_sib_exchange_rider(pk1) if dist else None,
                                           next_rider=first_half if dist else None)
    pk0 = _mm_tn(cat0, dxb, 1, name="dw_out0", tk=384, packed=(pk0, PK_ROWS[0], PK_OUT))
    pk0 = pk0.at[:, PK_KV:PK_OUT].set(0.0)
    dcat0 = _mm_nt(dxb, W["w_out"][0], name="d_mix0")
    dmq0, dkv0, *recv0 = _memattn_bwd(p0, Q_W // MEM_W + 1, kv0, dcat0, name="memattn_bwd0",
                                      rider=_sib_exchange_rider(pk0) if dist else None)
    dkv0b = dkv0.astype(BF16)
    g_kv0 = _mm_tn(memn, dkv0b, 1, name="dw_kv0", tm=256, tk=256)
    rider = None
    if dist:
        halves0, landing0 = _reduce_first(pk0, place, "0", recv0[0])
        rider = _exchange_rider((kept["halves1"], landing1, h1_rows, h1_rows), (halves0, landing0, 0, halves0.shape[1]))
    dq, dk, dv, dsink, *parts = _attn_bwd(q, kd, vd, cat0, lse, sinks, dcat0, rider=rider)
    dp0 = _qk_prep_bwd(dq, dk, dv, dmq0, cos_t, sin_t)
    g_in = _mm_tn(h0, dp0, N_CHIPS, name="dw_attn_in",
                  rider=_sib_allgather_rider(_sum_parts(parts[0], place, "1"), _sum_parts(parts[1], place, "0")) if dist else None)
    if dist:
        g_in, pk1, pk0 = g_in
    dx, _, gx0 = _mm_nt(dp0, W["attn_w_in"], norm_x=x, norm_g=W["mix_norm"][0], dres=dx, name="d_attn_in")

    w_kv_both = jnp.concatenate([W["w_mem_kv"][0], W["w_mem_kv"][1]], axis=0)
    _, _, G["mem_norm"] = _mm_nt(jnp.concatenate([dkv0b, dkv1b], axis=1), w_kv_both, norm_x=mem, norm_g=W["mem_norm"],
                                 name="d_mem", tm=256)

    G["mix_norm"] = jnp.concatenate([gx0, gx1], axis=0)
    G["mlp_norm"] = jnp.concatenate([gm0, gm1], axis=0)
    G["attn_sinks"] = dsink[0:1, 0:ATTN_HEADS]
    small = _flat_pad(_small_grad_list(G), N_CHIPS * SMALL_G_ROWS * ROW).reshape(N_CHIPS, SMALL_G_ROWS, ROW)
    pkf = jnp.concatenate([g_in.reshape(N_CHIPS, PKF_KV, ROW), g_kv0.reshape(N_CHIPS, PKF_SMALL - PKF_KV, ROW), small], axis=1)
    return loss[0, 0], dx, G, pkf, pk0, pk1


def _comm_call(body, out_shape, n_sems, name, *args, alias=None):
    return pl.pallas_call(
        body, out_shape=out_shape, in_specs=[HBM] * len(args), out_specs=HBM,
        scratch_shapes=[pltpu.SemaphoreType.DMA((n_sems,)), pltpu.SemaphoreType.DMA((n_sems,))],
        input_output_aliases=alias or {}, name=name)(*args)


def _place_slot(shard, slot, n_slots, *, name, tr):
    R, C = shard.shape

    def body(s_ref, a_ref, o_ref):
        o_ref[0] = a_ref[...]

    return pl.pallas_call(
        body,
        grid_spec=pltpu.PrefetchScalarGridSpec(
            num_scalar_prefetch=1, grid=(R // tr,), in_specs=[pl.BlockSpec((tr, C), lambda i, s_ref: (i, 0))],
            out_specs=pl.BlockSpec((1, tr, C), lambda i, s_ref: (s_ref[0], i, 0))),
        out_shape=jax.ShapeDtypeStruct((n_slots, R, C), shard.dtype), name=name,
        compiler_params=_cp(("parallel",)))(slot, shard)


def _allgather_chips(buf, *, name, forward_to_sibling):
    def body(b_ref, o_ref, send_sems, recv_sems):
        if forward_to_sibling:
            _gather_start(o_ref, send_sems, recv_sems)
            _gather_finish(o_ref, send_sems, recv_sems)
            return
        x, y, c, chips = _place()
        own = o_ref.at[2 * x + y]
        sends = [_remote(own, own, send_sems, recv_sems, j, (cx, cy, c)) for j, (cx, cy) in enumerate(chips)]
        for cp in sends:
            cp.start()
        for j, (cx, cy) in enumerate(chips):
            landed = o_ref.at[2 * cx + cy]
            _remote(landed, landed, send_sems, recv_sems, j, (cx, cy, c)).wait_recv()
        for cp in sends:
            cp.wait_send()

    return _comm_call(body, jax.ShapeDtypeStruct(buf.shape, buf.dtype), GATHER_SEMS, name, buf, alias={0: 0})


def _sibling_exchange(g, *, name):
    _, R, C = g.shape
    half = R // 2

    def body(g_ref, o_ref, send_sems, recv_sems):
        _sib_exchange_start(g_ref, o_ref, send_sems, recv_sems)
        _sib_exchange_finish(g_ref, o_ref, send_sems, recv_sems)

    return _comm_call(body, jax.ShapeDtypeStruct((N_CHIPS, half, C), g.dtype), N_CHIPS, name, g)


def _chip_exchange(h, parts, *, name):
    def body(h_ref, p_ref, o_ref, send_sems, recv_sems):
        x, y, c, chips = _place()
        me = 2 * x + y
        cps = [_remote(h_ref.at[2 * cx + cy], o_ref.at[me], send_sems, recv_sems, j, (cx, cy, c))
               for j, (cx, cy) in enumerate(chips)]
        for cp in cps:
            cp.start()
        for j, (cx, cy) in enumerate(chips):
            got = o_ref.at[2 * cx + cy]
            _remote(got, got, send_sems, recv_sems, j, (cx, cy, c)).wait_recv()
        for cp in cps:
            cp.wait_send()

    return _comm_call(body, jax.ShapeDtypeStruct(parts.shape, parts.dtype), 3, name, h, parts, alias={1: 0})


def _sibling_allgather(full, *, name):
    def body(f_ref, o_ref, send_sems, recv_sems):
        _sib_allgather_start(o_ref, send_sems, recv_sems)
        _sib_allgather_finish(o_ref, send_sems, recv_sems)

    return _comm_call(body, jax.ShapeDtypeStruct(full.shape, full.dtype), 1, name, full, alias={0: 0})


def _sum_halves(g, recv, place, *, name="sum_halves", tr=480):
    _, R, C = g.shape
    half = R // 2
    nblk = half // tr

    def body(pl_ref, g_ref, r_ref, o_ref, own_ref):
        v = (g_ref[...] + r_ref[...]).astype(BF16)
        o_ref[...] = v

        @pl.when(pl.program_id(1) == pl_ref[1])
        def _():
            own_ref[...] = v

    blk = pl.BlockSpec((1, tr, C), lambda i, s, p: (s, i, 0))
    return pl.pallas_call(
        body,
        grid_spec=pltpu.PrefetchScalarGridSpec(
            num_scalar_prefetch=1, grid=(nblk, N_CHIPS),
            in_specs=[pl.BlockSpec((1, tr, C), lambda i, s, p: (s, p[0] * nblk + i, 0)), blk],
            out_specs=[blk, pl.BlockSpec((1, tr, C), lambda i, s, p: (p[1], i, 0))]),
        out_shape=[jax.ShapeDtypeStruct((N_CHIPS, half, C), BF16)] * 2, name=name,
        compiler_params=_cp(("parallel", "arbitrary")))(place, g, recv)


def _sum_chips(parts, place, *, name="sum_chips", tr=480):
    _, R, C = parts.shape
    nblk = R // tr

    def body(pl_ref, p_ref, o_ref):
        acc = p_ref[0].astype(F32) + p_ref[1].astype(F32)
        o_ref[...] = (acc + p_ref[2].astype(F32)) + p_ref[3].astype(F32)

    return pl.pallas_call(
        body,
        grid_spec=pltpu.PrefetchScalarGridSpec(
            num_scalar_prefetch=1, grid=(nblk,), in_specs=[pl.BlockSpec((N_CHIPS, tr, C), lambda i, p: (0, i, 0))],
            out_specs=pl.BlockSpec((tr, C), lambda i, p: (p[0] * nblk + i, 0))),
        out_shape=jax.ShapeDtypeStruct((2 * R, C), F32), name=name, compiler_params=_cp(("parallel",)))(place, parts)


def _adamw(w, g, m, v, *, name, tr=128):
    R, C = w.shape
    bc1 = 1.0 - ADAM_B1 ** ADAM_STEP
    bc2 = 1.0 - ADAM_B2 ** ADAM_STEP

    def body(w_ref, g_ref, m_ref, v_ref, d_ref, nm_ref, nv_ref):
        gv = g_ref[...]
        nm = ADAM_B1 * m_ref[...] + (1.0 - ADAM_B1) * gv
        nv = ADAM_B2 * v_ref[...] + (1.0 - ADAM_B2) * (gv * gv)
        d_ref[...] = -ADAM_LR * ((nm / bc1) / (_sqrt(nv / bc2) + ADAM_EPS) + ADAM_WD * w_ref[...])
        nm_ref[...] = nm
        nv_ref[...] = nv

    blk = pl.BlockSpec((tr, C), lambda i: (i, 0))
    return pl.pallas_call(
        body, grid=(R // tr,), in_specs=[blk] * 4, out_specs=[blk] * 3,
        out_shape=[jax.ShapeDtypeStruct((R, C), F32)] * 3, name=name, compiler_params=_cp(("parallel",)))(w, g, m, v)


ROW = 1024
BIG = ("w_mem_kv", "w_out", "w_up", "w_down", "attn_w_in", "lru_w_in")
SMALL_SHARDED = ("lru_conv_w", "lru_conv_b", "lru_ba", "lru_bx", "lru_lambda")
REPLICATED = ("mix_norm", "mlp_norm", "mem_norm", "final_norm", "attn_sinks", "lru_wa", "lru_wx")
SMALL = REPLICATED + SMALL_SHARDED
WEIGHTS = ("mix_norm", "mlp_norm", "mem_norm", "final_norm", "w_mem_kv", "w_out", "w_up", "w_down", "attn_w_in",
           "attn_sinks", "lru_w_in", "lru_conv_w", "lru_conv_b", "lru_wa", "lru_ba", "lru_wx", "lru_bx", "lru_lambda")
SMALL_W_ROWS = 32
ADAM_SMALL_ROWS = 640


def _rows(a):
    return a.reshape(-1, ROW)


def _flat_pad(parts, total):
    flat = jnp.concatenate([p.reshape(-1) for p in parts])
    return jnp.pad(flat, (0, total - flat.shape[0]))


def _pad_rows(a):
    flat = a.reshape(-1)
    n = -(-flat.shape[0] // ROW) * ROW
    return jnp.pad(flat, (0, n - flat.shape[0])).reshape(-1, ROW)


LATE = ("w_out", "lru_w_in", "w_up", "w_down0", "w_down1")


def _ready(name, full):
    if name == "w_out":
        n_out = DEPTH * MIX_OUT_W // N_CHIPS
        wo = full[:, :n_out].reshape(N_CHIPS, DEPTH, -1, D_MODEL)
        kv = full[:, n_out:].reshape(N_CHIPS, DEPTH, -1, D_MODEL)
        return ([wo[:, l].reshape(1, MIX_OUT_W, D_MODEL) for l in range(DEPTH)],
                [kv[:, l].reshape(1, D_MODEL, D_MODEL) for l in range(DEPTH)])
    if name == "w_up":
        wu = full.reshape(N_CHIPS, DEPTH, D_MODEL, D_FF // N_CHIPS)
        return [wu[:, l] for l in range(DEPTH)]
    if name == "lru_w_in":
        return full.reshape(N_CHIPS, D_MODEL, LRU_IN_W // N_CHIPS)
    return full.reshape(1, D_FF, D_MODEL)


def _gather_weights(P, chip1):
    bf = lambda a: _rows(a.astype(BF16))
    small = _flat_pad([P[n] for n in SMALL_SHARDED], SMALL_W_ROWS * ROW // 2)
    small_bits = lax.bitcast_convert_type(small, BF16).reshape(SMALL_W_ROWS, ROW)
    early = jnp.concatenate([bf(P["attn_w_in"]), small_bits], axis=0)
    n_in = P["attn_w_in"].size // ROW
    placed = _place_slot(early, chip1, N_CHIPS, name="place_weights", tr=early.shape[0] // 2)
    full = _allgather_chips(placed, name="allgather_weights", forward_to_sibling=True)
    late = {"w_out": jnp.concatenate([bf(P["w_out"]), bf(P["w_mem_kv"])], axis=0), "lru_w_in": bf(P["lru_w_in"]),
            "w_up": bf(P["w_up"]), "w_down0": bf(P["w_down"][0]), "w_down1": bf(P["w_down"][1])}
    pending = {n: _place_slot(late[n], chip1, N_CHIPS, name=f"place_{n}", tr=late[n].shape[0] // 2) for n in LATE}
    W = {n: P[n] for n in REPLICATED}
    W["attn_w_in"] = full[:, :n_in].reshape(N_CHIPS, D_MODEL, ATTN_IN_W // N_CHIPS)
    sm = lax.bitcast_convert_type(full[:, n_in:].reshape(N_CHIPS, -1, 2), F32)
    o = 0
    for n in SMALL_SHARDED:
        shp = P[n].shape[1:]
        cnt = math.prod(shp)
        piece = sm[:, o:o + cnt].reshape((N_CHIPS,) + shp)
        piece = jnp.moveaxis(piece, 0, -2)
        W[n] = piece.reshape(shp[:-1] + (N_CHIPS * shp[-1],)).reshape(-1, D_MODEL)
        o += cnt
    W["lru_wa"] = P["lru_wa"][0].astype(BF16)
    W["lru_wx"] = P["lru_wx"][0].astype(BF16)
    return W, pending


def _small_grad_list(G):
    return [G["mix_norm"], G["mlp_norm"], G["mem_norm"], G["final_norm"], jnp.pad(G["attn_sinks"].reshape(-1), (0, ROW - ATTN_HEADS)),
            G["lru_wa"], G["lru_wx"], G["lru_conv_w"], G["lru_conv_b"], G["lru_ba"], G["lru_bx"], G["lru_lambda"]]


SMALL_G_SIZES = (2 * D_MODEL, 2 * D_MODEL, D_MODEL, D_MODEL, ROW, 2 * 8 * 128 * 128, 2 * 8 * 128 * 128,
                 4 * D_MODEL, D_MODEL, 2 * D_MODEL, 2 * D_MODEL, 2 * D_MODEL)


def _finish_grads(pkf, full0, full1, place, chip1):
    partsf = _chip_exchange(*_reduce_first(pkf, place, "f"), name="grad_chip_exchange_last")
    fullf = _reduce_last(partsf, place, "f")
    small_placed = _place_slot(fullf[PKF_SMALL:], chip1, N_CHIPS, name="place_small_grads", tr=SMALL_G_ROWS)
    small_all = _allgather_chips(small_placed, name="allgather_small_grads", forward_to_sibling=False)
    flat = small_all.reshape(-1)
    small = {}
    o = 0
    names = ("mix_norm", "mlp_norm", "mem_norm", "final_norm", "attn_sinks", "lru_wa", "lru_wx",
             "lru_conv_w", "lru_conv_b", "lru_ba", "lru_bx", "lru_lambda")
    for n, cnt in zip(names, SMALL_G_SIZES):
        small[n] = flat[o:o + cnt]
        o += cnt
    both = lambda off, r: jnp.concatenate([full0[off:off + r], full1[off:off + r]], axis=0)
    big = {"w_up": both(PK_UP, 1024), "w_down": both(PK_DOWN, 1024), "w_out": both(PK_OUT, 384),
           "w_mem_kv": jnp.concatenate([fullf[PKF_KV:PKF_SMALL], full1[PK_KV:PK_OUT]], axis=0),
           "attn_w_in": fullf[:PKF_KV], "lru_w_in": full1[PK_IN:PK_IN + 640]}
    return big, small


def kernel(x, mem, positions, mix_norm, mlp_norm, mem_norm, final_norm, w_mem_kv, w_out, w_up, w_down, attn_w_in, attn_sinks, lru_w_in, lru_conv_w, lru_conv_b, lru_wa, lru_ba, lru_wx, lru_bx, lru_lambda, loss_target, m_mix_norm, m_mlp_norm, m_mem_norm, m_final_norm, m_w_mem_kv, m_w_out, m_w_up, m_w_down, m_attn_w_in, m_attn_sinks, m_lru_w_in, m_lru_conv_w, m_lru_conv_b, m_lru_wa, m_lru_ba, m_lru_wx, m_lru_bx, m_lru_lambda, v_mix_norm, v_mlp_norm, v_mem_norm, v_final_norm, v_w_mem_kv, v_w_out, v_w_up, v_w_down, v_attn_w_in, v_attn_sinks, v_lru_w_in, v_lru_conv_w, v_lru_conv_b, v_lru_wa, v_lru_ba, v_lru_wx, v_lru_bx, v_lru_lambda):
    P = dict(mix_norm=mix_norm, mlp_norm=mlp_norm, mem_norm=mem_norm, final_norm=final_norm, w_mem_kv=w_mem_kv, w_out=w_out,
             w_up=w_up, w_down=w_down, attn_w_in=attn_w_in, attn_sinks=attn_sinks, lru_w_in=lru_w_in, lru_conv_w=lru_conv_w,
             lru_conv_b=lru_conv_b, lru_wa=lru_wa, lru_ba=lru_ba, lru_wx=lru_wx, lru_bx=lru_bx, lru_lambda=lru_lambda)
    M1 = dict(mix_norm=m_mix_norm, mlp_norm=m_mlp_norm, mem_norm=m_mem_norm, final_norm=m_final_norm, w_mem_kv=m_w_mem_kv,
              w_out=m_w_out, w_up=m_w_up, w_down=m_w_down, attn_w_in=m_attn_w_in, attn_sinks=m_attn_sinks, lru_w_in=m_lru_w_in,
              lru_conv_w=m_lru_conv_w, lru_conv_b=m_lru_conv_b, lru_wa=m_lru_wa, lru_ba=m_lru_ba, lru_wx=m_lru_wx,
              lru_bx=m_lru_bx, lru_lambda=m_lru_lambda)
    V2 = dict(mix_norm=v_mix_norm, mlp_norm=v_mlp_norm, mem_norm=v_mem_norm, final_norm=v_final_norm, w_mem_kv=v_w_mem_kv,
              w_out=v_w_out, w_up=v_w_up, w_down=v_w_down, attn_w_in=v_attn_w_in, attn_sinks=v_attn_sinks, lru_w_in=v_lru_w_in,
              lru_conv_w=v_lru_conv_w, lru_conv_b=v_lru_conv_b, lru_wa=v_lru_wa, lru_ba=v_lru_ba, lru_wx=v_lru_wx,
              lru_bx=v_lru_bx, lru_lambda=v_lru_lambda)
    chip = 2 * lax.axis_index("x") + lax.axis_index("y")
    chip1 = chip.astype(jnp.int32).reshape(1)
    place = jnp.stack([lax.axis_index("c").astype(jnp.int32), chip.astype(jnp.int32)])

    W, pending = _gather_weights(P, chip1)
    loss, dx, _, pkf, full0, full1 = _local_step(x[0], mem[0], positions[0], loss_target[0], W, pending, place)
    loss = lax.psum(loss, ("x", "y", "c"))
    big, small = _finish_grads(pkf, full0, full1, place, chip1)

    grads, deltas, new_m, new_v = {}, {}, {}, {}
    for n in BIG:
        g = big[n]
        d, nm, nv = _adamw(_rows(P[n]), g, _rows(M1[n]), _rows(V2[n]), name=f"adamw_{n}")
        grads[n], deltas[n], new_m[n], new_v[n] = (t.reshape(P[n].shape) for t in (g, d, nm, nv))

    for n in SMALL:
        g = small[n]
        if n in SMALL_SHARDED:
            shard = P[n].shape[-1]
            g = lax.dynamic_slice_in_dim(g.reshape(-1, N_CHIPS * shard), chip * shard, shard, axis=1)
        elif n == "attn_sinks":
            g = g[:ATTN_HEADS]
        grads[n] = g.reshape(P[n].shape)
    packs = []
    for src in (P, grads, M1, V2):
        a = jnp.concatenate([_pad_rows(src[n]) for n in SMALL], axis=0)
        packs.append(jnp.pad(a, ((0, ADAM_SMALL_ROWS - a.shape[0]), (0, 0))))
    d_s, nm_s, nv_s = _adamw(*packs, name="adamw_small")
    o = 0
    for n in SMALL:
        cnt = math.prod(P[n].shape)
        r = -(-cnt // ROW)
        for dst, src in ((deltas, d_s), (new_m, nm_s), (new_v, nv_s)):
            dst[n] = src[o:o + r].reshape(-1)[:cnt].reshape(P[n].shape)
        o += r

    return (loss, dx[None], *[grads[n] for n in WEIGHTS], *[deltas[n] for n in WEIGHTS],
            *[new_m[n] for n in WEIGHTS], *[new_v[n] for n in WEIGHTS])
```

```python
import math

import jax
import jax.numpy as jnp
from jax import lax
from jax.experimental import pallas as pl
from jax.experimental.pallas import tpu as pltpu

F32 = jnp.float32
BF16 = jnp.bfloat16
MESH = pl.DeviceIdType.MESH

D_MODEL = 1024
DEPTH = 2
EPS = 1e-6
ATTN_HEADS = 16
ATTN_KV_HEADS = 4
HEAD_DIM = 64
WINDOW = 128
BLOCK = 128
ROPE_THETA = 500000.0
ROPE_DIM = 16
Q_W = 1024
KV_W = 256
MEM_LEN = 256
MEM_HEADS = 4
MEM_HEAD_DIM = 128
MEM_W = 512
LRU_BLOCKS = 8
LRU_C = 8.0
ATTN_IN_W = 2048
LRU_IN_W = 2560
MIX_OUT_W = 1536
D_FF = 4096
NEG = -1e30
N_CHIPS = 4

ADAM_LR = 0.001
ADAM_B1 = 0.9
ADAM_B2 = 0.999
ADAM_EPS = 1e-08
ADAM_WD = 0.01
ADAM_STEP = 10

LANES = 128
SCAN_ROWS = 512
VMEM_LIMIT = 56 * 1024 * 1024

NT = (((1,), (1,)), ((), ()))
TN = (((0,), (0,)), ((), ()))


def _cp(sem=None):
    return pltpu.CompilerParams(dimension_semantics=sem, vmem_limit_bytes=VMEM_LIMIT)


HBM = pl.BlockSpec(memory_space=pl.ANY)
GATHER_SEMS = 6


def _place():
    x, y, c = lax.axis_index("x"), lax.axis_index("y"), lax.axis_index("c")
    chips = [(1 - x, y), (x, 1 - y), (1 - x, 1 - y)]
    return x, y, c, chips


def _remote(src, dst, send_sems, recv_sems, k, to):
    return pltpu.make_async_remote_copy(src_ref=src, dst_ref=dst, send_sem=send_sems.at[k], recv_sem=recv_sems.at[k],
                                        device_id=to, device_id_type=MESH)


def _gather_start(o_ref, send_sems, recv_sems):
    x, y, c, chips = _place()
    half = o_ref.shape[1] // 2
    own = o_ref.at[2 * x + y, pl.ds(pl.multiple_of(c * half, 16), half)]
    for j, (cx, cy) in enumerate(chips):
        _remote(own, own, send_sems, recv_sems, j, (cx, cy, c)).start()


def _gather_forward(o_ref, send_sems, recv_sems):
    x, y, c, chips = _place()
    half = o_ref.shape[1] // 2
    my_rows = pl.ds(pl.multiple_of(c * half, 16), half)
    for j, (cx, cy) in enumerate(chips):
        landed = o_ref.at[2 * cx + cy, my_rows]
        _remote(landed, landed, send_sems, recv_sems, j, (cx, cy, c)).wait_recv()
        _remote(landed, landed, send_sems, recv_sems, 3 + j, (x, y, 1 - c)).start()


def _gather_drain(o_ref, send_sems, recv_sems):
    x, y, c, chips = _place()
    half = o_ref.shape[1] // 2
    my_rows = pl.ds(pl.multiple_of(c * half, 16), half)
    sib_rows = pl.ds(pl.multiple_of((1 - c) * half, 16), half)
    own = o_ref.at[2 * x + y, my_rows]
    for j, (cx, cy) in enumerate(chips):
        got = o_ref.at[2 * cx + cy, sib_rows]
        _remote(got, got, send_sems, recv_sems, 3 + j, (x, y, 1 - c)).wait_recv()
    for j, (cx, cy) in enumerate(chips):
        _remote(own, own, send_sems, recv_sems, j, (cx, cy, c)).wait_send()
        landed = o_ref.at[2 * cx + cy, my_rows]
        _remote(landed, landed, send_sems, recv_sems, 3 + j, (x, y, 1 - c)).wait_send()


def _gather_finish(o_ref, send_sems, recv_sems):
    _gather_forward(o_ref, send_sems, recv_sems)
    _gather_drain(o_ref, send_sems, recv_sems)


def _exchange_start(h_ref, o_ref, send_sems, recv_sems, rows=None, base=0):
    x, y, c, chips = _place()
    rows = pl.ds(0, h_ref.shape[1]) if rows is None else rows
    for j, (cx, cy) in enumerate(chips):
        _remote(h_ref.at[2 * cx + cy, rows], o_ref.at[2 * x + y, rows], send_sems, recv_sems, base + j, (cx, cy, c)).start()


def _exchange_finish(h_ref, o_ref, send_sems, recv_sems, rows=None, base=0):
    x, y, c, chips = _place()
    rows = pl.ds(0, h_ref.shape[1]) if rows is None else rows
    for j, (cx, cy) in enumerate(chips):
        got = o_ref.at[2 * cx + cy, rows]
        _remote(got, got, send_sems, recv_sems, base + j, (cx, cy, c)).wait_recv()
    for j, (cx, cy) in enumerate(chips):
        _remote(h_ref.at[2 * cx + cy, rows], o_ref.at[2 * x + y, rows], send_sems, recv_sems, base + j, (cx, cy, c)).wait_send()


def _sib_exchange_copies(g_ref, o_ref, send_sems, recv_sems):
    x, y, c, _ = _place()
    half = g_ref.shape[1] // 2
    other = pl.ds(pl.multiple_of((1 - c) * half, 8), half)
    return [_remote(g_ref.at[s, other], o_ref.at[s], send_sems, recv_sems, s, (x, y, 1 - c)) for s in range(N_CHIPS)]


def _sib_exchange_start(*refs):
    for cp in _sib_exchange_copies(*refs):
        cp.start()


def _sib_exchange_finish(*refs):
    for cp in _sib_exchange_copies(*refs):
        cp.wait()


def _sib_allgather_start(*refs):
    *o_refs, send_sems, recv_sems = refs
    x, y, c, _ = _place()
    for i, o_ref in enumerate(o_refs):
        half = o_ref.shape[0] // 2
        mine = o_ref.at[pl.ds(pl.multiple_of(c * half, 8), half)]
        _remote(mine, mine, send_sems, recv_sems, i, (x, y, 1 - c)).start()


def _sib_allgather_finish(*refs):
    *o_refs, send_sems, recv_sems = refs
    x, y, c, _ = _place()
    for i, o_ref in enumerate(o_refs):
        half = o_ref.shape[0] // 2
        mine = o_ref.at[pl.ds(pl.multiple_of(c * half, 8), half)]
        got = o_ref.at[pl.ds(pl.multiple_of((1 - c) * half, 8), half)]
        _remote(got, got, send_sems, recv_sems, i, (x, y, 1 - c)).wait_recv()
        _remote(mine, mine, send_sems, recv_sems, i, (x, y, 1 - c)).wait_send()


class _Rider:
    def __init__(self, args, start, finish, inplace=1, mid=None):
        self.args, self.start, self.finish, self.inplace, self.mid = list(args), start, finish, inplace, mid


def _gather_rider(buf):
    return None if buf is None else _Rider([buf], _gather_start, _gather_finish, mid=(_gather_forward, _gather_drain))


def _exchange_rider(*parts):
    n = len(parts)
    assert 3 * n <= GATHER_SEMS

    def run(fn):
        def go(*refs):
            sems = refs[2 * n:]
            for i, (_, _, r0, nr) in enumerate(parts):
                fn(refs[i], refs[n + i], *sems, rows=pl.ds(r0, nr), base=3 * i)
        return go

    return _Rider([p[0] for p in parts] + [p[1] for p in parts], run(_exchange_start), run(_exchange_finish), inplace=n)


def _sib_exchange_rider(g):
    landing = lax.empty((N_CHIPS, g.shape[1] // 2, g.shape[2]), g.dtype)
    return _Rider([g, landing], _sib_exchange_start, _sib_exchange_finish)


def _sib_allgather_rider(*fulls):
    return _Rider(fulls, _sib_allgather_start, _sib_allgather_finish, inplace=len(fulls))


class _Hosted:
    def __init__(self, rider, n_in, n_out):
        self.rider = rider
        self.on = rider is not None
        self.args = rider.args if self.on else []
        k = len(self.args)
        p = self.p = rider.inplace if self.on else 0
        self.alias = {n_in + k - p + i: n_out + i for i in range(p)}
        self.in_specs = [HBM] * k
        self.out_specs = [HBM] * p
        self.out_shape = [jax.ShapeDtypeStruct(a.shape, a.dtype) for a in self.args[k - p:]]
        self.scratch = [pltpu.SemaphoreType.DMA((GATHER_SEMS,)), pltpu.SemaphoreType.DMA((GATHER_SEMS,))] if self.on else []

    def split(self, refs, n_in, n_out):
        refs = list(refs)
        if not self.on:
            return refs[:n_in], refs[n_in:n_in + n_out], refs[n_in + n_out:], None
        k, p = len(self.args), self.p
        ins, outs = refs[:n_in], refs[n_in + k:n_in + k + n_out]
        rest = refs[n_in + k + n_out + p:]
        rrefs = refs[n_in:n_in + k - p] + refs[n_in + k + n_out:n_in + k + n_out + p] + [rest[-2], rest[-1]]
        return ins, outs, rest[:-2], rrefs

    def run(self, rrefs, step, n_steps, compute):
        if rrefs is None:
            return compute()

        mid = self.rider.mid
        mid_step = (3 * n_steps) // 4
        two_stage = mid is not None and 0 < mid_step < n_steps - 1

        @pl.when(step == 0)
        def _():
            self.rider.start(*rrefs)

        compute()

        if two_stage:
            @pl.when(step == mid_step)
            def _():
                mid[0](*rrefs)

        @pl.when(step == n_steps - 1)
        def _():
            (mid[1] if two_stage else self.rider.finish)(*rrefs)


def _mm_nn(a, w3, *, name, out_dtype=F32, norm_g=None, resid=None, relu2=False, tm=512, gather=None):
    M, K = a.shape
    ns, _, n = w3.shape
    N = ns * n
    tm = min(tm, M)
    has_norm = norm_g is not None
    has_res = resid is not None
    n_in = 2 + has_norm + has_res
    n_out = (2 if relu2 else 1) + has_norm
    host = _Hosted(_gather_rider(gather), n_in, n_out)

    def body(*refs):
        ins, outs, _, gref = host.split(refs, n_in, n_out)
        a_ref, w_ref = ins[0], ins[1]
        g_ref = ins[2] if has_norm else None
        r_ref = ins[-1] if has_res else None

        def compute():
            if has_norm:
                xv = a_ref[...]
                rs = lax.rsqrt(jnp.mean(xv * xv, axis=-1, keepdims=True) + EPS)
                ab = (xv * rs * g_ref[...]).astype(BF16)
                outs[-1][...] = ab
            else:
                ab = a_ref[...]
            for s in range(ns):
                acc = jnp.dot(ab, w_ref[s], preferred_element_type=F32)
                sl = slice(s * n, (s + 1) * n)
                if relu2:
                    outs[0][:, sl] = acc.astype(BF16)
                    rl = jnp.maximum(acc, 0.0)
                    outs[1][:, sl] = (rl * rl).astype(BF16)
                elif has_res:
                    outs[0][:, sl] = r_ref[:, sl] + acc
                else:
                    outs[0][:, sl] = acc.astype(out_dtype)

        host.run(gref, pl.program_id(0), M // tm, compute)

    row = lambda w: pl.BlockSpec((tm, w), lambda i: (i, 0))
    in_specs = [row(K), pl.BlockSpec((ns, K, n), lambda i: (0, 0, 0))]
    args = [a, w3]
    if has_norm:
        in_specs.append(pl.BlockSpec((1, K), lambda i: (0, 0)))
        args.append(norm_g.reshape(1, K))
    if has_res:
        in_specs.append(row(N))
        args.append(resid)
    if relu2:
        out_shape = [jax.ShapeDtypeStruct((M, N), BF16), jax.ShapeDtypeStruct((M, N), BF16)]
        out_specs = [row(N), row(N)]
    else:
        out_shape = [jax.ShapeDtypeStruct((M, N), F32 if has_res else out_dtype)]
        out_specs = [row(N)]
    if has_norm:
        out_shape.append(jax.ShapeDtypeStruct((M, K), BF16))
        out_specs.append(row(K))
    res = pl.pallas_call(body, grid=(M // tm,), in_specs=in_specs + host.in_specs, out_specs=out_specs + host.out_specs,
                         out_shape=out_shape + host.out_shape, scratch_shapes=host.scratch, input_output_aliases=host.alias,
                         name=name, compiler_params=_cp(("arbitrary",) if host.on else ("parallel",)))(*args, *host.args)
    return res if len(res) > 1 else res[0]


def _mm_nt(g, w3, *, name, out_dtype=BF16, up=None, norm_x=None, norm_g=None, dres=None, tm=512):
    M = g.shape[0]
    ns, K, n = w3.shape
    tm = min(tm, M)
    has_up = up is not None
    has_norm = norm_x is not None
    has_res = dres is not None

    def body(*refs):
        refs = list(refs)
        g_ref, w_ref = refs[0], refs[1]
        pos = 2
        if has_up:
            up_ref = refs[pos]
            pos += 1
        if has_norm:
            x_ref, gn_ref = refs[pos], refs[pos + 1]
            pos += 2
        if has_res:
            r_ref = refs[pos]
            pos += 1
        outs = refs[pos:]
        acc = None
        for s in range(ns):
            part = lax.dot_general(g_ref[:, s * n:(s + 1) * n], w_ref[s], NT, preferred_element_type=F32)
            acc = part if acc is None else acc + part
        if has_up:
            outs[0][...] = (acc * (2.0 * jnp.maximum(up_ref[...].astype(F32), 0.0))).astype(BF16)
        elif has_norm:
            xv = x_ref[...]
            rs = lax.rsqrt(jnp.mean(xv * xv, axis=-1, keepdims=True) + EPS)
            xn = xv * rs
            dxn = acc * gn_ref[...]
            dx = rs * (dxn - xn * jnp.mean(dxn * xn, axis=-1, keepdims=True))
            if has_res:
                dx = dx + r_ref[...]
            outs[0][...] = dx
            outs[1][...] = dx.astype(BF16)

            @pl.when(pl.program_id(0) == 0)
            def _():
                outs[2][...] = jnp.zeros_like(outs[2])

            outs[2][...] += jnp.sum(acc * xn, axis=0, keepdims=True)
        else:
            outs[0][...] = acc.astype(out_dtype)

    row = lambda w: pl.BlockSpec((tm, w), lambda i: (i, 0))
    in_specs = [row(ns * n), pl.BlockSpec((ns, K, n), lambda i: (0, 0, 0))]
    args = [g, w3]
    if has_up:
        in_specs.append(row(K))
        args.append(up)
    if has_norm:
        in_specs += [row(K), pl.BlockSpec((1, K), lambda i: (0, 0))]
        args += [norm_x, norm_g.reshape(1, K)]
    if has_res:
        in_specs.append(row(K))
        args.append(dres)
    if has_norm:
        out_shape = [jax.ShapeDtypeStruct((M, K), F32), jax.ShapeDtypeStruct((M, K), BF16),
                     jax.ShapeDtypeStruct((1, K), F32)]
        out_specs = [row(K), row(K), pl.BlockSpec((1, K), lambda i: (0, 0))]
        sem = ("arbitrary",)
    else:
        out_shape = [jax.ShapeDtypeStruct((M, K), BF16 if has_up else out_dtype)]
        out_specs = [row(K)]
        sem = ("parallel",)
    res = pl.pallas_call(body, grid=(M // tm,), in_specs=in_specs, out_specs=out_specs, out_shape=out_shape,
                         name=name, compiler_params=_cp(sem))(*args)
    return res if len(res) > 1 else res[0]


def _mm_tn(a, g, ns, *, name, tk=512, tm=4096, packed=None, rider=None):
    M, K = a.shape
    n = g.shape[1] // ns
    tm = min(tm, M)
    tk = min(tk, K)
    nk, nm = K // tk, M // tm
    n_in = 3 if (packed is not None and packed[0] is not None) else 2
    host = _Hosted(rider, n_in, 1)

    def body(*refs):
        ins, outs, _, rrefs = host.split(refs, n_in, 1)
        a_ref, g_ref, o_ref = ins[0], ins[1], outs[0]

        def compute():
            @pl.when(pl.program_id(2) == 0)
            def _():
                o_ref[...] = jnp.zeros_like(o_ref)

            o_ref[0] += lax.dot_general(a_ref[...], g_ref[...], TN, preferred_element_type=F32)

        step = (pl.program_id(0) * nk + pl.program_id(1)) * nm + pl.program_id(2)
        host.run(rrefs, step, ns * nk * nm, compute)

    in_specs = [pl.BlockSpec((tm, tk), lambda s, k, m: (m, k)), pl.BlockSpec((tm, n), lambda s, k, m: (m, s))]
    args = [a, g]
    alias = {}
    if packed is None:
        out_spec = pl.BlockSpec((1, tk, n), lambda s, k, m: (s, k, 0))
        out_shape = jax.ShapeDtypeStruct((ns, K, n), F32)
    else:
        buf, rows, off = packed
        per_chip = K * ns // N_CHIPS
        assert n == ROW and per_chip % tk == 0 and off % tk == 0
        if ns == N_CHIPS:
            out_spec = pl.BlockSpec((1, tk, n), lambda s, k, m: (s, off // tk + k, 0))
        else:
            kpc = per_chip // tk
            out_spec = pl.BlockSpec((1, tk, n), lambda s, k, m: (k // kpc, off // tk + k % kpc, 0))
        out_shape = jax.ShapeDtypeStruct((N_CHIPS, rows, ROW), F32)
        if buf is not None:
            in_specs.append(HBM)
            args.append(buf)
            alias = {2: 0}
    sem = ("arbitrary",) * 3 if host.on else ("parallel", "parallel", "arbitrary")
    res = pl.pallas_call(
        body, grid=(ns, nk, nm), in_specs=in_specs + host.in_specs, out_specs=[out_spec] + host.out_specs,
        out_shape=[out_shape] + host.out_shape, scratch_shapes=host.scratch, name=name,
        input_output_aliases={**alias, **host.alias}, compiler_params=_cp(sem))(*args, *host.args)
    return res if host.on else res[0]


def _final(act, w_down, x, gain, target, *, name="mlp_down_final", tr=512):
    S, Dm = x.shape
    tr = min(tr, S)
    Kf = act.shape[1]

    def body(a_ref, w_ref, x_ref, g_ref, t_ref, loss_ref, dx_ref, dxb_ref, dg_ref):
        @pl.when(pl.program_id(0) == 0)
        def _():
            loss_ref[...] = jnp.zeros_like(loss_ref)
            dg_ref[...] = jnp.zeros_like(dg_ref)

        xv = x_ref[...] + jnp.dot(a_ref[...], w_ref[0], preferred_element_type=F32)
        gv = g_ref[...]
        rs = lax.rsqrt(jnp.mean(xv * xv, axis=-1, keepdims=True) + EPS)
        xn = xv * rs
        err = xn * gv - t_ref[...]
        loss_ref[...] += 0.5 * jnp.sum(jnp.mean(err * err, axis=-1, keepdims=True), axis=0, keepdims=True)
        dout = err * (1.0 / Dm)
        dg_ref[...] += jnp.sum(dout * xn, axis=0, keepdims=True)
        dxn = dout * gv
        dx = rs * (dxn - xn * jnp.mean(dxn * xn, axis=-1, keepdims=True))
        dx_ref[...] = dx
        dxb_ref[...] = dx.astype(BF16)

    row = pl.BlockSpec((tr, Dm), lambda i: (i, 0))
    return pl.pallas_call(
        body, grid=(S // tr,),
        in_specs=[pl.BlockSpec((tr, Kf), lambda i: (i, 0)), pl.BlockSpec((1, Kf, Dm), lambda i: (0, 0, 0)), row,
                  pl.BlockSpec((1, Dm), lambda i: (0, 0)), row],
        out_specs=[pl.BlockSpec((1, 1), lambda i: (0, 0)), row, row, pl.BlockSpec((1, Dm), lambda i: (0, 0))],
        out_shape=[jax.ShapeDtypeStruct((1, 1), F32), jax.ShapeDtypeStruct((S, Dm), F32),
                   jax.ShapeDtypeStruct((S, Dm), BF16), jax.ShapeDtypeStruct((1, Dm), F32)],
        name=name, compiler_params=_cp(("arbitrary",)))(act, w_down, x, gain.reshape(1, Dm), target)


def _rope_tables(positions):
    half = ROPE_DIM // 2
    inv_freq = ROPE_THETA ** (-2.0 * jnp.arange(half, dtype=F32) / ROPE_DIM)
    ang = positions.astype(F32)[:, None] * inv_freq
    cos, sin = jnp.cos(ang), jnp.sin(ang)
    S = positions.shape[0]
    ones = jnp.ones((S, HEAD_DIM - ROPE_DIM), F32)
    cos64 = jnp.concatenate([cos, cos, ones], axis=1)
    sin64 = jnp.concatenate([-sin, sin, 0.0 * ones], axis=1)
    return jnp.tile(cos64, (1, 2)), jnp.tile(sin64, (1, 2))


def _rope_partner(t):
    lane = lax.broadcasted_iota(jnp.int32, t.shape, 1)
    low = (lane & (HEAD_DIM - 1)) < (ROPE_DIM // 2)
    return jnp.where(low, pltpu.roll(t, LANES - ROPE_DIM // 2, 1), pltpu.roll(t, ROPE_DIM // 2, 1))


def _qk_prep(p, cos_t, sin_t, *, name="qk_prep", tr=256, gather=None):
    S = p.shape[0]
    tr = min(tr, S)
    scale = HEAD_DIM ** -0.5
    host = _Hosted(_gather_rider(gather), 3, 4)

    def body(*refs):
        ins, outs, _, gref = host.split(refs, 3, 4)
        host.run(gref, pl.program_id(0), S // tr, lambda: inner(*ins, *outs))

    def inner(p_ref, c_ref, s_ref, q_ref, k_ref, v_ref, va_ref):
        cs, sn = c_ref[...], s_ref[...]
        lane = lax.broadcasted_iota(jnp.int32, (tr, LANES), 1)
        lo = lane < HEAD_DIM
        for c in range(Q_W // LANES):
            t = p_ref[:, c * LANES:(c + 1) * LANES]
            q_ref[:, c * LANES:(c + 1) * LANES] = ((t * cs + _rope_partner(t) * sn) * scale).astype(BF16)
        for c in range(KV_W // LANES):
            t = p_ref[:, Q_W + c * LANES:Q_W + (c + 1) * LANES]
            kc = t * cs + _rope_partner(t) * sn
            vc = p_ref[:, Q_W + KV_W + c * LANES:Q_W + KV_W + (c + 1) * LANES]
            for arr, ref in ((kc, k_ref), (vc, v_ref)):
                sw = pltpu.roll(arr, HEAD_DIM, 1)
                ref[:, (2 * c) * LANES:(2 * c + 1) * LANES] = jnp.where(lo, arr, sw).astype(BF16)
                ref[:, (2 * c + 1) * LANES:(2 * c + 2) * LANES] = jnp.where(lo, sw, arr).astype(BF16)
            sw = pltpu.roll(vc, HEAD_DIM, 1)
            for k, aug in enumerate((jnp.where(lo, vc, 1.0), jnp.where(lo, 1.0, sw), jnp.where(lo, sw, 1.0), jnp.where(lo, 1.0, vc))):
                va_ref[:, (4 * c + k) * LANES:(4 * c + k + 1) * LANES] = aug.astype(BF16)

    row = lambda w: pl.BlockSpec((tr, w), lambda i: (i, 0))
    return pl.pallas_call(
        body, grid=(S // tr,), in_specs=[row(ATTN_IN_W), row(LANES), row(LANES)] + host.in_specs,
        out_specs=[row(Q_W), row(2 * KV_W), row(2 * KV_W), row(4 * KV_W)] + host.out_specs,
        out_shape=[jax.ShapeDtypeStruct((S, Q_W), BF16), jax.ShapeDtypeStruct((S, 2 * KV_W), BF16),
                   jax.ShapeDtypeStruct((S, 2 * KV_W), BF16), jax.ShapeDtypeStruct((S, 4 * KV_W), BF16)] + host.out_shape,
        scratch_shapes=host.scratch, input_output_aliases=host.alias,
        name=name, compiler_params=_cp(("arbitrary",) if host.on else ("parallel",)))(p, cos_t, sin_t, *host.args)


def _qk_prep_bwd(dq, dk, dv, dmq, cos_t, sin_t, *, name="qk_prep_bwd", tr=256):
    S = dq.shape[0]
    tr = min(tr, S)

    def body(dq_ref, dk_ref, dv_ref, dmq_ref, c_ref, s_ref, o_ref):
        cs, sn = c_ref[...], s_ref[...]
        for c in range(Q_W // LANES):
            t = dq_ref[:, c * LANES:(c + 1) * LANES]
            o_ref[:, c * LANES:(c + 1) * LANES] = (t * cs - _rope_partner(t) * sn).astype(BF16)
        for c in range(KV_W // LANES):
            t = dk_ref[:, c * LANES:(c + 1) * LANES]
            o_ref[:, Q_W + c * LANES:Q_W + (c + 1) * LANES] = (t * cs - _rope_partner(t) * sn).astype(BF16)
        o_ref[:, Q_W + KV_W:Q_W + 2 * KV_W] = dv_ref[...].astype(BF16)
        o_ref[:, Q_W + 2 * KV_W:] = dmq_ref[...]

    row = lambda w: pl.BlockSpec((tr, w), lambda i: (i, 0))
    return pl.pallas_call(
        body, grid=(S // tr,), in_specs=[row(Q_W), row(KV_W), row(KV_W), row(MEM_W), row(LANES), row(LANES)],
        out_specs=row(ATTN_IN_W), out_shape=jax.ShapeDtypeStruct((S, ATTN_IN_W), BF16),
        name=name, compiler_params=_cp(("parallel",)))(dq, dk, dv, dmq, cos_t, sin_t)


def _band(n, S):
    start = pl.multiple_of(jnp.clip((n - 1) * BLOCK, 0, S - 3 * BLOCK), BLOCK)
    qi = lax.broadcasted_iota(jnp.int32, (BLOCK, 3 * BLOCK), 0) + n * BLOCK
    ki = lax.broadcasted_iota(jnp.int32, (BLOCK, 3 * BLOCK), 1) + start
    return start, jnp.abs(ki - qi) <= WINDOW


def _head_operand(ref, h, lo):
    c = h // 2
    t = ref[:, c * LANES:(c + 1) * LANES].astype(F32)
    return jnp.where(lo if h % 2 == 0 else jnp.logical_not(lo), t, 0.0).astype(BF16)


GROUP = ATTN_HEADS // ATTN_KV_HEADS
EVENS_FIRST = (0, 2, 1, 3)


def _attn_fwd(q, kd, va, sinks, *, name="attn_fwd", gather=None):
    S = q.shape[0]
    host = _Hosted(_gather_rider(gather), 4, 2)

    def body(*refs):
        ins, outs, scr, gref = host.split(refs, 4, 2)
        host.run(gref, pl.program_id(0), S // BLOCK, lambda: inner(*ins, *outs, *scr))

    def inner(sink_ref, q_ref, k_ref, va_ref, o_ref, lse_ref, p_scr):
        n = pl.program_id(0)
        start, mask = _band(n, S)
        lane = lax.broadcasted_iota(jnp.int32, (BLOCK, LANES), 1)
        lo = lane < HEAD_DIM
        rows = pl.ds(start, 3 * BLOCK)
        scores = []
        for g in range(ATTN_KV_HEADS):
            qst = jnp.concatenate([_head_operand(q_ref, GROUP * g + j, lo) for j in EVENS_FIRST], axis=0)
            scores.append(lax.dot_general(qst, k_ref[rows, g * LANES:(g + 1) * LANES], NT, preferred_element_type=F32))
        ms = {}
        for g in range(ATTN_KV_HEADS):
            for pos, j in enumerate(EVENS_FIRST):
                h = GROUP * g + j
                s = jnp.where(mask, scores[g][pos * BLOCK:(pos + 1) * BLOCK], NEG)
                ms[h] = jnp.maximum(jnp.max(s, axis=-1, keepdims=True), sink_ref[h])
                p_scr[(GROUP * g + pos) * BLOCK:(GROUP * g + pos + 1) * BLOCK, :] = jnp.exp(s - ms[h]).astype(BF16)
        pvs = {}
        for g in range(ATTN_KV_HEADS):
            for par in range(2):
                r0 = (GROUP * g + 2 * par) * BLOCK
                pvs[g, par] = jnp.dot(p_scr[r0:r0 + 2 * BLOCK, :], va_ref[rows, (2 * g + par) * LANES:(2 * g + par + 1) * LANES],
                                      preferred_element_type=F32)
        lse_blk = jnp.zeros((BLOCK, LANES), F32)
        for g in range(ATTN_KV_HEADS):
            outs = {}
            for par in range(2):
                for k in range(2):
                    j = EVENS_FIRST[2 * par + k]
                    h = GROUP * g + j
                    pv = pvs[g, par][k * BLOCK:(k + 1) * BLOCK]
                    den = pltpu.roll(pv, HEAD_DIM, 1) + jnp.exp(sink_ref[h] - ms[h])
                    outs[j] = pv * (1.0 / den)
                    l = den[:, par * HEAD_DIM:par * HEAD_DIM + 1]
                    lse_blk = jnp.where(lane == h, ms[h] + jnp.log(l), lse_blk)
            for jj in range(2):
                o_ref[:, (2 * g + jj) * LANES:(2 * g + jj + 1) * LANES] = jnp.where(lo, outs[2 * jj], outs[2 * jj + 1]).astype(BF16)
        lse_ref[...] = lse_blk

    full = lambda w: pl.BlockSpec((S, w), lambda i: (0, 0))
    return pl.pallas_call(
        body, grid=(S // BLOCK,),
        in_specs=[pl.BlockSpec(memory_space=pltpu.SMEM), pl.BlockSpec((BLOCK, Q_W), lambda i: (i, 0)),
                  full(2 * KV_W), full(4 * KV_W)] + host.in_specs,
        out_specs=[pl.BlockSpec((BLOCK, Q_W), lambda i: (i, 0)), pl.BlockSpec((BLOCK, LANES), lambda i: (i, 0))] + host.out_specs,
        out_shape=[jax.ShapeDtypeStruct((S, MIX_OUT_W), BF16), jax.ShapeDtypeStruct((S, LANES), F32)] + host.out_shape,
        scratch_shapes=[pltpu.VMEM((ATTN_HEADS * BLOCK, 3 * BLOCK), BF16)] + host.scratch, input_output_aliases=host.alias,
        name=name, compiler_params=_cp(("arbitrary",) if host.on else ("parallel",)))(sinks, q, kd, va, *host.args)


def _attn_bwd(q, kd, vd, ao, lse, sinks, dcat, *, name="attn_bwd", rider=None):
    S = q.shape[0]
    scale = HEAD_DIM ** -0.5
    host = _Hosted(rider, 7, 4)

    def body(*refs):
        ins, outs, scr, rrefs = host.split(refs, 7, 4)
        host.run(rrefs, pl.program_id(0), S // BLOCK, lambda: inner(*ins, *outs, *scr))

    def inner(sink_ref, q_ref, k_ref, v_ref, ao_ref, lse_ref, do_ref, dq_ref, dk_ref, dv_ref, ds_ref, p_scr, dsb_scr):
        n = pl.program_id(0)

        @pl.when(n == 0)
        def _():
            dk_ref[...] = jnp.zeros_like(dk_ref)
            dv_ref[...] = jnp.zeros_like(dv_ref)
            ds_ref[...] = jnp.zeros_like(ds_ref)

        start, mask = _band(n, S)
        lane = lax.broadcasted_iota(jnp.int32, (BLOCK, LANES), 1)
        lo = lane < HEAD_DIM
        lane3 = lax.broadcasted_iota(jnp.int32, (3 * BLOCK, LANES), 1)
        row8 = lax.broadcasted_iota(jnp.int32, (8, LANES), 0)
        lane8 = lax.broadcasted_iota(jnp.int32, (8, LANES), 1)
        dsink = jnp.zeros((8, LANES), F32)
        lse_blk = lse_ref[...]
        rows = pl.ds(start, 3 * BLOCK)
        lses, deltas = {}, {}
        for c in range(Q_W // LANES):
            prod = do_ref[:, c * LANES:(c + 1) * LANES].astype(F32) * ao_ref[:, c * LANES:(c + 1) * LANES].astype(F32)
            for k in range(2):
                h = 2 * c + k
                deltas[h] = jnp.sum(jnp.where(lo if k == 0 else jnp.logical_not(lo), prod, 0.0), axis=1, keepdims=True)
                lses[h] = jnp.sum(jnp.where(lane == h, lse_blk, 0.0), axis=1, keepdims=True)
                val = -jnp.sum(jnp.exp(sink_ref[h] - lses[h]) * deltas[h], axis=0, keepdims=True)
                dsink = dsink + jnp.where((row8 == 0) & (lane8 == h), val, 0.0)
        stack = lambda ref, g: jnp.concatenate([_head_operand(ref, GROUP * g + j, lo) for j in range(GROUP)], axis=0)
        ss, dps = [], []
        for g in range(ATTN_KV_HEADS):
            ss.append(lax.dot_general(stack(q_ref, g), k_ref[rows, g * LANES:(g + 1) * LANES], NT, preferred_element_type=F32))
            dps.append(lax.dot_general(stack(do_ref, g), v_ref[rows, g * LANES:(g + 1) * LANES], NT, preferred_element_type=F32))
        for g in range(ATTN_KV_HEADS):
            for j in range(GROUP):
                h = GROUP * g + j
                r = slice(j * BLOCK, (j + 1) * BLOCK)
                hr = slice(h * BLOCK, (h + 1) * BLOCK)
                p = jnp.exp(jnp.where(mask, ss[g][r], NEG) - lses[h])
                p_scr[hr, :] = p.astype(BF16)
                dsb_scr[hr, :] = (p * (dps[g][r] - deltas[h])).astype(BF16)
        for g in range(ATTN_KV_HEADS):
            cols = slice((g // 2) * LANES, (g // 2 + 1) * LANES)
            gr = slice(GROUP * g * BLOCK, GROUP * (g + 1) * BLOCK)
            dsg = dsb_scr[gr, :]
            dqs = jnp.dot(dsg, k_ref[rows, g * LANES:(g + 1) * LANES], preferred_element_type=F32) * scale
            for jj in range(2):
                dq_ref[:, (2 * g + jj) * LANES:(2 * g + jj + 1) * LANES] = jnp.where(
                    lo, dqs[(2 * jj) * BLOCK:(2 * jj + 1) * BLOCK], dqs[(2 * jj + 1) * BLOCK:(2 * jj + 2) * BLOCK])
            half = (lane3 < HEAD_DIM) if g % 2 == 0 else (lane3 >= HEAD_DIM)
            dkr = lax.dot_general(dsg, stack(q_ref, g), TN, preferred_element_type=F32)
            dk_ref[rows, cols] += jnp.where(half, dkr + pltpu.roll(dkr, HEAD_DIM, 1), 0.0)
            dvr = lax.dot_general(p_scr[gr, :], stack(do_ref, g), TN, preferred_element_type=F32)
            dv_ref[rows, cols] += jnp.where(half, dvr + pltpu.roll(dvr, HEAD_DIM, 1), 0.0)
        ds_ref[...] += dsink

    full = lambda w: pl.BlockSpec((S, w), lambda i: (0, 0))
    blk = lambda w: pl.BlockSpec((BLOCK, w), lambda i: (i, 0))
    return pl.pallas_call(
        body, grid=(S // BLOCK,),
        in_specs=[pl.BlockSpec(memory_space=pltpu.SMEM), blk(Q_W), full(2 * KV_W), full(2 * KV_W), blk(Q_W), blk(LANES), blk(Q_W)]
        + host.in_specs,
        out_specs=[blk(Q_W), full(KV_W), full(KV_W), pl.BlockSpec((8, LANES), lambda i: (0, 0))] + host.out_specs,
        out_shape=[jax.ShapeDtypeStruct((S, Q_W), F32), jax.ShapeDtypeStruct((S, KV_W), F32),
                   jax.ShapeDtypeStruct((S, KV_W), F32), jax.ShapeDtypeStruct((8, LANES), F32)] + host.out_shape,
        scratch_shapes=[pltpu.VMEM((ATTN_HEADS * BLOCK, 3 * BLOCK), BF16), pltpu.VMEM((ATTN_HEADS * BLOCK, 3 * BLOCK), BF16)]
        + host.scratch, input_output_aliases=host.alias,
        name=name, compiler_params=_cp(("arbitrary",)))(sinks, q, kd, vd, ao, lse, dcat, *host.args)


def _mem_probs(q_ref, kv_ref, h):
    scale = MEM_HEAD_DIM ** -0.5
    qh = q_ref[:, h * LANES:(h + 1) * LANES].astype(BF16)
    s = lax.dot_general(qh, kv_ref[:, h * LANES:(h + 1) * LANES], NT, preferred_element_type=F32) * scale
    m = jnp.max(s, axis=-1, keepdims=True)
    pe = jnp.exp(s - m)
    return qh, pe * (1.0 / jnp.sum(pe, axis=-1, keepdims=True))


def _memattn_fwd(p, qblk, kv, cat, *, name="memattn_fwd", tr=512):
    S = p.shape[0]
    tr = min(tr, S)

    def body(q_ref, kv_ref, cat_ref, o_ref):
        for h in range(MEM_HEADS):
            _, pr = _mem_probs(q_ref, kv_ref, h)
            o = jnp.dot(pr.astype(BF16), kv_ref[:, MEM_W + h * LANES:MEM_W + (h + 1) * LANES], preferred_element_type=F32)
            o_ref[:, h * LANES:(h + 1) * LANES] = o.astype(BF16)

    return pl.pallas_call(
        body, grid=(S // tr,),
        in_specs=[pl.BlockSpec((tr, MEM_W), lambda i: (i, qblk)), pl.BlockSpec((MEM_LEN, 2 * MEM_W), lambda i: (0, 0)), HBM],
        out_specs=pl.BlockSpec((tr, MEM_W), lambda i: (i, Q_W // MEM_W)),
        out_shape=jax.ShapeDtypeStruct((S, MIX_OUT_W), BF16), input_output_aliases={2: 0},
        name=name, compiler_params=_cp(("parallel",)))(p, kv, cat)


def _memattn_bwd(p, qblk, kv, dcat, *, name="memattn_bwd", tr=512, rider=None):
    S = p.shape[0]
    tr = min(tr, S)
    scale = MEM_HEAD_DIM ** -0.5
    host = _Hosted(rider, 3, 2)

    def body(*refs):
        ins, outs, _, rrefs = host.split(refs, 3, 2)
        host.run(rrefs, pl.program_id(0), S // tr, lambda: inner(*ins, *outs))

    def inner(q_ref, kv_ref, do_ref, dq_ref, dkv_ref):
        @pl.when(pl.program_id(0) == 0)
        def _():
            dkv_ref[...] = jnp.zeros_like(dkv_ref)

        for h in range(MEM_HEADS):
            qh, pr = _mem_probs(q_ref, kv_ref, h)
            doh = do_ref[:, h * LANES:(h + 1) * LANES]
            dp = lax.dot_general(doh, kv_ref[:, MEM_W + h * LANES:MEM_W + (h + 1) * LANES], NT, preferred_element_type=F32)
            delta = jnp.sum(pr * dp, axis=-1, keepdims=True)
            dsb = (pr * (dp - delta) * scale).astype(BF16)
            dq = jnp.dot(dsb, kv_ref[:, h * LANES:(h + 1) * LANES], preferred_element_type=F32)
            dq_ref[:, h * LANES:(h + 1) * LANES] = dq.astype(BF16)
            dkv_ref[:, h * LANES:(h + 1) * LANES] += lax.dot_general(dsb, qh, TN, preferred_element_type=F32)
            dkv_ref[:, MEM_W + h * LANES:MEM_W + (h + 1) * LANES] += lax.dot_general(
                pr.astype(BF16), doh, TN, preferred_element_type=F32)

    return pl.pallas_call(
        body, grid=(S // tr,),
        in_specs=[pl.BlockSpec((tr, MEM_W), lambda i: (i, qblk)), pl.BlockSpec((MEM_LEN, 2 * MEM_W), lambda i: (0, 0)),
                  pl.BlockSpec((tr, MEM_W), lambda i: (i, Q_W // MEM_W))] + host.in_specs,
        out_specs=[pl.BlockSpec((tr, MEM_W), lambda i: (i, 0)), pl.BlockSpec((MEM_LEN, 2 * MEM_W), lambda i: (0, 0))]
        + host.out_specs,
        out_shape=[jax.ShapeDtypeStruct((S, MEM_W), BF16), jax.ShapeDtypeStruct((MEM_LEN, 2 * MEM_W), F32)] + host.out_shape,
        scratch_shapes=host.scratch, input_output_aliases=host.alias,
        name=name, compiler_params=_cp(("arbitrary",)))(p, kv, dcat, *host.args)


def _sqrt(v):
    return jnp.where(v > 0.0, v * lax.rsqrt(v), 0.0)


def _sigmoid(z):
    return 1.0 / (1.0 + jnp.exp(-z))


def _one_minus_exp(z, exp_z):
    poly = z * (1.0 + z * (0.5 + z * (1.0 / 6.0 + z * (1.0 / 24.0 + z * (1.0 / 120.0)))))
    return jnp.where(z > -0.1, -poly, 1.0 - exp_z)


def _softplus_neg(lam):
    z = -lam
    return jnp.maximum(z, 0.0) + jnp.log(1.0 + jnp.exp(-jnp.abs(z)))


_GELU_C = math.sqrt(2.0 / math.pi)


def _gelu(z):
    return 0.5 * z * (1.0 + jnp.tanh(_GELU_C * (z + 0.044715 * z * z * z)))


def _row_or_zero(ref, t, S):
    ok = jnp.logical_and(t >= 0, t < S)
    return jnp.where(ok, ref[pl.ds(jnp.clip(t, 0, S - 1), 1), :], 0.0)


def _shift_down(v, first):
    ri = lax.broadcasted_iota(jnp.int32, v.shape, 0)
    return jnp.where(ri == 0, first, pltpu.roll(v, 1, 0))


def _shift_up(v, last):
    T = v.shape[0]
    ri = lax.broadcasted_iota(jnp.int32, v.shape, 0)
    return jnp.where(ri == T - 1, last, pltpu.roll(v, T - 1, 0))


def _scan_chunk(a, u, reverse):
    T = a.shape[0]
    ri = lax.broadcasted_iota(jnp.int32, a.shape, 0)
    d = 1
    while d < T:
        if reverse:
            a_s, u_s, ok = pltpu.roll(a, T - d, 0), pltpu.roll(u, T - d, 0), ri < T - d
        else:
            a_s, u_s, ok = pltpu.roll(a, d, 0), pltpu.roll(u, d, 0), ri >= d
        u = jnp.where(ok, a * u_s + u, u)
        a = jnp.where(ok, a * a_s, a)
        d *= 2
    return a, u


def _conv_taps(xb_ref, t0, S):
    T = SCAN_ROWS
    x0 = xb_ref[pl.ds(t0, T), :]
    xm1 = _shift_down(x0, _row_or_zero(xb_ref, t0 - 1, S))
    nxt0 = _row_or_zero(xb_ref, t0 + T, S)
    xp1 = _shift_up(x0, nxt0)
    xp2 = _shift_up(xp1, _row_or_zero(xb_ref, t0 + T + 1, S))
    return xm1, x0, xp1, xp2


def _lru_gates(xc, w_a, b_a, w_x, b_x, sp):
    xcb = xc.astype(BF16)
    r = _sigmoid(jnp.dot(xcb, w_a, preferred_element_type=F32) + b_a)
    i = _sigmoid(jnp.dot(xcb, w_x, preferred_element_type=F32) + b_x)
    la = -LRU_C * r * sp
    a = jnp.exp(la)
    b2 = _one_minus_exp(2.0 * la, a * a)
    inv_beta = lax.rsqrt(b2)
    return r, i, a, jnp.where(b2 > 0.0, b2 * inv_beta, 0.0), inv_beta


def _lru_specs(S):
    col = lambda off: pl.BlockSpec((S, LANES), lambda n: (0, n + off), pipeline_mode=pl.Buffered(1))
    small = lambda r: pl.BlockSpec((r, LANES), lambda n: (0, n))
    wblk = pl.BlockSpec((2, 1, LANES, LANES), lambda n: (0, n, 0, 0))
    return col, small, wblk


def _lru_fwd(p, conv_w, conv_b, wa, ba, wx, bx, lam, *, name="lru_fwd"):
    S = p.shape[0]
    T = SCAN_ROWS
    nc = S // T

    def body(xb_ref, gate_ref, cw_ref, cb_ref, wa_ref, ba_ref, wx_ref, bx_ref, lam_ref, y_ref, hf_ref, hr_ref, xc_v):
        sp = _softplus_neg(lam_ref[...])
        cw = cw_ref[...]

        def fwd_step(c, h_in):
            t0 = pl.multiple_of(c * T, T)
            xm1, x0, xp1, xp2 = _conv_taps(xb_ref, t0, S)
            xc = cb_ref[...] + xm1 * cw[0:1] + x0 * cw[1:2] + xp1 * cw[2:3] + xp2 * cw[3:4]
            xc_v[pl.ds(t0, T), :] = xc
            _, i, a, beta, _ = _lru_gates(xc, wa_ref[0, 0], ba_ref[0:1], wx_ref[0, 0], bx_ref[0:1], sp[0:1])
            A, U = _scan_chunk(a, beta * (i * xc), False)
            hf_ref[pl.ds(t0, T), :] = A * h_in + U
            return hf_ref[pl.ds(t0 + T - 1, 1), :]

        lax.fori_loop(0, nc, fwd_step, jnp.zeros((1, LANES), F32))

        def rev_step(k, h_in):
            t0 = pl.multiple_of((nc - 1 - k) * T, T)
            xc = xc_v[pl.ds(t0, T), :]
            _, i, a, beta, _ = _lru_gates(xc, wa_ref[1, 0], ba_ref[1:2], wx_ref[1, 0], bx_ref[1:2], sp[1:2])
            A, U = _scan_chunk(a, beta * (i * xc), True)
            h = A * h_in + U
            hr_ref[pl.ds(t0, T), :] = h
            y_ref[pl.ds(t0, T), :] = ((hf_ref[pl.ds(t0, T), :] + h) * _gelu(gate_ref[pl.ds(t0, T), :])).astype(BF16)
            return hr_ref[pl.ds(t0, 1), :]

        lax.fori_loop(0, nc, rev_step, jnp.zeros((1, LANES), F32))

    col, small, wblk = _lru_specs(S)
    colo = lambda: pl.BlockSpec((S, LANES), lambda n: (0, n))
    return pl.pallas_call(
        body, grid=(LRU_BLOCKS,),
        in_specs=[col(0), col(LRU_BLOCKS), small(4), small(1), wblk, small(2), wblk, small(2), small(2)],
        out_specs=[colo(), colo(), colo()],
        out_shape=[jax.ShapeDtypeStruct((S, MIX_OUT_W), BF16), jax.ShapeDtypeStruct((S, D_MODEL), F32),
                   jax.ShapeDtypeStruct((S, D_MODEL), F32)],
        scratch_shapes=[pltpu.VMEM((S, LANES), F32)],
        name=name, compiler_params=_cp(("parallel",)))(p, p, conv_w, conv_b, wa, ba, wx, bx, lam)


def _lru_bwd(p, hf, hr, dcat, conv_w, conv_b, wa, ba, wx, bx, lam, *, name="lru_bwd"):
    S = p.shape[0]
    T = SCAN_ROWS
    nc = S // T

    def body(xb_ref, gate_ref, hf_ref, hr_ref, dy_ref, cw_ref, cb_ref, wa_ref, ba_ref, wx_ref, bx_ref, lam_ref,
             dxb_ref, dgate_ref, dcw_ref, dcb_ref, dwa_ref, dba_ref, dwx_ref, dbx_ref, dlam_ref, xc_v, dxc_v, dh_v):
        lam_v = lam_ref[...]
        sp = _softplus_neg(lam_v)
        cw = cw_ref[...]
        for ref in (dcw_ref, dcb_ref, dwa_ref, dba_ref, dwx_ref, dbx_ref, dlam_ref):
            ref[...] = jnp.zeros_like(ref)

        def prep_step(c, carry):
            t0 = pl.multiple_of(c * T, T)
            rows = pl.ds(t0, T)
            xm1, x0, xp1, xp2 = _conv_taps(xb_ref, t0, S)
            xc_v[rows, :] = cb_ref[...] + xm1 * cw[0:1] + x0 * cw[1:2] + xp1 * cw[2:3] + xp2 * cw[3:4]
            z = gate_ref[rows, :]
            dy = dy_ref[rows, :].astype(F32)
            th = jnp.tanh(_GELU_C * (z + 0.044715 * z * z * z))
            dgelu = 0.5 * (1.0 + th) + 0.5 * z * (1.0 - th * th) * _GELU_C * (1.0 + 3.0 * 0.044715 * z * z)
            dgate_ref[rows, :] = (dy * (hf_ref[rows, :] + hr_ref[rows, :]) * dgelu).astype(BF16)
            dh_v[rows, :] = dy * (0.5 * z * (1.0 + th))
            return carry

        lax.fori_loop(0, nc, prep_step, 0)

        def direction(d):
            h_ref = hf_ref if d == 0 else hr_ref
            w_a, w_x = wa_ref[d, 0], wx_ref[d, 0]
            b_a, b_x, sp_d = ba_ref[d:d + 1], bx_ref[d:d + 1], sp[d:d + 1]

            def step(k, carry):
                g_in, a_in = carry
                c = (nc - 1 - k) if d == 0 else k
                t0 = pl.multiple_of(c * T, T)
                rows = pl.ds(t0, T)
                xc = xc_v[rows, :]
                r, i, a, beta, inv_beta = _lru_gates(xc, w_a, b_a, w_x, b_x, sp_d)
                dh = dh_v[rows, :]
                hc = h_ref[rows, :]
                if d == 0:
                    A, U = _scan_chunk(_shift_up(a, a_in), dh, True)
                    g = A * g_in + U
                    h_nb = _shift_down(hc, _row_or_zero(h_ref, t0 - 1, S))
                    nxt = (g[0:1], a[0:1])
                else:
                    A, U = _scan_chunk(_shift_down(a, a_in), dh, False)
                    g = A * g_in + U
                    h_nb = _shift_up(hc, _row_or_zero(h_ref, t0 + T, S))
                    nxt = (g[T - 1:T], a[T - 1:T])
                da = g * h_nb
                dbeta = g * (i * xc)
                tb = g * beta
                dla = da * a - dbeta * (a * a * inv_beta)
                dzr = (dla * (-LRU_C * sp_d)) * (r * (1.0 - r))
                dzi = (tb * xc) * (i * (1.0 - i))
                dzrb, dzib, xcb = dzr.astype(BF16), dzi.astype(BF16), xc.astype(BF16)
                dwa_ref[d, 0] += lax.dot_general(xcb, dzrb, TN, preferred_element_type=F32)
                dwx_ref[d, 0] += lax.dot_general(xcb, dzib, TN, preferred_element_type=F32)
                dba_ref[d:d + 1] += jnp.sum(dzr, axis=0, keepdims=True)
                dbx_ref[d:d + 1] += jnp.sum(dzi, axis=0, keepdims=True)
                dlam_ref[d:d + 1] += jnp.sum(dla * (-LRU_C * r), axis=0, keepdims=True)
                dxc = (tb * i + lax.dot_general(dzrb, w_a, NT, preferred_element_type=F32)
                       + lax.dot_general(dzib, w_x, NT, preferred_element_type=F32))
                if d == 0:
                    dxc_v[rows, :] = dxc
                else:
                    dxc_v[rows, :] += dxc
                return nxt

            lax.fori_loop(0, nc, step, (jnp.zeros((1, LANES), F32), jnp.zeros((1, LANES), F32)))

        direction(0)
        direction(1)
        dlam_ref[...] = dlam_ref[...] * (-1.0 / (1.0 + jnp.exp(lam_v)))

        def conv_step(c, carry):
            t0 = pl.multiple_of(c * T, T)
            rows = pl.ds(t0, T)
            g0 = dxc_v[rows, :]
            gm1 = _shift_down(g0, _row_or_zero(dxc_v, t0 - 1, S))
            gm2 = _shift_down(gm1, _row_or_zero(dxc_v, t0 - 2, S))
            gp1 = _shift_up(g0, _row_or_zero(dxc_v, t0 + T, S))
            dxb_ref[rows, :] = (cw[0:1] * gp1 + cw[1:2] * g0 + cw[2:3] * gm1 + cw[3:4] * gm2).astype(BF16)
            xm1, x0, xp1, xp2 = _conv_taps(xb_ref, t0, S)
            for tap, xs in enumerate((xm1, x0, xp1, xp2)):
                dcw_ref[tap:tap + 1] += jnp.sum(g0 * xs, axis=0, keepdims=True)
            dcb_ref[...] += jnp.sum(g0, axis=0, keepdims=True)
            return carry

        lax.fori_loop(0, nc, conv_step, 0)

    col, small, wblk = _lru_specs(S)
    colo = lambda: pl.BlockSpec((S, LANES), lambda n: (0, n), pipeline_mode=pl.Buffered(1))
    return pl.pallas_call(
        body, grid=(LRU_BLOCKS,),
        in_specs=[col(0), col(LRU_BLOCKS), col(0), col(0), col(0), small(4), small(1), wblk, small(2), wblk, small(2), small(2)],
        out_specs=[colo(), colo(), small(4), small(1), wblk, small(2), wblk, small(2), small(2)],
        out_shape=[jax.ShapeDtypeStruct((S, D_MODEL), BF16), jax.ShapeDtypeStruct((S, D_MODEL), BF16),
                   jax.ShapeDtypeStruct((4, D_MODEL), F32), jax.ShapeDtypeStruct((1, D_MODEL), F32),
                   jax.ShapeDtypeStruct((2, LRU_BLOCKS, LANES, LANES), F32), jax.ShapeDtypeStruct((2, D_MODEL), F32),
                   jax.ShapeDtypeStruct((2, LRU_BLOCKS, LANES, LANES), F32), jax.ShapeDtypeStruct((2, D_MODEL), F32),
                   jax.ShapeDtypeStruct((2, D_MODEL), F32)],
        scratch_shapes=[pltpu.VMEM((S, LANES), F32), pltpu.VMEM((S, LANES), F32), pltpu.VMEM((S, LANES), F32)],
        name=name, compiler_params=_cp(("parallel",)))(p, p, hf, hr, dcat, conv_w, conv_b, wa, ba, wx, bx, lam)


PK_UP, PK_DOWN, PK_KV, PK_OUT, PK_IN = 0, 1024, 2048, 2304, 2688
PK_ROWS = {0: PK_IN, 1: PK_IN + 640}
SMALL_G_ROWS = 192
PKF_KV, PKF_SMALL = 512, 768
PKF_ROWS = PKF_SMALL + SMALL_G_ROWS


def _mlp_bwd(x, dx, dxb, saved, w_up, w_down, gain, l, rider=None, next_rider=None):
    up, act, h = saved
    pk = _mm_tn(act, dxb, 1, name=f"dw_down{l}", packed=(None, PK_ROWS[l], PK_DOWN), rider=rider)
    pk, carried = pk if rider is not None else (pk, None)
    dup = _mm_nt(dxb, w_down, up=up, name=f"d_up{l}")
    rider_up = next_rider(carried) if next_rider is not None else None
    pk = _mm_tn(h, dup, N_CHIPS, name=f"dw_up{l}", packed=(pk, PK_ROWS[l], PK_UP), rider=rider_up)
    pk, carried = pk if rider_up is not None else (pk, carried)
    dx, dxb, g_gain = _mm_nt(dup, w_up, norm_x=x, norm_g=gain, dres=dx, name=f"d_mlp_in{l}")
    return dx, dxb, pk, g_gain, carried


def _reduce_first(pk, place, tag, recv=None):
    if recv is None:
        recv = _sibling_exchange(pk, name=f"grad_sibling_exchange{tag}")
    return _sum_halves(pk, recv, place, name=f"sum_halves{tag}", tr=pk.shape[1] // 4)


def _sum_parts(parts, place, tag):
    return _sum_chips(parts, place, name=f"sum_chips{tag}", tr=parts.shape[1] // 2)


def _reduce_last(parts, place, tag):
    return _sibling_allgather(_sum_parts(parts, place, tag), name=f"grad_sibling_allgather{tag}")


def _local_step(x, mem, positions, target, W, pending=None, place=None):
    cos_t, sin_t = _rope_tables(positions)
    sinks = W["attn_sinks"].reshape(ATTN_HEADS)
    G = {}

    def hosting(late, fn, *args, **kw):
        if pending is None:
            return fn(*args, **kw)
        *res, buf = fn(*args, gather=pending[late], **kw)
        if late.startswith("w_down"):
            W.setdefault("w_down", [None] * DEPTH)[int(late[-1])] = _ready(late, buf)
        elif late == "w_up":
            W["w_up"], W["w_mem_kv"] = _ready(late, buf)
        else:
            W[late] = _ready(late, buf)
        return res if len(res) > 1 else res[0]

    p0, h0 = hosting("w_out", _mm_nn, x, W["attn_w_in"], norm_g=W["mix_norm"][0], name="attn_in")
    q, kd, vd, va = hosting("w_down0", _qk_prep, p0, cos_t, sin_t)
    ao, lse = hosting("w_up", _attn_fwd, q, kd, va, sinks)
    kv0, memn = _mm_nn(mem, W["w_mem_kv"][0], norm_g=W["mem_norm"], out_dtype=BF16, name="mem_kv0", tm=256)
    kv1 = _mm_nn(memn, W["w_mem_kv"][1], out_dtype=BF16, name="mem_kv1", tm=256)
    cat0 = _memattn_fwd(p0, Q_W // MEM_W + 1, kv0, ao, name="memattn_fwd0")
    x1 = hosting("lru_w_in", _mm_nn, cat0, W["w_out"][0], resid=x, name="mix_out0")
    up0, act0, h1 = hosting("w_down1", _mm_nn, x1, W["w_up"][0], norm_g=W["mlp_norm"][0], relu2=True, name="mlp_up0")
    x2, mlp0 = _mm_nn(act0, W["w_down"][0], resid=x1, name="mlp_down0"), (up0, act0, h1)
    p1, h2 = _mm_nn(x2, W["lru_w_in"], norm_g=W["mix_norm"][1], name="lru_in")
    lru_w = (W["lru_conv_w"], W["lru_conv_b"], W["lru_wa"], W["lru_ba"], W["lru_wx"], W["lru_bx"], W["lru_lambda"])
    y, hf, hr = _lru_fwd(p1, *lru_w)
    cat1 = _memattn_fwd(p1, 2 * D_MODEL // MEM_W, kv1, y, name="memattn_fwd1")
    x3 = _mm_nn(cat1, W["w_out"][1], resid=x2, name="mix_out1")
    mlp1 = _mm_nn(x3, W["w_up"][1], norm_g=W["mlp_norm"][1], relu2=True, name="mlp_up1")
    loss, dx, dxb, G["final_norm"] = _final(mlp1[1], W["w_down"][1], x3, W["final_norm"], target)

    def put(pk, off, g):
        return pk.at[:, off:off + g.size // (N_CHIPS * ROW)].set(g.reshape(N_CHIPS, -1, ROW))

    dx, dxb, pk1, gm1, _ = _mlp_bwd(x3, dx, dxb, mlp1, W["w_up"][1], W["w_down"][1], W["mlp_norm"][1], 1)
    pk1 = _mm_tn(cat1, dxb, 1, name="dw_out1", tk=384, packed=(pk1, PK_ROWS[1], PK_OUT))
    dcat1 = _mm_nt(dxb, W["w_out"][1], name="d_mix1")
    dmq1, dkv1 = _memattn_bwd(p1, 2 * D_MODEL // MEM_W, kv1, dcat1, name="memattn_bwd1")
    dkv1b = dkv1.astype(BF16)
    pk1 = _mm_tn(memn, dkv1b, 1, name="dw_kv1", tm=256, tk=256, packed=(pk1, PK_ROWS[1], PK_KV))
    (dxb1, dgate, G["lru_conv_w"], G["lru_conv_b"], G["lru_wa"], G["lru_ba"], G["lru_wx"], G["lru_bx"],
     G["lru_lambda"]) = _lru_bwd(p1, hf, hr, dcat1, *lru_w)
    dp1 = jnp.concatenate([dxb1, dgate, dmq1], axis=1)
    pk1 = put(pk1, PK_IN, _mm_tn(h2, dp1, N_CHIPS, name="dw_lru_in"))
    dx, dxb, gx1 = _mm_nt(dp1, W["lru_w_in"], norm_x=x2, norm_g=W["mix_norm"][1], dres=dx, name="d_lru_in")
    dist = place is not None
    h1_rows = PK_ROWS[1] // 4
    kept = {}

    def first_half(recv1):
        kept["halves1"], landing1 = _reduce_first(pk1, place, "1", recv1)
        return _exchange_rider((kept["halves1"], landing1, 0, h1_rows))

    dx, dxb, pk0, gm0, landing1 = _mlp_bwd(x1, dx, dxb, mlp0, W["w_up"][0], W["w_down"][0], W["mlp_norm"][0], 0,
                                           rider=_sib_exchange_rider(pk1) if dist else None,
                                           next_rider=first_half if dist else None)
    pk0 = _mm_tn(cat0, dxb, 1, name="dw_out0", tk=384, packed=(pk0, PK_ROWS[0], PK_OUT))
    pk0 = pk0.at[:, PK_KV:PK_OUT].set(0.0)
    dcat0 = _mm_nt(dxb, W["w_out"][0], name="d_mix0")
    dmq0, dkv0, *recv0 = _memattn_bwd(p0, Q_W // MEM_W + 1, kv0, dcat0, name="memattn_bwd0",
                                      rider=_sib_exchange_rider(pk0) if dist else None)
    dkv0b = dkv0.astype(BF16)
    g_kv0 = _mm_tn(memn, dkv0b, 1, name="dw_kv0", tm=256, tk=256)
    rider = None
    if dist:
        halves0, landing0 = _reduce_first(pk0, place, "0", recv0[0])
        rider = _exchange_rider((kept["halves1"], landing1, h1_rows, h1_rows), (halves0, landing0, 0, halves0.shape[1]))
    dq, dk, dv, dsink, *parts = _attn_bwd(q, kd, vd, cat0, lse, sinks, dcat0, rider=rider)
    dp0 = _qk_prep_bwd(dq, dk, dv, dmq0, cos_t, sin_t)
    g_in = _mm_tn(h0, dp0, N_CHIPS, name="dw_attn_in",
                  rider=_sib_allgather_rider(_sum_parts(parts[0], place, "1"), _sum_parts(parts[1], place, "0")) if dist else None)
    if dist:
        g_in, pk1, pk0 = g_in
    dx, _, gx0 = _mm_nt(dp0, W["attn_w_in"], norm_x=x, norm_g=W["mix_norm"][0], dres=dx, name="d_attn_in")

    w_kv_both = jnp.concatenate([W["w_mem_kv"][0], W["w_mem_kv"][1]], axis=0)
    _, _, G["mem_norm"] = _mm_nt(jnp.concatenate([dkv0b, dkv1b], axis=1), w_kv_both, norm_x=mem, norm_g=W["mem_norm"],
                                 name="d_mem", tm=256)

    G["mix_norm"] = jnp.concatenate([gx0, gx1], axis=0)
    G["mlp_norm"] = jnp.concatenate([gm0, gm1], axis=0)
    G["attn_sinks"] = dsink[0:1, 0:ATTN_HEADS]
    small = _flat_pad(_small_grad_list(G), N_CHIPS * SMALL_G_ROWS * ROW).reshape(N_CHIPS, SMALL_G_ROWS, ROW)
    pkf = jnp.concatenate([g_in.reshape(N_CHIPS, PKF_KV, ROW), g_kv0.reshape(N_CHIPS, PKF_SMALL - PKF_KV, ROW), small], axis=1)
    return loss[0, 0], dx, G, pkf, pk0, pk1


def _comm_call(body, out_shape, n_sems, name, *args, alias=None):
    return pl.pallas_call(
        body, out_shape=out_shape, in_specs=[HBM] * len(args), out_specs=HBM,
        scratch_shapes=[pltpu.SemaphoreType.DMA((n_sems,)), pltpu.SemaphoreType.DMA((n_sems,))],
        input_output_aliases=alias or {}, name=name)(*args)


def _place_slot(shard, slot, n_slots, *, name, tr):
    R, C = shard.shape

    def body(s_ref, a_ref, o_ref):
        o_ref[0] = a_ref[...]

    return pl.pallas_call(
        body,
        grid_spec=pltpu.PrefetchScalarGridSpec(
            num_scalar_prefetch=1, grid=(R // tr,), in_specs=[pl.BlockSpec((tr, C), lambda i, s_ref: (i, 0))],
            out_specs=pl.BlockSpec((1, tr, C), lambda i, s_ref: (s_ref[0], i, 0))),
        out_shape=jax.ShapeDtypeStruct((n_slots, R, C), shard.dtype), name=name,
        compiler_params=_cp(("parallel",)))(slot, shard)


def _allgather_chips(buf, *, name):
    def body(b_ref, o_ref, send_sems, recv_sems):
        _gather_start(o_ref, send_sems, recv_sems)
        _gather_finish(o_ref, send_sems, recv_sems)

    return _comm_call(body, jax.ShapeDtypeStruct(buf.shape, buf.dtype), GATHER_SEMS, name, buf, alias={0: 0})


def _sibling_exchange(g, *, name):
    _, R, C = g.shape
    half = R // 2

    def body(g_ref, o_ref, send_sems, recv_sems):
        _sib_exchange_start(g_ref, o_ref, send_sems, recv_sems)
        _sib_exchange_finish(g_ref, o_ref, send_sems, recv_sems)

    return _comm_call(body, jax.ShapeDtypeStruct((N_CHIPS, half, C), g.dtype), N_CHIPS, name, g)


def _chip_exchange(h, parts, *, name):
    def body(h_ref, p_ref, o_ref, send_sems, recv_sems):
        x, y, c, chips = _place()
        me = 2 * x + y
        cps = [_remote(h_ref.at[2 * cx + cy], o_ref.at[me], send_sems, recv_sems, j, (cx, cy, c))
               for j, (cx, cy) in enumerate(chips)]
        for cp in cps:
            cp.start()
        for j, (cx, cy) in enumerate(chips):
            got = o_ref.at[2 * cx + cy]
            _remote(got, got, send_sems, recv_sems, j, (cx, cy, c)).wait_recv()
        for cp in cps:
            cp.wait_send()

    return _comm_call(body, jax.ShapeDtypeStruct(parts.shape, parts.dtype), 3, name, h, parts, alias={1: 0})


def _sibling_allgather(full, *, name):
    def body(f_ref, o_ref, send_sems, recv_sems):
        _sib_allgather_start(o_ref, send_sems, recv_sems)
        _sib_allgather_finish(o_ref, send_sems, recv_sems)

    return _comm_call(body, jax.ShapeDtypeStruct(full.shape, full.dtype), 1, name, full, alias={0: 0})


def _sum_halves(g, recv, place, *, name="sum_halves", tr=480):
    _, R, C = g.shape
    half = R // 2
    nblk = half // tr

    def body(pl_ref, g_ref, r_ref, o_ref, own_ref):
        v = (g_ref[...] + r_ref[...]).astype(BF16)
        o_ref[...] = v

        @pl.when(pl.program_id(1) == pl_ref[1])
        def _():
            own_ref[...] = v

    blk = pl.BlockSpec((1, tr, C), lambda i, s, p: (s, i, 0))
    return pl.pallas_call(
        body,
        grid_spec=pltpu.PrefetchScalarGridSpec(
            num_scalar_prefetch=1, grid=(nblk, N_CHIPS),
            in_specs=[pl.BlockSpec((1, tr, C), lambda i, s, p: (s, p[0] * nblk + i, 0)), blk],
            out_specs=[blk, pl.BlockSpec((1, tr, C), lambda i, s, p: (p[1], i, 0))]),
        out_shape=[jax.ShapeDtypeStruct((N_CHIPS, half, C), BF16)] * 2, name=name,
        compiler_params=_cp(("parallel", "arbitrary")))(place, g, recv)


def _sum_chips(parts, place, *, name="sum_chips", tr=480):
    _, R, C = parts.shape
    nblk = R // tr

    def body(pl_ref, p_ref, o_ref):
        acc = p_ref[0].astype(F32) + p_ref[1].astype(F32)
        o_ref[...] = (acc + p_ref[2].astype(F32)) + p_ref[3].astype(F32)

    return pl.pallas_call(
        body,
        grid_spec=pltpu.PrefetchScalarGridSpec(
            num_scalar_prefetch=1, grid=(nblk,), in_specs=[pl.BlockSpec((N_CHIPS, tr, C), lambda i, p: (0, i, 0))],
            out_specs=pl.BlockSpec((tr, C), lambda i, p: (p[0] * nblk + i, 0))),
        out_shape=jax.ShapeDtypeStruct((2 * R, C), F32), name=name, compiler_params=_cp(("parallel",)))(place, parts)


def _adamw(w, g, m, v, *, name, tr=128):
    R, C = w.shape
    bc1 = 1.0 - ADAM_B1 ** ADAM_STEP
    bc2 = 1.0 - ADAM_B2 ** ADAM_STEP

    def body(w_ref, g_ref, m_ref, v_ref, d_ref, nm_ref, nv_ref):
        gv = g_ref[...]
        nm = ADAM_B1 * m_ref[...] + (1.0 - ADAM_B1) * gv
        nv = ADAM_B2 * v_ref[...] + (1.0 - ADAM_B2) * (gv * gv)
        d_ref[...] = -ADAM_LR * ((nm / bc1) / (_sqrt(nv / bc2) + ADAM_EPS) + ADAM_WD * w_ref[...])
        nm_ref[...] = nm
        nv_ref[...] = nv

    blk = pl.BlockSpec((tr, C), lambda i: (i, 0))
    return pl.pallas_call(
        body, grid=(R // tr,), in_specs=[blk] * 4, out_specs=[blk] * 3,
        out_shape=[jax.ShapeDtypeStruct((R, C), F32)] * 3, name=name, compiler_params=_cp(("parallel",)))(w, g, m, v)


ROW = 1024
BIG = ("w_mem_kv", "w_out", "w_up", "w_down", "attn_w_in", "lru_w_in")
SMALL_SHARDED = ("lru_conv_w", "lru_conv_b", "lru_ba", "lru_bx", "lru_lambda")
REPLICATED = ("mix_norm", "mlp_norm", "mem_norm", "final_norm", "attn_sinks", "lru_wa", "lru_wx")
SMALL = REPLICATED + SMALL_SHARDED
WEIGHTS = ("mix_norm", "mlp_norm", "mem_norm", "final_norm", "w_mem_kv", "w_out", "w_up", "w_down", "attn_w_in",
           "attn_sinks", "lru_w_in", "lru_conv_w", "lru_conv_b", "lru_wa", "lru_ba", "lru_wx", "lru_bx", "lru_lambda")
SMALL_W_ROWS = 32
ADAM_SMALL_ROWS = 640


def _rows(a):
    return a.reshape(-1, ROW)


def _flat_pad(parts, total):
    flat = jnp.concatenate([p.reshape(-1) for p in parts])
    return jnp.pad(flat, (0, total - flat.shape[0]))


def _pad_rows(a):
    flat = a.reshape(-1)
    n = -(-flat.shape[0] // ROW) * ROW
    return jnp.pad(flat, (0, n - flat.shape[0])).reshape(-1, ROW)


LATE = ("w_out", "w_down0", "w_up", "lru_w_in", "w_down1")


def _ready(name, full):
    if name == "w_out":
        wo = full.reshape(N_CHIPS, DEPTH, -1, D_MODEL)
        return [wo[:, l].reshape(1, MIX_OUT_W, D_MODEL) for l in range(DEPTH)]
    if name == "w_up":
        n_up = DEPTH * D_MODEL
        wu = full[:, :n_up].reshape(N_CHIPS, DEPTH, D_MODEL, D_FF // N_CHIPS)
        kv = full[:, n_up:].reshape(N_CHIPS, DEPTH, -1, D_MODEL)
        return [wu[:, l] for l in range(DEPTH)], [kv[:, l].reshape(1, D_MODEL, D_MODEL) for l in range(DEPTH)]
    if name == "lru_w_in":
        return full.reshape(N_CHIPS, D_MODEL, LRU_IN_W // N_CHIPS)
    return full.reshape(1, D_FF, D_MODEL)


def _gather_weights(P, chip1):
    bf = lambda a: _rows(a.astype(BF16))
    small = _flat_pad([P[n] for n in SMALL_SHARDED], SMALL_W_ROWS * ROW // 2)
    small_bits = lax.bitcast_convert_type(small, BF16).reshape(SMALL_W_ROWS, ROW)
    early = jnp.concatenate([bf(P["attn_w_in"]), small_bits], axis=0)
    n_in = P["attn_w_in"].size // ROW
    placed = _place_slot(early, chip1, N_CHIPS, name="place_weights", tr=early.shape[0] // 2)
    full = _allgather_chips(placed, name="allgather_weights")
    late = {"w_out": bf(P["w_out"]), "lru_w_in": bf(P["lru_w_in"]),
            "w_up": jnp.concatenate([bf(P["w_up"]), bf(P["w_mem_kv"])], axis=0),
            "w_down0": bf(P["w_down"][0]), "w_down1": bf(P["w_down"][1])}
    pending = {n: _place_slot(late[n], chip1, N_CHIPS, name=f"place_{n}", tr=late[n].shape[0] // 2) for n in LATE}
    W = {n: P[n] for n in REPLICATED}
    W["attn_w_in"] = full[:, :n_in].reshape(N_CHIPS, D_MODEL, ATTN_IN_W // N_CHIPS)
    sm = lax.bitcast_convert_type(full[:, n_in:].reshape(N_CHIPS, -1, 2), F32)
    o = 0
    for n in SMALL_SHARDED:
        shp = P[n].shape[1:]
        cnt = math.prod(shp)
        piece = sm[:, o:o + cnt].reshape((N_CHIPS,) + shp)
        piece = jnp.moveaxis(piece, 0, -2)
        W[n] = piece.reshape(shp[:-1] + (N_CHIPS * shp[-1],)).reshape(-1, D_MODEL)
        o += cnt
    W["lru_wa"] = P["lru_wa"][0].astype(BF16)
    W["lru_wx"] = P["lru_wx"][0].astype(BF16)
    return W, pending


def _small_grad_list(G):
    return [G["mix_norm"], G["mlp_norm"], G["mem_norm"], G["final_norm"], jnp.pad(G["attn_sinks"].reshape(-1), (0, ROW - ATTN_HEADS)),
            G["lru_wa"], G["lru_wx"], G["lru_conv_w"], G["lru_conv_b"], G["lru_ba"], G["lru_bx"], G["lru_lambda"]]


SMALL_G_SIZES = (2 * D_MODEL, 2 * D_MODEL, D_MODEL, D_MODEL, ROW, 2 * 8 * 128 * 128, 2 * 8 * 128 * 128,
                 4 * D_MODEL, D_MODEL, 2 * D_MODEL, 2 * D_MODEL, 2 * D_MODEL)


def _finish_grads(pkf, full0, full1, place, chip1):
    partsf = _chip_exchange(*_reduce_first(pkf, place, "f"), name="grad_chip_exchange_last")
    fullf = _reduce_last(partsf, place, "f")
    small_placed = _place_slot(fullf[PKF_SMALL:], chip1, N_CHIPS, name="place_small_grads", tr=SMALL_G_ROWS)
    small_all = _allgather_chips(small_placed, name="allgather_small_grads")
    flat = small_all.reshape(-1)
    small = {}
    o = 0
    names = ("mix_norm", "mlp_norm", "mem_norm", "final_norm", "attn_sinks", "lru_wa", "lru_wx",
             "lru_conv_w", "lru_conv_b", "lru_ba", "lru_bx", "lru_lambda")
    for n, cnt in zip(names, SMALL_G_SIZES):
        small[n] = flat[o:o + cnt]
        o += cnt
    both = lambda off, r: jnp.concatenate([full0[off:off + r], full1[off:off + r]], axis=0)
    big = {"w_up": both(PK_UP, 1024), "w_down": both(PK_DOWN, 1024), "w_out": both(PK_OUT, 384),
           "w_mem_kv": jnp.concatenate([fullf[PKF_KV:PKF_SMALL], full1[PK_KV:PK_OUT]], axis=0),
           "attn_w_in": fullf[:PKF_KV], "lru_w_in": full1[PK_IN:PK_IN + 640]}
    return big, small


def kernel(x, mem, positions, mix_norm, mlp_norm, mem_norm, final_norm, w_mem_kv, w_out, w_up, w_down, attn_w_in, attn_sinks, lru_w_in, lru_conv_w, lru_conv_b, lru_wa, lru_ba, lru_wx, lru_bx, lru_lambda, loss_target, m_mix_norm, m_mlp_norm, m_mem_norm, m_final_norm, m_w_mem_kv, m_w_out, m_w_up, m_w_down, m_attn_w_in, m_attn_sinks, m_lru_w_in, m_lru_conv_w, m_lru_conv_b, m_lru_wa, m_lru_ba, m_lru_wx, m_lru_bx, m_lru_lambda, v_mix_norm, v_mlp_norm, v_mem_norm, v_final_norm, v_w_mem_kv, v_w_out, v_w_up, v_w_down, v_attn_w_in, v_attn_sinks, v_lru_w_in, v_lru_conv_w, v_lru_conv_b, v_lru_wa, v_lru_ba, v_lru_wx, v_lru_bx, v_lru_lambda):
    P = dict(mix_norm=mix_norm, mlp_norm=mlp_norm, mem_norm=mem_norm, final_norm=final_norm, w_mem_kv=w_mem_kv, w_out=w_out,
             w_up=w_up, w_down=w_down, attn_w_in=attn_w_in, attn_sinks=attn_sinks, lru_w_in=lru_w_in, lru_conv_w=lru_conv_w,
             lru_conv_b=lru_conv_b, lru_wa=lru_wa, lru_ba=lru_ba, lru_wx=lru_wx, lru_bx=lru_bx, lru_lambda=lru_lambda)
    M1 = dict(mix_norm=m_mix_norm, mlp_norm=m_mlp_norm, mem_norm=m_mem_norm, final_norm=m_final_norm, w_mem_kv=m_w_mem_kv,
              w_out=m_w_out, w_up=m_w_up, w_down=m_w_down, attn_w_in=m_attn_w_in, attn_sinks=m_attn_sinks, lru_w_in=m_lru_w_in,
              lru_conv_w=m_lru_conv_w, lru_conv_b=m_lru_conv_b, lru_wa=m_lru_wa, lru_ba=m_lru_ba, lru_wx=m_lru_wx,
              lru_bx=m_lru_bx, lru_lambda=m_lru_lambda)
    V2 = dict(mix_norm=v_mix_norm, mlp_norm=v_mlp_norm, mem_norm=v_mem_norm, final_norm=v_final_norm, w_mem_kv=v_w_mem_kv,
              w_out=v_w_out, w_up=v_w_up, w_down=v_w_down, attn_w_in=v_attn_w_in, attn_sinks=v_attn_sinks, lru_w_in=v_lru_w_in,
              lru_conv_w=v_lru_conv_w, lru_conv_b=v_lru_conv_b, lru_wa=v_lru_wa, lru_ba=v_lru_ba, lru_wx=v_lru_wx,
              lru_bx=v_lru_bx, lru_lambda=v_lru_lambda)
    chip = 2 * lax.axis_index("x") + lax.axis_index("y")
    chip1 = chip.astype(jnp.int32).reshape(1)
    place = jnp.stack([lax.axis_index("c").astype(jnp.int32), chip.astype(jnp.int32)])

    W, pending = _gather_weights(P, chip1)
    loss, dx, _, pkf, full0, full1 = _local_step(x[0], mem[0], positions[0], loss_target[0], W, pending, place)
    loss = lax.psum(loss, ("x", "y", "c"))
    big, small = _finish_grads(pkf, full0, full1, place, chip1)

    grads, deltas, new_m, new_v = {}, {}, {}, {}
    for n in BIG:
        g = big[n]
        d, nm, nv = _adamw(_rows(P[n]), g, _rows(M1[n]), _rows(V2[n]), name=f"adamw_{n}")
        grads[n], deltas[n], new_m[n], new_v[n] = (t.reshape(P[n].shape) for t in (g, d, nm, nv))

    for n in SMALL:
        g = small[n]
        if n in SMALL_SHARDED:
            shard = P[n].shape[-1]
            g = lax.dynamic_slice_in_dim(g.reshape(-1, N_CHIPS * shard), chip * shard, shard, axis=1)
        elif n == "attn_sinks":
            g = g[:ATTN_HEADS]
        grads[n] = g.reshape(P[n].shape)
    packs = []
    for src in (P, grads, M1, V2):
        a = jnp.concatenate([_pad_rows(src[n]) for n in SMALL], axis=0)
        packs.append(jnp.pad(a, ((0, ADAM_SMALL_ROWS - a.shape[0]), (0, 0))))
    d_s, nm_s, nv_s = _adamw(*packs, name="adamw_small")
    o = 0
    for n in SMALL:
        cnt = math.prod(P[n].shape)
        r = -(-cnt // ROW)
        for dst, src in ((deltas, d_s), (new_m, nm_s), (new_v, nv_s)):
            dst[n] = src[o:o + r].reshape(-1)[:cnt].reshape(P[n].shape)
        o += r

    return (loss, dx[None], *[grads[n] for n in WEIGHTS], *[deltas[n] for n in WEIGHTS],
            *[new_m[n] for n in WEIGHTS], *[new_v[n] for n in WEIGHTS])
```

```python
import math

import jax
import jax.numpy as jnp
from jax import lax
from jax.experimental import pallas as pl
from jax.experimental.pallas import tpu as pltpu

F32 = jnp.float32
BF16 = jnp.bfloat16
MESH = pl.DeviceIdType.MESH

D_MODEL = 1024
DEPTH = 2
EPS = 1e-6
ATTN_HEADS = 16
ATTN_KV_HEADS = 4
HEAD_DIM = 64
WINDOW = 128
BLOCK = 128
ROPE_THETA = 500000.0
ROPE_DIM = 16
Q_W = 1024
KV_W = 256
MEM_LEN = 256
MEM_HEADS = 4
MEM_HEAD_DIM = 128
MEM_W = 512
LRU_BLOCKS = 8
LRU_C = 8.0
ATTN_IN_W = 2048
LRU_IN_W = 2560
MIX_OUT_W = 1536
D_FF = 4096
NEG = -1e30
N_CHIPS = 4

ADAM_LR = 0.001
ADAM_B1 = 0.9
ADAM_B2 = 0.999
ADAM_EPS = 1e-08
ADAM_WD = 0.01
ADAM_STEP = 10

LANES = 128
SCAN_ROWS = 512
VMEM_LIMIT = 56 * 1024 * 1024

NT = (((1,), (1,)), ((), ()))
TN = (((0,), (0,)), ((), ()))


def _cp(sem=None):
    return pltpu.CompilerParams(dimension_semantics=sem, vmem_limit_bytes=VMEM_LIMIT)


HBM = pl.BlockSpec(memory_space=pl.ANY)
GATHER_SEMS = 6


def _place():
    x, y, c = lax.axis_index("x"), lax.axis_index("y"), lax.axis_index("c")
    chips = [(1 - x, y), (x, 1 - y), (1 - x, 1 - y)]
    return x, y, c, chips


def _remote(src, dst, send_sems, recv_sems, k, to):
    return pltpu.make_async_remote_copy(src_ref=src, dst_ref=dst, send_sem=send_sems.at[k], recv_sem=recv_sems.at[k],
                                        device_id=to, device_id_type=MESH)


def _gather_start(o_ref, send_sems, recv_sems):
    x, y, c, chips = _place()
    half = o_ref.shape[1] // 2
    own = o_ref.at[2 * x + y, pl.ds(pl.multiple_of(c * half, 16), half)]
    for j, (cx, cy) in enumerate(chips):
        _remote(own, own, send_sems, recv_sems, j, (cx, cy, c)).start()


def _gather_forward(o_ref, send_sems, recv_sems):
    x, y, c, chips = _place()
    half = o_ref.shape[1] // 2
    my_rows = pl.ds(pl.multiple_of(c * half, 16), half)
    for j, (cx, cy) in enumerate(chips):
        landed = o_ref.at[2 * cx + cy, my_rows]
        _remote(landed, landed, send_sems, recv_sems, j, (cx, cy, c)).wait_recv()
        _remote(landed, landed, send_sems, recv_sems, 3 + j, (x, y, 1 - c)).start()


def _gather_drain(o_ref, send_sems, recv_sems):
    x, y, c, chips = _place()
    half = o_ref.shape[1] // 2
    my_rows = pl.ds(pl.multiple_of(c * half, 16), half)
    sib_rows = pl.ds(pl.multiple_of((1 - c) * half, 16), half)
    own = o_ref.at[2 * x + y, my_rows]
    for j, (cx, cy) in enumerate(chips):
        got = o_ref.at[2 * cx + cy, sib_rows]
        _remote(got, got, send_sems, recv_sems, 3 + j, (x, y, 1 - c)).wait_recv()
    for j, (cx, cy) in enumerate(chips):
        _remote(own, own, send_sems, recv_sems, j, (cx, cy, c)).wait_send()
        landed = o_ref.at[2 * cx + cy, my_rows]
        _remote(landed, landed, send_sems, recv_sems, 3 + j, (x, y, 1 - c)).wait_send()


def _gather_finish(o_ref, send_sems, recv_sems):
    _gather_forward(o_ref, send_sems, recv_sems)
    _gather_drain(o_ref, send_sems, recv_sems)


def _exchange_start(h_ref, o_ref, send_sems, recv_sems, rows=None, base=0):
    x, y, c, chips = _place()
    rows = pl.ds(0, h_ref.shape[1]) if rows is None else rows
    for j, (cx, cy) in enumerate(chips):
        _remote(h_ref.at[2 * cx + cy, rows], o_ref.at[2 * x + y, rows], send_sems, recv_sems, base + j, (cx, cy, c)).start()


def _exchange_finish(h_ref, o_ref, send_sems, recv_sems, rows=None, base=0):
    x, y, c, chips = _place()
    rows = pl.ds(0, h_ref.shape[1]) if rows is None else rows
    for j, (cx, cy) in enumerate(chips):
        got = o_ref.at[2 * cx + cy, rows]
        _remote(got, got, send_sems, recv_sems, base + j, (cx, cy, c)).wait_recv()
    for j, (cx, cy) in enumerate(chips):
        _remote(h_ref.at[2 * cx + cy, rows], o_ref.at[2 * x + y, rows], send_sems, recv_sems, base + j, (cx, cy, c)).wait_send()


def _sib_exchange_copies(g_ref, o_ref, send_sems, recv_sems):
    x, y, c, _ = _place()
    half = g_ref.shape[1] // 2
    other = pl.ds(pl.multiple_of((1 - c) * half, 8), half)
    return [_remote(g_ref.at[s, other], o_ref.at[s], send_sems, recv_sems, s, (x, y, 1 - c)) for s in range(N_CHIPS)]


def _sib_exchange_start(*refs):
    for cp in _sib_exchange_copies(*refs):
        cp.start()


def _sib_exchange_finish(*refs):
    for cp in _sib_exchange_copies(*refs):
        cp.wait()


def _sib_allgather_start(*refs):
    *o_refs, send_sems, recv_sems = refs
    x, y, c, _ = _place()
    for i, o_ref in enumerate(o_refs):
        half = o_ref.shape[0] // 2
        mine = o_ref.at[pl.ds(pl.multiple_of(c * half, 8), half)]
        _remote(mine, mine, send_sems, recv_sems, i, (x, y, 1 - c)).start()


def _sib_allgather_finish(*refs):
    *o_refs, send_sems, recv_sems = refs
    x, y, c, _ = _place()
    for i, o_ref in enumerate(o_refs):
        half = o_ref.shape[0] // 2
        mine = o_ref.at[pl.ds(pl.multiple_of(c * half, 8), half)]
        got = o_ref.at[pl.ds(pl.multiple_of((1 - c) * half, 8), half)]
        _remote(got, got, send_sems, recv_sems, i, (x, y, 1 - c)).wait_recv()
        _remote(mine, mine, send_sems, recv_sems, i, (x, y, 1 - c)).wait_send()


class _Rider:
    def __init__(self, args, start, finish, inplace=1, mid=None):
        self.args, self.start, self.finish, self.inplace, self.mid = list(args), start, finish, inplace, mid


def _gather_rider(buf):
    return None if buf is None else _Rider([buf], _gather_start, _gather_finish, mid=(_gather_forward, _gather_drain))


def _exchange_rider(*parts):
    n = len(parts)
    assert 3 * n <= GATHER_SEMS

    def run(fn):
        def go(*refs):
            sems = refs[2 * n:]
            for i, (_, _, r0, nr) in enumerate(parts):
                fn(refs[i], refs[n + i], *sems, rows=pl.ds(r0, nr), base=3 * i)
        return go

    return _Rider([p[0] for p in parts] + [p[1] for p in parts], run(_exchange_start), run(_exchange_finish), inplace=n)


def _sib_exchange_rider(g):
    landing = lax.empty((N_CHIPS, g.shape[1] // 2, g.shape[2]), g.dtype)
    return _Rider([g, landing], _sib_exchange_start, _sib_exchange_finish)


def _sib_allgather_rider(*fulls):
    return _Rider(fulls, _sib_allgather_start, _sib_allgather_finish, inplace=len(fulls))


class _Hosted:
    def __init__(self, rider, n_in, n_out):
        self.rider = rider
        self.on = rider is not None
        self.args = rider.args if self.on else []
        k = len(self.args)
        p = self.p = rider.inplace if self.on else 0
        self.alias = {n_in + k - p + i: n_out + i for i in range(p)}
        self.in_specs = [HBM] * k
        self.out_specs = [HBM] * p
        self.out_shape = [jax.ShapeDtypeStruct(a.shape, a.dtype) for a in self.args[k - p:]]
        self.scratch = [pltpu.SemaphoreType.DMA((GATHER_SEMS,)), pltpu.SemaphoreType.DMA((GATHER_SEMS,))] if self.on else []

    def split(self, refs, n_in, n_out):
        refs = list(refs)
        if not self.on:
            return refs[:n_in], refs[n_in:n_in + n_out], refs[n_in + n_out:], None
        k, p = len(self.args), self.p
        ins, outs = refs[:n_in], refs[n_in + k:n_in + k + n_out]
        rest = refs[n_in + k + n_out + p:]
        rrefs = refs[n_in:n_in + k - p] + refs[n_in + k + n_out:n_in + k + n_out + p] + [rest[-2], rest[-1]]
        return ins, outs, rest[:-2], rrefs

    def run(self, rrefs, step, n_steps, compute):
        if rrefs is None:
            return compute()

        mid = self.rider.mid
        mid_step = (3 * n_steps) // 4
        two_stage = mid is not None and 0 < mid_step < n_steps - 1

        @pl.when(step == 0)
        def _():
            self.rider.start(*rrefs)

        compute()

        if two_stage:
            @pl.when(step == mid_step)
            def _():
                mid[0](*rrefs)

        @pl.when(step == n_steps - 1)
        def _():
            (mid[1] if two_stage else self.rider.finish)(*rrefs)


def _mm_nn(a, w3, *, name, out_dtype=F32, norm_g=None, resid=None, relu2=False, tm=512, gather=None):
    M, K = a.shape
    ns, _, n = w3.shape
    N = ns * n
    tm = min(tm, M)
    has_norm = norm_g is not None
    has_res = resid is not None
    n_in = 2 + has_norm + has_res
    n_out = (2 if relu2 else 1) + has_norm
    host = _Hosted(_gather_rider(gather), n_in, n_out)

    def body(*refs):
        ins, outs, _, gref = host.split(refs, n_in, n_out)
        a_ref, w_ref = ins[0], ins[1]
        g_ref = ins[2] if has_norm else None
        r_ref = ins[-1] if has_res else None

        def compute():
            if has_norm:
                xv = a_ref[...]
                rs = lax.rsqrt(jnp.mean(xv * xv, axis=-1, keepdims=True) + EPS)
                ab = (xv * rs * g_ref[...]).astype(BF16)
                outs[-1][...] = ab
            else:
                ab = a_ref[...]
            for s in range(ns):
                acc = jnp.dot(ab, w_ref[s], preferred_element_type=F32)
                sl = slice(s * n, (s + 1) * n)
                if relu2:
                    outs[0][:, sl] = acc.astype(BF16)
                    rl = jnp.maximum(acc, 0.0)
                    outs[1][:, sl] = (rl * rl).astype(BF16)
                elif has_res:
                    outs[0][:, sl] = r_ref[:, sl] + acc
                else:
                    outs[0][:, sl] = acc.astype(out_dtype)

        host.run(gref, pl.program_id(0), M // tm, compute)

    row = lambda w: pl.BlockSpec((tm, w), lambda i: (i, 0))
    in_specs = [row(K), pl.BlockSpec((ns, K, n), lambda i: (0, 0, 0))]
    args = [a, w3]
    if has_norm:
        in_specs.append(pl.BlockSpec((1, K), lambda i: (0, 0)))
        args.append(norm_g.reshape(1, K))
    if has_res:
        in_specs.append(row(N))
        args.append(resid)
    if relu2:
        out_shape = [jax.ShapeDtypeStruct((M, N), BF16), jax.ShapeDtypeStruct((M, N), BF16)]
        out_specs = [row(N), row(N)]
    else:
        out_shape = [jax.ShapeDtypeStruct((M, N), F32 if has_res else out_dtype)]
        out_specs = [row(N)]
    if has_norm:
        out_shape.append(jax.ShapeDtypeStruct((M, K), BF16))
        out_specs.append(row(K))
    res = pl.pallas_call(body, grid=(M // tm,), in_specs=in_specs + host.in_specs, out_specs=out_specs + host.out_specs,
                         out_shape=out_shape + host.out_shape, scratch_shapes=host.scratch, input_output_aliases=host.alias,
                         name=name, compiler_params=_cp(("arbitrary",) if host.on else ("parallel",)))(*args, *host.args)
    return res if len(res) > 1 else res[0]


def _mm_nt(g, w3, *, name, out_dtype=BF16, up=None, norm_x=None, norm_g=None, dres=None, tm=512):
    M = g.shape[0]
    ns, K, n = w3.shape
    tm = min(tm, M)
    has_up = up is not None
    has_norm = norm_x is not None
    has_res = dres is not None

    def body(*refs):
        refs = list(refs)
        g_ref, w_ref = refs[0], refs[1]
        pos = 2
        if has_up:
            up_ref = refs[pos]
            pos += 1
        if has_norm:
            x_ref, gn_ref = refs[pos], refs[pos + 1]
            pos += 2
        if has_res:
            r_ref = refs[pos]
            pos += 1
        outs = refs[pos:]
        acc = None
        for s in range(ns):
            part = lax.dot_general(g_ref[:, s * n:(s + 1) * n], w_ref[s], NT, preferred_element_type=F32)
            acc = part if acc is None else acc + part
        if has_up:
            outs[0][...] = (acc * (2.0 * jnp.maximum(up_ref[...].astype(F32), 0.0))).astype(BF16)
        elif has_norm:
            xv = x_ref[...]
            rs = lax.rsqrt(jnp.mean(xv * xv, axis=-1, keepdims=True) + EPS)
            xn = xv * rs
            dxn = acc * gn_ref[...]
            dx = rs * (dxn - xn * jnp.mean(dxn * xn, axis=-1, keepdims=True))
            if has_res:
                dx = dx + r_ref[...]
            outs[0][...] = dx
            outs[1][...] = dx.astype(BF16)

            @pl.when(pl.program_id(0) == 0)
            def _():
                outs[2][...] = jnp.zeros_like(outs[2])

            outs[2][...] += jnp.sum(acc * xn, axis=0, keepdims=True)
        else:
            outs[0][...] = acc.astype(out_dtype)

    row = lambda w: pl.BlockSpec((tm, w), lambda i: (i, 0))
    in_specs = [row(ns * n), pl.BlockSpec((ns, K, n), lambda i: (0, 0, 0))]
    args = [g, w3]
    if has_up:
        in_specs.append(row(K))
        args.append(up)
    if has_norm:
        in_specs += [row(K), pl.BlockSpec((1, K), lambda i: (0, 0))]
        args += [norm_x, norm_g.reshape(1, K)]
    if has_res:
        in_specs.append(row(K))
        args.append(dres)
    if has_norm:
        out_shape = [jax.ShapeDtypeStruct((M, K), F32), jax.ShapeDtypeStruct((M, K), BF16),
                     jax.ShapeDtypeStruct((1, K), F32)]
        out_specs = [row(K), row(K), pl.BlockSpec((1, K), lambda i: (0, 0))]
        sem = ("arbitrary",)
    else:
        out_shape = [jax.ShapeDtypeStruct((M, K), BF16 if has_up else out_dtype)]
        out_specs = [row(K)]
        sem = ("parallel",)
    res = pl.pallas_call(body, grid=(M // tm,), in_specs=in_specs, out_specs=out_specs, out_shape=out_shape,
                         name=name, compiler_params=_cp(sem))(*args)
    return res if len(res) > 1 else res[0]


def _mm_tn(a, g, ns, *, name, tk=512, tm=4096, packed=None, rider=None):
    M, K = a.shape
    n = g.shape[1] // ns
    tm = min(tm, M)
    tk = min(tk, K)
    nk, nm = K // tk, M // tm
    n_in = 3 if (packed is not None and packed[0] is not None) else 2
    host = _Hosted(rider, n_in, 1)

    def body(*refs):
        ins, outs, _, rrefs = host.split(refs, n_in, 1)
        a_ref, g_ref, o_ref = ins[0], ins[1], outs[0]

        def compute():
            @pl.when(pl.program_id(2) == 0)
            def _():
                o_ref[...] = jnp.zeros_like(o_ref)

            o_ref[0] += lax.dot_general(a_ref[...], g_ref[...], TN, preferred_element_type=F32)

        step = (pl.program_id(0) * nk + pl.program_id(1)) * nm + pl.program_id(2)
        host.run(rrefs, step, ns * nk * nm, compute)

    in_specs = [pl.BlockSpec((tm, tk), lambda s, k, m: (m, k)), pl.BlockSpec((tm, n), lambda s, k, m: (m, s))]
    args = [a, g]
    alias = {}
    if packed is None:
        out_spec = pl.BlockSpec((1, tk, n), lambda s, k, m: (s, k, 0))
        out_shape = jax.ShapeDtypeStruct((ns, K, n), F32)
    else:
        buf, rows, off = packed
        per_chip = K * ns // N_CHIPS
        assert n == ROW and per_chip % tk == 0 and off % tk == 0
        if ns == N_CHIPS:
            out_spec = pl.BlockSpec((1, tk, n), lambda s, k, m: (s, off // tk + k, 0))
        else:
            kpc = per_chip // tk
            out_spec = pl.BlockSpec((1, tk, n), lambda s, k, m: (k // kpc, off // tk + k % kpc, 0))
        out_shape = jax.ShapeDtypeStruct((N_CHIPS, rows, ROW), F32)
        if buf is not None:
            in_specs.append(HBM)
            args.append(buf)
            alias = {2: 0}
    sem = ("arbitrary",) * 3 if host.on else ("parallel", "parallel", "arbitrary")
    res = pl.pallas_call(
        body, grid=(ns, nk, nm), in_specs=in_specs + host.in_specs, out_specs=[out_spec] + host.out_specs,
        out_shape=[out_shape] + host.out_shape, scratch_shapes=host.scratch, name=name,
        input_output_aliases={**alias, **host.alias}, compiler_params=_cp(sem))(*args, *host.args)
    return res if host.on else res[0]


def _final(act, w_down, x, gain, target, *, name="mlp_down_final", tr=512):
    S, Dm = x.shape
    tr = min(tr, S)
    Kf = act.shape[1]

    def body(a_ref, w_ref, x_ref, g_ref, t_ref, loss_ref, dx_ref, dxb_ref, dg_ref):
        @pl.when(pl.program_id(0) == 0)
        def _():
            loss_ref[...] = jnp.zeros_like(loss_ref)
            dg_ref[...] = jnp.zeros_like(dg_ref)

        xv = x_ref[...] + jnp.dot(a_ref[...], w_ref[0], preferred_element_type=F32)
        gv = g_ref[...]
        rs = lax.rsqrt(jnp.mean(xv * xv, axis=-1, keepdims=True) + EPS)
        xn = xv * rs
        err = xn * gv - t_ref[...]
        loss_ref[...] += 0.5 * jnp.sum(jnp.mean(err * err, axis=-1, keepdims=True), axis=0, keepdims=True)
        dout = err * (1.0 / Dm)
        dg_ref[...] += jnp.sum(dout * xn, axis=0, keepdims=True)
        dxn = dout * gv
        dx = rs * (dxn - xn * jnp.mean(dxn * xn, axis=-1, keepdims=True))
        dx_ref[...] = dx
        dxb_ref[...] = dx.astype(BF16)

    row = pl.BlockSpec((tr, Dm), lambda i: (i, 0))
    return pl.pallas_call(
        body, grid=(S // tr,),
        in_specs=[pl.BlockSpec((tr, Kf), lambda i: (i, 0)), pl.BlockSpec((1, Kf, Dm), lambda i: (0, 0, 0)), row,
                  pl.BlockSpec((1, Dm), lambda i: (0, 0)), row],
        out_specs=[pl.BlockSpec((1, 1), lambda i: (0, 0)), row, row, pl.BlockSpec((1, Dm), lambda i: (0, 0))],
        out_shape=[jax.ShapeDtypeStruct((1, 1), F32), jax.ShapeDtypeStruct((S, Dm), F32),
                   jax.ShapeDtypeStruct((S, Dm), BF16), jax.ShapeDtypeStruct((1, Dm), F32)],
        name=name, compiler_params=_cp(("arbitrary",)))(act, w_down, x, gain.reshape(1, Dm), target)


def _rope_tables(positions):
    half = ROPE_DIM // 2
    inv_freq = ROPE_THETA ** (-2.0 * jnp.arange(half, dtype=F32) / ROPE_DIM)
    ang = positions.astype(F32)[:, None] * inv_freq
    cos, sin = jnp.cos(ang), jnp.sin(ang)
    S = positions.shape[0]
    ones = jnp.ones((S, HEAD_DIM - ROPE_DIM), F32)
    cos64 = jnp.concatenate([cos, cos, ones], axis=1)
    sin64 = jnp.concatenate([-sin, sin, 0.0 * ones], axis=1)
    return jnp.tile(cos64, (1, 2)), jnp.tile(sin64, (1, 2))


def _rope_partner(t):
    lane = lax.broadcasted_iota(jnp.int32, t.shape, 1)
    low = (lane & (HEAD_DIM - 1)) < (ROPE_DIM // 2)
    return jnp.where(low, pltpu.roll(t, LANES - ROPE_DIM // 2, 1), pltpu.roll(t, ROPE_DIM // 2, 1))


def _qk_prep(p, cos_t, sin_t, *, name="qk_prep", tr=256, gather=None):
    S = p.shape[0]
    tr = min(tr, S)
    scale = HEAD_DIM ** -0.5
    host = _Hosted(_gather_rider(gather), 3, 4)

    def body(*refs):
        ins, outs, _, gref = host.split(refs, 3, 4)
        host.run(gref, pl.program_id(0), S // tr, lambda: inner(*ins, *outs))

    def inner(p_ref, c_ref, s_ref, q_ref, k_ref, v_ref, va_ref):
        cs, sn = c_ref[...], s_ref[...]
        lane = lax.broadcasted_iota(jnp.int32, (tr, LANES), 1)
        lo = lane < HEAD_DIM
        for c in range(Q_W // LANES):
            t = p_ref[:, c * LANES:(c + 1) * LANES]
            q_ref[:, c * LANES:(c + 1) * LANES] = ((t * cs + _rope_partner(t) * sn) * scale).astype(BF16)
        for c in range(KV_W // LANES):
            t = p_ref[:, Q_W + c * LANES:Q_W + (c + 1) * LANES]
            kc = t * cs + _rope_partner(t) * sn
            vc = p_ref[:, Q_W + KV_W + c * LANES:Q_W + KV_W + (c + 1) * LANES]
            for arr, ref in ((kc, k_ref), (vc, v_ref)):
                sw = pltpu.roll(arr, HEAD_DIM, 1)
                ref[:, (2 * c) * LANES:(2 * c + 1) * LANES] = jnp.where(lo, arr, sw).astype(BF16)
                ref[:, (2 * c + 1) * LANES:(2 * c + 2) * LANES] = jnp.where(lo, sw, arr).astype(BF16)
            sw = pltpu.roll(vc, HEAD_DIM, 1)
            for k, aug in enumerate((jnp.where(lo, vc, 1.0), jnp.where(lo, 1.0, sw), jnp.where(lo, sw, 1.0), jnp.where(lo, 1.0, vc))):
                va_ref[:, (4 * c + k) * LANES:(4 * c + k + 1) * LANES] = aug.astype(BF16)

    row = lambda w: pl.BlockSpec((tr, w), lambda i: (i, 0))
    return pl.pallas_call(
        body, grid=(S // tr,), in_specs=[row(ATTN_IN_W), row(LANES), row(LANES)] + host.in_specs,
        out_specs=[row(Q_W), row(2 * KV_W), row(2 * KV_W), row(4 * KV_W)] + host.out_specs,
        out_shape=[jax.ShapeDtypeStruct((S, Q_W), BF16), jax.ShapeDtypeStruct((S, 2 * KV_W), BF16),
                   jax.ShapeDtypeStruct((S, 2 * KV_W), BF16), jax.ShapeDtypeStruct((S, 4 * KV_W), BF16)] + host.out_shape,
        scratch_shapes=host.scratch, input_output_aliases=host.alias,
        name=name, compiler_params=_cp(("arbitrary",) if host.on else ("parallel",)))(p, cos_t, sin_t, *host.args)


def _qk_prep_bwd(dq, dk, dv, dmq, cos_t, sin_t, *, name="qk_prep_bwd", tr=256):
    S = dq.shape[0]
    tr = min(tr, S)

    def body(dq_ref, dk_ref, dv_ref, dmq_ref, c_ref, s_ref, o_ref):
        cs, sn = c_ref[...], s_ref[...]
        for c in range(Q_W // LANES):
            t = dq_ref[:, c * LANES:(c + 1) * LANES]
            o_ref[:, c * LANES:(c + 1) * LANES] = (t * cs - _rope_partner(t) * sn).astype(BF16)
        for c in range(KV_W // LANES):
            t = dk_ref[:, c * LANES:(c + 1) * LANES]
            o_ref[:, Q_W + c * LANES:Q_W + (c + 1) * LANES] = (t * cs - _rope_partner(t) * sn).astype(BF16)
        o_ref[:, Q_W + KV_W:Q_W + 2 * KV_W] = dv_ref[...].astype(BF16)
        o_ref[:, Q_W + 2 * KV_W:] = dmq_ref[...]

    row = lambda w: pl.BlockSpec((tr, w), lambda i: (i, 0))
    return pl.pallas_call(
        body, grid=(S // tr,), in_specs=[row(Q_W), row(KV_W), row(KV_W), row(MEM_W), row(LANES), row(LANES)],
        out_specs=row(ATTN_IN_W), out_shape=jax.ShapeDtypeStruct((S, ATTN_IN_W), BF16),
        name=name, compiler_params=_cp(("parallel",)))(dq, dk, dv, dmq, cos_t, sin_t)


def _band(n, S):
    start = pl.multiple_of(jnp.clip((n - 1) * BLOCK, 0, S - 3 * BLOCK), BLOCK)
    qi = lax.broadcasted_iota(jnp.int32, (BLOCK, 3 * BLOCK), 0) + n * BLOCK
    ki = lax.broadcasted_iota(jnp.int32, (BLOCK, 3 * BLOCK), 1) + start
    return start, jnp.abs(ki - qi) <= WINDOW


def _head_operand(ref, h, lo):
    c = h // 2
    t = ref[:, c * LANES:(c + 1) * LANES].astype(F32)
    return jnp.where(lo if h % 2 == 0 else jnp.logical_not(lo), t, 0.0).astype(BF16)


GROUP = ATTN_HEADS // ATTN_KV_HEADS
EVENS_FIRST = (0, 2, 1, 3)


def _attn_fwd(q, kd, va, sinks, *, name="attn_fwd", gather=None):
    S = q.shape[0]
    host = _Hosted(_gather_rider(gather), 4, 2)

    def body(*refs):
        ins, outs, scr, gref = host.split(refs, 4, 2)
        host.run(gref, pl.program_id(0), S // BLOCK, lambda: inner(*ins, *outs, *scr))

    def inner(sink_ref, q_ref, k_ref, va_ref, o_ref, lse_ref, p_scr):
        n = pl.program_id(0)
        start, mask = _band(n, S)
        lane = lax.broadcasted_iota(jnp.int32, (BLOCK, LANES), 1)
        lo = lane < HEAD_DIM
        rows = pl.ds(start, 3 * BLOCK)
        scores = []
        for g in range(ATTN_KV_HEADS):
            qst = jnp.concatenate([_head_operand(q_ref, GROUP * g + j, lo) for j in EVENS_FIRST], axis=0)
            scores.append(lax.dot_general(qst, k_ref[rows, g * LANES:(g + 1) * LANES], NT, preferred_element_type=F32))
        ms = {}
        for g in range(ATTN_KV_HEADS):
            for pos, j in enumerate(EVENS_FIRST):
                h = GROUP * g + j
                s = jnp.where(mask, scores[g][pos * BLOCK:(pos + 1) * BLOCK], NEG)
                ms[h] = jnp.maximum(jnp.max(s, axis=-1, keepdims=True), sink_ref[h])
                p_scr[(GROUP * g + pos) * BLOCK:(GROUP * g + pos + 1) * BLOCK, :] = jnp.exp(s - ms[h]).astype(BF16)
        pvs = {}
        for g in range(ATTN_KV_HEADS):
            for par in range(2):
                r0 = (GROUP * g + 2 * par) * BLOCK
                pvs[g, par] = jnp.dot(p_scr[r0:r0 + 2 * BLOCK, :], va_ref[rows, (2 * g + par) * LANES:(2 * g + par + 1) * LANES],
                                      preferred_element_type=F32)
        lse_blk = jnp.zeros((BLOCK, LANES), F32)
        for g in range(ATTN_KV_HEADS):
            outs = {}
            for par in range(2):
                for k in range(2):
                    j = EVENS_FIRST[2 * par + k]
                    h = GROUP * g + j
                    pv = pvs[g, par][k * BLOCK:(k + 1) * BLOCK]
                    den = pltpu.roll(pv, HEAD_DIM, 1) + jnp.exp(sink_ref[h] - ms[h])
                    outs[j] = pv * (1.0 / den)
                    l = den[:, par * HEAD_DIM:par * HEAD_DIM + 1]
                    lse_blk = jnp.where(lane == h, ms[h] + jnp.log(l), lse_blk)
            for jj in range(2):
                o_ref[:, (2 * g + jj) * LANES:(2 * g + jj + 1) * LANES] = jnp.where(lo, outs[2 * jj], outs[2 * jj + 1]).astype(BF16)
        lse_ref[...] = lse_blk

    full = lambda w: pl.BlockSpec((S, w), lambda i: (0, 0))
    return pl.pallas_call(
        body, grid=(S // BLOCK,),
        in_specs=[pl.BlockSpec(memory_space=pltpu.SMEM), pl.BlockSpec((BLOCK, Q_W), lambda i: (i, 0)),
                  full(2 * KV_W), full(4 * KV_W)] + host.in_specs,
        out_specs=[pl.BlockSpec((BLOCK, Q_W), lambda i: (i, 0)), pl.BlockSpec((BLOCK, LANES), lambda i: (i, 0))] + host.out_specs,
        out_shape=[jax.ShapeDtypeStruct((S, MIX_OUT_W), BF16), jax.ShapeDtypeStruct((S, LANES), F32)] + host.out_shape,
        scratch_shapes=[pltpu.VMEM((ATTN_HEADS * BLOCK, 3 * BLOCK), BF16)] + host.scratch, input_output_aliases=host.alias,
        name=name, compiler_params=_cp(("arbitrary",) if host.on else ("parallel",)))(sinks, q, kd, va, *host.args)


def _attn_bwd(q, kd, vd, ao, lse, sinks, dcat, *, name="attn_bwd", rider=None):
    S = q.shape[0]
    scale = HEAD_DIM ** -0.5
    host = _Hosted(rider, 7, 4)

    def body(*refs):
        ins, outs, scr, rrefs = host.split(refs, 7, 4)
        host.run(rrefs, pl.program_id(0), S // BLOCK, lambda: inner(*ins, *outs, *scr))

    def inner(sink_ref, q_ref, k_ref, v_ref, ao_ref, lse_ref, do_ref, dq_ref, dk_ref, dv_ref, ds_ref, p_scr, dsb_scr):
        n = pl.program_id(0)

        @pl.when(n == 0)
        def _():
            dk_ref[...] = jnp.zeros_like(dk_ref)
            dv_ref[...] = jnp.zeros_like(dv_ref)
            ds_ref[...] = jnp.zeros_like(ds_ref)

        start, mask = _band(n, S)
        lane = lax.broadcasted_iota(jnp.int32, (BLOCK, LANES), 1)
        lo = lane < HEAD_DIM
        lane3 = lax.broadcasted_iota(jnp.int32, (3 * BLOCK, LANES), 1)
        row8 = lax.broadcasted_iota(jnp.int32, (8, LANES), 0)
        lane8 = lax.broadcasted_iota(jnp.int32, (8, LANES), 1)
        dsink = jnp.zeros((8, LANES), F32)
        lse_blk = lse_ref[...]
        rows = pl.ds(start, 3 * BLOCK)
        lses, deltas = {}, {}
        for c in range(Q_W // LANES):
            prod = do_ref[:, c * LANES:(c + 1) * LANES].astype(F32) * ao_ref[:, c * LANES:(c + 1) * LANES].astype(F32)
            for k in range(2):
                h = 2 * c + k
                deltas[h] = jnp.sum(jnp.where(lo if k == 0 else jnp.logical_not(lo), prod, 0.0), axis=1, keepdims=True)
                lses[h] = jnp.sum(jnp.where(lane == h, lse_blk, 0.0), axis=1, keepdims=True)
                val = -jnp.sum(jnp.exp(sink_ref[h] - lses[h]) * deltas[h], axis=0, keepdims=True)
                dsink = dsink + jnp.where((row8 == 0) & (lane8 == h), val, 0.0)
        stack = lambda ref, g: jnp.concatenate([_head_operand(ref, GROUP * g + j, lo) for j in range(GROUP)], axis=0)
        ss, dps = [], []
        for g in range(ATTN_KV_HEADS):
            ss.append(lax.dot_general(stack(q_ref, g), k_ref[rows, g * LANES:(g + 1) * LANES], NT, preferred_element_type=F32))
            dps.append(lax.dot_general(stack(do_ref, g), v_ref[rows, g * LANES:(g + 1) * LANES], NT, preferred_element_type=F32))
        for g in range(ATTN_KV_HEADS):
            for j in range(GROUP):
                h = GROUP * g + j
                r = slice(j * BLOCK, (j + 1) * BLOCK)
                hr = slice(h * BLOCK, (h + 1) * BLOCK)
                p = jnp.exp(jnp.where(mask, ss[g][r], NEG) - lses[h])
                p_scr[hr, :] = p.astype(BF16)
                dsb_scr[hr, :] = (p * (dps[g][r] - deltas[h])).astype(BF16)
        for g in range(ATTN_KV_HEADS):
            cols = slice((g // 2) * LANES, (g // 2 + 1) * LANES)
            gr = slice(GROUP * g * BLOCK, GROUP * (g + 1) * BLOCK)
            dsg = dsb_scr[gr, :]
            dqs = jnp.dot(dsg, k_ref[rows, g * LANES:(g + 1) * LANES], preferred_element_type=F32) * scale
            for jj in range(2):
                dq_ref[:, (2 * g + jj) * LANES:(2 * g + jj + 1) * LANES] = jnp.where(
                    lo, dqs[(2 * jj) * BLOCK:(2 * jj + 1) * BLOCK], dqs[(2 * jj + 1) * BLOCK:(2 * jj + 2) * BLOCK])
            half = (lane3 < HEAD_DIM) if g % 2 == 0 else (lane3 >= HEAD_DIM)
            dkr = lax.dot_general(dsg, stack(q_ref, g), TN, preferred_element_type=F32)
            dk_ref[rows, cols] += jnp.where(half, dkr + pltpu.roll(dkr, HEAD_DIM, 1), 0.0)
            dvr = lax.dot_general(p_scr[gr, :], stack(do_ref, g), TN, preferred_element_type=F32)
            dv_ref[rows, cols] += jnp.where(half, dvr + pltpu.roll(dvr, HEAD_DIM, 1), 0.0)
        ds_ref[...] += dsink

    full = lambda w: pl.BlockSpec((S, w), lambda i: (0, 0))
    blk = lambda w: pl.BlockSpec((BLOCK, w), lambda i: (i, 0))
    return pl.pallas_call(
        body, grid=(S // BLOCK,),
        in_specs=[pl.BlockSpec(memory_space=pltpu.SMEM), blk(Q_W), full(2 * KV_W), full(2 * KV_W), blk(Q_W), blk(LANES), blk(Q_W)]
        + host.in_specs,
        out_specs=[blk(Q_W), full(KV_W), full(KV_W), pl.BlockSpec((8, LANES), lambda i: (0, 0))] + host.out_specs,
        out_shape=[jax.ShapeDtypeStruct((S, Q_W), F32), jax.ShapeDtypeStruct((S, KV_W), F32),
                   jax.ShapeDtypeStruct((S, KV_W), F32), jax.ShapeDtypeStruct((8, LANES), F32)] + host.out_shape,
        scratch_shapes=[pltpu.VMEM((ATTN_HEADS * BLOCK, 3 * BLOCK), BF16), pltpu.VMEM((ATTN_HEADS * BLOCK, 3 * BLOCK), BF16)]
        + host.scratch, input_output_aliases=host.alias,
        name=name, compiler_params=_cp(("arbitrary",)))(sinks, q, kd, vd, ao, lse, dcat, *host.args)


def _mem_probs(q_ref, kv_ref, h):
    scale = MEM_HEAD_DIM ** -0.5
    qh = q_ref[:, h * LANES:(h + 1) * LANES].astype(BF16)
    s = lax.dot_general(qh, kv_ref[:, h * LANES:(h + 1) * LANES], NT, preferred_element_type=F32) * scale
    m = jnp.max(s, axis=-1, keepdims=True)
    pe = jnp.exp(s - m)
    return qh, pe * (1.0 / jnp.sum(pe, axis=-1, keepdims=True))


def _memattn_fwd(p, qblk, kv, cat, *, name="memattn_fwd", tr=512):
    S = p.shape[0]
    tr = min(tr, S)

    def body(q_ref, kv_ref, cat_ref, o_ref):
        for h in range(MEM_HEADS):
            _, pr = _mem_probs(q_ref, kv_ref, h)
            o = jnp.dot(pr.astype(BF16), kv_ref[:, MEM_W + h * LANES:MEM_W + (h + 1) * LANES], preferred_element_type=F32)
            o_ref[:, h * LANES:(h + 1) * LANES] = o.astype(BF16)

    return pl.pallas_call(
        body, grid=(S // tr,),
        in_specs=[pl.BlockSpec((tr, MEM_W), lambda i: (i, qblk)), pl.BlockSpec((MEM_LEN, 2 * MEM_W), lambda i: (0, 0)), HBM],
        out_specs=pl.BlockSpec((tr, MEM_W), lambda i: (i, Q_W // MEM_W)),
        out_shape=jax.ShapeDtypeStruct((S, MIX_OUT_W), BF16), input_output_aliases={2: 0},
        name=name, compiler_params=_cp(("parallel",)))(p, kv, cat)


def _memattn_bwd(p, qblk, kv, dcat, *, name="memattn_bwd", tr=512, rider=None):
    S = p.shape[0]
    tr = min(tr, S)
    scale = MEM_HEAD_DIM ** -0.5
    host = _Hosted(rider, 3, 2)

    def body(*refs):
        ins, outs, _, rrefs = host.split(refs, 3, 2)
        host.run(rrefs, pl.program_id(0), S // tr, lambda: inner(*ins, *outs))

    def inner(q_ref, kv_ref, do_ref, dq_ref, dkv_ref):
        @pl.when(pl.program_id(0) == 0)
        def _():
            dkv_ref[...] = jnp.zeros_like(dkv_ref)

        for h in range(MEM_HEADS):
            qh, pr = _mem_probs(q_ref, kv_ref, h)
            doh = do_ref[:, h * LANES:(h + 1) * LANES]
            dp = lax.dot_general(doh, kv_ref[:, MEM_W + h * LANES:MEM_W + (h + 1) * LANES], NT, preferred_element_type=F32)
            delta = jnp.sum(pr * dp, axis=-1, keepdims=True)
            dsb = (pr * (dp - delta) * scale).astype(BF16)
            dq = jnp.dot(dsb, kv_ref[:, h * LANES:(h + 1) * LANES], preferred_element_type=F32)
            dq_ref[:, h * LANES:(h + 1) * LANES] = dq.astype(BF16)
            dkv_ref[:, h * LANES:(h + 1) * LANES] += lax.dot_general(dsb, qh, TN, preferred_element_type=F32)
            dkv_ref[:, MEM_W + h * LANES:MEM_W + (h + 1) * LANES] += lax.dot_general(
                pr.astype(BF16), doh, TN, preferred_element_type=F32)

    return pl.pallas_call(
        body, grid=(S // tr,),
        in_specs=[pl.BlockSpec((tr, MEM_W), lambda i: (i, qblk)), pl.BlockSpec((MEM_LEN, 2 * MEM_W), lambda i: (0, 0)),
                  pl.BlockSpec((tr, MEM_W), lambda i: (i, Q_W // MEM_W))] + host.in_specs,
        out_specs=[pl.BlockSpec((tr, MEM_W), lambda i: (i, 0)), pl.BlockSpec((MEM_LEN, 2 * MEM_W), lambda i: (0, 0))]
        + host.out_specs,
        out_shape=[jax.ShapeDtypeStruct((S, MEM_W), BF16), jax.ShapeDtypeStruct((MEM_LEN, 2 * MEM_W), F32)] + host.out_shape,
        scratch_shapes=host.scratch, input_output_aliases=host.alias,
        name=name, compiler_params=_cp(("arbitrary",)))(p, kv, dcat, *host.args)


def _sqrt(v):
    return jnp.where(v > 0.0, v * lax.rsqrt(v), 0.0)


def _sigmoid(z):
    return 1.0 / (1.0 + jnp.exp(-z))


def _one_minus_exp(z, exp_z):
    poly = z * (1.0 + z * (0.5 + z * (1.0 / 6.0 + z * (1.0 / 24.0 + z * (1.0 / 120.0)))))
    return jnp.where(z > -0.1, -poly, 1.0 - exp_z)


def _softplus_neg(lam):
    z = -lam
    return jnp.maximum(z, 0.0) + jnp.log(1.0 + jnp.exp(-jnp.abs(z)))


_GELU_C = math.sqrt(2.0 / math.pi)


def _gelu(z):
    return 0.5 * z * (1.0 + jnp.tanh(_GELU_C * (z + 0.044715 * z * z * z)))


def _row_or_zero(ref, t, S):
    ok = jnp.logical_and(t >= 0, t < S)
    return jnp.where(ok, ref[pl.ds(jnp.clip(t, 0, S - 1), 1), :], 0.0)


def _shift_down(v, first):
    ri = lax.broadcasted_iota(jnp.int32, v.shape, 0)
    return jnp.where(ri == 0, first, pltpu.roll(v, 1, 0))


def _shift_up(v, last):
    T = v.shape[0]
    ri = lax.broadcasted_iota(jnp.int32, v.shape, 0)
    return jnp.where(ri == T - 1, last, pltpu.roll(v, T - 1, 0))


def _scan_chunk(a, u, reverse):
    T = a.shape[0]
    ri = lax.broadcasted_iota(jnp.int32, a.shape, 0)
    d = 1
    while d < T:
        if reverse:
            a_s, u_s, ok = pltpu.roll(a, T - d, 0), pltpu.roll(u, T - d, 0), ri < T - d
        else:
            a_s, u_s, ok = pltpu.roll(a, d, 0), pltpu.roll(u, d, 0), ri >= d
        u = jnp.where(ok, a * u_s + u, u)
        a = jnp.where(ok, a * a_s, a)
        d *= 2
    return a, u


def _conv_taps(xb_ref, t0, S):
    T = SCAN_ROWS
    x0 = xb_ref[pl.ds(t0, T), :]
    xm1 = _shift_down(x0, _row_or_zero(xb_ref, t0 - 1, S))
    nxt0 = _row_or_zero(xb_ref, t0 + T, S)
    xp1 = _shift_up(x0, nxt0)
    xp2 = _shift_up(xp1, _row_or_zero(xb_ref, t0 + T + 1, S))
    return xm1, x0, xp1, xp2


def _lru_gates(xc, w_a, b_a, w_x, b_x, sp):
    xcb = xc.astype(BF16)
    r = _sigmoid(jnp.dot(xcb, w_a, preferred_element_type=F32) + b_a)
    i = _sigmoid(jnp.dot(xcb, w_x, preferred_element_type=F32) + b_x)
    la = -LRU_C * r * sp
    a = jnp.exp(la)
    b2 = _one_minus_exp(2.0 * la, a * a)
    inv_beta = lax.rsqrt(b2)
    return r, i, a, jnp.where(b2 > 0.0, b2 * inv_beta, 0.0), inv_beta


def _lru_specs(S):
    col = lambda off: pl.BlockSpec((S, LANES), lambda n: (0, n + off), pipeline_mode=pl.Buffered(1))
    small = lambda r: pl.BlockSpec((r, LANES), lambda n: (0, n))
    wblk = pl.BlockSpec((2, 1, LANES, LANES), lambda n: (0, n, 0, 0))
    return col, small, wblk


def _lru_fwd(p, conv_w, conv_b, wa, ba, wx, bx, lam, *, name="lru_fwd"):
    S = p.shape[0]
    T = SCAN_ROWS
    nc = S // T

    def body(xb_ref, gate_ref, cw_ref, cb_ref, wa_ref, ba_ref, wx_ref, bx_ref, lam_ref, y_ref, hf_ref, hr_ref, xc_v):
        sp = _softplus_neg(lam_ref[...])
        cw = cw_ref[...]

        def fwd_step(c, h_in):
            t0 = pl.multiple_of(c * T, T)
            xm1, x0, xp1, xp2 = _conv_taps(xb_ref, t0, S)
            xc = cb_ref[...] + xm1 * cw[0:1] + x0 * cw[1:2] + xp1 * cw[2:3] + xp2 * cw[3:4]
            xc_v[pl.ds(t0, T), :] = xc
            _, i, a, beta, _ = _lru_gates(xc, wa_ref[0, 0], ba_ref[0:1], wx_ref[0, 0], bx_ref[0:1], sp[0:1])
            A, U = _scan_chunk(a, beta * (i * xc), False)
            hf_ref[pl.ds(t0, T), :] = A * h_in + U
            return hf_ref[pl.ds(t0 + T - 1, 1), :]

        lax.fori_loop(0, nc, fwd_step, jnp.zeros((1, LANES), F32))

        def rev_step(k, h_in):
            t0 = pl.multiple_of((nc - 1 - k) * T, T)
            xc = xc_v[pl.ds(t0, T), :]
            _, i, a, beta, _ = _lru_gates(xc, wa_ref[1, 0], ba_ref[1:2], wx_ref[1, 0], bx_ref[1:2], sp[1:2])
            A, U = _scan_chunk(a, beta * (i * xc), True)
            h = A * h_in + U
            hr_ref[pl.ds(t0, T), :] = h
            y_ref[pl.ds(t0, T), :] = ((hf_ref[pl.ds(t0, T), :] + h) * _gelu(gate_ref[pl.ds(t0, T), :])).astype(BF16)
            return hr_ref[pl.ds(t0, 1), :]

        lax.fori_loop(0, nc, rev_step, jnp.zeros((1, LANES), F32))

    col, small, wblk = _lru_specs(S)
    colo = lambda: pl.BlockSpec((S, LANES), lambda n: (0, n))
    return pl.pallas_call(
        body, grid=(LRU_BLOCKS,),
        in_specs=[col(0), col(LRU_BLOCKS), small(4), small(1), wblk, small(2), wblk, small(2), small(2)],
        out_specs=[colo(), colo(), colo()],
        out_shape=[jax.ShapeDtypeStruct((S, MIX_OUT_W), BF16), jax.ShapeDtypeStruct((S, D_MODEL), F32),
                   jax.ShapeDtypeStruct((S, D_MODEL), F32)],
        scratch_shapes=[pltpu.VMEM((S, LANES), F32)],
        name=name, compiler_params=_cp(("parallel",)))(p, p, conv_w, conv_b, wa, ba, wx, bx, lam)


def _lru_bwd(p, hf, hr, dcat, conv_w, conv_b, wa, ba, wx, bx, lam, *, name="lru_bwd"):
    S = p.shape[0]
    T = SCAN_ROWS
    nc = S // T

    def body(xb_ref, gate_ref, hf_ref, hr_ref, dy_ref, cw_ref, cb_ref, wa_ref, ba_ref, wx_ref, bx_ref, lam_ref,
             dxb_ref, dgate_ref, dcw_ref, dcb_ref, dwa_ref, dba_ref, dwx_ref, dbx_ref, dlam_ref, xc_v, dxc_v, dh_v):
        lam_v = lam_ref[...]
        sp = _softplus_neg(lam_v)
        cw = cw_ref[...]
        for ref in (dcw_ref, dcb_ref, dwa_ref, dba_ref, dwx_ref, dbx_ref, dlam_ref):
            ref[...] = jnp.zeros_like(ref)

        def prep_step(c, carry):
            t0 = pl.multiple_of(c * T, T)
            rows = pl.ds(t0, T)
            xm1, x0, xp1, xp2 = _conv_taps(xb_ref, t0, S)
            xc_v[rows, :] = cb_ref[...] + xm1 * cw[0:1] + x0 * cw[1:2] + xp1 * cw[2:3] + xp2 * cw[3:4]
            z = gate_ref[rows, :]
            dy = dy_ref[rows, :].astype(F32)
            th = jnp.tanh(_GELU_C * (z + 0.044715 * z * z * z))
            dgelu = 0.5 * (1.0 + th) + 0.5 * z * (1.0 - th * th) * _GELU_C * (1.0 + 3.0 * 0.044715 * z * z)
            dgate_ref[rows, :] = (dy * (hf_ref[rows, :] + hr_ref[rows, :]) * dgelu).astype(BF16)
            dh_v[rows, :] = dy * (0.5 * z * (1.0 + th))
            return carry

        lax.fori_loop(0, nc, prep_step, 0)

        def direction(d):
            h_ref = hf_ref if d == 0 else hr_ref
            w_a, w_x = wa_ref[d, 0], wx_ref[d, 0]
            b_a, b_x, sp_d = ba_ref[d:d + 1], bx_ref[d:d + 1], sp[d:d + 1]

            def step(k, carry):
                g_in, a_in = carry
                c = (nc - 1 - k) if d == 0 else k
                t0 = pl.multiple_of(c * T, T)
                rows = pl.ds(t0, T)
                xc = xc_v[rows, :]
                r, i, a, beta, inv_beta = _lru_gates(xc, w_a, b_a, w_x, b_x, sp_d)
                dh = dh_v[rows, :]
                hc = h_ref[rows, :]
                if d == 0:
                    A, U = _scan_chunk(_shift_up(a, a_in), dh, True)
                    g = A * g_in + U
                    h_nb = _shift_down(hc, _row_or_zero(h_ref, t0 - 1, S))
                    nxt = (g[0:1], a[0:1])
                else:
                    A, U = _scan_chunk(_shift_down(a, a_in), dh, False)
                    g = A * g_in + U
                    h_nb = _shift_up(hc, _row_or_zero(h_ref, t0 + T, S))
                    nxt = (g[T - 1:T], a[T - 1:T])
                da = g * h_nb
                dbeta = g * (i * xc)
                tb = g * beta
                dla = da * a - dbeta * (a * a * inv_beta)
                dzr = (dla * (-LRU_C * sp_d)) * (r * (1.0 - r))
                dzi = (tb * xc) * (i * (1.0 - i))
                dzrb, dzib, xcb = dzr.astype(BF16), dzi.astype(BF16), xc.astype(BF16)
                dwa_ref[d, 0] += lax.dot_general(xcb, dzrb, TN, preferred_element_type=F32)
                dwx_ref[d, 0] += lax.dot_general(xcb, dzib, TN, preferred_element_type=F32)
                dba_ref[d:d + 1] += jnp.sum(dzr, axis=0, keepdims=True)
                dbx_ref[d:d + 1] += jnp.sum(dzi, axis=0, keepdims=True)
                dlam_ref[d:d + 1] += jnp.sum(dla * (-LRU_C * r), axis=0, keepdims=True)
                dxc = (tb * i + lax.dot_general(dzrb, w_a, NT, preferred_element_type=F32)
                       + lax.dot_general(dzib, w_x, NT, preferred_element_type=F32))
                if d == 0:
                    dxc_v[rows, :] = dxc
                else:
                    dxc_v[rows, :] += dxc
                return nxt

            lax.fori_loop(0, nc, step, (jnp.zeros((1, LANES), F32), jnp.zeros((1, LANES), F32)))

        direction(0)
        direction(1)
        dlam_ref[...] = dlam_ref[...] * (-1.0 / (1.0 + jnp.exp(lam_v)))

        def conv_step(c, carry):
            t0 = pl.multiple_of(c * T, T)
            rows = pl.ds(t0, T)
            g0 = dxc_v[rows, :]
            gm1 = _shift_down(g0, _row_or_zero(dxc_v, t0 - 1, S))
            gm2 = _shift_down(gm1, _row_or_zero(dxc_v, t0 - 2, S))
            gp1 = _shift_up(g0, _row_or_zero(dxc_v, t0 + T, S))
            dxb_ref[rows, :] = (cw[0:1] * gp1 + cw[1:2] * g0 + cw[2:3] * gm1 + cw[3:4] * gm2).astype(BF16)
            xm1, x0, xp1, xp2 = _conv_taps(xb_ref, t0, S)
            for tap, xs in enumerate((xm1, x0, xp1, xp2)):
                dcw_ref[tap:tap + 1] += jnp.sum(g0 * xs, axis=0, keepdims=True)
            dcb_ref[...] += jnp.sum(g0, axis=0, keepdims=True)
            return carry

        lax.fori_loop(0, nc, conv_step, 0)

    col, small, wblk = _lru_specs(S)
    colo = lambda: pl.BlockSpec((S, LANES), lambda n: (0, n), pipeline_mode=pl.Buffered(1))
    return pl.pallas_call(
        body, grid=(LRU_BLOCKS,),
        in_specs=[col(0), col(LRU_BLOCKS), col(0), col(0), col(0), small(4), small(1), wblk, small(2), wblk, small(2), small(2)],
        out_specs=[colo(), colo(), small(4), small(1), wblk, small(2), wblk, small(2), small(2)],
        out_shape=[jax.ShapeDtypeStruct((S, D_MODEL), BF16), jax.ShapeDtypeStruct((S, D_MODEL), BF16),
                   jax.ShapeDtypeStruct((4, D_MODEL), F32), jax.ShapeDtypeStruct((1, D_MODEL), F32),
                   jax.ShapeDtypeStruct((2, LRU_BLOCKS, LANES, LANES), F32), jax.ShapeDtypeStruct((2, D_MODEL), F32),
                   jax.ShapeDtypeStruct((2, LRU_BLOCKS, LANES, LANES), F32), jax.ShapeDtypeStruct((2, D_MODEL), F32),
                   jax.ShapeDtypeStruct((2, D_MODEL), F32)],
        scratch_shapes=[pltpu.VMEM((S, LANES), F32), pltpu.VMEM((S, LANES), F32), pltpu.VMEM((S, LANES), F32)],
        name=name, compiler_params=_cp(("parallel",)))(p, p, hf, hr, dcat, conv_w, conv_b, wa, ba, wx, bx, lam)


PK_UP, PK_DOWN, PK_KV, PK_OUT, PK_IN = 0, 1024, 2048, 2304, 2688
PK_ROWS = {0: PK_IN, 1: PK_IN + 640}
SMALL_G_ROWS = 192
PKF_KV, PKF_SMALL = 512, 768
PKF_ROWS = PKF_SMALL + SMALL_G_ROWS


def _mlp_bwd(x, dx, dxb, saved, w_up, w_down, gain, l, rider=None, next_rider=None):
    up, act, h = saved
    pk = _mm_tn(act, dxb, 1, name=f"dw_down{l}", packed=(None, PK_ROWS[l], PK_DOWN), rider=rider)
    pk, carried = pk if rider is not None else (pk, None)
    dup = _mm_nt(dxb, w_down, up=up, name=f"d_up{l}")
    rider_up = next_rider(carried) if next_rider is not None else None
    pk = _mm_tn(h, dup, N_CHIPS, name=f"dw_up{l}", packed=(pk, PK_ROWS[l], PK_UP), rider=rider_up)
    pk, carried = pk if rider_up is not None else (pk, carried)
    dx, dxb, g_gain = _mm_nt(dup, w_up, norm_x=x, norm_g=gain, dres=dx, name=f"d_mlp_in{l}")
    return dx, dxb, pk, g_gain, carried


def _reduce_first(pk, place, tag, recv=None):
    if recv is None:
        recv = _sibling_exchange(pk, name=f"grad_sibling_exchange{tag}")
    return _sum_halves(pk, recv, place, name=f"sum_halves{tag}", tr=pk.shape[1] // 4)


def _sum_parts(parts, place, tag):
    return _sum_chips(parts, place, name=f"sum_chips{tag}", tr=parts.shape[1] // 2)


def _reduce_last(parts, place, tag):
    return _sibling_allgather(_sum_parts(parts, place, tag), name=f"grad_sibling_allgather{tag}")


def _local_step(x, mem, positions, target, W, pending=None, place=None):
    cos_t, sin_t = _rope_tables(positions)
    sinks = W["attn_sinks"].reshape(ATTN_HEADS)
    G = {}

    def hosting(late, fn, *args, **kw):
        if pending is None:
            return fn(*args, **kw)
        *res, buf = fn(*args, gather=pending[late], **kw)
        if late.startswith("w_down"):
            W.setdefault("w_down", [None] * DEPTH)[int(late[-1])] = _ready(late, buf)
        elif late == "w_out":
            W["w_out"], W["w_mem_kv"] = _ready(late, buf)
        else:
            W[late] = _ready(late, buf)
        return res if len(res) > 1 else res[0]

    p0, h0 = hosting("w_out", _mm_nn, x, W["attn_w_in"], norm_g=W["mix_norm"][0], name="attn_in")
    q, kd, vd, va = hosting("w_down0", _qk_prep, p0, cos_t, sin_t)
    ao, lse = hosting("w_up", _attn_fwd, q, kd, va, sinks)
    kv0, memn = _mm_nn(mem, W["w_mem_kv"][0], norm_g=W["mem_norm"], out_dtype=BF16, name="mem_kv0", tm=256)
    kv1 = _mm_nn(memn, W["w_mem_kv"][1], out_dtype=BF16, name="mem_kv1", tm=256)
    cat0 = _memattn_fwd(p0, Q_W // MEM_W + 1, kv0, ao, name="memattn_fwd0")
    x1 = hosting("lru_w_in", _mm_nn, cat0, W["w_out"][0], resid=x, name="mix_out0")
    up0, act0, h1 = hosting("w_down1", _mm_nn, x1, W["w_up"][0], norm_g=W["mlp_norm"][0], relu2=True, name="mlp_up0")
    x2, mlp0 = _mm_nn(act0, W["w_down"][0], resid=x1, name="mlp_down0"), (up0, act0, h1)
    p1, h2 = _mm_nn(x2, W["lru_w_in"], norm_g=W["mix_norm"][1], name="lru_in")
    lru_w = (W["lru_conv_w"], W["lru_conv_b"], W["lru_wa"], W["lru_ba"], W["lru_wx"], W["lru_bx"], W["lru_lambda"])
    y, hf, hr = _lru_fwd(p1, *lru_w)
    cat1 = _memattn_fwd(p1, 2 * D_MODEL // MEM_W, kv1, y, name="memattn_fwd1")
    x3 = _mm_nn(cat1, W["w_out"][1], resid=x2, name="mix_out1")
    mlp1 = _mm_nn(x3, W["w_up"][1], norm_g=W["mlp_norm"][1], relu2=True, name="mlp_up1")
    loss, dx, dxb, G["final_norm"] = _final(mlp1[1], W["w_down"][1], x3, W["final_norm"], target)

    def put(pk, off, g):
        return pk.at[:, off:off + g.size // (N_CHIPS * ROW)].set(g.reshape(N_CHIPS, -1, ROW))

    dx, dxb, pk1, gm1, _ = _mlp_bwd(x3, dx, dxb, mlp1, W["w_up"][1], W["w_down"][1], W["mlp_norm"][1], 1)
    pk1 = _mm_tn(cat1, dxb, 1, name="dw_out1", tk=384, packed=(pk1, PK_ROWS[1], PK_OUT))
    dcat1 = _mm_nt(dxb, W["w_out"][1], name="d_mix1")
    dmq1, dkv1 = _memattn_bwd(p1, 2 * D_MODEL // MEM_W, kv1, dcat1, name="memattn_bwd1")
    dkv1b = dkv1.astype(BF16)
    pk1 = _mm_tn(memn, dkv1b, 1, name="dw_kv1", tm=256, tk=256, packed=(pk1, PK_ROWS[1], PK_KV))
    (dxb1, dgate, G["lru_conv_w"], G["lru_conv_b"], G["lru_wa"], G["lru_ba"], G["lru_wx"], G["lru_bx"],
     G["lru_lambda"]) = _lru_bwd(p1, hf, hr, dcat1, *lru_w)
    dp1 = jnp.concatenate([dxb1, dgate, dmq1], axis=1)
    pk1 = put(pk1, PK_IN, _mm_tn(h2, dp1, N_CHIPS, name="dw_lru_in"))
    dx, dxb, gx1 = _mm_nt(dp1, W["lru_w_in"], norm_x=x2, norm_g=W["mix_norm"][1], dres=dx, name="d_lru_in")
    dist = place is not None
    h1_rows = PK_ROWS[1] // 4
    kept = {}

    def first_half(recv1):
        kept["halves1"], landing1 = _reduce_first(pk1, place, "1", recv1)
        return _exchange_rider((kept["halves1"], landing1, 0, h1_rows))

    dx, dxb, pk0, gm0, landing1 = _mlp_bwd(x1, dx, dxb, mlp0, W["w_up"][0], W["w_down"][0], W["mlp_norm"][0], 0,
                                           rider=_sib_exchange_rider(pk1) if dist else None,
                                           next_rider=first_half if dist else None)
    pk0 = _mm_tn(cat0, dxb, 1, name="dw_out0", tk=384, packed=(pk0, PK_ROWS[0], PK_OUT))
    pk0 = pk0.at[:, PK_KV:PK_OUT].set(0.0)
    dcat0 = _mm_nt(dxb, W["w_out"][0], name="d_mix0")
    dmq0, dkv0, *recv0 = _memattn_bwd(p0, Q_W // MEM_W + 1, kv0, dcat0, name="memattn_bwd0",
                                      rider=_sib_exchange_rider(pk0) if dist else None)
    dkv0b = dkv0.astype(BF16)
    g_kv0 = _mm_tn(memn, dkv0b, 1, name="dw_kv0", tm=256, tk=256)
    rider = None
    if dist:
        halves0, landing0 = _reduce_first(pk0, place, "0", recv0[0])
        rider = _exchange_rider((kept["halves1"], landing1, h1_rows, h1_rows), (halves0, landing0, 0, halves0.shape[1]))
    dq, dk, dv, dsink, *parts = _attn_bwd(q, kd, vd, cat0, lse, sinks, dcat0, rider=rider)
    dp0 = _qk_prep_bwd(dq, dk, dv, dmq0, cos_t, sin_t)
    g_in = _mm_tn(h0, dp0, N_CHIPS, name="dw_attn_in",
                  rider=_sib_allgather_rider(_sum_parts(parts[0], place, "1"), _sum_parts(parts[1], place, "0")) if dist else None)
    if dist:
        g_in, pk1, pk0 = g_in
    dx, _, gx0 = _mm_nt(dp0, W["attn_w_in"], norm_x=x, norm_g=W["mix_norm"][0], dres=dx, name="d_attn_in")

    w_kv_both = jnp.concatenate([W["w_mem_kv"][0], W["w_mem_kv"][1]], axis=0)
    _, _, G["mem_norm"] = _mm_nt(jnp.concatenate([dkv0b, dkv1b], axis=1), w_kv_both, norm_x=mem, norm_g=W["mem_norm"],
                                 name="d_mem", tm=256)

    G["mix_norm"] = jnp.concatenate([gx0, gx1], axis=0)
    G["mlp_norm"] = jnp.concatenate([gm0, gm1], axis=0)
    G["attn_sinks"] = dsink[0:1, 0:ATTN_HEADS]
    small = _flat_pad(_small_grad_list(G), N_CHIPS * SMALL_G_ROWS * ROW).reshape(N_CHIPS, SMALL_G_ROWS, ROW)
    pkf = jnp.concatenate([g_in.reshape(N_CHIPS, PKF_KV, ROW), g_kv0.reshape(N_CHIPS, PKF_SMALL - PKF_KV, ROW), small], axis=1)
    return loss[0, 0], dx, G, pkf, pk0, pk1


def _comm_call(body, out_shape, n_sems, name, *args, alias=None):
    return pl.pallas_call(
        body, out_shape=out_shape, in_specs=[HBM] * len(args), out_specs=HBM,
        scratch_shapes=[pltpu.SemaphoreType.DMA((n_sems,)), pltpu.SemaphoreType.DMA((n_sems,))],
        input_output_aliases=alias or {}, name=name)(*args)


def _place_slot(shard, slot, n_slots, *, name, tr):
    R, C = shard.shape

    def body(s_ref, a_ref, o_ref):
        o_ref[0] = a_ref[...]

    return pl.pallas_call(
        body,
        grid_spec=pltpu.PrefetchScalarGridSpec(
            num_scalar_prefetch=1, grid=(R // tr,), in_specs=[pl.BlockSpec((tr, C), lambda i, s_ref: (i, 0))],
            out_specs=pl.BlockSpec((1, tr, C), lambda i, s_ref: (s_ref[0], i, 0))),
        out_shape=jax.ShapeDtypeStruct((n_slots, R, C), shard.dtype), name=name,
        compiler_params=_cp(("parallel",)))(slot, shard)


def _allgather_chips(buf, *, name):
    def body(b_ref, o_ref, send_sems, recv_sems):
        _gather_start(o_ref, send_sems, recv_sems)
        _gather_finish(o_ref, send_sems, recv_sems)

    return _comm_call(body, jax.ShapeDtypeStruct(buf.shape, buf.dtype), GATHER_SEMS, name, buf, alias={0: 0})


def _sibling_exchange(g, *, name):
    _, R, C = g.shape
    half = R // 2

    def body(g_ref, o_ref, send_sems, recv_sems):
        _sib_exchange_start(g_ref, o_ref, send_sems, recv_sems)
        _sib_exchange_finish(g_ref, o_ref, send_sems, recv_sems)

    return _comm_call(body, jax.ShapeDtypeStruct((N_CHIPS, half, C), g.dtype), N_CHIPS, name, g)


def _chip_exchange(h, parts, *, name):
    def body(h_ref, p_ref, o_ref, send_sems, recv_sems):
        x, y, c, chips = _place()
        me = 2 * x + y
        cps = [_remote(h_ref.at[2 * cx + cy], o_ref.at[me], send_sems, recv_sems, j, (cx, cy, c))
               for j, (cx, cy) in enumerate(chips)]
        for cp in cps:
            cp.start()
        for j, (cx, cy) in enumerate(chips):
            got = o_ref.at[2 * cx + cy]
            _remote(got, got, send_sems, recv_sems, j, (cx, cy, c)).wait_recv()
        for cp in cps:
            cp.wait_send()

    return _comm_call(body, jax.ShapeDtypeStruct(parts.shape, parts.dtype), 3, name, h, parts, alias={1: 0})


def _sibling_allgather(full, *, name):
    def body(f_ref, o_ref, send_sems, recv_sems):
        _sib_allgather_start(o_ref, send_sems, recv_sems)
        _sib_allgather_finish(o_ref, send_sems, recv_sems)

    return _comm_call(body, jax.ShapeDtypeStruct(full.shape, full.dtype), 1, name, full, alias={0: 0})


def _sum_halves(g, recv, place, *, name="sum_halves", tr=480):
    _, R, C = g.shape
    half = R // 2
    nblk = half // tr

    def body(pl_ref, g_ref, r_ref, o_ref, own_ref):
        v = (g_ref[...] + r_ref[...]).astype(BF16)
        o_ref[...] = v

        @pl.when(pl.program_id(1) == pl_ref[1])
        def _():
            own_ref[...] = v

    blk = pl.BlockSpec((1, tr, C), lambda i, s, p: (s, i, 0))
    return pl.pallas_call(
        body,
        grid_spec=pltpu.PrefetchScalarGridSpec(
            num_scalar_prefetch=1, grid=(nblk, N_CHIPS),
            in_specs=[pl.BlockSpec((1, tr, C), lambda i, s, p: (s, p[0] * nblk + i, 0)), blk],
            out_specs=[blk, pl.BlockSpec((1, tr, C), lambda i, s, p: (p[1], i, 0))]),
        out_shape=[jax.ShapeDtypeStruct((N_CHIPS, half, C), BF16)] * 2, name=name,
        compiler_params=_cp(("parallel", "arbitrary")))(place, g, recv)


def _sum_chips(parts, place, *, name="sum_chips", tr=480):
    _, R, C = parts.shape
    nblk = R // tr

    def body(pl_ref, p_ref, o_ref):
        acc = p_ref[0].astype(F32) + p_ref[1].astype(F32)
        o_ref[...] = (acc + p_ref[2].astype(F32)) + p_ref[3].astype(F32)

    return pl.pallas_call(
        body,
        grid_spec=pltpu.PrefetchScalarGridSpec(
            num_scalar_prefetch=1, grid=(nblk,), in_specs=[pl.BlockSpec((N_CHIPS, tr, C), lambda i, p: (0, i, 0))],
            out_specs=pl.BlockSpec((tr, C), lambda i, p: (p[0] * nblk + i, 0))),
        out_shape=jax.ShapeDtypeStruct((2 * R, C), F32), name=name, compiler_params=_cp(("parallel",)))(place, parts)


def _adamw(w, g, m, v, *, name, tr=128):
    R, C = w.shape
    bc1 = 1.0 - ADAM_B1 ** ADAM_STEP
    bc2 = 1.0 - ADAM_B2 ** ADAM_STEP

    def body(w_ref, g_ref, m_ref, v_ref, d_ref, nm_ref, nv_ref):
        gv = g_ref[...]
        nm = ADAM_B1 * m_ref[...] + (1.0 - ADAM_B1) * gv
        nv = ADAM_B2 * v_ref[...] + (1.0 - ADAM_B2) * (gv * gv)
        d_ref[...] = -ADAM_LR * ((nm / bc1) / (_sqrt(nv / bc2) + ADAM_EPS) + ADAM_WD * w_ref[...])
        nm_ref[...] = nm
        nv_ref[...] = nv

    blk = pl.BlockSpec((tr, C), lambda i: (i, 0))
    return pl.pallas_call(
        body, grid=(R // tr,), in_specs=[blk] * 4, out_specs=[blk] * 3,
        out_shape=[jax.ShapeDtypeStruct((R, C), F32)] * 3, name=name, compiler_params=_cp(("parallel",)))(w, g, m, v)


ROW = 1024
BIG = ("w_mem_kv", "w_out", "w_up", "w_down", "attn_w_in", "lru_w_in")
SMALL_SHARDED = ("lru_conv_w", "lru_conv_b", "lru_ba", "lru_bx", "lru_lambda")
REPLICATED = ("mix_norm", "mlp_norm", "mem_norm", "final_norm", "attn_sinks", "lru_wa", "lru_wx")
SMALL = REPLICATED + SMALL_SHARDED
WEIGHTS = ("mix_norm", "mlp_norm", "mem_norm", "final_norm", "w_mem_kv", "w_out", "w_up", "w_down", "attn_w_in",
           "attn_sinks", "lru_w_in", "lru_conv_w", "lru_conv_b", "lru_wa", "lru_ba", "lru_wx", "lru_bx", "lru_lambda")
SMALL_W_ROWS = 32
ADAM_SMALL_ROWS = 640


def _rows(a):
    return a.reshape(-1, ROW)


def _flat_pad(parts, total):
    flat = jnp.concatenate([p.reshape(-1) for p in parts])
    return jnp.pad(flat, (0, total - flat.shape[0]))


def _pad_rows(a):
    flat = a.reshape(-1)
    n = -(-flat.shape[0] // ROW) * ROW
    return jnp.pad(flat, (0, n - flat.shape[0])).reshape(-1, ROW)


LATE = ("w_out", "w_down0", "w_up", "lru_w_in", "w_down1")


def _ready(name, full):
    if name == "w_out":
        n_out = DEPTH * MIX_OUT_W // N_CHIPS
        wo = full[:, :n_out].reshape(N_CHIPS, DEPTH, -1, D_MODEL)
        kv = full[:, n_out:].reshape(N_CHIPS, DEPTH, -1, D_MODEL)
        return ([wo[:, l].reshape(1, MIX_OUT_W, D_MODEL) for l in range(DEPTH)],
                [kv[:, l].reshape(1, D_MODEL, D_MODEL) for l in range(DEPTH)])
    if name == "w_up":
        wu = full.reshape(N_CHIPS, DEPTH, D_MODEL, D_FF // N_CHIPS)
        return [wu[:, l] for l in range(DEPTH)]
    if name == "lru_w_in":
        return full.reshape(N_CHIPS, D_MODEL, LRU_IN_W // N_CHIPS)
    return full.reshape(1, D_FF, D_MODEL)


def _gather_weights(P, chip1):
    bf = lambda a: _rows(a.astype(BF16))
    small = _flat_pad([P[n] for n in SMALL_SHARDED], SMALL_W_ROWS * ROW // 2)
    small_bits = lax.bitcast_convert_type(small, BF16).reshape(SMALL_W_ROWS, ROW)
    early = jnp.concatenate([bf(P["attn_w_in"]), small_bits], axis=0)
    n_in = P["attn_w_in"].size // ROW
    placed = _place_slot(early, chip1, N_CHIPS, name="place_weights", tr=early.shape[0] // 2)
    full = _allgather_chips(placed, name="allgather_weights")
    late = {"w_out": jnp.concatenate([bf(P["w_out"]), bf(P["w_mem_kv"])], axis=0), "lru_w_in": bf(P["lru_w_in"]),
            "w_up": bf(P["w_up"]), "w_down0": bf(P["w_down"][0]), "w_down1": bf(P["w_down"][1])}
    pending = {n: _place_slot(late[n], chip1, N_CHIPS, name=f"place_{n}", tr=late[n].shape[0] // 2) for n in LATE}
    W = {n: P[n] for n in REPLICATED}
    W["attn_w_in"] = full[:, :n_in].reshape(N_CHIPS, D_MODEL, ATTN_IN_W // N_CHIPS)
    sm = lax.bitcast_convert_type(full[:, n_in:].reshape(N_CHIPS, -1, 2), F32)
    o = 0
    for n in SMALL_SHARDED:
        shp = P[n].shape[1:]
        cnt = math.prod(shp)
        piece = sm[:, o:o + cnt].reshape((N_CHIPS,) + shp)
        piece = jnp.moveaxis(piece, 0, -2)
        W[n] = piece.reshape(shp[:-1] + (N_CHIPS * shp[-1],)).reshape(-1, D_MODEL)
        o += cnt
    W["lru_wa"] = P["lru_wa"][0].astype(BF16)
    W["lru_wx"] = P["lru_wx"][0].astype(BF16)
    return W, pending


def _small_grad_list(G):
    return [G["mix_norm"], G["mlp_norm"], G["mem_norm"], G["final_norm"], jnp.pad(G["attn_sinks"].reshape(-1), (0, ROW - ATTN_HEADS)),
            G["lru_wa"], G["lru_wx"], G["lru_conv_w"], G["lru_conv_b"], G["lru_ba"], G["lru_bx"], G["lru_lambda"]]


SMALL_G_SIZES = (2 * D_MODEL, 2 * D_MODEL, D_MODEL, D_MODEL, ROW, 2 * 8 * 128 * 128, 2 * 8 * 128 * 128,
                 4 * D_MODEL, D_MODEL, 2 * D_MODEL, 2 * D_MODEL, 2 * D_MODEL)


def _finish_grads(pkf, full0, full1, place, chip1):
    partsf = _chip_exchange(*_reduce_first(pkf, place, "f"), name="grad_chip_exchange_last")
    fullf = _reduce_last(partsf, place, "f")
    small_placed = _place_slot(fullf[PKF_SMALL:], chip1, N_CHIPS, name="place_small_grads", tr=SMALL_G_ROWS)
    small_all = _allgather_chips(small_placed, name="allgather_small_grads")
    flat = small_all.reshape(-1)
    small = {}
    o = 0
    names = ("mix_norm", "mlp_norm", "mem_norm", "final_norm", "attn_sinks", "lru_wa", "lru_wx",
             "lru_conv_w", "lru_conv_b", "lru_ba", "lru_bx", "lru_lambda")
    for n, cnt in zip(names, SMALL_G_SIZES):
        small[n] = flat[o:o + cnt]
        o += cnt
    both = lambda off, r: jnp.concatenate([full0[off:off + r], full1[off:off + r]], axis=0)
    big = {"w_up": both(PK_UP, 1024), "w_down": both(PK_DOWN, 1024), "w_out": both(PK_OUT, 384),
           "w_mem_kv": jnp.concatenate([fullf[PKF_KV:PKF_SMALL], full1[PK_KV:PK_OUT]], axis=0),
           "attn_w_in": fullf[:PKF_KV], "lru_w_in": full1[PK_IN:PK_IN + 640]}
    return big, small


def kernel(x, mem, positions, mix_norm, mlp_norm, mem_norm, final_norm, w_mem_kv, w_out, w_up, w_down, attn_w_in, attn_sinks, lru_w_in, lru_conv_w, lru_conv_b, lru_wa, lru_ba, lru_wx, lru_bx, lru_lambda, loss_target, m_mix_norm, m_mlp_norm, m_mem_norm, m_final_norm, m_w_mem_kv, m_w_out, m_w_up, m_w_down, m_attn_w_in, m_attn_sinks, m_lru_w_in, m_lru_conv_w, m_lru_conv_b, m_lru_wa, m_lru_ba, m_lru_wx, m_lru_bx, m_lru_lambda, v_mix_norm, v_mlp_norm, v_mem_norm, v_final_norm, v_w_mem_kv, v_w_out, v_w_up, v_w_down, v_attn_w_in, v_attn_sinks, v_lru_w_in, v_lru_conv_w, v_lru_conv_b, v_lru_wa, v_lru_ba, v_lru_wx, v_lru_bx, v_lru_lambda):
    P = dict(mix_norm=mix_norm, mlp_norm=mlp_norm, mem_norm=mem_norm, final_norm=final_norm, w_mem_kv=w_mem_kv, w_out=w_out,
             w_up=w_up, w_down=w_down, attn_w_in=attn_w_in, attn_sinks=attn_sinks, lru_w_in=lru_w_in, lru_conv_w=lru_conv_w,
             lru_conv_b=lru_conv_b, lru_wa=lru_wa, lru_ba=lru_ba, lru_wx=lru_wx, lru_bx=lru_bx, lru_lambda=lru_lambda)
    M1 = dict(mix_norm=m_mix_norm, mlp_norm=m_mlp_norm, mem_norm=m_mem_norm, final_norm=m_final_norm, w_mem_kv=m_w_mem_kv,
              w_out=m_w_out, w_up=m_w_up, w_down=m_w_down, attn_w_in=m_attn_w_in, attn_sinks=m_attn_sinks, lru_w_in=m_lru_w_in,
              lru_conv_w=m_lru_conv_w, lru_conv_b=m_lru_conv_b, lru_wa=m_lru_wa, lru_ba=m_lru_ba, lru_wx=m_lru_wx,
              lru_bx=m_lru_bx, lru_lambda=m_lru_lambda)
    V2 = dict(mix_norm=v_mix_norm, mlp_norm=v_mlp_norm, mem_norm=v_mem_norm, final_norm=v_final_norm, w_mem_kv=v_w_mem_kv,
              w_out=v_w_out, w_up=v_w_up, w_down=v_w_down, attn_w_in=v_attn_w_in, attn_sinks=v_attn_sinks, lru_w_in=v_lru_w_in,
              lru_conv_w=v_lru_conv_w, lru_conv_b=v_lru_conv_b, lru_wa=v_lru_wa, lru_ba=v_lru_ba, lru_wx=v_lru_wx,
              lru_bx=v_lru_bx, lru_lambda=v_lru_lambda)
    chip = 2 * lax.axis_index("x") + lax.axis_index("y")
    chip1 = chip.astype(jnp.int32).reshape(1)
    place = jnp.stack([lax.axis_index("c").astype(jnp.int32), chip.astype(jnp.int32)])

    W, pending = _gather_weights(P, chip1)
    loss, dx, _, pkf, full0, full1 = _local_step(x[0], mem[0], positions[0], loss_target[0], W, pending, place)
    loss = lax.psum(loss, ("x", "y", "c"))
    big, small = _finish_grads(pkf, full0, full1, place, chip1)

    grads, deltas, new_m, new_v = {}, {}, {}, {}
    for n in BIG:
        g = big[n]
        d, nm, nv = _adamw(_rows(P[n]), g, _rows(M1[n]), _rows(V2[n]), name=f"adamw_{n}")
        grads[n], deltas[n], new_m[n], new_v[n] = (t.reshape(P[n].shape) for t in (g, d, nm, nv))

    for n in SMALL:
        g = small[n]
        if n in SMALL_SHARDED:
            shard = P[n].shape[-1]
            g = lax.dynamic_slice_in_dim(g.reshape(-1, N_CHIPS * shard), chip * shard, shard, axis=1)
        elif n == "attn_sinks":
            g = g[:ATTN_HEADS]
        grads[n] = g.reshape(P[n].shape)
    packs = []
    for src in (P, grads, M1, V2):
        a = jnp.concatenate([_pad_rows(src[n]) for n in SMALL], axis=0)
        packs.append(jnp.pad(a, ((0, ADAM_SMALL_ROWS - a.shape[0]), (0, 0))))
    d_s, nm_s, nv_s = _adamw(*packs, name="adamw_small")
    o = 0
    for n in SMALL:
        cnt = math.prod(P[n].shape)
        r = -(-cnt // ROW)
        for dst, src in ((deltas, d_s), (new_m, nm_s), (new_v, nv_s)):
            dst[n] = src[o:o + r].reshape(-1)[:cnt].reshape(P[n].shape)
        o += r

    return (loss, dx[None], *[grads[n] for n in WEIGHTS], *[deltas[n] for n in WEIGHTS],
            *[new_m[n] for n in WEIGHTS], *[new_v[n] for n in WEIGHTS])
```

```python
import math

import jax
import jax.numpy as jnp
from jax import lax
from jax.experimental import pallas as pl
from jax.experimental.pallas import tpu as pltpu

F32 = jnp.float32
BF16 = jnp.bfloat16
MESH = pl.DeviceIdType.MESH

D_MODEL = 1024
DEPTH = 2
EPS = 1e-6
ATTN_HEADS = 16
ATTN_KV_HEADS = 4
HEAD_DIM = 64
WINDOW = 128
BLOCK = 128
ROPE_THETA = 500000.0
ROPE_DIM = 16
Q_W = 1024
KV_W = 256
MEM_LEN = 256
MEM_HEADS = 4
MEM_HEAD_DIM = 128
MEM_W = 512
LRU_BLOCKS = 8
LRU_C = 8.0
ATTN_IN_W = 2048
LRU_IN_W = 2560
MIX_OUT_W = 1536
D_FF = 4096
NEG = -1e30
N_CHIPS = 4

ADAM_LR = 0.001
ADAM_B1 = 0.9
ADAM_B2 = 0.999
ADAM_EPS = 1e-08
ADAM_WD = 0.01
ADAM_STEP = 10

LANES = 128
SCAN_ROWS = 512
VMEM_LIMIT = 56 * 1024 * 1024

NT = (((1,), (1,)), ((), ()))
TN = (((0,), (0,)), ((), ()))


def _cp(sem=None):
    return pltpu.CompilerParams(dimension_semantics=sem, vmem_limit_bytes=VMEM_LIMIT)


HBM = pl.BlockSpec(memory_space=pl.ANY)
GATHER_SEMS = 6


def _place():
    x, y, c = lax.axis_index("x"), lax.axis_index("y"), lax.axis_index("c")
    chips = [(1 - x, y), (x, 1 - y), (1 - x, 1 - y)]
    return x, y, c, chips


def _remote(src, dst, send_sems, recv_sems, k, to):
    return pltpu.make_async_remote_copy(src_ref=src, dst_ref=dst, send_sem=send_sems.at[k], recv_sem=recv_sems.at[k],
                                        device_id=to, device_id_type=MESH)


def _gather_start(o_ref, send_sems, recv_sems):
    x, y, c, chips = _place()
    half = o_ref.shape[1] // 2
    own = o_ref.at[2 * x + y, pl.ds(pl.multiple_of(c * half, 16), half)]
    for j, (cx, cy) in enumerate(chips):
        _remote(own, own, send_sems, recv_sems, j, (cx, cy, c)).start()


def _gather_forward(o_ref, send_sems, recv_sems):
    x, y, c, chips = _place()
    half = o_ref.shape[1] // 2
    my_rows = pl.ds(pl.multiple_of(c * half, 16), half)
    for j, (cx, cy) in enumerate(chips):
        landed = o_ref.at[2 * cx + cy, my_rows]
        _remote(landed, landed, send_sems, recv_sems, j, (cx, cy, c)).wait_recv()
        _remote(landed, landed, send_sems, recv_sems, 3 + j, (x, y, 1 - c)).start()


def _gather_drain(o_ref, send_sems, recv_sems):
    x, y, c, chips = _place()
    half = o_ref.shape[1] // 2
    my_rows = pl.ds(pl.multiple_of(c * half, 16), half)
    sib_rows = pl.ds(pl.multiple_of((1 - c) * half, 16), half)
    own = o_ref.at[2 * x + y, my_rows]
    for j, (cx, cy) in enumerate(chips):
        got = o_ref.at[2 * cx + cy, sib_rows]
        _remote(got, got, send_sems, recv_sems, 3 + j, (x, y, 1 - c)).wait_recv()
    for j, (cx, cy) in enumerate(chips):
        _remote(own, own, send_sems, recv_sems, j, (cx, cy, c)).wait_send()
        landed = o_ref.at[2 * cx + cy, my_rows]
        _remote(landed, landed, send_sems, recv_sems, 3 + j, (x, y, 1 - c)).wait_send()


def _gather_finish(o_ref, send_sems, recv_sems):
    _gather_forward(o_ref, send_sems, recv_sems)
    _gather_drain(o_ref, send_sems, recv_sems)


def _exchange_start(h_ref, o_ref, send_sems, recv_sems, rows=None, base=0):
    x, y, c, chips = _place()
    rows = pl.ds(0, h_ref.shape[1]) if rows is None else rows
    for j, (cx, cy) in enumerate(chips):
        _remote(h_ref.at[2 * cx + cy, rows], o_ref.at[2 * x + y, rows], send_sems, recv_sems, base + j, (cx, cy, c)).start()


def _exchange_finish(h_ref, o_ref, send_sems, recv_sems, rows=None, base=0):
    x, y, c, chips = _place()
    rows = pl.ds(0, h_ref.shape[1]) if rows is None else rows
    for j, (cx, cy) in enumerate(chips):
        got = o_ref.at[2 * cx + cy, rows]
        _remote(got, got, send_sems, recv_sems, base + j, (cx, cy, c)).wait_recv()
    for j, (cx, cy) in enumerate(chips):
        _remote(h_ref.at[2 * cx + cy, rows], o_ref.at[2 * x + y, rows], send_sems, recv_sems, base + j, (cx, cy, c)).wait_send()


def _sib_exchange_copies(g_ref, o_ref, send_sems, recv_sems):
    x, y, c, _ = _place()
    half = g_ref.shape[1] // 2
    other = pl.ds(pl.multiple_of((1 - c) * half, 8), half)
    return [_remote(g_ref.at[s, other], o_ref.at[s], send_sems, recv_sems, s, (x, y, 1 - c)) for s in range(N_CHIPS)]


def _sib_exchange_start(*refs):
    for cp in _sib_exchange_copies(*refs):
        cp.start()


def _sib_exchange_finish(*refs):
    for cp in _sib_exchange_copies(*refs):
        cp.wait()


def _sib_allgather_start(*refs):
    *o_refs, send_sems, recv_sems = refs
    x, y, c, _ = _place()
    for i, o_ref in enumerate(o_refs):
        half = o_ref.shape[0] // 2
        mine = o_ref.at[pl.ds(pl.multiple_of(c * half, 8), half)]
        _remote(mine, mine, send_sems, recv_sems, i, (x, y, 1 - c)).start()


def _sib_allgather_finish(*refs):
    *o_refs, send_sems, recv_sems = refs
    x, y, c, _ = _place()
    for i, o_ref in enumerate(o_refs):
        half = o_ref.shape[0] // 2
        mine = o_ref.at[pl.ds(pl.multiple_of(c * half, 8), half)]
        got = o_ref.at[pl.ds(pl.multiple_of((1 - c) * half, 8), half)]
        _remote(got, got, send_sems, recv_sems, i, (x, y, 1 - c)).wait_recv()
        _remote(mine, mine, send_sems, recv_sems, i, (x, y, 1 - c)).wait_send()


class _Rider:
    def __init__(self, args, start, finish, inplace=1, mid=None):
        self.args, self.start, self.finish, self.inplace, self.mid = list(args), start, finish, inplace, mid


def _gather_rider(buf):
    return None if buf is None else _Rider([buf], _gather_start, _gather_finish, mid=(_gather_forward, _gather_drain))


def _exchange_rider(*parts):
    n = len(parts)
    assert 3 * n <= GATHER_SEMS

    def run(fn):
        def go(*refs):
            sems = refs[2 * n:]
            for i, (_, _, r0, nr) in enumerate(parts):
                fn(refs[i], refs[n + i], *sems, rows=pl.ds(r0, nr), base=3 * i)
        return go

    return _Rider([p[0] for p in parts] + [p[1] for p in parts], run(_exchange_start), run(_exchange_finish), inplace=n)


def _sib_exchange_rider(g):
    landing = lax.empty((N_CHIPS, g.shape[1] // 2, g.shape[2]), g.dtype)
    return _Rider([g, landing], _sib_exchange_start, _sib_exchange_finish)


def _sib_allgather_rider(*fulls):
    return _Rider(fulls, _sib_allgather_start, _sib_allgather_finish, inplace=len(fulls))


class _Hosted:
    def __init__(self, rider, n_in, n_out):
        self.rider = rider
        self.on = rider is not None
        self.args = rider.args if self.on else []
        k = len(self.args)
        p = self.p = rider.inplace if self.on else 0
        self.alias = {n_in + k - p + i: n_out + i for i in range(p)}
        self.in_specs = [HBM] * k
        self.out_specs = [HBM] * p
        self.out_shape = [jax.ShapeDtypeStruct(a.shape, a.dtype) for a in self.args[k - p:]]
        self.scratch = [pltpu.SemaphoreType.DMA((GATHER_SEMS,)), pltpu.SemaphoreType.DMA((GATHER_SEMS,))] if self.on else []

    def split(self, refs, n_in, n_out):
        refs = list(refs)
        if not self.on:
            return refs[:n_in], refs[n_in:n_in + n_out], refs[n_in + n_out:], None
        k, p = len(self.args), self.p
        ins, outs = refs[:n_in], refs[n_in + k:n_in + k + n_out]
        rest = refs[n_in + k + n_out + p:]
        rrefs = refs[n_in:n_in + k - p] + refs[n_in + k + n_out:n_in + k + n_out + p] + [rest[-2], rest[-1]]
        return ins, outs, rest[:-2], rrefs

    def run(self, rrefs, step, n_steps, compute):
        if rrefs is None:
            return compute()

        mid = self.rider.mid
        mid_step = (3 * n_steps) // 4
        two_stage = mid is not None and 0 < mid_step < n_steps - 1

        @pl.when(step == 0)
        def _():
            self.rider.start(*rrefs)

        compute()

        if two_stage:
            @pl.when(step == mid_step)
            def _():
                mid[0](*rrefs)

        @pl.when(step == n_steps - 1)
        def _():
            (mid[1] if two_stage else self.rider.finish)(*rrefs)


def _mm_nn(a, w3, *, name, out_dtype=F32, norm_g=None, resid=None, relu2=False, tm=512, gather=None):
    M, K = a.shape
    ns, _, n = w3.shape
    N = ns * n
    tm = min(tm, M)
    has_norm = norm_g is not None
    has_res = resid is not None
    n_in = 2 + has_norm + has_res
    n_out = (2 if relu2 else 1) + has_norm
    host = _Hosted(_gather_rider(gather), n_in, n_out)

    def body(*refs):
        ins, outs, _, gref = host.split(refs, n_in, n_out)
        a_ref, w_ref = ins[0], ins[1]
        g_ref = ins[2] if has_norm else None
        r_ref = ins[-1] if has_res else None

        def compute():
            if has_norm:
                xv = a_ref[...]
                rs = lax.rsqrt(jnp.mean(xv * xv, axis=-1, keepdims=True) + EPS)
                ab = (xv * rs * g_ref[...]).astype(BF16)
                outs[-1][...] = ab
            else:
                ab = a_ref[...]
            for s in range(ns):
                acc = jnp.dot(ab, w_ref[s], preferred_element_type=F32)
                sl = slice(s * n, (s + 1) * n)
                if relu2:
                    outs[0][:, sl] = acc.astype(BF16)
                    rl = jnp.maximum(acc, 0.0)
                    outs[1][:, sl] = (rl * rl).astype(BF16)
                elif has_res:
                    outs[0][:, sl] = r_ref[:, sl] + acc
                else:
                    outs[0][:, sl] = acc.astype(out_dtype)

        host.run(gref, pl.program_id(0), M // tm, compute)

    row = lambda w: pl.BlockSpec((tm, w), lambda i: (i, 0))
    in_specs = [row(K), pl.BlockSpec((ns, K, n), lambda i: (0, 0, 0))]
    args = [a, w3]
    if has_norm:
        in_specs.append(pl.BlockSpec((1, K), lambda i: (0, 0)))
        args.append(norm_g.reshape(1, K))
    if has_res:
        in_specs.append(row(N))
        args.append(resid)
    if relu2:
        out_shape = [jax.ShapeDtypeStruct((M, N), BF16), jax.ShapeDtypeStruct((M, N), BF16)]
        out_specs = [row(N), row(N)]
    else:
        out_shape = [jax.ShapeDtypeStruct((M, N), F32 if has_res else out_dtype)]
        out_specs = [row(N)]
    if has_norm:
        out_shape.append(jax.ShapeDtypeStruct((M, K), BF16))
        out_specs.append(row(K))
    res = pl.pallas_call(body, grid=(M // tm,), in_specs=in_specs + host.in_specs, out_specs=out_specs + host.out_specs,
                         out_shape=out_shape + host.out_shape, scratch_shapes=host.scratch, input_output_aliases=host.alias,
                         name=name, compiler_params=_cp(("arbitrary",) if host.on else ("parallel",)))(*args, *host.args)
    return res if len(res) > 1 else res[0]


def _mm_nt(g, w3, *, name, out_dtype=BF16, up=None, norm_x=None, norm_g=None, dres=None, tm=512):
    M = g.shape[0]
    ns, K, n = w3.shape
    tm = min(tm, M)
    has_up = up is not None
    has_norm = norm_x is not None
    has_res = dres is not None

    def body(*refs):
        refs = list(refs)
        g_ref, w_ref = refs[0], refs[1]
        pos = 2
        if has_up:
            up_ref = refs[pos]
            pos += 1
        if has_norm:
            x_ref, gn_ref = refs[pos], refs[pos + 1]
            pos += 2
        if has_res:
            r_ref = refs[pos]
            pos += 1
        outs = refs[pos:]
        acc = None
        for s in range(ns):
            part = lax.dot_general(g_ref[:, s * n:(s + 1) * n], w_ref[s], NT, preferred_element_type=F32)
            acc = part if acc is None else acc + part
        if has_up:
            outs[0][...] = (acc * (2.0 * jnp.maximum(up_ref[...].astype(F32), 0.0))).astype(BF16)
        elif has_norm:
            xv = x_ref[...]
            rs = lax.rsqrt(jnp.mean(xv * xv, axis=-1, keepdims=True) + EPS)
            xn = xv * rs
            dxn = acc * gn_ref[...]
            dx = rs * (dxn - xn * jnp.mean(dxn * xn, axis=-1, keepdims=True))
            if has_res:
                dx = dx + r_ref[...]
            outs[0][...] = dx
            outs[1][...] = dx.astype(BF16)

            @pl.when(pl.program_id(0) == 0)
            def _():
                outs[2][...] = jnp.zeros_like(outs[2])

            outs[2][...] += jnp.sum(acc * xn, axis=0, keepdims=True)
        else:
            outs[0][...] = acc.astype(out_dtype)

    row = lambda w: pl.BlockSpec((tm, w), lambda i: (i, 0))
    in_specs = [row(ns * n), pl.BlockSpec((ns, K, n), lambda i: (0, 0, 0))]
    args = [g, w3]
    if has_up:
        in_specs.append(row(K))
        args.append(up)
    if has_norm:
        in_specs += [row(K), pl.BlockSpec((1, K), lambda i: (0, 0))]
        args += [norm_x, norm_g.reshape(1, K)]
    if has_res:
        in_specs.append(row(K))
        args.append(dres)
    if has_norm:
        out_shape = [jax.ShapeDtypeStruct((M, K), F32), jax.ShapeDtypeStruct((M, K), BF16),
                     jax.ShapeDtypeStruct((1, K), F32)]
        out_specs = [row(K), row(K), pl.BlockSpec((1, K), lambda i: (0, 0))]
        sem = ("arbitrary",)
    else:
        out_shape = [jax.ShapeDtypeStruct((M, K), BF16 if has_up else out_dtype)]
        out_specs = [row(K)]
        sem = ("parallel",)
    res = pl.pallas_call(body, grid=(M // tm,), in_specs=in_specs, out_specs=out_specs, out_shape=out_shape,
                         name=name, compiler_params=_cp(sem))(*args)
    return res if len(res) > 1 else res[0]


def _mm_tn(a, g, ns, *, name, tk=512, tm=4096, packed=None, rider=None):
    M, K = a.shape
    n = g.shape[1] // ns
    tm = min(tm, M)
    tk = min(tk, K)
    nk, nm = K // tk, M // tm
    n_in = 3 if (packed is not None and packed[0] is not None) else 2
    host = _Hosted(rider, n_in, 1)

    def body(*refs):
        ins, outs, _, rrefs = host.split(refs, n_in, 1)
        a_ref, g_ref, o_ref = ins[0], ins[1], outs[0]

        def compute():
            @pl.when(pl.program_id(2) == 0)
            def _():
                o_ref[...] = jnp.zeros_like(o_ref)

            o_ref[0] += lax.dot_general(a_ref[...], g_ref[...], TN, preferred_element_type=F32)

        step = (pl.program_id(0) * nk + pl.program_id(1)) * nm + pl.program_id(2)
        host.run(rrefs, step, ns * nk * nm, compute)

    in_specs = [pl.BlockSpec((tm, tk), lambda s, k, m: (m, k)), pl.BlockSpec((tm, n), lambda s, k, m: (m, s))]
    args = [a, g]
    alias = {}
    if packed is None:
        out_spec = pl.BlockSpec((1, tk, n), lambda s, k, m: (s, k, 0))
        out_shape = jax.ShapeDtypeStruct((ns, K, n), F32)
    else:
        buf, rows, off = packed
        per_chip = K * ns // N_CHIPS
        assert n == ROW and per_chip % tk == 0 and off % tk == 0
        if ns == N_CHIPS:
            out_spec = pl.BlockSpec((1, tk, n), lambda s, k, m: (s, off // tk + k, 0))
        else:
            kpc = per_chip // tk
            out_spec = pl.BlockSpec((1, tk, n), lambda s, k, m: (k // kpc, off // tk + k % kpc, 0))
        out_shape = jax.ShapeDtypeStruct((N_CHIPS, rows, ROW), F32)
        if buf is not None:
            in_specs.append(HBM)
            args.append(buf)
            alias = {2: 0}
    sem = ("arbitrary",) * 3 if host.on else ("parallel", "parallel", "arbitrary")
    res = pl.pallas_call(
        body, grid=(ns, nk, nm), in_specs=in_specs + host.in_specs, out_specs=[out_spec] + host.out_specs,
        out_shape=[out_shape] + host.out_shape, scratch_shapes=host.scratch, name=name,
        input_output_aliases={**alias, **host.alias}, compiler_params=_cp(sem))(*args, *host.args)
    return res if host.on else res[0]


def _final(act, w_down, x, gain, target, *, name="mlp_down_final", tr=512):
    S, Dm = x.shape
    tr = min(tr, S)
    Kf = act.shape[1]

    def body(a_ref, w_ref, x_ref, g_ref, t_ref, loss_ref, dx_ref, dxb_ref, dg_ref):
        @pl.when(pl.program_id(0) == 0)
        def _():
            loss_ref[...] = jnp.zeros_like(loss_ref)
            dg_ref[...] = jnp.zeros_like(dg_ref)

        xv = x_ref[...] + jnp.dot(a_ref[...], w_ref[0], preferred_element_type=F32)
        gv = g_ref[...]
        rs = lax.rsqrt(jnp.mean(xv * xv, axis=-1, keepdims=True) + EPS)
        xn = xv * rs
        err = xn * gv - t_ref[...]
        loss_ref[...] += 0.5 * jnp.sum(jnp.mean(err * err, axis=-1, keepdims=True), axis=0, keepdims=True)
        dout = err * (1.0 / Dm)
        dg_ref[...] += jnp.sum(dout * xn, axis=0, keepdims=True)
        dxn = dout * gv
        dx = rs * (dxn - xn * jnp.mean(dxn * xn, axis=-1, keepdims=True))
        dx_ref[...] = dx
        dxb_ref[...] = dx.astype(BF16)

    row = pl.BlockSpec((tr, Dm), lambda i: (i, 0))
    return pl.pallas_call(
        body, grid=(S // tr,),
        in_specs=[pl.BlockSpec((tr, Kf), lambda i: (i, 0)), pl.BlockSpec((1, Kf, Dm), lambda i: (0, 0, 0)), row,
                  pl.BlockSpec((1, Dm), lambda i: (0, 0)), row],
        out_specs=[pl.BlockSpec((1, 1), lambda i: (0, 0)), row, row, pl.BlockSpec((1, Dm), lambda i: (0, 0))],
        out_shape=[jax.ShapeDtypeStruct((1, 1), F32), jax.ShapeDtypeStruct((S, Dm), F32),
                   jax.ShapeDtypeStruct((S, Dm), BF16), jax.ShapeDtypeStruct((1, Dm), F32)],
        name=name, compiler_params=_cp(("arbitrary",)))(act, w_down, x, gain.reshape(1, Dm), target)


def _rope_tables(positions):
    half = ROPE_DIM // 2
    inv_freq = ROPE_THETA ** (-2.0 * jnp.arange(half, dtype=F32) / ROPE_DIM)
    ang = positions.astype(F32)[:, None] * inv_freq
    cos, sin = jnp.cos(ang), jnp.sin(ang)
    S = positions.shape[0]
    ones = jnp.ones((S, HEAD_DIM - ROPE_DIM), F32)
    cos64 = jnp.concatenate([cos, cos, ones], axis=1)
    sin64 = jnp.concatenate([-sin, sin, 0.0 * ones], axis=1)
    return jnp.tile(cos64, (1, 2)), jnp.tile(sin64, (1, 2))


def _rope_partner(t):
    lane = lax.broadcasted_iota(jnp.int32, t.shape, 1)
    low = (lane & (HEAD_DIM - 1)) < (ROPE_DIM // 2)
    return jnp.where(low, pltpu.roll(t, LANES - ROPE_DIM // 2, 1), pltpu.roll(t, ROPE_DIM // 2, 1))


def _qk_prep(p, cos_t, sin_t, *, name="qk_prep", tr=256, gather=None):
    S = p.shape[0]
    tr = min(tr, S)
    scale = HEAD_DIM ** -0.5
    host = _Hosted(_gather_rider(gather), 3, 4)

    def body(*refs):
        ins, outs, _, gref = host.split(refs, 3, 4)
        host.run(gref, pl.program_id(0), S // tr, lambda: inner(*ins, *outs))

    def inner(p_ref, c_ref, s_ref, q_ref, k_ref, v_ref, va_ref):
        cs, sn = c_ref[...], s_ref[...]
        lane = lax.broadcasted_iota(jnp.int32, (tr, LANES), 1)
        lo = lane < HEAD_DIM
        for c in range(Q_W // LANES):
            t = p_ref[:, c * LANES:(c + 1) * LANES]
            q_ref[:, c * LANES:(c + 1) * LANES] = ((t * cs + _rope_partner(t) * sn) * scale).astype(BF16)
        for c in range(KV_W // LANES):
            t = p_ref[:, Q_W + c * LANES:Q_W + (c + 1) * LANES]
            kc = t * cs + _rope_partner(t) * sn
            vc = p_ref[:, Q_W + KV_W + c * LANES:Q_W + KV_W + (c + 1) * LANES]
            for arr, ref in ((kc, k_ref), (vc, v_ref)):
                sw = pltpu.roll(arr, HEAD_DIM, 1)
                ref[:, (2 * c) * LANES:(2 * c + 1) * LANES] = jnp.where(lo, arr, sw).astype(BF16)
                ref[:, (2 * c + 1) * LANES:(2 * c + 2) * LANES] = jnp.where(lo, sw, arr).astype(BF16)
            sw = pltpu.roll(vc, HEAD_DIM, 1)
            for k, aug in enumerate((jnp.where(lo, vc, 1.0), jnp.where(lo, 1.0, sw), jnp.where(lo, sw, 1.0), jnp.where(lo, 1.0, vc))):
                va_ref[:, (4 * c + k) * LANES:(4 * c + k + 1) * LANES] = aug.astype(BF16)

    row = lambda w: pl.BlockSpec((tr, w), lambda i: (i, 0))
    return pl.pallas_call(
        body, grid=(S // tr,), in_specs=[row(ATTN_IN_W), row(LANES), row(LANES)] + host.in_specs,
        out_specs=[row(Q_W), row(2 * KV_W), row(2 * KV_W), row(4 * KV_W)] + host.out_specs,
        out_shape=[jax.ShapeDtypeStruct((S, Q_W), BF16), jax.ShapeDtypeStruct((S, 2 * KV_W), BF16),
                   jax.ShapeDtypeStruct((S, 2 * KV_W), BF16), jax.ShapeDtypeStruct((S, 4 * KV_W), BF16)] + host.out_shape,
        scratch_shapes=host.scratch, input_output_aliases=host.alias,
        name=name, compiler_params=_cp(("arbitrary",) if host.on else ("parallel",)))(p, cos_t, sin_t, *host.args)


def _qk_prep_bwd(dq, dk, dv, dmq, cos_t, sin_t, *, name="qk_prep_bwd", tr=256):
    S = dq.shape[0]
    tr = min(tr, S)

    def body(dq_ref, dk_ref, dv_ref, dmq_ref, c_ref, s_ref, o_ref):
        cs, sn = c_ref[...], s_ref[...]
        for c in range(Q_W // LANES):
            t = dq_ref[:, c * LANES:(c + 1) * LANES]
            o_ref[:, c * LANES:(c + 1) * LANES] = (t * cs - _rope_partner(t) * sn).astype(BF16)
        for c in range(KV_W // LANES):
            t = dk_ref[:, c * LANES:(c + 1) * LANES]
            o_ref[:, Q_W + c * LANES:Q_W + (c + 1) * LANES] = (t * cs - _rope_partner(t) * sn).astype(BF16)
        o_ref[:, Q_W + KV_W:Q_W + 2 * KV_W] = dv_ref[...].astype(BF16)
        o_ref[:, Q_W + 2 * KV_W:] = dmq_ref[...]

    row = lambda w: pl.BlockSpec((tr, w), lambda i: (i, 0))
    return pl.pallas_call(
        body, grid=(S // tr,), in_specs=[row(Q_W), row(KV_W), row(KV_W), row(MEM_W), row(LANES), row(LANES)],
        out_specs=row(ATTN_IN_W), out_shape=jax.ShapeDtypeStruct((S, ATTN_IN_W), BF16),
        name=name, compiler_params=_cp(("parallel",)))(dq, dk, dv, dmq, cos_t, sin_t)


def _band(n, S):
    start = pl.multiple_of(jnp.clip((n - 1) * BLOCK, 0, S - 3 * BLOCK), BLOCK)
    qi = lax.broadcasted_iota(jnp.int32, (BLOCK, 3 * BLOCK), 0) + n * BLOCK
    ki = lax.broadcasted_iota(jnp.int32, (BLOCK, 3 * BLOCK), 1) + start
    return start, jnp.abs(ki - qi) <= WINDOW


def _head_operand(ref, h, lo):
    c = h // 2
    t = ref[:, c * LANES:(c + 1) * LANES].astype(F32)
    return jnp.where(lo if h % 2 == 0 else jnp.logical_not(lo), t, 0.0).astype(BF16)


GROUP = ATTN_HEADS // ATTN_KV_HEADS
EVENS_FIRST = (0, 2, 1, 3)


def _attn_fwd(q, kd, va, sinks, *, name="attn_fwd", gather=None):
    S = q.shape[0]
    host = _Hosted(_gather_rider(gather), 4, 2)

    def body(*refs):
        ins, outs, scr, gref = host.split(refs, 4, 2)
        host.run(gref, pl.program_id(0), S // BLOCK, lambda: inner(*ins, *outs, *scr))

    def inner(sink_ref, q_ref, k_ref, va_ref, o_ref, lse_ref, p_scr):
        n = pl.program_id(0)
        start, mask = _band(n, S)
        lane = lax.broadcasted_iota(jnp.int32, (BLOCK, LANES), 1)
        lo = lane < HEAD_DIM
        rows = pl.ds(start, 3 * BLOCK)
        scores = []
        for g in range(ATTN_KV_HEADS):
            qst = jnp.concatenate([_head_operand(q_ref, GROUP * g + j, lo) for j in EVENS_FIRST], axis=0)
            scores.append(lax.dot_general(qst, k_ref[rows, g * LANES:(g + 1) * LANES], NT, preferred_element_type=F32))
        ms = {}
        for g in range(ATTN_KV_HEADS):
            for pos, j in enumerate(EVENS_FIRST):
                h = GROUP * g + j
                s = jnp.where(mask, scores[g][pos * BLOCK:(pos + 1) * BLOCK], NEG)
                ms[h] = jnp.maximum(jnp.max(s, axis=-1, keepdims=True), sink_ref[h])
                p_scr[(GROUP * g + pos) * BLOCK:(GROUP * g + pos + 1) * BLOCK, :] = jnp.exp(s - ms[h]).astype(BF16)
        pvs = {}
        for g in range(ATTN_KV_HEADS):
            for par in range(2):
                r0 = (GROUP * g + 2 * par) * BLOCK
                pvs[g, par] = jnp.dot(p_scr[r0:r0 + 2 * BLOCK, :], va_ref[rows, (2 * g + par) * LANES:(2 * g + par + 1) * LANES],
                                      preferred_element_type=F32)
        lse_blk = jnp.zeros((BLOCK, LANES), F32)
        for g in range(ATTN_KV_HEADS):
            outs = {}
            for par in range(2):
                for k in range(2):
                    j = EVENS_FIRST[2 * par + k]
                    h = GROUP * g + j
                    pv = pvs[g, par][k * BLOCK:(k + 1) * BLOCK]
                    den = pltpu.roll(pv, HEAD_DIM, 1) + jnp.exp(sink_ref[h] - ms[h])
                    outs[j] = pv * (1.0 / den)
                    l = den[:, par * HEAD_DIM:par * HEAD_DIM + 1]
                    lse_blk = jnp.where(lane == h, ms[h] + jnp.log(l), lse_blk)
            for jj in range(2):
                o_ref[:, (2 * g + jj) * LANES:(2 * g + jj + 1) * LANES] = jnp.where(lo, outs[2 * jj], outs[2 * jj + 1]).astype(BF16)
        lse_ref[...] = lse_blk

    full = lambda w: pl.BlockSpec((S, w), lambda i: (0, 0))
    return pl.pallas_call(
        body, grid=(S // BLOCK,),
        in_specs=[pl.BlockSpec(memory_space=pltpu.SMEM), pl.BlockSpec((BLOCK, Q_W), lambda i: (i, 0)),
                  full(2 * KV_W), full(4 * KV_W)] + host.in_specs,
        out_specs=[pl.BlockSpec((BLOCK, Q_W), lambda i: (i, 0)), pl.BlockSpec((BLOCK, LANES), lambda i: (i, 0))] + host.out_specs,
        out_shape=[jax.ShapeDtypeStruct((S, MIX_OUT_W), BF16), jax.ShapeDtypeStruct((S, LANES), F32)] + host.out_shape,
        scratch_shapes=[pltpu.VMEM((ATTN_HEADS * BLOCK, 3 * BLOCK), BF16)] + host.scratch, input_output_aliases=host.alias,
        name=name, compiler_params=_cp(("arbitrary",) if host.on else ("parallel",)))(sinks, q, kd, va, *host.args)


def _attn_bwd(q, kd, vd, ao, lse, sinks, dcat, *, name="attn_bwd", rider=None):
    S = q.shape[0]
    scale = HEAD_DIM ** -0.5
    host = _Hosted(rider, 7, 4)

    def body(*refs):
        ins, outs, scr, rrefs = host.split(refs, 7, 4)
        host.run(rrefs, pl.program_id(0), S // BLOCK, lambda: inner(*ins, *outs, *scr))

    def inner(sink_ref, q_ref, k_ref, v_ref, ao_ref, lse_ref, do_ref, dq_ref, dk_ref, dv_ref, ds_ref, p_scr, dsb_scr):
        n = pl.program_id(0)

        @pl.when(n == 0)
        def _():
            dk_ref[...] = jnp.zeros_like(dk_ref)
            dv_ref[...] = jnp.zeros_like(dv_ref)
            ds_ref[...] = jnp.zeros_like(ds_ref)

        start, mask = _band(n, S)
        lane = lax.broadcasted_iota(jnp.int32, (BLOCK, LANES), 1)
        lo = lane < HEAD_DIM
        lane3 = lax.broadcasted_iota(jnp.int32, (3 * BLOCK, LANES), 1)
        row8 = lax.broadcasted_iota(jnp.int32, (8, LANES), 0)
        lane8 = lax.broadcasted_iota(jnp.int32, (8, LANES), 1)
        dsink = jnp.zeros((8, LANES), F32)
        lse_blk = lse_ref[...]
        rows = pl.ds(start, 3 * BLOCK)
        lses, deltas = {}, {}
        for c in range(Q_W // LANES):
            prod = do_ref[:, c * LANES:(c + 1) * LANES].astype(F32) * ao_ref[:, c * LANES:(c + 1) * LANES].astype(F32)
            for k in range(2):
                h = 2 * c + k
                deltas[h] = jnp.sum(jnp.where(lo if k == 0 else jnp.logical_not(lo), prod, 0.0), axis=1, keepdims=True)
                lses[h] = jnp.sum(jnp.where(lane == h, lse_blk, 0.0), axis=1, keepdims=True)
                val = -jnp.sum(jnp.exp(sink_ref[h] - lses[h]) * deltas[h], axis=0, keepdims=True)
                dsink = dsink + jnp.where((row8 == 0) & (lane8 == h), val, 0.0)
        stack = lambda ref, g: jnp.concatenate([_head_operand(ref, GROUP * g + j, lo) for j in range(GROUP)], axis=0)
        ss, dps = [], []
        for g in range(ATTN_KV_HEADS):
            ss.append(lax.dot_general(stack(q_ref, g), k_ref[rows, g * LANES:(g + 1) * LANES], NT, preferred_element_type=F32))
            dps.append(lax.dot_general(stack(do_ref, g), v_ref[rows, g * LANES:(g + 1) * LANES], NT, preferred_element_type=F32))
        for g in range(ATTN_KV_HEADS):
            for j in range(GROUP):
                h = GROUP * g + j
                r = slice(j * BLOCK, (j + 1) * BLOCK)
                hr = slice(h * BLOCK, (h + 1) * BLOCK)
                p = jnp.exp(jnp.where(mask, ss[g][r], NEG) - lses[h])
                p_scr[hr, :] = p.astype(BF16)
                dsb_scr[hr, :] = (p * (dps[g][r] - deltas[h])).astype(BF16)
        for g in range(ATTN_KV_HEADS):
            cols = slice((g // 2) * LANES, (g // 2 + 1) * LANES)
            gr = slice(GROUP * g * BLOCK, GROUP * (g + 1) * BLOCK)
            dsg = dsb_scr[gr, :]
            dqs = jnp.dot(dsg, k_ref[rows, g * LANES:(g + 1) * LANES], preferred_element_type=F32) * scale
            for jj in range(2):
                dq_ref[:, (2 * g + jj) * LANES:(2 * g + jj + 1) * LANES] = jnp.where(
                    lo, dqs[(2 * jj) * BLOCK:(2 * jj + 1) * BLOCK], dqs[(2 * jj + 1) * BLOCK:(2 * jj + 2) * BLOCK])
            half = (lane3 < HEAD_DIM) if g % 2 == 0 else (lane3 >= HEAD_DIM)
            dkr = lax.dot_general(dsg, stack(q_ref, g), TN, preferred_element_type=F32)
            dk_ref[rows, cols] += jnp.where(half, dkr + pltpu.roll(dkr, HEAD_DIM, 1), 0.0)
            dvr = lax.dot_general(p_scr[gr, :], stack(do_ref, g), TN, preferred_element_type=F32)
            dv_ref[rows, cols] += jnp.where(half, dvr + pltpu.roll(dvr, HEAD_DIM, 1), 0.0)
        ds_ref[...] += dsink

    full = lambda w: pl.BlockSpec((S, w), lambda i: (0, 0))
    blk = lambda w: pl.BlockSpec((BLOCK, w), lambda i: (i, 0))
    return pl.pallas_call(
        body, grid=(S // BLOCK,),
        in_specs=[pl.BlockSpec(memory_space=pltpu.SMEM), blk(Q_W), full(2 * KV_W), full(2 * KV_W), blk(Q_W), blk(LANES), blk(Q_W)]
        + host.in_specs,
        out_specs=[blk(Q_W), full(KV_W), full(KV_W), pl.BlockSpec((8, LANES), lambda i: (0, 0))] + host.out_specs,
        out_shape=[jax.ShapeDtypeStruct((S, Q_W), F32), jax.ShapeDtypeStruct((S, KV_W), F32),
                   jax.ShapeDtypeStruct((S, KV_W), F32), jax.ShapeDtypeStruct((8, LANES), F32)] + host.out_shape,
        scratch_shapes=[pltpu.VMEM((ATTN_HEADS * BLOCK, 3 * BLOCK), BF16), pltpu.VMEM((ATTN_HEADS * BLOCK, 3 * BLOCK), BF16)]
        + host.scratch, input_output_aliases=host.alias,
        name=name, compiler_params=_cp(("arbitrary",)))(sinks, q, kd, vd, ao, lse, dcat, *host.args)


def _mem_probs(q_ref, kv_ref, h):
    scale = MEM_HEAD_DIM ** -0.5
    qh = q_ref[:, h * LANES:(h + 1) * LANES].astype(BF16)
    s = lax.dot_general(qh, kv_ref[:, h * LANES:(h + 1) * LANES], NT, preferred_element_type=F32) * scale
    m = jnp.max(s, axis=-1, keepdims=True)
    pe = jnp.exp(s - m)
    return qh, pe * (1.0 / jnp.sum(pe, axis=-1, keepdims=True))


def _memattn_fwd(p, qblk, kv, cat, *, name="memattn_fwd", tr=512):
    S = p.shape[0]
    tr = min(tr, S)

    def body(q_ref, kv_ref, cat_ref, o_ref):
        for h in range(MEM_HEADS):
            _, pr = _mem_probs(q_ref, kv_ref, h)
            o = jnp.dot(pr.astype(BF16), kv_ref[:, MEM_W + h * LANES:MEM_W + (h + 1) * LANES], preferred_element_type=F32)
            o_ref[:, h * LANES:(h + 1) * LANES] = o.astype(BF16)

    return pl.pallas_call(
        body, grid=(S // tr,),
        in_specs=[pl.BlockSpec((tr, MEM_W), lambda i: (i, qblk)), pl.BlockSpec((MEM_LEN, 2 * MEM_W), lambda i: (0, 0)), HBM],
        out_specs=pl.BlockSpec((tr, MEM_W), lambda i: (i, Q_W // MEM_W)),
        out_shape=jax.ShapeDtypeStruct((S, MIX_OUT_W), BF16), input_output_aliases={2: 0},
        name=name, compiler_params=_cp(("parallel",)))(p, kv, cat)


def _memattn_bwd(p, qblk, kv, dcat, *, name="memattn_bwd", tr=512, rider=None):
    S = p.shape[0]
    tr = min(tr, S)
    scale = MEM_HEAD_DIM ** -0.5
    host = _Hosted(rider, 3, 2)

    def body(*refs):
        ins, outs, _, rrefs = host.split(refs, 3, 2)
        host.run(rrefs, pl.program_id(0), S // tr, lambda: inner(*ins, *outs))

    def inner(q_ref, kv_ref, do_ref, dq_ref, dkv_ref):
        @pl.when(pl.program_id(0) == 0)
        def _():
            dkv_ref[...] = jnp.zeros_like(dkv_ref)

        for h in range(MEM_HEADS):
            qh, pr = _mem_probs(q_ref, kv_ref, h)
            doh = do_ref[:, h * LANES:(h + 1) * LANES]
            dp = lax.dot_general(doh, kv_ref[:, MEM_W + h * LANES:MEM_W + (h + 1) * LANES], NT, preferred_element_type=F32)
            delta = jnp.sum(pr * dp, axis=-1, keepdims=True)
            dsb = (pr * (dp - delta) * scale).astype(BF16)
            dq = jnp.dot(dsb, kv_ref[:, h * LANES:(h + 1) * LANES], preferred_element_type=F32)
            dq_ref[:, h * LANES:(h + 1) * LANES] = dq.astype(BF16)
            dkv_ref[:, h * LANES:(h + 1) * LANES] += lax.dot_general(dsb, qh, TN, preferred_element_type=F32)
            dkv_ref[:, MEM_W + h * LANES:MEM_W + (h + 1) * LANES] += lax.dot_general(
                pr.astype(BF16), doh, TN, preferred_element_type=F32)

    return pl.pallas_call(
        body, grid=(S // tr,),
        in_specs=[pl.BlockSpec((tr, MEM_W), lambda i: (i, qblk)), pl.BlockSpec((MEM_LEN, 2 * MEM_W), lambda i: (0, 0)),
                  pl.BlockSpec((tr, MEM_W), lambda i: (i, Q_W // MEM_W))] + host.in_specs,
        out_specs=[pl.BlockSpec((tr, MEM_W), lambda i: (i, 0)), pl.BlockSpec((MEM_LEN, 2 * MEM_W), lambda i: (0, 0))]
        + host.out_specs,
        out_shape=[jax.ShapeDtypeStruct((S, MEM_W), BF16), jax.ShapeDtypeStruct((MEM_LEN, 2 * MEM_W), F32)] + host.out_shape,
        scratch_shapes=host.scratch, input_output_aliases=host.alias,
        name=name, compiler_params=_cp(("arbitrary",)))(p, kv, dcat, *host.args)


def _sqrt(v):
    return jnp.where(v > 0.0, v * lax.rsqrt(v), 0.0)


def _sigmoid(z):
    return 1.0 / (1.0 + jnp.exp(-z))


def _one_minus_exp(z, exp_z):
    poly = z * (1.0 + z * (0.5 + z * (1.0 / 6.0 + z * (1.0 / 24.0 + z * (1.0 / 120.0)))))
    return jnp.where(z > -0.1, -poly, 1.0 - exp_z)


def _softplus_neg(lam):
    z = -lam
    return jnp.maximum(z, 0.0) + jnp.log(1.0 + jnp.exp(-jnp.abs(z)))


_GELU_C = math.sqrt(2.0 / math.pi)


def _gelu(z):
    return 0.5 * z * (1.0 + jnp.tanh(_GELU_C * (z + 0.044715 * z * z * z)))


def _row_or_zero(ref, t, S):
    ok = jnp.logical_and(t >= 0, t < S)
    return jnp.where(ok, ref[pl.ds(jnp.clip(t, 0, S - 1), 1), :], 0.0)


def _shift_down(v, first):
    ri = lax.broadcasted_iota(jnp.int32, v.shape, 0)
    return jnp.where(ri == 0, first, pltpu.roll(v, 1, 0))


def _shift_up(v, last):
    T = v.shape[0]
    ri = lax.broadcasted_iota(jnp.int32, v.shape, 0)
    return jnp.where(ri == T - 1, last, pltpu.roll(v, T - 1, 0))


def _scan_chunk(a, u, reverse):
    T = a.shape[0]
    ri = lax.broadcasted_iota(jnp.int32, a.shape, 0)
    d = 1
    while d < T:
        if reverse:
            a_s, u_s, ok = pltpu.roll(a, T - d, 0), pltpu.roll(u, T - d, 0), ri < T - d
        else:
            a_s, u_s, ok = pltpu.roll(a, d, 0), pltpu.roll(u, d, 0), ri >= d
        u = jnp.where(ok, a * u_s + u, u)
        a = jnp.where(ok, a * a_s, a)
        d *= 2
    return a, u


def _conv_taps(xb_ref, t0, S):
    T = SCAN_ROWS
    x0 = xb_ref[pl.ds(t0, T), :]
    xm1 = _shift_down(x0, _row_or_zero(xb_ref, t0 - 1, S))
    nxt0 = _row_or_zero(xb_ref, t0 + T, S)
    xp1 = _shift_up(x0, nxt0)
    xp2 = _shift_up(xp1, _row_or_zero(xb_ref, t0 + T + 1, S))
    return xm1, x0, xp1, xp2


def _lru_gates(xc, w_a, b_a, w_x, b_x, sp):
    xcb = xc.astype(BF16)
    r = _sigmoid(jnp.dot(xcb, w_a, preferred_element_type=F32) + b_a)
    i = _sigmoid(jnp.dot(xcb, w_x, preferred_element_type=F32) + b_x)
    la = -LRU_C * r * sp
    a = jnp.exp(la)
    b2 = _one_minus_exp(2.0 * la, a * a)
    inv_beta = lax.rsqrt(b2)
    return r, i, a, jnp.where(b2 > 0.0, b2 * inv_beta, 0.0), inv_beta


def _lru_specs(S):
    col = lambda off: pl.BlockSpec((S, LANES), lambda n: (0, n + off), pipeline_mode=pl.Buffered(1))
    small = lambda r: pl.BlockSpec((r, LANES), lambda n: (0, n))
    wblk = pl.BlockSpec((2, 1, LANES, LANES), lambda n: (0, n, 0, 0))
    return col, small, wblk


def _lru_fwd(p, conv_w, conv_b, wa, ba, wx, bx, lam, *, name="lru_fwd"):
    S = p.shape[0]
    T = SCAN_ROWS
    nc = S // T

    def body(xb_ref, gate_ref, cw_ref, cb_ref, wa_ref, ba_ref, wx_ref, bx_ref, lam_ref, y_ref, hf_ref, hr_ref, xc_v):
        sp = _softplus_neg(lam_ref[...])
        cw = cw_ref[...]

        def fwd_step(c, h_in):
            t0 = pl.multiple_of(c * T, T)
            xm1, x0, xp1, xp2 = _conv_taps(xb_ref, t0, S)
            xc = cb_ref[...] + xm1 * cw[0:1] + x0 * cw[1:2] + xp1 * cw[2:3] + xp2 * cw[3:4]
            xc_v[pl.ds(t0, T), :] = xc
            _, i, a, beta, _ = _lru_gates(xc, wa_ref[0, 0], ba_ref[0:1], wx_ref[0, 0], bx_ref[0:1], sp[0:1])
            A, U = _scan_chunk(a, beta * (i * xc), False)
            hf_ref[pl.ds(t0, T), :] = A * h_in + U
            return hf_ref[pl.ds(t0 + T - 1, 1), :]

        lax.fori_loop(0, nc, fwd_step, jnp.zeros((1, LANES), F32))

        def rev_step(k, h_in):
            t0 = pl.multiple_of((nc - 1 - k) * T, T)
            xc = xc_v[pl.ds(t0, T), :]
            _, i, a, beta, _ = _lru_gates(xc, wa_ref[1, 0], ba_ref[1:2], wx_ref[1, 0], bx_ref[1:2], sp[1:2])
            A, U = _scan_chunk(a, beta * (i * xc), True)
            h = A * h_in + U
            hr_ref[pl.ds(t0, T), :] = h
            y_ref[pl.ds(t0, T), :] = ((hf_ref[pl.ds(t0, T), :] + h) * _gelu(gate_ref[pl.ds(t0, T), :])).astype(BF16)
            return hr_ref[pl.ds(t0, 1), :]

        lax.fori_loop(0, nc, rev_step, jnp.zeros((1, LANES), F32))

    col, small, wblk = _lru_specs(S)
    colo = lambda: pl.BlockSpec((S, LANES), lambda n: (0, n))
    return pl.pallas_call(
        body, grid=(LRU_BLOCKS,),
        in_specs=[col(0), col(LRU_BLOCKS), small(4), small(1), wblk, small(2), wblk, small(2), small(2)],
        out_specs=[colo(), colo(), colo()],
        out_shape=[jax.ShapeDtypeStruct((S, MIX_OUT_W), BF16), jax.ShapeDtypeStruct((S, D_MODEL), F32),
                   jax.ShapeDtypeStruct((S, D_MODEL), F32)],
        scratch_shapes=[pltpu.VMEM((S, LANES), F32)],
        name=name, compiler_params=_cp(("parallel",)))(p, p, conv_w, conv_b, wa, ba, wx, bx, lam)


def _lru_bwd(p, hf, hr, dcat, conv_w, conv_b, wa, ba, wx, bx, lam, *, name="lru_bwd"):
    S = p.shape[0]
    T = SCAN_ROWS
    nc = S // T

    def body(xb_ref, gate_ref, hf_ref, hr_ref, dy_ref, cw_ref, cb_ref, wa_ref, ba_ref, wx_ref, bx_ref, lam_ref,
             dxb_ref, dgate_ref, dcw_ref, dcb_ref, dwa_ref, dba_ref, dwx_ref, dbx_ref, dlam_ref, xc_v, dxc_v, dh_v):
        lam_v = lam_ref[...]
        sp = _softplus_neg(lam_v)
        cw = cw_ref[...]
        for ref in (dcw_ref, dcb_ref, dwa_ref, dba_ref, dwx_ref, dbx_ref, dlam_ref):
            ref[...] = jnp.zeros_like(ref)

        def prep_step(c, carry):
            t0 = pl.multiple_of(c * T, T)
            rows = pl.ds(t0, T)
            xm1, x0, xp1, xp2 = _conv_taps(xb_ref, t0, S)
            xc_v[rows, :] = cb_ref[...] + xm1 * cw[0:1] + x0 * cw[1:2] + xp1 * cw[2:3] + xp2 * cw[3:4]
            z = gate_ref[rows, :]
            dy = dy_ref[rows, :].astype(F32)
            th = jnp.tanh(_GELU_C * (z + 0.044715 * z * z * z))
            dgelu = 0.5 * (1.0 + th) + 0.5 * z * (1.0 - th * th) * _GELU_C * (1.0 + 3.0 * 0.044715 * z * z)
            dgate_ref[rows, :] = (dy * (hf_ref[rows, :] + hr_ref[rows, :]) * dgelu).astype(BF16)
            dh_v[rows, :] = dy * (0.5 * z * (1.0 + th))
            return carry

        lax.fori_loop(0, nc, prep_step, 0)

        def direction(d):
            h_ref = hf_ref if d == 0 else hr_ref
            w_a, w_x = wa_ref[d, 0], wx_ref[d, 0]
            b_a, b_x, sp_d = ba_ref[d:d + 1], bx_ref[d:d + 1], sp[d:d + 1]

            def step(k, carry):
                g_in, a_in = carry
                c = (nc - 1 - k) if d == 0 else k
                t0 = pl.multiple_of(c * T, T)
                rows = pl.ds(t0, T)
                xc = xc_v[rows, :]
                r, i, a, beta, inv_beta = _lru_gates(xc, w_a, b_a, w_x, b_x, sp_d)
                dh = dh_v[rows, :]
                hc = h_ref[rows, :]
                if d == 0:
                    A, U = _scan_chunk(_shift_up(a, a_in), dh, True)
                    g = A * g_in + U
                    h_nb = _shift_down(hc, _row_or_zero(h_ref, t0 - 1, S))
                    nxt = (g[0:1], a[0:1])
                else:
                    A, U = _scan_chunk(_shift_down(a, a_in), dh, False)
                    g = A * g_in + U
                    h_nb = _shift_up(hc, _row_or_zero(h_ref, t0 + T, S))
                    nxt = (g[T - 1:T], a[T - 1:T])
                da = g * h_nb
                dbeta = g * (i * xc)
                tb = g * beta
                dla = da * a - dbeta * (a * a * inv_beta)
                dzr = (dla * (-LRU_C * sp_d)) * (r * (1.0 - r))
                dzi = (tb * xc) * (i * (1.0 - i))
                dzrb, dzib, xcb = dzr.astype(BF16), dzi.astype(BF16), xc.astype(BF16)
                dwa_ref[d, 0] += lax.dot_general(xcb, dzrb, TN, preferred_element_type=F32)
                dwx_ref[d, 0] += lax.dot_general(xcb, dzib, TN, preferred_element_type=F32)
                dba_ref[d:d + 1] += jnp.sum(dzr, axis=0, keepdims=True)
                dbx_ref[d:d + 1] += jnp.sum(dzi, axis=0, keepdims=True)
                dlam_ref[d:d + 1] += jnp.sum(dla * (-LRU_C * r), axis=0, keepdims=True)
                dxc = (tb * i + lax.dot_general(dzrb, w_a, NT, preferred_element_type=F32)
                       + lax.dot_general(dzib, w_x, NT, preferred_element_type=F32))
                if d == 0:
                    dxc_v[rows, :] = dxc
                else:
                    dxc_v[rows, :] += dxc
                return nxt

            lax.fori_loop(0, nc, step, (jnp.zeros((1, LANES), F32), jnp.zeros((1, LANES), F32)))

        direction(0)
        direction(1)
        dlam_ref[...] = dlam_ref[...] * (-1.0 / (1.0 + jnp.exp(lam_v)))

        def conv_step(c, carry):
            t0 = pl.multiple_of(c * T, T)
            rows = pl.ds(t0, T)
            g0 = dxc_v[rows, :]
            gm1 = _shift_down(g0, _row_or_zero(dxc_v, t0 - 1, S))
            gm2 = _shift_down(gm1, _row_or_zero(dxc_v, t0 - 2, S))
            gp1 = _shift_up(g0, _row_or_zero(dxc_v, t0 + T, S))
            dxb_ref[rows, :] = (cw[0:1] * gp1 + cw[1:2] * g0 + cw[2:3] * gm1 + cw[3:4] * gm2).astype(BF16)
            xm1, x0, xp1, xp2 = _conv_taps(xb_ref, t0, S)
            for tap, xs in enumerate((xm1, x0, xp1, xp2)):
                dcw_ref[tap:tap + 1] += jnp.sum(g0 * xs, axis=0, keepdims=True)
            dcb_ref[...] += jnp.sum(g0, axis=0, keepdims=True)
            return carry

        lax.fori_loop(0, nc, conv_step, 0)

    col, small, wblk = _lru_specs(S)
    colo = lambda: pl.BlockSpec((S, LANES), lambda n: (0, n), pipeline_mode=pl.Buffered(1))
    return pl.pallas_call(
        body, grid=(LRU_BLOCKS,),
        in_specs=[col(0), col(LRU_BLOCKS), col(0), col(0), col(0), small(4), small(1), wblk, small(2), wblk, small(2), small(2)],
        out_specs=[colo(), colo(), small(4), small(1), wblk, small(2), wblk, small(2), small(2)],
        out_shape=[jax.ShapeDtypeStruct((S, D_MODEL), BF16), jax.ShapeDtypeStruct((S, D_MODEL), BF16),
                   jax.ShapeDtypeStruct((4, D_MODEL), F32), jax.ShapeDtypeStruct((1, D_MODEL), F32),
                   jax.ShapeDtypeStruct((2, LRU_BLOCKS, LANES, LANES), F32), jax.ShapeDtypeStruct((2, D_MODEL), F32),
                   jax.ShapeDtypeStruct((2, LRU_BLOCKS, LANES, LANES), F32), jax.ShapeDtypeStruct((2, D_MODEL), F32),
                   jax.ShapeDtypeStruct((2, D_MODEL), F32)],
        scratch_shapes=[pltpu.VMEM((S, LANES), F32), pltpu.VMEM((S, LANES), F32), pltpu.VMEM((S, LANES), F32)],
        name=name, compiler_params=_cp(("parallel",)))(p, p, hf, hr, dcat, conv_w, conv_b, wa, ba, wx, bx, lam)


PK_UP, PK_DOWN, PK_KV, PK_OUT, PK_IN = 0, 1024, 2048, 2304, 2688
PK_ROWS = {0: PK_IN, 1: PK_IN + 640}
SMALL_G_ROWS = 192
PKF_KV, PKF_SMALL = 512, 768
PKF_ROWS = PKF_SMALL + SMALL_G_ROWS


def _mlp_bwd(x, dx, dxb, saved, w_up, w_down, gain, l, rider=None, next_rider=None):
    up, act, h = saved
    pk = _mm_tn(act, dxb, 1, name=f"dw_down{l}", packed=(None, PK_ROWS[l], PK_DOWN), rider=rider)
    pk, carried = pk if rider is not None else (pk, None)
    dup = _mm_nt(dxb, w_down, up=up, name=f"d_up{l}")
    rider_up = next_rider(carried) if next_rider is not None else None
    pk = _mm_tn(h, dup, N_CHIPS, name=f"dw_up{l}", packed=(pk, PK_ROWS[l], PK_UP), rider=rider_up)
    pk, carried = pk if rider_up is not None else (pk, carried)
    dx, dxb, g_gain = _mm_nt(dup, w_up, norm_x=x, norm_g=gain, dres=dx, name=f"d_mlp_in{l}")
    return dx, dxb, pk, g_gain, carried


def _reduce_first(pk, place, tag, recv):
    return _sum_halves(pk, recv, place, name=f"sum_halves{tag}", tr=pk.shape[1] // 4)


def _sum_parts(parts, place, tag):
    return _sum_chips(parts, place, name=f"sum_chips{tag}", tr=parts.shape[1] // 2)


def _local_step(x, mem, positions, target, W, pending=None, place=None):
    cos_t, sin_t = _rope_tables(positions)
    sinks = W["attn_sinks"].reshape(ATTN_HEADS)
    G = {}

    def hosting(late, fn, *args, **kw):
        if pending is None:
            return fn(*args, **kw)
        *res, buf = fn(*args, gather=pending[late], **kw)
        if late.startswith("w_down"):
            W.setdefault("w_down", [None] * DEPTH)[int(late[-1])] = _ready(late, buf)
        elif late == "w_up":
            W["w_up"], W["w_mem_kv"] = _ready(late, buf)
        else:
            W[late] = _ready(late, buf)
        return res if len(res) > 1 else res[0]

    p0, h0 = hosting("w_out", _mm_nn, x, W["attn_w_in"], norm_g=W["mix_norm"][0], name="attn_in")
    q, kd, vd, va = hosting("w_down0", _qk_prep, p0, cos_t, sin_t)
    ao, lse = hosting("w_up", _attn_fwd, q, kd, va, sinks)
    kv0, memn = _mm_nn(mem, W["w_mem_kv"][0], norm_g=W["mem_norm"], out_dtype=BF16, name="mem_kv0", tm=256)
    kv1 = _mm_nn(memn, W["w_mem_kv"][1], out_dtype=BF16, name="mem_kv1", tm=256)
    cat0 = _memattn_fwd(p0, Q_W // MEM_W + 1, kv0, ao, name="memattn_fwd0")
    x1 = hosting("lru_w_in", _mm_nn, cat0, W["w_out"][0], resid=x, name="mix_out0")
    up0, act0, h1 = hosting("w_down1", _mm_nn, x1, W["w_up"][0], norm_g=W["mlp_norm"][0], relu2=True, name="mlp_up0")
    x2, mlp0 = _mm_nn(act0, W["w_down"][0], resid=x1, name="mlp_down0"), (up0, act0, h1)
    p1, h2 = _mm_nn(x2, W["lru_w_in"], norm_g=W["mix_norm"][1], name="lru_in")
    lru_w = (W["lru_conv_w"], W["lru_conv_b"], W["lru_wa"], W["lru_ba"], W["lru_wx"], W["lru_bx"], W["lru_lambda"])
    y, hf, hr = _lru_fwd(p1, *lru_w)
    cat1 = _memattn_fwd(p1, 2 * D_MODEL // MEM_W, kv1, y, name="memattn_fwd1")
    x3 = _mm_nn(cat1, W["w_out"][1], resid=x2, name="mix_out1")
    mlp1 = _mm_nn(x3, W["w_up"][1], norm_g=W["mlp_norm"][1], relu2=True, name="mlp_up1")
    loss, dx, dxb, G["final_norm"] = _final(mlp1[1], W["w_down"][1], x3, W["final_norm"], target)

    def put(pk, off, g):
        return pk.at[:, off:off + g.size // (N_CHIPS * ROW)].set(g.reshape(N_CHIPS, -1, ROW))

    dx, dxb, pk1, gm1, _ = _mlp_bwd(x3, dx, dxb, mlp1, W["w_up"][1], W["w_down"][1], W["mlp_norm"][1], 1)
    pk1 = _mm_tn(cat1, dxb, 1, name="dw_out1", tk=384, packed=(pk1, PK_ROWS[1], PK_OUT))
    dcat1 = _mm_nt(dxb, W["w_out"][1], name="d_mix1")
    dmq1, dkv1 = _memattn_bwd(p1, 2 * D_MODEL // MEM_W, kv1, dcat1, name="memattn_bwd1")
    dkv1b = dkv1.astype(BF16)
    pk1 = _mm_tn(memn, dkv1b, 1, name="dw_kv1", tm=256, tk=256, packed=(pk1, PK_ROWS[1], PK_KV))
    (dxb1, dgate, G["lru_conv_w"], G["lru_conv_b"], G["lru_wa"], G["lru_ba"], G["lru_wx"], G["lru_bx"],
     G["lru_lambda"]) = _lru_bwd(p1, hf, hr, dcat1, *lru_w)
    dp1 = jnp.concatenate([dxb1, dgate, dmq1], axis=1)
    pk1 = put(pk1, PK_IN, _mm_tn(h2, dp1, N_CHIPS, name="dw_lru_in"))
    dx, dxb, gx1 = _mm_nt(dp1, W["lru_w_in"], norm_x=x2, norm_g=W["mix_norm"][1], dres=dx, name="d_lru_in")
    dist = place is not None
    h1_rows = PK_ROWS[1] // 4
    kept = {}

    def first_half(recv1):
        kept["halves1"], landing1 = _reduce_first(pk1, place, "1", recv1)
        return _exchange_rider((kept["halves1"], landing1, 0, h1_rows))

    dx, dxb, pk0, gm0, landing1 = _mlp_bwd(x1, dx, dxb, mlp0, W["w_up"][0], W["w_down"][0], W["mlp_norm"][0], 0,
                                           rider=_sib_exchange_rider(pk1) if dist else None,
                                           next_rider=first_half if dist else None)
    pk0 = _mm_tn(cat0, dxb, 1, name="dw_out0", tk=384, packed=(pk0, PK_ROWS[0], PK_OUT))
    pk0 = pk0.at[:, PK_KV:PK_OUT].set(0.0)
    dcat0 = _mm_nt(dxb, W["w_out"][0], name="d_mix0")
    dmq0, dkv0, *recv0 = _memattn_bwd(p0, Q_W // MEM_W + 1, kv0, dcat0, name="memattn_bwd0",
                                      rider=_sib_exchange_rider(pk0) if dist else None)
    dkv0b = dkv0.astype(BF16)
    g_kv0 = _mm_tn(memn, dkv0b, 1, name="dw_kv0", tm=256, tk=256)
    rider = None
    if dist:
        halves0, landing0 = _reduce_first(pk0, place, "0", recv0[0])
        rider = _exchange_rider((kept["halves1"], landing1, h1_rows, h1_rows), (halves0, landing0, 0, halves0.shape[1]))
    dq, dk, dv, dsink, *parts = _attn_bwd(q, kd, vd, cat0, lse, sinks, dcat0, rider=rider)
    dp0 = _qk_prep_bwd(dq, dk, dv, dmq0, cos_t, sin_t)
    g_in = _mm_tn(h0, dp0, N_CHIPS, name="dw_attn_in",
                  rider=_sib_allgather_rider(_sum_parts(parts[0], place, "1"), _sum_parts(parts[1], place, "0")) if dist else None)
    if dist:
        g_in, pk1, pk0 = g_in
    dx, _, gx0 = _mm_nt(dp0, W["attn_w_in"], norm_x=x, norm_g=W["mix_norm"][0], dres=dx, name="d_attn_in")

    w_kv_both = jnp.concatenate([W["w_mem_kv"][0], W["w_mem_kv"][1]], axis=0)
    _, _, G["mem_norm"] = _mm_nt(jnp.concatenate([dkv0b, dkv1b], axis=1), w_kv_both, norm_x=mem, norm_g=W["mem_norm"],
                                 name="d_mem", tm=256)

    G["mix_norm"] = jnp.concatenate([gx0, gx1], axis=0)
    G["mlp_norm"] = jnp.concatenate([gm0, gm1], axis=0)
    G["attn_sinks"] = dsink[0:1, 0:ATTN_HEADS]
    small = _flat_pad(_small_grad_list(G), N_CHIPS * SMALL_G_ROWS * ROW).reshape(N_CHIPS, SMALL_G_ROWS, ROW)
    pkf = jnp.concatenate([g_in.reshape(N_CHIPS, PKF_KV, ROW), g_kv0.reshape(N_CHIPS, PKF_SMALL - PKF_KV, ROW), small], axis=1)
    return loss[0, 0], dx, G, pkf, pk0, pk1


def _comm_call(body, out_shape, n_sems, name, *args, alias=None):
    return pl.pallas_call(
        body, out_shape=out_shape, in_specs=[HBM] * len(args), out_specs=HBM,
        scratch_shapes=[pltpu.SemaphoreType.DMA((n_sems,)), pltpu.SemaphoreType.DMA((n_sems,))],
        input_output_aliases=alias or {}, name=name)(*args)


def _place_slot(shard, slot, n_slots, *, name, tr):
    R, C = shard.shape

    def body(s_ref, a_ref, o_ref):
        o_ref[0] = a_ref[...]

    return pl.pallas_call(
        body,
        grid_spec=pltpu.PrefetchScalarGridSpec(
            num_scalar_prefetch=1, grid=(R // tr,), in_specs=[pl.BlockSpec((tr, C), lambda i, s_ref: (i, 0))],
            out_specs=pl.BlockSpec((1, tr, C), lambda i, s_ref: (s_ref[0], i, 0))),
        out_shape=jax.ShapeDtypeStruct((n_slots, R, C), shard.dtype), name=name,
        compiler_params=_cp(("parallel",)))(slot, shard)


def _allgather_chips(buf, *, name):
    def body(b_ref, o_ref, send_sems, recv_sems):
        _gather_start(o_ref, send_sems, recv_sems)
        _gather_finish(o_ref, send_sems, recv_sems)

    return _comm_call(body, jax.ShapeDtypeStruct(buf.shape, buf.dtype), GATHER_SEMS, name, buf, alias={0: 0})


def _sum_halves(g, recv, place, *, name="sum_halves", tr=480):
    _, R, C = g.shape
    half = R // 2
    nblk = half // tr

    def body(pl_ref, g_ref, r_ref, o_ref, own_ref):
        v = (g_ref[...] + r_ref[...]).astype(BF16)
        o_ref[...] = v

        @pl.when(pl.program_id(1) == pl_ref[1])
        def _():
            own_ref[...] = v

    blk = pl.BlockSpec((1, tr, C), lambda i, s, p: (s, i, 0))
    return pl.pallas_call(
        body,
        grid_spec=pltpu.PrefetchScalarGridSpec(
            num_scalar_prefetch=1, grid=(nblk, N_CHIPS),
            in_specs=[pl.BlockSpec((1, tr, C), lambda i, s, p: (s, p[0] * nblk + i, 0)), blk],
            out_specs=[blk, pl.BlockSpec((1, tr, C), lambda i, s, p: (p[1], i, 0))]),
        out_shape=[jax.ShapeDtypeStruct((N_CHIPS, half, C), BF16)] * 2, name=name,
        compiler_params=_cp(("parallel", "arbitrary")))(place, g, recv)


def _sum_chips(parts, place, *, name="sum_chips", tr=480):
    _, R, C = parts.shape
    nblk = R // tr

    def body(pl_ref, p_ref, o_ref):
        acc = p_ref[0].astype(F32) + p_ref[1].astype(F32)
        o_ref[...] = (acc + p_ref[2].astype(F32)) + p_ref[3].astype(F32)

    return pl.pallas_call(
        body,
        grid_spec=pltpu.PrefetchScalarGridSpec(
            num_scalar_prefetch=1, grid=(nblk,), in_specs=[pl.BlockSpec((N_CHIPS, tr, C), lambda i, p: (0, i, 0))],
            out_specs=pl.BlockSpec((tr, C), lambda i, p: (p[0] * nblk + i, 0))),
        out_shape=jax.ShapeDtypeStruct((2 * R, C), F32), name=name, compiler_params=_cp(("parallel",)))(place, parts)


def _adamw(w, g, m, v, *, name, tr=128, rider=None):
    R, C = w.shape
    bc1 = 1.0 - ADAM_B1 ** ADAM_STEP
    bc2 = 1.0 - ADAM_B2 ** ADAM_STEP
    host = _Hosted(rider, 4, 3)

    def body(*refs):
        ins, outs, _, rrefs = host.split(refs, 4, 3)
        host.run(rrefs, pl.program_id(0), R // tr, lambda: inner(*ins, *outs))

    def inner(w_ref, g_ref, m_ref, v_ref, d_ref, nm_ref, nv_ref):
        gv = g_ref[...]
        nm = ADAM_B1 * m_ref[...] + (1.0 - ADAM_B1) * gv
        nv = ADAM_B2 * v_ref[...] + (1.0 - ADAM_B2) * (gv * gv)
        d_ref[...] = -ADAM_LR * ((nm / bc1) / (_sqrt(nv / bc2) + ADAM_EPS) + ADAM_WD * w_ref[...])
        nm_ref[...] = nm
        nv_ref[...] = nv

    blk = pl.BlockSpec((tr, C), lambda i: (i, 0))
    return pl.pallas_call(
        body, grid=(R // tr,), in_specs=[blk] * 4 + host.in_specs, out_specs=[blk] * 3 + host.out_specs,
        out_shape=[jax.ShapeDtypeStruct((R, C), F32)] * 3 + host.out_shape, scratch_shapes=host.scratch,
        input_output_aliases=host.alias, name=name,
        compiler_params=_cp(("arbitrary",) if host.on else ("parallel",)))(w, g, m, v, *host.args)


ROW = 1024
BIG = ("w_mem_kv", "w_out", "w_up", "w_down", "attn_w_in", "lru_w_in")
SMALL_SHARDED = ("lru_conv_w", "lru_conv_b", "lru_ba", "lru_bx", "lru_lambda")
REPLICATED = ("mix_norm", "mlp_norm", "mem_norm", "final_norm", "attn_sinks", "lru_wa", "lru_wx")
SMALL = REPLICATED + SMALL_SHARDED
WEIGHTS = ("mix_norm", "mlp_norm", "mem_norm", "final_norm", "w_mem_kv", "w_out", "w_up", "w_down", "attn_w_in",
           "attn_sinks", "lru_w_in", "lru_conv_w", "lru_conv_b", "lru_wa", "lru_ba", "lru_wx", "lru_bx", "lru_lambda")
SMALL_W_ROWS = 32
ADAM_SMALL_ROWS = 640


def _rows(a):
    return a.reshape(-1, ROW)


def _flat_pad(parts, total):
    flat = jnp.concatenate([p.reshape(-1) for p in parts])
    return jnp.pad(flat, (0, total - flat.shape[0]))


def _pad_rows(a):
    flat = a.reshape(-1)
    n = -(-flat.shape[0] // ROW) * ROW
    return jnp.pad(flat, (0, n - flat.shape[0])).reshape(-1, ROW)


LATE = ("w_out", "w_down0", "w_up", "lru_w_in", "w_down1")


def _ready(name, full):
    if name == "w_out":
        wo = full.reshape(N_CHIPS, DEPTH, -1, D_MODEL)
        return [wo[:, l].reshape(1, MIX_OUT_W, D_MODEL) for l in range(DEPTH)]
    if name == "w_up":
        n_up = DEPTH * D_MODEL
        wu = full[:, :n_up].reshape(N_CHIPS, DEPTH, D_MODEL, D_FF // N_CHIPS)
        kv = full[:, n_up:].reshape(N_CHIPS, DEPTH, -1, D_MODEL)
        return [wu[:, l] for l in range(DEPTH)], [kv[:, l].reshape(1, D_MODEL, D_MODEL) for l in range(DEPTH)]
    if name == "lru_w_in":
        return full.reshape(N_CHIPS, D_MODEL, LRU_IN_W // N_CHIPS)
    return full.reshape(1, D_FF, D_MODEL)


def _gather_weights(P, chip1):
    bf = lambda a: _rows(a.astype(BF16))
    small = _flat_pad([P[n] for n in SMALL_SHARDED], SMALL_W_ROWS * ROW // 2)
    small_bits = lax.bitcast_convert_type(small, BF16).reshape(SMALL_W_ROWS, ROW)
    early = jnp.concatenate([bf(P["attn_w_in"]), small_bits], axis=0)
    n_in = P["attn_w_in"].size // ROW
    placed = _place_slot(early, chip1, N_CHIPS, name="place_weights", tr=early.shape[0] // 2)
    full = _allgather_chips(placed, name="allgather_weights")
    late = {"w_out": bf(P["w_out"]), "lru_w_in": bf(P["lru_w_in"]),
            "w_up": jnp.concatenate([bf(P["w_up"]), bf(P["w_mem_kv"])], axis=0),
            "w_down0": bf(P["w_down"][0]), "w_down1": bf(P["w_down"][1])}
    pending = {n: _place_slot(late[n], chip1, N_CHIPS, name=f"place_{n}", tr=late[n].shape[0] // 2) for n in LATE}
    W = {n: P[n] for n in REPLICATED}
    W["attn_w_in"] = full[:, :n_in].reshape(N_CHIPS, D_MODEL, ATTN_IN_W // N_CHIPS)
    sm = lax.bitcast_convert_type(full[:, n_in:].reshape(N_CHIPS, -1, 2), F32)
    o = 0
    for n in SMALL_SHARDED:
        shp = P[n].shape[1:]
        cnt = math.prod(shp)
        piece = sm[:, o:o + cnt].reshape((N_CHIPS,) + shp)
        piece = jnp.moveaxis(piece, 0, -2)
        W[n] = piece.reshape(shp[:-1] + (N_CHIPS * shp[-1],)).reshape(-1, D_MODEL)
        o += cnt
    W["lru_wa"] = P["lru_wa"][0].astype(BF16)
    W["lru_wx"] = P["lru_wx"][0].astype(BF16)
    return W, pending


def _small_grad_list(G):
    return [G["mix_norm"], G["mlp_norm"], G["mem_norm"], G["final_norm"], jnp.pad(G["attn_sinks"].reshape(-1), (0, ROW - ATTN_HEADS)),
            G["lru_wa"], G["lru_wx"], G["lru_conv_w"], G["lru_conv_b"], G["lru_ba"], G["lru_bx"], G["lru_lambda"]]


SMALL_G_SIZES = (2 * D_MODEL, 2 * D_MODEL, D_MODEL, D_MODEL, ROW, 2 * 8 * 128 * 128, 2 * 8 * 128 * 128,
                 4 * D_MODEL, D_MODEL, 2 * D_MODEL, 2 * D_MODEL, 2 * D_MODEL)


def _finish_grads(pkf, full0, full1, place, chip1, adamw):
    both = lambda off, r: jnp.concatenate([full0[off:off + r], full1[off:off + r]], axis=0)
    recvf = adamw("w_up", both(PK_UP, 1024), _sib_exchange_rider(pkf))
    halvesf, landingf = _reduce_first(pkf, place, "f", recvf)
    partsf = adamw("w_down", both(PK_DOWN, 1024), _exchange_rider((halvesf, landingf, 0, halvesf.shape[1])))
    fullf = adamw("w_out", both(PK_OUT, 384), _sib_allgather_rider(_sum_parts(partsf, place, "f")))
    small_placed = _place_slot(fullf[PKF_SMALL:], chip1, N_CHIPS, name="place_small_grads", tr=SMALL_G_ROWS)
    small_all = adamw("lru_w_in", full1[PK_IN:PK_IN + 640], _gather_rider(small_placed))
    adamw("w_mem_kv", jnp.concatenate([fullf[PKF_KV:PKF_SMALL], full1[PK_KV:PK_OUT]], axis=0), None)
    adamw("attn_w_in", fullf[:PKF_KV], None)
    flat = small_all.reshape(-1)
    small = {}
    o = 0
    names = ("mix_norm", "mlp_norm", "mem_norm", "final_norm", "attn_sinks", "lru_wa", "lru_wx",
             "lru_conv_w", "lru_conv_b", "lru_ba", "lru_bx", "lru_lambda")
    for n, cnt in zip(names, SMALL_G_SIZES):
        small[n] = flat[o:o + cnt]
        o += cnt
    return small


def kernel(x, mem, positions, mix_norm, mlp_norm, mem_norm, final_norm, w_mem_kv, w_out, w_up, w_down, attn_w_in, attn_sinks, lru_w_in, lru_conv_w, lru_conv_b, lru_wa, lru_ba, lru_wx, lru_bx, lru_lambda, loss_target, m_mix_norm, m_mlp_norm, m_mem_norm, m_final_norm, m_w_mem_kv, m_w_out, m_w_up, m_w_down, m_attn_w_in, m_attn_sinks, m_lru_w_in, m_lru_conv_w, m_lru_conv_b, m_lru_wa, m_lru_ba, m_lru_wx, m_lru_bx, m_lru_lambda, v_mix_norm, v_mlp_norm, v_mem_norm, v_final_norm, v_w_mem_kv, v_w_out, v_w_up, v_w_down, v_attn_w_in, v_attn_sinks, v_lru_w_in, v_lru_conv_w, v_lru_conv_b, v_lru_wa, v_lru_ba, v_lru_wx, v_lru_bx, v_lru_lambda):
    P = dict(mix_norm=mix_norm, mlp_norm=mlp_norm, mem_norm=mem_norm, final_norm=final_norm, w_mem_kv=w_mem_kv, w_out=w_out,
             w_up=w_up, w_down=w_down, attn_w_in=attn_w_in, attn_sinks=attn_sinks, lru_w_in=lru_w_in, lru_conv_w=lru_conv_w,
             lru_conv_b=lru_conv_b, lru_wa=lru_wa, lru_ba=lru_ba, lru_wx=lru_wx, lru_bx=lru_bx, lru_lambda=lru_lambda)
    M1 = dict(mix_norm=m_mix_norm, mlp_norm=m_mlp_norm, mem_norm=m_mem_norm, final_norm=m_final_norm, w_mem_kv=m_w_mem_kv,
              w_out=m_w_out, w_up=m_w_up, w_down=m_w_down, attn_w_in=m_attn_w_in, attn_sinks=m_attn_sinks, lru_w_in=m_lru_w_in,
              lru_conv_w=m_lru_conv_w, lru_conv_b=m_lru_conv_b, lru_wa=m_lru_wa, lru_ba=m_lru_ba, lru_wx=m_lru_wx,
              lru_bx=m_lru_bx, lru_lambda=m_lru_lambda)
    V2 = dict(mix_norm=v_mix_norm, mlp_norm=v_mlp_norm, mem_norm=v_mem_norm, final_norm=v_final_norm, w_mem_kv=v_w_mem_kv,
              w_out=v_w_out, w_up=v_w_up, w_down=v_w_down, attn_w_in=v_attn_w_in, attn_sinks=v_attn_sinks, lru_w_in=v_lru_w_in,
              lru_conv_w=v_lru_conv_w, lru_conv_b=v_lru_conv_b, lru_wa=v_lru_wa, lru_ba=v_lru_ba, lru_wx=v_lru_wx,
              lru_bx=v_lru_bx, lru_lambda=v_lru_lambda)
    chip = 2 * lax.axis_index("x") + lax.axis_index("y")
    chip1 = chip.astype(jnp.int32).reshape(1)
    place = jnp.stack([lax.axis_index("c").astype(jnp.int32), chip.astype(jnp.int32)])

    W, pending = _gather_weights(P, chip1)
    loss, dx, _, pkf, full0, full1 = _local_step(x[0], mem[0], positions[0], loss_target[0], W, pending, place)
    loss = lax.psum(loss, ("x", "y", "c"))
    grads, deltas, new_m, new_v = {}, {}, {}, {}

    def adamw_big(n, g, rider):
        d, nm, nv, *carried = _adamw(_rows(P[n]), g, _rows(M1[n]), _rows(V2[n]), name=f"adamw_{n}", rider=rider)
        grads[n], deltas[n], new_m[n], new_v[n] = (t.reshape(P[n].shape) for t in (g, d, nm, nv))
        return carried[0] if carried else None

    small = _finish_grads(pkf, full0, full1, place, chip1, adamw_big)

    for n in SMALL:
        g = small[n]
        if n in SMALL_SHARDED:
            shard = P[n].shape[-1]
            g = lax.dynamic_slice_in_dim(g.reshape(-1, N_CHIPS * shard), chip * shard, shard, axis=1)
        elif n == "attn_sinks":
            g = g[:ATTN_HEADS]
        grads[n] = g.reshape(P[n].shape)
    packs = []
    for src in (P, grads, M1, V2):
        a = jnp.concatenate([_pad_rows(src[n]) for n in SMALL], axis=0)
        packs.append(jnp.pad(a, ((0, ADAM_SMALL_ROWS - a.shape[0]), (0, 0))))
    d_s, nm_s, nv_s = _adamw(*packs, name="adamw_small")
    o = 0
    for n in SMALL:
        cnt = math.prod(P[n].shape)
        r = -(-cnt // ROW)
        for dst, src in ((deltas, d_s), (new_m, nm_s), (new_v, nv_s)):
            dst[n] = src[o:o + r].reshape(-1)[:cnt].reshape(P[n].shape)
        o += r

    return (loss, dx[None], *[grads[n] for n in WEIGHTS], *[deltas[n] for n in WEIGHTS],
            *[new_m[n] for n in WEIGHTS], *[new_v[n] for n in WEIGHTS])
```

```python
import math

import jax
import jax.numpy as jnp
from jax import lax
from jax.experimental import pallas as pl
from jax.experimental.pallas import tpu as pltpu

F32 = jnp.float32
BF16 = jnp.bfloat16
MESH = pl.DeviceIdType.MESH

D_MODEL = 1024
DEPTH = 2
EPS = 1e-6
ATTN_HEADS = 16
ATTN_KV_HEADS = 4
HEAD_DIM = 64
WINDOW = 128
BLOCK = 128
ROPE_THETA = 500000.0
ROPE_DIM = 16
Q_W = 1024
KV_W = 256
MEM_LEN = 256
MEM_HEADS = 4
MEM_HEAD_DIM = 128
MEM_W = 512
LRU_BLOCKS = 8
LRU_C = 8.0
ATTN_IN_W = 2048
LRU_IN_W = 2560
MIX_OUT_W = 1536
D_FF = 4096
NEG = -1e30
N_CHIPS = 4

ADAM_LR = 0.001
ADAM_B1 = 0.9
ADAM_B2 = 0.999
ADAM_EPS = 1e-08
ADAM_WD = 0.01
ADAM_STEP = 10

LANES = 128
SCAN_ROWS = 512
VMEM_LIMIT = 56 * 1024 * 1024

NT = (((1,), (1,)), ((), ()))
TN = (((0,), (0,)), ((), ()))


def _cp(sem=None):
    return pltpu.CompilerParams(dimension_semantics=sem, vmem_limit_bytes=VMEM_LIMIT)


HBM = pl.BlockSpec(memory_space=pl.ANY)
GATHER_SEMS = 6


def _place():
    x, y, c = lax.axis_index("x"), lax.axis_index("y"), lax.axis_index("c")
    chips = [(1 - x, y), (x, 1 - y), (1 - x, 1 - y)]
    return x, y, c, chips


def _remote(src, dst, send_sems, recv_sems, k, to):
    return pltpu.make_async_remote_copy(src_ref=src, dst_ref=dst, send_sem=send_sems.at[k], recv_sem=recv_sems.at[k],
                                        device_id=to, device_id_type=MESH)


def _gather_start(o_ref, send_sems, recv_sems):
    x, y, c, chips = _place()
    half = o_ref.shape[1] // 2
    own = o_ref.at[2 * x + y, pl.ds(pl.multiple_of(c * half, 16), half)]
    for j, (cx, cy) in enumerate(chips):
        _remote(own, own, send_sems, recv_sems, j, (cx, cy, c)).start()


def _gather_forward(o_ref, send_sems, recv_sems):
    x, y, c, chips = _place()
    half = o_ref.shape[1] // 2
    my_rows = pl.ds(pl.multiple_of(c * half, 16), half)
    for j, (cx, cy) in enumerate(chips):
        landed = o_ref.at[2 * cx + cy, my_rows]
        _remote(landed, landed, send_sems, recv_sems, j, (cx, cy, c)).wait_recv()
        _remote(landed, landed, send_sems, recv_sems, 3 + j, (x, y, 1 - c)).start()


def _gather_drain(o_ref, send_sems, recv_sems):
    x, y, c, chips = _place()
    half = o_ref.shape[1] // 2
    my_rows = pl.ds(pl.multiple_of(c * half, 16), half)
    sib_rows = pl.ds(pl.multiple_of((1 - c) * half, 16), half)
    own = o_ref.at[2 * x + y, my_rows]
    for j, (cx, cy) in enumerate(chips):
        got = o_ref.at[2 * cx + cy, sib_rows]
        _remote(got, got, send_sems, recv_sems, 3 + j, (x, y, 1 - c)).wait_recv()
    for j, (cx, cy) in enumerate(chips):
        _remote(own, own, send_sems, recv_sems, j, (cx, cy, c)).wait_send()
        landed = o_ref.at[2 * cx + cy, my_rows]
        _remote(landed, landed, send_sems, recv_sems, 3 + j, (x, y, 1 - c)).wait_send()


def _gather_finish(o_ref, send_sems, recv_sems):
    _gather_forward(o_ref, send_sems, recv_sems)
    _gather_drain(o_ref, send_sems, recv_sems)


def _exchange_start(h_ref, o_ref, send_sems, recv_sems, rows=None, base=0):
    x, y, c, chips = _place()
    rows = pl.ds(0, h_ref.shape[1]) if rows is None else rows
    for j, (cx, cy) in enumerate(chips):
        _remote(h_ref.at[2 * cx + cy, rows], o_ref.at[2 * x + y, rows], send_sems, recv_sems, base + j, (cx, cy, c)).start()


def _exchange_finish(h_ref, o_ref, send_sems, recv_sems, rows=None, base=0):
    x, y, c, chips = _place()
    rows = pl.ds(0, h_ref.shape[1]) if rows is None else rows
    for j, (cx, cy) in enumerate(chips):
        got = o_ref.at[2 * cx + cy, rows]
        _remote(got, got, send_sems, recv_sems, base + j, (cx, cy, c)).wait_recv()
    for j, (cx, cy) in enumerate(chips):
        _remote(h_ref.at[2 * cx + cy, rows], o_ref.at[2 * x + y, rows], send_sems, recv_sems, base + j, (cx, cy, c)).wait_send()


def _sib_exchange_copies(g_ref, o_ref, send_sems, recv_sems):
    x, y, c, _ = _place()
    half = g_ref.shape[1] // 2
    other = pl.ds(pl.multiple_of((1 - c) * half, 8), half)
    return [_remote(g_ref.at[s, other], o_ref.at[s], send_sems, recv_sems, s, (x, y, 1 - c)) for s in range(N_CHIPS)]


def _sib_exchange_start(*refs):
    for cp in _sib_exchange_copies(*refs):
        cp.start()


def _sib_exchange_finish(*refs):
    for cp in _sib_exchange_copies(*refs):
        cp.wait()


def _sib_allgather_start(*refs):
    *o_refs, send_sems, recv_sems = refs
    x, y, c, _ = _place()
    for i, o_ref in enumerate(o_refs):
        half = o_ref.shape[0] // 2
        mine = o_ref.at[pl.ds(pl.multiple_of(c * half, 8), half)]
        _remote(mine, mine, send_sems, recv_sems, i, (x, y, 1 - c)).start()


def _sib_allgather_finish(*refs):
    *o_refs, send_sems, recv_sems = refs
    x, y, c, _ = _place()
    for i, o_ref in enumerate(o_refs):
        half = o_ref.shape[0] // 2
        mine = o_ref.at[pl.ds(pl.multiple_of(c * half, 8), half)]
        got = o_ref.at[pl.ds(pl.multiple_of((1 - c) * half, 8), half)]
        _remote(got, got, send_sems, recv_sems, i, (x, y, 1 - c)).wait_recv()
        _remote(mine, mine, send_sems, recv_sems, i, (x, y, 1 - c)).wait_send()


class _Rider:
    def __init__(self, args, start, finish, inplace=1, mid=None):
        self.args, self.start, self.finish, self.inplace, self.mid = list(args), start, finish, inplace, mid


def _gather_rider(buf):
    return None if buf is None else _Rider([buf], _gather_start, _gather_finish, mid=(_gather_forward, _gather_drain))


def _exchange_rider(*parts):
    n = len(parts)
    assert 3 * n <= GATHER_SEMS

    def run(fn):
        def go(*refs):
            sems = refs[2 * n:]
            for i, (_, _, r0, nr) in enumerate(parts):
                fn(refs[i], refs[n + i], *sems, rows=pl.ds(r0, nr), base=3 * i)
        return go

    return _Rider([p[0] for p in parts] + [p[1] for p in parts], run(_exchange_start), run(_exchange_finish), inplace=n)


def _sib_exchange_rider(g):
    landing = lax.empty((N_CHIPS, g.shape[1] // 2, g.shape[2]), g.dtype)
    return _Rider([g, landing], _sib_exchange_start, _sib_exchange_finish)


def _sib_allgather_rider(*fulls):
    return _Rider(fulls, _sib_allgather_start, _sib_allgather_finish, inplace=len(fulls))


class _Hosted:
    def __init__(self, rider, n_in, n_out):
        self.rider = rider
        self.on = rider is not None
        self.args = rider.args if self.on else []
        k = len(self.args)
        p = self.p = rider.inplace if self.on else 0
        self.alias = {n_in + k - p + i: n_out + i for i in range(p)}
        self.in_specs = [HBM] * k
        self.out_specs = [HBM] * p
        self.out_shape = [jax.ShapeDtypeStruct(a.shape, a.dtype) for a in self.args[k - p:]]
        self.scratch = [pltpu.SemaphoreType.DMA((GATHER_SEMS,)), pltpu.SemaphoreType.DMA((GATHER_SEMS,))] if self.on else []

    def split(self, refs, n_in, n_out):
        refs = list(refs)
        if not self.on:
            return refs[:n_in], refs[n_in:n_in + n_out], refs[n_in + n_out:], None
        k, p = len(self.args), self.p
        ins, outs = refs[:n_in], refs[n_in + k:n_in + k + n_out]
        rest = refs[n_in + k + n_out + p:]
        rrefs = refs[n_in:n_in + k - p] + refs[n_in + k + n_out:n_in + k + n_out + p] + [rest[-2], rest[-1]]
        return ins, outs, rest[:-2], rrefs

    def run(self, rrefs, step, n_steps, compute):
        if rrefs is None:
            return compute()

        mid = self.rider.mid
        mid_step = (3 * n_steps) // 4
        two_stage = mid is not None and 0 < mid_step < n_steps - 1

        @pl.when(step == 0)
        def _():
            self.rider.start(*rrefs)

        compute()

        if two_stage:
            @pl.when(step == mid_step)
            def _():
                mid[0](*rrefs)

        @pl.when(step == n_steps - 1)
        def _():
            (mid[1] if two_stage else self.rider.finish)(*rrefs)


def _mm_nn(a, w3, *, name, out_dtype=F32, norm_g=None, resid=None, relu2=False, tm=512, gather=None):
    M, K = a.shape
    ns, _, n = w3.shape
    N = ns * n
    tm = min(tm, M)
    has_norm = norm_g is not None
    has_res = resid is not None
    n_in = 2 + has_norm + has_res
    n_out = (2 if relu2 else 1) + has_norm
    host = _Hosted(_gather_rider(gather), n_in, n_out)

    def body(*refs):
        ins, outs, _, gref = host.split(refs, n_in, n_out)
        a_ref, w_ref = ins[0], ins[1]
        g_ref = ins[2] if has_norm else None
        r_ref = ins[-1] if has_res else None

        def compute():
            if has_norm:
                xv = a_ref[...]
                rs = lax.rsqrt(jnp.mean(xv * xv, axis=-1, keepdims=True) + EPS)
                ab = (xv * rs * g_ref[...]).astype(BF16)
                outs[-1][...] = ab
            else:
                ab = a_ref[...]
            for s in range(ns):
                acc = jnp.dot(ab, w_ref[s], preferred_element_type=F32)
                sl = slice(s * n, (s + 1) * n)
                if relu2:
                    outs[0][:, sl] = acc.astype(BF16)
                    rl = jnp.maximum(acc, 0.0)
                    outs[1][:, sl] = (rl * rl).astype(BF16)
                elif has_res:
                    outs[0][:, sl] = r_ref[:, sl] + acc
                else:
                    outs[0][:, sl] = acc.astype(out_dtype)

        host.run(gref, pl.program_id(0), M // tm, compute)

    row = lambda w: pl.BlockSpec((tm, w), lambda i: (i, 0))
    in_specs = [row(K), pl.BlockSpec((ns, K, n), lambda i: (0, 0, 0))]
    args = [a, w3]
    if has_norm:
        in_specs.append(pl.BlockSpec((1, K), lambda i: (0, 0)))
        args.append(norm_g.reshape(1, K))
    if has_res:
        in_specs.append(row(N))
        args.append(resid)
    if relu2:
        out_shape = [jax.ShapeDtypeStruct((M, N), BF16), jax.ShapeDtypeStruct((M, N), BF16)]
        out_specs = [row(N), row(N)]
    else:
        out_shape = [jax.ShapeDtypeStruct((M, N), F32 if has_res else out_dtype)]
        out_specs = [row(N)]
    if has_norm:
        out_shape.append(jax.ShapeDtypeStruct((M, K), BF16))
        out_specs.append(row(K))
    res = pl.pallas_call(body, grid=(M // tm,), in_specs=in_specs + host.in_specs, out_specs=out_specs + host.out_specs,
                         out_shape=out_shape + host.out_shape, scratch_shapes=host.scratch, input_output_aliases=host.alias,
                         name=name, compiler_params=_cp(("arbitrary",) if host.on else ("parallel",)))(*args, *host.args)
    return res if len(res) > 1 else res[0]


def _mm_nt(g, w3, *, name, out_dtype=BF16, up=None, norm_x=None, norm_g=None, dres=None, tm=512):
    M = g.shape[0]
    ns, K, n = w3.shape
    tm = min(tm, M)
    has_up = up is not None
    has_norm = norm_x is not None
    has_res = dres is not None

    def body(*refs):
        refs = list(refs)
        g_ref, w_ref = refs[0], refs[1]
        pos = 2
        if has_up:
            up_ref = refs[pos]
            pos += 1
        if has_norm:
            x_ref, gn_ref = refs[pos], refs[pos + 1]
            pos += 2
        if has_res:
            r_ref = refs[pos]
            pos += 1
        outs = refs[pos:]
        acc = None
        for s in range(ns):
            part = lax.dot_general(g_ref[:, s * n:(s + 1) * n], w_ref[s], NT, preferred_element_type=F32)
            acc = part if acc is None else acc + part
        if has_up:
            outs[0][...] = (acc * (2.0 * jnp.maximum(up_ref[...].astype(F32), 0.0))).astype(BF16)
        elif has_norm:
            xv = x_ref[...]
            rs = lax.rsqrt(jnp.mean(xv * xv, axis=-1, keepdims=True) + EPS)
            xn = xv * rs
            dxn = acc * gn_ref[...]
            dx = rs * (dxn - xn * jnp.mean(dxn * xn, axis=-1, keepdims=True))
            if has_res:
                dx = dx + r_ref[...]
            outs[0][...] = dx
            outs[1][...] = dx.astype(BF16)

            @pl.when(pl.program_id(0) == 0)
            def _():
                outs[2][...] = jnp.zeros_like(outs[2])

            outs[2][...] += jnp.sum(acc * xn, axis=0, keepdims=True)
        else:
            outs[0][...] = acc.astype(out_dtype)

    row = lambda w: pl.BlockSpec((tm, w), lambda i: (i, 0))
    in_specs = [row(ns * n), pl.BlockSpec((ns, K, n), lambda i: (0, 0, 0))]
    args = [g, w3]
    if has_up:
        in_specs.append(row(K))
        args.append(up)
    if has_norm:
        in_specs += [row(K), pl.BlockSpec((1, K), lambda i: (0, 0))]
        args += [norm_x, norm_g.reshape(1, K)]
    if has_res:
        in_specs.append(row(K))
        args.append(dres)
    if has_norm:
        out_shape = [jax.ShapeDtypeStruct((M, K), F32), jax.ShapeDtypeStruct((M, K), BF16),
                     jax.ShapeDtypeStruct((1, K), F32)]
        out_specs = [row(K), row(K), pl.BlockSpec((1, K), lambda i: (0, 0))]
        sem = ("arbitrary",)
    else:
        out_shape = [jax.ShapeDtypeStruct((M, K), BF16 if has_up else out_dtype)]
        out_specs = [row(K)]
        sem = ("parallel",)
    res = pl.pallas_call(body, grid=(M // tm,), in_specs=in_specs, out_specs=out_specs, out_shape=out_shape,
                         name=name, compiler_params=_cp(sem))(*args)
    return res if len(res) > 1 else res[0]


def _mm_tn(a, g, ns, *, name, tk=512, tm=4096, packed=None, rider=None):
    M, K = a.shape
    n = g.shape[1] // ns
    tm = min(tm, M)
    tk = min(tk, K)
    nk, nm = K // tk, M // tm
    n_in = 3 if (packed is not None and packed[0] is not None) else 2
    host = _Hosted(rider, n_in, 1)

    def body(*refs):
        ins, outs, _, rrefs = host.split(refs, n_in, 1)
        a_ref, g_ref, o_ref = ins[0], ins[1], outs[0]

        def compute():
            @pl.when(pl.program_id(2) == 0)
            def _():
                o_ref[...] = jnp.zeros_like(o_ref)

            o_ref[0] += lax.dot_general(a_ref[...], g_ref[...], TN, preferred_element_type=F32)

        step = (pl.program_id(0) * nk + pl.program_id(1)) * nm + pl.program_id(2)
        host.run(rrefs, step, ns * nk * nm, compute)

    in_specs = [pl.BlockSpec((tm, tk), lambda s, k, m: (m, k)), pl.BlockSpec((tm, n), lambda s, k, m: (m, s))]
    args = [a, g]
    alias = {}
    if packed is None:
        out_spec = pl.BlockSpec((1, tk, n), lambda s, k, m: (s, k, 0))
        out_shape = jax.ShapeDtypeStruct((ns, K, n), F32)
    else:
        buf, rows, off = packed
        per_chip = K * ns // N_CHIPS
        assert n == ROW and per_chip % tk == 0 and off % tk == 0
        if ns == N_CHIPS:
            out_spec = pl.BlockSpec((1, tk, n), lambda s, k, m: (s, off // tk + k, 0))
        else:
            kpc = per_chip // tk
            out_spec = pl.BlockSpec((1, tk, n), lambda s, k, m: (k // kpc, off // tk + k % kpc, 0))
        out_shape = jax.ShapeDtypeStruct((N_CHIPS, rows, ROW), F32)
        if buf is not None:
            in_specs.append(HBM)
            args.append(buf)
            alias = {2: 0}
    sem = ("arbitrary",) * 3 if host.on else ("parallel", "parallel", "arbitrary")
    res = pl.pallas_call(
        body, grid=(ns, nk, nm), in_specs=in_specs + host.in_specs, out_specs=[out_spec] + host.out_specs,
        out_shape=[out_shape] + host.out_shape, scratch_shapes=host.scratch, name=name,
        input_output_aliases={**alias, **host.alias}, compiler_params=_cp(sem))(*args, *host.args)
    return res if host.on else res[0]


def _final(act, w_down, x, gain, target, *, name="mlp_down_final", tr=512):
    S, Dm = x.shape
    tr = min(tr, S)
    Kf = act.shape[1]

    def body(a_ref, w_ref, x_ref, g_ref, t_ref, loss_ref, dx_ref, dxb_ref, dg_ref):
        @pl.when(pl.program_id(0) == 0)
        def _():
            loss_ref[...] = jnp.zeros_like(loss_ref)
            dg_ref[...] = jnp.zeros_like(dg_ref)

        xv = x_ref[...] + jnp.dot(a_ref[...], w_ref[0], preferred_element_type=F32)
        gv = g_ref[...]
        rs = lax.rsqrt(jnp.mean(xv * xv, axis=-1, keepdims=True) + EPS)
        xn = xv * rs
        err = xn * gv - t_ref[...]
        loss_ref[...] += 0.5 * jnp.sum(jnp.mean(err * err, axis=-1, keepdims=True), axis=0, keepdims=True)
        dout = err * (1.0 / Dm)
        dg_ref[...] += jnp.sum(dout * xn, axis=0, keepdims=True)
        dxn = dout * gv
        dx = rs * (dxn - xn * jnp.mean(dxn * xn, axis=-1, keepdims=True))
        dx_ref[...] = dx
        dxb_ref[...] = dx.astype(BF16)

    row = pl.BlockSpec((tr, Dm), lambda i: (i, 0))
    return pl.pallas_call(
        body, grid=(S // tr,),
        in_specs=[pl.BlockSpec((tr, Kf), lambda i: (i, 0)), pl.BlockSpec((1, Kf, Dm), lambda i: (0, 0, 0)), row,
                  pl.BlockSpec((1, Dm), lambda i: (0, 0)), row],
        out_specs=[pl.BlockSpec((1, 1), lambda i: (0, 0)), row, row, pl.BlockSpec((1, Dm), lambda i: (0, 0))],
        out_shape=[jax.ShapeDtypeStruct((1, 1), F32), jax.ShapeDtypeStruct((S, Dm), F32),
                   jax.ShapeDtypeStruct((S, Dm), BF16), jax.ShapeDtypeStruct((1, Dm), F32)],
        name=name, compiler_params=_cp(("arbitrary",)))(act, w_down, x, gain.reshape(1, Dm), target)


def _rope_tables(positions):
    half = ROPE_DIM // 2
    inv_freq = ROPE_THETA ** (-2.0 * jnp.arange(half, dtype=F32) / ROPE_DIM)
    ang = positions.astype(F32)[:, None] * inv_freq
    cos, sin = jnp.cos(ang), jnp.sin(ang)
    S = positions.shape[0]
    ones = jnp.ones((S, HEAD_DIM - ROPE_DIM), F32)
    cos64 = jnp.concatenate([cos, cos, ones], axis=1)
    sin64 = jnp.concatenate([-sin, sin, 0.0 * ones], axis=1)
    return jnp.tile(cos64, (1, 2)), jnp.tile(sin64, (1, 2))


def _rope_partner(t):
    lane = lax.broadcasted_iota(jnp.int32, t.shape, 1)
    low = (lane & (HEAD_DIM - 1)) < (ROPE_DIM // 2)
    return jnp.where(low, pltpu.roll(t, LANES - ROPE_DIM // 2, 1), pltpu.roll(t, ROPE_DIM // 2, 1))


def _qk_prep(p, cos_t, sin_t, *, name="qk_prep", tr=256, gather=None):
    S = p.shape[0]
    tr = min(tr, S)
    scale = HEAD_DIM ** -0.5
    host = _Hosted(_gather_rider(gather), 3, 4)

    def body(*refs):
        ins, outs, _, gref = host.split(refs, 3, 4)
        host.run(gref, pl.program_id(0), S // tr, lambda: inner(*ins, *outs))

    def inner(p_ref, c_ref, s_ref, q_ref, k_ref, v_ref, va_ref):
        cs, sn = c_ref[...], s_ref[...]
        lane = lax.broadcasted_iota(jnp.int32, (tr, LANES), 1)
        lo = lane < HEAD_DIM
        for c in range(Q_W // LANES):
            t = p_ref[:, c * LANES:(c + 1) * LANES]
            q_ref[:, c * LANES:(c + 1) * LANES] = ((t * cs + _rope_partner(t) * sn) * scale).astype(BF16)
        for c in range(KV_W // LANES):
            t = p_ref[:, Q_W + c * LANES:Q_W + (c + 1) * LANES]
            kc = t * cs + _rope_partner(t) * sn
            vc = p_ref[:, Q_W + KV_W + c * LANES:Q_W + KV_W + (c + 1) * LANES]
            for arr, ref in ((kc, k_ref), (vc, v_ref)):
                sw = pltpu.roll(arr, HEAD_DIM, 1)
                ref[:, (2 * c) * LANES:(2 * c + 1) * LANES] = jnp.where(lo, arr, sw).astype(BF16)
                ref[:, (2 * c + 1) * LANES:(2 * c + 2) * LANES] = jnp.where(lo, sw, arr).astype(BF16)
            sw = pltpu.roll(vc, HEAD_DIM, 1)
            for k, aug in enumerate((jnp.where(lo, vc, 1.0), jnp.where(lo, 1.0, sw), jnp.where(lo, sw, 1.0), jnp.where(lo, 1.0, vc))):
                va_ref[:, (4 * c + k) * LANES:(4 * c + k + 1) * LANES] = aug.astype(BF16)

    row = lambda w: pl.BlockSpec((tr, w), lambda i: (i, 0))
    return pl.pallas_call(
        body, grid=(S // tr,), in_specs=[row(ATTN_IN_W), row(LANES), row(LANES)] + host.in_specs,
        out_specs=[row(Q_W), row(2 * KV_W), row(2 * KV_W), row(4 * KV_W)] + host.out_specs,
        out_shape=[jax.ShapeDtypeStruct((S, Q_W), BF16), jax.ShapeDtypeStruct((S, 2 * KV_W), BF16),
                   jax.ShapeDtypeStruct((S, 2 * KV_W), BF16), jax.ShapeDtypeStruct((S, 4 * KV_W), BF16)] + host.out_shape,
        scratch_shapes=host.scratch, input_output_aliases=host.alias,
        name=name, compiler_params=_cp(("arbitrary",) if host.on else ("parallel",)))(p, cos_t, sin_t, *host.args)


def _qk_prep_bwd(dq, dk, dv, dmq, cos_t, sin_t, *, name="qk_prep_bwd", tr=256):
    S = dq.shape[0]
    tr = min(tr, S)

    def body(dq_ref, dk_ref, dv_ref, dmq_ref, c_ref, s_ref, o_ref):
        cs, sn = c_ref[...], s_ref[...]
        for c in range(Q_W // LANES):
            t = dq_ref[:, c * LANES:(c + 1) * LANES]
            o_ref[:, c * LANES:(c + 1) * LANES] = (t * cs - _rope_partner(t) * sn).astype(BF16)
        for c in range(KV_W // LANES):
            t = dk_ref[:, c * LANES:(c + 1) * LANES]
            o_ref[:, Q_W + c * LANES:Q_W + (c + 1) * LANES] = (t * cs - _rope_partner(t) * sn).astype(BF16)
        o_ref[:, Q_W + KV_W:Q_W + 2 * KV_W] = dv_ref[...].astype(BF16)
        o_ref[:, Q_W + 2 * KV_W:] = dmq_ref[...]

    row = lambda w: pl.BlockSpec((tr, w), lambda i: (i, 0))
    return pl.pallas_call(
        body, grid=(S // tr,), in_specs=[row(Q_W), row(KV_W), row(KV_W), row(MEM_W), row(LANES), row(LANES)],
        out_specs=row(ATTN_IN_W), out_shape=jax.ShapeDtypeStruct((S, ATTN_IN_W), BF16),
        name=name, compiler_params=_cp(("parallel",)))(dq, dk, dv, dmq, cos_t, sin_t)


def _band(n, S):
    start = pl.multiple_of(jnp.clip((n - 1) * BLOCK, 0, S - 3 * BLOCK), BLOCK)
    qi = lax.broadcasted_iota(jnp.int32, (BLOCK, 3 * BLOCK), 0) + n * BLOCK
    ki = lax.broadcasted_iota(jnp.int32, (BLOCK, 3 * BLOCK), 1) + start
    return start, jnp.abs(ki - qi) <= WINDOW


def _head_operand(ref, h, lo):
    c = h // 2
    t = ref[:, c * LANES:(c + 1) * LANES].astype(F32)
    return jnp.where(lo if h % 2 == 0 else jnp.logical_not(lo), t, 0.0).astype(BF16)


GROUP = ATTN_HEADS // ATTN_KV_HEADS
EVENS_FIRST = (0, 2, 1, 3)


def _attn_fwd(q, kd, va, sinks, *, name="attn_fwd", gather=None):
    S = q.shape[0]
    host = _Hosted(_gather_rider(gather), 4, 2)

    def body(*refs):
        ins, outs, scr, gref = host.split(refs, 4, 2)
        host.run(gref, pl.program_id(0), S // BLOCK, lambda: inner(*ins, *outs, *scr))

    def inner(sink_ref, q_ref, k_ref, va_ref, o_ref, lse_ref, p_scr):
        n = pl.program_id(0)
        start, mask = _band(n, S)
        lane = lax.broadcasted_iota(jnp.int32, (BLOCK, LANES), 1)
        lo = lane < HEAD_DIM
        rows = pl.ds(start, 3 * BLOCK)
        scores = []
        for g in range(ATTN_KV_HEADS):
            qst = jnp.concatenate([_head_operand(q_ref, GROUP * g + j, lo) for j in EVENS_FIRST], axis=0)
            scores.append(lax.dot_general(qst, k_ref[rows, g * LANES:(g + 1) * LANES], NT, preferred_element_type=F32))
        ms = {}
        for g in range(ATTN_KV_HEADS):
            for pos, j in enumerate(EVENS_FIRST):
                h = GROUP * g + j
                s = jnp.where(mask, scores[g][pos * BLOCK:(pos + 1) * BLOCK], NEG)
                ms[h] = jnp.maximum(jnp.max(s, axis=-1, keepdims=True), sink_ref[h])
                p_scr[(GROUP * g + pos) * BLOCK:(GROUP * g + pos + 1) * BLOCK, :] = jnp.exp(s - ms[h]).astype(BF16)
        pvs = {}
        for g in range(ATTN_KV_HEADS):
            for par in range(2):
                r0 = (GROUP * g + 2 * par) * BLOCK
                pvs[g, par] = jnp.dot(p_scr[r0:r0 + 2 * BLOCK, :], va_ref[rows, (2 * g + par) * LANES:(2 * g + par + 1) * LANES],
                                      preferred_element_type=F32)
        lse_blk = jnp.zeros((BLOCK, LANES), F32)
        for g in range(ATTN_KV_HEADS):
            outs = {}
            for par in range(2):
                for k in range(2):
                    j = EVENS_FIRST[2 * par + k]
                    h = GROUP * g + j
                    pv = pvs[g, par][k * BLOCK:(k + 1) * BLOCK]
                    den = pltpu.roll(pv, HEAD_DIM, 1) + jnp.exp(sink_ref[h] - ms[h])
                    outs[j] = pv * (1.0 / den)
                    l = den[:, par * HEAD_DIM:par * HEAD_DIM + 1]
                    lse_blk = jnp.where(lane == h, ms[h] + jnp.log(l), lse_blk)
            for jj in range(2):
                o_ref[:, (2 * g + jj) * LANES:(2 * g + jj + 1) * LANES] = jnp.where(lo, outs[2 * jj], outs[2 * jj + 1]).astype(BF16)
        lse_ref[...] = lse_blk

    full = lambda w: pl.BlockSpec((S, w), lambda i: (0, 0))
    return pl.pallas_call(
        body, grid=(S // BLOCK,),
        in_specs=[pl.BlockSpec(memory_space=pltpu.SMEM), pl.BlockSpec((BLOCK, Q_W), lambda i: (i, 0)),
                  full(2 * KV_W), full(4 * KV_W)] + host.in_specs,
        out_specs=[pl.BlockSpec((BLOCK, Q_W), lambda i: (i, 0)), pl.BlockSpec((BLOCK, LANES), lambda i: (i, 0))] + host.out_specs,
        out_shape=[jax.ShapeDtypeStruct((S, MIX_OUT_W), BF16), jax.ShapeDtypeStruct((S, LANES), F32)] + host.out_shape,
        scratch_shapes=[pltpu.VMEM((ATTN_HEADS * BLOCK, 3 * BLOCK), BF16)] + host.scratch, input_output_aliases=host.alias,
        name=name, compiler_params=_cp(("arbitrary",) if host.on else ("parallel",)))(sinks, q, kd, va, *host.args)


def _attn_bwd(q, kd, vd, ao, lse, sinks, dcat, *, name="attn_bwd", rider=None):
    S = q.shape[0]
    scale = HEAD_DIM ** -0.5
    host = _Hosted(rider, 7, 4)

    def body(*refs):
        ins, outs, scr, rrefs = host.split(refs, 7, 4)
        host.run(rrefs, pl.program_id(0), S // BLOCK, lambda: inner(*ins, *outs, *scr))

    def inner(sink_ref, q_ref, k_ref, v_ref, ao_ref, lse_ref, do_ref, dq_ref, dk_ref, dv_ref, ds_ref, p_scr, dsb_scr):
        n = pl.program_id(0)

        @pl.when(n == 0)
        def _():
            dk_ref[...] = jnp.zeros_like(dk_ref)
            dv_ref[...] = jnp.zeros_like(dv_ref)
            ds_ref[...] = jnp.zeros_like(ds_ref)

        start, mask = _band(n, S)
        lane = lax.broadcasted_iota(jnp.int32, (BLOCK, LANES), 1)
        lo = lane < HEAD_DIM
        lane3 = lax.broadcasted_iota(jnp.int32, (3 * BLOCK, LANES), 1)
        row8 = lax.broadcasted_iota(jnp.int32, (8, LANES), 0)
        lane8 = lax.broadcasted_iota(jnp.int32, (8, LANES), 1)
        dsink = jnp.zeros((8, LANES), F32)
        lse_blk = lse_ref[...]
        rows = pl.ds(start, 3 * BLOCK)
        lses, deltas = {}, {}
        for c in range(Q_W // LANES):
            prod = do_ref[:, c * LANES:(c + 1) * LANES].astype(F32) * ao_ref[:, c * LANES:(c + 1) * LANES].astype(F32)
            for k in range(2):
                h = 2 * c + k
                deltas[h] = jnp.sum(jnp.where(lo if k == 0 else jnp.logical_not(lo), prod, 0.0), axis=1, keepdims=True)
                lses[h] = jnp.sum(jnp.where(lane == h, lse_blk, 0.0), axis=1, keepdims=True)
                val = -jnp.sum(jnp.exp(sink_ref[h] - lses[h]) * deltas[h], axis=0, keepdims=True)
                dsink = dsink + jnp.where((row8 == 0) & (lane8 == h), val, 0.0)
        stack = lambda ref, g: jnp.concatenate([_head_operand(ref, GROUP * g + j, lo) for j in range(GROUP)], axis=0)
        ss, dps = [], []
        for g in range(ATTN_KV_HEADS):
            ss.append(lax.dot_general(stack(q_ref, g), k_ref[rows, g * LANES:(g + 1) * LANES], NT, preferred_element_type=F32))
            dps.append(lax.dot_general(stack(do_ref, g), v_ref[rows, g * LANES:(g + 1) * LANES], NT, preferred_element_type=F32))
        for g in range(ATTN_KV_HEADS):
            for j in range(GROUP):
                h = GROUP * g + j
                r = slice(j * BLOCK, (j + 1) * BLOCK)
                hr = slice(h * BLOCK, (h + 1) * BLOCK)
                p = jnp.exp(jnp.where(mask, ss[g][r], NEG) - lses[h])
                p_scr[hr, :] = p.astype(BF16)
                dsb_scr[hr, :] = (p * (dps[g][r] - deltas[h])).astype(BF16)
        for g in range(ATTN_KV_HEADS):
            cols = slice((g // 2) * LANES, (g // 2 + 1) * LANES)
            gr = slice(GROUP * g * BLOCK, GROUP * (g + 1) * BLOCK)
            dsg = dsb_scr[gr, :]
            dqs = jnp.dot(dsg, k_ref[rows, g * LANES:(g + 1) * LANES], preferred_element_type=F32) * scale
            for jj in range(2):
                dq_ref[:, (2 * g + jj) * LANES:(2 * g + jj + 1) * LANES] = jnp.where(
                    lo, dqs[(2 * jj) * BLOCK:(2 * jj + 1) * BLOCK], dqs[(2 * jj + 1) * BLOCK:(2 * jj + 2) * BLOCK])
            half = (lane3 < HEAD_DIM) if g % 2 == 0 else (lane3 >= HEAD_DIM)
            dkr = lax.dot_general(dsg, stack(q_ref, g), TN, preferred_element_type=F32)
            dk_ref[rows, cols] += jnp.where(half, dkr + pltpu.roll(dkr, HEAD_DIM, 1), 0.0)
            dvr = lax.dot_general(p_scr[gr, :], stack(do_ref, g), TN, preferred_element_type=F32)
            dv_ref[rows, cols] += jnp.where(half, dvr + pltpu.roll(dvr, HEAD_DIM, 1), 0.0)
        ds_ref[...] += dsink

    full = lambda w: pl.BlockSpec((S, w), lambda i: (0, 0))
    blk = lambda w: pl.BlockSpec((BLOCK, w), lambda i: (i, 0))
    return pl.pallas_call(
        body, grid=(S // BLOCK,),
        in_specs=[pl.BlockSpec(memory_space=pltpu.SMEM), blk(Q_W), full(2 * KV_W), full(2 * KV_W), blk(Q_W), blk(LANES), blk(Q_W)]
        + host.in_specs,
        out_specs=[blk(Q_W), full(KV_W), full(KV_W), pl.BlockSpec((8, LANES), lambda i: (0, 0))] + host.out_specs,
        out_shape=[jax.ShapeDtypeStruct((S, Q_W), F32), jax.ShapeDtypeStruct((S, KV_W), F32),
                   jax.ShapeDtypeStruct((S, KV_W), F32), jax.ShapeDtypeStruct((8, LANES), F32)] + host.out_shape,
        scratch_shapes=[pltpu.VMEM((ATTN_HEADS * BLOCK, 3 * BLOCK), BF16), pltpu.VMEM((ATTN_HEADS * BLOCK, 3 * BLOCK), BF16)]
        + host.scratch, input_output_aliases=host.alias,
        name=name, compiler_params=_cp(("arbitrary",)))(sinks, q, kd, vd, ao, lse, dcat, *host.args)


def _mem_probs(q_ref, kv_ref, h):
    scale = MEM_HEAD_DIM ** -0.5
    qh = q_ref[:, h * LANES:(h + 1) * LANES].astype(BF16)
    s = lax.dot_general(qh, kv_ref[:, h * LANES:(h + 1) * LANES], NT, preferred_element_type=F32) * scale
    m = jnp.max(s, axis=-1, keepdims=True)
    pe = jnp.exp(s - m)
    return qh, pe * (1.0 / jnp.sum(pe, axis=-1, keepdims=True))


def _memattn_fwd(p, qblk, kv, cat, *, name="memattn_fwd", tr=512):
    S = p.shape[0]
    tr = min(tr, S)

    def body(q_ref, kv_ref, cat_ref, o_ref):
        for h in range(MEM_HEADS):
            _, pr = _mem_probs(q_ref, kv_ref, h)
            o = jnp.dot(pr.astype(BF16), kv_ref[:, MEM_W + h * LANES:MEM_W + (h + 1) * LANES], preferred_element_type=F32)
            o_ref[:, h * LANES:(h + 1) * LANES] = o.astype(BF16)

    return pl.pallas_call(
        body, grid=(S // tr,),
        in_specs=[pl.BlockSpec((tr, MEM_W), lambda i: (i, qblk)), pl.BlockSpec((MEM_LEN, 2 * MEM_W), lambda i: (0, 0)), HBM],
        out_specs=pl.BlockSpec((tr, MEM_W), lambda i: (i, Q_W // MEM_W)),
        out_shape=jax.ShapeDtypeStruct((S, MIX_OUT_W), BF16), input_output_aliases={2: 0},
        name=name, compiler_params=_cp(("parallel",)))(p, kv, cat)


def _memattn_bwd(p, qblk, kv, dcat, *, name="memattn_bwd", tr=512, rider=None):
    S = p.shape[0]
    tr = min(tr, S)
    scale = MEM_HEAD_DIM ** -0.5
    host = _Hosted(rider, 3, 2)

    def body(*refs):
        ins, outs, _, rrefs = host.split(refs, 3, 2)
        host.run(rrefs, pl.program_id(0), S // tr, lambda: inner(*ins, *outs))

    def inner(q_ref, kv_ref, do_ref, dq_ref, dkv_ref):
        @pl.when(pl.program_id(0) == 0)
        def _():
            dkv_ref[...] = jnp.zeros_like(dkv_ref)

        for h in range(MEM_HEADS):
            qh, pr = _mem_probs(q_ref, kv_ref, h)
            doh = do_ref[:, h * LANES:(h + 1) * LANES]
            dp = lax.dot_general(doh, kv_ref[:, MEM_W + h * LANES:MEM_W + (h + 1) * LANES], NT, preferred_element_type=F32)
            delta = jnp.sum(pr * dp, axis=-1, keepdims=True)
            dsb = (pr * (dp - delta) * scale).astype(BF16)
            dq = jnp.dot(dsb, kv_ref[:, h * LANES:(h + 1) * LANES], preferred_element_type=F32)
            dq_ref[:, h * LANES:(h + 1) * LANES] = dq.astype(BF16)
            dkv_ref[:, h * LANES:(h + 1) * LANES] += lax.dot_general(dsb, qh, TN, preferred_element_type=F32)
            dkv_ref[:, MEM_W + h * LANES:MEM_W + (h + 1) * LANES] += lax.dot_general(
                pr.astype(BF16), doh, TN, preferred_element_type=F32)

    return pl.pallas_call(
        body, grid=(S // tr,),
        in_specs=[pl.BlockSpec((tr, MEM_W), lambda i: (i, qblk)), pl.BlockSpec((MEM_LEN, 2 * MEM_W), lambda i: (0, 0)),
                  pl.BlockSpec((tr, MEM_W), lambda i: (i, Q_W // MEM_W))] + host.in_specs,
        out_specs=[pl.BlockSpec((tr, MEM_W), lambda i: (i, 0)), pl.BlockSpec((MEM_LEN, 2 * MEM_W), lambda i: (0, 0))]
        + host.out_specs,
        out_shape=[jax.ShapeDtypeStruct((S, MEM_W), BF16), jax.ShapeDtypeStruct((MEM_LEN, 2 * MEM_W), F32)] + host.out_shape,
        scratch_shapes=host.scratch, input_output_aliases=host.alias,
        name=name, compiler_params=_cp(("arbitrary",)))(p, kv, dcat, *host.args)


def _sqrt(v):
    return jnp.where(v > 0.0, v * lax.rsqrt(v), 0.0)


def _sigmoid(z):
    return 1.0 / (1.0 + jnp.exp(-z))


def _one_minus_exp(z, exp_z):
    poly = z * (1.0 + z * (0.5 + z * (1.0 / 6.0 + z * (1.0 / 24.0 + z * (1.0 / 120.0)))))
    return jnp.where(z > -0.1, -poly, 1.0 - exp_z)


def _softplus_neg(lam):
    z = -lam
    return jnp.maximum(z, 0.0) + jnp.log(1.0 + jnp.exp(-jnp.abs(z)))


_GELU_C = math.sqrt(2.0 / math.pi)


def _gelu(z):
    return 0.5 * z * (1.0 + jnp.tanh(_GELU_C * (z + 0.044715 * z * z * z)))


def _row_or_zero(ref, t, S):
    ok = jnp.logical_and(t >= 0, t < S)
    return jnp.where(ok, ref[pl.ds(jnp.clip(t, 0, S - 1), 1), :], 0.0)


def _shift_down(v, first):
    ri = lax.broadcasted_iota(jnp.int32, v.shape, 0)
    return jnp.where(ri == 0, first, pltpu.roll(v, 1, 0))


def _shift_up(v, last):
    T = v.shape[0]
    ri = lax.broadcasted_iota(jnp.int32, v.shape, 0)
    return jnp.where(ri == T - 1, last, pltpu.roll(v, T - 1, 0))


def _scan_chunk(a, u, reverse):
    T = a.shape[0]
    ri = lax.broadcasted_iota(jnp.int32, a.shape, 0)
    d = 1
    while d < T:
        if reverse:
            a_s, u_s, ok = pltpu.roll(a, T - d, 0), pltpu.roll(u, T - d, 0), ri < T - d
        else:
            a_s, u_s, ok = pltpu.roll(a, d, 0), pltpu.roll(u, d, 0), ri >= d
        u = jnp.where(ok, a * u_s + u, u)
        a = jnp.where(ok, a * a_s, a)
        d *= 2
    return a, u


def _conv_taps(xb_ref, t0, S):
    T = SCAN_ROWS
    x0 = xb_ref[pl.ds(t0, T), :]
    xm1 = _shift_down(x0, _row_or_zero(xb_ref, t0 - 1, S))
    nxt0 = _row_or_zero(xb_ref, t0 + T, S)
    xp1 = _shift_up(x0, nxt0)
    xp2 = _shift_up(xp1, _row_or_zero(xb_ref, t0 + T + 1, S))
    return xm1, x0, xp1, xp2


def _lru_gates(xc, w_a, b_a, w_x, b_x, sp):
    xcb = xc.astype(BF16)
    r = _sigmoid(jnp.dot(xcb, w_a, preferred_element_type=F32) + b_a)
    i = _sigmoid(jnp.dot(xcb, w_x, preferred_element_type=F32) + b_x)
    la = -LRU_C * r * sp
    a = jnp.exp(la)
    b2 = _one_minus_exp(2.0 * la, a * a)
    inv_beta = lax.rsqrt(b2)
    return r, i, a, jnp.where(b2 > 0.0, b2 * inv_beta, 0.0), inv_beta


def _lru_specs(S):
    col = lambda off: pl.BlockSpec((S, LANES), lambda n: (0, n + off), pipeline_mode=pl.Buffered(1))
    small = lambda r: pl.BlockSpec((r, LANES), lambda n: (0, n))
    wblk = pl.BlockSpec((2, 1, LANES, LANES), lambda n: (0, n, 0, 0))
    return col, small, wblk


def _lru_fwd(p, conv_w, conv_b, wa, ba, wx, bx, lam, *, name="lru_fwd"):
    S = p.shape[0]
    T = SCAN_ROWS
    nc = S // T

    def body(xb_ref, gate_ref, cw_ref, cb_ref, wa_ref, ba_ref, wx_ref, bx_ref, lam_ref, y_ref, hf_ref, hr_ref, xc_v):
        sp = _softplus_neg(lam_ref[...])
        cw = cw_ref[...]

        def fwd_step(c, h_in):
            t0 = pl.multiple_of(c * T, T)
            xm1, x0, xp1, xp2 = _conv_taps(xb_ref, t0, S)
            xc = cb_ref[...] + xm1 * cw[0:1] + x0 * cw[1:2] + xp1 * cw[2:3] + xp2 * cw[3:4]
            xc_v[pl.ds(t0, T), :] = xc
            _, i, a, beta, _ = _lru_gates(xc, wa_ref[0, 0], ba_ref[0:1], wx_ref[0, 0], bx_ref[0:1], sp[0:1])
            A, U = _scan_chunk(a, beta * (i * xc), False)
            hf_ref[pl.ds(t0, T), :] = A * h_in + U
            return hf_ref[pl.ds(t0 + T - 1, 1), :]

        lax.fori_loop(0, nc, fwd_step, jnp.zeros((1, LANES), F32))

        def rev_step(k, h_in):
            t0 = pl.multiple_of((nc - 1 - k) * T, T)
            xc = xc_v[pl.ds(t0, T), :]
            _, i, a, beta, _ = _lru_gates(xc, wa_ref[1, 0], ba_ref[1:2], wx_ref[1, 0], bx_ref[1:2], sp[1:2])
            A, U = _scan_chunk(a, beta * (i * xc), True)
            h = A * h_in + U
            hr_ref[pl.ds(t0, T), :] = h
            y_ref[pl.ds(t0, T), :] = ((hf_ref[pl.ds(t0, T), :] + h) * _gelu(gate_ref[pl.ds(t0, T), :])).astype(BF16)
            return hr_ref[pl.ds(t0, 1), :]

        lax.fori_loop(0, nc, rev_step, jnp.zeros((1, LANES), F32))

    col, small, wblk = _lru_specs(S)
    colo = lambda: pl.BlockSpec((S, LANES), lambda n: (0, n))
    return pl.pallas_call(
        body, grid=(LRU_BLOCKS,),
        in_specs=[col(0), col(LRU_BLOCKS), small(4), small(1), wblk, small(2), wblk, small(2), small(2)],
        out_specs=[colo(), colo(), colo()],
        out_shape=[jax.ShapeDtypeStruct((S, MIX_OUT_W), BF16), jax.ShapeDtypeStruct((S, D_MODEL), F32),
                   jax.ShapeDtypeStruct((S, D_MODEL), F32)],
        scratch_shapes=[pltpu.VMEM((S, LANES), F32)],
        name=name, compiler_params=_cp(("parallel",)))(p, p, conv_w, conv_b, wa, ba, wx, bx, lam)


def _lru_bwd(p, hf, hr, dcat, conv_w, conv_b, wa, ba, wx, bx, lam, *, name="lru_bwd"):
    S = p.shape[0]
    T = SCAN_ROWS
    nc = S // T

    def body(xb_ref, gate_ref, hf_ref, hr_ref, dy_ref, cw_ref, cb_ref, wa_ref, ba_ref, wx_ref, bx_ref, lam_ref,
             dxb_ref, dgate_ref, dcw_ref, dcb_ref, dwa_ref, dba_ref, dwx_ref, dbx_ref, dlam_ref, xc_v, dxc_v, dh_v):
        lam_v = lam_ref[...]
        sp = _softplus_neg(lam_v)
        cw = cw_ref[...]
        for ref in (dcw_ref, dcb_ref, dwa_ref, dba_ref, dwx_ref, dbx_ref, dlam_ref):
            ref[...] = jnp.zeros_like(ref)

        def prep_step(c, carry):
            t0 = pl.multiple_of(c * T, T)
            rows = pl.ds(t0, T)
            xm1, x0, xp1, xp2 = _conv_taps(xb_ref, t0, S)
            xc_v[rows, :] = cb_ref[...] + xm1 * cw[0:1] + x0 * cw[1:2] + xp1 * cw[2:3] + xp2 * cw[3:4]
            z = gate_ref[rows, :]
            dy = dy_ref[rows, :].astype(F32)
            th = jnp.tanh(_GELU_C * (z + 0.044715 * z * z * z))
            dgelu = 0.5 * (1.0 + th) + 0.5 * z * (1.0 - th * th) * _GELU_C * (1.0 + 3.0 * 0.044715 * z * z)
            dgate_ref[rows, :] = (dy * (hf_ref[rows, :] + hr_ref[rows, :]) * dgelu).astype(BF16)
            dh_v[rows, :] = dy * (0.5 * z * (1.0 + th))
            return carry

        lax.fori_loop(0, nc, prep_step, 0)

        def direction(d):
            h_ref = hf_ref if d == 0 else hr_ref
            w_a, w_x = wa_ref[d, 0], wx_ref[d, 0]
            b_a, b_x, sp_d = ba_ref[d:d + 1], bx_ref[d:d + 1], sp[d:d + 1]

            def step(k, carry):
                g_in, a_in = carry
                c = (nc - 1 - k) if d == 0 else k
                t0 = pl.multiple_of(c * T, T)
                rows = pl.ds(t0, T)
                xc = xc_v[rows, :]
                r, i, a, beta, inv_beta = _lru_gates(xc, w_a, b_a, w_x, b_x, sp_d)
                dh = dh_v[rows, :]
                hc = h_ref[rows, :]
                if d == 0:
                    A, U = _scan_chunk(_shift_up(a, a_in), dh, True)
                    g = A * g_in + U
                    h_nb = _shift_down(hc, _row_or_zero(h_ref, t0 - 1, S))
                    nxt = (g[0:1], a[0:1])
                else:
                    A, U = _scan_chunk(_shift_down(a, a_in), dh, False)
                    g = A * g_in + U
                    h_nb = _shift_up(hc, _row_or_zero(h_ref, t0 + T, S))
                    nxt = (g[T - 1:T], a[T - 1:T])
                da = g * h_nb
                dbeta = g * (i * xc)
                tb = g * beta
                dla = da * a - dbeta * (a * a * inv_beta)
                dzr = (dla * (-LRU_C * sp_d)) * (r * (1.0 - r))
                dzi = (tb * xc) * (i * (1.0 - i))
                dzrb, dzib, xcb = dzr.astype(BF16), dzi.astype(BF16), xc.astype(BF16)
                dwa_ref[d, 0] += lax.dot_general(xcb, dzrb, TN, preferred_element_type=F32)
                dwx_ref[d, 0] += lax.dot_general(xcb, dzib, TN, preferred_element_type=F32)
                dba_ref[d:d + 1] += jnp.sum(dzr, axis=0, keepdims=True)
                dbx_ref[d:d + 1] += jnp.sum(dzi, axis=0, keepdims=True)
                dlam_ref[d:d + 1] += jnp.sum(dla * (-LRU_C * r), axis=0, keepdims=True)
                dxc = (tb * i + lax.dot_general(dzrb, w_a, NT, preferred_element_type=F32)
                       + lax.dot_general(dzib, w_x, NT, preferred_element_type=F32))
                if d == 0:
                    dxc_v[rows, :] = dxc
                else:
                    dxc_v[rows, :] += dxc
                return nxt

            lax.fori_loop(0, nc, step, (jnp.zeros((1, LANES), F32), jnp.zeros((1, LANES), F32)))

        direction(0)
        direction(1)
        dlam_ref[...] = dlam_ref[...] * (-1.0 / (1.0 + jnp.exp(lam_v)))

        def conv_step(c, carry):
            t0 = pl.multiple_of(c * T, T)
            rows = pl.ds(t0, T)
            g0 = dxc_v[rows, :]
            gm1 = _shift_down(g0, _row_or_zero(dxc_v, t0 - 1, S))
            gm2 = _shift_down(gm1, _row_or_zero(dxc_v, t0 - 2, S))
            gp1 = _shift_up(g0, _row_or_zero(dxc_v, t0 + T, S))
            dxb_ref[rows, :] = (cw[0:1] * gp1 + cw[1:2] * g0 + cw[2:3] * gm1 + cw[3:4] * gm2).astype(BF16)
            xm1, x0, xp1, xp2 = _conv_taps(xb_ref, t0, S)
            for tap, xs in enumerate((xm1, x0, xp1, xp2)):
                dcw_ref[tap:tap + 1] += jnp.sum(g0 * xs, axis=0, keepdims=True)
            dcb_ref[...] += jnp.sum(g0, axis=0, keepdims=True)
            return carry

        lax.fori_loop(0, nc, conv_step, 0)

    col, small, wblk = _lru_specs(S)
    colo = lambda: pl.BlockSpec((S, LANES), lambda n: (0, n), pipeline_mode=pl.Buffered(1))
    return pl.pallas_call(
        body, grid=(LRU_BLOCKS,),
        in_specs=[col(0), col(LRU_BLOCKS), col(0), col(0), col(0), small(4), small(1), wblk, small(2), wblk, small(2), small(2)],
        out_specs=[colo(), colo(), small(4), small(1), wblk, small(2), wblk, small(2), small(2)],
        out_shape=[jax.ShapeDtypeStruct((S, D_MODEL), BF16), jax.ShapeDtypeStruct((S, D_MODEL), BF16),
                   jax.ShapeDtypeStruct((4, D_MODEL), F32), jax.ShapeDtypeStruct((1, D_MODEL), F32),
                   jax.ShapeDtypeStruct((2, LRU_BLOCKS, LANES, LANES), F32), jax.ShapeDtypeStruct((2, D_MODEL), F32),
                   jax.ShapeDtypeStruct((2, LRU_BLOCKS, LANES, LANES), F32), jax.ShapeDtypeStruct((2, D_MODEL), F32),
                   jax.ShapeDtypeStruct((2, D_MODEL), F32)],
        scratch_shapes=[pltpu.VMEM((S, LANES), F32), pltpu.VMEM((S, LANES), F32), pltpu.VMEM((S, LANES), F32)],
        name=name, compiler_params=_cp(("parallel",)))(p, p, hf, hr, dcat, conv_w, conv_b, wa, ba, wx, bx, lam)


PK_UP, PK_DOWN, PK_KV, PK_OUT, PK_IN = 0, 1024, 2048, 2304, 2688
PK_ROWS = {0: PK_IN, 1: PK_IN + 640}
SMALL_G_ROWS = 192
PKF_KV, PKF_SMALL = 512, 768
PKF_ROWS = PKF_SMALL + SMALL_G_ROWS


def _mlp_bwd(x, dx, dxb, saved, w_up, w_down, gain, l, rider=None, next_rider=None):
    up, act, h = saved
    pk = _mm_tn(act, dxb, 1, name=f"dw_down{l}", packed=(None, PK_ROWS[l], PK_DOWN), rider=rider)
    pk, carried = pk if rider is not None else (pk, None)
    dup = _mm_nt(dxb, w_down, up=up, name=f"d_up{l}")
    rider_up = next_rider(carried) if next_rider is not None else None
    pk = _mm_tn(h, dup, N_CHIPS, name=f"dw_up{l}", packed=(pk, PK_ROWS[l], PK_UP), rider=rider_up)
    pk, carried = pk if rider_up is not None else (pk, carried)
    dx, dxb, g_gain = _mm_nt(dup, w_up, norm_x=x, norm_g=gain, dres=dx, name=f"d_mlp_in{l}")
    return dx, dxb, pk, g_gain, carried


def _reduce_first(pk, place, tag, recv):
    return _sum_halves(pk, recv, place, name=f"sum_halves{tag}", tr=pk.shape[1] // 4)


def _sum_parts(parts, place, tag):
    return _sum_chips(parts, place, name=f"sum_chips{tag}", tr=parts.shape[1] // 2)


def _local_step(x, mem, positions, target, W, pending=None, place=None):
    cos_t, sin_t = _rope_tables(positions)
    sinks = W["attn_sinks"].reshape(ATTN_HEADS)
    G = {}

    def hosting(late, fn, *args, **kw):
        if pending is None:
            return fn(*args, **kw)
        *res, buf = fn(*args, gather=pending[late], **kw)
        if late.startswith("w_down"):
            W.setdefault("w_down", [None] * DEPTH)[int(late[-1])] = _ready(late, buf)
        elif late == "w_up":
            W["w_up"], W["w_mem_kv"] = _ready(late, buf)
        else:
            W[late] = _ready(late, buf)
        return res if len(res) > 1 else res[0]

    p0, h0 = hosting("w_out", _mm_nn, x, W["attn_w_in"], norm_g=W["mix_norm"][0], name="attn_in")
    q, kd, vd, va = hosting("w_down0", _qk_prep, p0, cos_t, sin_t)
    ao, lse = hosting("w_up", _attn_fwd, q, kd, va, sinks)
    kv0, memn = _mm_nn(mem, W["w_mem_kv"][0], norm_g=W["mem_norm"], out_dtype=BF16, name="mem_kv0", tm=256)
    kv1 = _mm_nn(memn, W["w_mem_kv"][1], out_dtype=BF16, name="mem_kv1", tm=256)
    cat0 = _memattn_fwd(p0, Q_W // MEM_W + 1, kv0, ao, name="memattn_fwd0")
    x1 = hosting("lru_w_in", _mm_nn, cat0, W["w_out"][0], resid=x, name="mix_out0")
    up0, act0, h1 = hosting("w_down1", _mm_nn, x1, W["w_up"][0], norm_g=W["mlp_norm"][0], relu2=True, name="mlp_up0")
    x2, mlp0 = _mm_nn(act0, W["w_down"][0], resid=x1, name="mlp_down0"), (up0, act0, h1)
    p1, h2 = _mm_nn(x2, W["lru_w_in"], norm_g=W["mix_norm"][1], name="lru_in")
    lru_w = (W["lru_conv_w"], W["lru_conv_b"], W["lru_wa"], W["lru_ba"], W["lru_wx"], W["lru_bx"], W["lru_lambda"])
    y, hf, hr = _lru_fwd(p1, *lru_w)
    cat1 = _memattn_fwd(p1, 2 * D_MODEL // MEM_W, kv1, y, name="memattn_fwd1")
    x3 = _mm_nn(cat1, W["w_out"][1], resid=x2, name="mix_out1")
    mlp1 = _mm_nn(x3, W["w_up"][1], norm_g=W["mlp_norm"][1], relu2=True, name="mlp_up1")
    loss, dx, dxb, G["final_norm"] = _final(mlp1[1], W["w_down"][1], x3, W["final_norm"], target)

    def put(pk, off, g):
        return pk.at[:, off:off + g.size // (N_CHIPS * ROW)].set(g.reshape(N_CHIPS, -1, ROW))

    dx, dxb, pk1, gm1, _ = _mlp_bwd(x3, dx, dxb, mlp1, W["w_up"][1], W["w_down"][1], W["mlp_norm"][1], 1)
    pk1 = _mm_tn(cat1, dxb, 1, name="dw_out1", tk=384, packed=(pk1, PK_ROWS[1], PK_OUT))
    dcat1 = _mm_nt(dxb, W["w_out"][1], name="d_mix1")
    dmq1, dkv1 = _memattn_bwd(p1, 2 * D_MODEL // MEM_W, kv1, dcat1, name="memattn_bwd1")
    dkv1b = dkv1.astype(BF16)
    pk1 = _mm_tn(memn, dkv1b, 1, name="dw_kv1", tm=256, tk=256, packed=(pk1, PK_ROWS[1], PK_KV))
    (dxb1, dgate, G["lru_conv_w"], G["lru_conv_b"], G["lru_wa"], G["lru_ba"], G["lru_wx"], G["lru_bx"],
     G["lru_lambda"]) = _lru_bwd(p1, hf, hr, dcat1, *lru_w)
    dp1 = jnp.concatenate([dxb1, dgate, dmq1], axis=1)
    pk1 = put(pk1, PK_IN, _mm_tn(h2, dp1, N_CHIPS, name="dw_lru_in"))
    dx, dxb, gx1 = _mm_nt(dp1, W["lru_w_in"], norm_x=x2, norm_g=W["mix_norm"][1], dres=dx, name="d_lru_in")
    dist = place is not None
    h1_rows = PK_ROWS[1] // 4
    kept = {}

    def first_half(recv1):
        kept["halves1"], landing1 = _reduce_first(pk1, place, "1", recv1)
        return _exchange_rider((kept["halves1"], landing1, 0, h1_rows))

    dx, dxb, pk0, gm0, landing1 = _mlp_bwd(x1, dx, dxb, mlp0, W["w_up"][0], W["w_down"][0], W["mlp_norm"][0], 0,
                                           rider=_sib_exchange_rider(pk1) if dist else None,
                                           next_rider=first_half if dist else None)
    pk0 = _mm_tn(cat0, dxb, 1, name="dw_out0", tk=384, packed=(pk0, PK_ROWS[0], PK_OUT))
    pk0 = pk0.at[:, PK_KV:PK_OUT].set(0.0)
    dcat0 = _mm_nt(dxb, W["w_out"][0], name="d_mix0")
    dmq0, dkv0, *recv0 = _memattn_bwd(p0, Q_W // MEM_W + 1, kv0, dcat0, name="memattn_bwd0",
                                      rider=_sib_exchange_rider(pk0) if dist else None)
    dkv0b = dkv0.astype(BF16)
    g_kv0 = _mm_tn(memn, dkv0b, 1, name="dw_kv0", tm=256, tk=256)
    rider = None
    if dist:
        halves0, landing0 = _reduce_first(pk0, place, "0", recv0[0])
        rider = _exchange_rider((kept["halves1"], landing1, h1_rows, h1_rows), (halves0, landing0, 0, halves0.shape[1]))
    dq, dk, dv, dsink, *parts = _attn_bwd(q, kd, vd, cat0, lse, sinks, dcat0, rider=rider)
    dp0 = _qk_prep_bwd(dq, dk, dv, dmq0, cos_t, sin_t)
    g_in = _mm_tn(h0, dp0, N_CHIPS, name="dw_attn_in",
                  rider=_sib_allgather_rider(_sum_parts(parts[0], place, "1"), _sum_parts(parts[1], place, "0")) if dist else None)
    if dist:
        g_in, pk1, pk0 = g_in
    dx, _, gx0 = _mm_nt(dp0, W["attn_w_in"], norm_x=x, norm_g=W["mix_norm"][0], dres=dx, name="d_attn_in")

    w_kv_both = jnp.concatenate([W["w_mem_kv"][0], W["w_mem_kv"][1]], axis=0)
    _, _, G["mem_norm"] = _mm_nt(jnp.concatenate([dkv0b, dkv1b], axis=1), w_kv_both, norm_x=mem, norm_g=W["mem_norm"],
                                 name="d_mem", tm=256)

    G["mix_norm"] = jnp.concatenate([gx0, gx1], axis=0)
    G["mlp_norm"] = jnp.concatenate([gm0, gm1], axis=0)
    G["attn_sinks"] = dsink[0:1, 0:ATTN_HEADS]
    small = _flat_pad(_small_grad_list(G), N_CHIPS * SMALL_G_ROWS * ROW).reshape(N_CHIPS, SMALL_G_ROWS, ROW)
    pkf = jnp.concatenate([g_in.reshape(N_CHIPS, PKF_KV, ROW), g_kv0.reshape(N_CHIPS, PKF_SMALL - PKF_KV, ROW), small], axis=1)
    return loss[0, 0], dx, G, pkf, pk0, pk1


def _comm_call(body, out_shape, n_sems, name, *args, alias=None):
    return pl.pallas_call(
        body, out_shape=out_shape, in_specs=[HBM] * len(args), out_specs=HBM,
        scratch_shapes=[pltpu.SemaphoreType.DMA((n_sems,)), pltpu.SemaphoreType.DMA((n_sems,))],
        input_output_aliases=alias or {}, name=name)(*args)


def _place_slot(shard, slot, n_slots, *, name, tr):
    R, C = shard.shape

    def body(s_ref, a_ref, o_ref):
        o_ref[0] = a_ref[...]

    return pl.pallas_call(
        body,
        grid_spec=pltpu.PrefetchScalarGridSpec(
            num_scalar_prefetch=1, grid=(R // tr,), in_specs=[pl.BlockSpec((tr, C), lambda i, s_ref: (i, 0))],
            out_specs=pl.BlockSpec((1, tr, C), lambda i, s_ref: (s_ref[0], i, 0))),
        out_shape=jax.ShapeDtypeStruct((n_slots, R, C), shard.dtype), name=name,
        compiler_params=_cp(("parallel",)))(slot, shard)


def _allgather_chips(buf, *, name):
    def body(b_ref, o_ref, send_sems, recv_sems):
        _gather_start(o_ref, send_sems, recv_sems)
        _gather_finish(o_ref, send_sems, recv_sems)

    return _comm_call(body, jax.ShapeDtypeStruct(buf.shape, buf.dtype), GATHER_SEMS, name, buf, alias={0: 0})


def _sum_halves(g, recv, place, *, name="sum_halves", tr=480):
    _, R, C = g.shape
    half = R // 2
    nblk = half // tr

    def body(pl_ref, g_ref, r_ref, o_ref, own_ref):
        v = (g_ref[...] + r_ref[...]).astype(BF16)
        o_ref[...] = v

        @pl.when(pl.program_id(1) == pl_ref[1])
        def _():
            own_ref[...] = v

    blk = pl.BlockSpec((1, tr, C), lambda i, s, p: (s, i, 0))
    return pl.pallas_call(
        body,
        grid_spec=pltpu.PrefetchScalarGridSpec(
            num_scalar_prefetch=1, grid=(nblk, N_CHIPS),
            in_specs=[pl.BlockSpec((1, tr, C), lambda i, s, p: (s, p[0] * nblk + i, 0)), blk],
            out_specs=[blk, pl.BlockSpec((1, tr, C), lambda i, s, p: (p[1], i, 0))]),
        out_shape=[jax.ShapeDtypeStruct((N_CHIPS, half, C), BF16)] * 2, name=name,
        compiler_params=_cp(("parallel", "arbitrary")))(place, g, recv)


def _sum_chips(parts, place, *, name="sum_chips", tr=480):
    _, R, C = parts.shape
    nblk = R // tr

    def body(pl_ref, p_ref, o_ref):
        acc = p_ref[0].astype(F32) + p_ref[1].astype(F32)
        o_ref[...] = (acc + p_ref[2].astype(F32)) + p_ref[3].astype(F32)

    return pl.pallas_call(
        body,
        grid_spec=pltpu.PrefetchScalarGridSpec(
            num_scalar_prefetch=1, grid=(nblk,), in_specs=[pl.BlockSpec((N_CHIPS, tr, C), lambda i, p: (0, i, 0))],
            out_specs=pl.BlockSpec((tr, C), lambda i, p: (p[0] * nblk + i, 0))),
        out_shape=jax.ShapeDtypeStruct((2 * R, C), F32), name=name, compiler_params=_cp(("parallel",)))(place, parts)


def _adamw(w, g, m, v, *, name, tr=128, rider=None):
    R, C = w.shape
    bc1 = 1.0 - ADAM_B1 ** ADAM_STEP
    bc2 = 1.0 - ADAM_B2 ** ADAM_STEP
    host = _Hosted(rider, 4, 3)

    def body(*refs):
        ins, outs, _, rrefs = host.split(refs, 4, 3)
        host.run(rrefs, pl.program_id(0), R // tr, lambda: inner(*ins, *outs))

    def inner(w_ref, g_ref, m_ref, v_ref, d_ref, nm_ref, nv_ref):
        gv = g_ref[...]
        nm = ADAM_B1 * m_ref[...] + (1.0 - ADAM_B1) * gv
        nv = ADAM_B2 * v_ref[...] + (1.0 - ADAM_B2) * (gv * gv)
        d_ref[...] = -ADAM_LR * ((nm / bc1) / (_sqrt(nv / bc2) + ADAM_EPS) + ADAM_WD * w_ref[...])
        nm_ref[...] = nm
        nv_ref[...] = nv

    blk = pl.BlockSpec((tr, C), lambda i: (i, 0))
    return pl.pallas_call(
        body, grid=(R // tr,), in_specs=[blk] * 4 + host.in_specs, out_specs=[blk] * 3 + host.out_specs,
        out_shape=[jax.ShapeDtypeStruct((R, C), F32)] * 3 + host.out_shape, scratch_shapes=host.scratch,
        input_output_aliases=host.alias, name=name,
        compiler_params=_cp(("arbitrary",) if host.on else ("parallel",)))(w, g, m, v, *host.args)


ROW = 1024
BIG = ("w_mem_kv", "w_out", "w_up", "w_down", "attn_w_in", "lru_w_in")
SMALL_SHARDED = ("lru_conv_w", "lru_conv_b", "lru_ba", "lru_bx", "lru_lambda")
REPLICATED = ("mix_norm", "mlp_norm", "mem_norm", "final_norm", "attn_sinks", "lru_wa", "lru_wx")
SMALL = REPLICATED + SMALL_SHARDED
WEIGHTS = ("mix_norm", "mlp_norm", "mem_norm", "final_norm", "w_mem_kv", "w_out", "w_up", "w_down", "attn_w_in",
           "attn_sinks", "lru_w_in", "lru_conv_w", "lru_conv_b", "lru_wa", "lru_ba", "lru_wx", "lru_bx", "lru_lambda")
SMALL_W_ROWS = 32
ADAM_SMALL_ROWS = 640


def _rows(a):
    return a.reshape(-1, ROW)


def _flat_pad(parts, total):
    flat = jnp.concatenate([p.reshape(-1) for p in parts])
    return jnp.pad(flat, (0, total - flat.shape[0]))


def _pad_rows(a):
    flat = a.reshape(-1)
    n = -(-flat.shape[0] // ROW) * ROW
    return jnp.pad(flat, (0, n - flat.shape[0])).reshape(-1, ROW)


LATE = ("w_out", "w_down0", "w_up", "lru_w_in", "w_down1")


def _ready(name, full):
    if name == "w_out":
        wo = full.reshape(N_CHIPS, DEPTH, -1, D_MODEL)
        return [wo[:, l].reshape(1, MIX_OUT_W, D_MODEL) for l in range(DEPTH)]
    if name == "w_up":
        n_up = DEPTH * D_MODEL
        wu = full[:, :n_up].reshape(N_CHIPS, DEPTH, D_MODEL, D_FF // N_CHIPS)
        kv = full[:, n_up:].reshape(N_CHIPS, DEPTH, -1, D_MODEL)
        return [wu[:, l] for l in range(DEPTH)], [kv[:, l].reshape(1, D_MODEL, D_MODEL) for l in range(DEPTH)]
    if name == "lru_w_in":
        return full.reshape(N_CHIPS, D_MODEL, LRU_IN_W // N_CHIPS)
    return full.reshape(1, D_FF, D_MODEL)


def _gather_weights(P, chip1):
    bf = lambda a: _rows(a.astype(BF16))
    small = _flat_pad([P[n] for n in SMALL_SHARDED], SMALL_W_ROWS * ROW // 2)
    small_bits = lax.bitcast_convert_type(small, BF16).reshape(SMALL_W_ROWS, ROW)
    early = jnp.concatenate([bf(P["attn_w_in"]), small_bits], axis=0)
    n_in = P["attn_w_in"].size // ROW
    placed = _place_slot(early, chip1, N_CHIPS, name="place_weights", tr=early.shape[0] // 2)
    full = _allgather_chips(placed, name="allgather_weights")
    late = {"w_out": bf(P["w_out"]), "lru_w_in": bf(P["lru_w_in"]),
            "w_up": jnp.concatenate([bf(P["w_up"]), bf(P["w_mem_kv"])], axis=0),
            "w_down0": bf(P["w_down"][0]), "w_down1": bf(P["w_down"][1])}
    pending = {n: _place_slot(late[n], chip1, N_CHIPS, name=f"place_{n}", tr=late[n].shape[0] // 2) for n in LATE}
    W = {n: P[n] for n in REPLICATED}
    W["attn_w_in"] = full[:, :n_in].reshape(N_CHIPS, D_MODEL, ATTN_IN_W // N_CHIPS)
    sm = lax.bitcast_convert_type(full[:, n_in:].reshape(N_CHIPS, -1, 2), F32)
    o = 0
    for n in SMALL_SHARDED:
        shp = P[n].shape[1:]
        cnt = math.prod(shp)
        piece = sm[:, o:o + cnt].reshape((N_CHIPS,) + shp)
        piece = jnp.moveaxis(piece, 0, -2)
        W[n] = piece.reshape(shp[:-1] + (N_CHIPS * shp[-1],)).reshape(-1, D_MODEL)
        o += cnt
    W["lru_wa"] = P["lru_wa"][0].astype(BF16)
    W["lru_wx"] = P["lru_wx"][0].astype(BF16)
    return W, pending


def _small_grad_list(G):
    return [G["mix_norm"], G["mlp_norm"], G["mem_norm"], G["final_norm"], jnp.pad(G["attn_sinks"].reshape(-1), (0, ROW - ATTN_HEADS)),
            G["lru_wa"], G["lru_wx"], G["lru_conv_w"], G["lru_conv_b"], G["lru_ba"], G["lru_bx"], G["lru_lambda"]]


SMALL_G_SIZES = (2 * D_MODEL, 2 * D_MODEL, D_MODEL, D_MODEL, ROW, 2 * 8 * 128 * 128, 2 * 8 * 128 * 128,
                 4 * D_MODEL, D_MODEL, 2 * D_MODEL, 2 * D_MODEL, 2 * D_MODEL)


def _finish_grads(pkf, full0, full1, place, chip1, adamw):
    both = lambda off, r: jnp.concatenate([full0[off:off + r], full1[off:off + r]], axis=0)
    recvf = adamw("w_out", both(PK_OUT, 384), _sib_exchange_rider(pkf))
    halvesf, landingf = _reduce_first(pkf, place, "f", recvf)
    hr = halvesf.shape[1] // 2
    landingf = adamw("w_up", both(PK_UP, 1024), _exchange_rider((halvesf, landingf, 0, hr)))
    partsf = adamw("w_down", both(PK_DOWN, 1024), _exchange_rider((halvesf, landingf, hr, hr)))
    fullf = adamw("lru_w_in", full1[PK_IN:PK_IN + 640], _sib_allgather_rider(_sum_parts(partsf, place, "f")))
    small_placed = _place_slot(fullf[PKF_SMALL:], chip1, N_CHIPS, name="place_small_grads", tr=SMALL_G_ROWS)
    small_all = adamw("w_mem_kv", jnp.concatenate([fullf[PKF_KV:PKF_SMALL], full1[PK_KV:PK_OUT]], axis=0),
                      _gather_rider(small_placed))
    adamw("attn_w_in", fullf[:PKF_KV], None)
    flat = small_all.reshape(-1)
    small = {}
    o = 0
    names = ("mix_norm", "mlp_norm", "mem_norm", "final_norm", "attn_sinks", "lru_wa", "lru_wx",
             "lru_conv_w", "lru_conv_b", "lru_ba", "lru_bx", "lru_lambda")
    for n, cnt in zip(names, SMALL_G_SIZES):
        small[n] = flat[o:o + cnt]
        o += cnt
    return small


def kernel(x, mem, positions, mix_norm, mlp_norm, mem_norm, final_norm, w_mem_kv, w_out, w_up, w_down, attn_w_in, attn_sinks, lru_w_in, lru_conv_w, lru_conv_b, lru_wa, lru_ba, lru_wx, lru_bx, lru_lambda, loss_target, m_mix_norm, m_mlp_norm, m_mem_norm, m_final_norm, m_w_mem_kv, m_w_out, m_w_up, m_w_down, m_attn_w_in, m_attn_sinks, m_lru_w_in, m_lru_conv_w, m_lru_conv_b, m_lru_wa, m_lru_ba, m_lru_wx, m_lru_bx, m_lru_lambda, v_mix_norm, v_mlp_norm, v_mem_norm, v_final_norm, v_w_mem_kv, v_w_out, v_w_up, v_w_down, v_attn_w_in, v_attn_sinks, v_lru_w_in, v_lru_conv_w, v_lru_conv_b, v_lru_wa, v_lru_ba, v_lru_wx, v_lru_bx, v_lru_lambda):
    P = dict(mix_norm=mix_norm, mlp_norm=mlp_norm, mem_norm=mem_norm, final_norm=final_norm, w_mem_kv=w_mem_kv, w_out=w_out,
             w_up=w_up, w_down=w_down, attn_w_in=attn_w_in, attn_sinks=attn_sinks, lru_w_in=lru_w_in, lru_conv_w=lru_conv_w,
             lru_conv_b=lru_conv_b, lru_wa=lru_wa, lru_ba=lru_ba, lru_wx=lru_wx, lru_bx=lru_bx, lru_lambda=lru_lambda)
    M1 = dict(mix_norm=m_mix_norm, mlp_norm=m_mlp_norm, mem_norm=m_mem_norm, final_norm=m_final_norm, w_mem_kv=m_w_mem_kv,
              w_out=m_w_out, w_up=m_w_up, w_down=m_w_down, attn_w_in=m_attn_w_in, attn_sinks=m_attn_sinks, lru_w_in=m_lru_w_in,
              lru_conv_w=m_lru_conv_w, lru_conv_b=m_lru_conv_b, lru_wa=m_lru_wa, lru_ba=m_lru_ba, lru_wx=m_lru_wx,
              lru_bx=m_lru_bx, lru_lambda=m_lru_lambda)
    V2 = dict(mix_norm=v_mix_norm, mlp_norm=v_mlp_norm, mem_norm=v_mem_norm, final_norm=v_final_norm, w_mem_kv=v_w_mem_kv,
              w_out=v_w_out, w_up=v_w_up, w_down=v_w_down, attn_w_in=v_attn_w_in, attn_sinks=v_attn_sinks, lru_w_in=v_lru_w_in,
              lru_conv_w=v_lru_conv_w, lru_conv_b=v_lru_conv_b, lru_wa=v_lru_wa, lru_ba=v_lru_ba, lru_wx=v_lru_wx,
              lru_bx=v_lru_bx, lru_lambda=v_lru_lambda)
    chip = 2 * lax.axis_index("x") + lax.axis_index("y")
    chip1 = chip.astype(jnp.int32).reshape(1)
    place = jnp.stack([lax.axis_index("c").astype(jnp.int32), chip.astype(jnp.int32)])

    W, pending = _gather_weights(P, chip1)
    loss, dx, _, pkf, full0, full1 = _local_step(x[0], mem[0], positions[0], loss_target[0], W, pending, place)
    loss = lax.psum(loss, ("x", "y", "c"))
    grads, deltas, new_m, new_v = {}, {}, {}, {}

    def adamw_big(n, g, rider):
        d, nm, nv, *carried = _adamw(_rows(P[n]), g, _rows(M1[n]), _rows(V2[n]), name=f"adamw_{n}", rider=rider)
        grads[n], deltas[n], new_m[n], new_v[n] = (t.reshape(P[n].shape) for t in (g, d, nm, nv))
        return carried[0] if carried else None

    small = _finish_grads(pkf, full0, full1, place, chip1, adamw_big)

    for n in SMALL:
        g = small[n]
        if n in SMALL_SHARDED:
            shard = P[n].shape[-1]
            g = lax.dynamic_slice_in_dim(g.reshape(-1, N_CHIPS * shard), chip * shard, shard, axis=1)
        elif n == "attn_sinks":
            g = g[:ATTN_HEADS]
        grads[n] = g.reshape(P[n].shape)
    packs = []
    for src in (P, grads, M1, V2):
        a = jnp.concatenate([_pad_rows(src[n]) for n in SMALL], axis=0)
        packs.append(jnp.pad(a, ((0, ADAM_SMALL_ROWS - a.shape[0]), (0, 0))))
    d_s, nm_s, nv_s = _adamw(*packs, name="adamw_small")
    o = 0
    for n in SMALL:
        cnt = math.prod(P[n].shape)
        r = -(-cnt // ROW)
        for dst, src in ((deltas, d_s), (new_m, nm_s), (new_v, nv_s)):
            dst[n] = src[o:o + r].reshape(-1)[:cnt].reshape(P[n].shape)
        o += r

    return (loss, dx[None], *[grads[n] for n in WEIGHTS], *[deltas[n] for n in WEIGHTS],
            *[new_m[n] for n in WEIGHTS], *[new_v[n] for n in WEIGHTS])
```

```python
import math

import jax
import jax.numpy as jnp
from jax import lax
from jax.experimental import pallas as pl
from jax.experimental.pallas import tpu as pltpu

F32 = jnp.float32
BF16 = jnp.bfloat16
MESH = pl.DeviceIdType.MESH

D_MODEL = 1024
DEPTH = 2
EPS = 1e-6
ATTN_HEADS = 16
ATTN_KV_HEADS = 4
HEAD_DIM = 64
WINDOW = 128
BLOCK = 128
ROPE_THETA = 500000.0
ROPE_DIM = 16
Q_W = 1024
KV_W = 256
MEM_LEN = 256
MEM_HEADS = 4
MEM_HEAD_DIM = 128
MEM_W = 512
LRU_BLOCKS = 8
LRU_C = 8.0
ATTN_IN_W = 2048
LRU_IN_W = 2560
MIX_OUT_W = 1536
D_FF = 4096
NEG = -1e30
N_CHIPS = 4

ADAM_LR = 0.001
ADAM_B1 = 0.9
ADAM_B2 = 0.999
ADAM_EPS = 1e-08
ADAM_WD = 0.01
ADAM_STEP = 10

LANES = 128
SCAN_ROWS = 512
VMEM_LIMIT = 56 * 1024 * 1024

NT = (((1,), (1,)), ((), ()))
TN = (((0,), (0,)), ((), ()))


def _cp(sem=None):
    return pltpu.CompilerParams(dimension_semantics=sem, vmem_limit_bytes=VMEM_LIMIT)


HBM = pl.BlockSpec(memory_space=pl.ANY)
GATHER_SEMS = 6


def _place():
    x, y, c = lax.axis_index("x"), lax.axis_index("y"), lax.axis_index("c")
    chips = [(1 - x, y), (x, 1 - y), (1 - x, 1 - y)]
    return x, y, c, chips


def _remote(src, dst, send_sems, recv_sems, k, to):
    return pltpu.make_async_remote_copy(src_ref=src, dst_ref=dst, send_sem=send_sems.at[k], recv_sem=recv_sems.at[k],
                                        device_id=to, device_id_type=MESH)


def _gather_start(o_ref, send_sems, recv_sems):
    x, y, c, chips = _place()
    half = o_ref.shape[1] // 2
    own = o_ref.at[2 * x + y, pl.ds(pl.multiple_of(c * half, 16), half)]
    for j, (cx, cy) in enumerate(chips):
        _remote(own, own, send_sems, recv_sems, j, (cx, cy, c)).start()


def _gather_forward(o_ref, send_sems, recv_sems):
    x, y, c, chips = _place()
    half = o_ref.shape[1] // 2
    my_rows = pl.ds(pl.multiple_of(c * half, 16), half)
    for j, (cx, cy) in enumerate(chips):
        landed = o_ref.at[2 * cx + cy, my_rows]
        _remote(landed, landed, send_sems, recv_sems, j, (cx, cy, c)).wait_recv()
        _remote(landed, landed, send_sems, recv_sems, 3 + j, (x, y, 1 - c)).start()


def _gather_drain(o_ref, send_sems, recv_sems):
    x, y, c, chips = _place()
    half = o_ref.shape[1] // 2
    my_rows = pl.ds(pl.multiple_of(c * half, 16), half)
    sib_rows = pl.ds(pl.multiple_of((1 - c) * half, 16), half)
    own = o_ref.at[2 * x + y, my_rows]
    for j, (cx, cy) in enumerate(chips):
        got = o_ref.at[2 * cx + cy, sib_rows]
        _remote(got, got, send_sems, recv_sems, 3 + j, (x, y, 1 - c)).wait_recv()
    for j, (cx, cy) in enumerate(chips):
        _remote(own, own, send_sems, recv_sems, j, (cx, cy, c)).wait_send()
        landed = o_ref.at[2 * cx + cy, my_rows]
        _remote(landed, landed, send_sems, recv_sems, 3 + j, (x, y, 1 - c)).wait_send()


def _gather_finish(o_ref, send_sems, recv_sems):
    _gather_forward(o_ref, send_sems, recv_sems)
    _gather_drain(o_ref, send_sems, recv_sems)


def _exchange_start(h_ref, o_ref, send_sems, recv_sems, rows=None, base=0):
    x, y, c, chips = _place()
    rows = pl.ds(0, h_ref.shape[1]) if rows is None else rows
    for j, (cx, cy) in enumerate(chips):
        _remote(h_ref.at[2 * cx + cy, rows], o_ref.at[2 * x + y, rows], send_sems, recv_sems, base + j, (cx, cy, c)).start()


def _exchange_finish(h_ref, o_ref, send_sems, recv_sems, rows=None, base=0):
    x, y, c, chips = _place()
    rows = pl.ds(0, h_ref.shape[1]) if rows is None else rows
    for j, (cx, cy) in enumerate(chips):
        got = o_ref.at[2 * cx + cy, rows]
        _remote(got, got, send_sems, recv_sems, base + j, (cx, cy, c)).wait_recv()
    for j, (cx, cy) in enumerate(chips):
        _remote(h_ref.at[2 * cx + cy, rows], o_ref.at[2 * x + y, rows], send_sems, recv_sems, base + j, (cx, cy, c)).wait_send()


def _sib_exchange_copies(g_ref, o_ref, send_sems, recv_sems):
    x, y, c, _ = _place()
    half = g_ref.shape[1] // 2
    other = pl.ds(pl.multiple_of((1 - c) * half, 8), half)
    return [_remote(g_ref.at[s, other], o_ref.at[s], send_sems, recv_sems, s, (x, y, 1 - c)) for s in range(N_CHIPS)]


def _sib_exchange_start(*refs):
    for cp in _sib_exchange_copies(*refs):
        cp.start()


def _sib_exchange_finish(*refs):
    for cp in _sib_exchange_copies(*refs):
        cp.wait()


def _sib_allgather_start(*refs):
    *o_refs, send_sems, recv_sems = refs
    x, y, c, _ = _place()
    for i, o_ref in enumerate(o_refs):
        half = o_ref.shape[0] // 2
        mine = o_ref.at[pl.ds(pl.multiple_of(c * half, 8), half)]
        _remote(mine, mine, send_sems, recv_sems, i, (x, y, 1 - c)).start()


def _sib_allgather_finish(*refs):
    *o_refs, send_sems, recv_sems = refs
    x, y, c, _ = _place()
    for i, o_ref in enumerate(o_refs):
        half = o_ref.shape[0] // 2
        mine = o_ref.at[pl.ds(pl.multiple_of(c * half, 8), half)]
        got = o_ref.at[pl.ds(pl.multiple_of((1 - c) * half, 8), half)]
        _remote(got, got, send_sems, recv_sems, i, (x, y, 1 - c)).wait_recv()
        _remote(mine, mine, send_sems, recv_sems, i, (x, y, 1 - c)).wait_send()


class _Rider:
    def __init__(self, args, start, finish, inplace=1, mid=None):
        self.args, self.start, self.finish, self.inplace, self.mid = list(args), start, finish, inplace, mid


def _gather_rider(buf, early=True):
    if buf is None:
        return None
    return _Rider([buf], _gather_start, _gather_finish, mid=(_gather_forward, _gather_drain) if early else None)


def _exchange_rider(*parts):
    n = len(parts)
    assert 3 * n <= GATHER_SEMS

    def run(fn):
        def go(*refs):
            sems = refs[2 * n:]
            for i, (_, _, r0, nr) in enumerate(parts):
                fn(refs[i], refs[n + i], *sems, rows=pl.ds(r0, nr), base=3 * i)
        return go

    return _Rider([p[0] for p in parts] + [p[1] for p in parts], run(_exchange_start), run(_exchange_finish), inplace=n)


def _sib_exchange_rider(g):
    landing = lax.empty((N_CHIPS, g.shape[1] // 2, g.shape[2]), g.dtype)
    return _Rider([g, landing], _sib_exchange_start, _sib_exchange_finish)


def _sib_allgather_rider(*fulls):
    return _Rider(fulls, _sib_allgather_start, _sib_allgather_finish, inplace=len(fulls))


class _Hosted:
    def __init__(self, rider, n_in, n_out):
        self.rider = rider
        self.on = rider is not None
        self.args = rider.args if self.on else []
        k = len(self.args)
        p = self.p = rider.inplace if self.on else 0
        self.alias = {n_in + k - p + i: n_out + i for i in range(p)}
        self.in_specs = [HBM] * k
        self.out_specs = [HBM] * p
        self.out_shape = [jax.ShapeDtypeStruct(a.shape, a.dtype) for a in self.args[k - p:]]
        self.scratch = [pltpu.SemaphoreType.DMA((GATHER_SEMS,)), pltpu.SemaphoreType.DMA((GATHER_SEMS,))] if self.on else []

    def split(self, refs, n_in, n_out):
        refs = list(refs)
        if not self.on:
            return refs[:n_in], refs[n_in:n_in + n_out], refs[n_in + n_out:], None
        k, p = len(self.args), self.p
        ins, outs = refs[:n_in], refs[n_in + k:n_in + k + n_out]
        rest = refs[n_in + k + n_out + p:]
        rrefs = refs[n_in:n_in + k - p] + refs[n_in + k + n_out:n_in + k + n_out + p] + [rest[-2], rest[-1]]
        return ins, outs, rest[:-2], rrefs

    def run(self, rrefs, step, n_steps, compute):
        if rrefs is None:
            return compute()

        mid = self.rider.mid
        mid_step = (3 * n_steps) // 4
        two_stage = mid is not None and 0 < mid_step < n_steps - 1

        @pl.when(step == 0)
        def _():
            self.rider.start(*rrefs)

        compute()

        if two_stage:
            @pl.when(step == mid_step)
            def _():
                mid[0](*rrefs)

        @pl.when(step == n_steps - 1)
        def _():
            (mid[1] if two_stage else self.rider.finish)(*rrefs)


def _mm_nn(a, w3, *, name, out_dtype=F32, norm_g=None, resid=None, relu2=False, tm=512, gather=None):
    M, K = a.shape
    ns, _, n = w3.shape
    N = ns * n
    tm = min(tm, M)
    has_norm = norm_g is not None
    has_res = resid is not None
    n_in = 2 + has_norm + has_res
    n_out = (2 if relu2 else 1) + has_norm
    host = _Hosted(_gather_rider(gather), n_in, n_out)

    def body(*refs):
        ins, outs, _, gref = host.split(refs, n_in, n_out)
        a_ref, w_ref = ins[0], ins[1]
        g_ref = ins[2] if has_norm else None
        r_ref = ins[-1] if has_res else None

        def compute():
            if has_norm:
                xv = a_ref[...]
                rs = lax.rsqrt(jnp.mean(xv * xv, axis=-1, keepdims=True) + EPS)
                ab = (xv * rs * g_ref[...]).astype(BF16)
                outs[-1][...] = ab
            else:
                ab = a_ref[...]
            for s in range(ns):
                acc = jnp.dot(ab, w_ref[s], preferred_element_type=F32)
                sl = slice(s * n, (s + 1) * n)
                if relu2:
                    outs[0][:, sl] = acc.astype(BF16)
                    rl = jnp.maximum(acc, 0.0)
                    outs[1][:, sl] = (rl * rl).astype(BF16)
                elif has_res:
                    outs[0][:, sl] = r_ref[:, sl] + acc
                else:
                    outs[0][:, sl] = acc.astype(out_dtype)

        host.run(gref, pl.program_id(0), M // tm, compute)

    row = lambda w: pl.BlockSpec((tm, w), lambda i: (i, 0))
    in_specs = [row(K), pl.BlockSpec((ns, K, n), lambda i: (0, 0, 0))]
    args = [a, w3]
    if has_norm:
        in_specs.append(pl.BlockSpec((1, K), lambda i: (0, 0)))
        args.append(norm_g.reshape(1, K))
    if has_res:
        in_specs.append(row(N))
        args.append(resid)
    if relu2:
        out_shape = [jax.ShapeDtypeStruct((M, N), BF16), jax.ShapeDtypeStruct((M, N), BF16)]
        out_specs = [row(N), row(N)]
    else:
        out_shape = [jax.ShapeDtypeStruct((M, N), F32 if has_res else out_dtype)]
        out_specs = [row(N)]
    if has_norm:
        out_shape.append(jax.ShapeDtypeStruct((M, K), BF16))
        out_specs.append(row(K))
    res = pl.pallas_call(body, grid=(M // tm,), in_specs=in_specs + host.in_specs, out_specs=out_specs + host.out_specs,
                         out_shape=out_shape + host.out_shape, scratch_shapes=host.scratch, input_output_aliases=host.alias,
                         name=name, compiler_params=_cp(("arbitrary",) if host.on else ("parallel",)))(*args, *host.args)
    return res if len(res) > 1 else res[0]


def _mm_nt(g, w3, *, name, out_dtype=BF16, up=None, norm_x=None, norm_g=None, dres=None, tm=512):
    M = g.shape[0]
    ns, K, n = w3.shape
    tm = min(tm, M)
    has_up = up is not None
    has_norm = norm_x is not None
    has_res = dres is not None

    def body(*refs):
        refs = list(refs)
        g_ref, w_ref = refs[0], refs[1]
        pos = 2
        if has_up:
            up_ref = refs[pos]
            pos += 1
        if has_norm:
            x_ref, gn_ref = refs[pos], refs[pos + 1]
            pos += 2
        if has_res:
            r_ref = refs[pos]
            pos += 1
        outs = refs[pos:]
        acc = None
        for s in range(ns):
            part = lax.dot_general(g_ref[:, s * n:(s + 1) * n], w_ref[s], NT, preferred_element_type=F32)
            acc = part if acc is None else acc + part
        if has_up:
            outs[0][...] = (acc * (2.0 * jnp.maximum(up_ref[...].astype(F32), 0.0))).astype(BF16)
        elif has_norm:
            xv = x_ref[...]
            rs = lax.rsqrt(jnp.mean(xv * xv, axis=-1, keepdims=True) + EPS)
            xn = xv * rs
            dxn = acc * gn_ref[...]
            dx = rs * (dxn - xn * jnp.mean(dxn * xn, axis=-1, keepdims=True))
            if has_res:
                dx = dx + r_ref[...]
            outs[0][...] = dx
            outs[1][...] = dx.astype(BF16)

            @pl.when(pl.program_id(0) == 0)
            def _():
                outs[2][...] = jnp.zeros_like(outs[2])

            outs[2][...] += jnp.sum(acc * xn, axis=0, keepdims=True)
        else:
            outs[0][...] = acc.astype(out_dtype)

    row = lambda w: pl.BlockSpec((tm, w), lambda i: (i, 0))
    in_specs = [row(ns * n), pl.BlockSpec((ns, K, n), lambda i: (0, 0, 0))]
    args = [g, w3]
    if has_up:
        in_specs.append(row(K))
        args.append(up)
    if has_norm:
        in_specs += [row(K), pl.BlockSpec((1, K), lambda i: (0, 0))]
        args += [norm_x, norm_g.reshape(1, K)]
    if has_res:
        in_specs.append(row(K))
        args.append(dres)
    if has_norm:
        out_shape = [jax.ShapeDtypeStruct((M, K), F32), jax.ShapeDtypeStruct((M, K), BF16),
                     jax.ShapeDtypeStruct((1, K), F32)]
        out_specs = [row(K), row(K), pl.BlockSpec((1, K), lambda i: (0, 0))]
        sem = ("arbitrary",)
    else:
        out_shape = [jax.ShapeDtypeStruct((M, K), BF16 if has_up else out_dtype)]
        out_specs = [row(K)]
        sem = ("parallel",)
    res = pl.pallas_call(body, grid=(M // tm,), in_specs=in_specs, out_specs=out_specs, out_shape=out_shape,
                         name=name, compiler_params=_cp(sem))(*args)
    return res if len(res) > 1 else res[0]


def _mm_tn(a, g, ns, *, name, tk=512, tm=4096, packed=None, rider=None):
    M, K = a.shape
    n = g.shape[1] // ns
    tm = min(tm, M)
    tk = min(tk, K)
    nk, nm = K // tk, M // tm
    n_in = 3 if (packed is not None and packed[0] is not None) else 2
    host = _Hosted(rider, n_in, 1)

    def body(*refs):
        ins, outs, _, rrefs = host.split(refs, n_in, 1)
        a_ref, g_ref, o_ref = ins[0], ins[1], outs[0]

        def compute():
            @pl.when(pl.program_id(2) == 0)
            def _():
                o_ref[...] = jnp.zeros_like(o_ref)

            o_ref[0] += lax.dot_general(a_ref[...], g_ref[...], TN, preferred_element_type=F32)

        step = (pl.program_id(0) * nk + pl.program_id(1)) * nm + pl.program_id(2)
        host.run(rrefs, step, ns * nk * nm, compute)

    in_specs = [pl.BlockSpec((tm, tk), lambda s, k, m: (m, k)), pl.BlockSpec((tm, n), lambda s, k, m: (m, s))]
    args = [a, g]
    alias = {}
    if packed is None:
        out_spec = pl.BlockSpec((1, tk, n), lambda s, k, m: (s, k, 0))
        out_shape = jax.ShapeDtypeStruct((ns, K, n), F32)
    else:
        buf, rows, off = packed
        per_chip = K * ns // N_CHIPS
        assert n == ROW and per_chip % tk == 0 and off % tk == 0
        if ns == N_CHIPS:
            out_spec = pl.BlockSpec((1, tk, n), lambda s, k, m: (s, off // tk + k, 0))
        else:
            kpc = per_chip // tk
            out_spec = pl.BlockSpec((1, tk, n), lambda s, k, m: (k // kpc, off // tk + k % kpc, 0))
        out_shape = jax.ShapeDtypeStruct((N_CHIPS, rows, ROW), F32)
        if buf is not None:
            in_specs.append(HBM)
            args.append(buf)
            alias = {2: 0}
    sem = ("arbitrary",) * 3 if host.on else ("parallel", "parallel", "arbitrary")
    res = pl.pallas_call(
        body, grid=(ns, nk, nm), in_specs=in_specs + host.in_specs, out_specs=[out_spec] + host.out_specs,
        out_shape=[out_shape] + host.out_shape, scratch_shapes=host.scratch, name=name,
        input_output_aliases={**alias, **host.alias}, compiler_params=_cp(sem))(*args, *host.args)
    return res if host.on else res[0]


def _final(act, w_down, x, gain, target, *, name="mlp_down_final", tr=512):
    S, Dm = x.shape
    tr = min(tr, S)
    Kf = act.shape[1]

    def body(a_ref, w_ref, x_ref, g_ref, t_ref, loss_ref, dx_ref, dxb_ref, dg_ref):
        @pl.when(pl.program_id(0) == 0)
        def _():
            loss_ref[...] = jnp.zeros_like(loss_ref)
            dg_ref[...] = jnp.zeros_like(dg_ref)

        xv = x_ref[...] + jnp.dot(a_ref[...], w_ref[0], preferred_element_type=F32)
        gv = g_ref[...]
        rs = lax.rsqrt(jnp.mean(xv * xv, axis=-1, keepdims=True) + EPS)
        xn = xv * rs
        err = xn * gv - t_ref[...]
        loss_ref[...] += 0.5 * jnp.sum(jnp.mean(err * err, axis=-1, keepdims=True), axis=0, keepdims=True)
        dout = err * (1.0 / Dm)
        dg_ref[...] += jnp.sum(dout * xn, axis=0, keepdims=True)
        dxn = dout * gv
        dx = rs * (dxn - xn * jnp.mean(dxn * xn, axis=-1, keepdims=True))
        dx_ref[...] = dx
        dxb_ref[...] = dx.astype(BF16)

    row = pl.BlockSpec((tr, Dm), lambda i: (i, 0))
    return pl.pallas_call(
        body, grid=(S // tr,),
        in_specs=[pl.BlockSpec((tr, Kf), lambda i: (i, 0)), pl.BlockSpec((1, Kf, Dm), lambda i: (0, 0, 0)), row,
                  pl.BlockSpec((1, Dm), lambda i: (0, 0)), row],
        out_specs=[pl.BlockSpec((1, 1), lambda i: (0, 0)), row, row, pl.BlockSpec((1, Dm), lambda i: (0, 0))],
        out_shape=[jax.ShapeDtypeStruct((1, 1), F32), jax.ShapeDtypeStruct((S, Dm), F32),
                   jax.ShapeDtypeStruct((S, Dm), BF16), jax.ShapeDtypeStruct((1, Dm), F32)],
        name=name, compiler_params=_cp(("arbitrary",)))(act, w_down, x, gain.reshape(1, Dm), target)


def _rope_tables(positions):
    half = ROPE_DIM // 2
    inv_freq = ROPE_THETA ** (-2.0 * jnp.arange(half, dtype=F32) / ROPE_DIM)
    ang = positions.astype(F32)[:, None] * inv_freq
    cos, sin = jnp.cos(ang), jnp.sin(ang)
    S = positions.shape[0]
    ones = jnp.ones((S, HEAD_DIM - ROPE_DIM), F32)
    cos64 = jnp.concatenate([cos, cos, ones], axis=1)
    sin64 = jnp.concatenate([-sin, sin, 0.0 * ones], axis=1)
    return jnp.tile(cos64, (1, 2)), jnp.tile(sin64, (1, 2))


def _rope_partner(t):
    lane = lax.broadcasted_iota(jnp.int32, t.shape, 1)
    low = (lane & (HEAD_DIM - 1)) < (ROPE_DIM // 2)
    return jnp.where(low, pltpu.roll(t, LANES - ROPE_DIM // 2, 1), pltpu.roll(t, ROPE_DIM // 2, 1))


def _qk_prep(p, cos_t, sin_t, *, name="qk_prep", tr=256, gather=None):
    S = p.shape[0]
    tr = min(tr, S)
    scale = HEAD_DIM ** -0.5
    host = _Hosted(_gather_rider(gather, early=False), 3, 4)

    def body(*refs):
        ins, outs, _, gref = host.split(refs, 3, 4)
        host.run(gref, pl.program_id(0), S // tr, lambda: inner(*ins, *outs))

    def inner(p_ref, c_ref, s_ref, q_ref, k_ref, v_ref, va_ref):
        cs, sn = c_ref[...], s_ref[...]
        lane = lax.broadcasted_iota(jnp.int32, (tr, LANES), 1)
        lo = lane < HEAD_DIM
        for c in range(Q_W // LANES):
            t = p_ref[:, c * LANES:(c + 1) * LANES]
            q_ref[:, c * LANES:(c + 1) * LANES] = ((t * cs + _rope_partner(t) * sn) * scale).astype(BF16)
        for c in range(KV_W // LANES):
            t = p_ref[:, Q_W + c * LANES:Q_W + (c + 1) * LANES]
            kc = t * cs + _rope_partner(t) * sn
            vc = p_ref[:, Q_W + KV_W + c * LANES:Q_W + KV_W + (c + 1) * LANES]
            for arr, ref in ((kc, k_ref), (vc, v_ref)):
                sw = pltpu.roll(arr, HEAD_DIM, 1)
                ref[:, (2 * c) * LANES:(2 * c + 1) * LANES] = jnp.where(lo, arr, sw).astype(BF16)
                ref[:, (2 * c + 1) * LANES:(2 * c + 2) * LANES] = jnp.where(lo, sw, arr).astype(BF16)
            sw = pltpu.roll(vc, HEAD_DIM, 1)
            for k, aug in enumerate((jnp.where(lo, vc, 1.0), jnp.where(lo, 1.0, sw), jnp.where(lo, sw, 1.0), jnp.where(lo, 1.0, vc))):
                va_ref[:, (4 * c + k) * LANES:(4 * c + k + 1) * LANES] = aug.astype(BF16)

    row = lambda w: pl.BlockSpec((tr, w), lambda i: (i, 0))
    return pl.pallas_call(
        body, grid=(S // tr,), in_specs=[row(ATTN_IN_W), row(LANES), row(LANES)] + host.in_specs,
        out_specs=[row(Q_W), row(2 * KV_W), row(2 * KV_W), row(4 * KV_W)] + host.out_specs,
        out_shape=[jax.ShapeDtypeStruct((S, Q_W), BF16), jax.ShapeDtypeStruct((S, 2 * KV_W), BF16),
                   jax.ShapeDtypeStruct((S, 2 * KV_W), BF16), jax.ShapeDtypeStruct((S, 4 * KV_W), BF16)] + host.out_shape,
        scratch_shapes=host.scratch, input_output_aliases=host.alias,
        name=name, compiler_params=_cp(("arbitrary",) if host.on else ("parallel",)))(p, cos_t, sin_t, *host.args)


def _qk_prep_bwd(dq, dk, dv, dmq, cos_t, sin_t, *, name="qk_prep_bwd", tr=256):
    S = dq.shape[0]
    tr = min(tr, S)

    def body(dq_ref, dk_ref, dv_ref, dmq_ref, c_ref, s_ref, o_ref):
        cs, sn = c_ref[...], s_ref[...]
        for c in range(Q_W // LANES):
            t = dq_ref[:, c * LANES:(c + 1) * LANES]
            o_ref[:, c * LANES:(c + 1) * LANES] = (t * cs - _rope_partner(t) * sn).astype(BF16)
        for c in range(KV_W // LANES):
            t = dk_ref[:, c * LANES:(c + 1) * LANES]
            o_ref[:, Q_W + c * LANES:Q_W + (c + 1) * LANES] = (t * cs - _rope_partner(t) * sn).astype(BF16)
        o_ref[:, Q_W + KV_W:Q_W + 2 * KV_W] = dv_ref[...].astype(BF16)
        o_ref[:, Q_W + 2 * KV_W:] = dmq_ref[...]

    row = lambda w: pl.BlockSpec((tr, w), lambda i: (i, 0))
    return pl.pallas_call(
        body, grid=(S // tr,), in_specs=[row(Q_W), row(KV_W), row(KV_W), row(MEM_W), row(LANES), row(LANES)],
        out_specs=row(ATTN_IN_W), out_shape=jax.ShapeDtypeStruct((S, ATTN_IN_W), BF16),
        name=name, compiler_params=_cp(("parallel",)))(dq, dk, dv, dmq, cos_t, sin_t)


def _band(n, S):
    start = pl.multiple_of(jnp.clip((n - 1) * BLOCK, 0, S - 3 * BLOCK), BLOCK)
    qi = lax.broadcasted_iota(jnp.int32, (BLOCK, 3 * BLOCK), 0) + n * BLOCK
    ki = lax.broadcasted_iota(jnp.int32, (BLOCK, 3 * BLOCK), 1) + start
    return start, jnp.abs(ki - qi) <= WINDOW


def _head_operand(ref, h, lo):
    c = h // 2
    t = ref[:, c * LANES:(c + 1) * LANES].astype(F32)
    return jnp.where(lo if h % 2 == 0 else jnp.logical_not(lo), t, 0.0).astype(BF16)


GROUP = ATTN_HEADS // ATTN_KV_HEADS
EVENS_FIRST = (0, 2, 1, 3)


def _attn_fwd(q, kd, va, sinks, *, name="attn_fwd", gather=None):
    S = q.shape[0]
    host = _Hosted(_gather_rider(gather, early=False), 4, 2)

    def body(*refs):
        ins, outs, scr, gref = host.split(refs, 4, 2)
        host.run(gref, pl.program_id(0), S // BLOCK, lambda: inner(*ins, *outs, *scr))

    def inner(sink_ref, q_ref, k_ref, va_ref, o_ref, lse_ref, p_scr):
        n = pl.program_id(0)
        start, mask = _band(n, S)
        lane = lax.broadcasted_iota(jnp.int32, (BLOCK, LANES), 1)
        lo = lane < HEAD_DIM
        rows = pl.ds(start, 3 * BLOCK)
        scores = []
        for g in range(ATTN_KV_HEADS):
            qst = jnp.concatenate([_head_operand(q_ref, GROUP * g + j, lo) for j in EVENS_FIRST], axis=0)
            scores.append(lax.dot_general(qst, k_ref[rows, g * LANES:(g + 1) * LANES], NT, preferred_element_type=F32))
        ms = {}
        for g in range(ATTN_KV_HEADS):
            for pos, j in enumerate(EVENS_FIRST):
                h = GROUP * g + j
                s = jnp.where(mask, scores[g][pos * BLOCK:(pos + 1) * BLOCK], NEG)
                ms[h] = jnp.maximum(jnp.max(s, axis=-1, keepdims=True), sink_ref[h])
                p_scr[(GROUP * g + pos) * BLOCK:(GROUP * g + pos + 1) * BLOCK, :] = jnp.exp(s - ms[h]).astype(BF16)
        pvs = {}
        for g in range(ATTN_KV_HEADS):
            for par in range(2):
                r0 = (GROUP * g + 2 * par) * BLOCK
                pvs[g, par] = jnp.dot(p_scr[r0:r0 + 2 * BLOCK, :], va_ref[rows, (2 * g + par) * LANES:(2 * g + par + 1) * LANES],
                                      preferred_element_type=F32)
        lse_blk = jnp.zeros((BLOCK, LANES), F32)
        for g in range(ATTN_KV_HEADS):
            outs = {}
            for par in range(2):
                for k in range(2):
                    j = EVENS_FIRST[2 * par + k]
                    h = GROUP * g + j
                    pv = pvs[g, par][k * BLOCK:(k + 1) * BLOCK]
                    den = pltpu.roll(pv, HEAD_DIM, 1) + jnp.exp(sink_ref[h] - ms[h])
                    outs[j] = pv * (1.0 / den)
                    l = den[:, par * HEAD_DIM:par * HEAD_DIM + 1]
                    lse_blk = jnp.where(lane == h, ms[h] + jnp.log(l), lse_blk)
            for jj in range(2):
                o_ref[:, (2 * g + jj) * LANES:(2 * g + jj + 1) * LANES] = jnp.where(lo, outs[2 * jj], outs[2 * jj + 1]).astype(BF16)
        lse_ref[...] = lse_blk

    full = lambda w: pl.BlockSpec((S, w), lambda i: (0, 0))
    return pl.pallas_call(
        body, grid=(S // BLOCK,),
        in_specs=[pl.BlockSpec(memory_space=pltpu.SMEM), pl.BlockSpec((BLOCK, Q_W), lambda i: (i, 0)),
                  full(2 * KV_W), full(4 * KV_W)] + host.in_specs,
        out_specs=[pl.BlockSpec((BLOCK, Q_W), lambda i: (i, 0)), pl.BlockSpec((BLOCK, LANES), lambda i: (i, 0))] + host.out_specs,
        out_shape=[jax.ShapeDtypeStruct((S, MIX_OUT_W), BF16), jax.ShapeDtypeStruct((S, LANES), F32)] + host.out_shape,
        scratch_shapes=[pltpu.VMEM((ATTN_HEADS * BLOCK, 3 * BLOCK), BF16)] + host.scratch, input_output_aliases=host.alias,
        name=name, compiler_params=_cp(("arbitrary",) if host.on else ("parallel",)))(sinks, q, kd, va, *host.args)


def _attn_bwd(q, kd, vd, ao, lse, sinks, dcat, *, name="attn_bwd", rider=None):
    S = q.shape[0]
    scale = HEAD_DIM ** -0.5
    host = _Hosted(rider, 7, 4)

    def body(*refs):
        ins, outs, scr, rrefs = host.split(refs, 7, 4)
        host.run(rrefs, pl.program_id(0), S // BLOCK, lambda: inner(*ins, *outs, *scr))

    def inner(sink_ref, q_ref, k_ref, v_ref, ao_ref, lse_ref, do_ref, dq_ref, dk_ref, dv_ref, ds_ref, p_scr, dsb_scr):
        n = pl.program_id(0)

        @pl.when(n == 0)
        def _():
            dk_ref[...] = jnp.zeros_like(dk_ref)
            dv_ref[...] = jnp.zeros_like(dv_ref)
            ds_ref[...] = jnp.zeros_like(ds_ref)

        start, mask = _band(n, S)
        lane = lax.broadcasted_iota(jnp.int32, (BLOCK, LANES), 1)
        lo = lane < HEAD_DIM
        lane3 = lax.broadcasted_iota(jnp.int32, (3 * BLOCK, LANES), 1)
        row8 = lax.broadcasted_iota(jnp.int32, (8, LANES), 0)
        lane8 = lax.broadcasted_iota(jnp.int32, (8, LANES), 1)
        dsink = jnp.zeros((8, LANES), F32)
        lse_blk = lse_ref[...]
        rows = pl.ds(start, 3 * BLOCK)
        lses, deltas = {}, {}
        for c in range(Q_W // LANES):
            prod = do_ref[:, c * LANES:(c + 1) * LANES].astype(F32) * ao_ref[:, c * LANES:(c + 1) * LANES].astype(F32)
            for k in range(2):
                h = 2 * c + k
                deltas[h] = jnp.sum(jnp.where(lo if k == 0 else jnp.logical_not(lo), prod, 0.0), axis=1, keepdims=True)
                lses[h] = jnp.sum(jnp.where(lane == h, lse_blk, 0.0), axis=1, keepdims=True)
                val = -jnp.sum(jnp.exp(sink_ref[h] - lses[h]) * deltas[h], axis=0, keepdims=True)
                dsink = dsink + jnp.where((row8 == 0) & (lane8 == h), val, 0.0)
        stack = lambda ref, g: jnp.concatenate([_head_operand(ref, GROUP * g + j, lo) for j in range(GROUP)], axis=0)
        ss, dps = [], []
        for g in range(ATTN_KV_HEADS):
            ss.append(lax.dot_general(stack(q_ref, g), k_ref[rows, g * LANES:(g + 1) * LANES], NT, preferred_element_type=F32))
            dps.append(lax.dot_general(stack(do_ref, g), v_ref[rows, g * LANES:(g + 1) * LANES], NT, preferred_element_type=F32))
        for g in range(ATTN_KV_HEADS):
            for j in range(GROUP):
                h = GROUP * g + j
                r = slice(j * BLOCK, (j + 1) * BLOCK)
                hr = slice(h * BLOCK, (h + 1) * BLOCK)
                p = jnp.exp(jnp.where(mask, ss[g][r], NEG) - lses[h])
                p_scr[hr, :] = p.astype(BF16)
                dsb_scr[hr, :] = (p * (dps[g][r] - deltas[h])).astype(BF16)
        for g in range(ATTN_KV_HEADS):
            cols = slice((g // 2) * LANES, (g // 2 + 1) * LANES)
            gr = slice(GROUP * g * BLOCK, GROUP * (g + 1) * BLOCK)
            dsg = dsb_scr[gr, :]
            dqs = jnp.dot(dsg, k_ref[rows, g * LANES:(g + 1) * LANES], preferred_element_type=F32) * scale
            for jj in range(2):
                dq_ref[:, (2 * g + jj) * LANES:(2 * g + jj + 1) * LANES] = jnp.where(
                    lo, dqs[(2 * jj) * BLOCK:(2 * jj + 1) * BLOCK], dqs[(2 * jj + 1) * BLOCK:(2 * jj + 2) * BLOCK])
            half = (lane3 < HEAD_DIM) if g % 2 == 0 else (lane3 >= HEAD_DIM)
            dkr = lax.dot_general(dsg, stack(q_ref, g), TN, preferred_element_type=F32)
            dk_ref[rows, cols] += jnp.where(half, dkr + pltpu.roll(dkr, HEAD_DIM, 1), 0.0)
            dvr = lax.dot_general(p_scr[gr, :], stack(do_ref, g), TN, preferred_element_type=F32)
            dv_ref[rows, cols] += jnp.where(half, dvr + pltpu.roll(dvr, HEAD_DIM, 1), 0.0)
        ds_ref[...] += dsink

    full = lambda w: pl.BlockSpec((S, w), lambda i: (0, 0))
    blk = lambda w: pl.BlockSpec((BLOCK, w), lambda i: (i, 0))
    return pl.pallas_call(
        body, grid=(S // BLOCK,),
        in_specs=[pl.BlockSpec(memory_space=pltpu.SMEM), blk(Q_W), full(2 * KV_W), full(2 * KV_W), blk(Q_W), blk(LANES), blk(Q_W)]
        + host.in_specs,
        out_specs=[blk(Q_W), full(KV_W), full(KV_W), pl.BlockSpec((8, LANES), lambda i: (0, 0))] + host.out_specs,
        out_shape=[jax.ShapeDtypeStruct((S, Q_W), F32), jax.ShapeDtypeStruct((S, KV_W), F32),
                   jax.ShapeDtypeStruct((S, KV_W), F32), jax.ShapeDtypeStruct((8, LANES), F32)] + host.out_shape,
        scratch_shapes=[pltpu.VMEM((ATTN_HEADS * BLOCK, 3 * BLOCK), BF16), pltpu.VMEM((ATTN_HEADS * BLOCK, 3 * BLOCK), BF16)]
        + host.scratch, input_output_aliases=host.alias,
        name=name, compiler_params=_cp(("arbitrary",)))(sinks, q, kd, vd, ao, lse, dcat, *host.args)


def _mem_probs(q_ref, kv_ref, h):
    scale = MEM_HEAD_DIM ** -0.5
    qh = q_ref[:, h * LANES:(h + 1) * LANES].astype(BF16)
    s = lax.dot_general(qh, kv_ref[:, h * LANES:(h + 1) * LANES], NT, preferred_element_type=F32) * scale
    m = jnp.max(s, axis=-1, keepdims=True)
    pe = jnp.exp(s - m)
    return qh, pe * (1.0 / jnp.sum(pe, axis=-1, keepdims=True))


def _memattn_fwd(p, qblk, kv, cat, *, name="memattn_fwd", tr=512):
    S = p.shape[0]
    tr = min(tr, S)

    def body(q_ref, kv_ref, cat_ref, o_ref):
        for h in range(MEM_HEADS):
            _, pr = _mem_probs(q_ref, kv_ref, h)
            o = jnp.dot(pr.astype(BF16), kv_ref[:, MEM_W + h * LANES:MEM_W + (h + 1) * LANES], preferred_element_type=F32)
            o_ref[:, h * LANES:(h + 1) * LANES] = o.astype(BF16)

    return pl.pallas_call(
        body, grid=(S // tr,),
        in_specs=[pl.BlockSpec((tr, MEM_W), lambda i: (i, qblk)), pl.BlockSpec((MEM_LEN, 2 * MEM_W), lambda i: (0, 0)), HBM],
        out_specs=pl.BlockSpec((tr, MEM_W), lambda i: (i, Q_W // MEM_W)),
        out_shape=jax.ShapeDtypeStruct((S, MIX_OUT_W), BF16), input_output_aliases={2: 0},
        name=name, compiler_params=_cp(("parallel",)))(p, kv, cat)


def _memattn_bwd(p, qblk, kv, dcat, *, name="memattn_bwd", tr=512, rider=None):
    S = p.shape[0]
    tr = min(tr, S)
    scale = MEM_HEAD_DIM ** -0.5
    host = _Hosted(rider, 3, 2)

    def body(*refs):
        ins, outs, _, rrefs = host.split(refs, 3, 2)
        host.run(rrefs, pl.program_id(0), S // tr, lambda: inner(*ins, *outs))

    def inner(q_ref, kv_ref, do_ref, dq_ref, dkv_ref):
        @pl.when(pl.program_id(0) == 0)
        def _():
            dkv_ref[...] = jnp.zeros_like(dkv_ref)

        for h in range(MEM_HEADS):
            qh, pr = _mem_probs(q_ref, kv_ref, h)
            doh = do_ref[:, h * LANES:(h + 1) * LANES]
            dp = lax.dot_general(doh, kv_ref[:, MEM_W + h * LANES:MEM_W + (h + 1) * LANES], NT, preferred_element_type=F32)
            delta = jnp.sum(pr * dp, axis=-1, keepdims=True)
            dsb = (pr * (dp - delta) * scale).astype(BF16)
            dq = jnp.dot(dsb, kv_ref[:, h * LANES:(h + 1) * LANES], preferred_element_type=F32)
            dq_ref[:, h * LANES:(h + 1) * LANES] = dq.astype(BF16)
            dkv_ref[:, h * LANES:(h + 1) * LANES] += lax.dot_general(dsb, qh, TN, preferred_element_type=F32)
            dkv_ref[:, MEM_W + h * LANES:MEM_W + (h + 1) * LANES] += lax.dot_general(
                pr.astype(BF16), doh, TN, preferred_element_type=F32)

    return pl.pallas_call(
        body, grid=(S // tr,),
        in_specs=[pl.BlockSpec((tr, MEM_W), lambda i: (i, qblk)), pl.BlockSpec((MEM_LEN, 2 * MEM_W), lambda i: (0, 0)),
                  pl.BlockSpec((tr, MEM_W), lambda i: (i, Q_W // MEM_W))] + host.in_specs,
        out_specs=[pl.BlockSpec((tr, MEM_W), lambda i: (i, 0)), pl.BlockSpec((MEM_LEN, 2 * MEM_W), lambda i: (0, 0))]
        + host.out_specs,
        out_shape=[jax.ShapeDtypeStruct((S, MEM_W), BF16), jax.ShapeDtypeStruct((MEM_LEN, 2 * MEM_W), F32)] + host.out_shape,
        scratch_shapes=host.scratch, input_output_aliases=host.alias,
        name=name, compiler_params=_cp(("arbitrary",)))(p, kv, dcat, *host.args)


def _sqrt(v):
    return jnp.where(v > 0.0, v * lax.rsqrt(v), 0.0)


def _sigmoid(z):
    return 1.0 / (1.0 + jnp.exp(-z))


def _one_minus_exp(z, exp_z):
    poly = z * (1.0 + z * (0.5 + z * (1.0 / 6.0 + z * (1.0 / 24.0 + z * (1.0 / 120.0)))))
    return jnp.where(z > -0.1, -poly, 1.0 - exp_z)


def _softplus_neg(lam):
    z = -lam
    return jnp.maximum(z, 0.0) + jnp.log(1.0 + jnp.exp(-jnp.abs(z)))


_GELU_C = math.sqrt(2.0 / math.pi)


def _gelu(z):
    return 0.5 * z * (1.0 + jnp.tanh(_GELU_C * (z + 0.044715 * z * z * z)))


def _row_or_zero(ref, t, S):
    ok = jnp.logical_and(t >= 0, t < S)
    return jnp.where(ok, ref[pl.ds(jnp.clip(t, 0, S - 1), 1), :], 0.0)


def _shift_down(v, first):
    ri = lax.broadcasted_iota(jnp.int32, v.shape, 0)
    return jnp.where(ri == 0, first, pltpu.roll(v, 1, 0))


def _shift_up(v, last):
    T = v.shape[0]
    ri = lax.broadcasted_iota(jnp.int32, v.shape, 0)
    return jnp.where(ri == T - 1, last, pltpu.roll(v, T - 1, 0))


def _scan_chunk(a, u, reverse):
    T = a.shape[0]
    ri = lax.broadcasted_iota(jnp.int32, a.shape, 0)
    d = 1
    while d < T:
        if reverse:
            a_s, u_s, ok = pltpu.roll(a, T - d, 0), pltpu.roll(u, T - d, 0), ri < T - d
        else:
            a_s, u_s, ok = pltpu.roll(a, d, 0), pltpu.roll(u, d, 0), ri >= d
        u = jnp.where(ok, a * u_s + u, u)
        a = jnp.where(ok, a * a_s, a)
        d *= 2
    return a, u


def _conv_taps(xb_ref, t0, S):
    T = SCAN_ROWS
    x0 = xb_ref[pl.ds(t0, T), :]
    xm1 = _shift_down(x0, _row_or_zero(xb_ref, t0 - 1, S))
    nxt0 = _row_or_zero(xb_ref, t0 + T, S)
    xp1 = _shift_up(x0, nxt0)
    xp2 = _shift_up(xp1, _row_or_zero(xb_ref, t0 + T + 1, S))
    return xm1, x0, xp1, xp2


def _lru_gates(xc, w_a, b_a, w_x, b_x, sp):
    xcb = xc.astype(BF16)
    r = _sigmoid(jnp.dot(xcb, w_a, preferred_element_type=F32) + b_a)
    i = _sigmoid(jnp.dot(xcb, w_x, preferred_element_type=F32) + b_x)
    la = -LRU_C * r * sp
    a = jnp.exp(la)
    b2 = _one_minus_exp(2.0 * la, a * a)
    inv_beta = lax.rsqrt(b2)
    return r, i, a, jnp.where(b2 > 0.0, b2 * inv_beta, 0.0), inv_beta


def _lru_specs(S):
    col = lambda off: pl.BlockSpec((S, LANES), lambda n: (0, n + off), pipeline_mode=pl.Buffered(1))
    small = lambda r: pl.BlockSpec((r, LANES), lambda n: (0, n))
    wblk = pl.BlockSpec((2, 1, LANES, LANES), lambda n: (0, n, 0, 0))
    return col, small, wblk


def _lru_fwd(p, conv_w, conv_b, wa, ba, wx, bx, lam, *, name="lru_fwd"):
    S = p.shape[0]
    T = SCAN_ROWS
    nc = S // T

    def body(xb_ref, gate_ref, cw_ref, cb_ref, wa_ref, ba_ref, wx_ref, bx_ref, lam_ref, y_ref, hf_ref, hr_ref, xc_v):
        sp = _softplus_neg(lam_ref[...])
        cw = cw_ref[...]

        def fwd_step(c, h_in):
            t0 = pl.multiple_of(c * T, T)
            xm1, x0, xp1, xp2 = _conv_taps(xb_ref, t0, S)
            xc = cb_ref[...] + xm1 * cw[0:1] + x0 * cw[1:2] + xp1 * cw[2:3] + xp2 * cw[3:4]
            xc_v[pl.ds(t0, T), :] = xc
            _, i, a, beta, _ = _lru_gates(xc, wa_ref[0, 0], ba_ref[0:1], wx_ref[0, 0], bx_ref[0:1], sp[0:1])
            A, U = _scan_chunk(a, beta * (i * xc), False)
            hf_ref[pl.ds(t0, T), :] = A * h_in + U
            return hf_ref[pl.ds(t0 + T - 1, 1), :]

        lax.fori_loop(0, nc, fwd_step, jnp.zeros((1, LANES), F32))

        def rev_step(k, h_in):
            t0 = pl.multiple_of((nc - 1 - k) * T, T)
            xc = xc_v[pl.ds(t0, T), :]
            _, i, a, beta, _ = _lru_gates(xc, wa_ref[1, 0], ba_ref[1:2], wx_ref[1, 0], bx_ref[1:2], sp[1:2])
            A, U = _scan_chunk(a, beta * (i * xc), True)
            h = A * h_in + U
            hr_ref[pl.ds(t0, T), :] = h
            y_ref[pl.ds(t0, T), :] = ((hf_ref[pl.ds(t0, T), :] + h) * _gelu(gate_ref[pl.ds(t0, T), :])).astype(BF16)
            return hr_ref[pl.ds(t0, 1), :]

        lax.fori_loop(0, nc, rev_step, jnp.zeros((1, LANES), F32))

    col, small, wblk = _lru_specs(S)
    colo = lambda: pl.BlockSpec((S, LANES), lambda n: (0, n))
    return pl.pallas_call(
        body, grid=(LRU_BLOCKS,),
        in_specs=[col(0), col(LRU_BLOCKS), small(4), small(1), wblk, small(2), wblk, small(2), small(2)],
        out_specs=[colo(), colo(), colo()],
        out_shape=[jax.ShapeDtypeStruct((S, MIX_OUT_W), BF16), jax.ShapeDtypeStruct((S, D_MODEL), F32),
                   jax.ShapeDtypeStruct((S, D_MODEL), F32)],
        scratch_shapes=[pltpu.VMEM((S, LANES), F32)],
        name=name, compiler_params=_cp(("parallel",)))(p, p, conv_w, conv_b, wa, ba, wx, bx, lam)


def _lru_bwd(p, hf, hr, dcat, conv_w, conv_b, wa, ba, wx, bx, lam, *, name="lru_bwd"):
    S = p.shape[0]
    T = SCAN_ROWS
    nc = S // T

    def body(xb_ref, gate_ref, hf_ref, hr_ref, dy_ref, cw_ref, cb_ref, wa_ref, ba_ref, wx_ref, bx_ref, lam_ref,
             dxb_ref, dgate_ref, dcw_ref, dcb_ref, dwa_ref, dba_ref, dwx_ref, dbx_ref, dlam_ref, xc_v, dxc_v, dh_v):
        lam_v = lam_ref[...]
        sp = _softplus_neg(lam_v)
        cw = cw_ref[...]
        for ref in (dcw_ref, dcb_ref, dwa_ref, dba_ref, dwx_ref, dbx_ref, dlam_ref):
            ref[...] = jnp.zeros_like(ref)

        def prep_step(c, carry):
            t0 = pl.multiple_of(c * T, T)
            rows = pl.ds(t0, T)
            xm1, x0, xp1, xp2 = _conv_taps(xb_ref, t0, S)
            xc_v[rows, :] = cb_ref[...] + xm1 * cw[0:1] + x0 * cw[1:2] + xp1 * cw[2:3] + xp2 * cw[3:4]
            z = gate_ref[rows, :]
            dy = dy_ref[rows, :].astype(F32)
            th = jnp.tanh(_GELU_C * (z + 0.044715 * z * z * z))
            dgelu = 0.5 * (1.0 + th) + 0.5 * z * (1.0 - th * th) * _GELU_C * (1.0 + 3.0 * 0.044715 * z * z)
            dgate_ref[rows, :] = (dy * (hf_ref[rows, :] + hr_ref[rows, :]) * dgelu).astype(BF16)
            dh_v[rows, :] = dy * (0.5 * z * (1.0 + th))
            return carry

        lax.fori_loop(0, nc, prep_step, 0)

        def direction(d):
            h_ref = hf_ref if d == 0 else hr_ref
            w_a, w_x = wa_ref[d, 0], wx_ref[d, 0]
            b_a, b_x, sp_d = ba_ref[d:d + 1], bx_ref[d:d + 1], sp[d:d + 1]

            def step(k, carry):
                g_in, a_in = carry
                c = (nc - 1 - k) if d == 0 else k
                t0 = pl.multiple_of(c * T, T)
                rows = pl.ds(t0, T)
                xc = xc_v[rows, :]
                r, i, a, beta, inv_beta = _lru_gates(xc, w_a, b_a, w_x, b_x, sp_d)
                dh = dh_v[rows, :]
                hc = h_ref[rows, :]
                if d == 0:
                    A, U = _scan_chunk(_shift_up(a, a_in), dh, True)
                    g = A * g_in + U
                    h_nb = _shift_down(hc, _row_or_zero(h_ref, t0 - 1, S))
                    nxt = (g[0:1], a[0:1])
                else:
                    A, U = _scan_chunk(_shift_down(a, a_in), dh, False)
                    g = A * g_in + U
                    h_nb = _shift_up(hc, _row_or_zero(h_ref, t0 + T, S))
                    nxt = (g[T - 1:T], a[T - 1:T])
                da = g * h_nb
                dbeta = g * (i * xc)
                tb = g * beta
                dla = da * a - dbeta * (a * a * inv_beta)
                dzr = (dla * (-LRU_C * sp_d)) * (r * (1.0 - r))
                dzi = (tb * xc) * (i * (1.0 - i))
                dzrb, dzib, xcb = dzr.astype(BF16), dzi.astype(BF16), xc.astype(BF16)
                dwa_ref[d, 0] += lax.dot_general(xcb, dzrb, TN, preferred_element_type=F32)
                dwx_ref[d, 0] += lax.dot_general(xcb, dzib, TN, preferred_element_type=F32)
                dba_ref[d:d + 1] += jnp.sum(dzr, axis=0, keepdims=True)
                dbx_ref[d:d + 1] += jnp.sum(dzi, axis=0, keepdims=True)
                dlam_ref[d:d + 1] += jnp.sum(dla * (-LRU_C * r), axis=0, keepdims=True)
                dxc = (tb * i + lax.dot_general(dzrb, w_a, NT, preferred_element_type=F32)
                       + lax.dot_general(dzib, w_x, NT, preferred_element_type=F32))
                if d == 0:
                    dxc_v[rows, :] = dxc
                else:
                    dxc_v[rows, :] += dxc
                return nxt

            lax.fori_loop(0, nc, step, (jnp.zeros((1, LANES), F32), jnp.zeros((1, LANES), F32)))

        direction(0)
        direction(1)
        dlam_ref[...] = dlam_ref[...] * (-1.0 / (1.0 + jnp.exp(lam_v)))

        def conv_step(c, carry):
            t0 = pl.multiple_of(c * T, T)
            rows = pl.ds(t0, T)
            g0 = dxc_v[rows, :]
            gm1 = _shift_down(g0, _row_or_zero(dxc_v, t0 - 1, S))
            gm2 = _shift_down(gm1, _row_or_zero(dxc_v, t0 - 2, S))
            gp1 = _shift_up(g0, _row_or_zero(dxc_v, t0 + T, S))
            dxb_ref[rows, :] = (cw[0:1] * gp1 + cw[1:2] * g0 + cw[2:3] * gm1 + cw[3:4] * gm2).astype(BF16)
            xm1, x0, xp1, xp2 = _conv_taps(xb_ref, t0, S)
            for tap, xs in enumerate((xm1, x0, xp1, xp2)):
                dcw_ref[tap:tap + 1] += jnp.sum(g0 * xs, axis=0, keepdims=True)
            dcb_ref[...] += jnp.sum(g0, axis=0, keepdims=True)
            return carry

        lax.fori_loop(0, nc, conv_step, 0)

    col, small, wblk = _lru_specs(S)
    colo = lambda: pl.BlockSpec((S, LANES), lambda n: (0, n), pipeline_mode=pl.Buffered(1))
    return pl.pallas_call(
        body, grid=(LRU_BLOCKS,),
        in_specs=[col(0), col(LRU_BLOCKS), col(0), col(0), col(0), small(4), small(1), wblk, small(2), wblk, small(2), small(2)],
        out_specs=[colo(), colo(), small(4), small(1), wblk, small(2), wblk, small(2), small(2)],
        out_shape=[jax.ShapeDtypeStruct((S, D_MODEL), BF16), jax.ShapeDtypeStruct((S, D_MODEL), BF16),
                   jax.ShapeDtypeStruct((4, D_MODEL), F32), jax.ShapeDtypeStruct((1, D_MODEL), F32),
                   jax.ShapeDtypeStruct((2, LRU_BLOCKS, LANES, LANES), F32), jax.ShapeDtypeStruct((2, D_MODEL), F32),
                   jax.ShapeDtypeStruct((2, LRU_BLOCKS, LANES, LANES), F32), jax.ShapeDtypeStruct((2, D_MODEL), F32),
                   jax.ShapeDtypeStruct((2, D_MODEL), F32)],
        scratch_shapes=[pltpu.VMEM((S, LANES), F32), pltpu.VMEM((S, LANES), F32), pltpu.VMEM((S, LANES), F32)],
        name=name, compiler_params=_cp(("parallel",)))(p, p, hf, hr, dcat, conv_w, conv_b, wa, ba, wx, bx, lam)


PK_UP, PK_DOWN, PK_KV, PK_OUT, PK_IN = 0, 1024, 2048, 2304, 2688
PK_ROWS = {0: PK_IN, 1: PK_IN + 640}
SMALL_G_ROWS = 192
PKF_KV, PKF_SMALL = 512, 768
PKF_ROWS = PKF_SMALL + SMALL_G_ROWS


def _mlp_bwd(x, dx, dxb, saved, w_up, w_down, gain, l, rider=None, next_rider=None):
    up, act, h = saved
    pk = _mm_tn(act, dxb, 1, name=f"dw_down{l}", packed=(None, PK_ROWS[l], PK_DOWN), rider=rider)
    pk, carried = pk if rider is not None else (pk, None)
    dup = _mm_nt(dxb, w_down, up=up, name=f"d_up{l}")
    rider_up = next_rider(carried) if next_rider is not None else None
    pk = _mm_tn(h, dup, N_CHIPS, name=f"dw_up{l}", packed=(pk, PK_ROWS[l], PK_UP), rider=rider_up)
    pk, carried = pk if rider_up is not None else (pk, carried)
    dx, dxb, g_gain = _mm_nt(dup, w_up, norm_x=x, norm_g=gain, dres=dx, name=f"d_mlp_in{l}")
    return dx, dxb, pk, g_gain, carried


def _reduce_first(pk, place, tag, recv):
    return _sum_halves(pk, recv, place, name=f"sum_halves{tag}", tr=pk.shape[1] // 4)


def _sum_parts(parts, place, tag):
    return _sum_chips(parts, place, name=f"sum_chips{tag}", tr=parts.shape[1] // 2)


def _local_step(x, mem, positions, target, W, pending=None, place=None):
    cos_t, sin_t = _rope_tables(positions)
    sinks = W["attn_sinks"].reshape(ATTN_HEADS)
    G = {}

    def hosting(late, fn, *args, **kw):
        if pending is None:
            return fn(*args, **kw)
        *res, buf = fn(*args, gather=pending[late], **kw)
        if late.startswith("w_down"):
            W.setdefault("w_down", [None] * DEPTH)[int(late[-1])] = _ready(late, buf)
        elif late == "w_up":
            W["w_up"], W["w_mem_kv"] = _ready(late, buf)
        else:
            W[late] = _ready(late, buf)
        return res if len(res) > 1 else res[0]

    p0, h0 = hosting("w_out", _mm_nn, x, W["attn_w_in"], norm_g=W["mix_norm"][0], name="attn_in")
    q, kd, vd, va = hosting("w_down0", _qk_prep, p0, cos_t, sin_t)
    ao, lse = hosting("w_up", _attn_fwd, q, kd, va, sinks)
    kv0, memn = _mm_nn(mem, W["w_mem_kv"][0], norm_g=W["mem_norm"], out_dtype=BF16, name="mem_kv0", tm=256)
    kv1 = _mm_nn(memn, W["w_mem_kv"][1], out_dtype=BF16, name="mem_kv1", tm=256)
    cat0 = _memattn_fwd(p0, Q_W // MEM_W + 1, kv0, ao, name="memattn_fwd0")
    x1 = hosting("lru_w_in", _mm_nn, cat0, W["w_out"][0], resid=x, name="mix_out0")
    up0, act0, h1 = hosting("w_down1", _mm_nn, x1, W["w_up"][0], norm_g=W["mlp_norm"][0], relu2=True, name="mlp_up0")
    x2, mlp0 = _mm_nn(act0, W["w_down"][0], resid=x1, name="mlp_down0"), (up0, act0, h1)
    p1, h2 = _mm_nn(x2, W["lru_w_in"], norm_g=W["mix_norm"][1], name="lru_in")
    lru_w = (W["lru_conv_w"], W["lru_conv_b"], W["lru_wa"], W["lru_ba"], W["lru_wx"], W["lru_bx"], W["lru_lambda"])
    y, hf, hr = _lru_fwd(p1, *lru_w)
    cat1 = _memattn_fwd(p1, 2 * D_MODEL // MEM_W, kv1, y, name="memattn_fwd1")
    x3 = _mm_nn(cat1, W["w_out"][1], resid=x2, name="mix_out1")
    mlp1 = _mm_nn(x3, W["w_up"][1], norm_g=W["mlp_norm"][1], relu2=True, name="mlp_up1")
    loss, dx, dxb, G["final_norm"] = _final(mlp1[1], W["w_down"][1], x3, W["final_norm"], target)

    def put(pk, off, g):
        return pk.at[:, off:off + g.size // (N_CHIPS * ROW)].set(g.reshape(N_CHIPS, -1, ROW))

    dx, dxb, pk1, gm1, _ = _mlp_bwd(x3, dx, dxb, mlp1, W["w_up"][1], W["w_down"][1], W["mlp_norm"][1], 1)
    pk1 = _mm_tn(cat1, dxb, 1, name="dw_out1", tk=384, packed=(pk1, PK_ROWS[1], PK_OUT))
    dcat1 = _mm_nt(dxb, W["w_out"][1], name="d_mix1")
    dmq1, dkv1 = _memattn_bwd(p1, 2 * D_MODEL // MEM_W, kv1, dcat1, name="memattn_bwd1")
    dkv1b = dkv1.astype(BF16)
    pk1 = _mm_tn(memn, dkv1b, 1, name="dw_kv1", tm=256, tk=256, packed=(pk1, PK_ROWS[1], PK_KV))
    (dxb1, dgate, G["lru_conv_w"], G["lru_conv_b"], G["lru_wa"], G["lru_ba"], G["lru_wx"], G["lru_bx"],
     G["lru_lambda"]) = _lru_bwd(p1, hf, hr, dcat1, *lru_w)
    dp1 = jnp.concatenate([dxb1, dgate, dmq1], axis=1)
    pk1 = put(pk1, PK_IN, _mm_tn(h2, dp1, N_CHIPS, name="dw_lru_in"))
    dx, dxb, gx1 = _mm_nt(dp1, W["lru_w_in"], norm_x=x2, norm_g=W["mix_norm"][1], dres=dx, name="d_lru_in")
    dist = place is not None
    h1_rows = PK_ROWS[1] // 4
    kept = {}

    def first_half(recv1):
        kept["halves1"], landing1 = _reduce_first(pk1, place, "1", recv1)
        return _exchange_rider((kept["halves1"], landing1, 0, h1_rows))

    dx, dxb, pk0, gm0, landing1 = _mlp_bwd(x1, dx, dxb, mlp0, W["w_up"][0], W["w_down"][0], W["mlp_norm"][0], 0,
                                           rider=_sib_exchange_rider(pk1) if dist else None,
                                           next_rider=first_half if dist else None)
    pk0 = _mm_tn(cat0, dxb, 1, name="dw_out0", tk=384, packed=(pk0, PK_ROWS[0], PK_OUT))
    pk0 = pk0.at[:, PK_KV:PK_OUT].set(0.0)
    dcat0 = _mm_nt(dxb, W["w_out"][0], name="d_mix0")
    dmq0, dkv0, *recv0 = _memattn_bwd(p0, Q_W // MEM_W + 1, kv0, dcat0, name="memattn_bwd0",
                                      rider=_sib_exchange_rider(pk0) if dist else None)
    dkv0b = dkv0.astype(BF16)
    g_kv0 = _mm_tn(memn, dkv0b, 1, name="dw_kv0", tm=256, tk=256)
    rider = None
    if dist:
        halves0, landing0 = _reduce_first(pk0, place, "0", recv0[0])
        rider = _exchange_rider((kept["halves1"], landing1, h1_rows, h1_rows), (halves0, landing0, 0, halves0.shape[1]))
    dq, dk, dv, dsink, *parts = _attn_bwd(q, kd, vd, cat0, lse, sinks, dcat0, rider=rider)
    dp0 = _qk_prep_bwd(dq, dk, dv, dmq0, cos_t, sin_t)
    g_in = _mm_tn(h0, dp0, N_CHIPS, name="dw_attn_in",
                  rider=_sib_allgather_rider(_sum_parts(parts[0], place, "1"), _sum_parts(parts[1], place, "0")) if dist else None)
    if dist:
        g_in, pk1, pk0 = g_in
    dx, _, gx0 = _mm_nt(dp0, W["attn_w_in"], norm_x=x, norm_g=W["mix_norm"][0], dres=dx, name="d_attn_in")

    w_kv_both = jnp.concatenate([W["w_mem_kv"][0], W["w_mem_kv"][1]], axis=0)
    _, _, G["mem_norm"] = _mm_nt(jnp.concatenate([dkv0b, dkv1b], axis=1), w_kv_both, norm_x=mem, norm_g=W["mem_norm"],
                                 name="d_mem", tm=256)

    G["mix_norm"] = jnp.concatenate([gx0, gx1], axis=0)
    G["mlp_norm"] = jnp.concatenate([gm0, gm1], axis=0)
    G["attn_sinks"] = dsink[0:1, 0:ATTN_HEADS]
    small = _flat_pad(_small_grad_list(G), N_CHIPS * SMALL_G_ROWS * ROW).reshape(N_CHIPS, SMALL_G_ROWS, ROW)
    pkf = jnp.concatenate([g_in.reshape(N_CHIPS, PKF_KV, ROW), g_kv0.reshape(N_CHIPS, PKF_SMALL - PKF_KV, ROW), small], axis=1)
    return loss[0, 0], dx, G, pkf, pk0, pk1


def _comm_call(body, out_shape, n_sems, name, *args, alias=None):
    return pl.pallas_call(
        body, out_shape=out_shape, in_specs=[HBM] * len(args), out_specs=HBM,
        scratch_shapes=[pltpu.SemaphoreType.DMA((n_sems,)), pltpu.SemaphoreType.DMA((n_sems,))],
        input_output_aliases=alias or {}, name=name)(*args)


def _place_slot(shard, slot, n_slots, *, name, tr):
    R, C = shard.shape

    def body(s_ref, a_ref, o_ref):
        o_ref[0] = a_ref[...]

    return pl.pallas_call(
        body,
        grid_spec=pltpu.PrefetchScalarGridSpec(
            num_scalar_prefetch=1, grid=(R // tr,), in_specs=[pl.BlockSpec((tr, C), lambda i, s_ref: (i, 0))],
            out_specs=pl.BlockSpec((1, tr, C), lambda i, s_ref: (s_ref[0], i, 0))),
        out_shape=jax.ShapeDtypeStruct((n_slots, R, C), shard.dtype), name=name,
        compiler_params=_cp(("parallel",)))(slot, shard)


def _allgather_chips(buf, *, name):
    def body(b_ref, o_ref, send_sems, recv_sems):
        _gather_start(o_ref, send_sems, recv_sems)
        _gather_finish(o_ref, send_sems, recv_sems)

    return _comm_call(body, jax.ShapeDtypeStruct(buf.shape, buf.dtype), GATHER_SEMS, name, buf, alias={0: 0})


def _sum_halves(g, recv, place, *, name="sum_halves", tr=480):
    _, R, C = g.shape
    half = R // 2
    nblk = half // tr

    def body(pl_ref, g_ref, r_ref, o_ref, own_ref):
        v = (g_ref[...] + r_ref[...]).astype(BF16)
        o_ref[...] = v

        @pl.when(pl.program_id(1) == pl_ref[1])
        def _():
            own_ref[...] = v

    blk = pl.BlockSpec((1, tr, C), lambda i, s, p: (s, i, 0))
    return pl.pallas_call(
        body,
        grid_spec=pltpu.PrefetchScalarGridSpec(
            num_scalar_prefetch=1, grid=(nblk, N_CHIPS),
            in_specs=[pl.BlockSpec((1, tr, C), lambda i, s, p: (s, p[0] * nblk + i, 0)), blk],
            out_specs=[blk, pl.BlockSpec((1, tr, C), lambda i, s, p: (p[1], i, 0))]),
        out_shape=[jax.ShapeDtypeStruct((N_CHIPS, half, C), BF16)] * 2, name=name,
        compiler_params=_cp(("parallel", "arbitrary")))(place, g, recv)


def _sum_chips(parts, place, *, name="sum_chips", tr=480):
    _, R, C = parts.shape
    nblk = R // tr

    def body(pl_ref, p_ref, o_ref):
        acc = p_ref[0].astype(F32) + p_ref[1].astype(F32)
        o_ref[...] = (acc + p_ref[2].astype(F32)) + p_ref[3].astype(F32)

    return pl.pallas_call(
        body,
        grid_spec=pltpu.PrefetchScalarGridSpec(
            num_scalar_prefetch=1, grid=(nblk,), in_specs=[pl.BlockSpec((N_CHIPS, tr, C), lambda i, p: (0, i, 0))],
            out_specs=pl.BlockSpec((tr, C), lambda i, p: (p[0] * nblk + i, 0))),
        out_shape=jax.ShapeDtypeStruct((2 * R, C), F32), name=name, compiler_params=_cp(("parallel",)))(place, parts)


def _adamw(w, g, m, v, *, name, tr=128, rider=None):
    R, C = w.shape
    bc1 = 1.0 - ADAM_B1 ** ADAM_STEP
    bc2 = 1.0 - ADAM_B2 ** ADAM_STEP
    host = _Hosted(rider, 4, 3)

    def body(*refs):
        ins, outs, _, rrefs = host.split(refs, 4, 3)
        host.run(rrefs, pl.program_id(0), R // tr, lambda: inner(*ins, *outs))

    def inner(w_ref, g_ref, m_ref, v_ref, d_ref, nm_ref, nv_ref):
        gv = g_ref[...]
        nm = ADAM_B1 * m_ref[...] + (1.0 - ADAM_B1) * gv
        nv = ADAM_B2 * v_ref[...] + (1.0 - ADAM_B2) * (gv * gv)
        d_ref[...] = -ADAM_LR * ((nm / bc1) / (_sqrt(nv / bc2) + ADAM_EPS) + ADAM_WD * w_ref[...])
        nm_ref[...] = nm
        nv_ref[...] = nv

    blk = pl.BlockSpec((tr, C), lambda i: (i, 0))
    return pl.pallas_call(
        body, grid=(R // tr,), in_specs=[blk] * 4 + host.in_specs, out_specs=[blk] * 3 + host.out_specs,
        out_shape=[jax.ShapeDtypeStruct((R, C), F32)] * 3 + host.out_shape, scratch_shapes=host.scratch,
        input_output_aliases=host.alias, name=name,
        compiler_params=_cp(("arbitrary",) if host.on else ("parallel",)))(w, g, m, v, *host.args)


ROW = 1024
BIG = ("w_mem_kv", "w_out", "w_up", "w_down", "attn_w_in", "lru_w_in")
SMALL_SHARDED = ("lru_conv_w", "lru_conv_b", "lru_ba", "lru_bx", "lru_lambda")
REPLICATED = ("mix_norm", "mlp_norm", "mem_norm", "final_norm", "attn_sinks", "lru_wa", "lru_wx")
SMALL = REPLICATED + SMALL_SHARDED
WEIGHTS = ("mix_norm", "mlp_norm", "mem_norm", "final_norm", "w_mem_kv", "w_out", "w_up", "w_down", "attn_w_in",
           "attn_sinks", "lru_w_in", "lru_conv_w", "lru_conv_b", "lru_wa", "lru_ba", "lru_wx", "lru_bx", "lru_lambda")
SMALL_W_ROWS = 32
ADAM_SMALL_ROWS = 640


def _rows(a):
    return a.reshape(-1, ROW)


def _flat_pad(parts, total):
    flat = jnp.concatenate([p.reshape(-1) for p in parts])
    return jnp.pad(flat, (0, total - flat.shape[0]))


def _pad_rows(a):
    flat = a.reshape(-1)
    n = -(-flat.shape[0] // ROW) * ROW
    return jnp.pad(flat, (0, n - flat.shape[0])).reshape(-1, ROW)


LATE = ("w_out", "w_down0", "w_up", "lru_w_in", "w_down1")


def _ready(name, full):
    if name == "w_out":
        wo = full.reshape(N_CHIPS, DEPTH, -1, D_MODEL)
        return [wo[:, l].reshape(1, MIX_OUT_W, D_MODEL) for l in range(DEPTH)]
    if name == "w_up":
        n_up = DEPTH * D_MODEL
        wu = full[:, :n_up].reshape(N_CHIPS, DEPTH, D_MODEL, D_FF // N_CHIPS)
        kv = full[:, n_up:].reshape(N_CHIPS, DEPTH, -1, D_MODEL)
        return [wu[:, l] for l in range(DEPTH)], [kv[:, l].reshape(1, D_MODEL, D_MODEL) for l in range(DEPTH)]
    if name == "lru_w_in":
        return full.reshape(N_CHIPS, D_MODEL, LRU_IN_W // N_CHIPS)
    return full.reshape(1, D_FF, D_MODEL)


def _gather_weights(P, chip1):
    bf = lambda a: _rows(a.astype(BF16))
    small = _flat_pad([P[n] for n in SMALL_SHARDED], SMALL_W_ROWS * ROW // 2)
    small_bits = lax.bitcast_convert_type(small, BF16).reshape(SMALL_W_ROWS, ROW)
    early = jnp.concatenate([bf(P["attn_w_in"]), small_bits], axis=0)
    n_in = P["attn_w_in"].size // ROW
    placed = _place_slot(early, chip1, N_CHIPS, name="place_weights", tr=early.shape[0] // 2)
    full = _allgather_chips(placed, name="allgather_weights")
    late = {"w_out": bf(P["w_out"]), "lru_w_in": bf(P["lru_w_in"]),
            "w_up": jnp.concatenate([bf(P["w_up"]), bf(P["w_mem_kv"])], axis=0),
            "w_down0": bf(P["w_down"][0]), "w_down1": bf(P["w_down"][1])}
    pending = {n: _place_slot(late[n], chip1, N_CHIPS, name=f"place_{n}", tr=late[n].shape[0] // 2) for n in LATE}
    W = {n: P[n] for n in REPLICATED}
    W["attn_w_in"] = full[:, :n_in].reshape(N_CHIPS, D_MODEL, ATTN_IN_W // N_CHIPS)
    sm = lax.bitcast_convert_type(full[:, n_in:].reshape(N_CHIPS, -1, 2), F32)
    o = 0
    for n in SMALL_SHARDED:
        shp = P[n].shape[1:]
        cnt = math.prod(shp)
        piece = sm[:, o:o + cnt].reshape((N_CHIPS,) + shp)
        piece = jnp.moveaxis(piece, 0, -2)
        W[n] = piece.reshape(shp[:-1] + (N_CHIPS * shp[-1],)).reshape(-1, D_MODEL)
        o += cnt
    W["lru_wa"] = P["lru_wa"][0].astype(BF16)
    W["lru_wx"] = P["lru_wx"][0].astype(BF16)
    return W, pending


def _small_grad_list(G):
    return [G["mix_norm"], G["mlp_norm"], G["mem_norm"], G["final_norm"], jnp.pad(G["attn_sinks"].reshape(-1), (0, ROW - ATTN_HEADS)),
            G["lru_wa"], G["lru_wx"], G["lru_conv_w"], G["lru_conv_b"], G["lru_ba"], G["lru_bx"], G["lru_lambda"]]


SMALL_G_SIZES = (2 * D_MODEL, 2 * D_MODEL, D_MODEL, D_MODEL, ROW, 2 * 8 * 128 * 128, 2 * 8 * 128 * 128,
                 4 * D_MODEL, D_MODEL, 2 * D_MODEL, 2 * D_MODEL, 2 * D_MODEL)


def _finish_grads(pkf, full0, full1, place, chip1, adamw):
    both = lambda off, r: jnp.concatenate([full0[off:off + r], full1[off:off + r]], axis=0)
    recvf = adamw("w_up", both(PK_UP, 1024), _sib_exchange_rider(pkf))
    halvesf, landingf = _reduce_first(pkf, place, "f", recvf)
    partsf = adamw("w_down", both(PK_DOWN, 1024), _exchange_rider((halvesf, landingf, 0, halvesf.shape[1])))
    fullf = adamw("w_out", both(PK_OUT, 384), _sib_allgather_rider(_sum_parts(partsf, place, "f")))
    small_placed = _place_slot(fullf[PKF_SMALL:], chip1, N_CHIPS, name="place_small_grads", tr=SMALL_G_ROWS)
    small_all = adamw("lru_w_in", full1[PK_IN:PK_IN + 640], _gather_rider(small_placed))
    adamw("w_mem_kv", jnp.concatenate([fullf[PKF_KV:PKF_SMALL], full1[PK_KV:PK_OUT]], axis=0), None)
    adamw("attn_w_in", fullf[:PKF_KV], None)
    flat = small_all.reshape(-1)
    small = {}
    o = 0
    names = ("mix_norm", "mlp_norm", "mem_norm", "final_norm", "attn_sinks", "lru_wa", "lru_wx",
             "lru_conv_w", "lru_conv_b", "lru_ba", "lru_bx", "lru_lambda")
    for n, cnt in zip(names, SMALL_G_SIZES):
        small[n] = flat[o:o + cnt]
        o += cnt
    return small


def kernel(x, mem, positions, mix_norm, mlp_norm, mem_norm, final_norm, w_mem_kv, w_out, w_up, w_down, attn_w_in, attn_sinks, lru_w_in, lru_conv_w, lru_conv_b, lru_wa, lru_ba, lru_wx, lru_bx, lru_lambda, loss_target, m_mix_norm, m_mlp_norm, m_mem_norm, m_final_norm, m_w_mem_kv, m_w_out, m_w_up, m_w_down, m_attn_w_in, m_attn_sinks, m_lru_w_in, m_lru_conv_w, m_lru_conv_b, m_lru_wa, m_lru_ba, m_lru_wx, m_lru_bx, m_lru_lambda, v_mix_norm, v_mlp_norm, v_mem_norm, v_final_norm, v_w_mem_kv, v_w_out, v_w_up, v_w_down, v_attn_w_in, v_attn_sinks, v_lru_w_in, v_lru_conv_w, v_lru_conv_b, v_lru_wa, v_lru_ba, v_lru_wx, v_lru_bx, v_lru_lambda):
    P = dict(mix_norm=mix_norm, mlp_norm=mlp_norm, mem_norm=mem_norm, final_norm=final_norm, w_mem_kv=w_mem_kv, w_out=w_out,
             w_up=w_up, w_down=w_down, attn_w_in=attn_w_in, attn_sinks=attn_sinks, lru_w_in=lru_w_in, lru_conv_w=lru_conv_w,
             lru_conv_b=lru_conv_b, lru_wa=lru_wa, lru_ba=lru_ba, lru_wx=lru_wx, lru_bx=lru_bx, lru_lambda=lru_lambda)
    M1 = dict(mix_norm=m_mix_norm, mlp_norm=m_mlp_norm, mem_norm=m_mem_norm, final_norm=m_final_norm, w_mem_kv=m_w_mem_kv,
              w_out=m_w_out, w_up=m_w_up, w_down=m_w_down, attn_w_in=m_attn_w_in, attn_sinks=m_attn_sinks, lru_w_in=m_lru_w_in,
              lru_conv_w=m_lru_conv_w, lru_conv_b=m_lru_conv_b, lru_wa=m_lru_wa, lru_ba=m_lru_ba, lru_wx=m_lru_wx,
              lru_bx=m_lru_bx, lru_lambda=m_lru_lambda)
    V2 = dict(mix_norm=v_mix_norm, mlp_norm=v_mlp_norm, mem_norm=v_mem_norm, final_norm=v_final_norm, w_mem_kv=v_w_mem_kv,
              w_out=v_w_out, w_up=v_w_up, w_down=v_w_down, attn_w_in=v_attn_w_in, attn_sinks=v_attn_sinks, lru_w_in=v_lru_w_in,
              lru_conv_w=v_lru_conv_w, lru_conv_b=v_lru_conv_b, lru_wa=v_lru_wa, lru_ba=v_lru_ba, lru_wx=v_lru_wx,
              lru_bx=v_lru_bx, lru_lambda=v_lru_lambda)
    chip = 2 * lax.axis_index("x") + lax.axis_index("y")
    chip1 = chip.astype(jnp.int32).reshape(1)
    place = jnp.stack([lax.axis_index("c").astype(jnp.int32), chip.astype(jnp.int32)])

    W, pending = _gather_weights(P, chip1)
    loss, dx, _, pkf, full0, full1 = _local_step(x[0], mem[0], positions[0], loss_target[0], W, pending, place)
    loss = lax.psum(loss, ("x", "y", "c"))
    grads, deltas, new_m, new_v = {}, {}, {}, {}

    def adamw_big(n, g, rider):
        d, nm, nv, *carried = _adamw(_rows(P[n]), g, _rows(M1[n]), _rows(V2[n]), name=f"adamw_{n}", rider=rider)
        grads[n], deltas[n], new_m[n], new_v[n] = (t.reshape(P[n].shape) for t in (g, d, nm, nv))
        return carried[0] if carried else None

    small = _finish_grads(pkf, full0, full1, place, chip1, adamw_big)

    for n in SMALL:
        g = small[n]
        if n in SMALL_SHARDED:
            shard = P[n].shape[-1]
            g = lax.dynamic_slice_in_dim(g.reshape(-1, N_CHIPS * shard), chip * shard, shard, axis=1)
        elif n == "attn_sinks":
            g = g[:ATTN_HEADS]
        grads[n] = g.reshape(P[n].shape)
    packs = []
    for src in (P, grads, M1, V2):
        a = jnp.concatenate([_pad_rows(src[n]) for n in SMALL], axis=0)
        packs.append(jnp.pad(a, ((0, ADAM_SMALL_ROWS - a.shape[0]), (0, 0))))
    d_s, nm_s, nv_s = _adamw(*packs, name="adamw_small")
    o = 0
    for n in SMALL:
        cnt = math.prod(P[n].shape)
        r = -(-cnt // ROW)
        for dst, src in ((deltas, d_s), (new_m, nm_s), (new_v, nv_s)):
            dst[n] = src[o:o + r].reshape(-1)[:cnt].reshape(P[n].shape)
        o += r

    return (loss, dx[None], *[grads[n] for n in WEIGHTS], *[deltas[n] for n in WEIGHTS],
            *[new_m[n] for n in WEIGHTS], *[new_v[n] for n in WEIGHTS])
```

```python
import math

import jax
import jax.numpy as jnp
from jax import lax
from jax.experimental import pallas as pl
from jax.experimental.pallas import tpu as pltpu

F32 = jnp.float32
BF16 = jnp.bfloat16
MESH = pl.DeviceIdType.MESH

D_MODEL = 1024
DEPTH = 2
EPS = 1e-6
ATTN_HEADS = 16
ATTN_KV_HEADS = 4
HEAD_DIM = 64
WINDOW = 128
BLOCK = 128
ROPE_THETA = 500000.0
ROPE_DIM = 16
Q_W = 1024
KV_W = 256
MEM_LEN = 256
MEM_HEADS = 4
MEM_HEAD_DIM = 128
MEM_W = 512
LRU_BLOCKS = 8
LRU_C = 8.0
ATTN_IN_W = 2048
LRU_IN_W = 2560
MIX_OUT_W = 1536
D_FF = 4096
NEG = -1e30
N_CHIPS = 4

ADAM_LR = 0.001
ADAM_B1 = 0.9
ADAM_B2 = 0.999
ADAM_EPS = 1e-08
ADAM_WD = 0.01
ADAM_STEP = 10

LANES = 128
SCAN_ROWS = 1024
VMEM_LIMIT = 56 * 1024 * 1024

NT = (((1,), (1,)), ((), ()))
TN = (((0,), (0,)), ((), ()))


def _cp(sem=None):
    return pltpu.CompilerParams(dimension_semantics=sem, vmem_limit_bytes=VMEM_LIMIT)


HBM = pl.BlockSpec(memory_space=pl.ANY)
GATHER_SEMS = 6


def _place():
    x, y, c = lax.axis_index("x"), lax.axis_index("y"), lax.axis_index("c")
    chips = [(1 - x, y), (x, 1 - y), (1 - x, 1 - y)]
    return x, y, c, chips


def _remote(src, dst, send_sems, recv_sems, k, to):
    return pltpu.make_async_remote_copy(src_ref=src, dst_ref=dst, send_sem=send_sems.at[k], recv_sem=recv_sems.at[k],
                                        device_id=to, device_id_type=MESH)


def _gather_start(o_ref, send_sems, recv_sems):
    x, y, c, chips = _place()
    half = o_ref.shape[1] // 2
    own = o_ref.at[2 * x + y, pl.ds(pl.multiple_of(c * half, 16), half)]
    for j, (cx, cy) in enumerate(chips):
        _remote(own, own, send_sems, recv_sems, j, (cx, cy, c)).start()


def _gather_forward(o_ref, send_sems, recv_sems):
    x, y, c, chips = _place()
    half = o_ref.shape[1] // 2
    my_rows = pl.ds(pl.multiple_of(c * half, 16), half)
    for j, (cx, cy) in enumerate(chips):
        landed = o_ref.at[2 * cx + cy, my_rows]
        _remote(landed, landed, send_sems, recv_sems, j, (cx, cy, c)).wait_recv()
        _remote(landed, landed, send_sems, recv_sems, 3 + j, (x, y, 1 - c)).start()


def _gather_drain(o_ref, send_sems, recv_sems):
    x, y, c, chips = _place()
    half = o_ref.shape[1] // 2
    my_rows = pl.ds(pl.multiple_of(c * half, 16), half)
    sib_rows = pl.ds(pl.multiple_of((1 - c) * half, 16), half)
    own = o_ref.at[2 * x + y, my_rows]
    for j, (cx, cy) in enumerate(chips):
        got = o_ref.at[2 * cx + cy, sib_rows]
        _remote(got, got, send_sems, recv_sems, 3 + j, (x, y, 1 - c)).wait_recv()
    for j, (cx, cy) in enumerate(chips):
        _remote(own, own, send_sems, recv_sems, j, (cx, cy, c)).wait_send()
        landed = o_ref.at[2 * cx + cy, my_rows]
        _remote(landed, landed, send_sems, recv_sems, 3 + j, (x, y, 1 - c)).wait_send()


def _gather_finish(o_ref, send_sems, recv_sems):
    _gather_forward(o_ref, send_sems, recv_sems)
    _gather_drain(o_ref, send_sems, recv_sems)


def _exchange_start(h_ref, o_ref, send_sems, recv_sems, rows=None, base=0):
    x, y, c, chips = _place()
    rows = pl.ds(0, h_ref.shape[1]) if rows is None else rows
    for j, (cx, cy) in enumerate(chips):
        _remote(h_ref.at[2 * cx + cy, rows], o_ref.at[2 * x + y, rows], send_sems, recv_sems, base + j, (cx, cy, c)).start()


def _exchange_finish(h_ref, o_ref, send_sems, recv_sems, rows=None, base=0):
    x, y, c, chips = _place()
    rows = pl.ds(0, h_ref.shape[1]) if rows is None else rows
    for j, (cx, cy) in enumerate(chips):
        got = o_ref.at[2 * cx + cy, rows]
        _remote(got, got, send_sems, recv_sems, base + j, (cx, cy, c)).wait_recv()
    for j, (cx, cy) in enumerate(chips):
        _remote(h_ref.at[2 * cx + cy, rows], o_ref.at[2 * x + y, rows], send_sems, recv_sems, base + j, (cx, cy, c)).wait_send()


def _sib_exchange_copies(g_ref, o_ref, send_sems, recv_sems):
    x, y, c, _ = _place()
    half = g_ref.shape[1] // 2
    other = pl.ds(pl.multiple_of((1 - c) * half, 8), half)
    return [_remote(g_ref.at[s, other], o_ref.at[s], send_sems, recv_sems, s, (x, y, 1 - c)) for s in range(N_CHIPS)]


def _sib_exchange_start(*refs):
    for cp in _sib_exchange_copies(*refs):
        cp.start()


def _sib_exchange_finish(*refs):
    for cp in _sib_exchange_copies(*refs):
        cp.wait()


def _sib_allgather_start(*refs):
    *o_refs, send_sems, recv_sems = refs
    x, y, c, _ = _place()
    for i, o_ref in enumerate(o_refs):
        half = o_ref.shape[0] // 2
        mine = o_ref.at[pl.ds(pl.multiple_of(c * half, 8), half)]
        _remote(mine, mine, send_sems, recv_sems, i, (x, y, 1 - c)).start()


def _sib_allgather_finish(*refs):
    *o_refs, send_sems, recv_sems = refs
    x, y, c, _ = _place()
    for i, o_ref in enumerate(o_refs):
        half = o_ref.shape[0] // 2
        mine = o_ref.at[pl.ds(pl.multiple_of(c * half, 8), half)]
        got = o_ref.at[pl.ds(pl.multiple_of((1 - c) * half, 8), half)]
        _remote(got, got, send_sems, recv_sems, i, (x, y, 1 - c)).wait_recv()
        _remote(mine, mine, send_sems, recv_sems, i, (x, y, 1 - c)).wait_send()


class _Rider:
    def __init__(self, args, start, finish, inplace=1, mid=None):
        self.args, self.start, self.finish, self.inplace, self.mid = list(args), start, finish, inplace, mid


def _gather_rider(buf):
    return None if buf is None else _Rider([buf], _gather_start, _gather_finish, mid=(_gather_forward, _gather_drain))


def _exchange_rider(*parts):
    n = len(parts)
    assert 3 * n <= GATHER_SEMS

    def run(fn):
        def go(*refs):
            sems = refs[2 * n:]
            for i, (_, _, r0, nr) in enumerate(parts):
                fn(refs[i], refs[n + i], *sems, rows=pl.ds(r0, nr), base=3 * i)
        return go

    return _Rider([p[0] for p in parts] + [p[1] for p in parts], run(_exchange_start), run(_exchange_finish), inplace=n)


def _sib_exchange_rider(g):
    landing = lax.empty((N_CHIPS, g.shape[1] // 2, g.shape[2]), g.dtype)
    return _Rider([g, landing], _sib_exchange_start, _sib_exchange_finish)


def _sib_allgather_rider(*fulls):
    return _Rider(fulls, _sib_allgather_start, _sib_allgather_finish, inplace=len(fulls))


class _Hosted:
    def __init__(self, rider, n_in, n_out):
        self.rider = rider
        self.on = rider is not None
        self.args = rider.args if self.on else []
        k = len(self.args)
        p = self.p = rider.inplace if self.on else 0
        self.alias = {n_in + k - p + i: n_out + i for i in range(p)}
        self.in_specs = [HBM] * k
        self.out_specs = [HBM] * p
        self.out_shape = [jax.ShapeDtypeStruct(a.shape, a.dtype) for a in self.args[k - p:]]
        self.scratch = [pltpu.SemaphoreType.DMA((GATHER_SEMS,)), pltpu.SemaphoreType.DMA((GATHER_SEMS,))] if self.on else []

    def split(self, refs, n_in, n_out):
        refs = list(refs)
        if not self.on:
            return refs[:n_in], refs[n_in:n_in + n_out], refs[n_in + n_out:], None
        k, p = len(self.args), self.p
        ins, outs = refs[:n_in], refs[n_in + k:n_in + k + n_out]
        rest = refs[n_in + k + n_out + p:]
        rrefs = refs[n_in:n_in + k - p] + refs[n_in + k + n_out:n_in + k + n_out + p] + [rest[-2], rest[-1]]
        return ins, outs, rest[:-2], rrefs

    def run(self, rrefs, step, n_steps, compute):
        if rrefs is None:
            return compute()

        mid = self.rider.mid
        mid_step = (3 * n_steps) // 4
        two_stage = mid is not None and 0 < mid_step < n_steps - 1

        @pl.when(step == 0)
        def _():
            self.rider.start(*rrefs)

        compute()

        if two_stage:
            @pl.when(step == mid_step)
            def _():
                mid[0](*rrefs)

        @pl.when(step == n_steps - 1)
        def _():
            (mid[1] if two_stage else self.rider.finish)(*rrefs)


def _mm_nn(a, w3, *, name, out_dtype=F32, norm_g=None, resid=None, relu2=False, tm=512, gather=None):
    M, K = a.shape
    ns, _, n = w3.shape
    N = ns * n
    tm = min(tm, M)
    has_norm = norm_g is not None
    has_res = resid is not None
    n_in = 2 + has_norm + has_res
    n_out = (2 if relu2 else 1) + has_norm
    host = _Hosted(_gather_rider(gather), n_in, n_out)

    def body(*refs):
        ins, outs, _, gref = host.split(refs, n_in, n_out)
        a_ref, w_ref = ins[0], ins[1]
        g_ref = ins[2] if has_norm else None
        r_ref = ins[-1] if has_res else None

        def compute():
            if has_norm:
                xv = a_ref[...]
                rs = lax.rsqrt(jnp.mean(xv * xv, axis=-1, keepdims=True) + EPS)
                ab = (xv * rs * g_ref[...]).astype(BF16)
                outs[-1][...] = ab
            else:
                ab = a_ref[...]
            for s in range(ns):
                acc = jnp.dot(ab, w_ref[s], preferred_element_type=F32)
                sl = slice(s * n, (s + 1) * n)
                if relu2:
                    outs[0][:, sl] = acc.astype(BF16)
                    rl = jnp.maximum(acc, 0.0)
                    outs[1][:, sl] = (rl * rl).astype(BF16)
                elif has_res:
                    outs[0][:, sl] = r_ref[:, sl] + acc
                else:
                    outs[0][:, sl] = acc.astype(out_dtype)

        host.run(gref, pl.program_id(0), M // tm, compute)

    row = lambda w: pl.BlockSpec((tm, w), lambda i: (i, 0))
    in_specs = [row(K), pl.BlockSpec((ns, K, n), lambda i: (0, 0, 0))]
    args = [a, w3]
    if has_norm:
        in_specs.append(pl.BlockSpec((1, K), lambda i: (0, 0)))
        args.append(norm_g.reshape(1, K))
    if has_res:
        in_specs.append(row(N))
        args.append(resid)
    if relu2:
        out_shape = [jax.ShapeDtypeStruct((M, N), BF16), jax.ShapeDtypeStruct((M, N), BF16)]
        out_specs = [row(N), row(N)]
    else:
        out_shape = [jax.ShapeDtypeStruct((M, N), F32 if has_res else out_dtype)]
        out_specs = [row(N)]
    if has_norm:
        out_shape.append(jax.ShapeDtypeStruct((M, K), BF16))
        out_specs.append(row(K))
    res = pl.pallas_call(body, grid=(M // tm,), in_specs=in_specs + host.in_specs, out_specs=out_specs + host.out_specs,
                         out_shape=out_shape + host.out_shape, scratch_shapes=host.scratch, input_output_aliases=host.alias,
                         name=name, compiler_params=_cp(("arbitrary",) if host.on else ("parallel",)))(*args, *host.args)
    return res if len(res) > 1 else res[0]


def _mm_nt(g, w3, *, name, out_dtype=BF16, up=None, norm_x=None, norm_g=None, dres=None, tm=512):
    M = g.shape[0]
    ns, K, n = w3.shape
    tm = min(tm, M)
    has_up = up is not None
    has_norm = norm_x is not None
    has_res = dres is not None

    def body(*refs):
        refs = list(refs)
        g_ref, w_ref = refs[0], refs[1]
        pos = 2
        if has_up:
            up_ref = refs[pos]
            pos += 1
        if has_norm:
            x_ref, gn_ref = refs[pos], refs[pos + 1]
            pos += 2
        if has_res:
            r_ref = refs[pos]
            pos += 1
        outs = refs[pos:]
        acc = None
        for s in range(ns):
            part = lax.dot_general(g_ref[:, s * n:(s + 1) * n], w_ref[s], NT, preferred_element_type=F32)
            acc = part if acc is None else acc + part
        if has_up:
            outs[0][...] = (acc * (2.0 * jnp.maximum(up_ref[...].astype(F32), 0.0))).astype(BF16)
        elif has_norm:
            xv = x_ref[...]
            rs = lax.rsqrt(jnp.mean(xv * xv, axis=-1, keepdims=True) + EPS)
            xn = xv * rs
            dxn = acc * gn_ref[...]
            dx = rs * (dxn - xn * jnp.mean(dxn * xn, axis=-1, keepdims=True))
            if has_res:
                dx = dx + r_ref[...]
            outs[0][...] = dx
            outs[1][...] = dx.astype(BF16)

            @pl.when(pl.program_id(0) == 0)
            def _():
                outs[2][...] = jnp.zeros_like(outs[2])

            outs[2][...] += jnp.sum(acc * xn, axis=0, keepdims=True)
        else:
            outs[0][...] = acc.astype(out_dtype)

    row = lambda w: pl.BlockSpec((tm, w), lambda i: (i, 0))
    in_specs = [row(ns * n), pl.BlockSpec((ns, K, n), lambda i: (0, 0, 0))]
    args = [g, w3]
    if has_up:
        in_specs.append(row(K))
        args.append(up)
    if has_norm:
        in_specs += [row(K), pl.BlockSpec((1, K), lambda i: (0, 0))]
        args += [norm_x, norm_g.reshape(1, K)]
    if has_res:
        in_specs.append(row(K))
        args.append(dres)
    if has_norm:
        out_shape = [jax.ShapeDtypeStruct((M, K), F32), jax.ShapeDtypeStruct((M, K), BF16),
                     jax.ShapeDtypeStruct((1, K), F32)]
        out_specs = [row(K), row(K), pl.BlockSpec((1, K), lambda i: (0, 0))]
        sem = ("arbitrary",)
    else:
        out_shape = [jax.ShapeDtypeStruct((M, K), BF16 if has_up else out_dtype)]
        out_specs = [row(K)]
        sem = ("parallel",)
    res = pl.pallas_call(body, grid=(M // tm,), in_specs=in_specs, out_specs=out_specs, out_shape=out_shape,
                         name=name, compiler_params=_cp(sem))(*args)
    return res if len(res) > 1 else res[0]


def _mm_tn(a, g, ns, *, name, tk=512, tm=4096, packed=None, rider=None):
    M, K = a.shape
    n = g.shape[1] // ns
    tm = min(tm, M)
    tk = min(tk, K)
    nk, nm = K // tk, M // tm
    n_in = 3 if (packed is not None and packed[0] is not None) else 2
    host = _Hosted(rider, n_in, 1)

    def body(*refs):
        ins, outs, _, rrefs = host.split(refs, n_in, 1)
        a_ref, g_ref, o_ref = ins[0], ins[1], outs[0]

        def compute():
            @pl.when(pl.program_id(2) == 0)
            def _():
                o_ref[...] = jnp.zeros_like(o_ref)

            o_ref[0] += lax.dot_general(a_ref[...], g_ref[...], TN, preferred_element_type=F32)

        step = (pl.program_id(0) * nk + pl.program_id(1)) * nm + pl.program_id(2)
        host.run(rrefs, step, ns * nk * nm, compute)

    in_specs = [pl.BlockSpec((tm, tk), lambda s, k, m: (m, k)), pl.BlockSpec((tm, n), lambda s, k, m: (m, s))]
    args = [a, g]
    alias = {}
    if packed is None:
        out_spec = pl.BlockSpec((1, tk, n), lambda s, k, m: (s, k, 0))
        out_shape = jax.ShapeDtypeStruct((ns, K, n), F32)
    else:
        buf, rows, off = packed
        per_chip = K * ns // N_CHIPS
        assert n == ROW and per_chip % tk == 0 and off % tk == 0
        if ns == N_CHIPS:
            out_spec = pl.BlockSpec((1, tk, n), lambda s, k, m: (s, off // tk + k, 0))
        else:
            kpc = per_chip // tk
            out_spec = pl.BlockSpec((1, tk, n), lambda s, k, m: (k // kpc, off // tk + k % kpc, 0))
        out_shape = jax.ShapeDtypeStruct((N_CHIPS, rows, ROW), F32)
        if buf is not None:
            in_specs.append(HBM)
            args.append(buf)
            alias = {2: 0}
    sem = ("arbitrary",) * 3 if host.on else ("parallel", "parallel", "arbitrary")
    res = pl.pallas_call(
        body, grid=(ns, nk, nm), in_specs=in_specs + host.in_specs, out_specs=[out_spec] + host.out_specs,
        out_shape=[out_shape] + host.out_shape, scratch_shapes=host.scratch, name=name,
        input_output_aliases={**alias, **host.alias}, compiler_params=_cp(sem))(*args, *host.args)
    return res if host.on else res[0]


def _final(act, w_down, x, gain, target, *, name="mlp_down_final", tr=512):
    S, Dm = x.shape
    tr = min(tr, S)
    Kf = act.shape[1]

    def body(a_ref, w_ref, x_ref, g_ref, t_ref, loss_ref, dx_ref, dxb_ref, dg_ref):
        @pl.when(pl.program_id(0) == 0)
        def _():
            loss_ref[...] = jnp.zeros_like(loss_ref)
            dg_ref[...] = jnp.zeros_like(dg_ref)

        xv = x_ref[...] + jnp.dot(a_ref[...], w_ref[0], preferred_element_type=F32)
        gv = g_ref[...]
        rs = lax.rsqrt(jnp.mean(xv * xv, axis=-1, keepdims=True) + EPS)
        xn = xv * rs
        err = xn * gv - t_ref[...]
        loss_ref[...] += 0.5 * jnp.sum(jnp.mean(err * err, axis=-1, keepdims=True), axis=0, keepdims=True)
        dout = err * (1.0 / Dm)
        dg_ref[...] += jnp.sum(dout * xn, axis=0, keepdims=True)
        dxn = dout * gv
        dx = rs * (dxn - xn * jnp.mean(dxn * xn, axis=-1, keepdims=True))
        dx_ref[...] = dx
        dxb_ref[...] = dx.astype(BF16)

    row = pl.BlockSpec((tr, Dm), lambda i: (i, 0))
    return pl.pallas_call(
        body, grid=(S // tr,),
        in_specs=[pl.BlockSpec((tr, Kf), lambda i: (i, 0)), pl.BlockSpec((1, Kf, Dm), lambda i: (0, 0, 0)), row,
                  pl.BlockSpec((1, Dm), lambda i: (0, 0)), row],
        out_specs=[pl.BlockSpec((1, 1), lambda i: (0, 0)), row, row, pl.BlockSpec((1, Dm), lambda i: (0, 0))],
        out_shape=[jax.ShapeDtypeStruct((1, 1), F32), jax.ShapeDtypeStruct((S, Dm), F32),
                   jax.ShapeDtypeStruct((S, Dm), BF16), jax.ShapeDtypeStruct((1, Dm), F32)],
        name=name, compiler_params=_cp(("arbitrary",)))(act, w_down, x, gain.reshape(1, Dm), target)


def _rope_tables(positions):
    half = ROPE_DIM // 2
    inv_freq = ROPE_THETA ** (-2.0 * jnp.arange(half, dtype=F32) / ROPE_DIM)
    ang = positions.astype(F32)[:, None] * inv_freq
    cos, sin = jnp.cos(ang), jnp.sin(ang)
    S = positions.shape[0]
    ones = jnp.ones((S, HEAD_DIM - ROPE_DIM), F32)
    cos64 = jnp.concatenate([cos, cos, ones], axis=1)
    sin64 = jnp.concatenate([-sin, sin, 0.0 * ones], axis=1)
    return jnp.tile(cos64, (1, 2)), jnp.tile(sin64, (1, 2))


def _rope_partner(t):
    lane = lax.broadcasted_iota(jnp.int32, t.shape, 1)
    low = (lane & (HEAD_DIM - 1)) < (ROPE_DIM // 2)
    return jnp.where(low, pltpu.roll(t, LANES - ROPE_DIM // 2, 1), pltpu.roll(t, ROPE_DIM // 2, 1))


def _qk_prep(p, cos_t, sin_t, *, name="qk_prep", tr=256, gather=None):
    S = p.shape[0]
    tr = min(tr, S)
    scale = HEAD_DIM ** -0.5
    host = _Hosted(_gather_rider(gather), 3, 4)

    def body(*refs):
        ins, outs, _, gref = host.split(refs, 3, 4)
        host.run(gref, pl.program_id(0), S // tr, lambda: inner(*ins, *outs))

    def inner(p_ref, c_ref, s_ref, q_ref, k_ref, v_ref, va_ref):
        cs, sn = c_ref[...], s_ref[...]
        lane = lax.broadcasted_iota(jnp.int32, (tr, LANES), 1)
        lo = lane < HEAD_DIM
        for c in range(Q_W // LANES):
            t = p_ref[:, c * LANES:(c + 1) * LANES]
            q_ref[:, c * LANES:(c + 1) * LANES] = ((t * cs + _rope_partner(t) * sn) * scale).astype(BF16)
        for c in range(KV_W // LANES):
            t = p_ref[:, Q_W + c * LANES:Q_W + (c + 1) * LANES]
            kc = t * cs + _rope_partner(t) * sn
            vc = p_ref[:, Q_W + KV_W + c * LANES:Q_W + KV_W + (c + 1) * LANES]
            for arr, ref in ((kc, k_ref), (vc, v_ref)):
                sw = pltpu.roll(arr, HEAD_DIM, 1)
                ref[:, (2 * c) * LANES:(2 * c + 1) * LANES] = jnp.where(lo, arr, sw).astype(BF16)
                ref[:, (2 * c + 1) * LANES:(2 * c + 2) * LANES] = jnp.where(lo, sw, arr).astype(BF16)
            sw = pltpu.roll(vc, HEAD_DIM, 1)
            for k, aug in enumerate((jnp.where(lo, vc, 1.0), jnp.where(lo, 1.0, sw), jnp.where(lo, sw, 1.0), jnp.where(lo, 1.0, vc))):
                va_ref[:, (4 * c + k) * LANES:(4 * c + k + 1) * LANES] = aug.astype(BF16)

    row = lambda w: pl.BlockSpec((tr, w), lambda i: (i, 0))
    return pl.pallas_call(
        body, grid=(S // tr,), in_specs=[row(ATTN_IN_W), row(LANES), row(LANES)] + host.in_specs,
        out_specs=[row(Q_W), row(2 * KV_W), row(2 * KV_W), row(4 * KV_W)] + host.out_specs,
        out_shape=[jax.ShapeDtypeStruct((S, Q_W), BF16), jax.ShapeDtypeStruct((S, 2 * KV_W), BF16),
                   jax.ShapeDtypeStruct((S, 2 * KV_W), BF16), jax.ShapeDtypeStruct((S, 4 * KV_W), BF16)] + host.out_shape,
        scratch_shapes=host.scratch, input_output_aliases=host.alias,
        name=name, compiler_params=_cp(("arbitrary",) if host.on else ("parallel",)))(p, cos_t, sin_t, *host.args)


def _qk_prep_bwd(dq, dk, dv, dmq, cos_t, sin_t, *, name="qk_prep_bwd", tr=256):
    S = dq.shape[0]
    tr = min(tr, S)

    def body(dq_ref, dk_ref, dv_ref, dmq_ref, c_ref, s_ref, o_ref):
        cs, sn = c_ref[...], s_ref[...]
        for c in range(Q_W // LANES):
            t = dq_ref[:, c * LANES:(c + 1) * LANES]
            o_ref[:, c * LANES:(c + 1) * LANES] = (t * cs - _rope_partner(t) * sn).astype(BF16)
        for c in range(KV_W // LANES):
            t = dk_ref[:, c * LANES:(c + 1) * LANES]
            o_ref[:, Q_W + c * LANES:Q_W + (c + 1) * LANES] = (t * cs - _rope_partner(t) * sn).astype(BF16)
        o_ref[:, Q_W + KV_W:Q_W + 2 * KV_W] = dv_ref[...].astype(BF16)
        o_ref[:, Q_W + 2 * KV_W:] = dmq_ref[...]

    row = lambda w: pl.BlockSpec((tr, w), lambda i: (i, 0))
    return pl.pallas_call(
        body, grid=(S // tr,), in_specs=[row(Q_W), row(KV_W), row(KV_W), row(MEM_W), row(LANES), row(LANES)],
        out_specs=row(ATTN_IN_W), out_shape=jax.ShapeDtypeStruct((S, ATTN_IN_W), BF16),
        name=name, compiler_params=_cp(("parallel",)))(dq, dk, dv, dmq, cos_t, sin_t)


def _band(n, S):
    start = pl.multiple_of(jnp.clip((n - 1) * BLOCK, 0, S - 3 * BLOCK), BLOCK)
    qi = lax.broadcasted_iota(jnp.int32, (BLOCK, 3 * BLOCK), 0) + n * BLOCK
    ki = lax.broadcasted_iota(jnp.int32, (BLOCK, 3 * BLOCK), 1) + start
    return start, jnp.abs(ki - qi) <= WINDOW


def _head_operand(ref, h, lo):
    c = h // 2
    t = ref[:, c * LANES:(c + 1) * LANES].astype(F32)
    return jnp.where(lo if h % 2 == 0 else jnp.logical_not(lo), t, 0.0).astype(BF16)


GROUP = ATTN_HEADS // ATTN_KV_HEADS
EVENS_FIRST = (0, 2, 1, 3)


def _attn_fwd(q, kd, va, sinks, *, name="attn_fwd", gather=None):
    S = q.shape[0]
    host = _Hosted(_gather_rider(gather), 4, 2)

    def body(*refs):
        ins, outs, scr, gref = host.split(refs, 4, 2)
        host.run(gref, pl.program_id(0), S // BLOCK, lambda: inner(*ins, *outs, *scr))

    def inner(sink_ref, q_ref, k_ref, va_ref, o_ref, lse_ref, p_scr):
        n = pl.program_id(0)
        start, mask = _band(n, S)
        lane = lax.broadcasted_iota(jnp.int32, (BLOCK, LANES), 1)
        lo = lane < HEAD_DIM
        rows = pl.ds(start, 3 * BLOCK)
        scores = []
        for g in range(ATTN_KV_HEADS):
            qst = jnp.concatenate([_head_operand(q_ref, GROUP * g + j, lo) for j in EVENS_FIRST], axis=0)
            scores.append(lax.dot_general(qst, k_ref[rows, g * LANES:(g + 1) * LANES], NT, preferred_element_type=F32))
        ms = {}
        for g in range(ATTN_KV_HEADS):
            for pos, j in enumerate(EVENS_FIRST):
                h = GROUP * g + j
                s = jnp.where(mask, scores[g][pos * BLOCK:(pos + 1) * BLOCK], NEG)
                ms[h] = jnp.maximum(jnp.max(s, axis=-1, keepdims=True), sink_ref[h])
                p_scr[(GROUP * g + pos) * BLOCK:(GROUP * g + pos + 1) * BLOCK, :] = jnp.exp(s - ms[h]).astype(BF16)
        pvs = {}
        for g in range(ATTN_KV_HEADS):
            for par in range(2):
                r0 = (GROUP * g + 2 * par) * BLOCK
                pvs[g, par] = jnp.dot(p_scr[r0:r0 + 2 * BLOCK, :], va_ref[rows, (2 * g + par) * LANES:(2 * g + par + 1) * LANES],
                                      preferred_element_type=F32)
        lse_blk = jnp.zeros((BLOCK, LANES), F32)
        for g in range(ATTN_KV_HEADS):
            outs = {}
            for par in range(2):
                for k in range(2):
                    j = EVENS_FIRST[2 * par + k]
                    h = GROUP * g + j
                    pv = pvs[g, par][k * BLOCK:(k + 1) * BLOCK]
                    den = pltpu.roll(pv, HEAD_DIM, 1) + jnp.exp(sink_ref[h] - ms[h])
                    outs[j] = pv * (1.0 / den)
                    l = den[:, par * HEAD_DIM:par * HEAD_DIM + 1]
                    lse_blk = jnp.where(lane == h, ms[h] + jnp.log(l), lse_blk)
            for jj in range(2):
                o_ref[:, (2 * g + jj) * LANES:(2 * g + jj + 1) * LANES] = jnp.where(lo, outs[2 * jj], outs[2 * jj + 1]).astype(BF16)
        lse_ref[...] = lse_blk

    full = lambda w: pl.BlockSpec((S, w), lambda i: (0, 0))
    return pl.pallas_call(
        body, grid=(S // BLOCK,),
        in_specs=[pl.BlockSpec(memory_space=pltpu.SMEM), pl.BlockSpec((BLOCK, Q_W), lambda i: (i, 0)),
                  full(2 * KV_W), full(4 * KV_W)] + host.in_specs,
        out_specs=[pl.BlockSpec((BLOCK, Q_W), lambda i: (i, 0)), pl.BlockSpec((BLOCK, LANES), lambda i: (i, 0))] + host.out_specs,
        out_shape=[jax.ShapeDtypeStruct((S, MIX_OUT_W), BF16), jax.ShapeDtypeStruct((S, LANES), F32)] + host.out_shape,
        scratch_shapes=[pltpu.VMEM((ATTN_HEADS * BLOCK, 3 * BLOCK), BF16)] + host.scratch, input_output_aliases=host.alias,
        name=name, compiler_params=_cp(("arbitrary",) if host.on else ("parallel",)))(sinks, q, kd, va, *host.args)


def _attn_bwd(q, kd, vd, ao, lse, sinks, dcat, *, name="attn_bwd", rider=None):
    S = q.shape[0]
    scale = HEAD_DIM ** -0.5
    host = _Hosted(rider, 7, 4)

    def body(*refs):
        ins, outs, scr, rrefs = host.split(refs, 7, 4)
        host.run(rrefs, pl.program_id(0), S // BLOCK, lambda: inner(*ins, *outs, *scr))

    def inner(sink_ref, q_ref, k_ref, v_ref, ao_ref, lse_ref, do_ref, dq_ref, dk_ref, dv_ref, ds_ref, p_scr, dsb_scr):
        n = pl.program_id(0)

        @pl.when(n == 0)
        def _():
            dk_ref[...] = jnp.zeros_like(dk_ref)
            dv_ref[...] = jnp.zeros_like(dv_ref)
            ds_ref[...] = jnp.zeros_like(ds_ref)

        start, mask = _band(n, S)
        lane = lax.broadcasted_iota(jnp.int32, (BLOCK, LANES), 1)
        lo = lane < HEAD_DIM
        lane3 = lax.broadcasted_iota(jnp.int32, (3 * BLOCK, LANES), 1)
        row8 = lax.broadcasted_iota(jnp.int32, (8, LANES), 0)
        lane8 = lax.broadcasted_iota(jnp.int32, (8, LANES), 1)
        dsink = jnp.zeros((8, LANES), F32)
        lse_blk = lse_ref[...]
        rows = pl.ds(start, 3 * BLOCK)
        lses, deltas = {}, {}
        for c in range(Q_W // LANES):
            prod = do_ref[:, c * LANES:(c + 1) * LANES].astype(F32) * ao_ref[:, c * LANES:(c + 1) * LANES].astype(F32)
            for k in range(2):
                h = 2 * c + k
                deltas[h] = jnp.sum(jnp.where(lo if k == 0 else jnp.logical_not(lo), prod, 0.0), axis=1, keepdims=True)
                lses[h] = jnp.sum(jnp.where(lane == h, lse_blk, 0.0), axis=1, keepdims=True)
                val = -jnp.sum(jnp.exp(sink_ref[h] - lses[h]) * deltas[h], axis=0, keepdims=True)
                dsink = dsink + jnp.where((row8 == 0) & (lane8 == h), val, 0.0)
        stack = lambda ref, g: jnp.concatenate([_head_operand(ref, GROUP * g + j, lo) for j in range(GROUP)], axis=0)
        ss, dps = [], []
        for g in range(ATTN_KV_HEADS):
            ss.append(lax.dot_general(stack(q_ref, g), k_ref[rows, g * LANES:(g + 1) * LANES], NT, preferred_element_type=F32))
            dps.append(lax.dot_general(stack(do_ref, g), v_ref[rows, g * LANES:(g + 1) * LANES], NT, preferred_element_type=F32))
        for g in range(ATTN_KV_HEADS):
            for j in range(GROUP):
                h = GROUP * g + j
                r = slice(j * BLOCK, (j + 1) * BLOCK)
                hr = slice(h * BLOCK, (h + 1) * BLOCK)
                p = jnp.exp(jnp.where(mask, ss[g][r], NEG) - lses[h])
                p_scr[hr, :] = p.astype(BF16)
                dsb_scr[hr, :] = (p * (dps[g][r] - deltas[h])).astype(BF16)
        for g in range(ATTN_KV_HEADS):
            cols = slice((g // 2) * LANES, (g // 2 + 1) * LANES)
            gr = slice(GROUP * g * BLOCK, GROUP * (g + 1) * BLOCK)
            dsg = dsb_scr[gr, :]
            dqs = jnp.dot(dsg, k_ref[rows, g * LANES:(g + 1) * LANES], preferred_element_type=F32) * scale
            for jj in range(2):
                dq_ref[:, (2 * g + jj) * LANES:(2 * g + jj + 1) * LANES] = jnp.where(
                    lo, dqs[(2 * jj) * BLOCK:(2 * jj + 1) * BLOCK], dqs[(2 * jj + 1) * BLOCK:(2 * jj + 2) * BLOCK])
            half = (lane3 < HEAD_DIM) if g % 2 == 0 else (lane3 >= HEAD_DIM)
            dkr = lax.dot_general(dsg, stack(q_ref, g), TN, preferred_element_type=F32)
            dk_ref[rows, cols] += jnp.where(half, dkr + pltpu.roll(dkr, HEAD_DIM, 1), 0.0)
            dvr = lax.dot_general(p_scr[gr, :], stack(do_ref, g), TN, preferred_element_type=F32)
            dv_ref[rows, cols] += jnp.where(half, dvr + pltpu.roll(dvr, HEAD_DIM, 1), 0.0)
        ds_ref[...] += dsink

    full = lambda w: pl.BlockSpec((S, w), lambda i: (0, 0))
    blk = lambda w: pl.BlockSpec((BLOCK, w), lambda i: (i, 0))
    return pl.pallas_call(
        body, grid=(S // BLOCK,),
        in_specs=[pl.BlockSpec(memory_space=pltpu.SMEM), blk(Q_W), full(2 * KV_W), full(2 * KV_W), blk(Q_W), blk(LANES), blk(Q_W)]
        + host.in_specs,
        out_specs=[blk(Q_W), full(KV_W), full(KV_W), pl.BlockSpec((8, LANES), lambda i: (0, 0))] + host.out_specs,
        out_shape=[jax.ShapeDtypeStruct((S, Q_W), F32), jax.ShapeDtypeStruct((S, KV_W), F32),
                   jax.ShapeDtypeStruct((S, KV_W), F32), jax.ShapeDtypeStruct((8, LANES), F32)] + host.out_shape,
        scratch_shapes=[pltpu.VMEM((ATTN_HEADS * BLOCK, 3 * BLOCK), BF16), pltpu.VMEM((ATTN_HEADS * BLOCK, 3 * BLOCK), BF16)]
        + host.scratch, input_output_aliases=host.alias,
        name=name, compiler_params=_cp(("arbitrary",)))(sinks, q, kd, vd, ao, lse, dcat, *host.args)


def _mem_probs(q_ref, kv_ref, h):
    scale = MEM_HEAD_DIM ** -0.5
    qh = q_ref[:, h * LANES:(h + 1) * LANES].astype(BF16)
    s = lax.dot_general(qh, kv_ref[:, h * LANES:(h + 1) * LANES], NT, preferred_element_type=F32) * scale
    m = jnp.max(s, axis=-1, keepdims=True)
    pe = jnp.exp(s - m)
    return qh, pe * (1.0 / jnp.sum(pe, axis=-1, keepdims=True))


def _memattn_fwd(p, qblk, kv, cat, *, name="memattn_fwd", tr=512):
    S = p.shape[0]
    tr = min(tr, S)

    def body(q_ref, kv_ref, cat_ref, o_ref):
        for h in range(MEM_HEADS):
            _, pr = _mem_probs(q_ref, kv_ref, h)
            o = jnp.dot(pr.astype(BF16), kv_ref[:, MEM_W + h * LANES:MEM_W + (h + 1) * LANES], preferred_element_type=F32)
            o_ref[:, h * LANES:(h + 1) * LANES] = o.astype(BF16)

    return pl.pallas_call(
        body, grid=(S // tr,),
        in_specs=[pl.BlockSpec((tr, MEM_W), lambda i: (i, qblk)), pl.BlockSpec((MEM_LEN, 2 * MEM_W), lambda i: (0, 0)), HBM],
        out_specs=pl.BlockSpec((tr, MEM_W), lambda i: (i, Q_W // MEM_W)),
        out_shape=jax.ShapeDtypeStruct((S, MIX_OUT_W), BF16), input_output_aliases={2: 0},
        name=name, compiler_params=_cp(("parallel",)))(p, kv, cat)


def _memattn_bwd(p, qblk, kv, dcat, *, name="memattn_bwd", tr=512, rider=None):
    S = p.shape[0]
    tr = min(tr, S)
    scale = MEM_HEAD_DIM ** -0.5
    host = _Hosted(rider, 3, 2)

    def body(*refs):
        ins, outs, _, rrefs = host.split(refs, 3, 2)
        host.run(rrefs, pl.program_id(0), S // tr, lambda: inner(*ins, *outs))

    def inner(q_ref, kv_ref, do_ref, dq_ref, dkv_ref):
        @pl.when(pl.program_id(0) == 0)
        def _():
            dkv_ref[...] = jnp.zeros_like(dkv_ref)

        for h in range(MEM_HEADS):
            qh, pr = _mem_probs(q_ref, kv_ref, h)
            doh = do_ref[:, h * LANES:(h + 1) * LANES]
            dp = lax.dot_general(doh, kv_ref[:, MEM_W + h * LANES:MEM_W + (h + 1) * LANES], NT, preferred_element_type=F32)
            delta = jnp.sum(pr * dp, axis=-1, keepdims=True)
            dsb = (pr * (dp - delta) * scale).astype(BF16)
            dq = jnp.dot(dsb, kv_ref[:, h * LANES:(h + 1) * LANES], preferred_element_type=F32)
            dq_ref[:, h * LANES:(h + 1) * LANES] = dq.astype(BF16)
            dkv_ref[:, h * LANES:(h + 1) * LANES] += lax.dot_general(dsb, qh, TN, preferred_element_type=F32)
            dkv_ref[:, MEM_W + h * LANES:MEM_W + (h + 1) * LANES] += lax.dot_general(
                pr.astype(BF16), doh, TN, preferred_element_type=F32)

    return pl.pallas_call(
        body, grid=(S // tr,),
        in_specs=[pl.BlockSpec((tr, MEM_W), lambda i: (i, qblk)), pl.BlockSpec((MEM_LEN, 2 * MEM_W), lambda i: (0, 0)),
                  pl.BlockSpec((tr, MEM_W), lambda i: (i, Q_W // MEM_W))] + host.in_specs,
        out_specs=[pl.BlockSpec((tr, MEM_W), lambda i: (i, 0)), pl.BlockSpec((MEM_LEN, 2 * MEM_W), lambda i: (0, 0))]
        + host.out_specs,
        out_shape=[jax.ShapeDtypeStruct((S, MEM_W), BF16), jax.ShapeDtypeStruct((MEM_LEN, 2 * MEM_W), F32)] + host.out_shape,
        scratch_shapes=host.scratch, input_output_aliases=host.alias,
        name=name, compiler_params=_cp(("arbitrary",)))(p, kv, dcat, *host.args)


def _sqrt(v):
    return jnp.where(v > 0.0, v * lax.rsqrt(v), 0.0)


def _sigmoid(z):
    return 1.0 / (1.0 + jnp.exp(-z))


def _one_minus_exp(z, exp_z):
    poly = z * (1.0 + z * (0.5 + z * (1.0 / 6.0 + z * (1.0 / 24.0 + z * (1.0 / 120.0)))))
    return jnp.where(z > -0.1, -poly, 1.0 - exp_z)


def _softplus_neg(lam):
    z = -lam
    return jnp.maximum(z, 0.0) + jnp.log(1.0 + jnp.exp(-jnp.abs(z)))


_GELU_C = math.sqrt(2.0 / math.pi)


def _gelu(z):
    return 0.5 * z * (1.0 + jnp.tanh(_GELU_C * (z + 0.044715 * z * z * z)))


def _row_or_zero(ref, t, S):
    ok = jnp.logical_and(t >= 0, t < S)
    return jnp.where(ok, ref[pl.ds(jnp.clip(t, 0, S - 1), 1), :], 0.0)


def _shift_down(v, first):
    ri = lax.broadcasted_iota(jnp.int32, v.shape, 0)
    return jnp.where(ri == 0, first, pltpu.roll(v, 1, 0))


def _shift_up(v, last):
    T = v.shape[0]
    ri = lax.broadcasted_iota(jnp.int32, v.shape, 0)
    return jnp.where(ri == T - 1, last, pltpu.roll(v, T - 1, 0))


def _scan_chunk(a, u, reverse):
    T = a.shape[0]
    ri = lax.broadcasted_iota(jnp.int32, a.shape, 0)
    d = 1
    while d < T:
        if reverse:
            a_s, u_s, ok = pltpu.roll(a, T - d, 0), pltpu.roll(u, T - d, 0), ri < T - d
        else:
            a_s, u_s, ok = pltpu.roll(a, d, 0), pltpu.roll(u, d, 0), ri >= d
        u = jnp.where(ok, a * u_s + u, u)
        a = jnp.where(ok, a * a_s, a)
        d *= 2
    return a, u


def _conv_taps(xb_ref, t0, S):
    T = SCAN_ROWS
    x0 = xb_ref[pl.ds(t0, T), :]
    xm1 = _shift_down(x0, _row_or_zero(xb_ref, t0 - 1, S))
    nxt0 = _row_or_zero(xb_ref, t0 + T, S)
    xp1 = _shift_up(x0, nxt0)
    xp2 = _shift_up(xp1, _row_or_zero(xb_ref, t0 + T + 1, S))
    return xm1, x0, xp1, xp2


def _lru_gates(xc, w_a, b_a, w_x, b_x, sp):
    xcb = xc.astype(BF16)
    r = _sigmoid(jnp.dot(xcb, w_a, preferred_element_type=F32) + b_a)
    i = _sigmoid(jnp.dot(xcb, w_x, preferred_element_type=F32) + b_x)
    la = -LRU_C * r * sp
    a = jnp.exp(la)
    b2 = _one_minus_exp(2.0 * la, a * a)
    inv_beta = lax.rsqrt(b2)
    return r, i, a, jnp.where(b2 > 0.0, b2 * inv_beta, 0.0), inv_beta


def _lru_specs(S):
    col = lambda off: pl.BlockSpec((S, LANES), lambda n: (0, n + off), pipeline_mode=pl.Buffered(1))
    small = lambda r: pl.BlockSpec((r, LANES), lambda n: (0, n))
    wblk = pl.BlockSpec((2, 1, LANES, LANES), lambda n: (0, n, 0, 0))
    return col, small, wblk


def _lru_fwd(p, conv_w, conv_b, wa, ba, wx, bx, lam, *, name="lru_fwd"):
    S = p.shape[0]
    T = SCAN_ROWS
    nc = S // T

    def body(xb_ref, gate_ref, cw_ref, cb_ref, wa_ref, ba_ref, wx_ref, bx_ref, lam_ref, y_ref, hf_ref, hr_ref, xc_v):
        sp = _softplus_neg(lam_ref[...])
        cw = cw_ref[...]

        def fwd_step(c, h_in):
            t0 = pl.multiple_of(c * T, T)
            xm1, x0, xp1, xp2 = _conv_taps(xb_ref, t0, S)
            xc = cb_ref[...] + xm1 * cw[0:1] + x0 * cw[1:2] + xp1 * cw[2:3] + xp2 * cw[3:4]
            xc_v[pl.ds(t0, T), :] = xc
            _, i, a, beta, _ = _lru_gates(xc, wa_ref[0, 0], ba_ref[0:1], wx_ref[0, 0], bx_ref[0:1], sp[0:1])
            A, U = _scan_chunk(a, beta * (i * xc), False)
            hf_ref[pl.ds(t0, T), :] = A * h_in + U
            return hf_ref[pl.ds(t0 + T - 1, 1), :]

        lax.fori_loop(0, nc, fwd_step, jnp.zeros((1, LANES), F32))

        def rev_step(k, h_in):
            t0 = pl.multiple_of((nc - 1 - k) * T, T)
            xc = xc_v[pl.ds(t0, T), :]
            _, i, a, beta, _ = _lru_gates(xc, wa_ref[1, 0], ba_ref[1:2], wx_ref[1, 0], bx_ref[1:2], sp[1:2])
            A, U = _scan_chunk(a, beta * (i * xc), True)
            h = A * h_in + U
            hr_ref[pl.ds(t0, T), :] = h
            y_ref[pl.ds(t0, T), :] = ((hf_ref[pl.ds(t0, T), :] + h) * _gelu(gate_ref[pl.ds(t0, T), :])).astype(BF16)
            return hr_ref[pl.ds(t0, 1), :]

        lax.fori_loop(0, nc, rev_step, jnp.zeros((1, LANES), F32))

    col, small, wblk = _lru_specs(S)
    colo = lambda: pl.BlockSpec((S, LANES), lambda n: (0, n))
    return pl.pallas_call(
        body, grid=(LRU_BLOCKS,),
        in_specs=[col(0), col(LRU_BLOCKS), small(4), small(1), wblk, small(2), wblk, small(2), small(2)],
        out_specs=[colo(), colo(), colo()],
        out_shape=[jax.ShapeDtypeStruct((S, MIX_OUT_W), BF16), jax.ShapeDtypeStruct((S, D_MODEL), F32),
                   jax.ShapeDtypeStruct((S, D_MODEL), F32)],
        scratch_shapes=[pltpu.VMEM((S, LANES), F32)],
        name=name, compiler_params=_cp(("parallel",)))(p, p, conv_w, conv_b, wa, ba, wx, bx, lam)


def _lru_bwd(p, hf, hr, dcat, conv_w, conv_b, wa, ba, wx, bx, lam, *, name="lru_bwd"):
    S = p.shape[0]
    T = SCAN_ROWS
    nc = S // T

    def body(xb_ref, gate_ref, hf_ref, hr_ref, dy_ref, cw_ref, cb_ref, wa_ref, ba_ref, wx_ref, bx_ref, lam_ref,
             dxb_ref, dgate_ref, dcw_ref, dcb_ref, dwa_ref, dba_ref, dwx_ref, dbx_ref, dlam_ref, xc_v, dxc_v, dh_v):
        lam_v = lam_ref[...]
        sp = _softplus_neg(lam_v)
        cw = cw_ref[...]
        for ref in (dcw_ref, dcb_ref, dwa_ref, dba_ref, dwx_ref, dbx_ref, dlam_ref):
            ref[...] = jnp.zeros_like(ref)

        def prep_step(c, carry):
            t0 = pl.multiple_of(c * T, T)
            rows = pl.ds(t0, T)
            xm1, x0, xp1, xp2 = _conv_taps(xb_ref, t0, S)
            xc_v[rows, :] = cb_ref[...] + xm1 * cw[0:1] + x0 * cw[1:2] + xp1 * cw[2:3] + xp2 * cw[3:4]
            z = gate_ref[rows, :]
            dy = dy_ref[rows, :].astype(F32)
            th = jnp.tanh(_GELU_C * (z + 0.044715 * z * z * z))
            dgelu = 0.5 * (1.0 + th) + 0.5 * z * (1.0 - th * th) * _GELU_C * (1.0 + 3.0 * 0.044715 * z * z)
            dgate_ref[rows, :] = (dy * (hf_ref[rows, :] + hr_ref[rows, :]) * dgelu).astype(BF16)
            dh_v[rows, :] = dy * (0.5 * z * (1.0 + th))
            return carry

        lax.fori_loop(0, nc, prep_step, 0)

        def direction(d):
            h_ref = hf_ref if d == 0 else hr_ref
            w_a, w_x = wa_ref[d, 0], wx_ref[d, 0]
            b_a, b_x, sp_d = ba_ref[d:d + 1], bx_ref[d:d + 1], sp[d:d + 1]

            def step(k, carry):
                g_in, a_in = carry
                c = (nc - 1 - k) if d == 0 else k
                t0 = pl.multiple_of(c * T, T)
                rows = pl.ds(t0, T)
                xc = xc_v[rows, :]
                r, i, a, beta, inv_beta = _lru_gates(xc, w_a, b_a, w_x, b_x, sp_d)
                dh = dh_v[rows, :]
                hc = h_ref[rows, :]
                if d == 0:
                    A, U = _scan_chunk(_shift_up(a, a_in), dh, True)
                    g = A * g_in + U
                    h_nb = _shift_down(hc, _row_or_zero(h_ref, t0 - 1, S))
                    nxt = (g[0:1], a[0:1])
                else:
                    A, U = _scan_chunk(_shift_down(a, a_in), dh, False)
                    g = A * g_in + U
                    h_nb = _shift_up(hc, _row_or_zero(h_ref, t0 + T, S))
                    nxt = (g[T - 1:T], a[T - 1:T])
                da = g * h_nb
                dbeta = g * (i * xc)
                tb = g * beta
                dla = da * a - dbeta * (a * a * inv_beta)
                dzr = (dla * (-LRU_C * sp_d)) * (r * (1.0 - r))
                dzi = (tb * xc) * (i * (1.0 - i))
                dzrb, dzib, xcb = dzr.astype(BF16), dzi.astype(BF16), xc.astype(BF16)
                dwa_ref[d, 0] += lax.dot_general(xcb, dzrb, TN, preferred_element_type=F32)
                dwx_ref[d, 0] += lax.dot_general(xcb, dzib, TN, preferred_element_type=F32)
                dba_ref[d:d + 1] += jnp.sum(dzr, axis=0, keepdims=True)
                dbx_ref[d:d + 1] += jnp.sum(dzi, axis=0, keepdims=True)
                dlam_ref[d:d + 1] += jnp.sum(dla * (-LRU_C * r), axis=0, keepdims=True)
                dxc = (tb * i + lax.dot_general(dzrb, w_a, NT, preferred_element_type=F32)
                       + lax.dot_general(dzib, w_x, NT, preferred_element_type=F32))
                if d == 0:
                    dxc_v[rows, :] = dxc
                else:
                    dxc_v[rows, :] += dxc
                return nxt

            lax.fori_loop(0, nc, step, (jnp.zeros((1, LANES), F32), jnp.zeros((1, LANES), F32)))

        direction(0)
        direction(1)
        dlam_ref[...] = dlam_ref[...] * (-1.0 / (1.0 + jnp.exp(lam_v)))

        def conv_step(c, carry):
            t0 = pl.multiple_of(c * T, T)
            rows = pl.ds(t0, T)
            g0 = dxc_v[rows, :]
            gm1 = _shift_down(g0, _row_or_zero(dxc_v, t0 - 1, S))
            gm2 = _shift_down(gm1, _row_or_zero(dxc_v, t0 - 2, S))
            gp1 = _shift_up(g0, _row_or_zero(dxc_v, t0 + T, S))
            dxb_ref[rows, :] = (cw[0:1] * gp1 + cw[1:2] * g0 + cw[2:3] * gm1 + cw[3:4] * gm2).astype(BF16)
            xm1, x0, xp1, xp2 = _conv_taps(xb_ref, t0, S)
            for tap, xs in enumerate((xm1, x0, xp1, xp2)):
                dcw_ref[tap:tap + 1] += jnp.sum(g0 * xs, axis=0, keepdims=True)
            dcb_ref[...] += jnp.sum(g0, axis=0, keepdims=True)
            return carry

        lax.fori_loop(0, nc, conv_step, 0)

    col, small, wblk = _lru_specs(S)
    colo = lambda: pl.BlockSpec((S, LANES), lambda n: (0, n), pipeline_mode=pl.Buffered(1))
    return pl.pallas_call(
        body, grid=(LRU_BLOCKS,),
        in_specs=[col(0), col(LRU_BLOCKS), col(0), col(0), col(0), small(4), small(1), wblk, small(2), wblk, small(2), small(2)],
        out_specs=[colo(), colo(), small(4), small(1), wblk, small(2), wblk, small(2), small(2)],
        out_shape=[jax.ShapeDtypeStruct((S, D_MODEL), BF16), jax.ShapeDtypeStruct((S, D_MODEL), BF16),
                   jax.ShapeDtypeStruct((4, D_MODEL), F32), jax.ShapeDtypeStruct((1, D_MODEL), F32),
                   jax.ShapeDtypeStruct((2, LRU_BLOCKS, LANES, LANES), F32), jax.ShapeDtypeStruct((2, D_MODEL), F32),
                   jax.ShapeDtypeStruct((2, LRU_BLOCKS, LANES, LANES), F32), jax.ShapeDtypeStruct((2, D_MODEL), F32),
                   jax.ShapeDtypeStruct((2, D_MODEL), F32)],
        scratch_shapes=[pltpu.VMEM((S, LANES), F32), pltpu.VMEM((S, LANES), F32), pltpu.VMEM((S, LANES), F32)],
        name=name, compiler_params=_cp(("parallel",)))(p, p, hf, hr, dcat, conv_w, conv_b, wa, ba, wx, bx, lam)


PK_UP, PK_DOWN, PK_KV, PK_OUT, PK_IN = 0, 1024, 2048, 2304, 2688
PK_ROWS = {0: PK_IN, 1: PK_IN + 640}
SMALL_G_ROWS = 192
PKF_KV, PKF_SMALL = 512, 768
PKF_ROWS = PKF_SMALL + SMALL_G_ROWS


def _mlp_bwd(x, dx, dxb, saved, w_up, w_down, gain, l, rider=None, next_rider=None):
    up, act, h = saved
    pk = _mm_tn(act, dxb, 1, name=f"dw_down{l}", packed=(None, PK_ROWS[l], PK_DOWN), rider=rider)
    pk, carried = pk if rider is not None else (pk, None)
    dup = _mm_nt(dxb, w_down, up=up, name=f"d_up{l}")
    rider_up = next_rider(carried) if next_rider is not None else None
    pk = _mm_tn(h, dup, N_CHIPS, name=f"dw_up{l}", packed=(pk, PK_ROWS[l], PK_UP), rider=rider_up)
    pk, carried = pk if rider_up is not None else (pk, carried)
    dx, dxb, g_gain = _mm_nt(dup, w_up, norm_x=x, norm_g=gain, dres=dx, name=f"d_mlp_in{l}")
    return dx, dxb, pk, g_gain, carried


def _reduce_first(pk, place, tag, recv):
    return _sum_halves(pk, recv, place, name=f"sum_halves{tag}", tr=pk.shape[1] // 4)


def _sum_parts(parts, place, tag):
    return _sum_chips(parts, place, name=f"sum_chips{tag}", tr=parts.shape[1] // 2)


def _local_step(x, mem, positions, target, W, pending=None, place=None):
    cos_t, sin_t = _rope_tables(positions)
    sinks = W["attn_sinks"].reshape(ATTN_HEADS)
    G = {}

    def hosting(late, fn, *args, **kw):
        if pending is None:
            return fn(*args, **kw)
        *res, buf = fn(*args, gather=pending[late], **kw)
        if late.startswith("w_down"):
            W.setdefault("w_down", [None] * DEPTH)[int(late[-1])] = _ready(late, buf)
        elif late == "w_up":
            W["w_up"], W["w_mem_kv"] = _ready(late, buf)
        else:
            W[late] = _ready(late, buf)
        return res if len(res) > 1 else res[0]

    p0, h0 = hosting("w_out", _mm_nn, x, W["attn_w_in"], norm_g=W["mix_norm"][0], name="attn_in")
    q, kd, vd, va = hosting("w_down0", _qk_prep, p0, cos_t, sin_t)
    ao, lse = hosting("w_up", _attn_fwd, q, kd, va, sinks)
    kv0, memn = _mm_nn(mem, W["w_mem_kv"][0], norm_g=W["mem_norm"], out_dtype=BF16, name="mem_kv0", tm=256)
    kv1 = _mm_nn(memn, W["w_mem_kv"][1], out_dtype=BF16, name="mem_kv1", tm=256)
    cat0 = _memattn_fwd(p0, Q_W // MEM_W + 1, kv0, ao, name="memattn_fwd0")
    x1 = hosting("lru_w_in", _mm_nn, cat0, W["w_out"][0], resid=x, name="mix_out0")
    up0, act0, h1 = hosting("w_down1", _mm_nn, x1, W["w_up"][0], norm_g=W["mlp_norm"][0], relu2=True, name="mlp_up0")
    x2, mlp0 = _mm_nn(act0, W["w_down"][0], resid=x1, name="mlp_down0"), (up0, act0, h1)
    p1, h2 = _mm_nn(x2, W["lru_w_in"], norm_g=W["mix_norm"][1], name="lru_in")
    lru_w = (W["lru_conv_w"], W["lru_conv_b"], W["lru_wa"], W["lru_ba"], W["lru_wx"], W["lru_bx"], W["lru_lambda"])
    y, hf, hr = _lru_fwd(p1, *lru_w)
    cat1 = _memattn_fwd(p1, 2 * D_MODEL // MEM_W, kv1, y, name="memattn_fwd1")
    x3 = _mm_nn(cat1, W["w_out"][1], resid=x2, name="mix_out1")
    mlp1 = _mm_nn(x3, W["w_up"][1], norm_g=W["mlp_norm"][1], relu2=True, name="mlp_up1")
    loss, dx, dxb, G["final_norm"] = _final(mlp1[1], W["w_down"][1], x3, W["final_norm"], target)

    def put(pk, off, g):
        return pk.at[:, off:off + g.size // (N_CHIPS * ROW)].set(g.reshape(N_CHIPS, -1, ROW))

    dx, dxb, pk1, gm1, _ = _mlp_bwd(x3, dx, dxb, mlp1, W["w_up"][1], W["w_down"][1], W["mlp_norm"][1], 1)
    pk1 = _mm_tn(cat1, dxb, 1, name="dw_out1", tk=384, packed=(pk1, PK_ROWS[1], PK_OUT))
    dcat1 = _mm_nt(dxb, W["w_out"][1], name="d_mix1")
    dmq1, dkv1 = _memattn_bwd(p1, 2 * D_MODEL // MEM_W, kv1, dcat1, name="memattn_bwd1")
    dkv1b = dkv1.astype(BF16)
    pk1 = _mm_tn(memn, dkv1b, 1, name="dw_kv1", tm=256, tk=256, packed=(pk1, PK_ROWS[1], PK_KV))
    (dxb1, dgate, G["lru_conv_w"], G["lru_conv_b"], G["lru_wa"], G["lru_ba"], G["lru_wx"], G["lru_bx"],
     G["lru_lambda"]) = _lru_bwd(p1, hf, hr, dcat1, *lru_w)
    dp1 = jnp.concatenate([dxb1, dgate, dmq1], axis=1)
    pk1 = put(pk1, PK_IN, _mm_tn(h2, dp1, N_CHIPS, name="dw_lru_in"))
    dx, dxb, gx1 = _mm_nt(dp1, W["lru_w_in"], norm_x=x2, norm_g=W["mix_norm"][1], dres=dx, name="d_lru_in")
    dist = place is not None
    h1_rows = PK_ROWS[1] // 4
    kept = {}

    def first_half(recv1):
        kept["halves1"], landing1 = _reduce_first(pk1, place, "1", recv1)
        return _exchange_rider((kept["halves1"], landing1, 0, h1_rows))

    dx, dxb, pk0, gm0, landing1 = _mlp_bwd(x1, dx, dxb, mlp0, W["w_up"][0], W["w_down"][0], W["mlp_norm"][0], 0,
                                           rider=_sib_exchange_rider(pk1) if dist else None,
                                           next_rider=first_half if dist else None)
    pk0 = _mm_tn(cat0, dxb, 1, name="dw_out0", tk=384, packed=(pk0, PK_ROWS[0], PK_OUT))
    pk0 = pk0.at[:, PK_KV:PK_OUT].set(0.0)
    dcat0 = _mm_nt(dxb, W["w_out"][0], name="d_mix0")
    dmq0, dkv0, *recv0 = _memattn_bwd(p0, Q_W // MEM_W + 1, kv0, dcat0, name="memattn_bwd0",
                                      rider=_sib_exchange_rider(pk0) if dist else None)
    dkv0b = dkv0.astype(BF16)
    g_kv0 = _mm_tn(memn, dkv0b, 1, name="dw_kv0", tm=256, tk=256)
    rider = None
    if dist:
        halves0, landing0 = _reduce_first(pk0, place, "0", recv0[0])
        rider = _exchange_rider((kept["halves1"], landing1, h1_rows, h1_rows), (halves0, landing0, 0, halves0.shape[1]))
    dq, dk, dv, dsink, *parts = _attn_bwd(q, kd, vd, cat0, lse, sinks, dcat0, rider=rider)
    dp0 = _qk_prep_bwd(dq, dk, dv, dmq0, cos_t, sin_t)
    g_in = _mm_tn(h0, dp0, N_CHIPS, name="dw_attn_in",
                  rider=_sib_allgather_rider(_sum_parts(parts[0], place, "1"), _sum_parts(parts[1], place, "0")) if dist else None)
    if dist:
        g_in, pk1, pk0 = g_in
    dx, _, gx0 = _mm_nt(dp0, W["attn_w_in"], norm_x=x, norm_g=W["mix_norm"][0], dres=dx, name="d_attn_in")

    w_kv_both = jnp.concatenate([W["w_mem_kv"][0], W["w_mem_kv"][1]], axis=0)
    _, _, G["mem_norm"] = _mm_nt(jnp.concatenate([dkv0b, dkv1b], axis=1), w_kv_both, norm_x=mem, norm_g=W["mem_norm"],
                                 name="d_mem", tm=256)

    G["mix_norm"] = jnp.concatenate([gx0, gx1], axis=0)
    G["mlp_norm"] = jnp.concatenate([gm0, gm1], axis=0)
    G["attn_sinks"] = dsink[0:1, 0:ATTN_HEADS]
    small = _flat_pad(_small_grad_list(G), N_CHIPS * SMALL_G_ROWS * ROW).reshape(N_CHIPS, SMALL_G_ROWS, ROW)
    pkf = jnp.concatenate([g_in.reshape(N_CHIPS, PKF_KV, ROW), g_kv0.reshape(N_CHIPS, PKF_SMALL - PKF_KV, ROW), small], axis=1)
    return loss[0, 0], dx, G, pkf, pk0, pk1


def _comm_call(body, out_shape, n_sems, name, *args, alias=None):
    return pl.pallas_call(
        body, out_shape=out_shape, in_specs=[HBM] * len(args), out_specs=HBM,
        scratch_shapes=[pltpu.SemaphoreType.DMA((n_sems,)), pltpu.SemaphoreType.DMA((n_sems,))],
        input_output_aliases=alias or {}, name=name)(*args)


def _place_slot(shard, slot, n_slots, *, name, tr):
    R, C = shard.shape

    def body(s_ref, a_ref, o_ref):
        o_ref[0] = a_ref[...]

    return pl.pallas_call(
        body,
        grid_spec=pltpu.PrefetchScalarGridSpec(
            num_scalar_prefetch=1, grid=(R // tr,), in_specs=[pl.BlockSpec((tr, C), lambda i, s_ref: (i, 0))],
            out_specs=pl.BlockSpec((1, tr, C), lambda i, s_ref: (s_ref[0], i, 0))),
        out_shape=jax.ShapeDtypeStruct((n_slots, R, C), shard.dtype), name=name,
        compiler_params=_cp(("parallel",)))(slot, shard)


def _allgather_chips(buf, *, name):
    def body(b_ref, o_ref, send_sems, recv_sems):
        _gather_start(o_ref, send_sems, recv_sems)
        _gather_finish(o_ref, send_sems, recv_sems)

    return _comm_call(body, jax.ShapeDtypeStruct(buf.shape, buf.dtype), GATHER_SEMS, name, buf, alias={0: 0})


def _sum_halves(g, recv, place, *, name="sum_halves", tr=480):
    _, R, C = g.shape
    half = R // 2
    nblk = half // tr

    def body(pl_ref, g_ref, r_ref, o_ref, own_ref):
        v = (g_ref[...] + r_ref[...]).astype(BF16)
        o_ref[...] = v

        @pl.when(pl.program_id(1) == pl_ref[1])
        def _():
            own_ref[...] = v

    blk = pl.BlockSpec((1, tr, C), lambda i, s, p: (s, i, 0))
    return pl.pallas_call(
        body,
        grid_spec=pltpu.PrefetchScalarGridSpec(
            num_scalar_prefetch=1, grid=(nblk, N_CHIPS),
            in_specs=[pl.BlockSpec((1, tr, C), lambda i, s, p: (s, p[0] * nblk + i, 0)), blk],
            out_specs=[blk, pl.BlockSpec((1, tr, C), lambda i, s, p: (p[1], i, 0))]),
        out_shape=[jax.ShapeDtypeStruct((N_CHIPS, half, C), BF16)] * 2, name=name,
        compiler_params=_cp(("parallel", "arbitrary")))(place, g, recv)


def _sum_chips(parts, place, *, name="sum_chips", tr=480):
    _, R, C = parts.shape
    nblk = R // tr

    def body(pl_ref, p_ref, o_ref):
        acc = p_ref[0].astype(F32) + p_ref[1].astype(F32)
        o_ref[...] = (acc + p_ref[2].astype(F32)) + p_ref[3].astype(F32)

    return pl.pallas_call(
        body,
        grid_spec=pltpu.PrefetchScalarGridSpec(
            num_scalar_prefetch=1, grid=(nblk,), in_specs=[pl.BlockSpec((N_CHIPS, tr, C), lambda i, p: (0, i, 0))],
            out_specs=pl.BlockSpec((tr, C), lambda i, p: (p[0] * nblk + i, 0))),
        out_shape=jax.ShapeDtypeStruct((2 * R, C), F32), name=name, compiler_params=_cp(("parallel",)))(place, parts)


def _adamw(w, g, m, v, *, name, tr=128, rider=None):
    R, C = w.shape
    bc1 = 1.0 - ADAM_B1 ** ADAM_STEP
    bc2 = 1.0 - ADAM_B2 ** ADAM_STEP
    host = _Hosted(rider, 4, 3)

    def body(*refs):
        ins, outs, _, rrefs = host.split(refs, 4, 3)
        host.run(rrefs, pl.program_id(0), R // tr, lambda: inner(*ins, *outs))

    def inner(w_ref, g_ref, m_ref, v_ref, d_ref, nm_ref, nv_ref):
        gv = g_ref[...]
        nm = ADAM_B1 * m_ref[...] + (1.0 - ADAM_B1) * gv
        nv = ADAM_B2 * v_ref[...] + (1.0 - ADAM_B2) * (gv * gv)
        d_ref[...] = -ADAM_LR * ((nm / bc1) / (_sqrt(nv / bc2) + ADAM_EPS) + ADAM_WD * w_ref[...])
        nm_ref[...] = nm
        nv_ref[...] = nv

    blk = pl.BlockSpec((tr, C), lambda i: (i, 0))
    return pl.pallas_call(
        body, grid=(R // tr,), in_specs=[blk] * 4 + host.in_specs, out_specs=[blk] * 3 + host.out_specs,
        out_shape=[jax.ShapeDtypeStruct((R, C), F32)] * 3 + host.out_shape, scratch_shapes=host.scratch,
        input_output_aliases=host.alias, name=name,
        compiler_params=_cp(("arbitrary",) if host.on else ("parallel",)))(w, g, m, v, *host.args)


ROW = 1024
BIG = ("w_mem_kv", "w_out", "w_up", "w_down", "attn_w_in", "lru_w_in")
SMALL_SHARDED = ("lru_conv_w", "lru_conv_b", "lru_ba", "lru_bx", "lru_lambda")
REPLICATED = ("mix_norm", "mlp_norm", "mem_norm", "final_norm", "attn_sinks", "lru_wa", "lru_wx")
SMALL = REPLICATED + SMALL_SHARDED
WEIGHTS = ("mix_norm", "mlp_norm", "mem_norm", "final_norm", "w_mem_kv", "w_out", "w_up", "w_down", "attn_w_in",
           "attn_sinks", "lru_w_in", "lru_conv_w", "lru_conv_b", "lru_wa", "lru_ba", "lru_wx", "lru_bx", "lru_lambda")
SMALL_W_ROWS = 32
ADAM_SMALL_ROWS = 640


def _rows(a):
    return a.reshape(-1, ROW)


def _flat_pad(parts, total):
    flat = jnp.concatenate([p.reshape(-1) for p in parts])
    return jnp.pad(flat, (0, total - flat.shape[0]))


def _pad_rows(a):
    flat = a.reshape(-1)
    n = -(-flat.shape[0] // ROW) * ROW
    return jnp.pad(flat, (0, n - flat.shape[0])).reshape(-1, ROW)


LATE = ("w_out", "w_down0", "w_up", "lru_w_in", "w_down1")


def _ready(name, full):
    if name == "w_out":
        wo = full.reshape(N_CHIPS, DEPTH, -1, D_MODEL)
        return [wo[:, l].reshape(1, MIX_OUT_W, D_MODEL) for l in range(DEPTH)]
    if name == "w_up":
        n_up = DEPTH * D_MODEL
        wu = full[:, :n_up].reshape(N_CHIPS, DEPTH, D_MODEL, D_FF // N_CHIPS)
        kv = full[:, n_up:].reshape(N_CHIPS, DEPTH, -1, D_MODEL)
        return [wu[:, l] for l in range(DEPTH)], [kv[:, l].reshape(1, D_MODEL, D_MODEL) for l in range(DEPTH)]
    if name == "lru_w_in":
        return full.reshape(N_CHIPS, D_MODEL, LRU_IN_W // N_CHIPS)
    return full.reshape(1, D_FF, D_MODEL)


def _gather_weights(P, chip1):
    bf = lambda a: _rows(a.astype(BF16))
    small = _flat_pad([P[n] for n in SMALL_SHARDED], SMALL_W_ROWS * ROW // 2)
    small_bits = lax.bitcast_convert_type(small, BF16).reshape(SMALL_W_ROWS, ROW)
    early = jnp.concatenate([bf(P["attn_w_in"]), small_bits], axis=0)
    n_in = P["attn_w_in"].size // ROW
    placed = _place_slot(early, chip1, N_CHIPS, name="place_weights", tr=early.shape[0] // 2)
    full = _allgather_chips(placed, name="allgather_weights")
    late = {"w_out": bf(P["w_out"]), "lru_w_in": bf(P["lru_w_in"]),
            "w_up": jnp.concatenate([bf(P["w_up"]), bf(P["w_mem_kv"])], axis=0),
            "w_down0": bf(P["w_down"][0]), "w_down1": bf(P["w_down"][1])}
    pending = {n: _place_slot(late[n], chip1, N_CHIPS, name=f"place_{n}", tr=late[n].shape[0] // 2) for n in LATE}
    W = {n: P[n] for n in REPLICATED}
    W["attn_w_in"] = full[:, :n_in].reshape(N_CHIPS, D_MODEL, ATTN_IN_W // N_CHIPS)
    sm = lax.bitcast_convert_type(full[:, n_in:].reshape(N_CHIPS, -1, 2), F32)
    o = 0
    for n in SMALL_SHARDED:
        shp = P[n].shape[1:]
        cnt = math.prod(shp)
        piece = sm[:, o:o + cnt].reshape((N_CHIPS,) + shp)
        piece = jnp.moveaxis(piece, 0, -2)
        W[n] = piece.reshape(shp[:-1] + (N_CHIPS * shp[-1],)).reshape(-1, D_MODEL)
        o += cnt
    W["lru_wa"] = P["lru_wa"][0].astype(BF16)
    W["lru_wx"] = P["lru_wx"][0].astype(BF16)
    return W, pending


def _small_grad_list(G):
    return [G["mix_norm"], G["mlp_norm"], G["mem_norm"], G["final_norm"], jnp.pad(G["attn_sinks"].reshape(-1), (0, ROW - ATTN_HEADS)),
            G["lru_wa"], G["lru_wx"], G["lru_conv_w"], G["lru_conv_b"], G["lru_ba"], G["lru_bx"], G["lru_lambda"]]


SMALL_G_SIZES = (2 * D_MODEL, 2 * D_MODEL, D_MODEL, D_MODEL, ROW, 2 * 8 * 128 * 128, 2 * 8 * 128 * 128,
                 4 * D_MODEL, D_MODEL, 2 * D_MODEL, 2 * D_MODEL, 2 * D_MODEL)


def _finish_grads(pkf, full0, full1, place, chip1, adamw):
    both = lambda off, r: jnp.concatenate([full0[off:off + r], full1[off:off + r]], axis=0)
    recvf = adamw("w_up", both(PK_UP, 1024), _sib_exchange_rider(pkf))
    halvesf, landingf = _reduce_first(pkf, place, "f", recvf)
    partsf = adamw("w_down", both(PK_DOWN, 1024), _exchange_rider((halvesf, landingf, 0, halvesf.shape[1])))
    fullf = adamw("w_out", both(PK_OUT, 384), _sib_allgather_rider(_sum_parts(partsf, place, "f")))
    small_placed = _place_slot(fullf[PKF_SMALL:], chip1, N_CHIPS, name="place_small_grads", tr=SMALL_G_ROWS)
    small_all = adamw("lru_w_in", full1[PK_IN:PK_IN + 640], _gather_rider(small_placed))
    adamw("w_mem_kv", jnp.concatenate([fullf[PKF_KV:PKF_SMALL], full1[PK_KV:PK_OUT]], axis=0), None)
    adamw("attn_w_in", fullf[:PKF_KV], None)
    flat = small_all.reshape(-1)
    small = {}
    o = 0
    names = ("mix_norm", "mlp_norm", "mem_norm", "final_norm", "attn_sinks", "lru_wa", "lru_wx",
             "lru_conv_w", "lru_conv_b", "lru_ba", "lru_bx", "lru_lambda")
    for n, cnt in zip(names, SMALL_G_SIZES):
        small[n] = flat[o:o + cnt]
        o += cnt
    return small


def kernel(x, mem, positions, mix_norm, mlp_norm, mem_norm, final_norm, w_mem_kv, w_out, w_up, w_down, attn_w_in, attn_sinks, lru_w_in, lru_conv_w, lru_conv_b, lru_wa, lru_ba, lru_wx, lru_bx, lru_lambda, loss_target, m_mix_norm, m_mlp_norm, m_mem_norm, m_final_norm, m_w_mem_kv, m_w_out, m_w_up, m_w_down, m_attn_w_in, m_attn_sinks, m_lru_w_in, m_lru_conv_w, m_lru_conv_b, m_lru_wa, m_lru_ba, m_lru_wx, m_lru_bx, m_lru_lambda, v_mix_norm, v_mlp_norm, v_mem_norm, v_final_norm, v_w_mem_kv, v_w_out, v_w_up, v_w_down, v_attn_w_in, v_attn_sinks, v_lru_w_in, v_lru_conv_w, v_lru_conv_b, v_lru_wa, v_lru_ba, v_lru_wx, v_lru_bx, v_lru_lambda):
    P = dict(mix_norm=mix_norm, mlp_norm=mlp_norm, mem_norm=mem_norm, final_norm=final_norm, w_mem_kv=w_mem_kv, w_out=w_out,
             w_up=w_up, w_down=w_down, attn_w_in=attn_w_in, attn_sinks=attn_sinks, lru_w_in=lru_w_in, lru_conv_w=lru_conv_w,
             lru_conv_b=lru_conv_b, lru_wa=lru_wa, lru_ba=lru_ba, lru_wx=lru_wx, lru_bx=lru_bx, lru_lambda=lru_lambda)
    M1 = dict(mix_norm=m_mix_norm, mlp_norm=m_mlp_norm, mem_norm=m_mem_norm, final_norm=m_final_norm, w_mem_kv=m_w_mem_kv,
              w_out=m_w_out, w_up=m_w_up, w_down=m_w_down, attn_w_in=m_attn_w_in, attn_sinks=m_attn_sinks, lru_w_in=m_lru_w_in,
              lru_conv_w=m_lru_conv_w, lru_conv_b=m_lru_conv_b, lru_wa=m_lru_wa, lru_ba=m_lru_ba, lru_wx=m_lru_wx,
              lru_bx=m_lru_bx, lru_lambda=m_lru_lambda)
    V2 = dict(mix_norm=v_mix_norm, mlp_norm=v_mlp_norm, mem_norm=v_mem_norm, final_norm=v_final_norm, w_mem_kv=v_w_mem_kv,
              w_out=v_w_out, w_up=v_w_up, w_down=v_w_down, attn_w_in=v_attn_w_in, attn_sinks=v_attn_sinks, lru_w_in=v_lru_w_in,
              lru_conv_w=v_lru_conv_w, lru_conv_b=v_lru_conv_b, lru_wa=v_lru_wa, lru_ba=v_lru_ba, lru_wx=v_lru_wx,
              lru_bx=v_lru_bx, lru_lambda=v_lru_lambda)
    chip = 2 * lax.axis_index("x") + lax.axis_index("y")
    chip1 = chip.astype(jnp.int32).reshape(1)
    place = jnp.stack([lax.axis_index("c").astype(jnp.int32), chip.astype(jnp.int32)])

    W, pending = _gather_weights(P, chip1)
    loss, dx, _, pkf, full0, full1 = _local_step(x[0], mem[0], positions[0], loss_target[0], W, pending, place)
    loss = lax.psum(loss, ("x", "y", "c"))
    grads, deltas, new_m, new_v = {}, {}, {}, {}

    def adamw_big(n, g, rider):
        d, nm, nv, *carried = _adamw(_rows(P[n]), g, _rows(M1[n]), _rows(V2[n]), name=f"adamw_{n}", rider=rider)
        grads[n], deltas[n], new_m[n], new_v[n] = (t.reshape(P[n].shape) for t in (g, d, nm, nv))
        return carried[0] if carried else None

    small = _finish_grads(pkf, full0, full1, place, chip1, adamw_big)

    for n in SMALL:
        g = small[n]
        if n in SMALL_SHARDED:
            shard = P[n].shape[-1]
            g = lax.dynamic_slice_in_dim(g.reshape(-1, N_CHIPS * shard), chip * shard, shard, axis=1)
        elif n == "attn_sinks":
            g = g[:ATTN_HEADS]
        grads[n] = g.reshape(P[n].shape)
    packs = []
    for src in (P, grads, M1, V2):
        a = jnp.concatenate([_pad_rows(src[n]) for n in SMALL], axis=0)
        packs.append(jnp.pad(a, ((0, ADAM_SMALL_ROWS - a.shape[0]), (0, 0))))
    d_s, nm_s, nv_s = _adamw(*packs, name="adamw_small")
    o = 0
    for n in SMALL:
        cnt = math.prod(P[n].shape)
        r = -(-cnt // ROW)
        for dst, src in ((deltas, d_s), (new_m, nm_s), (new_v, nv_s)):
            dst[n] = src[o:o + r].reshape(-1)[:cnt].reshape(P[n].shape)
        o += r

    return (loss, dx[None], *[grads[n] for n in WEIGHTS], *[deltas[n] for n in WEIGHTS],
            *[new_m[n] for n in WEIGHTS], *[new_v[n] for n in WEIGHTS])
```

```python
import math

import jax
import jax.numpy as jnp
from jax import lax
from jax.experimental import pallas as pl
from jax.experimental.pallas import tpu as pltpu

F32 = jnp.float32
BF16 = jnp.bfloat16
MESH = pl.DeviceIdType.MESH

D_MODEL = 1024
DEPTH = 2
EPS = 1e-6
ATTN_HEADS = 16
ATTN_KV_HEADS = 4
HEAD_DIM = 64
WINDOW = 128
BLOCK = 128
ROPE_THETA = 500000.0
ROPE_DIM = 16
Q_W = 1024
KV_W = 256
MEM_LEN = 256
MEM_HEADS = 4
MEM_HEAD_DIM = 128
MEM_W = 512
LRU_BLOCKS = 8
LRU_C = 8.0
ATTN_IN_W = 2048
LRU_IN_W = 2560
MIX_OUT_W = 1536
D_FF = 4096
NEG = -1e30
N_CHIPS = 4

ADAM_LR = 0.001
ADAM_B1 = 0.9
ADAM_B2 = 0.999
ADAM_EPS = 1e-08
ADAM_WD = 0.01
ADAM_STEP = 10

LANES = 128
SCAN_ROWS = 1024
ADAM_ROWS = 512
VMEM_LIMIT = 56 * 1024 * 1024

NT = (((1,), (1,)), ((), ()))
TN = (((0,), (0,)), ((), ()))


def _cp(sem=None):
    return pltpu.CompilerParams(dimension_semantics=sem, vmem_limit_bytes=VMEM_LIMIT)


HBM = pl.BlockSpec(memory_space=pl.ANY)
GATHER_SEMS = 6


def _place():
    x, y, c = lax.axis_index("x"), lax.axis_index("y"), lax.axis_index("c")
    chips = [(1 - x, y), (x, 1 - y), (1 - x, 1 - y)]
    return x, y, c, chips


def _remote(src, dst, send_sems, recv_sems, k, to):
    return pltpu.make_async_remote_copy(src_ref=src, dst_ref=dst, send_sem=send_sems.at[k], recv_sem=recv_sems.at[k],
                                        device_id=to, device_id_type=MESH)


def _gather_start(o_ref, send_sems, recv_sems):
    x, y, c, chips = _place()
    half = o_ref.shape[1] // 2
    own = o_ref.at[2 * x + y, pl.ds(pl.multiple_of(c * half, 16), half)]
    for j, (cx, cy) in enumerate(chips):
        _remote(own, own, send_sems, recv_sems, j, (cx, cy, c)).start()


def _gather_forward(o_ref, send_sems, recv_sems):
    x, y, c, chips = _place()
    half = o_ref.shape[1] // 2
    my_rows = pl.ds(pl.multiple_of(c * half, 16), half)
    for j, (cx, cy) in enumerate(chips):
        landed = o_ref.at[2 * cx + cy, my_rows]
        _remote(landed, landed, send_sems, recv_sems, j, (cx, cy, c)).wait_recv()
        _remote(landed, landed, send_sems, recv_sems, 3 + j, (x, y, 1 - c)).start()


def _gather_drain(o_ref, send_sems, recv_sems):
    x, y, c, chips = _place()
    half = o_ref.shape[1] // 2
    my_rows = pl.ds(pl.multiple_of(c * half, 16), half)
    sib_rows = pl.ds(pl.multiple_of((1 - c) * half, 16), half)
    own = o_ref.at[2 * x + y, my_rows]
    for j, (cx, cy) in enumerate(chips):
        got = o_ref.at[2 * cx + cy, sib_rows]
        _remote(got, got, send_sems, recv_sems, 3 + j, (x, y, 1 - c)).wait_recv()
    for j, (cx, cy) in enumerate(chips):
        _remote(own, own, send_sems, recv_sems, j, (cx, cy, c)).wait_send()
        landed = o_ref.at[2 * cx + cy, my_rows]
        _remote(landed, landed, send_sems, recv_sems, 3 + j, (x, y, 1 - c)).wait_send()


def _gather_finish(o_ref, send_sems, recv_sems):
    _gather_forward(o_ref, send_sems, recv_sems)
    _gather_drain(o_ref, send_sems, recv_sems)


def _exchange_start(h_ref, o_ref, send_sems, recv_sems, rows=None, base=0):
    x, y, c, chips = _place()
    rows = pl.ds(0, h_ref.shape[1]) if rows is None else rows
    for j, (cx, cy) in enumerate(chips):
        _remote(h_ref.at[2 * cx + cy, rows], o_ref.at[2 * x + y, rows], send_sems, recv_sems, base + j, (cx, cy, c)).start()


def _exchange_finish(h_ref, o_ref, send_sems, recv_sems, rows=None, base=0):
    x, y, c, chips = _place()
    rows = pl.ds(0, h_ref.shape[1]) if rows is None else rows
    for j, (cx, cy) in enumerate(chips):
        got = o_ref.at[2 * cx + cy, rows]
        _remote(got, got, send_sems, recv_sems, base + j, (cx, cy, c)).wait_recv()
    for j, (cx, cy) in enumerate(chips):
        _remote(h_ref.at[2 * cx + cy, rows], o_ref.at[2 * x + y, rows], send_sems, recv_sems, base + j, (cx, cy, c)).wait_send()


def _sib_exchange_copies(g_ref, o_ref, send_sems, recv_sems):
    x, y, c, _ = _place()
    half = g_ref.shape[1] // 2
    other = pl.ds(pl.multiple_of((1 - c) * half, 8), half)
    return [_remote(g_ref.at[s, other], o_ref.at[s], send_sems, recv_sems, s, (x, y, 1 - c)) for s in range(N_CHIPS)]


def _sib_exchange_start(*refs):
    for cp in _sib_exchange_copies(*refs):
        cp.start()


def _sib_exchange_finish(*refs):
    for cp in _sib_exchange_copies(*refs):
        cp.wait()


def _sib_allgather_start(*refs):
    *o_refs, send_sems, recv_sems = refs
    x, y, c, _ = _place()
    for i, o_ref in enumerate(o_refs):
        half = o_ref.shape[0] // 2
        mine = o_ref.at[pl.ds(pl.multiple_of(c * half, 8), half)]
        _remote(mine, mine, send_sems, recv_sems, i, (x, y, 1 - c)).start()


def _sib_allgather_finish(*refs):
    *o_refs, send_sems, recv_sems = refs
    x, y, c, _ = _place()
    for i, o_ref in enumerate(o_refs):
        half = o_ref.shape[0] // 2
        mine = o_ref.at[pl.ds(pl.multiple_of(c * half, 8), half)]
        got = o_ref.at[pl.ds(pl.multiple_of((1 - c) * half, 8), half)]
        _remote(got, got, send_sems, recv_sems, i, (x, y, 1 - c)).wait_recv()
        _remote(mine, mine, send_sems, recv_sems, i, (x, y, 1 - c)).wait_send()


class _Rider:
    def __init__(self, args, start, finish, inplace=1, mid=None):
        self.args, self.start, self.finish, self.inplace, self.mid = list(args), start, finish, inplace, mid


def _gather_rider(buf):
    return None if buf is None else _Rider([buf], _gather_start, _gather_finish, mid=(_gather_forward, _gather_drain))


def _exchange_rider(*parts):
    n = len(parts)
    assert 3 * n <= GATHER_SEMS

    def run(fn):
        def go(*refs):
            sems = refs[2 * n:]
            for i, (_, _, r0, nr) in enumerate(parts):
                fn(refs[i], refs[n + i], *sems, rows=pl.ds(r0, nr), base=3 * i)
        return go

    return _Rider([p[0] for p in parts] + [p[1] for p in parts], run(_exchange_start), run(_exchange_finish), inplace=n)


def _sib_exchange_rider(g):
    landing = lax.empty((N_CHIPS, g.shape[1] // 2, g.shape[2]), g.dtype)
    return _Rider([g, landing], _sib_exchange_start, _sib_exchange_finish)


def _sib_allgather_rider(*fulls):
    return _Rider(fulls, _sib_allgather_start, _sib_allgather_finish, inplace=len(fulls))


class _Hosted:
    def __init__(self, rider, n_in, n_out):
        self.rider = rider
        self.on = rider is not None
        self.args = rider.args if self.on else []
        k = len(self.args)
        p = self.p = rider.inplace if self.on else 0
        self.alias = {n_in + k - p + i: n_out + i for i in range(p)}
        self.in_specs = [HBM] * k
        self.out_specs = [HBM] * p
        self.out_shape = [jax.ShapeDtypeStruct(a.shape, a.dtype) for a in self.args[k - p:]]
        self.scratch = [pltpu.SemaphoreType.DMA((GATHER_SEMS,)), pltpu.SemaphoreType.DMA((GATHER_SEMS,))] if self.on else []

    def split(self, refs, n_in, n_out):
        refs = list(refs)
        if not self.on:
            return refs[:n_in], refs[n_in:n_in + n_out], refs[n_in + n_out:], None
        k, p = len(self.args), self.p
        ins, outs = refs[:n_in], refs[n_in + k:n_in + k + n_out]
        rest = refs[n_in + k + n_out + p:]
        rrefs = refs[n_in:n_in + k - p] + refs[n_in + k + n_out:n_in + k + n_out + p] + [rest[-2], rest[-1]]
        return ins, outs, rest[:-2], rrefs

    def run(self, rrefs, step, n_steps, compute):
        if rrefs is None:
            return compute()

        mid = self.rider.mid
        mid_step = (3 * n_steps) // 4
        two_stage = mid is not None and 0 < mid_step < n_steps - 1

        @pl.when(step == 0)
        def _():
            self.rider.start(*rrefs)

        compute()

        if two_stage:
            @pl.when(step == mid_step)
            def _():
                mid[0](*rrefs)

        @pl.when(step == n_steps - 1)
        def _():
            (mid[1] if two_stage else self.rider.finish)(*rrefs)


def _mm_nn(a, w3, *, name, out_dtype=F32, norm_g=None, resid=None, relu2=False, tm=512, gather=None):
    M, K = a.shape
    ns, _, n = w3.shape
    N = ns * n
    tm = min(tm, M)
    has_norm = norm_g is not None
    has_res = resid is not None
    n_in = 2 + has_norm + has_res
    n_out = (2 if relu2 else 1) + has_norm
    host = _Hosted(_gather_rider(gather), n_in, n_out)

    def body(*refs):
        ins, outs, _, gref = host.split(refs, n_in, n_out)
        a_ref, w_ref = ins[0], ins[1]
        g_ref = ins[2] if has_norm else None
        r_ref = ins[-1] if has_res else None

        def compute():
            if has_norm:
                xv = a_ref[...]
                rs = lax.rsqrt(jnp.mean(xv * xv, axis=-1, keepdims=True) + EPS)
                ab = (xv * rs * g_ref[...]).astype(BF16)
                outs[-1][...] = ab
            else:
                ab = a_ref[...]
            for s in range(ns):
                acc = jnp.dot(ab, w_ref[s], preferred_element_type=F32)
                sl = slice(s * n, (s + 1) * n)
                if relu2:
                    outs[0][:, sl] = acc.astype(BF16)
                    rl = jnp.maximum(acc, 0.0)
                    outs[1][:, sl] = (rl * rl).astype(BF16)
                elif has_res:
                    outs[0][:, sl] = r_ref[:, sl] + acc
                else:
                    outs[0][:, sl] = acc.astype(out_dtype)

        host.run(gref, pl.program_id(0), M // tm, compute)

    row = lambda w: pl.BlockSpec((tm, w), lambda i: (i, 0))
    in_specs = [row(K), pl.BlockSpec((ns, K, n), lambda i: (0, 0, 0))]
    args = [a, w3]
    if has_norm:
        in_specs.append(pl.BlockSpec((1, K), lambda i: (0, 0)))
        args.append(norm_g.reshape(1, K))
    if has_res:
        in_specs.append(row(N))
        args.append(resid)
    if relu2:
        out_shape = [jax.ShapeDtypeStruct((M, N), BF16), jax.ShapeDtypeStruct((M, N), BF16)]
        out_specs = [row(N), row(N)]
    else:
        out_shape = [jax.ShapeDtypeStruct((M, N), F32 if has_res else out_dtype)]
        out_specs = [row(N)]
    if has_norm:
        out_shape.append(jax.ShapeDtypeStruct((M, K), BF16))
        out_specs.append(row(K))
    res = pl.pallas_call(body, grid=(M // tm,), in_specs=in_specs + host.in_specs, out_specs=out_specs + host.out_specs,
                         out_shape=out_shape + host.out_shape, scratch_shapes=host.scratch, input_output_aliases=host.alias,
                         name=name, compiler_params=_cp(("arbitrary",) if host.on else ("parallel",)))(*args, *host.args)
    return res if len(res) > 1 else res[0]


def _mm_nt(g, w3, *, name, out_dtype=BF16, up=None, norm_x=None, norm_g=None, dres=None, tm=512):
    M = g.shape[0]
    ns, K, n = w3.shape
    tm = min(tm, M)
    has_up = up is not None
    has_norm = norm_x is not None
    has_res = dres is not None

    def body(*refs):
        refs = list(refs)
        g_ref, w_ref = refs[0], refs[1]
        pos = 2
        if has_up:
            up_ref = refs[pos]
            pos += 1
        if has_norm:
            x_ref, gn_ref = refs[pos], refs[pos + 1]
            pos += 2
        if has_res:
            r_ref = refs[pos]
            pos += 1
        outs = refs[pos:]
        acc = None
        for s in range(ns):
            part = lax.dot_general(g_ref[:, s * n:(s + 1) * n], w_ref[s], NT, preferred_element_type=F32)
            acc = part if acc is None else acc + part
        if has_up:
            outs[0][...] = (acc * (2.0 * jnp.maximum(up_ref[...].astype(F32), 0.0))).astype(BF16)
        elif has_norm:
            xv = x_ref[...]
            rs = lax.rsqrt(jnp.mean(xv * xv, axis=-1, keepdims=True) + EPS)
            xn = xv * rs
            dxn = acc * gn_ref[...]
            dx = rs * (dxn - xn * jnp.mean(dxn * xn, axis=-1, keepdims=True))
            if has_res:
                dx = dx + r_ref[...]
            outs[0][...] = dx
            outs[1][...] = dx.astype(BF16)

            @pl.when(pl.program_id(0) == 0)
            def _():
                outs[2][...] = jnp.zeros_like(outs[2])

            outs[2][...] += jnp.sum(acc * xn, axis=0, keepdims=True)
        else:
            outs[0][...] = acc.astype(out_dtype)

    row = lambda w: pl.BlockSpec((tm, w), lambda i: (i, 0))
    in_specs = [row(ns * n), pl.BlockSpec((ns, K, n), lambda i: (0, 0, 0))]
    args = [g, w3]
    if has_up:
        in_specs.append(row(K))
        args.append(up)
    if has_norm:
        in_specs += [row(K), pl.BlockSpec((1, K), lambda i: (0, 0))]
        args += [norm_x, norm_g.reshape(1, K)]
    if has_res:
        in_specs.append(row(K))
        args.append(dres)
    if has_norm:
        out_shape = [jax.ShapeDtypeStruct((M, K), F32), jax.ShapeDtypeStruct((M, K), BF16),
                     jax.ShapeDtypeStruct((1, K), F32)]
        out_specs = [row(K), row(K), pl.BlockSpec((1, K), lambda i: (0, 0))]
        sem = ("arbitrary",)
    else:
        out_shape = [jax.ShapeDtypeStruct((M, K), BF16 if has_up else out_dtype)]
        out_specs = [row(K)]
        sem = ("parallel",)
    res = pl.pallas_call(body, grid=(M // tm,), in_specs=in_specs, out_specs=out_specs, out_shape=out_shape,
                         name=name, compiler_params=_cp(sem))(*args)
    return res if len(res) > 1 else res[0]


def _mm_tn(a, g, ns, *, name, tk=512, tm=4096, packed=None, rider=None):
    M, K = a.shape
    n = g.shape[1] // ns
    tm = min(tm, M)
    tk = min(tk, K)
    nk, nm = K // tk, M // tm
    n_in = 3 if (packed is not None and packed[0] is not None) else 2
    host = _Hosted(rider, n_in, 1)

    def body(*refs):
        ins, outs, _, rrefs = host.split(refs, n_in, 1)
        a_ref, g_ref, o_ref = ins[0], ins[1], outs[0]

        def compute():
            @pl.when(pl.program_id(2) == 0)
            def _():
                o_ref[...] = jnp.zeros_like(o_ref)

            o_ref[0] += lax.dot_general(a_ref[...], g_ref[...], TN, preferred_element_type=F32)

        step = (pl.program_id(0) * nk + pl.program_id(1)) * nm + pl.program_id(2)
        host.run(rrefs, step, ns * nk * nm, compute)

    in_specs = [pl.BlockSpec((tm, tk), lambda s, k, m: (m, k)), pl.BlockSpec((tm, n), lambda s, k, m: (m, s))]
    args = [a, g]
    alias = {}
    if packed is None:
        out_spec = pl.BlockSpec((1, tk, n), lambda s, k, m: (s, k, 0))
        out_shape = jax.ShapeDtypeStruct((ns, K, n), F32)
    else:
        buf, rows, off = packed
        per_chip = K * ns // N_CHIPS
        assert n == ROW and per_chip % tk == 0 and off % tk == 0
        if ns == N_CHIPS:
            out_spec = pl.BlockSpec((1, tk, n), lambda s, k, m: (s, off // tk + k, 0))
        else:
            kpc = per_chip // tk
            out_spec = pl.BlockSpec((1, tk, n), lambda s, k, m: (k // kpc, off // tk + k % kpc, 0))
        out_shape = jax.ShapeDtypeStruct((N_CHIPS, rows, ROW), F32)
        if buf is not None:
            in_specs.append(HBM)
            args.append(buf)
            alias = {2: 0}
    sem = ("arbitrary",) * 3 if host.on else ("parallel", "parallel", "arbitrary")
    res = pl.pallas_call(
        body, grid=(ns, nk, nm), in_specs=in_specs + host.in_specs, out_specs=[out_spec] + host.out_specs,
        out_shape=[out_shape] + host.out_shape, scratch_shapes=host.scratch, name=name,
        input_output_aliases={**alias, **host.alias}, compiler_params=_cp(sem))(*args, *host.args)
    return res if host.on else res[0]


def _final(act, w_down, x, gain, target, *, name="mlp_down_final", tr=512):
    S, Dm = x.shape
    tr = min(tr, S)
    Kf = act.shape[1]

    def body(a_ref, w_ref, x_ref, g_ref, t_ref, loss_ref, dx_ref, dxb_ref, dg_ref):
        @pl.when(pl.program_id(0) == 0)
        def _():
            loss_ref[...] = jnp.zeros_like(loss_ref)
            dg_ref[...] = jnp.zeros_like(dg_ref)

        xv = x_ref[...] + jnp.dot(a_ref[...], w_ref[0], preferred_element_type=F32)
        gv = g_ref[...]
        rs = lax.rsqrt(jnp.mean(xv * xv, axis=-1, keepdims=True) + EPS)
        xn = xv * rs
        err = xn * gv - t_ref[...]
        loss_ref[...] += 0.5 * jnp.sum(jnp.mean(err * err, axis=-1, keepdims=True), axis=0, keepdims=True)
        dout = err * (1.0 / Dm)
        dg_ref[...] += jnp.sum(dout * xn, axis=0, keepdims=True)
        dxn = dout * gv
        dx = rs * (dxn - xn * jnp.mean(dxn * xn, axis=-1, keepdims=True))
        dx_ref[...] = dx
        dxb_ref[...] = dx.astype(BF16)

    row = pl.BlockSpec((tr, Dm), lambda i: (i, 0))
    return pl.pallas_call(
        body, grid=(S // tr,),
        in_specs=[pl.BlockSpec((tr, Kf), lambda i: (i, 0)), pl.BlockSpec((1, Kf, Dm), lambda i: (0, 0, 0)), row,
                  pl.BlockSpec((1, Dm), lambda i: (0, 0)), row],
        out_specs=[pl.BlockSpec((1, 1), lambda i: (0, 0)), row, row, pl.BlockSpec((1, Dm), lambda i: (0, 0))],
        out_shape=[jax.ShapeDtypeStruct((1, 1), F32), jax.ShapeDtypeStruct((S, Dm), F32),
                   jax.ShapeDtypeStruct((S, Dm), BF16), jax.ShapeDtypeStruct((1, Dm), F32)],
        name=name, compiler_params=_cp(("arbitrary",)))(act, w_down, x, gain.reshape(1, Dm), target)


def _rope_tables(positions):
    half = ROPE_DIM // 2
    inv_freq = ROPE_THETA ** (-2.0 * jnp.arange(half, dtype=F32) / ROPE_DIM)
    ang = positions.astype(F32)[:, None] * inv_freq
    cos, sin = jnp.cos(ang), jnp.sin(ang)
    S = positions.shape[0]
    ones = jnp.ones((S, HEAD_DIM - ROPE_DIM), F32)
    cos64 = jnp.concatenate([cos, cos, ones], axis=1)
    sin64 = jnp.concatenate([-sin, sin, 0.0 * ones], axis=1)
    return jnp.tile(cos64, (1, 2)), jnp.tile(sin64, (1, 2))


def _rope_partner(t):
    lane = lax.broadcasted_iota(jnp.int32, t.shape, 1)
    low = (lane & (HEAD_DIM - 1)) < (ROPE_DIM // 2)
    return jnp.where(low, pltpu.roll(t, LANES - ROPE_DIM // 2, 1), pltpu.roll(t, ROPE_DIM // 2, 1))


def _qk_prep(p, cos_t, sin_t, *, name="qk_prep", tr=256, gather=None):
    S = p.shape[0]
    tr = min(tr, S)
    scale = HEAD_DIM ** -0.5
    host = _Hosted(_gather_rider(gather), 3, 4)

    def body(*refs):
        ins, outs, _, gref = host.split(refs, 3, 4)
        host.run(gref, pl.program_id(0), S // tr, lambda: inner(*ins, *outs))

    def inner(p_ref, c_ref, s_ref, q_ref, k_ref, v_ref, va_ref):
        cs, sn = c_ref[...], s_ref[...]
        lane = lax.broadcasted_iota(jnp.int32, (tr, LANES), 1)
        lo = lane < HEAD_DIM
        for c in range(Q_W // LANES):
            t = p_ref[:, c * LANES:(c + 1) * LANES]
            q_ref[:, c * LANES:(c + 1) * LANES] = ((t * cs + _rope_partner(t) * sn) * scale).astype(BF16)
        for c in range(KV_W // LANES):
            t = p_ref[:, Q_W + c * LANES:Q_W + (c + 1) * LANES]
            kc = t * cs + _rope_partner(t) * sn
            vc = p_ref[:, Q_W + KV_W + c * LANES:Q_W + KV_W + (c + 1) * LANES]
            for arr, ref in ((kc, k_ref), (vc, v_ref)):
                sw = pltpu.roll(arr, HEAD_DIM, 1)
                ref[:, (2 * c) * LANES:(2 * c + 1) * LANES] = jnp.where(lo, arr, sw).astype(BF16)
                ref[:, (2 * c + 1) * LANES:(2 * c + 2) * LANES] = jnp.where(lo, sw, arr).astype(BF16)
            sw = pltpu.roll(vc, HEAD_DIM, 1)
            for k, aug in enumerate((jnp.where(lo, vc, 1.0), jnp.where(lo, 1.0, sw), jnp.where(lo, sw, 1.0), jnp.where(lo, 1.0, vc))):
                va_ref[:, (4 * c + k) * LANES:(4 * c + k + 1) * LANES] = aug.astype(BF16)

    row = lambda w: pl.BlockSpec((tr, w), lambda i: (i, 0))
    return pl.pallas_call(
        body, grid=(S // tr,), in_specs=[row(ATTN_IN_W), row(LANES), row(LANES)] + host.in_specs,
        out_specs=[row(Q_W), row(2 * KV_W), row(2 * KV_W), row(4 * KV_W)] + host.out_specs,
        out_shape=[jax.ShapeDtypeStruct((S, Q_W), BF16), jax.ShapeDtypeStruct((S, 2 * KV_W), BF16),
                   jax.ShapeDtypeStruct((S, 2 * KV_W), BF16), jax.ShapeDtypeStruct((S, 4 * KV_W), BF16)] + host.out_shape,
        scratch_shapes=host.scratch, input_output_aliases=host.alias,
        name=name, compiler_params=_cp(("arbitrary",) if host.on else ("parallel",)))(p, cos_t, sin_t, *host.args)


def _qk_prep_bwd(dq, dk, dv, dmq, cos_t, sin_t, *, name="qk_prep_bwd", tr=256):
    S = dq.shape[0]
    tr = min(tr, S)

    def body(dq_ref, dk_ref, dv_ref, dmq_ref, c_ref, s_ref, o_ref):
        cs, sn = c_ref[...], s_ref[...]
        for c in range(Q_W // LANES):
            t = dq_ref[:, c * LANES:(c + 1) * LANES]
            o_ref[:, c * LANES:(c + 1) * LANES] = (t * cs - _rope_partner(t) * sn).astype(BF16)
        for c in range(KV_W // LANES):
            t = dk_ref[:, c * LANES:(c + 1) * LANES]
            o_ref[:, Q_W + c * LANES:Q_W + (c + 1) * LANES] = (t * cs - _rope_partner(t) * sn).astype(BF16)
        o_ref[:, Q_W + KV_W:Q_W + 2 * KV_W] = dv_ref[...].astype(BF16)
        o_ref[:, Q_W + 2 * KV_W:] = dmq_ref[...]

    row = lambda w: pl.BlockSpec((tr, w), lambda i: (i, 0))
    return pl.pallas_call(
        body, grid=(S // tr,), in_specs=[row(Q_W), row(KV_W), row(KV_W), row(MEM_W), row(LANES), row(LANES)],
        out_specs=row(ATTN_IN_W), out_shape=jax.ShapeDtypeStruct((S, ATTN_IN_W), BF16),
        name=name, compiler_params=_cp(("parallel",)))(dq, dk, dv, dmq, cos_t, sin_t)


def _band(n, S):
    start = pl.multiple_of(jnp.clip((n - 1) * BLOCK, 0, S - 3 * BLOCK), BLOCK)
    qi = lax.broadcasted_iota(jnp.int32, (BLOCK, 3 * BLOCK), 0) + n * BLOCK
    ki = lax.broadcasted_iota(jnp.int32, (BLOCK, 3 * BLOCK), 1) + start
    return start, jnp.abs(ki - qi) <= WINDOW


def _head_operand(ref, h, lo):
    c = h // 2
    t = ref[:, c * LANES:(c + 1) * LANES].astype(F32)
    return jnp.where(lo if h % 2 == 0 else jnp.logical_not(lo), t, 0.0).astype(BF16)


GROUP = ATTN_HEADS // ATTN_KV_HEADS
EVENS_FIRST = (0, 2, 1, 3)


def _attn_fwd(q, kd, va, sinks, *, name="attn_fwd", gather=None):
    S = q.shape[0]
    host = _Hosted(_gather_rider(gather), 4, 2)

    def body(*refs):
        ins, outs, scr, gref = host.split(refs, 4, 2)
        host.run(gref, pl.program_id(0), S // BLOCK, lambda: inner(*ins, *outs, *scr))

    def inner(sink_ref, q_ref, k_ref, va_ref, o_ref, lse_ref, p_scr):
        n = pl.program_id(0)
        start, mask = _band(n, S)
        lane = lax.broadcasted_iota(jnp.int32, (BLOCK, LANES), 1)
        lo = lane < HEAD_DIM
        rows = pl.ds(start, 3 * BLOCK)
        scores = []
        for g in range(ATTN_KV_HEADS):
            qst = jnp.concatenate([_head_operand(q_ref, GROUP * g + j, lo) for j in EVENS_FIRST], axis=0)
            scores.append(lax.dot_general(qst, k_ref[rows, g * LANES:(g + 1) * LANES], NT, preferred_element_type=F32))
        ms = {}
        for g in range(ATTN_KV_HEADS):
            for pos, j in enumerate(EVENS_FIRST):
                h = GROUP * g + j
                s = jnp.where(mask, scores[g][pos * BLOCK:(pos + 1) * BLOCK], NEG)
                ms[h] = jnp.maximum(jnp.max(s, axis=-1, keepdims=True), sink_ref[h])
                p_scr[(GROUP * g + pos) * BLOCK:(GROUP * g + pos + 1) * BLOCK, :] = jnp.exp(s - ms[h]).astype(BF16)
        pvs = {}
        for g in range(ATTN_KV_HEADS):
            for par in range(2):
                r0 = (GROUP * g + 2 * par) * BLOCK
                pvs[g, par] = jnp.dot(p_scr[r0:r0 + 2 * BLOCK, :], va_ref[rows, (2 * g + par) * LANES:(2 * g + par + 1) * LANES],
                                      preferred_element_type=F32)
        lse_blk = jnp.zeros((BLOCK, LANES), F32)
        for g in range(ATTN_KV_HEADS):
            outs = {}
            for par in range(2):
                for k in range(2):
                    j = EVENS_FIRST[2 * par + k]
                    h = GROUP * g + j
                    pv = pvs[g, par][k * BLOCK:(k + 1) * BLOCK]
                    den = pltpu.roll(pv, HEAD_DIM, 1) + jnp.exp(sink_ref[h] - ms[h])
                    outs[j] = pv * (1.0 / den)
                    l = den[:, par * HEAD_DIM:par * HEAD_DIM + 1]
                    lse_blk = jnp.where(lane == h, ms[h] + jnp.log(l), lse_blk)
            for jj in range(2):
                o_ref[:, (2 * g + jj) * LANES:(2 * g + jj + 1) * LANES] = jnp.where(lo, outs[2 * jj], outs[2 * jj + 1]).astype(BF16)
        lse_ref[...] = lse_blk

    full = lambda w: pl.BlockSpec((S, w), lambda i: (0, 0))
    return pl.pallas_call(
        body, grid=(S // BLOCK,),
        in_specs=[pl.BlockSpec(memory_space=pltpu.SMEM), pl.BlockSpec((BLOCK, Q_W), lambda i: (i, 0)),
                  full(2 * KV_W), full(4 * KV_W)] + host.in_specs,
        out_specs=[pl.BlockSpec((BLOCK, Q_W), lambda i: (i, 0)), pl.BlockSpec((BLOCK, LANES), lambda i: (i, 0))] + host.out_specs,
        out_shape=[jax.ShapeDtypeStruct((S, MIX_OUT_W), BF16), jax.ShapeDtypeStruct((S, LANES), F32)] + host.out_shape,
        scratch_shapes=[pltpu.VMEM((ATTN_HEADS * BLOCK, 3 * BLOCK), BF16)] + host.scratch, input_output_aliases=host.alias,
        name=name, compiler_params=_cp(("arbitrary",) if host.on else ("parallel",)))(sinks, q, kd, va, *host.args)


def _attn_bwd(q, kd, vd, ao, lse, sinks, dcat, *, name="attn_bwd", rider=None):
    S = q.shape[0]
    scale = HEAD_DIM ** -0.5
    host = _Hosted(rider, 7, 4)

    def body(*refs):
        ins, outs, scr, rrefs = host.split(refs, 7, 4)
        host.run(rrefs, pl.program_id(0), S // BLOCK, lambda: inner(*ins, *outs, *scr))

    def inner(sink_ref, q_ref, k_ref, v_ref, ao_ref, lse_ref, do_ref, dq_ref, dk_ref, dv_ref, ds_ref, p_scr, dsb_scr):
        n = pl.program_id(0)

        @pl.when(n == 0)
        def _():
            dk_ref[...] = jnp.zeros_like(dk_ref)
            dv_ref[...] = jnp.zeros_like(dv_ref)
            ds_ref[...] = jnp.zeros_like(ds_ref)

        start, mask = _band(n, S)
        lane = lax.broadcasted_iota(jnp.int32, (BLOCK, LANES), 1)
        lo = lane < HEAD_DIM
        lane3 = lax.broadcasted_iota(jnp.int32, (3 * BLOCK, LANES), 1)
        row8 = lax.broadcasted_iota(jnp.int32, (8, LANES), 0)
        lane8 = lax.broadcasted_iota(jnp.int32, (8, LANES), 1)
        dsink = jnp.zeros((8, LANES), F32)
        lse_blk = lse_ref[...]
        rows = pl.ds(start, 3 * BLOCK)
        lses, deltas = {}, {}
        for c in range(Q_W // LANES):
            prod = do_ref[:, c * LANES:(c + 1) * LANES].astype(F32) * ao_ref[:, c * LANES:(c + 1) * LANES].astype(F32)
            for k in range(2):
                h = 2 * c + k
                deltas[h] = jnp.sum(jnp.where(lo if k == 0 else jnp.logical_not(lo), prod, 0.0), axis=1, keepdims=True)
                lses[h] = jnp.sum(jnp.where(lane == h, lse_blk, 0.0), axis=1, keepdims=True)
                val = -jnp.sum(jnp.exp(sink_ref[h] - lses[h]) * deltas[h], axis=0, keepdims=True)
                dsink = dsink + jnp.where((row8 == 0) & (lane8 == h), val, 0.0)
        stack = lambda ref, g: jnp.concatenate([_head_operand(ref, GROUP * g + j, lo) for j in range(GROUP)], axis=0)
        ss, dps = [], []
        for g in range(ATTN_KV_HEADS):
            ss.append(lax.dot_general(stack(q_ref, g), k_ref[rows, g * LANES:(g + 1) * LANES], NT, preferred_element_type=F32))
            dps.append(lax.dot_general(stack(do_ref, g), v_ref[rows, g * LANES:(g + 1) * LANES], NT, preferred_element_type=F32))
        for g in range(ATTN_KV_HEADS):
            for j in range(GROUP):
                h = GROUP * g + j
                r = slice(j * BLOCK, (j + 1) * BLOCK)
                hr = slice(h * BLOCK, (h + 1) * BLOCK)
                p = jnp.exp(jnp.where(mask, ss[g][r], NEG) - lses[h])
                p_scr[hr, :] = p.astype(BF16)
                dsb_scr[hr, :] = (p * (dps[g][r] - deltas[h])).astype(BF16)
        for g in range(ATTN_KV_HEADS):
            cols = slice((g // 2) * LANES, (g // 2 + 1) * LANES)
            gr = slice(GROUP * g * BLOCK, GROUP * (g + 1) * BLOCK)
            dsg = dsb_scr[gr, :]
            dqs = jnp.dot(dsg, k_ref[rows, g * LANES:(g + 1) * LANES], preferred_element_type=F32) * scale
            for jj in range(2):
                dq_ref[:, (2 * g + jj) * LANES:(2 * g + jj + 1) * LANES] = jnp.where(
                    lo, dqs[(2 * jj) * BLOCK:(2 * jj + 1) * BLOCK], dqs[(2 * jj + 1) * BLOCK:(2 * jj + 2) * BLOCK])
            half = (lane3 < HEAD_DIM) if g % 2 == 0 else (lane3 >= HEAD_DIM)
            dkr = lax.dot_general(dsg, stack(q_ref, g), TN, preferred_element_type=F32)
            dk_ref[rows, cols] += jnp.where(half, dkr + pltpu.roll(dkr, HEAD_DIM, 1), 0.0)
            dvr = lax.dot_general(p_scr[gr, :], stack(do_ref, g), TN, preferred_element_type=F32)
            dv_ref[rows, cols] += jnp.where(half, dvr + pltpu.roll(dvr, HEAD_DIM, 1), 0.0)
        ds_ref[...] += dsink

    full = lambda w: pl.BlockSpec((S, w), lambda i: (0, 0))
    blk = lambda w: pl.BlockSpec((BLOCK, w), lambda i: (i, 0))
    return pl.pallas_call(
        body, grid=(S // BLOCK,),
        in_specs=[pl.BlockSpec(memory_space=pltpu.SMEM), blk(Q_W), full(2 * KV_W), full(2 * KV_W), blk(Q_W), blk(LANES), blk(Q_W)]
        + host.in_specs,
        out_specs=[blk(Q_W), full(KV_W), full(KV_W), pl.BlockSpec((8, LANES), lambda i: (0, 0))] + host.out_specs,
        out_shape=[jax.ShapeDtypeStruct((S, Q_W), F32), jax.ShapeDtypeStruct((S, KV_W), F32),
                   jax.ShapeDtypeStruct((S, KV_W), F32), jax.ShapeDtypeStruct((8, LANES), F32)] + host.out_shape,
        scratch_shapes=[pltpu.VMEM((ATTN_HEADS * BLOCK, 3 * BLOCK), BF16), pltpu.VMEM((ATTN_HEADS * BLOCK, 3 * BLOCK), BF16)]
        + host.scratch, input_output_aliases=host.alias,
        name=name, compiler_params=_cp(("arbitrary",)))(sinks, q, kd, vd, ao, lse, dcat, *host.args)


def _mem_probs(q_ref, kv_ref, h):
    scale = MEM_HEAD_DIM ** -0.5
    qh = q_ref[:, h * LANES:(h + 1) * LANES].astype(BF16)
    s = lax.dot_general(qh, kv_ref[:, h * LANES:(h + 1) * LANES], NT, preferred_element_type=F32) * scale
    m = jnp.max(s, axis=-1, keepdims=True)
    pe = jnp.exp(s - m)
    return qh, pe * (1.0 / jnp.sum(pe, axis=-1, keepdims=True))


def _memattn_fwd(p, qblk, kv, cat, *, name="memattn_fwd", tr=512):
    S = p.shape[0]
    tr = min(tr, S)

    def body(q_ref, kv_ref, cat_ref, o_ref):
        for h in range(MEM_HEADS):
            _, pr = _mem_probs(q_ref, kv_ref, h)
            o = jnp.dot(pr.astype(BF16), kv_ref[:, MEM_W + h * LANES:MEM_W + (h + 1) * LANES], preferred_element_type=F32)
            o_ref[:, h * LANES:(h + 1) * LANES] = o.astype(BF16)

    return pl.pallas_call(
        body, grid=(S // tr,),
        in_specs=[pl.BlockSpec((tr, MEM_W), lambda i: (i, qblk)), pl.BlockSpec((MEM_LEN, 2 * MEM_W), lambda i: (0, 0)), HBM],
        out_specs=pl.BlockSpec((tr, MEM_W), lambda i: (i, Q_W // MEM_W)),
        out_shape=jax.ShapeDtypeStruct((S, MIX_OUT_W), BF16), input_output_aliases={2: 0},
        name=name, compiler_params=_cp(("parallel",)))(p, kv, cat)


def _memattn_bwd(p, qblk, kv, dcat, *, name="memattn_bwd", tr=512, rider=None):
    S = p.shape[0]
    tr = min(tr, S)
    scale = MEM_HEAD_DIM ** -0.5
    host = _Hosted(rider, 3, 2)

    def body(*refs):
        ins, outs, _, rrefs = host.split(refs, 3, 2)
        host.run(rrefs, pl.program_id(0), S // tr, lambda: inner(*ins, *outs))

    def inner(q_ref, kv_ref, do_ref, dq_ref, dkv_ref):
        @pl.when(pl.program_id(0) == 0)
        def _():
            dkv_ref[...] = jnp.zeros_like(dkv_ref)

        for h in range(MEM_HEADS):
            qh, pr = _mem_probs(q_ref, kv_ref, h)
            doh = do_ref[:, h * LANES:(h + 1) * LANES]
            dp = lax.dot_general(doh, kv_ref[:, MEM_W + h * LANES:MEM_W + (h + 1) * LANES], NT, preferred_element_type=F32)
            delta = jnp.sum(pr * dp, axis=-1, keepdims=True)
            dsb = (pr * (dp - delta) * scale).astype(BF16)
            dq = jnp.dot(dsb, kv_ref[:, h * LANES:(h + 1) * LANES], preferred_element_type=F32)
            dq_ref[:, h * LANES:(h + 1) * LANES] = dq.astype(BF16)
            dkv_ref[:, h * LANES:(h + 1) * LANES] += lax.dot_general(dsb, qh, TN, preferred_element_type=F32)
            dkv_ref[:, MEM_W + h * LANES:MEM_W + (h + 1) * LANES] += lax.dot_general(
                pr.astype(BF16), doh, TN, preferred_element_type=F32)

    return pl.pallas_call(
        body, grid=(S // tr,),
        in_specs=[pl.BlockSpec((tr, MEM_W), lambda i: (i, qblk)), pl.BlockSpec((MEM_LEN, 2 * MEM_W), lambda i: (0, 0)),
                  pl.BlockSpec((tr, MEM_W), lambda i: (i, Q_W // MEM_W))] + host.in_specs,
        out_specs=[pl.BlockSpec((tr, MEM_W), lambda i: (i, 0)), pl.BlockSpec((MEM_LEN, 2 * MEM_W), lambda i: (0, 0))]
        + host.out_specs,
        out_shape=[jax.ShapeDtypeStruct((S, MEM_W), BF16), jax.ShapeDtypeStruct((MEM_LEN, 2 * MEM_W), F32)] + host.out_shape,
        scratch_shapes=host.scratch, input_output_aliases=host.alias,
        name=name, compiler_params=_cp(("arbitrary",)))(p, kv, dcat, *host.args)


def _sqrt(v):
    return jnp.where(v > 0.0, v * lax.rsqrt(v), 0.0)


def _sigmoid(z):
    return 1.0 / (1.0 + jnp.exp(-z))


def _one_minus_exp(z, exp_z):
    poly = z * (1.0 + z * (0.5 + z * (1.0 / 6.0 + z * (1.0 / 24.0 + z * (1.0 / 120.0)))))
    return jnp.where(z > -0.1, -poly, 1.0 - exp_z)


def _softplus_neg(lam):
    z = -lam
    return jnp.maximum(z, 0.0) + jnp.log(1.0 + jnp.exp(-jnp.abs(z)))


_GELU_C = math.sqrt(2.0 / math.pi)


def _gelu(z):
    return 0.5 * z * (1.0 + jnp.tanh(_GELU_C * (z + 0.044715 * z * z * z)))


def _row_or_zero(ref, t, S):
    ok = jnp.logical_and(t >= 0, t < S)
    return jnp.where(ok, ref[pl.ds(jnp.clip(t, 0, S - 1), 1), :], 0.0)


def _shift_down(v, first):
    ri = lax.broadcasted_iota(jnp.int32, v.shape, 0)
    return jnp.where(ri == 0, first, pltpu.roll(v, 1, 0))


def _shift_up(v, last):
    T = v.shape[0]
    ri = lax.broadcasted_iota(jnp.int32, v.shape, 0)
    return jnp.where(ri == T - 1, last, pltpu.roll(v, T - 1, 0))


def _scan_chunk(a, u, reverse):
    T = a.shape[0]
    ri = lax.broadcasted_iota(jnp.int32, a.shape, 0)
    d = 1
    while d < T:
        if reverse:
            a_s, u_s, ok = pltpu.roll(a, T - d, 0), pltpu.roll(u, T - d, 0), ri < T - d
        else:
            a_s, u_s, ok = pltpu.roll(a, d, 0), pltpu.roll(u, d, 0), ri >= d
        u = jnp.where(ok, a * u_s + u, u)
        a = jnp.where(ok, a * a_s, a)
        d *= 2
    return a, u


def _conv_taps(xb_ref, t0, S):
    T = SCAN_ROWS
    x0 = xb_ref[pl.ds(t0, T), :]
    xm1 = _shift_down(x0, _row_or_zero(xb_ref, t0 - 1, S))
    nxt0 = _row_or_zero(xb_ref, t0 + T, S)
    xp1 = _shift_up(x0, nxt0)
    xp2 = _shift_up(xp1, _row_or_zero(xb_ref, t0 + T + 1, S))
    return xm1, x0, xp1, xp2


def _lru_gates(xc, w_a, b_a, w_x, b_x, sp):
    xcb = xc.astype(BF16)
    r = _sigmoid(jnp.dot(xcb, w_a, preferred_element_type=F32) + b_a)
    i = _sigmoid(jnp.dot(xcb, w_x, preferred_element_type=F32) + b_x)
    la = -LRU_C * r * sp
    a = jnp.exp(la)
    b2 = _one_minus_exp(2.0 * la, a * a)
    inv_beta = lax.rsqrt(b2)
    return r, i, a, jnp.where(b2 > 0.0, b2 * inv_beta, 0.0), inv_beta


def _lru_specs(S):
    col = lambda off: pl.BlockSpec((S, LANES), lambda n: (0, n + off), pipeline_mode=pl.Buffered(1))
    small = lambda r: pl.BlockSpec((r, LANES), lambda n: (0, n))
    wblk = pl.BlockSpec((2, 1, LANES, LANES), lambda n: (0, n, 0, 0))
    return col, small, wblk


def _lru_fwd(p, conv_w, conv_b, wa, ba, wx, bx, lam, *, name="lru_fwd"):
    S = p.shape[0]
    T = SCAN_ROWS
    nc = S // T

    def body(xb_ref, gate_ref, cw_ref, cb_ref, wa_ref, ba_ref, wx_ref, bx_ref, lam_ref, y_ref, hf_ref, hr_ref, xc_v):
        sp = _softplus_neg(lam_ref[...])
        cw = cw_ref[...]

        def fwd_step(c, h_in):
            t0 = pl.multiple_of(c * T, T)
            xm1, x0, xp1, xp2 = _conv_taps(xb_ref, t0, S)
            xc = cb_ref[...] + xm1 * cw[0:1] + x0 * cw[1:2] + xp1 * cw[2:3] + xp2 * cw[3:4]
            xc_v[pl.ds(t0, T), :] = xc
            _, i, a, beta, _ = _lru_gates(xc, wa_ref[0, 0], ba_ref[0:1], wx_ref[0, 0], bx_ref[0:1], sp[0:1])
            A, U = _scan_chunk(a, beta * (i * xc), False)
            hf_ref[pl.ds(t0, T), :] = A * h_in + U
            return hf_ref[pl.ds(t0 + T - 1, 1), :]

        lax.fori_loop(0, nc, fwd_step, jnp.zeros((1, LANES), F32))

        def rev_step(k, h_in):
            t0 = pl.multiple_of((nc - 1 - k) * T, T)
            xc = xc_v[pl.ds(t0, T), :]
            _, i, a, beta, _ = _lru_gates(xc, wa_ref[1, 0], ba_ref[1:2], wx_ref[1, 0], bx_ref[1:2], sp[1:2])
            A, U = _scan_chunk(a, beta * (i * xc), True)
            h = A * h_in + U
            hr_ref[pl.ds(t0, T), :] = h
            y_ref[pl.ds(t0, T), :] = ((hf_ref[pl.ds(t0, T), :] + h) * _gelu(gate_ref[pl.ds(t0, T), :])).astype(BF16)
            return hr_ref[pl.ds(t0, 1), :]

        lax.fori_loop(0, nc, rev_step, jnp.zeros((1, LANES), F32))

    col, small, wblk = _lru_specs(S)
    colo = lambda: pl.BlockSpec((S, LANES), lambda n: (0, n))
    return pl.pallas_call(
        body, grid=(LRU_BLOCKS,),
        in_specs=[col(0), col(LRU_BLOCKS), small(4), small(1), wblk, small(2), wblk, small(2), small(2)],
        out_specs=[colo(), colo(), colo()],
        out_shape=[jax.ShapeDtypeStruct((S, MIX_OUT_W), BF16), jax.ShapeDtypeStruct((S, D_MODEL), F32),
                   jax.ShapeDtypeStruct((S, D_MODEL), F32)],
        scratch_shapes=[pltpu.VMEM((S, LANES), F32)],
        name=name, compiler_params=_cp(("parallel",)))(p, p, conv_w, conv_b, wa, ba, wx, bx, lam)


def _lru_bwd(p, hf, hr, dcat, conv_w, conv_b, wa, ba, wx, bx, lam, *, name="lru_bwd"):
    S = p.shape[0]
    T = SCAN_ROWS
    nc = S // T

    def body(xb_ref, gate_ref, hf_ref, hr_ref, dy_ref, cw_ref, cb_ref, wa_ref, ba_ref, wx_ref, bx_ref, lam_ref,
             dxb_ref, dgate_ref, dcw_ref, dcb_ref, dwa_ref, dba_ref, dwx_ref, dbx_ref, dlam_ref, xc_v, dxc_v, dh_v):
        lam_v = lam_ref[...]
        sp = _softplus_neg(lam_v)
        cw = cw_ref[...]
        for ref in (dcw_ref, dcb_ref, dwa_ref, dba_ref, dwx_ref, dbx_ref, dlam_ref):
            ref[...] = jnp.zeros_like(ref)

        def prep_step(c, carry):
            t0 = pl.multiple_of(c * T, T)
            rows = pl.ds(t0, T)
            xm1, x0, xp1, xp2 = _conv_taps(xb_ref, t0, S)
            xc_v[rows, :] = cb_ref[...] + xm1 * cw[0:1] + x0 * cw[1:2] + xp1 * cw[2:3] + xp2 * cw[3:4]
            z = gate_ref[rows, :]
            dy = dy_ref[rows, :].astype(F32)
            th = jnp.tanh(_GELU_C * (z + 0.044715 * z * z * z))
            dgelu = 0.5 * (1.0 + th) + 0.5 * z * (1.0 - th * th) * _GELU_C * (1.0 + 3.0 * 0.044715 * z * z)
            dgate_ref[rows, :] = (dy * (hf_ref[rows, :] + hr_ref[rows, :]) * dgelu).astype(BF16)
            dh_v[rows, :] = dy * (0.5 * z * (1.0 + th))
            return carry

        lax.fori_loop(0, nc, prep_step, 0)

        def direction(d):
            h_ref = hf_ref if d == 0 else hr_ref
            w_a, w_x = wa_ref[d, 0], wx_ref[d, 0]
            b_a, b_x, sp_d = ba_ref[d:d + 1], bx_ref[d:d + 1], sp[d:d + 1]

            def step(k, carry):
                g_in, a_in = carry
                c = (nc - 1 - k) if d == 0 else k
                t0 = pl.multiple_of(c * T, T)
                rows = pl.ds(t0, T)
                xc = xc_v[rows, :]
                r, i, a, beta, inv_beta = _lru_gates(xc, w_a, b_a, w_x, b_x, sp_d)
                dh = dh_v[rows, :]
                hc = h_ref[rows, :]
                if d == 0:
                    A, U = _scan_chunk(_shift_up(a, a_in), dh, True)
                    g = A * g_in + U
                    h_nb = _shift_down(hc, _row_or_zero(h_ref, t0 - 1, S))
                    nxt = (g[0:1], a[0:1])
                else:
                    A, U = _scan_chunk(_shift_down(a, a_in), dh, False)
                    g = A * g_in + U
                    h_nb = _shift_up(hc, _row_or_zero(h_ref, t0 + T, S))
                    nxt = (g[T - 1:T], a[T - 1:T])
                da = g * h_nb
                dbeta = g * (i * xc)
                tb = g * beta
                dla = da * a - dbeta * (a * a * inv_beta)
                dzr = (dla * (-LRU_C * sp_d)) * (r * (1.0 - r))
                dzi = (tb * xc) * (i * (1.0 - i))
                dzrb, dzib, xcb = dzr.astype(BF16), dzi.astype(BF16), xc.astype(BF16)
                dwa_ref[d, 0] += lax.dot_general(xcb, dzrb, TN, preferred_element_type=F32)
                dwx_ref[d, 0] += lax.dot_general(xcb, dzib, TN, preferred_element_type=F32)
                dba_ref[d:d + 1] += jnp.sum(dzr, axis=0, keepdims=True)
                dbx_ref[d:d + 1] += jnp.sum(dzi, axis=0, keepdims=True)
                dlam_ref[d:d + 1] += jnp.sum(dla * (-LRU_C * r), axis=0, keepdims=True)
                dxc = (tb * i + lax.dot_general(dzrb, w_a, NT, preferred_element_type=F32)
                       + lax.dot_general(dzib, w_x, NT, preferred_element_type=F32))
                if d == 0:
                    dxc_v[rows, :] = dxc
                else:
                    dxc_v[rows, :] += dxc
                return nxt

            lax.fori_loop(0, nc, step, (jnp.zeros((1, LANES), F32), jnp.zeros((1, LANES), F32)))

        direction(0)
        direction(1)
        dlam_ref[...] = dlam_ref[...] * (-1.0 / (1.0 + jnp.exp(lam_v)))

        def conv_step(c, carry):
            t0 = pl.multiple_of(c * T, T)
            rows = pl.ds(t0, T)
            g0 = dxc_v[rows, :]
            gm1 = _shift_down(g0, _row_or_zero(dxc_v, t0 - 1, S))
            gm2 = _shift_down(gm1, _row_or_zero(dxc_v, t0 - 2, S))
            gp1 = _shift_up(g0, _row_or_zero(dxc_v, t0 + T, S))
            dxb_ref[rows, :] = (cw[0:1] * gp1 + cw[1:2] * g0 + cw[2:3] * gm1 + cw[3:4] * gm2).astype(BF16)
            xm1, x0, xp1, xp2 = _conv_taps(xb_ref, t0, S)
            for tap, xs in enumerate((xm1, x0, xp1, xp2)):
                dcw_ref[tap:tap + 1] += jnp.sum(g0 * xs, axis=0, keepdims=True)
            dcb_ref[...] += jnp.sum(g0, axis=0, keepdims=True)
            return carry

        lax.fori_loop(0, nc, conv_step, 0)

    col, small, wblk = _lru_specs(S)
    colo = lambda: pl.BlockSpec((S, LANES), lambda n: (0, n), pipeline_mode=pl.Buffered(1))
    return pl.pallas_call(
        body, grid=(LRU_BLOCKS,),
        in_specs=[col(0), col(LRU_BLOCKS), col(0), col(0), col(0), small(4), small(1), wblk, small(2), wblk, small(2), small(2)],
        out_specs=[colo(), colo(), small(4), small(1), wblk, small(2), wblk, small(2), small(2)],
        out_shape=[jax.ShapeDtypeStruct((S, D_MODEL), BF16), jax.ShapeDtypeStruct((S, D_MODEL), BF16),
                   jax.ShapeDtypeStruct((4, D_MODEL), F32), jax.ShapeDtypeStruct((1, D_MODEL), F32),
                   jax.ShapeDtypeStruct((2, LRU_BLOCKS, LANES, LANES), F32), jax.ShapeDtypeStruct((2, D_MODEL), F32),
                   jax.ShapeDtypeStruct((2, LRU_BLOCKS, LANES, LANES), F32), jax.ShapeDtypeStruct((2, D_MODEL), F32),
                   jax.ShapeDtypeStruct((2, D_MODEL), F32)],
        scratch_shapes=[pltpu.VMEM((S, LANES), F32), pltpu.VMEM((S, LANES), F32), pltpu.VMEM((S, LANES), F32)],
        name=name, compiler_params=_cp(("parallel",)))(p, p, hf, hr, dcat, conv_w, conv_b, wa, ba, wx, bx, lam)


PK_UP, PK_DOWN, PK_KV, PK_OUT, PK_IN = 0, 1024, 2048, 2304, 2688
PK_ROWS = {0: PK_IN, 1: PK_IN + 640}
SMALL_G_ROWS = 192
PKF_KV, PKF_SMALL = 512, 768
PKF_ROWS = PKF_SMALL + SMALL_G_ROWS


def _mlp_bwd(x, dx, dxb, saved, w_up, w_down, gain, l, rider=None, next_rider=None):
    up, act, h = saved
    pk = _mm_tn(act, dxb, 1, name=f"dw_down{l}", packed=(None, PK_ROWS[l], PK_DOWN), rider=rider)
    pk, carried = pk if rider is not None else (pk, None)
    dup = _mm_nt(dxb, w_down, up=up, name=f"d_up{l}")
    rider_up = next_rider(carried) if next_rider is not None else None
    pk = _mm_tn(h, dup, N_CHIPS, name=f"dw_up{l}", packed=(pk, PK_ROWS[l], PK_UP), rider=rider_up)
    pk, carried = pk if rider_up is not None else (pk, carried)
    dx, dxb, g_gain = _mm_nt(dup, w_up, norm_x=x, norm_g=gain, dres=dx, name=f"d_mlp_in{l}")
    return dx, dxb, pk, g_gain, carried


def _reduce_first(pk, place, tag, recv):
    return _sum_halves(pk, recv, place, name=f"sum_halves{tag}", tr=pk.shape[1] // 4)


def _sum_parts(parts, place, tag):
    return _sum_chips(parts, place, name=f"sum_chips{tag}", tr=parts.shape[1] // 2)


def _local_step(x, mem, positions, target, W, pending=None, place=None):
    cos_t, sin_t = _rope_tables(positions)
    sinks = W["attn_sinks"].reshape(ATTN_HEADS)
    G = {}

    def hosting(late, fn, *args, **kw):
        if pending is None:
            return fn(*args, **kw)
        *res, buf = fn(*args, gather=pending[late], **kw)
        if late.startswith("w_down"):
            W.setdefault("w_down", [None] * DEPTH)[int(late[-1])] = _ready(late, buf)
        elif late == "w_up":
            W["w_up"], W["w_mem_kv"] = _ready(late, buf)
        else:
            W[late] = _ready(late, buf)
        return res if len(res) > 1 else res[0]

    p0, h0 = hosting("w_out", _mm_nn, x, W["attn_w_in"], norm_g=W["mix_norm"][0], name="attn_in")
    q, kd, vd, va = hosting("w_down0", _qk_prep, p0, cos_t, sin_t)
    ao, lse = hosting("w_up", _attn_fwd, q, kd, va, sinks)
    kv0, memn = _mm_nn(mem, W["w_mem_kv"][0], norm_g=W["mem_norm"], out_dtype=BF16, name="mem_kv0", tm=256)
    kv1 = _mm_nn(memn, W["w_mem_kv"][1], out_dtype=BF16, name="mem_kv1", tm=256)
    cat0 = _memattn_fwd(p0, Q_W // MEM_W + 1, kv0, ao, name="memattn_fwd0")
    x1 = hosting("lru_w_in", _mm_nn, cat0, W["w_out"][0], resid=x, name="mix_out0")
    up0, act0, h1 = hosting("w_down1", _mm_nn, x1, W["w_up"][0], norm_g=W["mlp_norm"][0], relu2=True, name="mlp_up0")
    x2, mlp0 = _mm_nn(act0, W["w_down"][0], resid=x1, name="mlp_down0"), (up0, act0, h1)
    p1, h2 = _mm_nn(x2, W["lru_w_in"], norm_g=W["mix_norm"][1], name="lru_in")
    lru_w = (W["lru_conv_w"], W["lru_conv_b"], W["lru_wa"], W["lru_ba"], W["lru_wx"], W["lru_bx"], W["lru_lambda"])
    y, hf, hr = _lru_fwd(p1, *lru_w)
    cat1 = _memattn_fwd(p1, 2 * D_MODEL // MEM_W, kv1, y, name="memattn_fwd1")
    x3 = _mm_nn(cat1, W["w_out"][1], resid=x2, name="mix_out1")
    mlp1 = _mm_nn(x3, W["w_up"][1], norm_g=W["mlp_norm"][1], relu2=True, name="mlp_up1")
    loss, dx, dxb, G["final_norm"] = _final(mlp1[1], W["w_down"][1], x3, W["final_norm"], target)

    def put(pk, off, g):
        return pk.at[:, off:off + g.size // (N_CHIPS * ROW)].set(g.reshape(N_CHIPS, -1, ROW))

    dx, dxb, pk1, gm1, _ = _mlp_bwd(x3, dx, dxb, mlp1, W["w_up"][1], W["w_down"][1], W["mlp_norm"][1], 1)
    pk1 = _mm_tn(cat1, dxb, 1, name="dw_out1", tk=384, packed=(pk1, PK_ROWS[1], PK_OUT))
    dcat1 = _mm_nt(dxb, W["w_out"][1], name="d_mix1")
    dmq1, dkv1 = _memattn_bwd(p1, 2 * D_MODEL // MEM_W, kv1, dcat1, name="memattn_bwd1")
    dkv1b = dkv1.astype(BF16)
    pk1 = _mm_tn(memn, dkv1b, 1, name="dw_kv1", tm=256, tk=256, packed=(pk1, PK_ROWS[1], PK_KV))
    (dxb1, dgate, G["lru_conv_w"], G["lru_conv_b"], G["lru_wa"], G["lru_ba"], G["lru_wx"], G["lru_bx"],
     G["lru_lambda"]) = _lru_bwd(p1, hf, hr, dcat1, *lru_w)
    dp1 = jnp.concatenate([dxb1, dgate, dmq1], axis=1)
    pk1 = put(pk1, PK_IN, _mm_tn(h2, dp1, N_CHIPS, name="dw_lru_in"))
    dx, dxb, gx1 = _mm_nt(dp1, W["lru_w_in"], norm_x=x2, norm_g=W["mix_norm"][1], dres=dx, name="d_lru_in")
    dist = place is not None
    h1_rows = PK_ROWS[1] // 4
    kept = {}

    def first_half(recv1):
        kept["halves1"], landing1 = _reduce_first(pk1, place, "1", recv1)
        return _exchange_rider((kept["halves1"], landing1, 0, h1_rows))

    dx, dxb, pk0, gm0, landing1 = _mlp_bwd(x1, dx, dxb, mlp0, W["w_up"][0], W["w_down"][0], W["mlp_norm"][0], 0,
                                           rider=_sib_exchange_rider(pk1) if dist else None,
                                           next_rider=first_half if dist else None)
    pk0 = _mm_tn(cat0, dxb, 1, name="dw_out0", tk=384, packed=(pk0, PK_ROWS[0], PK_OUT))
    pk0 = pk0.at[:, PK_KV:PK_OUT].set(0.0)
    dcat0 = _mm_nt(dxb, W["w_out"][0], name="d_mix0")
    dmq0, dkv0, *recv0 = _memattn_bwd(p0, Q_W // MEM_W + 1, kv0, dcat0, name="memattn_bwd0",
                                      rider=_sib_exchange_rider(pk0) if dist else None)
    dkv0b = dkv0.astype(BF16)
    g_kv0 = _mm_tn(memn, dkv0b, 1, name="dw_kv0", tm=256, tk=256)
    rider = None
    if dist:
        halves0, landing0 = _reduce_first(pk0, place, "0", recv0[0])
        rider = _exchange_rider((kept["halves1"], landing1, h1_rows, h1_rows), (halves0, landing0, 0, halves0.shape[1]))
    dq, dk, dv, dsink, *parts = _attn_bwd(q, kd, vd, cat0, lse, sinks, dcat0, rider=rider)
    dp0 = _qk_prep_bwd(dq, dk, dv, dmq0, cos_t, sin_t)
    g_in = _mm_tn(h0, dp0, N_CHIPS, name="dw_attn_in",
                  rider=_sib_allgather_rider(_sum_parts(parts[0], place, "1"), _sum_parts(parts[1], place, "0")) if dist else None)
    if dist:
        g_in, pk1, pk0 = g_in
    dx, _, gx0 = _mm_nt(dp0, W["attn_w_in"], norm_x=x, norm_g=W["mix_norm"][0], dres=dx, name="d_attn_in")

    w_kv_both = jnp.concatenate([W["w_mem_kv"][0], W["w_mem_kv"][1]], axis=0)
    _, _, G["mem_norm"] = _mm_nt(jnp.concatenate([dkv0b, dkv1b], axis=1), w_kv_both, norm_x=mem, norm_g=W["mem_norm"],
                                 name="d_mem", tm=256)

    G["mix_norm"] = jnp.concatenate([gx0, gx1], axis=0)
    G["mlp_norm"] = jnp.concatenate([gm0, gm1], axis=0)
    G["attn_sinks"] = dsink[0:1, 0:ATTN_HEADS]
    small = _flat_pad(_small_grad_list(G), N_CHIPS * SMALL_G_ROWS * ROW).reshape(N_CHIPS, SMALL_G_ROWS, ROW)
    pkf = jnp.concatenate([g_in.reshape(N_CHIPS, PKF_KV, ROW), g_kv0.reshape(N_CHIPS, PKF_SMALL - PKF_KV, ROW), small], axis=1)
    return loss[0, 0], dx, G, pkf, pk0, pk1


def _comm_call(body, out_shape, n_sems, name, *args, alias=None):
    return pl.pallas_call(
        body, out_shape=out_shape, in_specs=[HBM] * len(args), out_specs=HBM,
        scratch_shapes=[pltpu.SemaphoreType.DMA((n_sems,)), pltpu.SemaphoreType.DMA((n_sems,))],
        input_output_aliases=alias or {}, name=name)(*args)


def _place_slot(shard, slot, n_slots, *, name, tr):
    R, C = shard.shape

    def body(s_ref, a_ref, o_ref):
        o_ref[0] = a_ref[...]

    return pl.pallas_call(
        body,
        grid_spec=pltpu.PrefetchScalarGridSpec(
            num_scalar_prefetch=1, grid=(R // tr,), in_specs=[pl.BlockSpec((tr, C), lambda i, s_ref: (i, 0))],
            out_specs=pl.BlockSpec((1, tr, C), lambda i, s_ref: (s_ref[0], i, 0))),
        out_shape=jax.ShapeDtypeStruct((n_slots, R, C), shard.dtype), name=name,
        compiler_params=_cp(("parallel",)))(slot, shard)


def _allgather_chips(buf, *, name):
    def body(b_ref, o_ref, send_sems, recv_sems):
        _gather_start(o_ref, send_sems, recv_sems)
        _gather_finish(o_ref, send_sems, recv_sems)

    return _comm_call(body, jax.ShapeDtypeStruct(buf.shape, buf.dtype), GATHER_SEMS, name, buf, alias={0: 0})


def _sum_halves(g, recv, place, *, name="sum_halves", tr=480):
    _, R, C = g.shape
    half = R // 2
    nblk = half // tr

    def body(pl_ref, g_ref, r_ref, o_ref, own_ref):
        v = (g_ref[...] + r_ref[...]).astype(BF16)
        o_ref[...] = v

        @pl.when(pl.program_id(1) == pl_ref[1])
        def _():
            own_ref[...] = v

    blk = pl.BlockSpec((1, tr, C), lambda i, s, p: (s, i, 0))
    return pl.pallas_call(
        body,
        grid_spec=pltpu.PrefetchScalarGridSpec(
            num_scalar_prefetch=1, grid=(nblk, N_CHIPS),
            in_specs=[pl.BlockSpec((1, tr, C), lambda i, s, p: (s, p[0] * nblk + i, 0)), blk],
            out_specs=[blk, pl.BlockSpec((1, tr, C), lambda i, s, p: (p[1], i, 0))]),
        out_shape=[jax.ShapeDtypeStruct((N_CHIPS, half, C), BF16)] * 2, name=name,
        compiler_params=_cp(("parallel", "arbitrary")))(place, g, recv)


def _sum_chips(parts, place, *, name="sum_chips", tr=480):
    _, R, C = parts.shape
    nblk = R // tr

    def body(pl_ref, p_ref, o_ref):
        acc = p_ref[0].astype(F32) + p_ref[1].astype(F32)
        o_ref[...] = (acc + p_ref[2].astype(F32)) + p_ref[3].astype(F32)

    return pl.pallas_call(
        body,
        grid_spec=pltpu.PrefetchScalarGridSpec(
            num_scalar_prefetch=1, grid=(nblk,), in_specs=[pl.BlockSpec((N_CHIPS, tr, C), lambda i, p: (0, i, 0))],
            out_specs=pl.BlockSpec((tr, C), lambda i, p: (p[0] * nblk + i, 0))),
        out_shape=jax.ShapeDtypeStruct((2 * R, C), F32), name=name, compiler_params=_cp(("parallel",)))(place, parts)


def _adamw(w, g, m, v, *, name, rider=None):
    R, C = w.shape
    tr = R // -(-R // ADAM_ROWS)
    assert R % tr == 0 and tr % 8 == 0
    bc1 = 1.0 - ADAM_B1 ** ADAM_STEP
    bc2 = 1.0 - ADAM_B2 ** ADAM_STEP
    host = _Hosted(rider, 4, 3)

    def body(*refs):
        ins, outs, _, rrefs = host.split(refs, 4, 3)
        host.run(rrefs, pl.program_id(0), R // tr, lambda: inner(*ins, *outs))

    def inner(w_ref, g_ref, m_ref, v_ref, d_ref, nm_ref, nv_ref):
        gv = g_ref[...]
        nm = ADAM_B1 * m_ref[...] + (1.0 - ADAM_B1) * gv
        nv = ADAM_B2 * v_ref[...] + (1.0 - ADAM_B2) * (gv * gv)
        d_ref[...] = -ADAM_LR * ((nm / bc1) / (_sqrt(nv / bc2) + ADAM_EPS) + ADAM_WD * w_ref[...])
        nm_ref[...] = nm
        nv_ref[...] = nv

    blk = pl.BlockSpec((tr, C), lambda i: (i, 0))
    return pl.pallas_call(
        body, grid=(R // tr,), in_specs=[blk] * 4 + host.in_specs, out_specs=[blk] * 3 + host.out_specs,
        out_shape=[jax.ShapeDtypeStruct((R, C), F32)] * 3 + host.out_shape, scratch_shapes=host.scratch,
        input_output_aliases=host.alias, name=name,
        compiler_params=_cp(("arbitrary",) if host.on else ("parallel",)))(w, g, m, v, *host.args)


ROW = 1024
BIG = ("w_mem_kv", "w_out", "w_up", "w_down", "attn_w_in", "lru_w_in")
SMALL_SHARDED = ("lru_conv_w", "lru_conv_b", "lru_ba", "lru_bx", "lru_lambda")
REPLICATED = ("mix_norm", "mlp_norm", "mem_norm", "final_norm", "attn_sinks", "lru_wa", "lru_wx")
SMALL = REPLICATED + SMALL_SHARDED
WEIGHTS = ("mix_norm", "mlp_norm", "mem_norm", "final_norm", "w_mem_kv", "w_out", "w_up", "w_down", "attn_w_in",
           "attn_sinks", "lru_w_in", "lru_conv_w", "lru_conv_b", "lru_wa", "lru_ba", "lru_wx", "lru_bx", "lru_lambda")
SMALL_W_ROWS = 32
ADAM_SMALL_ROWS = 640


def _rows(a):
    return a.reshape(-1, ROW)


def _flat_pad(parts, total):
    flat = jnp.concatenate([p.reshape(-1) for p in parts])
    return jnp.pad(flat, (0, total - flat.shape[0]))


def _pad_rows(a):
    flat = a.reshape(-1)
    n = -(-flat.shape[0] // ROW) * ROW
    return jnp.pad(flat, (0, n - flat.shape[0])).reshape(-1, ROW)


LATE = ("w_out", "w_down0", "w_up", "lru_w_in", "w_down1")


def _ready(name, full):
    if name == "w_out":
        wo = full.reshape(N_CHIPS, DEPTH, -1, D_MODEL)
        return [wo[:, l].reshape(1, MIX_OUT_W, D_MODEL) for l in range(DEPTH)]
    if name == "w_up":
        n_up = DEPTH * D_MODEL
        wu = full[:, :n_up].reshape(N_CHIPS, DEPTH, D_MODEL, D_FF // N_CHIPS)
        kv = full[:, n_up:].reshape(N_CHIPS, DEPTH, -1, D_MODEL)
        return [wu[:, l] for l in range(DEPTH)], [kv[:, l].reshape(1, D_MODEL, D_MODEL) for l in range(DEPTH)]
    if name == "lru_w_in":
        return full.reshape(N_CHIPS, D_MODEL, LRU_IN_W // N_CHIPS)
    return full.reshape(1, D_FF, D_MODEL)


def _gather_weights(P, chip1):
    bf = lambda a: _rows(a.astype(BF16))
    small = _flat_pad([P[n] for n in SMALL_SHARDED], SMALL_W_ROWS * ROW // 2)
    small_bits = lax.bitcast_convert_type(small, BF16).reshape(SMALL_W_ROWS, ROW)
    early = jnp.concatenate([bf(P["attn_w_in"]), small_bits], axis=0)
    n_in = P["attn_w_in"].size // ROW
    placed = _place_slot(early, chip1, N_CHIPS, name="place_weights", tr=early.shape[0] // 2)
    full = _allgather_chips(placed, name="allgather_weights")
    late = {"w_out": bf(P["w_out"]), "lru_w_in": bf(P["lru_w_in"]),
            "w_up": jnp.concatenate([bf(P["w_up"]), bf(P["w_mem_kv"])], axis=0),
            "w_down0": bf(P["w_down"][0]), "w_down1": bf(P["w_down"][1])}
    pending = {n: _place_slot(late[n], chip1, N_CHIPS, name=f"place_{n}", tr=late[n].shape[0] // 2) for n in LATE}
    W = {n: P[n] for n in REPLICATED}
    W["attn_w_in"] = full[:, :n_in].reshape(N_CHIPS, D_MODEL, ATTN_IN_W // N_CHIPS)
    sm = lax.bitcast_convert_type(full[:, n_in:].reshape(N_CHIPS, -1, 2), F32)
    o = 0
    for n in SMALL_SHARDED:
        shp = P[n].shape[1:]
        cnt = math.prod(shp)
        piece = sm[:, o:o + cnt].reshape((N_CHIPS,) + shp)
        piece = jnp.moveaxis(piece, 0, -2)
        W[n] = piece.reshape(shp[:-1] + (N_CHIPS * shp[-1],)).reshape(-1, D_MODEL)
        o += cnt
    W["lru_wa"] = P["lru_wa"][0].astype(BF16)
    W["lru_wx"] = P["lru_wx"][0].astype(BF16)
    return W, pending


def _small_grad_list(G):
    return [G["mix_norm"], G["mlp_norm"], G["mem_norm"], G["final_norm"], jnp.pad(G["attn_sinks"].reshape(-1), (0, ROW - ATTN_HEADS)),
            G["lru_wa"], G["lru_wx"], G["lru_conv_w"], G["lru_conv_b"], G["lru_ba"], G["lru_bx"], G["lru_lambda"]]


SMALL_G_SIZES = (2 * D_MODEL, 2 * D_MODEL, D_MODEL, D_MODEL, ROW, 2 * 8 * 128 * 128, 2 * 8 * 128 * 128,
                 4 * D_MODEL, D_MODEL, 2 * D_MODEL, 2 * D_MODEL, 2 * D_MODEL)


def _finish_grads(pkf, full0, full1, place, chip1, adamw):
    both = lambda off, r: jnp.concatenate([full0[off:off + r], full1[off:off + r]], axis=0)
    recvf = adamw("w_up", both(PK_UP, 1024), _sib_exchange_rider(pkf))
    halvesf, landingf = _reduce_first(pkf, place, "f", recvf)
    partsf = adamw("w_down", both(PK_DOWN, 1024), _exchange_rider((halvesf, landingf, 0, halvesf.shape[1])))
    fullf = adamw("w_out", both(PK_OUT, 384), _sib_allgather_rider(_sum_parts(partsf, place, "f")))
    small_placed = _place_slot(fullf[PKF_SMALL:], chip1, N_CHIPS, name="place_small_grads", tr=SMALL_G_ROWS)
    small_all = adamw("lru_w_in", full1[PK_IN:PK_IN + 640], _gather_rider(small_placed))
    adamw("w_mem_kv", jnp.concatenate([fullf[PKF_KV:PKF_SMALL], full1[PK_KV:PK_OUT]], axis=0), None)
    adamw("attn_w_in", fullf[:PKF_KV], None)
    flat = small_all.reshape(-1)
    small = {}
    o = 0
    names = ("mix_norm", "mlp_norm", "mem_norm", "final_norm", "attn_sinks", "lru_wa", "lru_wx",
             "lru_conv_w", "lru_conv_b", "lru_ba", "lru_bx", "lru_lambda")
    for n, cnt in zip(names, SMALL_G_SIZES):
        small[n] = flat[o:o + cnt]
        o += cnt
    return small


def kernel(x, mem, positions, mix_norm, mlp_norm, mem_norm, final_norm, w_mem_kv, w_out, w_up, w_down, attn_w_in, attn_sinks, lru_w_in, lru_conv_w, lru_conv_b, lru_wa, lru_ba, lru_wx, lru_bx, lru_lambda, loss_target, m_mix_norm, m_mlp_norm, m_mem_norm, m_final_norm, m_w_mem_kv, m_w_out, m_w_up, m_w_down, m_attn_w_in, m_attn_sinks, m_lru_w_in, m_lru_conv_w, m_lru_conv_b, m_lru_wa, m_lru_ba, m_lru_wx, m_lru_bx, m_lru_lambda, v_mix_norm, v_mlp_norm, v_mem_norm, v_final_norm, v_w_mem_kv, v_w_out, v_w_up, v_w_down, v_attn_w_in, v_attn_sinks, v_lru_w_in, v_lru_conv_w, v_lru_conv_b, v_lru_wa, v_lru_ba, v_lru_wx, v_lru_bx, v_lru_lambda):
    P = dict(mix_norm=mix_norm, mlp_norm=mlp_norm, mem_norm=mem_norm, final_norm=final_norm, w_mem_kv=w_mem_kv, w_out=w_out,
             w_up=w_up, w_down=w_down, attn_w_in=attn_w_in, attn_sinks=attn_sinks, lru_w_in=lru_w_in, lru_conv_w=lru_conv_w,
             lru_conv_b=lru_conv_b, lru_wa=lru_wa, lru_ba=lru_ba, lru_wx=lru_wx, lru_bx=lru_bx, lru_lambda=lru_lambda)
    M1 = dict(mix_norm=m_mix_norm, mlp_norm=m_mlp_norm, mem_norm=m_mem_norm, final_norm=m_final_norm, w_mem_kv=m_w_mem_kv,
              w_out=m_w_out, w_up=m_w_up, w_down=m_w_down, attn_w_in=m_attn_w_in, attn_sinks=m_attn_sinks, lru_w_in=m_lru_w_in,
              lru_conv_w=m_lru_conv_w, lru_conv_b=m_lru_conv_b, lru_wa=m_lru_wa, lru_ba=m_lru_ba, lru_wx=m_lru_wx,
              lru_bx=m_lru_bx, lru_lambda=m_lru_lambda)
    V2 = dict(mix_norm=v_mix_norm, mlp_norm=v_mlp_norm, mem_norm=v_mem_norm, final_norm=v_final_norm, w_mem_kv=v_w_mem_kv,
              w_out=v_w_out, w_up=v_w_up, w_down=v_w_down, attn_w_in=v_attn_w_in, attn_sinks=v_attn_sinks, lru_w_in=v_lru_w_in,
              lru_conv_w=v_lru_conv_w, lru_conv_b=v_lru_conv_b, lru_wa=v_lru_wa, lru_ba=v_lru_ba, lru_wx=v_lru_wx,
              lru_bx=v_lru_bx, lru_lambda=v_lru_lambda)
    chip = 2 * lax.axis_index("x") + lax.axis_index("y")
    chip1 = chip.astype(jnp.int32).reshape(1)
    place = jnp.stack([lax.axis_index("c").astype(jnp.int32), chip.astype(jnp.int32)])

    W, pending = _gather_weights(P, chip1)
    loss, dx, _, pkf, full0, full1 = _local_step(x[0], mem[0], positions[0], loss_target[0], W, pending, place)
    loss = lax.psum(loss, ("x", "y", "c"))
    grads, deltas, new_m, new_v = {}, {}, {}, {}

    def adamw_big(n, g, rider):
        d, nm, nv, *carried = _adamw(_rows(P[n]), g, _rows(M1[n]), _rows(V2[n]), name=f"adamw_{n}", rider=rider)
        grads[n], deltas[n], new_m[n], new_v[n] = (t.reshape(P[n].shape) for t in (g, d, nm, nv))
        return carried[0] if carried else None

    small = _finish_grads(pkf, full0, full1, place, chip1, adamw_big)

    for n in SMALL:
        g = small[n]
        if n in SMALL_SHARDED:
            shard = P[n].shape[-1]
            g = lax.dynamic_slice_in_dim(g.reshape(-1, N_CHIPS * shard), chip * shard, shard, axis=1)
        elif n == "attn_sinks":
            g = g[:ATTN_HEADS]
        grads[n] = g.reshape(P[n].shape)
    packs = []
    for src in (P, grads, M1, V2):
        a = jnp.concatenate([_pad_rows(src[n]) for n in SMALL], axis=0)
        packs.append(jnp.pad(a, ((0, ADAM_SMALL_ROWS - a.shape[0]), (0, 0))))
    d_s, nm_s, nv_s = _adamw(*packs, name="adamw_small")
    o = 0
    for n in SMALL:
        cnt = math.prod(P[n].shape)
        r = -(-cnt // ROW)
        for dst, src in ((deltas, d_s), (new_m, nm_s), (new_v, nv_s)):
            dst[n] = src[o:o + r].reshape(-1)[:cnt].reshape(P[n].shape)
        o += r

    return (loss, dx[None], *[grads[n] for n in WEIGHTS], *[deltas[n] for n in WEIGHTS],
            *[new_m[n] for n in WEIGHTS], *[new_v[n] for n in WEIGHTS])
```

```python
import math

import jax
import jax.numpy as jnp
from jax import lax
from jax.experimental import pallas as pl
from jax.experimental.pallas import tpu as pltpu

F32 = jnp.float32
BF16 = jnp.bfloat16
MESH = pl.DeviceIdType.MESH

D_MODEL = 1024
DEPTH = 2
EPS = 1e-6
ATTN_HEADS = 16
ATTN_KV_HEADS = 4
HEAD_DIM = 64
WINDOW = 128
BLOCK = 128
ROPE_THETA = 500000.0
ROPE_DIM = 16
Q_W = 1024
KV_W = 256
MEM_LEN = 256
MEM_HEADS = 4
MEM_HEAD_DIM = 128
MEM_W = 512
LRU_BLOCKS = 8
LRU_C = 8.0
ATTN_IN_W = 2048
LRU_IN_W = 2560
MIX_OUT_W = 1536
D_FF = 4096
NEG = -1e30
N_CHIPS = 4

ADAM_LR = 0.001
ADAM_B1 = 0.9
ADAM_B2 = 0.999
ADAM_EPS = 1e-08
ADAM_WD = 0.01
ADAM_STEP = 10

LANES = 128
SCAN_ROWS = 1024
ADAM_ROWS = 512
VMEM_LIMIT = 56 * 1024 * 1024

NT = (((1,), (1,)), ((), ()))
TN = (((0,), (0,)), ((), ()))


def _cp(sem=None):
    return pltpu.CompilerParams(dimension_semantics=sem, vmem_limit_bytes=VMEM_LIMIT)


HBM = pl.BlockSpec(memory_space=pl.ANY)
GATHER_SEMS = 6


def _place():
    x, y, c = lax.axis_index("x"), lax.axis_index("y"), lax.axis_index("c")
    chips = [(1 - x, y), (x, 1 - y), (1 - x, 1 - y)]
    return x, y, c, chips


def _remote(src, dst, send_sems, recv_sems, k, to):
    return pltpu.make_async_remote_copy(src_ref=src, dst_ref=dst, send_sem=send_sems.at[k], recv_sem=recv_sems.at[k],
                                        device_id=to, device_id_type=MESH)


def _gather_start(o_ref, send_sems, recv_sems):
    x, y, c, chips = _place()
    half = o_ref.shape[1] // 2
    own = o_ref.at[2 * x + y, pl.ds(pl.multiple_of(c * half, 16), half)]
    for j, (cx, cy) in enumerate(chips):
        _remote(own, own, send_sems, recv_sems, j, (cx, cy, c)).start()


def _gather_forward(o_ref, send_sems, recv_sems):
    x, y, c, chips = _place()
    half = o_ref.shape[1] // 2
    my_rows = pl.ds(pl.multiple_of(c * half, 16), half)
    for j, (cx, cy) in enumerate(chips):
        landed = o_ref.at[2 * cx + cy, my_rows]
        _remote(landed, landed, send_sems, recv_sems, j, (cx, cy, c)).wait_recv()
        _remote(landed, landed, send_sems, recv_sems, 3 + j, (x, y, 1 - c)).start()


def _gather_drain(o_ref, send_sems, recv_sems):
    x, y, c, chips = _place()
    half = o_ref.shape[1] // 2
    my_rows = pl.ds(pl.multiple_of(c * half, 16), half)
    sib_rows = pl.ds(pl.multiple_of((1 - c) * half, 16), half)
    own = o_ref.at[2 * x + y, my_rows]
    for j, (cx, cy) in enumerate(chips):
        got = o_ref.at[2 * cx + cy, sib_rows]
        _remote(got, got, send_sems, recv_sems, 3 + j, (x, y, 1 - c)).wait_recv()
    for j, (cx, cy) in enumerate(chips):
        _remote(own, own, send_sems, recv_sems, j, (cx, cy, c)).wait_send()
        landed = o_ref.at[2 * cx + cy, my_rows]
        _remote(landed, landed, send_sems, recv_sems, 3 + j, (x, y, 1 - c)).wait_send()


def _gather_finish(o_ref, send_sems, recv_sems):
    _gather_forward(o_ref, send_sems, recv_sems)
    _gather_drain(o_ref, send_sems, recv_sems)


def _exchange_start(h_ref, o_ref, send_sems, recv_sems, rows=None, base=0):
    x, y, c, chips = _place()
    rows = pl.ds(0, h_ref.shape[1]) if rows is None else rows
    for j, (cx, cy) in enumerate(chips):
        _remote(h_ref.at[2 * cx + cy, rows], o_ref.at[2 * x + y, rows], send_sems, recv_sems, base + j, (cx, cy, c)).start()


def _exchange_finish(h_ref, o_ref, send_sems, recv_sems, rows=None, base=0):
    x, y, c, chips = _place()
    rows = pl.ds(0, h_ref.shape[1]) if rows is None else rows
    for j, (cx, cy) in enumerate(chips):
        got = o_ref.at[2 * cx + cy, rows]
        _remote(got, got, send_sems, recv_sems, base + j, (cx, cy, c)).wait_recv()
    for j, (cx, cy) in enumerate(chips):
        _remote(h_ref.at[2 * cx + cy, rows], o_ref.at[2 * x + y, rows], send_sems, recv_sems, base + j, (cx, cy, c)).wait_send()


def _sib_exchange_copies(g_ref, o_ref, send_sems, recv_sems):
    x, y, c, _ = _place()
    half = g_ref.shape[1] // 2
    other = pl.ds(pl.multiple_of((1 - c) * half, 8), half)
    return [_remote(g_ref.at[s, other], o_ref.at[s], send_sems, recv_sems, s, (x, y, 1 - c)) for s in range(N_CHIPS)]


def _sib_exchange_start(*refs):
    for cp in _sib_exchange_copies(*refs):
        cp.start()


def _sib_exchange_finish(*refs):
    for cp in _sib_exchange_copies(*refs):
        cp.wait()


def _sib_allgather_start(*refs):
    *o_refs, send_sems, recv_sems = refs
    x, y, c, _ = _place()
    for i, o_ref in enumerate(o_refs):
        half = o_ref.shape[0] // 2
        mine = o_ref.at[pl.ds(pl.multiple_of(c * half, 8), half)]
        _remote(mine, mine, send_sems, recv_sems, i, (x, y, 1 - c)).start()


def _sib_allgather_finish(*refs):
    *o_refs, send_sems, recv_sems = refs
    x, y, c, _ = _place()
    for i, o_ref in enumerate(o_refs):
        half = o_ref.shape[0] // 2
        mine = o_ref.at[pl.ds(pl.multiple_of(c * half, 8), half)]
        got = o_ref.at[pl.ds(pl.multiple_of((1 - c) * half, 8), half)]
        _remote(got, got, send_sems, recv_sems, i, (x, y, 1 - c)).wait_recv()
        _remote(mine, mine, send_sems, recv_sems, i, (x, y, 1 - c)).wait_send()


class _Rider:
    def __init__(self, args, start, finish, inplace=1, mid=None):
        self.args, self.start, self.finish, self.inplace, self.mid = list(args), start, finish, inplace, mid


def _gather_rider(buf):
    return None if buf is None else _Rider([buf], _gather_start, _gather_finish, mid=(_gather_forward, _gather_drain))


def _exchange_rider(*parts):
    n = len(parts)
    assert 3 * n <= GATHER_SEMS

    def run(fn):
        def go(*refs):
            sems = refs[2 * n:]
            for i, (_, _, r0, nr) in enumerate(parts):
                fn(refs[i], refs[n + i], *sems, rows=pl.ds(r0, nr), base=3 * i)
        return go

    return _Rider([p[0] for p in parts] + [p[1] for p in parts], run(_exchange_start), run(_exchange_finish), inplace=n)


def _sib_exchange_rider(g):
    landing = lax.empty((N_CHIPS, g.shape[1] // 2, g.shape[2]), g.dtype)
    return _Rider([g, landing], _sib_exchange_start, _sib_exchange_finish)


def _sib_allgather_rider(*fulls):
    return _Rider(fulls, _sib_allgather_start, _sib_allgather_finish, inplace=len(fulls))


class _Hosted:
    def __init__(self, rider, n_in, n_out):
        self.rider = rider
        self.on = rider is not None
        self.args = rider.args if self.on else []
        k = len(self.args)
        p = self.p = rider.inplace if self.on else 0
        self.alias = {n_in + k - p + i: n_out + i for i in range(p)}
        self.in_specs = [HBM] * k
        self.out_specs = [HBM] * p
        self.out_shape = [jax.ShapeDtypeStruct(a.shape, a.dtype) for a in self.args[k - p:]]
        self.scratch = [pltpu.SemaphoreType.DMA((GATHER_SEMS,)), pltpu.SemaphoreType.DMA((GATHER_SEMS,))] if self.on else []

    def split(self, refs, n_in, n_out):
        refs = list(refs)
        if not self.on:
            return refs[:n_in], refs[n_in:n_in + n_out], refs[n_in + n_out:], None
        k, p = len(self.args), self.p
        ins, outs = refs[:n_in], refs[n_in + k:n_in + k + n_out]
        rest = refs[n_in + k + n_out + p:]
        rrefs = refs[n_in:n_in + k - p] + refs[n_in + k + n_out:n_in + k + n_out + p] + [rest[-2], rest[-1]]
        return ins, outs, rest[:-2], rrefs

    def run(self, rrefs, step, n_steps, compute):
        if rrefs is None:
            return compute()

        mid = self.rider.mid
        mid_step = (7 * n_steps) // 8
        two_stage = mid is not None and 0 < mid_step < n_steps - 1

        @pl.when(step == 0)
        def _():
            self.rider.start(*rrefs)

        compute()

        if two_stage:
            @pl.when(step == mid_step)
            def _():
                mid[0](*rrefs)

        @pl.when(step == n_steps - 1)
        def _():
            (mid[1] if two_stage else self.rider.finish)(*rrefs)


def _mm_nn(a, w3, *, name, out_dtype=F32, norm_g=None, resid=None, relu2=False, tm=512, gather=None):
    M, K = a.shape
    ns, _, n = w3.shape
    N = ns * n
    tm = min(tm, M)
    has_norm = norm_g is not None
    has_res = resid is not None
    n_in = 2 + has_norm + has_res
    n_out = (2 if relu2 else 1) + has_norm
    host = _Hosted(_gather_rider(gather), n_in, n_out)

    def body(*refs):
        ins, outs, _, gref = host.split(refs, n_in, n_out)
        a_ref, w_ref = ins[0], ins[1]
        g_ref = ins[2] if has_norm else None
        r_ref = ins[-1] if has_res else None

        def compute():
            if has_norm:
                xv = a_ref[...]
                rs = lax.rsqrt(jnp.mean(xv * xv, axis=-1, keepdims=True) + EPS)
                ab = (xv * rs * g_ref[...]).astype(BF16)
                outs[-1][...] = ab
            else:
                ab = a_ref[...]
            for s in range(ns):
                acc = jnp.dot(ab, w_ref[s], preferred_element_type=F32)
                sl = slice(s * n, (s + 1) * n)
                if relu2:
                    outs[0][:, sl] = acc.astype(BF16)
                    rl = jnp.maximum(acc, 0.0)
                    outs[1][:, sl] = (rl * rl).astype(BF16)
                elif has_res:
                    outs[0][:, sl] = r_ref[:, sl] + acc
                else:
                    outs[0][:, sl] = acc.astype(out_dtype)

        host.run(gref, pl.program_id(0), M // tm, compute)

    row = lambda w: pl.BlockSpec((tm, w), lambda i: (i, 0))
    in_specs = [row(K), pl.BlockSpec((ns, K, n), lambda i: (0, 0, 0))]
    args = [a, w3]
    if has_norm:
        in_specs.append(pl.BlockSpec((1, K), lambda i: (0, 0)))
        args.append(norm_g.reshape(1, K))
    if has_res:
        in_specs.append(row(N))
        args.append(resid)
    if relu2:
        out_shape = [jax.ShapeDtypeStruct((M, N), BF16), jax.ShapeDtypeStruct((M, N), BF16)]
        out_specs = [row(N), row(N)]
    else:
        out_shape = [jax.ShapeDtypeStruct((M, N), F32 if has_res else out_dtype)]
        out_specs = [row(N)]
    if has_norm:
        out_shape.append(jax.ShapeDtypeStruct((M, K), BF16))
        out_specs.append(row(K))
    res = pl.pallas_call(body, grid=(M // tm,), in_specs=in_specs + host.in_specs, out_specs=out_specs + host.out_specs,
                         out_shape=out_shape + host.out_shape, scratch_shapes=host.scratch, input_output_aliases=host.alias,
                         name=name, compiler_params=_cp(("arbitrary",) if host.on else ("parallel",)))(*args, *host.args)
    return res if len(res) > 1 else res[0]


def _mm_nt(g, w3, *, name, out_dtype=BF16, up=None, norm_x=None, norm_g=None, dres=None, tm=512):
    M = g.shape[0]
    ns, K, n = w3.shape
    tm = min(tm, M)
    has_up = up is not None
    has_norm = norm_x is not None
    has_res = dres is not None

    def body(*refs):
        refs = list(refs)
        g_ref, w_ref = refs[0], refs[1]
        pos = 2
        if has_up:
            up_ref = refs[pos]
            pos += 1
        if has_norm:
            x_ref, gn_ref = refs[pos], refs[pos + 1]
            pos += 2
        if has_res:
            r_ref = refs[pos]
            pos += 1
        outs = refs[pos:]
        acc = None
        for s in range(ns):
            part = lax.dot_general(g_ref[:, s * n:(s + 1) * n], w_ref[s], NT, preferred_element_type=F32)
            acc = part if acc is None else acc + part
        if has_up:
            outs[0][...] = (acc * (2.0 * jnp.maximum(up_ref[...].astype(F32), 0.0))).astype(BF16)
        elif has_norm:
            xv = x_ref[...]
            rs = lax.rsqrt(jnp.mean(xv * xv, axis=-1, keepdims=True) + EPS)
            xn = xv * rs
            dxn = acc * gn_ref[...]
            dx = rs * (dxn - xn * jnp.mean(dxn * xn, axis=-1, keepdims=True))
            if has_res:
                dx = dx + r_ref[...]
            outs[0][...] = dx
            outs[1][...] = dx.astype(BF16)

            @pl.when(pl.program_id(0) == 0)
            def _():
                outs[2][...] = jnp.zeros_like(outs[2])

            outs[2][...] += jnp.sum(acc * xn, axis=0, keepdims=True)
        else:
            outs[0][...] = acc.astype(out_dtype)

    row = lambda w: pl.BlockSpec((tm, w), lambda i: (i, 0))
    in_specs = [row(ns * n), pl.BlockSpec((ns, K, n), lambda i: (0, 0, 0))]
    args = [g, w3]
    if has_up:
        in_specs.append(row(K))
        args.append(up)
    if has_norm:
        in_specs += [row(K), pl.BlockSpec((1, K), lambda i: (0, 0))]
        args += [norm_x, norm_g.reshape(1, K)]
    if has_res:
        in_specs.append(row(K))
        args.append(dres)
    if has_norm:
        out_shape = [jax.ShapeDtypeStruct((M, K), F32), jax.ShapeDtypeStruct((M, K), BF16),
                     jax.ShapeDtypeStruct((1, K), F32)]
        out_specs = [row(K), row(K), pl.BlockSpec((1, K), lambda i: (0, 0))]
        sem = ("arbitrary",)
    else:
        out_shape = [jax.ShapeDtypeStruct((M, K), BF16 if has_up else out_dtype)]
        out_specs = [row(K)]
        sem = ("parallel",)
    res = pl.pallas_call(body, grid=(M // tm,), in_specs=in_specs, out_specs=out_specs, out_shape=out_shape,
                         name=name, compiler_params=_cp(sem))(*args)
    return res if len(res) > 1 else res[0]


def _mm_tn(a, g, ns, *, name, tk=512, tm=4096, packed=None, rider=None):
    M, K = a.shape
    n = g.shape[1] // ns
    tm = min(tm, M)
    tk = min(tk, K)
    nk, nm = K // tk, M // tm
    n_in = 3 if (packed is not None and packed[0] is not None) else 2
    host = _Hosted(rider, n_in, 1)

    def body(*refs):
        ins, outs, _, rrefs = host.split(refs, n_in, 1)
        a_ref, g_ref, o_ref = ins[0], ins[1], outs[0]

        def compute():
            @pl.when(pl.program_id(2) == 0)
            def _():
                o_ref[...] = jnp.zeros_like(o_ref)

            o_ref[0] += lax.dot_general(a_ref[...], g_ref[...], TN, preferred_element_type=F32)

        step = (pl.program_id(0) * nk + pl.program_id(1)) * nm + pl.program_id(2)
        host.run(rrefs, step, ns * nk * nm, compute)

    in_specs = [pl.BlockSpec((tm, tk), lambda s, k, m: (m, k)), pl.BlockSpec((tm, n), lambda s, k, m: (m, s))]
    args = [a, g]
    alias = {}
    if packed is None:
        out_spec = pl.BlockSpec((1, tk, n), lambda s, k, m: (s, k, 0))
        out_shape = jax.ShapeDtypeStruct((ns, K, n), F32)
    else:
        buf, rows, off = packed
        per_chip = K * ns // N_CHIPS
        assert n == ROW and per_chip % tk == 0 and off % tk == 0
        if ns == N_CHIPS:
            out_spec = pl.BlockSpec((1, tk, n), lambda s, k, m: (s, off // tk + k, 0))
        else:
            kpc = per_chip // tk
            out_spec = pl.BlockSpec((1, tk, n), lambda s, k, m: (k // kpc, off // tk + k % kpc, 0))
        out_shape = jax.ShapeDtypeStruct((N_CHIPS, rows, ROW), F32)
        if buf is not None:
            in_specs.append(HBM)
            args.append(buf)
            alias = {2: 0}
    sem = ("arbitrary",) * 3 if host.on else ("parallel", "parallel", "arbitrary")
    res = pl.pallas_call(
        body, grid=(ns, nk, nm), in_specs=in_specs + host.in_specs, out_specs=[out_spec] + host.out_specs,
        out_shape=[out_shape] + host.out_shape, scratch_shapes=host.scratch, name=name,
        input_output_aliases={**alias, **host.alias}, compiler_params=_cp(sem))(*args, *host.args)
    return res if host.on else res[0]


def _final(act, w_down, x, gain, target, *, name="mlp_down_final", tr=512):
    S, Dm = x.shape
    tr = min(tr, S)
    Kf = act.shape[1]

    def body(a_ref, w_ref, x_ref, g_ref, t_ref, loss_ref, dx_ref, dxb_ref, dg_ref):
        @pl.when(pl.program_id(0) == 0)
        def _():
            loss_ref[...] = jnp.zeros_like(loss_ref)
            dg_ref[...] = jnp.zeros_like(dg_ref)

        xv = x_ref[...] + jnp.dot(a_ref[...], w_ref[0], preferred_element_type=F32)
        gv = g_ref[...]
        rs = lax.rsqrt(jnp.mean(xv * xv, axis=-1, keepdims=True) + EPS)
        xn = xv * rs
        err = xn * gv - t_ref[...]
        loss_ref[...] += 0.5 * jnp.sum(jnp.mean(err * err, axis=-1, keepdims=True), axis=0, keepdims=True)
        dout = err * (1.0 / Dm)
        dg_ref[...] += jnp.sum(dout * xn, axis=0, keepdims=True)
        dxn = dout * gv
        dx = rs * (dxn - xn * jnp.mean(dxn * xn, axis=-1, keepdims=True))
        dx_ref[...] = dx
        dxb_ref[...] = dx.astype(BF16)

    row = pl.BlockSpec((tr, Dm), lambda i: (i, 0))
    return pl.pallas_call(
        body, grid=(S // tr,),
        in_specs=[pl.BlockSpec((tr, Kf), lambda i: (i, 0)), pl.BlockSpec((1, Kf, Dm), lambda i: (0, 0, 0)), row,
                  pl.BlockSpec((1, Dm), lambda i: (0, 0)), row],
        out_specs=[pl.BlockSpec((1, 1), lambda i: (0, 0)), row, row, pl.BlockSpec((1, Dm), lambda i: (0, 0))],
        out_shape=[jax.ShapeDtypeStruct((1, 1), F32), jax.ShapeDtypeStruct((S, Dm), F32),
                   jax.ShapeDtypeStruct((S, Dm), BF16), jax.ShapeDtypeStruct((1, Dm), F32)],
        name=name, compiler_params=_cp(("arbitrary",)))(act, w_down, x, gain.reshape(1, Dm), target)


def _rope_tables(positions):
    half = ROPE_DIM // 2
    inv_freq = ROPE_THETA ** (-2.0 * jnp.arange(half, dtype=F32) / ROPE_DIM)
    ang = positions.astype(F32)[:, None] * inv_freq
    cos, sin = jnp.cos(ang), jnp.sin(ang)
    S = positions.shape[0]
    ones = jnp.ones((S, HEAD_DIM - ROPE_DIM), F32)
    cos64 = jnp.concatenate([cos, cos, ones], axis=1)
    sin64 = jnp.concatenate([-sin, sin, 0.0 * ones], axis=1)
    return jnp.tile(cos64, (1, 2)), jnp.tile(sin64, (1, 2))


def _rope_partner(t):
    lane = lax.broadcasted_iota(jnp.int32, t.shape, 1)
    low = (lane & (HEAD_DIM - 1)) < (ROPE_DIM // 2)
    return jnp.where(low, pltpu.roll(t, LANES - ROPE_DIM // 2, 1), pltpu.roll(t, ROPE_DIM // 2, 1))


def _qk_prep(p, cos_t, sin_t, *, name="qk_prep", tr=256, gather=None):
    S = p.shape[0]
    tr = min(tr, S)
    scale = HEAD_DIM ** -0.5
    host = _Hosted(_gather_rider(gather), 3, 4)

    def body(*refs):
        ins, outs, _, gref = host.split(refs, 3, 4)
        host.run(gref, pl.program_id(0), S // tr, lambda: inner(*ins, *outs))

    def inner(p_ref, c_ref, s_ref, q_ref, k_ref, v_ref, va_ref):
        cs, sn = c_ref[...], s_ref[...]
        lane = lax.broadcasted_iota(jnp.int32, (tr, LANES), 1)
        lo = lane < HEAD_DIM
        for c in range(Q_W // LANES):
            t = p_ref[:, c * LANES:(c + 1) * LANES]
            q_ref[:, c * LANES:(c + 1) * LANES] = ((t * cs + _rope_partner(t) * sn) * scale).astype(BF16)
        for c in range(KV_W // LANES):
            t = p_ref[:, Q_W + c * LANES:Q_W + (c + 1) * LANES]
            kc = t * cs + _rope_partner(t) * sn
            vc = p_ref[:, Q_W + KV_W + c * LANES:Q_W + KV_W + (c + 1) * LANES]
            for arr, ref in ((kc, k_ref), (vc, v_ref)):
                sw = pltpu.roll(arr, HEAD_DIM, 1)
                ref[:, (2 * c) * LANES:(2 * c + 1) * LANES] = jnp.where(lo, arr, sw).astype(BF16)
                ref[:, (2 * c + 1) * LANES:(2 * c + 2) * LANES] = jnp.where(lo, sw, arr).astype(BF16)
            sw = pltpu.roll(vc, HEAD_DIM, 1)
            for k, aug in enumerate((jnp.where(lo, vc, 1.0), jnp.where(lo, 1.0, sw), jnp.where(lo, sw, 1.0), jnp.where(lo, 1.0, vc))):
                va_ref[:, (4 * c + k) * LANES:(4 * c + k + 1) * LANES] = aug.astype(BF16)

    row = lambda w: pl.BlockSpec((tr, w), lambda i: (i, 0))
    return pl.pallas_call(
        body, grid=(S // tr,), in_specs=[row(ATTN_IN_W), row(LANES), row(LANES)] + host.in_specs,
        out_specs=[row(Q_W), row(2 * KV_W), row(2 * KV_W), row(4 * KV_W)] + host.out_specs,
        out_shape=[jax.ShapeDtypeStruct((S, Q_W), BF16), jax.ShapeDtypeStruct((S, 2 * KV_W), BF16),
                   jax.ShapeDtypeStruct((S, 2 * KV_W), BF16), jax.ShapeDtypeStruct((S, 4 * KV_W), BF16)] + host.out_shape,
        scratch_shapes=host.scratch, input_output_aliases=host.alias,
        name=name, compiler_params=_cp(("arbitrary",) if host.on else ("parallel",)))(p, cos_t, sin_t, *host.args)


def _qk_prep_bwd(dq, dk, dv, dmq, cos_t, sin_t, *, name="qk_prep_bwd", tr=256):
    S = dq.shape[0]
    tr = min(tr, S)

    def body(dq_ref, dk_ref, dv_ref, dmq_ref, c_ref, s_ref, o_ref):
        cs, sn = c_ref[...], s_ref[...]
        for c in range(Q_W // LANES):
            t = dq_ref[:, c * LANES:(c + 1) * LANES]
            o_ref[:, c * LANES:(c + 1) * LANES] = (t * cs - _rope_partner(t) * sn).astype(BF16)
        for c in range(KV_W // LANES):
            t = dk_ref[:, c * LANES:(c + 1) * LANES]
            o_ref[:, Q_W + c * LANES:Q_W + (c + 1) * LANES] = (t * cs - _rope_partner(t) * sn).astype(BF16)
        o_ref[:, Q_W + KV_W:Q_W + 2 * KV_W] = dv_ref[...].astype(BF16)
        o_ref[:, Q_W + 2 * KV_W:] = dmq_ref[...]

    row = lambda w: pl.BlockSpec((tr, w), lambda i: (i, 0))
    return pl.pallas_call(
        body, grid=(S // tr,), in_specs=[row(Q_W), row(KV_W), row(KV_W), row(MEM_W), row(LANES), row(LANES)],
        out_specs=row(ATTN_IN_W), out_shape=jax.ShapeDtypeStruct((S, ATTN_IN_W), BF16),
        name=name, compiler_params=_cp(("parallel",)))(dq, dk, dv, dmq, cos_t, sin_t)


def _band(n, S):
    start = pl.multiple_of(jnp.clip((n - 1) * BLOCK, 0, S - 3 * BLOCK), BLOCK)
    qi = lax.broadcasted_iota(jnp.int32, (BLOCK, 3 * BLOCK), 0) + n * BLOCK
    ki = lax.broadcasted_iota(jnp.int32, (BLOCK, 3 * BLOCK), 1) + start
    return start, jnp.abs(ki - qi) <= WINDOW


def _head_operand(ref, h, lo):
    c = h // 2
    t = ref[:, c * LANES:(c + 1) * LANES].astype(F32)
    return jnp.where(lo if h % 2 == 0 else jnp.logical_not(lo), t, 0.0).astype(BF16)


GROUP = ATTN_HEADS // ATTN_KV_HEADS
EVENS_FIRST = (0, 2, 1, 3)


def _attn_fwd(q, kd, va, sinks, *, name="attn_fwd", gather=None):
    S = q.shape[0]
    host = _Hosted(_gather_rider(gather), 4, 2)

    def body(*refs):
        ins, outs, scr, gref = host.split(refs, 4, 2)
        host.run(gref, pl.program_id(0), S // BLOCK, lambda: inner(*ins, *outs, *scr))

    def inner(sink_ref, q_ref, k_ref, va_ref, o_ref, lse_ref, p_scr):
        n = pl.program_id(0)
        start, mask = _band(n, S)
        lane = lax.broadcasted_iota(jnp.int32, (BLOCK, LANES), 1)
        lo = lane < HEAD_DIM
        rows = pl.ds(start, 3 * BLOCK)
        scores = []
        for g in range(ATTN_KV_HEADS):
            qst = jnp.concatenate([_head_operand(q_ref, GROUP * g + j, lo) for j in EVENS_FIRST], axis=0)
            scores.append(lax.dot_general(qst, k_ref[rows, g * LANES:(g + 1) * LANES], NT, preferred_element_type=F32))
        ms = {}
        for g in range(ATTN_KV_HEADS):
            for pos, j in enumerate(EVENS_FIRST):
                h = GROUP * g + j
                s = jnp.where(mask, scores[g][pos * BLOCK:(pos + 1) * BLOCK], NEG)
                ms[h] = jnp.maximum(jnp.max(s, axis=-1, keepdims=True), sink_ref[h])
                p_scr[(GROUP * g + pos) * BLOCK:(GROUP * g + pos + 1) * BLOCK, :] = jnp.exp(s - ms[h]).astype(BF16)
        pvs = {}
        for g in range(ATTN_KV_HEADS):
            for par in range(2):
                r0 = (GROUP * g + 2 * par) * BLOCK
                pvs[g, par] = jnp.dot(p_scr[r0:r0 + 2 * BLOCK, :], va_ref[rows, (2 * g + par) * LANES:(2 * g + par + 1) * LANES],
                                      preferred_element_type=F32)
        lse_blk = jnp.zeros((BLOCK, LANES), F32)
        for g in range(ATTN_KV_HEADS):
            outs = {}
            for par in range(2):
                for k in range(2):
                    j = EVENS_FIRST[2 * par + k]
                    h = GROUP * g + j
                    pv = pvs[g, par][k * BLOCK:(k + 1) * BLOCK]
                    den = pltpu.roll(pv, HEAD_DIM, 1) + jnp.exp(sink_ref[h] - ms[h])
                    outs[j] = pv * (1.0 / den)
                    l = den[:, par * HEAD_DIM:par * HEAD_DIM + 1]
                    lse_blk = jnp.where(lane == h, ms[h] + jnp.log(l), lse_blk)
            for jj in range(2):
                o_ref[:, (2 * g + jj) * LANES:(2 * g + jj + 1) * LANES] = jnp.where(lo, outs[2 * jj], outs[2 * jj + 1]).astype(BF16)
        lse_ref[...] = lse_blk

    full = lambda w: pl.BlockSpec((S, w), lambda i: (0, 0))
    return pl.pallas_call(
        body, grid=(S // BLOCK,),
        in_specs=[pl.BlockSpec(memory_space=pltpu.SMEM), pl.BlockSpec((BLOCK, Q_W), lambda i: (i, 0)),
                  full(2 * KV_W), full(4 * KV_W)] + host.in_specs,
        out_specs=[pl.BlockSpec((BLOCK, Q_W), lambda i: (i, 0)), pl.BlockSpec((BLOCK, LANES), lambda i: (i, 0))] + host.out_specs,
        out_shape=[jax.ShapeDtypeStruct((S, MIX_OUT_W), BF16), jax.ShapeDtypeStruct((S, LANES), F32)] + host.out_shape,
        scratch_shapes=[pltpu.VMEM((ATTN_HEADS * BLOCK, 3 * BLOCK), BF16)] + host.scratch, input_output_aliases=host.alias,
        name=name, compiler_params=_cp(("arbitrary",) if host.on else ("parallel",)))(sinks, q, kd, va, *host.args)


def _attn_bwd(q, kd, vd, ao, lse, sinks, dcat, *, name="attn_bwd", rider=None):
    S = q.shape[0]
    scale = HEAD_DIM ** -0.5
    host = _Hosted(rider, 7, 4)

    def body(*refs):
        ins, outs, scr, rrefs = host.split(refs, 7, 4)
        host.run(rrefs, pl.program_id(0), S // BLOCK, lambda: inner(*ins, *outs, *scr))

    def inner(sink_ref, q_ref, k_ref, v_ref, ao_ref, lse_ref, do_ref, dq_ref, dk_ref, dv_ref, ds_ref, p_scr, dsb_scr):
        n = pl.program_id(0)

        @pl.when(n == 0)
        def _():
            dk_ref[...] = jnp.zeros_like(dk_ref)
            dv_ref[...] = jnp.zeros_like(dv_ref)
            ds_ref[...] = jnp.zeros_like(ds_ref)

        start, mask = _band(n, S)
        lane = lax.broadcasted_iota(jnp.int32, (BLOCK, LANES), 1)
        lo = lane < HEAD_DIM
        lane3 = lax.broadcasted_iota(jnp.int32, (3 * BLOCK, LANES), 1)
        row8 = lax.broadcasted_iota(jnp.int32, (8, LANES), 0)
        lane8 = lax.broadcasted_iota(jnp.int32, (8, LANES), 1)
        dsink = jnp.zeros((8, LANES), F32)
        lse_blk = lse_ref[...]
        rows = pl.ds(start, 3 * BLOCK)
        lses, deltas = {}, {}
        for c in range(Q_W // LANES):
            prod = do_ref[:, c * LANES:(c + 1) * LANES].astype(F32) * ao_ref[:, c * LANES:(c + 1) * LANES].astype(F32)
            for k in range(2):
                h = 2 * c + k
                deltas[h] = jnp.sum(jnp.where(lo if k == 0 else jnp.logical_not(lo), prod, 0.0), axis=1, keepdims=True)
                lses[h] = jnp.sum(jnp.where(lane == h, lse_blk, 0.0), axis=1, keepdims=True)
                val = -jnp.sum(jnp.exp(sink_ref[h] - lses[h]) * deltas[h], axis=0, keepdims=True)
                dsink = dsink + jnp.where((row8 == 0) & (lane8 == h), val, 0.0)
        stack = lambda ref, g: jnp.concatenate([_head_operand(ref, GROUP * g + j, lo) for j in range(GROUP)], axis=0)
        ss, dps = [], []
        for g in range(ATTN_KV_HEADS):
            ss.append(lax.dot_general(stack(q_ref, g), k_ref[rows, g * LANES:(g + 1) * LANES], NT, preferred_element_type=F32))
            dps.append(lax.dot_general(stack(do_ref, g), v_ref[rows, g * LANES:(g + 1) * LANES], NT, preferred_element_type=F32))
        for g in range(ATTN_KV_HEADS):
            for j in range(GROUP):
                h = GROUP * g + j
                r = slice(j * BLOCK, (j + 1) * BLOCK)
                hr = slice(h * BLOCK, (h + 1) * BLOCK)
                p = jnp.exp(jnp.where(mask, ss[g][r], NEG) - lses[h])
                p_scr[hr, :] = p.astype(BF16)
                dsb_scr[hr, :] = (p * (dps[g][r] - deltas[h])).astype(BF16)
        for g in range(ATTN_KV_HEADS):
            cols = slice((g // 2) * LANES, (g // 2 + 1) * LANES)
            gr = slice(GROUP * g * BLOCK, GROUP * (g + 1) * BLOCK)
            dsg = dsb_scr[gr, :]
            dqs = jnp.dot(dsg, k_ref[rows, g * LANES:(g + 1) * LANES], preferred_element_type=F32) * scale
            for jj in range(2):
                dq_ref[:, (2 * g + jj) * LANES:(2 * g + jj + 1) * LANES] = jnp.where(
                    lo, dqs[(2 * jj) * BLOCK:(2 * jj + 1) * BLOCK], dqs[(2 * jj + 1) * BLOCK:(2 * jj + 2) * BLOCK])
            half = (lane3 < HEAD_DIM) if g % 2 == 0 else (lane3 >= HEAD_DIM)
            dkr = lax.dot_general(dsg, stack(q_ref, g), TN, preferred_element_type=F32)
            dk_ref[rows, cols] += jnp.where(half, dkr + pltpu.roll(dkr, HEAD_DIM, 1), 0.0)
            dvr = lax.dot_general(p_scr[gr, :], stack(do_ref, g), TN, preferred_element_type=F32)
            dv_ref[rows, cols] += jnp.where(half, dvr + pltpu.roll(dvr, HEAD_DIM, 1), 0.0)
        ds_ref[...] += dsink

    full = lambda w: pl.BlockSpec((S, w), lambda i: (0, 0))
    blk = lambda w: pl.BlockSpec((BLOCK, w), lambda i: (i, 0))
    return pl.pallas_call(
        body, grid=(S // BLOCK,),
        in_specs=[pl.BlockSpec(memory_space=pltpu.SMEM), blk(Q_W), full(2 * KV_W), full(2 * KV_W), blk(Q_W), blk(LANES), blk(Q_W)]
        + host.in_specs,
        out_specs=[blk(Q_W), full(KV_W), full(KV_W), pl.BlockSpec((8, LANES), lambda i: (0, 0))] + host.out_specs,
        out_shape=[jax.ShapeDtypeStruct((S, Q_W), F32), jax.ShapeDtypeStruct((S, KV_W), F32),
                   jax.ShapeDtypeStruct((S, KV_W), F32), jax.ShapeDtypeStruct((8, LANES), F32)] + host.out_shape,
        scratch_shapes=[pltpu.VMEM((ATTN_HEADS * BLOCK, 3 * BLOCK), BF16), pltpu.VMEM((ATTN_HEADS * BLOCK, 3 * BLOCK), BF16)]
        + host.scratch, input_output_aliases=host.alias,
        name=name, compiler_params=_cp(("arbitrary",)))(sinks, q, kd, vd, ao, lse, dcat, *host.args)


def _mem_probs(q_ref, kv_ref, h):
    scale = MEM_HEAD_DIM ** -0.5
    qh = q_ref[:, h * LANES:(h + 1) * LANES].astype(BF16)
    s = lax.dot_general(qh, kv_ref[:, h * LANES:(h + 1) * LANES], NT, preferred_element_type=F32) * scale
    m = jnp.max(s, axis=-1, keepdims=True)
    pe = jnp.exp(s - m)
    return qh, pe * (1.0 / jnp.sum(pe, axis=-1, keepdims=True))


def _memattn_fwd(p, qblk, kv, cat, *, name="memattn_fwd", tr=512):
    S = p.shape[0]
    tr = min(tr, S)

    def body(q_ref, kv_ref, cat_ref, o_ref):
        for h in range(MEM_HEADS):
            _, pr = _mem_probs(q_ref, kv_ref, h)
            o = jnp.dot(pr.astype(BF16), kv_ref[:, MEM_W + h * LANES:MEM_W + (h + 1) * LANES], preferred_element_type=F32)
            o_ref[:, h * LANES:(h + 1) * LANES] = o.astype(BF16)

    return pl.pallas_call(
        body, grid=(S // tr,),
        in_specs=[pl.BlockSpec((tr, MEM_W), lambda i: (i, qblk)), pl.BlockSpec((MEM_LEN, 2 * MEM_W), lambda i: (0, 0)), HBM],
        out_specs=pl.BlockSpec((tr, MEM_W), lambda i: (i, Q_W // MEM_W)),
        out_shape=jax.ShapeDtypeStruct((S, MIX_OUT_W), BF16), input_output_aliases={2: 0},
        name=name, compiler_params=_cp(("parallel",)))(p, kv, cat)


def _memattn_bwd(p, qblk, kv, dcat, *, name="memattn_bwd", tr=512, rider=None):
    S = p.shape[0]
    tr = min(tr, S)
    scale = MEM_HEAD_DIM ** -0.5
    host = _Hosted(rider, 3, 2)

    def body(*refs):
        ins, outs, _, rrefs = host.split(refs, 3, 2)
        host.run(rrefs, pl.program_id(0), S // tr, lambda: inner(*ins, *outs))

    def inner(q_ref, kv_ref, do_ref, dq_ref, dkv_ref):
        @pl.when(pl.program_id(0) == 0)
        def _():
            dkv_ref[...] = jnp.zeros_like(dkv_ref)

        for h in range(MEM_HEADS):
            qh, pr = _mem_probs(q_ref, kv_ref, h)
            doh = do_ref[:, h * LANES:(h + 1) * LANES]
            dp = lax.dot_general(doh, kv_ref[:, MEM_W + h * LANES:MEM_W + (h + 1) * LANES], NT, preferred_element_type=F32)
            delta = jnp.sum(pr * dp, axis=-1, keepdims=True)
            dsb = (pr * (dp - delta) * scale).astype(BF16)
            dq = jnp.dot(dsb, kv_ref[:, h * LANES:(h + 1) * LANES], preferred_element_type=F32)
            dq_ref[:, h * LANES:(h + 1) * LANES] = dq.astype(BF16)
            dkv_ref[:, h * LANES:(h + 1) * LANES] += lax.dot_general(dsb, qh, TN, preferred_element_type=F32)
            dkv_ref[:, MEM_W + h * LANES:MEM_W + (h + 1) * LANES] += lax.dot_general(
                pr.astype(BF16), doh, TN, preferred_element_type=F32)

    return pl.pallas_call(
        body, grid=(S // tr,),
        in_specs=[pl.BlockSpec((tr, MEM_W), lambda i: (i, qblk)), pl.BlockSpec((MEM_LEN, 2 * MEM_W), lambda i: (0, 0)),
                  pl.BlockSpec((tr, MEM_W), lambda i: (i, Q_W // MEM_W))] + host.in_specs,
        out_specs=[pl.BlockSpec((tr, MEM_W), lambda i: (i, 0)), pl.BlockSpec((MEM_LEN, 2 * MEM_W), lambda i: (0, 0))]
        + host.out_specs,
        out_shape=[jax.ShapeDtypeStruct((S, MEM_W), BF16), jax.ShapeDtypeStruct((MEM_LEN, 2 * MEM_W), F32)] + host.out_shape,
        scratch_shapes=host.scratch, input_output_aliases=host.alias,
        name=name, compiler_params=_cp(("arbitrary",)))(p, kv, dcat, *host.args)


def _sqrt(v):
    return jnp.where(v > 0.0, v * lax.rsqrt(v), 0.0)


def _sigmoid(z):
    return 1.0 / (1.0 + jnp.exp(-z))


def _one_minus_exp(z, exp_z):
    poly = z * (1.0 + z * (0.5 + z * (1.0 / 6.0 + z * (1.0 / 24.0 + z * (1.0 / 120.0)))))
    return jnp.where(z > -0.1, -poly, 1.0 - exp_z)


def _softplus_neg(lam):
    z = -lam
    return jnp.maximum(z, 0.0) + jnp.log(1.0 + jnp.exp(-jnp.abs(z)))


_GELU_C = math.sqrt(2.0 / math.pi)


def _gelu(z):
    return 0.5 * z * (1.0 + jnp.tanh(_GELU_C * (z + 0.044715 * z * z * z)))


def _row_or_zero(ref, t, S):
    ok = jnp.logical_and(t >= 0, t < S)
    return jnp.where(ok, ref[pl.ds(jnp.clip(t, 0, S - 1), 1), :], 0.0)


def _shift_down(v, first):
    ri = lax.broadcasted_iota(jnp.int32, v.shape, 0)
    return jnp.where(ri == 0, first, pltpu.roll(v, 1, 0))


def _shift_up(v, last):
    T = v.shape[0]
    ri = lax.broadcasted_iota(jnp.int32, v.shape, 0)
    return jnp.where(ri == T - 1, last, pltpu.roll(v, T - 1, 0))


def _scan_chunk(a, u, reverse):
    T = a.shape[0]
    ri = lax.broadcasted_iota(jnp.int32, a.shape, 0)
    d = 1
    while d < T:
        if reverse:
            a_s, u_s, ok = pltpu.roll(a, T - d, 0), pltpu.roll(u, T - d, 0), ri < T - d
        else:
            a_s, u_s, ok = pltpu.roll(a, d, 0), pltpu.roll(u, d, 0), ri >= d
        u = jnp.where(ok, a * u_s + u, u)
        a = jnp.where(ok, a * a_s, a)
        d *= 2
    return a, u


def _conv_taps(xb_ref, t0, S):
    T = SCAN_ROWS
    x0 = xb_ref[pl.ds(t0, T), :]
    xm1 = _shift_down(x0, _row_or_zero(xb_ref, t0 - 1, S))
    nxt0 = _row_or_zero(xb_ref, t0 + T, S)
    xp1 = _shift_up(x0, nxt0)
    xp2 = _shift_up(xp1, _row_or_zero(xb_ref, t0 + T + 1, S))
    return xm1, x0, xp1, xp2


def _lru_gates(xc, w_a, b_a, w_x, b_x, sp):
    xcb = xc.astype(BF16)
    r = _sigmoid(jnp.dot(xcb, w_a, preferred_element_type=F32) + b_a)
    i = _sigmoid(jnp.dot(xcb, w_x, preferred_element_type=F32) + b_x)
    la = -LRU_C * r * sp
    a = jnp.exp(la)
    b2 = _one_minus_exp(2.0 * la, a * a)
    inv_beta = lax.rsqrt(b2)
    return r, i, a, jnp.where(b2 > 0.0, b2 * inv_beta, 0.0), inv_beta


def _lru_specs(S):
    col = lambda off: pl.BlockSpec((S, LANES), lambda n: (0, n + off), pipeline_mode=pl.Buffered(1))
    small = lambda r: pl.BlockSpec((r, LANES), lambda n: (0, n))
    wblk = pl.BlockSpec((2, 1, LANES, LANES), lambda n: (0, n, 0, 0))
    return col, small, wblk


def _lru_fwd(p, conv_w, conv_b, wa, ba, wx, bx, lam, *, name="lru_fwd"):
    S = p.shape[0]
    T = SCAN_ROWS
    nc = S // T

    def body(xb_ref, gate_ref, cw_ref, cb_ref, wa_ref, ba_ref, wx_ref, bx_ref, lam_ref, y_ref, hf_ref, hr_ref, xc_v):
        sp = _softplus_neg(lam_ref[...])
        cw = cw_ref[...]

        def fwd_step(c, h_in):
            t0 = pl.multiple_of(c * T, T)
            xm1, x0, xp1, xp2 = _conv_taps(xb_ref, t0, S)
            xc = cb_ref[...] + xm1 * cw[0:1] + x0 * cw[1:2] + xp1 * cw[2:3] + xp2 * cw[3:4]
            xc_v[pl.ds(t0, T), :] = xc
            _, i, a, beta, _ = _lru_gates(xc, wa_ref[0, 0], ba_ref[0:1], wx_ref[0, 0], bx_ref[0:1], sp[0:1])
            A, U = _scan_chunk(a, beta * (i * xc), False)
            hf_ref[pl.ds(t0, T), :] = A * h_in + U
            return hf_ref[pl.ds(t0 + T - 1, 1), :]

        lax.fori_loop(0, nc, fwd_step, jnp.zeros((1, LANES), F32))

        def rev_step(k, h_in):
            t0 = pl.multiple_of((nc - 1 - k) * T, T)
            xc = xc_v[pl.ds(t0, T), :]
            _, i, a, beta, _ = _lru_gates(xc, wa_ref[1, 0], ba_ref[1:2], wx_ref[1, 0], bx_ref[1:2], sp[1:2])
            A, U = _scan_chunk(a, beta * (i * xc), True)
            h = A * h_in + U
            hr_ref[pl.ds(t0, T), :] = h
            y_ref[pl.ds(t0, T), :] = ((hf_ref[pl.ds(t0, T), :] + h) * _gelu(gate_ref[pl.ds(t0, T), :])).astype(BF16)
            return hr_ref[pl.ds(t0, 1), :]

        lax.fori_loop(0, nc, rev_step, jnp.zeros((1, LANES), F32))

    col, small, wblk = _lru_specs(S)
    colo = lambda: pl.BlockSpec((S, LANES), lambda n: (0, n))
    return pl.pallas_call(
        body, grid=(LRU_BLOCKS,),
        in_specs=[col(0), col(LRU_BLOCKS), small(4), small(1), wblk, small(2), wblk, small(2), small(2)],
        out_specs=[colo(), colo(), colo()],
        out_shape=[jax.ShapeDtypeStruct((S, MIX_OUT_W), BF16), jax.ShapeDtypeStruct((S, D_MODEL), F32),
                   jax.ShapeDtypeStruct((S, D_MODEL), F32)],
        scratch_shapes=[pltpu.VMEM((S, LANES), F32)],
        name=name, compiler_params=_cp(("parallel",)))(p, p, conv_w, conv_b, wa, ba, wx, bx, lam)


def _lru_bwd(p, hf, hr, dcat, conv_w, conv_b, wa, ba, wx, bx, lam, *, name="lru_bwd"):
    S = p.shape[0]
    T = SCAN_ROWS
    nc = S // T

    def body(xb_ref, gate_ref, hf_ref, hr_ref, dy_ref, cw_ref, cb_ref, wa_ref, ba_ref, wx_ref, bx_ref, lam_ref,
             dxb_ref, dgate_ref, dcw_ref, dcb_ref, dwa_ref, dba_ref, dwx_ref, dbx_ref, dlam_ref, xc_v, dxc_v, dh_v):
        lam_v = lam_ref[...]
        sp = _softplus_neg(lam_v)
        cw = cw_ref[...]
        for ref in (dcw_ref, dcb_ref, dwa_ref, dba_ref, dwx_ref, dbx_ref, dlam_ref):
            ref[...] = jnp.zeros_like(ref)

        def prep_step(c, carry):
            t0 = pl.multiple_of(c * T, T)
            rows = pl.ds(t0, T)
            xm1, x0, xp1, xp2 = _conv_taps(xb_ref, t0, S)
            xc_v[rows, :] = cb_ref[...] + xm1 * cw[0:1] + x0 * cw[1:2] + xp1 * cw[2:3] + xp2 * cw[3:4]
            z = gate_ref[rows, :]
            dy = dy_ref[rows, :].astype(F32)
            th = jnp.tanh(_GELU_C * (z + 0.044715 * z * z * z))
            dgelu = 0.5 * (1.0 + th) + 0.5 * z * (1.0 - th * th) * _GELU_C * (1.0 + 3.0 * 0.044715 * z * z)
            dgate_ref[rows, :] = (dy * (hf_ref[rows, :] + hr_ref[rows, :]) * dgelu).astype(BF16)
            dh_v[rows, :] = dy * (0.5 * z * (1.0 + th))
            return carry

        lax.fori_loop(0, nc, prep_step, 0)

        def direction(d):
            h_ref = hf_ref if d == 0 else hr_ref
            w_a, w_x = wa_ref[d, 0], wx_ref[d, 0]
            b_a, b_x, sp_d = ba_ref[d:d + 1], bx_ref[d:d + 1], sp[d:d + 1]

            def step(k, carry):
                g_in, a_in = carry
                c = (nc - 1 - k) if d == 0 else k
                t0 = pl.multiple_of(c * T, T)
                rows = pl.ds(t0, T)
                xc = xc_v[rows, :]
                r, i, a, beta, inv_beta = _lru_gates(xc, w_a, b_a, w_x, b_x, sp_d)
                dh = dh_v[rows, :]
                hc = h_ref[rows, :]
                if d == 0:
                    A, U = _scan_chunk(_shift_up(a, a_in), dh, True)
                    g = A * g_in + U
                    h_nb = _shift_down(hc, _row_or_zero(h_ref, t0 - 1, S))
                    nxt = (g[0:1], a[0:1])
                else:
                    A, U = _scan_chunk(_shift_down(a, a_in), dh, False)
                    g = A * g_in + U
                    h_nb = _shift_up(hc, _row_or_zero(h_ref, t0 + T, S))
                    nxt = (g[T - 1:T], a[T - 1:T])
                da = g * h_nb
                dbeta = g * (i * xc)
                tb = g * beta
                dla = da * a - dbeta * (a * a * inv_beta)
                dzr = (dla * (-LRU_C * sp_d)) * (r * (1.0 - r))
                dzi = (tb * xc) * (i * (1.0 - i))
                dzrb, dzib, xcb = dzr.astype(BF16), dzi.astype(BF16), xc.astype(BF16)
                dwa_ref[d, 0] += lax.dot_general(xcb, dzrb, TN, preferred_element_type=F32)
                dwx_ref[d, 0] += lax.dot_general(xcb, dzib, TN, preferred_element_type=F32)
                dba_ref[d:d + 1] += jnp.sum(dzr, axis=0, keepdims=True)
                dbx_ref[d:d + 1] += jnp.sum(dzi, axis=0, keepdims=True)
                dlam_ref[d:d + 1] += jnp.sum(dla * (-LRU_C * r), axis=0, keepdims=True)
                dxc = (tb * i + lax.dot_general(dzrb, w_a, NT, preferred_element_type=F32)
                       + lax.dot_general(dzib, w_x, NT, preferred_element_type=F32))
                if d == 0:
                    dxc_v[rows, :] = dxc
                else:
                    dxc_v[rows, :] += dxc
                return nxt

            lax.fori_loop(0, nc, step, (jnp.zeros((1, LANES), F32), jnp.zeros((1, LANES), F32)))

        direction(0)
        direction(1)
        dlam_ref[...] = dlam_ref[...] * (-1.0 / (1.0 + jnp.exp(lam_v)))

        def conv_step(c, carry):
            t0 = pl.multiple_of(c * T, T)
            rows = pl.ds(t0, T)
            g0 = dxc_v[rows, :]
            gm1 = _shift_down(g0, _row_or_zero(dxc_v, t0 - 1, S))
            gm2 = _shift_down(gm1, _row_or_zero(dxc_v, t0 - 2, S))
            gp1 = _shift_up(g0, _row_or_zero(dxc_v, t0 + T, S))
            dxb_ref[rows, :] = (cw[0:1] * gp1 + cw[1:2] * g0 + cw[2:3] * gm1 + cw[3:4] * gm2).astype(BF16)
            xm1, x0, xp1, xp2 = _conv_taps(xb_ref, t0, S)
            for tap, xs in enumerate((xm1, x0, xp1, xp2)):
                dcw_ref[tap:tap + 1] += jnp.sum(g0 * xs, axis=0, keepdims=True)
            dcb_ref[...] += jnp.sum(g0, axis=0, keepdims=True)
            return carry

        lax.fori_loop(0, nc, conv_step, 0)

    col, small, wblk = _lru_specs(S)
    colo = lambda: pl.BlockSpec((S, LANES), lambda n: (0, n), pipeline_mode=pl.Buffered(1))
    return pl.pallas_call(
        body, grid=(LRU_BLOCKS,),
        in_specs=[col(0), col(LRU_BLOCKS), col(0), col(0), col(0), small(4), small(1), wblk, small(2), wblk, small(2), small(2)],
        out_specs=[colo(), colo(), small(4), small(1), wblk, small(2), wblk, small(2), small(2)],
        out_shape=[jax.ShapeDtypeStruct((S, D_MODEL), BF16), jax.ShapeDtypeStruct((S, D_MODEL), BF16),
                   jax.ShapeDtypeStruct((4, D_MODEL), F32), jax.ShapeDtypeStruct((1, D_MODEL), F32),
                   jax.ShapeDtypeStruct((2, LRU_BLOCKS, LANES, LANES), F32), jax.ShapeDtypeStruct((2, D_MODEL), F32),
                   jax.ShapeDtypeStruct((2, LRU_BLOCKS, LANES, LANES), F32), jax.ShapeDtypeStruct((2, D_MODEL), F32),
                   jax.ShapeDtypeStruct((2, D_MODEL), F32)],
        scratch_shapes=[pltpu.VMEM((S, LANES), F32), pltpu.VMEM((S, LANES), F32), pltpu.VMEM((S, LANES), F32)],
        name=name, compiler_params=_cp(("parallel",)))(p, p, hf, hr, dcat, conv_w, conv_b, wa, ba, wx, bx, lam)


PK_UP, PK_DOWN, PK_KV, PK_OUT, PK_IN = 0, 1024, 2048, 2304, 2688
PK_ROWS = {0: PK_IN, 1: PK_IN + 640}
SMALL_G_ROWS = 192
PKF_KV, PKF_SMALL = 512, 768
PKF_ROWS = PKF_SMALL + SMALL_G_ROWS


def _mlp_bwd(x, dx, dxb, saved, w_up, w_down, gain, l, rider=None, next_rider=None):
    up, act, h = saved
    pk = _mm_tn(act, dxb, 1, name=f"dw_down{l}", packed=(None, PK_ROWS[l], PK_DOWN), rider=rider)
    pk, carried = pk if rider is not None else (pk, None)
    dup = _mm_nt(dxb, w_down, up=up, name=f"d_up{l}")
    rider_up = next_rider(carried) if next_rider is not None else None
    pk = _mm_tn(h, dup, N_CHIPS, name=f"dw_up{l}", packed=(pk, PK_ROWS[l], PK_UP), rider=rider_up)
    pk, carried = pk if rider_up is not None else (pk, carried)
    dx, dxb, g_gain = _mm_nt(dup, w_up, norm_x=x, norm_g=gain, dres=dx, name=f"d_mlp_in{l}")
    return dx, dxb, pk, g_gain, carried


def _reduce_first(pk, place, tag, recv):
    return _sum_halves(pk, recv, place, name=f"sum_halves{tag}", tr=pk.shape[1] // 4)


def _sum_parts(parts, place, tag):
    return _sum_chips(parts, place, name=f"sum_chips{tag}", tr=parts.shape[1] // 2)


def _local_step(x, mem, positions, target, W, pending=None, place=None):
    cos_t, sin_t = _rope_tables(positions)
    sinks = W["attn_sinks"].reshape(ATTN_HEADS)
    G = {}

    def hosting(late, fn, *args, **kw):
        if pending is None:
            return fn(*args, **kw)
        *res, buf = fn(*args, gather=pending[late], **kw)
        if late.startswith("w_down"):
            W.setdefault("w_down", [None] * DEPTH)[int(late[-1])] = _ready(late, buf)
        elif late == "w_up":
            W["w_up"], W["w_mem_kv"] = _ready(late, buf)
        else:
            W[late] = _ready(late, buf)
        return res if len(res) > 1 else res[0]

    p0, h0 = hosting("w_out", _mm_nn, x, W["attn_w_in"], norm_g=W["mix_norm"][0], name="attn_in")
    q, kd, vd, va = hosting("w_down0", _qk_prep, p0, cos_t, sin_t)
    ao, lse = hosting("w_up", _attn_fwd, q, kd, va, sinks)
    kv0, memn = _mm_nn(mem, W["w_mem_kv"][0], norm_g=W["mem_norm"], out_dtype=BF16, name="mem_kv0", tm=256)
    kv1 = _mm_nn(memn, W["w_mem_kv"][1], out_dtype=BF16, name="mem_kv1", tm=256)
    cat0 = _memattn_fwd(p0, Q_W // MEM_W + 1, kv0, ao, name="memattn_fwd0")
    x1 = hosting("lru_w_in", _mm_nn, cat0, W["w_out"][0], resid=x, name="mix_out0")
    up0, act0, h1 = hosting("w_down1", _mm_nn, x1, W["w_up"][0], norm_g=W["mlp_norm"][0], relu2=True, name="mlp_up0")
    x2, mlp0 = _mm_nn(act0, W["w_down"][0], resid=x1, name="mlp_down0"), (up0, act0, h1)
    p1, h2 = _mm_nn(x2, W["lru_w_in"], norm_g=W["mix_norm"][1], name="lru_in")
    lru_w = (W["lru_conv_w"], W["lru_conv_b"], W["lru_wa"], W["lru_ba"], W["lru_wx"], W["lru_bx"], W["lru_lambda"])
    y, hf, hr = _lru_fwd(p1, *lru_w)
    cat1 = _memattn_fwd(p1, 2 * D_MODEL // MEM_W, kv1, y, name="memattn_fwd1")
    x3 = _mm_nn(cat1, W["w_out"][1], resid=x2, name="mix_out1")
    mlp1 = _mm_nn(x3, W["w_up"][1], norm_g=W["mlp_norm"][1], relu2=True, name="mlp_up1")
    loss, dx, dxb, G["final_norm"] = _final(mlp1[1], W["w_down"][1], x3, W["final_norm"], target)

    def put(pk, off, g):
        return pk.at[:, off:off + g.size // (N_CHIPS * ROW)].set(g.reshape(N_CHIPS, -1, ROW))

    dx, dxb, pk1, gm1, _ = _mlp_bwd(x3, dx, dxb, mlp1, W["w_up"][1], W["w_down"][1], W["mlp_norm"][1], 1)
    pk1 = _mm_tn(cat1, dxb, 1, name="dw_out1", tk=384, packed=(pk1, PK_ROWS[1], PK_OUT))
    dcat1 = _mm_nt(dxb, W["w_out"][1], name="d_mix1")
    dmq1, dkv1 = _memattn_bwd(p1, 2 * D_MODEL // MEM_W, kv1, dcat1, name="memattn_bwd1")
    dkv1b = dkv1.astype(BF16)
    pk1 = _mm_tn(memn, dkv1b, 1, name="dw_kv1", tm=256, tk=256, packed=(pk1, PK_ROWS[1], PK_KV))
    (dxb1, dgate, G["lru_conv_w"], G["lru_conv_b"], G["lru_wa"], G["lru_ba"], G["lru_wx"], G["lru_bx"],
     G["lru_lambda"]) = _lru_bwd(p1, hf, hr, dcat1, *lru_w)
    dp1 = jnp.concatenate([dxb1, dgate, dmq1], axis=1)
    pk1 = put(pk1, PK_IN, _mm_tn(h2, dp1, N_CHIPS, name="dw_lru_in"))
    dx, dxb, gx1 = _mm_nt(dp1, W["lru_w_in"], norm_x=x2, norm_g=W["mix_norm"][1], dres=dx, name="d_lru_in")
    dist = place is not None
    h1_rows = PK_ROWS[1] // 4
    kept = {}

    def first_half(recv1):
        kept["halves1"], landing1 = _reduce_first(pk1, place, "1", recv1)
        return _exchange_rider((kept["halves1"], landing1, 0, h1_rows))

    dx, dxb, pk0, gm0, landing1 = _mlp_bwd(x1, dx, dxb, mlp0, W["w_up"][0], W["w_down"][0], W["mlp_norm"][0], 0,
                                           rider=_sib_exchange_rider(pk1) if dist else None,
                                           next_rider=first_half if dist else None)
    pk0 = _mm_tn(cat0, dxb, 1, name="dw_out0", tk=384, packed=(pk0, PK_ROWS[0], PK_OUT))
    pk0 = pk0.at[:, PK_KV:PK_OUT].set(0.0)
    dcat0 = _mm_nt(dxb, W["w_out"][0], name="d_mix0")
    dmq0, dkv0, *recv0 = _memattn_bwd(p0, Q_W // MEM_W + 1, kv0, dcat0, name="memattn_bwd0",
                                      rider=_sib_exchange_rider(pk0) if dist else None)
    dkv0b = dkv0.astype(BF16)
    g_kv0 = _mm_tn(memn, dkv0b, 1, name="dw_kv0", tm=256, tk=256)
    rider = None
    if dist:
        halves0, landing0 = _reduce_first(pk0, place, "0", recv0[0])
        rider = _exchange_rider((kept["halves1"], landing1, h1_rows, h1_rows), (halves0, landing0, 0, halves0.shape[1]))
    dq, dk, dv, dsink, *parts = _attn_bwd(q, kd, vd, cat0, lse, sinks, dcat0, rider=rider)
    dp0 = _qk_prep_bwd(dq, dk, dv, dmq0, cos_t, sin_t)
    g_in = _mm_tn(h0, dp0, N_CHIPS, name="dw_attn_in",
                  rider=_sib_allgather_rider(_sum_parts(parts[0], place, "1"), _sum_parts(parts[1], place, "0")) if dist else None)
    if dist:
        g_in, pk1, pk0 = g_in
    dx, _, gx0 = _mm_nt(dp0, W["attn_w_in"], norm_x=x, norm_g=W["mix_norm"][0], dres=dx, name="d_attn_in")

    w_kv_both = jnp.concatenate([W["w_mem_kv"][0], W["w_mem_kv"][1]], axis=0)
    _, _, G["mem_norm"] = _mm_nt(jnp.concatenate([dkv0b, dkv1b], axis=1), w_kv_both, norm_x=mem, norm_g=W["mem_norm"],
                                 name="d_mem", tm=256)

    G["mix_norm"] = jnp.concatenate([gx0, gx1], axis=0)
    G["mlp_norm"] = jnp.concatenate([gm0, gm1], axis=0)
    G["attn_sinks"] = dsink[0:1, 0:ATTN_HEADS]
    small = _flat_pad(_small_grad_list(G), N_CHIPS * SMALL_G_ROWS * ROW).reshape(N_CHIPS, SMALL_G_ROWS, ROW)
    pkf = jnp.concatenate([g_in.reshape(N_CHIPS, PKF_KV, ROW), g_kv0.reshape(N_CHIPS, PKF_SMALL - PKF_KV, ROW), small], axis=1)
    return loss[0, 0], dx, G, pkf, pk0, pk1


def _comm_call(body, out_shape, n_sems, name, *args, alias=None):
    return pl.pallas_call(
        body, out_shape=out_shape, in_specs=[HBM] * len(args), out_specs=HBM,
        scratch_shapes=[pltpu.SemaphoreType.DMA((n_sems,)), pltpu.SemaphoreType.DMA((n_sems,))],
        input_output_aliases=alias or {}, name=name)(*args)


def _place_slot(shard, slot, n_slots, *, name, tr):
    R, C = shard.shape

    def body(s_ref, a_ref, o_ref):
        o_ref[0] = a_ref[...]

    return pl.pallas_call(
        body,
        grid_spec=pltpu.PrefetchScalarGridSpec(
            num_scalar_prefetch=1, grid=(R // tr,), in_specs=[pl.BlockSpec((tr, C), lambda i, s_ref: (i, 0))],
            out_specs=pl.BlockSpec((1, tr, C), lambda i, s_ref: (s_ref[0], i, 0))),
        out_shape=jax.ShapeDtypeStruct((n_slots, R, C), shard.dtype), name=name,
        compiler_params=_cp(("parallel",)))(slot, shard)


def _allgather_chips(buf, *, name):
    def body(b_ref, o_ref, send_sems, recv_sems):
        _gather_start(o_ref, send_sems, recv_sems)
        _gather_finish(o_ref, send_sems, recv_sems)

    return _comm_call(body, jax.ShapeDtypeStruct(buf.shape, buf.dtype), GATHER_SEMS, name, buf, alias={0: 0})


def _sum_halves(g, recv, place, *, name="sum_halves", tr=480):
    _, R, C = g.shape
    half = R // 2
    nblk = half // tr

    def body(pl_ref, g_ref, r_ref, o_ref, own_ref):
        v = (g_ref[...] + r_ref[...]).astype(BF16)
        o_ref[...] = v

        @pl.when(pl.program_id(1) == pl_ref[1])
        def _():
            own_ref[...] = v

    blk = pl.BlockSpec((1, tr, C), lambda i, s, p: (s, i, 0))
    return pl.pallas_call(
        body,
        grid_spec=pltpu.PrefetchScalarGridSpec(
            num_scalar_prefetch=1, grid=(nblk, N_CHIPS),
            in_specs=[pl.BlockSpec((1, tr, C), lambda i, s, p: (s, p[0] * nblk + i, 0)), blk],
            out_specs=[blk, pl.BlockSpec((1, tr, C), lambda i, s, p: (p[1], i, 0))]),
        out_shape=[jax.ShapeDtypeStruct((N_CHIPS, half, C), BF16)] * 2, name=name,
        compiler_params=_cp(("parallel", "arbitrary")))(place, g, recv)


def _sum_chips(parts, place, *, name="sum_chips", tr=480):
    _, R, C = parts.shape
    nblk = R // tr

    def body(pl_ref, p_ref, o_ref):
        acc = p_ref[0].astype(F32) + p_ref[1].astype(F32)
        o_ref[...] = (acc + p_ref[2].astype(F32)) + p_ref[3].astype(F32)

    return pl.pallas_call(
        body,
        grid_spec=pltpu.PrefetchScalarGridSpec(
            num_scalar_prefetch=1, grid=(nblk,), in_specs=[pl.BlockSpec((N_CHIPS, tr, C), lambda i, p: (0, i, 0))],
            out_specs=pl.BlockSpec((tr, C), lambda i, p: (p[0] * nblk + i, 0))),
        out_shape=jax.ShapeDtypeStruct((2 * R, C), F32), name=name, compiler_params=_cp(("parallel",)))(place, parts)


def _adamw(w, g, m, v, *, name, rider=None):
    R, C = w.shape
    tr = R // -(-R // ADAM_ROWS)
    assert R % tr == 0 and tr % 8 == 0
    bc1 = 1.0 - ADAM_B1 ** ADAM_STEP
    bc2 = 1.0 - ADAM_B2 ** ADAM_STEP
    host = _Hosted(rider, 4, 3)

    def body(*refs):
        ins, outs, _, rrefs = host.split(refs, 4, 3)
        host.run(rrefs, pl.program_id(0), R // tr, lambda: inner(*ins, *outs))

    def inner(w_ref, g_ref, m_ref, v_ref, d_ref, nm_ref, nv_ref):
        gv = g_ref[...]
        nm = ADAM_B1 * m_ref[...] + (1.0 - ADAM_B1) * gv
        nv = ADAM_B2 * v_ref[...] + (1.0 - ADAM_B2) * (gv * gv)
        d_ref[...] = -ADAM_LR * ((nm / bc1) / (_sqrt(nv / bc2) + ADAM_EPS) + ADAM_WD * w_ref[...])
        nm_ref[...] = nm
        nv_ref[...] = nv

    blk = pl.BlockSpec((tr, C), lambda i: (i, 0))
    return pl.pallas_call(
        body, grid=(R // tr,), in_specs=[blk] * 4 + host.in_specs, out_specs=[blk] * 3 + host.out_specs,
        out_shape=[jax.ShapeDtypeStruct((R, C), F32)] * 3 + host.out_shape, scratch_shapes=host.scratch,
        input_output_aliases=host.alias, name=name,
        compiler_params=_cp(("arbitrary",) if host.on else ("parallel",)))(w, g, m, v, *host.args)


ROW = 1024
BIG = ("w_mem_kv", "w_out", "w_up", "w_down", "attn_w_in", "lru_w_in")
SMALL_SHARDED = ("lru_conv_w", "lru_conv_b", "lru_ba", "lru_bx", "lru_lambda")
REPLICATED = ("mix_norm", "mlp_norm", "mem_norm", "final_norm", "attn_sinks", "lru_wa", "lru_wx")
SMALL = REPLICATED + SMALL_SHARDED
WEIGHTS = ("mix_norm", "mlp_norm", "mem_norm", "final_norm", "w_mem_kv", "w_out", "w_up", "w_down", "attn_w_in",
           "attn_sinks", "lru_w_in", "lru_conv_w", "lru_conv_b", "lru_wa", "lru_ba", "lru_wx", "lru_bx", "lru_lambda")
SMALL_W_ROWS = 32
ADAM_SMALL_ROWS = 640


def _rows(a):
    return a.reshape(-1, ROW)


def _flat_pad(parts, total):
    flat = jnp.concatenate([p.reshape(-1) for p in parts])
    return jnp.pad(flat, (0, total - flat.shape[0]))


def _pad_rows(a):
    flat = a.reshape(-1)
    n = -(-flat.shape[0] // ROW) * ROW
    return jnp.pad(flat, (0, n - flat.shape[0])).reshape(-1, ROW)


LATE = ("w_out", "w_down0", "w_up", "lru_w_in", "w_down1")


def _ready(name, full):
    if name == "w_out":
        wo = full.reshape(N_CHIPS, DEPTH, -1, D_MODEL)
        return [wo[:, l].reshape(1, MIX_OUT_W, D_MODEL) for l in range(DEPTH)]
    if name == "w_up":
        n_up = DEPTH * D_MODEL
        wu = full[:, :n_up].reshape(N_CHIPS, DEPTH, D_MODEL, D_FF // N_CHIPS)
        kv = full[:, n_up:].reshape(N_CHIPS, DEPTH, -1, D_MODEL)
        return [wu[:, l] for l in range(DEPTH)], [kv[:, l].reshape(1, D_MODEL, D_MODEL) for l in range(DEPTH)]
    if name == "lru_w_in":
        return full.reshape(N_CHIPS, D_MODEL, LRU_IN_W // N_CHIPS)
    return full.reshape(1, D_FF, D_MODEL)


def _gather_weights(P, chip1):
    bf = lambda a: _rows(a.astype(BF16))
    small = _flat_pad([P[n] for n in SMALL_SHARDED], SMALL_W_ROWS * ROW // 2)
    small_bits = lax.bitcast_convert_type(small, BF16).reshape(SMALL_W_ROWS, ROW)
    early = jnp.concatenate([bf(P["attn_w_in"]), small_bits], axis=0)
    n_in = P["attn_w_in"].size // ROW
    placed = _place_slot(early, chip1, N_CHIPS, name="place_weights", tr=early.shape[0] // 2)
    full = _allgather_chips(placed, name="allgather_weights")
    late = {"w_out": bf(P["w_out"]), "lru_w_in": bf(P["lru_w_in"]),
            "w_up": jnp.concatenate([bf(P["w_up"]), bf(P["w_mem_kv"])], axis=0),
            "w_down0": bf(P["w_down"][0]), "w_down1": bf(P["w_down"][1])}
    pending = {n: _place_slot(late[n], chip1, N_CHIPS, name=f"place_{n}", tr=late[n].shape[0] // 2) for n in LATE}
    W = {n: P[n] for n in REPLICATED}
    W["attn_w_in"] = full[:, :n_in].reshape(N_CHIPS, D_MODEL, ATTN_IN_W // N_CHIPS)
    sm = lax.bitcast_convert_type(full[:, n_in:].reshape(N_CHIPS, -1, 2), F32)
    o = 0
    for n in SMALL_SHARDED:
        shp = P[n].shape[1:]
        cnt = math.prod(shp)
        piece = sm[:, o:o + cnt].reshape((N_CHIPS,) + shp)
        piece = jnp.moveaxis(piece, 0, -2)
        W[n] = piece.reshape(shp[:-1] + (N_CHIPS * shp[-1],)).reshape(-1, D_MODEL)
        o += cnt
    W["lru_wa"] = P["lru_wa"][0].astype(BF16)
    W["lru_wx"] = P["lru_wx"][0].astype(BF16)
    return W, pending


def _small_grad_list(G):
    return [G["mix_norm"], G["mlp_norm"], G["mem_norm"], G["final_norm"], jnp.pad(G["attn_sinks"].reshape(-1), (0, ROW - ATTN_HEADS)),
            G["lru_wa"], G["lru_wx"], G["lru_conv_w"], G["lru_conv_b"], G["lru_ba"], G["lru_bx"], G["lru_lambda"]]


SMALL_G_SIZES = (2 * D_MODEL, 2 * D_MODEL, D_MODEL, D_MODEL, ROW, 2 * 8 * 128 * 128, 2 * 8 * 128 * 128,
                 4 * D_MODEL, D_MODEL, 2 * D_MODEL, 2 * D_MODEL, 2 * D_MODEL)


def _finish_grads(pkf, full0, full1, place, chip1, adamw):
    both = lambda off, r: jnp.concatenate([full0[off:off + r], full1[off:off + r]], axis=0)
    recvf = adamw("w_up", both(PK_UP, 1024), _sib_exchange_rider(pkf))
    halvesf, landingf = _reduce_first(pkf, place, "f", recvf)
    partsf = adamw("w_down", both(PK_DOWN, 1024), _exchange_rider((halvesf, landingf, 0, halvesf.shape[1])))
    fullf = adamw("w_out", both(PK_OUT, 384), _sib_allgather_rider(_sum_parts(partsf, place, "f")))
    small_placed = _place_slot(fullf[PKF_SMALL:], chip1, N_CHIPS, name="place_small_grads", tr=SMALL_G_ROWS)
    small_all = adamw("lru_w_in", full1[PK_IN:PK_IN + 640], _gather_rider(small_placed))
    adamw("w_mem_kv", jnp.concatenate([fullf[PKF_KV:PKF_SMALL], full1[PK_KV:PK_OUT]], axis=0), None)
    adamw("attn_w_in", fullf[:PKF_KV], None)
    flat = small_all.reshape(-1)
    small = {}
    o = 0
    names = ("mix_norm", "mlp_norm", "mem_norm", "final_norm", "attn_sinks", "lru_wa", "lru_wx",
             "lru_conv_w", "lru_conv_b", "lru_ba", "lru_bx", "lru_lambda")
    for n, cnt in zip(names, SMALL_G_SIZES):
        small[n] = flat[o:o + cnt]
        o += cnt
    return small


def kernel(x, mem, positions, mix_norm, mlp_norm, mem_norm, final_norm, w_mem_kv, w_out, w_up, w_down, attn_w_in, attn_sinks, lru_w_in, lru_conv_w, lru_conv_b, lru_wa, lru_ba, lru_wx, lru_bx, lru_lambda, loss_target, m_mix_norm, m_mlp_norm, m_mem_norm, m_final_norm, m_w_mem_kv, m_w_out, m_w_up, m_w_down, m_attn_w_in, m_attn_sinks, m_lru_w_in, m_lru_conv_w, m_lru_conv_b, m_lru_wa, m_lru_ba, m_lru_wx, m_lru_bx, m_lru_lambda, v_mix_norm, v_mlp_norm, v_mem_norm, v_final_norm, v_w_mem_kv, v_w_out, v_w_up, v_w_down, v_attn_w_in, v_attn_sinks, v_lru_w_in, v_lru_conv_w, v_lru_conv_b, v_lru_wa, v_lru_ba, v_lru_wx, v_lru_bx, v_lru_lambda):
    P = dict(mix_norm=mix_norm, mlp_norm=mlp_norm, mem_norm=mem_norm, final_norm=final_norm, w_mem_kv=w_mem_kv, w_out=w_out,
             w_up=w_up, w_down=w_down, attn_w_in=attn_w_in, attn_sinks=attn_sinks, lru_w_in=lru_w_in, lru_conv_w=lru_conv_w,
             lru_conv_b=lru_conv_b, lru_wa=lru_wa, lru_ba=lru_ba, lru_wx=lru_wx, lru_bx=lru_bx, lru_lambda=lru_lambda)
    M1 = dict(mix_norm=m_mix_norm, mlp_norm=m_mlp_norm, mem_norm=m_mem_norm, final_norm=m_final_norm, w_mem_kv=m_w_mem_kv,
              w_out=m_w_out, w_up=m_w_up, w_down=m_w_down, attn_w_in=m_attn_w_in, attn_sinks=m_attn_sinks, lru_w_in=m_lru_w_in,
              lru_conv_w=m_lru_conv_w, lru_conv_b=m_lru_conv_b, lru_wa=m_lru_wa, lru_ba=m_lru_ba, lru_wx=m_lru_wx,
              lru_bx=m_lru_bx, lru_lambda=m_lru_lambda)
    V2 = dict(mix_norm=v_mix_norm, mlp_norm=v_mlp_norm, mem_norm=v_mem_norm, final_norm=v_final_norm, w_mem_kv=v_w_mem_kv,
              w_out=v_w_out, w_up=v_w_up, w_down=v_w_down, attn_w_in=v_attn_w_in, attn_sinks=v_attn_sinks, lru_w_in=v_lru_w_in,
              lru_conv_w=v_lru_conv_w, lru_conv_b=v_lru_conv_b, lru_wa=v_lru_wa, lru_ba=v_lru_ba, lru_wx=v_lru_wx,
              lru_bx=v_lru_bx, lru_lambda=v_lru_lambda)
    chip = 2 * lax.axis_index("x") + lax.axis_index("y")
    chip1 = chip.astype(jnp.int32).reshape(1)
    place = jnp.stack([lax.axis_index("c").astype(jnp.int32), chip.astype(jnp.int32)])

    W, pending = _gather_weights(P, chip1)
    loss, dx, _, pkf, full0, full1 = _local_step(x[0], mem[0], positions[0], loss_target[0], W, pending, place)
    loss = lax.psum(loss, ("x", "y", "c"))
    grads, deltas, new_m, new_v = {}, {}, {}, {}

    def adamw_big(n, g, rider):
        d, nm, nv, *carried = _adamw(_rows(P[n]), g, _rows(M1[n]), _rows(V2[n]), name=f"adamw_{n}", rider=rider)
        grads[n], deltas[n], new_m[n], new_v[n] = (t.reshape(P[n].shape) for t in (g, d, nm, nv))
        return carried[0] if carried else None

    small = _finish_grads(pkf, full0, full1, place, chip1, adamw_big)

    for n in SMALL:
        g = small[n]
        if n in SMALL_SHARDED:
            shard = P[n].shape[-1]
            g = lax.dynamic_slice_in_dim(g.reshape(-1, N_CHIPS * shard), chip * shard, shard, axis=1)
        elif n == "attn_sinks":
            g = g[:ATTN_HEADS]
        grads[n] = g.reshape(P[n].shape)
    packs = []
    for src in (P, grads, M1, V2):
        a = jnp.concatenate([_pad_rows(src[n]) for n in SMALL], axis=0)
        packs.append(jnp.pad(a, ((0, ADAM_SMALL_ROWS - a.shape[0]), (0, 0))))
    d_s, nm_s, nv_s = _adamw(*packs, name="adamw_small")
    o = 0
    for n in SMALL:
        cnt = math.prod(P[n].shape)
        r = -(-cnt // ROW)
        for dst, src in ((deltas, d_s), (new_m, nm_s), (new_v, nv_s)):
            dst[n] = src[o:o + r].reshape(-1)[:cnt].reshape(P[n].shape)
        o += r

    return (loss, dx[None], *[grads[n] for n in WEIGHTS], *[deltas[n] for n in WEIGHTS],
            *[new_m[n] for n in WEIGHTS], *[new_v[n] for n in WEIGHTS])
```
